```python
import math
import jax
import jax.numpy as jnp
from jax import lax
import numpy as np

D_MODEL = 2048
BATCH = 1
SEQ = 8192
DEPTH = 2
DEC_BATCH = 32
DEC_SEQ = 4
PAST_LEN = 8192
PAGE_SIZE = 128

N_EVEN = (DEPTH + 1) // 2
N_ODD = DEPTH // 2
RET_HEADS = 8
RET_DK = 128
RET_DV = 256
RET_CHUNK = 128
RET_THETA = 10000.0
SWA_HEADS = 8
SWA_HD = 128
SWA_PATTERNS = ((128, 1), (512, 4), (2048, 16))
SWA_MAX_WINDOW = 2048
SWA_BLOCK = 128
ROPE_THETA = 500000.0
ROPE_DIMS = SWA_HD // 4
GMLP_CHUNK = 128
GMLP_WIDTH = D_MODEL
GMLP_GROUP = 128
GMLP_GROUPS = GMLP_WIDTH // GMLP_GROUP
MEM_LEN = 256
MEM_HEADS = 4
MEM_HD = 128
NORM_EPS = 1e-6
NEG_INF = -1e30
EVEN_IN_WIDTHS = (RET_HEADS * RET_DK, RET_HEADS * RET_DK, RET_HEADS * RET_DV, RET_HEADS * RET_DV,
                  SWA_HEADS * SWA_HD, SWA_HEADS * SWA_HD, SWA_HEADS * SWA_HD, SWA_HEADS * SWA_HD,
                  MEM_HEADS * MEM_HD, MEM_HEADS * MEM_HD)
ODD_IN_WIDTHS = (2 * GMLP_WIDTH, GMLP_WIDTH, MEM_HEADS * MEM_HD, MEM_HEADS * MEM_HD)
EVEN_IN = sum(EVEN_IN_WIDTHS)
EVEN_OUT = RET_HEADS * RET_DV + SWA_HEADS * SWA_HD + MEM_HEADS * MEM_HD
ODD_IN = sum(ODD_IN_WIDTHS)
ODD_OUT = GMLP_WIDTH + MEM_HEADS * MEM_HD

kernel_name = 'hybrid_retnet_dilated_gmlp_decoder_step'

F32 = jnp.float32


def split_cols(a, widths):
    out, start = [], 0
    for w in widths:
        out.append(a[..., start:start + w])
        start += w
    return out


def rms_norm(x, g):
    xf = x.astype(F32)
    y = xf * lax.rsqrt(jnp.mean(xf * xf, axis=-1, keepdims=True) + NORM_EPS)
    return (y * g.astype(F32)).astype(x.dtype)


def layer_norm(x, g):
    xf = x.astype(F32)
    mu = jnp.mean(xf, axis=-1, keepdims=True)
    var = jnp.mean(jnp.square(xf - mu), axis=-1, keepdims=True)
    return ((xf - mu) * lax.rsqrt(var + NORM_EPS) * g.astype(F32)).astype(x.dtype)


def rotary(x, pos, n_rot, theta):
    half = n_rot // 2
    inv = 1.0 / (theta ** (jnp.arange(half, dtype=F32) / half))
    ang = pos.astype(F32)[:, None] * inv[None, :]
    cos = jnp.cos(ang)[:, None, :]
    sin = jnp.sin(ang)[:, None, :]
    xf = x.astype(F32)
    x1, x2 = xf[..., :half], xf[..., half:n_rot]
    out = jnp.concatenate([x1 * cos - x2 * sin, x2 * cos + x1 * sin, xf[..., n_rot:]], axis=-1)
    return out.astype(x.dtype)


def retention_chunk(state, q, k, v, log_gamma):
    c = q.shape[1]
    idx = jnp.arange(c, dtype=F32)
    rel = idx[:, None] - idx[None, :]
    decay = jnp.where(rel >= 0, jnp.exp(log_gamma[:, None, None] * jnp.maximum(rel, 0.0)), 0.0)
    qf, kf, vf = q.astype(F32), k.astype(F32), v.astype(F32)
    scores = jnp.einsum('bihd,bjhd->bhij', qf, kf) * decay[None]
    inner = jnp.einsum('bhij,bjhe->bihe', scores, vf)
    q_decay = jnp.exp(log_gamma[:, None] * (idx[None, :] + 1.0))
    cross = jnp.einsum('bihd,hi,bhde->bihe', qf, q_decay, state)
    k_decay = jnp.exp(log_gamma[:, None] * (c - 1.0 - idx[None, :]))
    chunk_decay = jnp.exp(log_gamma * c)
    new_state = chunk_decay[None, :, None, None] * state + jnp.einsum('bjhd,hj,bjhe->bhde', kf, k_decay, vf)
    return new_state, inner + cross


def retention(state0, q, k, v):
    b, t, h, _ = q.shape
    c = math.gcd(t, RET_CHUNK)
    nc = t // c
    log_gamma = jnp.log1p(-jnp.exp2(-5.0 - jnp.arange(RET_HEADS, dtype=F32)))

    def chunks(a):
        return a.reshape(b, nc, c, h, a.shape[-1]).swapaxes(0, 1)

    def step(s, qkv):
        q_c, k_c, v_c = qkv
        return retention_chunk(s, q_c, k_c, v_c, log_gamma)

    state, out = lax.scan(step, state0.astype(F32), (chunks(q), chunks(k), chunks(v)))
    return state, out.swapaxes(0, 1).reshape(b, t, h, RET_DV)


def head_group_norm(o, g):
    mu = jnp.mean(o, axis=-1, keepdims=True)
    var = jnp.mean(jnp.square(o - mu), axis=-1, keepdims=True)
    y = (o - mu) * lax.rsqrt(var + NORM_EPS)
    return y.reshape(o.shape[0], o.shape[1], -1) * g.astype(F32)


def dilated_prompt(q, k, v, window, dilation):
    b, t, h, d = q.shape
    span = window // dilation
    blk = SWA_BLOCK
    unit = dilation * blk
    t_pad = -(-t // unit) * unit
    L = t_pad // dilation
    nb = L // blk

    def strided(a):
        a = jnp.pad(a, ((0, 0), (0, t_pad - t), (0, 0), (0, 0)))
        a = a.reshape(b, L, dilation, h, d).transpose(0, 2, 1, 3, 4)
        return a.reshape(b, dilation, nb, blk, h, d).astype(F32)

    qs, ks, vs = strided(q), strided(k), strided(v)

    def with_prev(a):
        prev = jnp.pad(a[:, :, :-1], ((0, 0), (0, 0), (1, 0), (0, 0), (0, 0), (0, 0)))
        return jnp.concatenate([prev, a], axis=3)

    kw, vw = with_prev(ks), with_prev(vs)
    s = jnp.einsum('brnihd,brnjhd->brnihj', qs, kw)
    qi = jnp.arange(blk)[:, None] + blk
    kj = jnp.arange(2 * blk)[None, :]
    dist = qi - kj
    key_pos = jnp.arange(nb)[:, None, None] * blk + kj[None] - blk
    valid = ((dist >= 0) & (dist <= span))[None] & (key_pos >= 0)
    valid = valid[:, :, None, :]
    s = jnp.where(valid, s, NEG_INF)
    m = jnp.max(s, axis=-1)
    p = jnp.where(valid, jnp.exp(s - m[..., None]), 0.0)
    l = jnp.sum(p, axis=-1)
    acc = jnp.einsum('brnihj,brnjhd->brnihd', p, vw)

    def unstride(a):
        a = a.reshape((b, dilation, L) + a.shape[4:])
        a = jnp.moveaxis(a, 1, 2).reshape((b, t_pad) + a.shape[3:])
        return a[:, :t]

    return unstride(m), unstride(l), unstride(acc)


def dilated_sample(q, k_all, v_all, window, dilation, buf_len):
    t = q.shape[1]
    steps = jnp.arange(window // dilation + 1)
    idx = buf_len + jnp.arange(t)[:, None] - steps[None, :] * dilation
    valid = (idx >= 0)[None, :, None, :]
    idx_c = jnp.maximum(idx, 0)
    kg = k_all[:, idx_c].astype(F32)
    vg = v_all[:, idx_c].astype(F32)
    s = jnp.einsum('bthd,btkhd->bthk', q.astype(F32), kg)
    s = jnp.where(valid, s, NEG_INF)
    m = jnp.max(s, axis=-1)
    p = jnp.where(valid, jnp.exp(s - m[..., None]), 0.0)
    l = jnp.sum(p, axis=-1)
    acc = jnp.einsum('bthk,btkhd->bthd', p, vg)
    return m, l, acc


def combine_dilations(parts):
    ms = jnp.stack([p[0] for p in parts])
    mmax = jnp.max(ms, axis=0)
    wts = jnp.exp(ms - mmax[None])
    num = sum(wts[i][..., None] * parts[i][2] for i in range(len(parts)))
    den = sum(wts[i] * parts[i][1] for i in range(len(parts)))
    return num / den[..., None]


def memory_attend(q, mem_k, mem_v):
    b, t, _ = q.shape
    qh = q.reshape(b, t, MEM_HEADS, MEM_HD).astype(F32) * (MEM_HD ** -0.5)
    s = jnp.einsum('bthd,bmhd->bthm', qh, mem_k.astype(F32))
    p = jax.nn.softmax(s, axis=-1)
    o = jnp.einsum('bthm,bmhd->bthd', p, mem_v.astype(F32))
    return o.reshape(b, t, -1).astype(q.dtype)


def even_mix(xn, pos, ret_state0, swa_past, mem_k, mem_v, w_in, ret_gn, w_out):
    b, t, _ = xn.shape
    proj = xn @ w_in
    aq, ak, av, ag, bq, bk, bv, bg, mq, mg = split_cols(proj, EVEN_IN_WIDTHS)
    aq = rotary(aq.reshape(b, t, RET_HEADS, RET_DK), pos, RET_DK, RET_THETA)
    ak = rotary(ak.reshape(b, t, RET_HEADS, RET_DK), pos, RET_DK, RET_THETA) * (RET_DK ** -0.5)
    av = av.reshape(b, t, RET_HEADS, RET_DV)
    ret_state, ao = retention(ret_state0, aq, ak, av)
    ya = head_group_norm(ao, ret_gn).astype(xn.dtype) * jax.nn.silu(ag)
    bq = rotary(bq.reshape(b, t, SWA_HEADS, SWA_HD), pos, ROPE_DIMS, ROPE_THETA) * (SWA_HD ** -0.5)
    bk = rotary(bk.reshape(b, t, SWA_HEADS, SWA_HD), pos, ROPE_DIMS, ROPE_THETA)
    bv = bv.reshape(b, t, SWA_HEADS, SWA_HD)
    if swa_past is None:
        parts = [dilated_prompt(bq, bk, bv, w, r) for (w, r) in SWA_PATTERNS]
        buf = min(SWA_MAX_WINDOW, t)
        new_k, new_v = bk[:, t - buf:], bv[:, t - buf:]
    else:
        past_k, past_v = swa_past
        k_all = jnp.concatenate([past_k.astype(bk.dtype), bk], axis=1)
        v_all = jnp.concatenate([past_v.astype(bv.dtype), bv], axis=1)
        parts = [dilated_sample(bq, k_all, v_all, w, r, past_k.shape[1]) for (w, r) in SWA_PATTERNS]
        new_k, new_v = bk, bv
    yb = combine_dilations(parts).reshape(b, t, -1).astype(xn.dtype) * jax.nn.silu(bg)
    ym = memory_attend(mq, mem_k, mem_v) * jax.nn.silu(mg)
    out = jnp.concatenate([ya, yb, ym], axis=-1) @ w_out
    return out, ret_state, new_k, new_v


def spatial_gating(u, v, w_s, b_s):
    b, t, w = v.shape
    c = GMLP_CHUNK
    t_pad = -(-t // c) * c
    nc = t_pad // c
    vp = jnp.pad(v, ((0, 0), (0, t_pad - t), (0, 0))).reshape(b, nc, c, GMLP_GROUPS, GMLP_GROUP)
    w_causal = w_s * jnp.tril(jnp.ones((c, c), w_s.dtype))[None]
    mixed = jnp.einsum('gts,bcsgk->bctgk', w_causal, vp) + b_s.T[None, None, :, :, None]
    mixed = mixed.reshape(b, t_pad, w)[:, :t]
    return u * mixed


def odd_mix(xn, mem_k, mem_v, w_in, gmlp_ln, w_s, b_s, w_out):
    t = xn.shape[1]
    proj = xn @ w_in
    uv, z, mq, mg = split_cols(proj, ODD_IN_WIDTHS)
    uv = jax.nn.gelu(uv)
    u, v = uv[..., :GMLP_WIDTH], uv[..., GMLP_WIDTH:]
    v = layer_norm(v, gmlp_ln)
    yc = spatial_gating(u, v, w_s, b_s) * jax.nn.silu(z)
    ym = memory_attend(mq, mem_k, mem_v) * jax.nn.silu(mg)
    out = jnp.concatenate([yc, ym], axis=-1) @ w_out
    last_start = ((t - 1) // GMLP_CHUNK) * GMLP_CHUNK
    return out, v[:, last_start:]


def setup_inputs(seed: int = 0) -> dict:
    key = jax.random.key(seed)
    ks = jax.random.split(key, 24)
    buf = min(SWA_MAX_WINDOW, PAST_LEN)

    def nrm(k, shape, scale=1.0):
        return jax.random.normal(k, shape, F32) * scale

    def gain(k, shape):
        return 1.0 + 0.05 * jax.random.normal(k, shape, F32)

    return {
        'x_prompt': nrm(ks[0], (BATCH, SEQ, D_MODEL)),
        'x_sample': nrm(ks[1], (DEC_BATCH, DEC_SEQ, D_MODEL)),
        'state_ret': nrm(ks[2], (N_EVEN, DEC_BATCH, RET_HEADS, RET_DK, RET_DV)),
        'cache_swa_k': nrm(ks[3], (N_EVEN, DEC_BATCH, buf, SWA_HEADS, SWA_HD)),
        'cache_swa_v': nrm(ks[4], (N_EVEN, DEC_BATCH, buf, SWA_HEADS, SWA_HD)),
        'cache_mem_k': nrm(ks[5], (DEPTH, DEC_BATCH, MEM_LEN, MEM_HEADS, MEM_HD)),
        'cache_mem_v': nrm(ks[6], (DEPTH, DEC_BATCH, MEM_LEN, MEM_HEADS, MEM_HD)),
        'mem_prompt': nrm(ks[7], (BATCH, MEM_LEN, D_MODEL)),
        'pre_norm': gain(ks[8], (DEPTH, D_MODEL)),
        'post_norm': gain(ks[9], (DEPTH, D_MODEL)),
        'mem_norm': gain(ks[10], (DEPTH, D_MODEL)),
        'w_mem_k': nrm(ks[11], (DEPTH, D_MODEL, MEM_HEADS * MEM_HD), D_MODEL ** -0.5),
        'w_mem_v': nrm(ks[12], (DEPTH, D_MODEL, MEM_HEADS * MEM_HD), D_MODEL ** -0.5),
        'w_in_even': nrm(ks[13], (N_EVEN, D_MODEL, EVEN_IN), D_MODEL ** -0.5),
        'ret_gn': gain(ks[14], (N_EVEN, RET_HEADS * RET_DV)),
        'w_out_even': nrm(ks[15], (N_EVEN, EVEN_OUT, D_MODEL), EVEN_OUT ** -0.5),
        'w_in_odd': nrm(ks[16], (N_ODD, D_MODEL, ODD_IN), D_MODEL ** -0.5),
        'gmlp_ln': gain(ks[17], (N_ODD, GMLP_WIDTH)),
        'w_spatial': nrm(ks[18], (N_ODD, GMLP_GROUPS, GMLP_CHUNK, GMLP_CHUNK), GMLP_CHUNK ** -0.5),
        'b_spatial': 1.0 + 0.1 * jax.random.normal(ks[19], (N_ODD, GMLP_GROUPS, GMLP_CHUNK), F32),
        'w_out_odd': nrm(ks[20], (N_ODD, ODD_OUT, D_MODEL), ODD_OUT ** -0.5),
    }


def reference(x_prompt, x_sample, state_ret, cache_swa_k, cache_swa_v, cache_mem_k, cache_mem_v, mem_prompt,
              pre_norm, post_norm, mem_norm, w_mem_k, w_mem_v, w_in_even, ret_gn, w_out_even,
              w_in_odd, gmlp_ln, w_spatial, b_spatial, w_out_odd):
    bp, tp, _ = x_prompt.shape
    pos_p = jnp.arange(tp, dtype=jnp.int32)
    pos_s = PAST_LEN + jnp.arange(x_sample.shape[1], dtype=jnp.int32)
    hp, hs = x_prompt, x_sample
    p_ret, p_k, p_v, p_mk, p_mv, p_gv = [], [], [], [], [], []
    s_ret, s_k, s_v, s_gv = [], [], [], []
    for layer in range(DEPTH):
        e = layer // 2
        memn = rms_norm(mem_prompt, mem_norm[layer])
        mk_p = (memn @ w_mem_k[layer]).reshape(bp, MEM_LEN, MEM_HEADS, MEM_HD)
        mv_p = (memn @ w_mem_v[layer]).reshape(bp, MEM_LEN, MEM_HEADS, MEM_HD)
        p_mk.append(mk_p)
        p_mv.append(mv_p)
        xn_p = rms_norm(hp, pre_norm[layer])
        xn_s = rms_norm(hs, pre_norm[layer])
        if layer % 2 == 0:
            zero_state = jnp.zeros((bp, RET_HEADS, RET_DK, RET_DV), F32)
            op, st_p, kp, vp = even_mix(xn_p, pos_p, zero_state, None, mk_p, mv_p,
                                        w_in_even[e], ret_gn[e], w_out_even[e])
            osm, st_s, ksm, vsm = even_mix(xn_s, pos_s, state_ret[e], (cache_swa_k[e], cache_swa_v[e]),
                                           cache_mem_k[layer], cache_mem_v[layer],
                                           w_in_even[e], ret_gn[e], w_out_even[e])
            p_ret.append(st_p)
            p_k.append(kp)
            p_v.append(vp)
            s_ret.append(st_s)
            s_k.append(ksm)
            s_v.append(vsm)
        else:
            op, gv_p = odd_mix(xn_p, mk_p, mv_p, w_in_odd[e], gmlp_ln[e], w_spatial[e], b_spatial[e], w_out_odd[e])
            osm, gv_s = odd_mix(xn_s, cache_mem_k[layer], cache_mem_v[layer], w_in_odd[e], gmlp_ln[e],
                                w_spatial[e], b_spatial[e], w_out_odd[e])
            p_gv.append(gv_p)
            s_gv.append(gv_s)
        hp = hp + rms_norm(op, post_norm[layer])
        hs = hs + rms_norm(osm, post_norm[layer])
    return (hp, hs,
            jnp.stack(p_ret), jnp.stack(p_k), jnp.stack(p_v), jnp.stack(p_mk), jnp.stack(p_mv), jnp.stack(p_gv),
            jnp.stack(s_ret), jnp.stack(s_k), jnp.stack(s_v), jnp.stack(s_gv))
```

```python
import functools
import math

import jax
import jax.numpy as jnp
from jax import lax
from jax.experimental import pallas as pl
from jax.experimental.pallas import tpu as pltpu

F32 = jnp.float32
BF16 = jnp.bfloat16

D_MODEL = 2048
PAST_LEN = 8192
RET_HEADS = 8
RET_DK = 128
RET_DV = 256
RET_CHUNK = 128
RET_THETA = 10000.0
SWA_HEADS = 8
SWA_HD = 128
SWA_PATTERNS = ((128, 1), (512, 4), (2048, 16))
SWA_MAX_WINDOW = 2048
SWA_BLOCK = 128
ROPE_THETA = 500000.0
ROPE_DIMS = SWA_HD // 4
GMLP_CHUNK = 128
GMLP_WIDTH = D_MODEL
GMLP_GROUP = 128
GMLP_GROUPS = GMLP_WIDTH // GMLP_GROUP
MEM_LEN = 256
MEM_HEADS = 4
MEM_HD = 128
MEM_W = MEM_HEADS * MEM_HD
NORM_EPS = 1e-6
NEG_INF = -1e30

E_AQ = 0
E_AK = E_AQ + RET_HEADS * RET_DK
E_AV = E_AK + RET_HEADS * RET_DK
E_AG = E_AV + RET_HEADS * RET_DV
E_BQ = E_AG + RET_HEADS * RET_DV
E_BK = E_BQ + SWA_HEADS * SWA_HD
E_BV = E_BK + SWA_HEADS * SWA_HD
E_BG = E_BV + SWA_HEADS * SWA_HD
E_MQ = E_BG + SWA_HEADS * SWA_HD
E_MG = E_MQ + MEM_W
EVEN_IN = E_MG + MEM_W
O_U = 0
O_V = O_U + GMLP_WIDTH
O_Z = O_V + GMLP_WIDTH
O_MQ = O_Z + GMLP_WIDTH
O_MG = O_MQ + MEM_W
ODD_IN = O_MG + MEM_W

SWA_TILE = SWA_MAX_WINDOW
SAMPLE_PAD = 8
VMEM_LIMIT = 48 * 1024 * 1024


def _params(*sem):
    return pltpu.CompilerParams(dimension_semantics=sem, vmem_limit_bytes=VMEM_LIMIT)


def _dot(a, b):
    return jnp.dot(a, b, preferred_element_type=F32)


def _dot_nt(a, b):
    return lax.dot_general(a, b, (((1,), (1,)), ((), ())), preferred_element_type=F32)


def _silu(x):
    return x * jax.nn.sigmoid(x)


def _div(x, n):
    assert n & (n - 1) == 0
    return jnp.right_shift(x, n.bit_length() - 1)


def _mod(x, n):
    assert n & (n - 1) == 0
    return jnp.bitwise_and(x, n - 1)


def _rot_tables(pos, n_rot, theta, width):
    half = n_rot // 2
    inv = 1.0 / (theta ** (jnp.arange(half, dtype=F32) / half))
    ang = pos.astype(F32)[:, None] * inv[None, :]
    cos, sin = jnp.cos(ang), jnp.sin(ang)
    t = pos.shape[0]
    cos_t = jnp.concatenate([cos, cos, jnp.ones((t, width - n_rot), F32)], axis=-1)
    sin_t = jnp.concatenate([-sin, sin, jnp.zeros((t, width - n_rot), F32)], axis=-1)
    return cos_t, sin_t


def _rot_full(x, cos, sin):
    return x * cos + pltpu.roll(x, RET_DK // 2, 1) * sin


def _rot_partial(x, cos, sin):
    half = ROPE_DIMS // 2
    lane = _mod(lax.broadcasted_iota(jnp.int32, x.shape, 1), SWA_HD)
    width = x.shape[1]
    partner = jnp.where(lane < half, pltpu.roll(x, width - half, 1), pltpu.roll(x, half, 1))
    return x * cos + partner * sin


def _norm_matmul_kernel(x_ref, g_ref, w_ref, o_ref, xn_ref):
    @pl.when(pl.program_id(1) == 0)
    def _():
        x = x_ref[...]
        ms = jnp.mean(x * x, axis=-1, keepdims=True)
        xn_ref[...] = (x * lax.rsqrt(ms + NORM_EPS) * g_ref[...]).astype(BF16)

    o_ref[...] = _dot(xn_ref[...], w_ref[...]).astype(o_ref.dtype)


def _norm_matmul(x, g, w, tm, tn):
    m, d = x.shape
    n = w.shape[1]
    return pl.pallas_call(
        _norm_matmul_kernel,
        grid=(m // tm, n // tn),
        in_specs=[pl.BlockSpec((tm, d), lambda i, j: (i, 0)),
                  pl.BlockSpec((1, d), lambda i, j: (0, 0)),
                  pl.BlockSpec((d, tn), lambda i, j: (0, j))],
        out_specs=pl.BlockSpec((tm, tn), lambda i, j: (i, j)),
        out_shape=jax.ShapeDtypeStruct((m, n), F32),
        scratch_shapes=[pltpu.VMEM((tm, d), BF16)],
        compiler_params=_params("arbitrary", "arbitrary"),
        name="norm_matmul",
    )(x, g.reshape(1, d), w)


def _ret_tables(c, reps):
    lg = jnp.log1p(-jnp.exp2(-5.0 - jnp.arange(RET_HEADS, dtype=F32)))
    idx = jnp.arange(c, dtype=F32)
    rel = idx[:, None] - idx[None, :]
    decay = jnp.where(rel >= 0, jnp.exp(lg[:, None, None] * jnp.maximum(rel, 0.0)), 0.0)
    q_decay = jnp.exp(lg[:, None] * (idx[None, :] + 1.0))
    k_decay = jnp.exp(lg[:, None] * (c - 1.0 - idx[None, :]))
    chunk_decay = jnp.exp(lg * c)
    if reps > 1:
        eye = jnp.eye(reps, dtype=F32)
        decay = jax.vmap(lambda dm: jnp.kron(eye, dm))(decay)
        q_decay = jnp.tile(q_decay, (1, reps))
        k_decay = jnp.tile(k_decay, (1, reps))
    rows = c * reps
    qd = jnp.broadcast_to(q_decay[:, :, None], (RET_HEADS, rows, RET_DK))
    kd = jnp.broadcast_to(k_decay[:, :, None], (RET_HEADS, rows, RET_DK))
    cd = jnp.broadcast_to(chunk_decay[:, None, None], (RET_HEADS, 1, RET_DV))
    return decay, qd, kd, cd


def _group_norm_gate(o, gn, g):
    mu = jnp.mean(o, axis=-1, keepdims=True)
    d = o - mu
    var = jnp.mean(d * d, axis=-1, keepdims=True)
    return d * lax.rsqrt(var + NORM_EPS) * gn * _silu(g)


def _ret_prompt_kernel(q_ref, k_ref, v_ref, g_ref, cos_ref, sin_ref, dec_ref, qd_ref, kd_ref, cd_ref, gn_ref,
                       y_ref, st_ref):
    @pl.when(pl.program_id(0) == 0)
    def _():
        st_ref[...] = jnp.zeros_like(st_ref)

    cos = cos_ref[...]
    sin = sin_ref[...]
    for h in range(RET_HEADS):
        ks = slice(h * RET_DK, (h + 1) * RET_DK)
        vs = slice(h * RET_DV, (h + 1) * RET_DV)
        q = _rot_full(q_ref[:, ks], cos, sin)
        k = _rot_full(k_ref[:, ks] * (RET_DK ** -0.5), cos, sin)
        v = v_ref[:, vs].astype(BF16)
        st = st_ref[h]
        s = _dot_nt(q.astype(BF16), k.astype(BF16)) * dec_ref[h]
        o = _dot(s.astype(BF16), v) + _dot((q * qd_ref[h]).astype(BF16), st.astype(BF16))
        st_ref[h] = cd_ref[h] * st + _dot((k * kd_ref[h]).T.astype(BF16), v)
        y_ref[:, vs] = _group_norm_gate(o, gn_ref[:, vs], g_ref[:, vs]).astype(y_ref.dtype)


def _ret_prompt(proj, cos, sin, ret_gn):
    t = proj.shape[0]
    c = RET_CHUNK
    dec, qd, kd, cd = _ret_tables(c, 1)
    qw = RET_HEADS * RET_DK
    vw = RET_HEADS * RET_DV
    full3 = lambda n: (0, 0, 0)
    return pl.pallas_call(
        _ret_prompt_kernel,
        grid=(t // c,),
        in_specs=[pl.BlockSpec((c, qw), lambda n: (n, E_AQ // qw)),
                  pl.BlockSpec((c, qw), lambda n: (n, E_AK // qw)),
                  pl.BlockSpec((c, vw), lambda n: (n, E_AV // vw)),
                  pl.BlockSpec((c, vw), lambda n: (n, E_AG // vw)),
                  pl.BlockSpec((c, RET_DK), lambda n: (n, 0)),
                  pl.BlockSpec((c, RET_DK), lambda n: (n, 0)),
                  pl.BlockSpec(dec.shape, full3),
                  pl.BlockSpec(qd.shape, full3),
                  pl.BlockSpec(kd.shape, full3),
                  pl.BlockSpec(cd.shape, full3),
                  pl.BlockSpec((1, vw), lambda n: (0, 0))],
        out_specs=[pl.BlockSpec((c, vw), lambda n: (n, 0)),
                   pl.BlockSpec((RET_HEADS, RET_DK, RET_DV), full3)],
        out_shape=[jax.ShapeDtypeStruct((t, vw), BF16),
                   jax.ShapeDtypeStruct((RET_HEADS, RET_DK, RET_DV), F32)],
        compiler_params=_params("arbitrary"),
        name="ret_prompt",
    )(proj, proj, proj, proj, cos, sin, dec, qd, kd, cd, ret_gn.reshape(1, vw))


def _ret_sample_kernel(q_ref, k_ref, v_ref, g_ref, cos_ref, sin_ref, dec_ref, qd_ref, kd_ref, cd_ref, gn_ref,
                       st_ref, y_ref, sto_ref, cross_ref, *, seq):
    b = pl.program_id(0)
    nb = pl.num_programs(0)

    @pl.when(b == 0)
    def _():
        cross_ref[...] = jnp.zeros_like(cross_ref)

    cos = cos_ref[...]
    sin = sin_ref[...]
    rows = lax.broadcasted_iota(jnp.int32, (q_ref.shape[0], RET_DK), 0)
    mine = (rows >= b * seq) & (rows < (b + 1) * seq)
    for h in range(RET_HEADS):
        ks = slice(h * RET_DK, (h + 1) * RET_DK)
        vs = slice(h * RET_DV, (h + 1) * RET_DV)
        q = _rot_full(q_ref[:, ks], cos, sin)
        k = _rot_full(k_ref[:, ks] * (RET_DK ** -0.5), cos, sin)
        v = v_ref[:, vs].astype(BF16)
        st = st_ref[0, h]
        qm = jnp.where(mine, q * qd_ref[h], 0.0)
        km = jnp.where(mine, k * kd_ref[h], 0.0)
        cross_ref[:, vs] += _dot(qm.astype(BF16), st.astype(BF16))
        sto_ref[0, h] = cd_ref[h] * st + _dot(km.T.astype(BF16), v)

    @pl.when(b == nb - 1)
    def _():
        for h in range(RET_HEADS):
            ks = slice(h * RET_DK, (h + 1) * RET_DK)
            vs = slice(h * RET_DV, (h + 1) * RET_DV)
            q = _rot_full(q_ref[:, ks], cos, sin)
            k = _rot_full(k_ref[:, ks] * (RET_DK ** -0.5), cos, sin)
            v = v_ref[:, vs].astype(BF16)
            s = _dot_nt(q.astype(BF16), k.astype(BF16)) * dec_ref[h]
            o = _dot(s.astype(BF16), v) + cross_ref[:, vs]
            y_ref[:, vs] = _group_norm_gate(o, gn_ref[:, vs], g_ref[:, vs]).astype(y_ref.dtype)


def _ret_sample(proj, state, cos, sin, ret_gn, nseq, seq):
    rows = nseq * seq
    c = math.gcd(seq, RET_CHUNK)
    assert c == seq, "sample chunk must cover the new tokens"
    dec, qd, kd, cd = _ret_tables(c, nseq)
    qw = RET_HEADS * RET_DK
    vw = RET_HEADS * RET_DV
    full3 = lambda b: (0, 0, 0)
    st_spec = pl.BlockSpec((1, RET_HEADS, RET_DK, RET_DV), lambda b: (b, 0, 0, 0))
    return pl.pallas_call(
        functools.partial(_ret_sample_kernel, seq=seq),
        grid=(nseq,),
        in_specs=[pl.BlockSpec((rows, qw), lambda b: (0, E_AQ // qw)),
                  pl.BlockSpec((rows, qw), lambda b: (0, E_AK // qw)),
                  pl.BlockSpec((rows, vw), lambda b: (0, E_AV // vw)),
                  pl.BlockSpec((rows, vw), lambda b: (0, E_AG // vw)),
                  pl.BlockSpec((rows, RET_DK), lambda b: (0, 0)),
                  pl.BlockSpec((rows, RET_DK), lambda b: (0, 0)),
                  pl.BlockSpec(dec.shape, full3),
                  pl.BlockSpec(qd.shape, full3),
                  pl.BlockSpec(kd.shape, full3),
                  pl.BlockSpec(cd.shape, full3),
                  pl.BlockSpec((1, vw), lambda b: (0, 0)),
                  st_spec],
        out_specs=[pl.BlockSpec((rows, vw), lambda b: (0, 0)), st_spec],
        out_shape=[jax.ShapeDtypeStruct((rows, vw), F32),
                   jax.ShapeDtypeStruct(state.shape, F32)],
        scratch_shapes=[pltpu.VMEM((rows, vw), F32)],
        compiler_params=_params("arbitrary"),
        name="ret_sample",
    )(proj, proj, proj, proj, cos, sin, dec, qd, kd, cd, ret_gn.reshape(1, vw), state)


def _swa_prompt_kernel(q_ref, kc_ref, kp_ref, vc_ref, vp_ref, g_ref, cos_ref, sin_ref, cosp_ref, sinp_ref,
                       y_ref, ko_ref, q_s, k_s, v_s, m_s, l_s, acc_s):
    t = pl.program_id(1)
    nt = pl.num_programs(1)
    tb = SWA_TILE
    blk = SWA_BLOCK
    cos = cos_ref[...]
    sin = sin_ref[...]
    q_s[...] = _rot_partial(q_ref[...] * (SWA_HD ** -0.5), cos, sin)
    kr = _rot_partial(kc_ref[...], cos, sin)
    k_s[tb:2 * tb, :] = kr
    k_s[0:tb, :] = _rot_partial(kp_ref[...], cosp_ref[...], sinp_ref[...])
    v_s[tb:2 * tb, :] = vc_ref[...]
    v_s[0:tb, :] = vp_ref[...]

    @pl.when(t == nt - 1)
    def _():
        ko_ref[...] = kr

    m_s[...] = jnp.full(m_s.shape, NEG_INF, F32)
    l_s[...] = jnp.zeros_like(l_s)
    acc_s[...] = jnp.zeros_like(acc_s)

    qi = lax.broadcasted_iota(jnp.int32, (blk, 2 * blk), 0)
    kj = lax.broadcasted_iota(jnp.int32, (blk, 2 * blk), 1)
    band = (kj >= qi) & (kj <= qi + blk)
    band_first = band & ((kj >= blk) | (t > 0))

    def ds(start, size, stride):
        return pl.ds(start, size) if stride == 1 else pl.ds(start, size, stride=stride)

    for window, r in SWA_PATTERNS:
        assert window // r == blk
        for idx in range(tb // blk):
            c, n = idx % r, idx // r
            qs = n * blk * r + c
            ks = qs + tb - blk * r
            qrows = ds(qs, blk, r)
            krows = ds(ks, 2 * blk, r)
            valid = band_first if n == 0 else band
            s = _dot_nt(q_s[qrows, :].astype(BF16), k_s[krows, :].astype(BF16))
            s = jnp.where(valid, s, NEG_INF)
            m_old = m_s[qrows, :]
            m_new = jnp.maximum(m_old, jnp.max(s, axis=-1, keepdims=True))
            alpha = jnp.exp(m_old - m_new)
            p = jnp.where(valid, jnp.exp(s - m_new), 0.0)
            l_s[qrows, :] = alpha * l_s[qrows, :] + jnp.sum(p, axis=-1, keepdims=True)
            acc_s[qrows, :] = alpha * acc_s[qrows, :] + _dot(p.astype(BF16), v_s[krows, :].astype(BF16))
            m_s[qrows, :] = m_new

    y_ref[...] = (acc_s[...] / l_s[...] * _silu(g_ref[...])).astype(y_ref.dtype)


def _swa_prompt(proj, cos, sin):
    t = proj.shape[0]
    tb = SWA_TILE
    assert t % tb == 0
    hd = SWA_HD
    cur = lambda off: pl.BlockSpec((tb, hd), lambda h, i: (i, off // hd + h))
    prev = lambda off: pl.BlockSpec((tb, hd), lambda h, i: (jnp.maximum(i - 1, 0), off // hd + h))
    return pl.pallas_call(
        _swa_prompt_kernel,
        grid=(SWA_HEADS, t // tb),
        in_specs=[cur(E_BQ), cur(E_BK), prev(E_BK), cur(E_BV), prev(E_BV), cur(E_BG),
                  pl.BlockSpec((tb, hd), lambda h, i: (i, 0)),
                  pl.BlockSpec((tb, hd), lambda h, i: (i, 0)),
                  pl.BlockSpec((tb, hd), lambda h, i: (jnp.maximum(i - 1, 0), 0)),
                  pl.BlockSpec((tb, hd), lambda h, i: (jnp.maximum(i - 1, 0), 0))],
        out_specs=[pl.BlockSpec((tb, hd), lambda h, i: (i, h)),
                   pl.BlockSpec((tb, hd), lambda h, i: (0, h))],
        out_shape=[jax.ShapeDtypeStruct((t, SWA_HEADS * hd), BF16),
                   jax.ShapeDtypeStruct((tb, SWA_HEADS * hd), F32)],
        scratch_shapes=[pltpu.VMEM((tb, hd), F32),
                        pltpu.VMEM((2 * tb, hd), F32),
                        pltpu.VMEM((2 * tb, hd), F32),
                        pltpu.VMEM((tb, 1), F32),
                        pltpu.VMEM((tb, 1), F32),
                        pltpu.VMEM((tb, hd), F32)],
        compiler_params=_params("arbitrary", "arbitrary"),
        name="swa_prompt",
    )(proj, proj, proj, proj, proj, proj, cos, sin, cos, sin)


def _swa_sample_kernel(q_ref, k_ref, v_ref, g_ref, cos_ref, sin_ref, ka_ref, kb_ref, va_ref, vb_ref,
                       y_ref, ko_ref, kn_s, vn_s, *, seq):
    pad = SAMPLE_PAD
    w = SWA_HEADS * SWA_HD
    na = ka_ref.shape[1]
    nbk = kb_ref.shape[1]
    cos = cos_ref[...]
    sin = sin_ref[...]
    q8 = _rot_partial(q_ref[0] * (SWA_HD ** -0.5), cos, sin)
    k8 = _rot_partial(k_ref[0], cos, sin)
    ko_ref[0] = k8
    kn_s[...] = jnp.zeros_like(kn_s)
    vn_s[...] = jnp.zeros_like(vn_s)
    kn_s[0:pad, :] = k8.astype(BF16)
    vn_s[0:pad, :] = v_ref[0].astype(BF16)
    nn = kn_s.shape[0]

    nr = SWA_HEADS * pad
    r1 = lax.broadcasted_iota(jnp.int32, (nr, w), 0)
    c1 = lax.broadcasted_iota(jnp.int32, (nr, w), 1)
    qbd = jnp.where(_div(r1, pad) == _div(c1, SWA_HD),jnp.concatenate([q8] * SWA_HEADS, axis=0), 0.0)
    r2 = lax.broadcasted_iota(jnp.int32, (nr, seq * w), 0)
    c2 = lax.broadcasted_iota(jnp.int32, (nr, seq * w), 1)
    qbd2 = jnp.where(_mod(r2, pad) == _div(c2, w),jnp.concatenate([qbd] * seq, axis=1), 0.0)
    qbd = qbd.astype(BF16)
    qbd2 = qbd2.astype(BF16)

    s_a = _dot_nt(qbd, ka_ref[0].astype(BF16))
    s_b = _dot_nt(qbd2, kb_ref[0].astype(BF16))
    s_n = _dot_nt(qbd, kn_s[...])

    (w1, d1), (w2, d2), (w3, d3) = SWA_PATTERNS
    buf = SWA_MAX_WINDOW
    base = buf - na
    ta = _mod(lax.broadcasted_iota(jnp.int32, (nr, na), 0), pad)
    ia = lax.broadcasted_iota(jnp.int32, (nr, na), 1)
    dist = buf + ta - (base + ia)
    mult_a = ((dist <= w1).astype(F32) + ((_mod(dist, d2) == 0) & (dist <= w2)).astype(F32))
    mult_a = jnp.where(ta < seq, mult_a, 0.0)
    tb_ = _mod(lax.broadcasted_iota(jnp.int32, (nr, nbk), 0), pad)
    mult_b = jnp.where(tb_ < seq, 1.0, 0.0)
    tn = _mod(lax.broadcasted_iota(jnp.int32, (nr, nn), 0), pad)
    jn = lax.broadcasted_iota(jnp.int32, (nr, nn), 1)
    mult_n = (jn <= tn).astype(F32) + 2.0 * (jn == tn).astype(F32)
    mult_n = jnp.where((tn < seq) & (jn < seq), mult_n, 0.0)

    def rowmax(s, mult):
        return jnp.max(jnp.where(mult > 0, s, NEG_INF), axis=-1, keepdims=True)

    m = jnp.maximum(jnp.maximum(rowmax(s_a, mult_a), rowmax(s_b, mult_b)), rowmax(s_n, mult_n))

    def probs(s, mult):
        return jnp.where(mult > 0, jnp.exp(s - m), 0.0) * mult

    p_a, p_b, p_n = probs(s_a, mult_a), probs(s_b, mult_b), probs(s_n, mult_n)
    l = (jnp.sum(p_a, axis=-1, keepdims=True) + jnp.sum(p_b, axis=-1, keepdims=True)
         + jnp.sum(p_n, axis=-1, keepdims=True))
    l = jnp.where(l > 0, l, 1.0)
    o_a = _dot(p_a.astype(BF16), va_ref[0].astype(BF16))
    o_b = _dot(p_b.astype(BF16), vb_ref[0].astype(BF16))
    o_n = _dot(p_n.astype(BF16), vn_s[...])
    o_an = (o_a + o_n) / l
    o_b = o_b / l
    step = lax.broadcasted_iota(jnp.int32, (pad, SWA_HD), 0)
    outs = []
    for h in range(SWA_HEADS):
        rs = slice(h * pad, (h + 1) * pad)
        o = o_an[rs, h * SWA_HD:(h + 1) * SWA_HD]
        for t in range(seq):
            o = o + jnp.where(step == t, o_b[rs, t * w + h * SWA_HD:t * w + (h + 1) * SWA_HD], 0.0)
        outs.append(o)
    y_ref[0] = jnp.concatenate(outs, axis=1) * _silu(g_ref[0])


def _swa_sample(proj_pad, cache_k, cache_v, cos, sin, seq):
    nseq, pad, _ = proj_pad.shape
    buf = cache_k.shape[1]
    w = SWA_HEADS * SWA_HD
    (w1, d1), (w2, d2), (w3, d3) = SWA_PATTERNS
    assert buf == w3 and d1 == 1 and seq <= d2 and w1 <= w2 and buf % w2 == 0 and buf % d3 == 0
    flat = lambda a: a.reshape(nseq, buf, w)
    strided = lambda a: a.reshape(nseq, buf // d3, d3 * w)
    col = lambda off: pl.BlockSpec((1, pad, w), lambda b: (b, 0, off // w))
    tail = pl.BlockSpec((1, w2, w), lambda b: (b, buf // w2 - 1, 0))
    resid = pl.BlockSpec((1, buf // d3, seq * w), lambda b: (b, 0, 0))
    tab = pl.BlockSpec((pad, w), lambda b: (0, 0))
    out = pl.BlockSpec((1, pad, w), lambda b: (b, 0, 0))
    return pl.pallas_call(
        functools.partial(_swa_sample_kernel, seq=seq),
        grid=(nseq,),
        in_specs=[col(E_BQ), col(E_BK), col(E_BV), col(E_BG), tab, tab, tail, resid, tail, resid],
        out_specs=[out, out],
        out_shape=[jax.ShapeDtypeStruct((nseq, pad, w), F32),
                   jax.ShapeDtypeStruct((nseq, pad, w), F32)],
        scratch_shapes=[pltpu.VMEM((SWA_BLOCK, w), BF16), pltpu.VMEM((SWA_BLOCK, w), BF16)],
        compiler_params=_params("arbitrary"),
        name="swa_sample",
    )(proj_pad, proj_pad, proj_pad, proj_pad, cos, sin,
      flat(cache_k), strided(cache_k), flat(cache_v), strided(cache_v))


def _mem_attn_kernel(q_ref, g_ref, k_ref, v_ref, o_ref):
    for h in range(MEM_HEADS):
        hs = slice(h * MEM_HD, (h + 1) * MEM_HD)
        q = (q_ref[0, :, hs] * (MEM_HD ** -0.5)).astype(BF16)
        s = _dot_nt(q, k_ref[0, :, hs].astype(BF16))
        m = jnp.max(s, axis=-1, keepdims=True)
        p = jnp.exp(s - m)
        l = jnp.sum(p, axis=-1, keepdims=True)
        o = _dot(p.astype(BF16), v_ref[0, :, hs].astype(BF16)) / l
        o_ref[0, :, hs] = (o * _silu(g_ref[0, :, hs])).astype(o_ref.dtype)


def _mem_attn(proj, q_off, g_off, mem_k, k_off, mem_v, v_off, tq, out_dtype):
    b, t, _ = proj.shape
    w = MEM_W
    return pl.pallas_call(
        _mem_attn_kernel,
        grid=(b, t // tq),
        in_specs=[pl.BlockSpec((1, tq, w), lambda i, j: (i, j, q_off // w)),
                  pl.BlockSpec((1, tq, w), lambda i, j: (i, j, g_off // w)),
                  pl.BlockSpec((1, MEM_LEN, w), lambda i, j: (i, 0, k_off // w)),
                  pl.BlockSpec((1, MEM_LEN, w), lambda i, j: (i, 0, v_off // w))],
        out_specs=pl.BlockSpec((1, tq, w), lambda i, j: (i, j, 0)),
        out_shape=jax.ShapeDtypeStruct((b, t, w), out_dtype),
        compiler_params=_params("arbitrary", "arbitrary"),
        name="mem_attn",
    )(proj, proj, mem_k, mem_v)


def _gmlp_kernel(u_ref, v_ref, z_ref, ln_ref, w_ref, b_ref, y_ref, vo_ref):
    c = GMLP_CHUNK
    v = jax.nn.gelu(v_ref[...])
    mu = jnp.mean(v, axis=-1, keepdims=True)
    d = v - mu
    var = jnp.mean(d * d, axis=-1, keepdims=True)
    vn = d * lax.rsqrt(var + NORM_EPS) * ln_ref[...]
    vo_ref[...] = vn
    causal = (lax.broadcasted_iota(jnp.int32, (c, c), 0) >= lax.broadcasted_iota(jnp.int32, (c, c), 1))
    for g in range(GMLP_GROUPS):
        gs = slice(g * GMLP_GROUP, (g + 1) * GMLP_GROUP)
        wc = jnp.where(causal, w_ref[g], 0.0).astype(BF16)
        mixed = _dot(wc, vn[:, gs].astype(BF16)) + b_ref[:, gs]
        y_ref[:, gs] = (jax.nn.gelu(u_ref[:, gs]) * mixed * _silu(z_ref[:, gs])).astype(y_ref.dtype)


def _gmlp(proj, gmlp_ln, w_mix, bias_rows):
    t = proj.shape[0]
    c = GMLP_CHUNK
    w = GMLP_WIDTH
    col = lambda off: pl.BlockSpec((c, w), lambda n: (n, off // w))
    return pl.pallas_call(
        _gmlp_kernel,
        grid=(t // c,),
        in_specs=[col(O_U), col(O_V), col(O_Z),
                  pl.BlockSpec((1, w), lambda n: (0, 0)),
                  pl.BlockSpec(w_mix.shape, lambda n: (0, 0, 0)),
                  pl.BlockSpec((c, w), lambda n: (0, 0))],
        out_specs=[pl.BlockSpec((c, w), lambda n: (n, 0)),
                   pl.BlockSpec((c, w), lambda n: (0, 0))],
        out_shape=[jax.ShapeDtypeStruct((t, w), BF16),
                   jax.ShapeDtypeStruct((c, w), F32)],
        compiler_params=_params("arbitrary"),
        name="gmlp",
    )(proj, proj, proj, gmlp_ln.reshape(1, w), w_mix, bias_rows)


def _out_proj_kernel(*refs, n_in):
    a_refs, w_refs = refs[:n_in], refs[n_in:2 * n_in]
    g_ref, x_ref, o_ref = refs[2 * n_in:]
    acc = _dot(a_refs[0][...].astype(BF16), w_refs[0][...])
    for a_ref, w_ref in zip(a_refs[1:], w_refs[1:]):
        acc = acc + _dot(a_ref[...].astype(BF16), w_ref[...])
    ms = jnp.mean(acc * acc, axis=-1, keepdims=True)
    o_ref[...] = x_ref[...] + acc * lax.rsqrt(ms + NORM_EPS) * g_ref[...]


def _out_proj(acts, weights, g, x, tm):
    m, d = x.shape
    n_in = len(acts)
    return pl.pallas_call(
        functools.partial(_out_proj_kernel, n_in=n_in),
        grid=(m // tm,),
        in_specs=([pl.BlockSpec((tm, a.shape[1]), lambda i: (i, 0)) for a in acts]
                  + [pl.BlockSpec(w.shape, lambda i: (0, 0)) for w in weights]
                  + [pl.BlockSpec((1, d), lambda i: (0, 0)),
                     pl.BlockSpec((tm, d), lambda i: (i, 0))]),
        out_specs=pl.BlockSpec((tm, d), lambda i: (i, 0)),
        out_shape=jax.ShapeDtypeStruct((m, d), F32),
        compiler_params=_params("arbitrary"),
        name="out_proj",
    )(*acts, *weights, g.reshape(1, d), x)


def _split_rows(w, widths):
    out, start = [], 0
    for n in widths:
        out.append(w[start:start + n])
        start += n
    return out


def _pad_rows(a, nseq, seq):
    return jnp.pad(a.reshape(nseq, seq, a.shape[-1]), ((0, 0), (0, SAMPLE_PAD - seq), (0, 0)))


def kernel(x_prompt, x_sample, state_ret, cache_swa_k, cache_swa_v, cache_mem_k, cache_mem_v, mem_prompt,
           pre_norm, post_norm, mem_norm, w_mem_k, w_mem_v, w_in_even, ret_gn, w_out_even,
           w_in_odd, gmlp_ln, w_spatial, b_spatial, w_out_odd):
    bp, tp, d = x_prompt.shape
    nseq, seq, _ = x_sample.shape
    assert bp == 1 and seq <= SAMPLE_PAD and (nseq * seq) % GMLP_CHUNK == 0
    rows_s = nseq * seq
    hp = x_prompt.reshape(tp, d)
    hs = x_sample.reshape(rows_s, d)
    mem = mem_prompt.reshape(MEM_LEN, d)
    tm_p = 1024 if tp % 1024 == 0 else 512

    pos_p = jnp.arange(tp, dtype=jnp.int32)
    pos_s = PAST_LEN + jnp.arange(seq, dtype=jnp.int32)
    ret_cos_p, ret_sin_p = _rot_tables(pos_p, RET_DK, RET_THETA, RET_DK)
    ret_cos_s, ret_sin_s = _rot_tables(jnp.tile(pos_s, nseq), RET_DK, RET_THETA, RET_DK)
    swa_cos_p, swa_sin_p = _rot_tables(pos_p, ROPE_DIMS, ROPE_THETA, SWA_HD)
    pos_pad = jnp.concatenate([pos_s, jnp.zeros((SAMPLE_PAD - seq,), jnp.int32)])
    swa_cos_s, swa_sin_s = _rot_tables(pos_pad, ROPE_DIMS, ROPE_THETA, SWA_HD)
    swa_cos_s = jnp.tile(swa_cos_s, (1, SWA_HEADS))
    swa_sin_s = jnp.tile(swa_sin_s, (1, SWA_HEADS))

    w_mem0 = jnp.concatenate([w_mem_k[0], w_mem_v[0]], axis=1).astype(BF16)
    memkv0 = _norm_matmul(mem, mem_norm[0], w_mem0, MEM_LEN, 2 * MEM_W)
    w_in0 = w_in_even[0].astype(BF16)
    wo_a, wo_b, wo_m = _split_rows(w_out_even[0].astype(BF16),
                                   (RET_HEADS * RET_DV, SWA_HEADS * SWA_HD, MEM_W))

    proj_p = _norm_matmul(hp, pre_norm[0], w_in0, tm_p, 1024)
    ya_p, st_p = _ret_prompt(proj_p, ret_cos_p, ret_sin_p, ret_gn[0])
    yb_p, pk = _swa_prompt(proj_p, swa_cos_p, swa_sin_p)
    ym_p = _mem_attn(proj_p[None], E_MQ, E_MG, memkv0[None], 0, memkv0[None], MEM_W, 1024, BF16)[0]
    hp1 = _out_proj([ya_p, yb_p, ym_p], [wo_a, wo_b, wo_m], post_norm[0], hp, 256)
    buf_p = min(SWA_MAX_WINDOW, tp)
    pv = proj_p[tp - buf_p:, E_BV:E_BV + SWA_HEADS * SWA_HD]

    proj_s = _norm_matmul(hs, pre_norm[0], w_in0, rows_s, 1024)
    proj_s8 = _pad_rows(proj_s, nseq, seq)
    ya_s, st_s = _ret_sample(proj_s, state_ret[0], ret_cos_s, ret_sin_s, ret_gn[0], nseq, seq)
    yb_s8, sk8 = _swa_sample(proj_s8, cache_swa_k[0], cache_swa_v[0], swa_cos_s, swa_sin_s, seq)
    ck0 = cache_mem_k[0].reshape(nseq, MEM_LEN, MEM_W)
    cv0 = cache_mem_v[0].reshape(nseq, MEM_LEN, MEM_W)
    ym_s8 = _mem_attn(proj_s8, E_MQ, E_MG, ck0, 0, cv0, 0, SAMPLE_PAD, F32)
    yb_s = yb_s8[:, :seq].reshape(rows_s, -1)
    ym_s = ym_s8[:, :seq].reshape(rows_s, -1)
    hs1 = _out_proj([ya_s, yb_s, ym_s], [wo_a, wo_b, wo_m], post_norm[0], hs, rows_s)
    sk = sk8[:, :seq].reshape(nseq, seq, SWA_HEADS, SWA_HD)
    sv = proj_s[:, E_BV:E_BV + SWA_HEADS * SWA_HD].reshape(nseq, seq, SWA_HEADS, SWA_HD)

    w_mem1 = jnp.concatenate([w_mem_k[1], w_mem_v[1]], axis=1).astype(BF16)
    memkv1 = _norm_matmul(mem, mem_norm[1], w_mem1, MEM_LEN, 2 * MEM_W)
    w_in1 = w_in_odd[0].astype(BF16)
    wo_c, wo_m1 = _split_rows(w_out_odd[0].astype(BF16), (GMLP_WIDTH, MEM_W))
    c = GMLP_CHUNK
    bias_p = jnp.repeat(b_spatial[0].T, GMLP_GROUP, axis=1)
    w_mix_s = jax.vmap(lambda wg: jnp.kron(jnp.eye(c // seq, dtype=F32), wg[:seq, :seq]))(w_spatial[0])
    bias_s = jnp.tile(bias_p[:seq], (c // seq, 1))

    proj_p1 = _norm_matmul(hp1, pre_norm[1], w_in1, tm_p, 1024)
    yc_p, gv_p = _gmlp(proj_p1, gmlp_ln[0], w_spatial[0], bias_p)
    ym_p1 = _mem_attn(proj_p1[None], O_MQ, O_MG, memkv1[None], 0, memkv1[None], MEM_W, 1024, BF16)[0]
    hp2 = _out_proj([yc_p, ym_p1], [wo_c, wo_m1], post_norm[1], hp1, 256)

    proj_s1 = _norm_matmul(hs1, pre_norm[1], w_in1, rows_s, 1024)
    yc_s, gv_s = _gmlp(proj_s1, gmlp_ln[0], w_mix_s, bias_s)
    ck1 = cache_mem_k[1].reshape(nseq, MEM_LEN, MEM_W)
    cv1 = cache_mem_v[1].reshape(nseq, MEM_LEN, MEM_W)
    ym_s1 = _mem_attn(_pad_rows(proj_s1, nseq, seq), O_MQ, O_MG, ck1, 0, cv1, 0, SAMPLE_PAD, F32)
    ym_s1 = ym_s1[:, :seq].reshape(rows_s, -1)
    hs2 = _out_proj([yc_s, ym_s1], [wo_c, wo_m1], post_norm[1], hs1, rows_s)

    memkv = jnp.stack([memkv0, memkv1])
    p_mk = memkv[:, :, :MEM_W].reshape(2, bp, MEM_LEN, MEM_HEADS, MEM_HD)
    p_mv = memkv[:, :, MEM_W:].reshape(2, bp, MEM_LEN, MEM_HEADS, MEM_HD)
    return (hp2.reshape(bp, tp, d), hs2.reshape(nseq, seq, d),
            st_p[None, None],
            pk.reshape(1, bp, buf_p, SWA_HEADS, SWA_HD), pv.reshape(1, bp, buf_p, SWA_HEADS, SWA_HD),
            p_mk, p_mv,
            gv_p[None, None],
            st_s[None],
            sk[None], sv[None],
            gv_s.reshape(1, nseq, seq, d))
```

```python
import functools
import math

import jax
import jax.numpy as jnp
from jax import lax
from jax.experimental import pallas as pl
from jax.experimental.pallas import tpu as pltpu

F32 = jnp.float32
BF16 = jnp.bfloat16

D_MODEL = 2048
PAST_LEN = 8192
RET_HEADS = 8
RET_DK = 128
RET_DV = 256
RET_CHUNK = 128
RET_THETA = 10000.0
SWA_HEADS = 8
SWA_HD = 128
SWA_PATTERNS = ((128, 1), (512, 4), (2048, 16))
SWA_MAX_WINDOW = 2048
SWA_BLOCK = 128
ROPE_THETA = 500000.0
ROPE_DIMS = SWA_HD // 4
GMLP_CHUNK = 128
GMLP_WIDTH = D_MODEL
GMLP_GROUP = 128
GMLP_GROUPS = GMLP_WIDTH // GMLP_GROUP
MEM_LEN = 256
MEM_HEADS = 4
MEM_HD = 128
MEM_W = MEM_HEADS * MEM_HD
NORM_EPS = 1e-6
NEG_INF = -1e30

E_AQ = 0
E_AK = E_AQ + RET_HEADS * RET_DK
E_AV = E_AK + RET_HEADS * RET_DK
E_AG = E_AV + RET_HEADS * RET_DV
E_BQ = E_AG + RET_HEADS * RET_DV
E_BK = E_BQ + SWA_HEADS * SWA_HD
E_BV = E_BK + SWA_HEADS * SWA_HD
E_BG = E_BV + SWA_HEADS * SWA_HD
E_MQ = E_BG + SWA_HEADS * SWA_HD
E_MG = E_MQ + MEM_W
EVEN_IN = E_MG + MEM_W
O_U = 0
O_V = O_U + GMLP_WIDTH
O_Z = O_V + GMLP_WIDTH
O_MQ = O_Z + GMLP_WIDTH
O_MG = O_MQ + MEM_W
ODD_IN = O_MG + MEM_W

SWA_TILE = SWA_MAX_WINDOW
SAMPLE_PAD = 8
VMEM_LIMIT = 48 * 1024 * 1024


def _params(*sem):
    return pltpu.CompilerParams(dimension_semantics=sem, vmem_limit_bytes=VMEM_LIMIT)


def _dot(a, b):
    return jnp.dot(a, b, preferred_element_type=F32)


def _dot_nt(a, b):
    return lax.dot_general(a, b, (((1,), (1,)), ((), ())), preferred_element_type=F32)


def _silu(x):
    return x * jax.nn.sigmoid(x)


def _div(x, n):
    assert n & (n - 1) == 0
    return jnp.right_shift(x, n.bit_length() - 1)


def _mod(x, n):
    assert n & (n - 1) == 0
    return jnp.bitwise_and(x, n - 1)


def _rot_tables(pos, n_rot, theta, width):
    half = n_rot // 2
    inv = 1.0 / (theta ** (jnp.arange(half, dtype=F32) / half))
    ang = pos.astype(F32)[:, None] * inv[None, :]
    cos, sin = jnp.cos(ang), jnp.sin(ang)
    t = pos.shape[0]
    cos_t = jnp.concatenate([cos, cos, jnp.ones((t, width - n_rot), F32)], axis=-1)
    sin_t = jnp.concatenate([-sin, sin, jnp.zeros((t, width - n_rot), F32)], axis=-1)
    return cos_t, sin_t


def _rot_full(x, cos, sin):
    return x * cos + pltpu.roll(x, RET_DK // 2, 1) * sin


def _rot_partial(x, cos, sin):
    half = ROPE_DIMS // 2
    lane = _mod(lax.broadcasted_iota(jnp.int32, x.shape, 1), SWA_HD)
    width = x.shape[1]
    partner = jnp.where(lane < half, pltpu.roll(x, width - half, 1), pltpu.roll(x, half, 1))
    return x * cos + partner * sin


def _norm_matmul_kernel(x_ref, g_ref, w_ref, o_ref, xn_ref):
    @pl.when(pl.program_id(1) == 0)
    def _():
        x = x_ref[...]
        ms = jnp.mean(x * x, axis=-1, keepdims=True)
        xn_ref[...] = (x * lax.rsqrt(ms + NORM_EPS) * g_ref[...]).astype(BF16)

    o_ref[...] = _dot(xn_ref[...], w_ref[...]).astype(o_ref.dtype)


def _norm_matmul(x, g, w, tm, tn):
    m, d = x.shape
    n = w.shape[1]
    return pl.pallas_call(
        _norm_matmul_kernel,
        grid=(m // tm, n // tn),
        in_specs=[pl.BlockSpec((tm, d), lambda i, j: (i, 0)),
                  pl.BlockSpec((1, d), lambda i, j: (0, 0)),
                  pl.BlockSpec((d, tn), lambda i, j: (0, j))],
        out_specs=pl.BlockSpec((tm, tn), lambda i, j: (i, j)),
        out_shape=jax.ShapeDtypeStruct((m, n), F32),
        scratch_shapes=[pltpu.VMEM((tm, d), BF16)],
        compiler_params=_params("arbitrary", "arbitrary"),
        name="norm_matmul",
    )(x, g.reshape(1, d), w)


def _ret_tables(c, reps):
    lg = jnp.log1p(-jnp.exp2(-5.0 - jnp.arange(RET_HEADS, dtype=F32)))
    idx = jnp.arange(c, dtype=F32)
    rel = idx[:, None] - idx[None, :]
    decay = jnp.where(rel >= 0, jnp.exp(lg[:, None, None] * jnp.maximum(rel, 0.0)), 0.0)
    q_decay = jnp.exp(lg[:, None] * (idx[None, :] + 1.0))
    k_decay = jnp.exp(lg[:, None] * (c - 1.0 - idx[None, :]))
    chunk_decay = jnp.exp(lg * c)
    if reps > 1:
        eye = jnp.eye(reps, dtype=F32)
        decay = jax.vmap(lambda dm: jnp.kron(eye, dm))(decay)
        q_decay = jnp.tile(q_decay, (1, reps))
        k_decay = jnp.tile(k_decay, (1, reps))
    rows = c * reps
    qd = jnp.broadcast_to(q_decay[:, :, None], (RET_HEADS, rows, RET_DK))
    kd = jnp.broadcast_to(k_decay[:, :, None], (RET_HEADS, rows, RET_DK))
    cd = jnp.broadcast_to(chunk_decay[:, None, None], (RET_HEADS, 1, RET_DV))
    return decay, qd, kd, cd


def _group_norm_gate(o, gn, g):
    mu = jnp.mean(o, axis=-1, keepdims=True)
    d = o - mu
    var = jnp.mean(d * d, axis=-1, keepdims=True)
    return d * lax.rsqrt(var + NORM_EPS) * gn * _silu(g)


def _ret_prompt_kernel(q_ref, k_ref, v_ref, g_ref, cos_ref, sin_ref, dec_ref, qd_ref, kd_ref, cd_ref, gn_ref,
                       y_ref, st_ref):
    @pl.when(pl.program_id(0) == 0)
    def _():
        st_ref[...] = jnp.zeros_like(st_ref)

    cos = cos_ref[...]
    sin = sin_ref[...]
    for h in range(RET_HEADS):
        ks = slice(h * RET_DK, (h + 1) * RET_DK)
        vs = slice(h * RET_DV, (h + 1) * RET_DV)
        q = _rot_full(q_ref[:, ks], cos, sin)
        k = _rot_full(k_ref[:, ks] * (RET_DK ** -0.5), cos, sin)
        v = v_ref[:, vs].astype(BF16)
        st = st_ref[h]
        s = _dot_nt(q.astype(BF16), k.astype(BF16)) * dec_ref[h]
        o = _dot(s.astype(BF16), v) + _dot((q * qd_ref[h]).astype(BF16), st.astype(BF16))
        st_ref[h] = cd_ref[h] * st + _dot((k * kd_ref[h]).T.astype(BF16), v)
        y_ref[:, vs] = _group_norm_gate(o, gn_ref[:, vs], g_ref[:, vs]).astype(y_ref.dtype)


def _ret_prompt(proj, cos, sin, ret_gn):
    t = proj.shape[0]
    c = RET_CHUNK
    dec, qd, kd, cd = _ret_tables(c, 1)
    qw = RET_HEADS * RET_DK
    vw = RET_HEADS * RET_DV
    full3 = lambda n: (0, 0, 0)
    return pl.pallas_call(
        _ret_prompt_kernel,
        grid=(t // c,),
        in_specs=[pl.BlockSpec((c, qw), lambda n: (n, E_AQ // qw)),
                  pl.BlockSpec((c, qw), lambda n: (n, E_AK // qw)),
                  pl.BlockSpec((c, vw), lambda n: (n, E_AV // vw)),
                  pl.BlockSpec((c, vw), lambda n: (n, E_AG // vw)),
                  pl.BlockSpec((c, RET_DK), lambda n: (n, 0)),
                  pl.BlockSpec((c, RET_DK), lambda n: (n, 0)),
                  pl.BlockSpec(dec.shape, full3),
                  pl.BlockSpec(qd.shape, full3),
                  pl.BlockSpec(kd.shape, full3),
                  pl.BlockSpec(cd.shape, full3),
                  pl.BlockSpec((1, vw), lambda n: (0, 0))],
        out_specs=[pl.BlockSpec((c, vw), lambda n: (n, 0)),
                   pl.BlockSpec((RET_HEADS, RET_DK, RET_DV), full3)],
        out_shape=[jax.ShapeDtypeStruct((t, vw), BF16),
                   jax.ShapeDtypeStruct((RET_HEADS, RET_DK, RET_DV), F32)],
        compiler_params=_params("arbitrary"),
        name="ret_prompt",
    )(proj, proj, proj, proj, cos, sin, dec, qd, kd, cd, ret_gn.reshape(1, vw))


def _ret_sample_kernel(q_ref, k_ref, v_ref, g_ref, cos_ref, sin_ref, dec_ref, qd_ref, kd_ref, cd_ref, gn_ref,
                       st_ref, y_ref, sto_ref, cross_ref, *, seq):
    b = pl.program_id(0)
    nb = pl.num_programs(0)

    @pl.when(b == 0)
    def _():
        cross_ref[...] = jnp.zeros_like(cross_ref)

    cos = cos_ref[...]
    sin = sin_ref[...]
    rows = lax.broadcasted_iota(jnp.int32, (q_ref.shape[0], RET_DK), 0)
    mine = (rows >= b * seq) & (rows < (b + 1) * seq)
    for h in range(RET_HEADS):
        ks = slice(h * RET_DK, (h + 1) * RET_DK)
        vs = slice(h * RET_DV, (h + 1) * RET_DV)
        q = _rot_full(q_ref[:, ks], cos, sin)
        k = _rot_full(k_ref[:, ks] * (RET_DK ** -0.5), cos, sin)
        v = v_ref[:, vs].astype(BF16)
        st = st_ref[0, h]
        qm = jnp.where(mine, q * qd_ref[h], 0.0)
        km = jnp.where(mine, k * kd_ref[h], 0.0)
        cross_ref[:, vs] += _dot(qm.astype(BF16), st.astype(BF16))
        sto_ref[0, h] = cd_ref[h] * st + _dot(km.T.astype(BF16), v)

    @pl.when(b == nb - 1)
    def _():
        for h in range(RET_HEADS):
            ks = slice(h * RET_DK, (h + 1) * RET_DK)
            vs = slice(h * RET_DV, (h + 1) * RET_DV)
            q = _rot_full(q_ref[:, ks], cos, sin)
            k = _rot_full(k_ref[:, ks] * (RET_DK ** -0.5), cos, sin)
            v = v_ref[:, vs].astype(BF16)
            s = _dot_nt(q.astype(BF16), k.astype(BF16)) * dec_ref[h]
            o = _dot(s.astype(BF16), v) + cross_ref[:, vs]
            y_ref[:, vs] = _group_norm_gate(o, gn_ref[:, vs], g_ref[:, vs]).astype(y_ref.dtype)


def _ret_sample(proj, state, cos, sin, ret_gn, nseq, seq):
    rows = nseq * seq
    c = math.gcd(seq, RET_CHUNK)
    assert c == seq, "sample chunk must cover the new tokens"
    dec, qd, kd, cd = _ret_tables(c, nseq)
    qw = RET_HEADS * RET_DK
    vw = RET_HEADS * RET_DV
    full3 = lambda b: (0, 0, 0)
    st_spec = pl.BlockSpec((1, RET_HEADS, RET_DK, RET_DV), lambda b: (b, 0, 0, 0))
    return pl.pallas_call(
        functools.partial(_ret_sample_kernel, seq=seq),
        grid=(nseq,),
        in_specs=[pl.BlockSpec((rows, qw), lambda b: (0, E_AQ // qw)),
                  pl.BlockSpec((rows, qw), lambda b: (0, E_AK // qw)),
                  pl.BlockSpec((rows, vw), lambda b: (0, E_AV // vw)),
                  pl.BlockSpec((rows, vw), lambda b: (0, E_AG // vw)),
                  pl.BlockSpec((rows, RET_DK), lambda b: (0, 0)),
                  pl.BlockSpec((rows, RET_DK), lambda b: (0, 0)),
                  pl.BlockSpec(dec.shape, full3),
                  pl.BlockSpec(qd.shape, full3),
                  pl.BlockSpec(kd.shape, full3),
                  pl.BlockSpec(cd.shape, full3),
                  pl.BlockSpec((1, vw), lambda b: (0, 0)),
                  st_spec],
        out_specs=[pl.BlockSpec((rows, vw), lambda b: (0, 0)), st_spec],
        out_shape=[jax.ShapeDtypeStruct((rows, vw), F32),
                   jax.ShapeDtypeStruct(state.shape, F32)],
        scratch_shapes=[pltpu.VMEM((rows, vw), F32)],
        compiler_params=_params("arbitrary"),
        name="ret_sample",
    )(proj, proj, proj, proj, cos, sin, dec, qd, kd, cd, ret_gn.reshape(1, vw), state)


def _swa_prompt_kernel(q_ref, kc_ref, kp_ref, vc_ref, vp_ref, g_ref, cos_ref, sin_ref, cosp_ref, sinp_ref,
                       y_ref, ko_ref, q_s, k_s, v_s, m_s, l_s, acc_s):
    t = pl.program_id(1)
    nt = pl.num_programs(1)
    tb = SWA_TILE
    blk = SWA_BLOCK
    cos = cos_ref[...]
    sin = sin_ref[...]
    q_s[...] = _rot_partial(q_ref[...] * (SWA_HD ** -0.5), cos, sin)
    kr = _rot_partial(kc_ref[...], cos, sin)
    k_s[tb:2 * tb, :] = kr
    k_s[0:tb, :] = _rot_partial(kp_ref[...], cosp_ref[...], sinp_ref[...])
    v_s[tb:2 * tb, :] = vc_ref[...]
    v_s[0:tb, :] = vp_ref[...]

    @pl.when(t == nt - 1)
    def _():
        ko_ref[...] = kr

    qi = lax.broadcasted_iota(jnp.int32, (blk, 2 * blk), 0)
    kj = lax.broadcasted_iota(jnp.int32, (blk, 2 * blk), 1)
    band = (kj >= qi) & (kj <= qi + blk)
    band_first = band & ((kj >= blk) | (t > 0))

    def ds(start, size, stride):
        return pl.ds(start, size) if stride == 1 else pl.ds(start, size, stride=stride)

    for bi, (window, r) in enumerate(SWA_PATTERNS):
        assert window // r == blk
        for idx in range(tb // blk):
            c, n = idx % r, idx // r
            qs = n * blk * r + c
            ks = qs + tb - blk * r
            qrows = ds(qs, blk, r)
            krows = ds(ks, 2 * blk, r)
            valid = band_first if n == 0 else band
            s = _dot_nt(q_s[qrows, :].astype(BF16), k_s[krows, :].astype(BF16))
            s = jnp.where(valid, s, NEG_INF)
            m = jnp.max(s, axis=-1, keepdims=True)
            p = jnp.where(valid, jnp.exp(s - m), 0.0)
            m_s[bi, qrows, :] = m
            l_s[bi, qrows, :] = jnp.sum(p, axis=-1, keepdims=True)
            acc_s[bi, qrows, :] = _dot(p.astype(BF16), v_s[krows, :].astype(BF16))

    nbr = len(SWA_PATTERNS)
    ms = [m_s[i] for i in range(nbr)]
    mmax = functools.reduce(jnp.maximum, ms)
    wts = [jnp.exp(mi - mmax) for mi in ms]
    num = sum(wts[i] * acc_s[i] for i in range(nbr))
    den = sum(wts[i] * l_s[i] for i in range(nbr))
    y_ref[...] = (num / den * _silu(g_ref[...])).astype(y_ref.dtype)


def _swa_prompt(proj, cos, sin):
    t = proj.shape[0]
    tb = SWA_TILE
    assert t % tb == 0
    hd = SWA_HD
    cur = lambda off: pl.BlockSpec((tb, hd), lambda h, i: (i, off // hd + h))
    prev = lambda off: pl.BlockSpec((tb, hd), lambda h, i: (jnp.maximum(i - 1, 0), off // hd + h))
    return pl.pallas_call(
        _swa_prompt_kernel,
        grid=(SWA_HEADS, t // tb),
        in_specs=[cur(E_BQ), cur(E_BK), prev(E_BK), cur(E_BV), prev(E_BV), cur(E_BG),
                  pl.BlockSpec((tb, hd), lambda h, i: (i, 0)),
                  pl.BlockSpec((tb, hd), lambda h, i: (i, 0)),
                  pl.BlockSpec((tb, hd), lambda h, i: (jnp.maximum(i - 1, 0), 0)),
                  pl.BlockSpec((tb, hd), lambda h, i: (jnp.maximum(i - 1, 0), 0))],
        out_specs=[pl.BlockSpec((tb, hd), lambda h, i: (i, h)),
                   pl.BlockSpec((tb, hd), lambda h, i: (0, h))],
        out_shape=[jax.ShapeDtypeStruct((t, SWA_HEADS * hd), BF16),
                   jax.ShapeDtypeStruct((tb, SWA_HEADS * hd), F32)],
        scratch_shapes=[pltpu.VMEM((tb, hd), F32),
                        pltpu.VMEM((2 * tb, hd), F32),
                        pltpu.VMEM((2 * tb, hd), F32),
                        pltpu.VMEM((len(SWA_PATTERNS), tb, 1), F32),
                        pltpu.VMEM((len(SWA_PATTERNS), tb, 1), F32),
                        pltpu.VMEM((len(SWA_PATTERNS), tb, hd), F32)],
        compiler_params=_params("arbitrary", "arbitrary"),
        name="swa_prompt",
    )(proj, proj, proj, proj, proj, proj, cos, sin, cos, sin)


def _swa_sample_kernel(q_ref, k_ref, v_ref, g_ref, cos_ref, sin_ref, ka_ref, kb_ref, va_ref, vb_ref,
                       y_ref, ko_ref, kn_s, vn_s, *, seq):
    pad = SAMPLE_PAD
    w = SWA_HEADS * SWA_HD
    na = ka_ref.shape[1]
    nbk = kb_ref.shape[1]
    cos = cos_ref[...]
    sin = sin_ref[...]
    q8 = _rot_partial(q_ref[0] * (SWA_HD ** -0.5), cos, sin)
    k8 = _rot_partial(k_ref[0], cos, sin)
    ko_ref[0] = k8
    kn_s[...] = jnp.zeros_like(kn_s)
    vn_s[...] = jnp.zeros_like(vn_s)
    kn_s[0:pad, :] = k8.astype(BF16)
    vn_s[0:pad, :] = v_ref[0].astype(BF16)
    nn = kn_s.shape[0]

    nr = SWA_HEADS * pad
    r1 = lax.broadcasted_iota(jnp.int32, (nr, w), 0)
    c1 = lax.broadcasted_iota(jnp.int32, (nr, w), 1)
    qbd = jnp.where(_div(r1, pad) == _div(c1, SWA_HD),jnp.concatenate([q8] * SWA_HEADS, axis=0), 0.0)
    r2 = lax.broadcasted_iota(jnp.int32, (nr, seq * w), 0)
    c2 = lax.broadcasted_iota(jnp.int32, (nr, seq * w), 1)
    qbd2 = jnp.where(_mod(r2, pad) == _div(c2, w),jnp.concatenate([qbd] * seq, axis=1), 0.0)
    qbd = qbd.astype(BF16)
    qbd2 = qbd2.astype(BF16)

    s_a = _dot_nt(qbd, ka_ref[0].astype(BF16))
    s_b = _dot_nt(qbd2, kb_ref[0].astype(BF16))
    s_n = _dot_nt(qbd, kn_s[...])

    (w1, d1), (w2, d2), (w3, d3) = SWA_PATTERNS
    buf = SWA_MAX_WINDOW
    base = buf - na
    ta = _mod(lax.broadcasted_iota(jnp.int32, (nr, na), 0), pad)
    ia = lax.broadcasted_iota(jnp.int32, (nr, na), 1)
    dist = buf + ta - (base + ia)
    mult_a = ((dist <= w1).astype(F32) + ((_mod(dist, d2) == 0) & (dist <= w2)).astype(F32))
    mult_a = jnp.where(ta < seq, mult_a, 0.0)
    tb_ = _mod(lax.broadcasted_iota(jnp.int32, (nr, nbk), 0), pad)
    mult_b = jnp.where(tb_ < seq, 1.0, 0.0)
    tn = _mod(lax.broadcasted_iota(jnp.int32, (nr, nn), 0), pad)
    jn = lax.broadcasted_iota(jnp.int32, (nr, nn), 1)
    mult_n = (jn <= tn).astype(F32) + 2.0 * (jn == tn).astype(F32)
    mult_n = jnp.where((tn < seq) & (jn < seq), mult_n, 0.0)

    def rowmax(s, mult):
        return jnp.max(jnp.where(mult > 0, s, NEG_INF), axis=-1, keepdims=True)

    m = jnp.maximum(jnp.maximum(rowmax(s_a, mult_a), rowmax(s_b, mult_b)), rowmax(s_n, mult_n))

    def probs(s, mult):
        return jnp.where(mult > 0, jnp.exp(s - m), 0.0) * mult

    p_a, p_b, p_n = probs(s_a, mult_a), probs(s_b, mult_b), probs(s_n, mult_n)
    l = (jnp.sum(p_a, axis=-1, keepdims=True) + jnp.sum(p_b, axis=-1, keepdims=True)
         + jnp.sum(p_n, axis=-1, keepdims=True))
    l = jnp.where(l > 0, l, 1.0)
    o_a = _dot(p_a.astype(BF16), va_ref[0].astype(BF16))
    o_b = _dot(p_b.astype(BF16), vb_ref[0].astype(BF16))
    o_n = _dot(p_n.astype(BF16), vn_s[...])
    o_an = (o_a + o_n) / l
    o_b = o_b / l
    step = lax.broadcasted_iota(jnp.int32, (pad, SWA_HD), 0)
    outs = []
    for h in range(SWA_HEADS):
        rs = slice(h * pad, (h + 1) * pad)
        o = o_an[rs, h * SWA_HD:(h + 1) * SWA_HD]
        for t in range(seq):
            o = o + jnp.where(step == t, o_b[rs, t * w + h * SWA_HD:t * w + (h + 1) * SWA_HD], 0.0)
        outs.append(o)
    y_ref[0] = jnp.concatenate(outs, axis=1) * _silu(g_ref[0])


def _swa_sample(proj_pad, cache_k, cache_v, cos, sin, seq):
    nseq, pad, _ = proj_pad.shape
    buf = cache_k.shape[1]
    w = SWA_HEADS * SWA_HD
    (w1, d1), (w2, d2), (w3, d3) = SWA_PATTERNS
    assert buf == w3 and d1 == 1 and seq <= d2 and w1 <= w2 and buf % w2 == 0 and buf % d3 == 0
    flat = lambda a: a[:, buf - w2:].reshape(nseq, w2, w)
    strided = lambda a: a.reshape(nseq, buf // d3, d3, w)[:, :, :seq].reshape(nseq, buf // d3, seq * w)
    col = lambda off: pl.BlockSpec((1, pad, w), lambda b: (b, 0, off // w))
    tail = pl.BlockSpec((1, w2, w), lambda b: (b, 0, 0))
    resid = pl.BlockSpec((1, buf // d3, seq * w), lambda b: (b, 0, 0))
    tab = pl.BlockSpec((pad, w), lambda b: (0, 0))
    out = pl.BlockSpec((1, pad, w), lambda b: (b, 0, 0))
    return pl.pallas_call(
        functools.partial(_swa_sample_kernel, seq=seq),
        grid=(nseq,),
        in_specs=[col(E_BQ), col(E_BK), col(E_BV), col(E_BG), tab, tab, tail, resid, tail, resid],
        out_specs=[out, out],
        out_shape=[jax.ShapeDtypeStruct((nseq, pad, w), F32),
                   jax.ShapeDtypeStruct((nseq, pad, w), F32)],
        scratch_shapes=[pltpu.VMEM((SWA_BLOCK, w), BF16), pltpu.VMEM((SWA_BLOCK, w), BF16)],
        compiler_params=_params("arbitrary"),
        name="swa_sample",
    )(proj_pad, proj_pad, proj_pad, proj_pad, cos, sin,
      flat(cache_k), strided(cache_k), flat(cache_v), strided(cache_v))


def _mem_attn_kernel(q_ref, g_ref, k_ref, v_ref, o_ref):
    for h in range(MEM_HEADS):
        hs = slice(h * MEM_HD, (h + 1) * MEM_HD)
        q = (q_ref[0, :, hs] * (MEM_HD ** -0.5)).astype(BF16)
        s = _dot_nt(q, k_ref[0, :, hs].astype(BF16))
        m = jnp.max(s, axis=-1, keepdims=True)
        p = jnp.exp(s - m)
        l = jnp.sum(p, axis=-1, keepdims=True)
        o = _dot(p.astype(BF16), v_ref[0, :, hs].astype(BF16)) / l
        o_ref[0, :, hs] = (o * _silu(g_ref[0, :, hs])).astype(o_ref.dtype)


def _mem_attn(proj, q_off, g_off, mem_k, k_off, mem_v, v_off, tq, out_dtype):
    b, t, _ = proj.shape
    w = MEM_W
    return pl.pallas_call(
        _mem_attn_kernel,
        grid=(b, t // tq),
        in_specs=[pl.BlockSpec((1, tq, w), lambda i, j: (i, j, q_off // w)),
                  pl.BlockSpec((1, tq, w), lambda i, j: (i, j, g_off // w)),
                  pl.BlockSpec((1, MEM_LEN, w), lambda i, j: (i, 0, k_off // w)),
                  pl.BlockSpec((1, MEM_LEN, w), lambda i, j: (i, 0, v_off // w))],
        out_specs=pl.BlockSpec((1, tq, w), lambda i, j: (i, j, 0)),
        out_shape=jax.ShapeDtypeStruct((b, t, w), out_dtype),
        compiler_params=_params("arbitrary", "arbitrary"),
        name="mem_attn",
    )(proj, proj, mem_k, mem_v)


def _gmlp_kernel(u_ref, v_ref, z_ref, ln_ref, w_ref, b_ref, y_ref, vo_ref):
    c = GMLP_CHUNK
    v = jax.nn.gelu(v_ref[...])
    mu = jnp.mean(v, axis=-1, keepdims=True)
    d = v - mu
    var = jnp.mean(d * d, axis=-1, keepdims=True)
    vn = d * lax.rsqrt(var + NORM_EPS) * ln_ref[...]
    vo_ref[...] = vn
    causal = (lax.broadcasted_iota(jnp.int32, (c, c), 0) >= lax.broadcasted_iota(jnp.int32, (c, c), 1))
    for g in range(GMLP_GROUPS):
        gs = slice(g * GMLP_GROUP, (g + 1) * GMLP_GROUP)
        wc = jnp.where(causal, w_ref[g], 0.0).astype(BF16)
        mixed = _dot(wc, vn[:, gs].astype(BF16)) + b_ref[:, gs]
        y_ref[:, gs] = (jax.nn.gelu(u_ref[:, gs]) * mixed * _silu(z_ref[:, gs])).astype(y_ref.dtype)


def _gmlp(proj, gmlp_ln, w_mix, bias_rows):
    t = proj.shape[0]
    c = GMLP_CHUNK
    w = GMLP_WIDTH
    col = lambda off: pl.BlockSpec((c, w), lambda n: (n, off // w))
    return pl.pallas_call(
        _gmlp_kernel,
        grid=(t // c,),
        in_specs=[col(O_U), col(O_V), col(O_Z),
                  pl.BlockSpec((1, w), lambda n: (0, 0)),
                  pl.BlockSpec(w_mix.shape, lambda n: (0, 0, 0)),
                  pl.BlockSpec((c, w), lambda n: (0, 0))],
        out_specs=[pl.BlockSpec((c, w), lambda n: (n, 0)),
                   pl.BlockSpec((c, w), lambda n: (0, 0))],
        out_shape=[jax.ShapeDtypeStruct((t, w), BF16),
                   jax.ShapeDtypeStruct((c, w), F32)],
        compiler_params=_params("arbitrary"),
        name="gmlp",
    )(proj, proj, proj, gmlp_ln.reshape(1, w), w_mix, bias_rows)


def _out_proj_kernel(*refs, n_in):
    a_refs, w_refs = refs[:n_in], refs[n_in:2 * n_in]
    g_ref, x_ref, o_ref = refs[2 * n_in:]
    acc = _dot(a_refs[0][...].astype(BF16), w_refs[0][...])
    for a_ref, w_ref in zip(a_refs[1:], w_refs[1:]):
        acc = acc + _dot(a_ref[...].astype(BF16), w_ref[...])
    ms = jnp.mean(acc * acc, axis=-1, keepdims=True)
    o_ref[...] = x_ref[...] + acc * lax.rsqrt(ms + NORM_EPS) * g_ref[...]


def _out_proj(acts, weights, g, x, tm):
    m, d = x.shape
    n_in = len(acts)
    return pl.pallas_call(
        functools.partial(_out_proj_kernel, n_in=n_in),
        grid=(m // tm,),
        in_specs=([pl.BlockSpec((tm, a.shape[1]), lambda i: (i, 0)) for a in acts]
                  + [pl.BlockSpec(w.shape, lambda i: (0, 0)) for w in weights]
                  + [pl.BlockSpec((1, d), lambda i: (0, 0)),
                     pl.BlockSpec((tm, d), lambda i: (i, 0))]),
        out_specs=pl.BlockSpec((tm, d), lambda i: (i, 0)),
        out_shape=jax.ShapeDtypeStruct((m, d), F32),
        compiler_params=_params("arbitrary"),
        name="out_proj",
    )(*acts, *weights, g.reshape(1, d), x)


def _split_rows(w, widths):
    out, start = [], 0
    for n in widths:
        out.append(w[start:start + n])
        start += n
    return out


def _pad_rows(a, nseq, seq):
    return jnp.pad(a.reshape(nseq, seq, a.shape[-1]), ((0, 0), (0, SAMPLE_PAD - seq), (0, 0)))


def kernel(x_prompt, x_sample, state_ret, cache_swa_k, cache_swa_v, cache_mem_k, cache_mem_v, mem_prompt,
           pre_norm, post_norm, mem_norm, w_mem_k, w_mem_v, w_in_even, ret_gn, w_out_even,
           w_in_odd, gmlp_ln, w_spatial, b_spatial, w_out_odd):
    bp, tp, d = x_prompt.shape
    nseq, seq, _ = x_sample.shape
    assert bp == 1 and seq <= SAMPLE_PAD and (nseq * seq) % GMLP_CHUNK == 0
    rows_s = nseq * seq
    hp = x_prompt.reshape(tp, d)
    hs = x_sample.reshape(rows_s, d)
    mem = mem_prompt.reshape(MEM_LEN, d)
    tm_p = 1024 if tp % 1024 == 0 else 512

    pos_p = jnp.arange(tp, dtype=jnp.int32)
    pos_s = PAST_LEN + jnp.arange(seq, dtype=jnp.int32)
    ret_cos_p, ret_sin_p = _rot_tables(pos_p, RET_DK, RET_THETA, RET_DK)
    ret_cos_s, ret_sin_s = _rot_tables(jnp.tile(pos_s, nseq), RET_DK, RET_THETA, RET_DK)
    swa_cos_p, swa_sin_p = _rot_tables(pos_p, ROPE_DIMS, ROPE_THETA, SWA_HD)
    pos_pad = jnp.concatenate([pos_s, jnp.zeros((SAMPLE_PAD - seq,), jnp.int32)])
    swa_cos_s, swa_sin_s = _rot_tables(pos_pad, ROPE_DIMS, ROPE_THETA, SWA_HD)
    swa_cos_s = jnp.tile(swa_cos_s, (1, SWA_HEADS))
    swa_sin_s = jnp.tile(swa_sin_s, (1, SWA_HEADS))

    w_mem0 = jnp.concatenate([w_mem_k[0], w_mem_v[0]], axis=1).astype(BF16)
    memkv0 = _norm_matmul(mem, mem_norm[0], w_mem0, MEM_LEN, 2 * MEM_W)
    w_in0 = w_in_even[0].astype(BF16)
    wo_a, wo_b, wo_m = _split_rows(w_out_even[0].astype(BF16),
                                   (RET_HEADS * RET_DV, SWA_HEADS * SWA_HD, MEM_W))

    proj_p = _norm_matmul(hp, pre_norm[0], w_in0, tm_p, 1024)
    ya_p, st_p = _ret_prompt(proj_p, ret_cos_p, ret_sin_p, ret_gn[0])
    yb_p, pk = _swa_prompt(proj_p, swa_cos_p, swa_sin_p)
    ym_p = _mem_attn(proj_p[None], E_MQ, E_MG, memkv0[None], 0, memkv0[None], MEM_W, 1024, BF16)[0]
    hp1 = _out_proj([ya_p, yb_p, ym_p], [wo_a, wo_b, wo_m], post_norm[0], hp, 256)
    buf_p = min(SWA_MAX_WINDOW, tp)
    pv = proj_p[tp - buf_p:, E_BV:E_BV + SWA_HEADS * SWA_HD]

    proj_s = _norm_matmul(hs, pre_norm[0], w_in0, rows_s, 1024)
    proj_s8 = _pad_rows(proj_s, nseq, seq)
    ya_s, st_s = _ret_sample(proj_s, state_ret[0], ret_cos_s, ret_sin_s, ret_gn[0], nseq, seq)
    yb_s8, sk8 = _swa_sample(proj_s8, cache_swa_k[0], cache_swa_v[0], swa_cos_s, swa_sin_s, seq)
    ck0 = cache_mem_k[0].reshape(nseq, MEM_LEN, MEM_W)
    cv0 = cache_mem_v[0].reshape(nseq, MEM_LEN, MEM_W)
    ym_s8 = _mem_attn(proj_s8, E_MQ, E_MG, ck0, 0, cv0, 0, SAMPLE_PAD, F32)
    yb_s = yb_s8[:, :seq].reshape(rows_s, -1)
    ym_s = ym_s8[:, :seq].reshape(rows_s, -1)
    hs1 = _out_proj([ya_s, yb_s, ym_s], [wo_a, wo_b, wo_m], post_norm[0], hs, rows_s)
    sk = sk8[:, :seq].reshape(nseq, seq, SWA_HEADS, SWA_HD)
    sv = proj_s[:, E_BV:E_BV + SWA_HEADS * SWA_HD].reshape(nseq, seq, SWA_HEADS, SWA_HD)

    w_mem1 = jnp.concatenate([w_mem_k[1], w_mem_v[1]], axis=1).astype(BF16)
    memkv1 = _norm_matmul(mem, mem_norm[1], w_mem1, MEM_LEN, 2 * MEM_W)
    w_in1 = w_in_odd[0].astype(BF16)
    wo_c, wo_m1 = _split_rows(w_out_odd[0].astype(BF16), (GMLP_WIDTH, MEM_W))
    c = GMLP_CHUNK
    bias_p = jnp.repeat(b_spatial[0].T, GMLP_GROUP, axis=1)
    w_mix_s = jax.vmap(lambda wg: jnp.kron(jnp.eye(c // seq, dtype=F32), wg[:seq, :seq]))(w_spatial[0])
    bias_s = jnp.tile(bias_p[:seq], (c // seq, 1))

    proj_p1 = _norm_matmul(hp1, pre_norm[1], w_in1, tm_p, 1024)
    yc_p, gv_p = _gmlp(proj_p1, gmlp_ln[0], w_spatial[0], bias_p)
    ym_p1 = _mem_attn(proj_p1[None], O_MQ, O_MG, memkv1[None], 0, memkv1[None], MEM_W, 1024, BF16)[0]
    hp2 = _out_proj([yc_p, ym_p1], [wo_c, wo_m1], post_norm[1], hp1, 256)

    proj_s1 = _norm_matmul(hs1, pre_norm[1], w_in1, rows_s, 1024)
    yc_s, gv_s = _gmlp(proj_s1, gmlp_ln[0], w_mix_s, bias_s)
    ck1 = cache_mem_k[1].reshape(nseq, MEM_LEN, MEM_W)
    cv1 = cache_mem_v[1].reshape(nseq, MEM_LEN, MEM_W)
    ym_s1 = _mem_attn(_pad_rows(proj_s1, nseq, seq), O_MQ, O_MG, ck1, 0, cv1, 0, SAMPLE_PAD, F32)
    ym_s1 = ym_s1[:, :seq].reshape(rows_s, -1)
    hs2 = _out_proj([yc_s, ym_s1], [wo_c, wo_m1], post_norm[1], hs1, rows_s)

    memkv = jnp.stack([memkv0, memkv1])
    p_mk = memkv[:, :, :MEM_W].reshape(2, bp, MEM_LEN, MEM_HEADS, MEM_HD)
    p_mv = memkv[:, :, MEM_W:].reshape(2, bp, MEM_LEN, MEM_HEADS, MEM_HD)
    return (hp2.reshape(bp, tp, d), hs2.reshape(nseq, seq, d),
            st_p[None, None],
            pk.reshape(1, bp, buf_p, SWA_HEADS, SWA_HD), pv.reshape(1, bp, buf_p, SWA_HEADS, SWA_HD),
            p_mk, p_mv,
            gv_p[None, None],
            st_s[None],
            sk[None], sv[None],
            gv_s.reshape(1, nseq, seq, d))
```

```python
import functools
import math

import jax
import jax.numpy as jnp
from jax import lax
from jax.experimental import pallas as pl
from jax.experimental.pallas import tpu as pltpu

F32 = jnp.float32
BF16 = jnp.bfloat16

D_MODEL = 2048
PAST_LEN = 8192
RET_HEADS = 8
RET_DK = 128
RET_DV = 256
RET_CHUNK = 128
RET_THETA = 10000.0
SWA_HEADS = 8
SWA_HD = 128
SWA_PATTERNS = ((128, 1), (512, 4), (2048, 16))
SWA_MAX_WINDOW = 2048
SWA_BLOCK = 128
ROPE_THETA = 500000.0
ROPE_DIMS = SWA_HD // 4
GMLP_CHUNK = 128
GMLP_WIDTH = D_MODEL
GMLP_GROUP = 128
GMLP_GROUPS = GMLP_WIDTH // GMLP_GROUP
MEM_LEN = 256
MEM_HEADS = 4
MEM_HD = 128
MEM_W = MEM_HEADS * MEM_HD
NORM_EPS = 1e-6
NEG_INF = -1e30

E_AQ = 0
E_AK = E_AQ + RET_HEADS * RET_DK
E_AV = E_AK + RET_HEADS * RET_DK
E_AG = E_AV + RET_HEADS * RET_DV
E_BQ = E_AG + RET_HEADS * RET_DV
E_BK = E_BQ + SWA_HEADS * SWA_HD
E_BV = E_BK + SWA_HEADS * SWA_HD
E_BG = E_BV + SWA_HEADS * SWA_HD
E_MQ = E_BG + SWA_HEADS * SWA_HD
E_MG = E_MQ + MEM_W
EVEN_IN = E_MG + MEM_W
O_U = 0
O_V = O_U + GMLP_WIDTH
O_Z = O_V + GMLP_WIDTH
O_MQ = O_Z + GMLP_WIDTH
O_MG = O_MQ + MEM_W
ODD_IN = O_MG + MEM_W

SWA_TILE = SWA_MAX_WINDOW
SAMPLE_PAD = 8
VMEM_LIMIT = 48 * 1024 * 1024


def _params(*sem):
    return pltpu.CompilerParams(dimension_semantics=sem, vmem_limit_bytes=VMEM_LIMIT)


def _dot(a, b):
    return jnp.dot(a, b, preferred_element_type=F32)


def _dot_nt(a, b):
    return lax.dot_general(a, b, (((1,), (1,)), ((), ())), preferred_element_type=F32)


def _silu(x):
    return x * jax.nn.sigmoid(x)


def _div(x, n):
    assert n & (n - 1) == 0
    return jnp.right_shift(x, n.bit_length() - 1)


def _mod(x, n):
    assert n & (n - 1) == 0
    return jnp.bitwise_and(x, n - 1)


def _rot_tables(pos, n_rot, theta, width):
    half = n_rot // 2
    inv = 1.0 / (theta ** (jnp.arange(half, dtype=F32) / half))
    ang = pos.astype(F32)[:, None] * inv[None, :]
    cos, sin = jnp.cos(ang), jnp.sin(ang)
    t = pos.shape[0]
    cos_t = jnp.concatenate([cos, cos, jnp.ones((t, width - n_rot), F32)], axis=-1)
    sin_t = jnp.concatenate([-sin, sin, jnp.zeros((t, width - n_rot), F32)], axis=-1)
    return cos_t, sin_t


def _rot_full(x, cos, sin):
    return x * cos + pltpu.roll(x, RET_DK // 2, 1) * sin


def _rot_partial(x, cos, sin):
    half = ROPE_DIMS // 2
    lane = _mod(lax.broadcasted_iota(jnp.int32, x.shape, 1), SWA_HD)
    width = x.shape[1]
    partner = jnp.where(lane < half, pltpu.roll(x, width - half, 1), pltpu.roll(x, half, 1))
    return x * cos + partner * sin


def _norm_matmul_kernel(x_ref, g_ref, w_ref, o_ref, xn_ref):
    @pl.when(pl.program_id(1) == 0)
    def _():
        x = x_ref[...]
        ms = jnp.mean(x * x, axis=-1, keepdims=True)
        xn_ref[...] = (x * lax.rsqrt(ms + NORM_EPS) * g_ref[...]).astype(BF16)

    o_ref[...] = _dot(xn_ref[...], w_ref[...]).astype(o_ref.dtype)


def _norm_matmul(x, g, w, tm, tn):
    m, d = x.shape
    n = w.shape[1]
    return pl.pallas_call(
        _norm_matmul_kernel,
        grid=(m // tm, n // tn),
        in_specs=[pl.BlockSpec((tm, d), lambda i, j: (i, 0)),
                  pl.BlockSpec((1, d), lambda i, j: (0, 0)),
                  pl.BlockSpec((d, tn), lambda i, j: (0, j))],
        out_specs=pl.BlockSpec((tm, tn), lambda i, j: (i, j)),
        out_shape=jax.ShapeDtypeStruct((m, n), F32),
        scratch_shapes=[pltpu.VMEM((tm, d), BF16)],
        compiler_params=_params("arbitrary", "arbitrary"),
        name="norm_matmul",
    )(x, g.reshape(1, d), w)


def _ret_tables(c, reps):
    lg = jnp.log1p(-jnp.exp2(-5.0 - jnp.arange(RET_HEADS, dtype=F32)))
    idx = jnp.arange(c, dtype=F32)
    rel = idx[:, None] - idx[None, :]
    decay = jnp.where(rel >= 0, jnp.exp(lg[:, None, None] * jnp.maximum(rel, 0.0)), 0.0)
    q_decay = jnp.exp(lg[:, None] * (idx[None, :] + 1.0))
    k_decay = jnp.exp(lg[:, None] * (c - 1.0 - idx[None, :]))
    chunk_decay = jnp.exp(lg * c)
    if reps > 1:
        eye = jnp.eye(reps, dtype=F32)
        decay = jax.vmap(lambda dm: jnp.kron(eye, dm))(decay)
        q_decay = jnp.tile(q_decay, (1, reps))
        k_decay = jnp.tile(k_decay, (1, reps))
    rows = c * reps
    qd = jnp.broadcast_to(q_decay[:, :, None], (RET_HEADS, rows, RET_DK))
    kd = jnp.broadcast_to(k_decay[:, :, None], (RET_HEADS, rows, RET_DK))
    cd = jnp.broadcast_to(chunk_decay[:, None, None], (RET_HEADS, 1, RET_DV))
    return decay, qd, kd, cd


def _group_norm_gate(o, gn, g):
    mu = jnp.mean(o, axis=-1, keepdims=True)
    d = o - mu
    var = jnp.mean(d * d, axis=-1, keepdims=True)
    return d * lax.rsqrt(var + NORM_EPS) * gn * _silu(g)


def _ret_prompt_kernel(q_ref, k_ref, v_ref, g_ref, cos_ref, sin_ref, dec_ref, qd_ref, kd_ref, cd_ref, gn_ref,
                       y_ref, st_ref):
    @pl.when(pl.program_id(0) == 0)
    def _():
        st_ref[...] = jnp.zeros_like(st_ref)

    cos = cos_ref[...]
    sin = sin_ref[...]
    for h in range(RET_HEADS):
        ks = slice(h * RET_DK, (h + 1) * RET_DK)
        vs = slice(h * RET_DV, (h + 1) * RET_DV)
        q = _rot_full(q_ref[:, ks], cos, sin)
        k = _rot_full(k_ref[:, ks] * (RET_DK ** -0.5), cos, sin)
        v = v_ref[:, vs].astype(BF16)
        st = st_ref[h]
        s = _dot_nt(q.astype(BF16), k.astype(BF16)) * dec_ref[h]
        o = _dot(s.astype(BF16), v) + _dot((q * qd_ref[h]).astype(BF16), st.astype(BF16))
        st_ref[h] = cd_ref[h] * st + _dot((k * kd_ref[h]).T.astype(BF16), v)
        y_ref[:, vs] = _group_norm_gate(o, gn_ref[:, vs], g_ref[:, vs]).astype(y_ref.dtype)


def _ret_prompt(proj, cos, sin, ret_gn):
    t = proj.shape[0]
    c = RET_CHUNK
    dec, qd, kd, cd = _ret_tables(c, 1)
    qw = RET_HEADS * RET_DK
    vw = RET_HEADS * RET_DV
    full3 = lambda n: (0, 0, 0)
    return pl.pallas_call(
        _ret_prompt_kernel,
        grid=(t // c,),
        in_specs=[pl.BlockSpec((c, qw), lambda n: (n, E_AQ // qw)),
                  pl.BlockSpec((c, qw), lambda n: (n, E_AK // qw)),
                  pl.BlockSpec((c, vw), lambda n: (n, E_AV // vw)),
                  pl.BlockSpec((c, vw), lambda n: (n, E_AG // vw)),
                  pl.BlockSpec((c, RET_DK), lambda n: (n, 0)),
                  pl.BlockSpec((c, RET_DK), lambda n: (n, 0)),
                  pl.BlockSpec(dec.shape, full3),
                  pl.BlockSpec(qd.shape, full3),
                  pl.BlockSpec(kd.shape, full3),
                  pl.BlockSpec(cd.shape, full3),
                  pl.BlockSpec((1, vw), lambda n: (0, 0))],
        out_specs=[pl.BlockSpec((c, vw), lambda n: (n, 0)),
                   pl.BlockSpec((RET_HEADS, RET_DK, RET_DV), full3)],
        out_shape=[jax.ShapeDtypeStruct((t, vw), BF16),
                   jax.ShapeDtypeStruct((RET_HEADS, RET_DK, RET_DV), F32)],
        compiler_params=_params("arbitrary"),
        name="ret_prompt",
    )(proj, proj, proj, proj, cos, sin, dec, qd, kd, cd, ret_gn.reshape(1, vw))


def _ret_sample_kernel(q_ref, k_ref, v_ref, g_ref, cos_ref, sin_ref, dec_ref, qd_ref, kd_ref, cd_ref, gn_ref,
                       st_ref, y_ref, sto_ref, cross_ref, *, seq):
    b = pl.program_id(0)
    nb = pl.num_programs(0)

    @pl.when(b == 0)
    def _():
        cross_ref[...] = jnp.zeros_like(cross_ref)

    cos = cos_ref[...]
    sin = sin_ref[...]
    rows = lax.broadcasted_iota(jnp.int32, (q_ref.shape[0], RET_DK), 0)
    mine = (rows >= b * seq) & (rows < (b + 1) * seq)
    for h in range(RET_HEADS):
        ks = slice(h * RET_DK, (h + 1) * RET_DK)
        vs = slice(h * RET_DV, (h + 1) * RET_DV)
        q = _rot_full(q_ref[:, ks], cos, sin)
        k = _rot_full(k_ref[:, ks] * (RET_DK ** -0.5), cos, sin)
        v = v_ref[:, vs].astype(BF16)
        st = st_ref[0, h]
        qm = jnp.where(mine, q * qd_ref[h], 0.0)
        km = jnp.where(mine, k * kd_ref[h], 0.0)
        cross_ref[:, vs] += _dot(qm.astype(BF16), st.astype(BF16))
        sto_ref[0, h] = cd_ref[h] * st + _dot(km.T.astype(BF16), v)

    @pl.when(b == nb - 1)
    def _():
        for h in range(RET_HEADS):
            ks = slice(h * RET_DK, (h + 1) * RET_DK)
            vs = slice(h * RET_DV, (h + 1) * RET_DV)
            q = _rot_full(q_ref[:, ks], cos, sin)
            k = _rot_full(k_ref[:, ks] * (RET_DK ** -0.5), cos, sin)
            v = v_ref[:, vs].astype(BF16)
            s = _dot_nt(q.astype(BF16), k.astype(BF16)) * dec_ref[h]
            o = _dot(s.astype(BF16), v) + cross_ref[:, vs]
            y_ref[:, vs] = _group_norm_gate(o, gn_ref[:, vs], g_ref[:, vs]).astype(y_ref.dtype)


def _ret_sample(proj, state, cos, sin, ret_gn, nseq, seq):
    rows = nseq * seq
    c = math.gcd(seq, RET_CHUNK)
    assert c == seq, "sample chunk must cover the new tokens"
    dec, qd, kd, cd = _ret_tables(c, nseq)
    qw = RET_HEADS * RET_DK
    vw = RET_HEADS * RET_DV
    full3 = lambda b: (0, 0, 0)
    st_spec = pl.BlockSpec((1, RET_HEADS, RET_DK, RET_DV), lambda b: (b, 0, 0, 0))
    return pl.pallas_call(
        functools.partial(_ret_sample_kernel, seq=seq),
        grid=(nseq,),
        in_specs=[pl.BlockSpec((rows, qw), lambda b: (0, E_AQ // qw)),
                  pl.BlockSpec((rows, qw), lambda b: (0, E_AK // qw)),
                  pl.BlockSpec((rows, vw), lambda b: (0, E_AV // vw)),
                  pl.BlockSpec((rows, vw), lambda b: (0, E_AG // vw)),
                  pl.BlockSpec((rows, RET_DK), lambda b: (0, 0)),
                  pl.BlockSpec((rows, RET_DK), lambda b: (0, 0)),
                  pl.BlockSpec(dec.shape, full3),
                  pl.BlockSpec(qd.shape, full3),
                  pl.BlockSpec(kd.shape, full3),
                  pl.BlockSpec(cd.shape, full3),
                  pl.BlockSpec((1, vw), lambda b: (0, 0)),
                  st_spec],
        out_specs=[pl.BlockSpec((rows, vw), lambda b: (0, 0)), st_spec],
        out_shape=[jax.ShapeDtypeStruct((rows, vw), F32),
                   jax.ShapeDtypeStruct(state.shape, F32)],
        scratch_shapes=[pltpu.VMEM((rows, vw), F32)],
        compiler_params=_params("arbitrary"),
        name="ret_sample",
    )(proj, proj, proj, proj, cos, sin, dec, qd, kd, cd, ret_gn.reshape(1, vw), state)


def _swa_prompt_kernel(q_ref, kc_ref, kp_ref, vc_ref, vp_ref, g_ref, cos_ref, sin_ref, cosp_ref, sinp_ref,
                       y_ref, ko_ref, q_s, k_s, v_s, m_s, l_s, acc_s):
    t = pl.program_id(1)
    nt = pl.num_programs(1)
    tb = SWA_TILE
    blk = SWA_BLOCK
    cos = cos_ref[...]
    sin = sin_ref[...]
    q_s[...] = _rot_partial(q_ref[...] * (SWA_HD ** -0.5), cos, sin)
    kr = _rot_partial(kc_ref[...], cos, sin)
    k_s[tb:2 * tb, :] = kr
    k_s[0:tb, :] = _rot_partial(kp_ref[...], cosp_ref[...], sinp_ref[...])
    v_s[tb:2 * tb, :] = vc_ref[...]
    v_s[0:tb, :] = vp_ref[...]

    @pl.when(t == nt - 1)
    def _():
        ko_ref[...] = kr

    qi = lax.broadcasted_iota(jnp.int32, (blk, 2 * blk), 0)
    kj = lax.broadcasted_iota(jnp.int32, (blk, 2 * blk), 1)
    band = (kj >= qi) & (kj <= qi + blk)
    band_first = band & ((kj >= blk) | (t > 0))

    def ds(start, size, stride):
        return pl.ds(start, size) if stride == 1 else pl.ds(start, size, stride=stride)

    for bi, (window, r) in enumerate(SWA_PATTERNS):
        assert window // r == blk
        for idx in range(tb // blk):
            c, n = idx % r, idx // r
            qs = n * blk * r + c
            ks = qs + tb - blk * r
            qrows = ds(qs, blk, r)
            krows = ds(ks, 2 * blk, r)
            valid = band_first if n == 0 else band
            s = _dot_nt(q_s[qrows, :].astype(BF16), k_s[krows, :].astype(BF16))
            s = jnp.where(valid, s, NEG_INF)
            m = jnp.max(s, axis=-1, keepdims=True)
            p = jnp.where(valid, jnp.exp(s - m), 0.0)
            m_s[bi, qrows, :] = m
            l_s[bi, qrows, :] = jnp.sum(p, axis=-1, keepdims=True)
            acc_s[bi, qrows, :] = _dot(p.astype(BF16), v_s[krows, :].astype(BF16))

    nbr = len(SWA_PATTERNS)
    ms = [m_s[i] for i in range(nbr)]
    mmax = functools.reduce(jnp.maximum, ms)
    wts = [jnp.exp(mi - mmax) for mi in ms]
    num = sum(wts[i] * acc_s[i] for i in range(nbr))
    den = sum(wts[i] * l_s[i] for i in range(nbr))
    y_ref[...] = (num / den * _silu(g_ref[...])).astype(y_ref.dtype)


def _swa_prompt(proj, cos, sin):
    t = proj.shape[0]
    tb = SWA_TILE
    assert t % tb == 0
    hd = SWA_HD
    cur = lambda off: pl.BlockSpec((tb, hd), lambda h, i: (i, off // hd + h))
    prev = lambda off: pl.BlockSpec((tb, hd), lambda h, i: (jnp.maximum(i - 1, 0), off // hd + h))
    return pl.pallas_call(
        _swa_prompt_kernel,
        grid=(SWA_HEADS, t // tb),
        in_specs=[cur(E_BQ), cur(E_BK), prev(E_BK), cur(E_BV), prev(E_BV), cur(E_BG),
                  pl.BlockSpec((tb, hd), lambda h, i: (i, 0)),
                  pl.BlockSpec((tb, hd), lambda h, i: (i, 0)),
                  pl.BlockSpec((tb, hd), lambda h, i: (jnp.maximum(i - 1, 0), 0)),
                  pl.BlockSpec((tb, hd), lambda h, i: (jnp.maximum(i - 1, 0), 0))],
        out_specs=[pl.BlockSpec((tb, hd), lambda h, i: (i, h)),
                   pl.BlockSpec((tb, hd), lambda h, i: (0, h))],
        out_shape=[jax.ShapeDtypeStruct((t, SWA_HEADS * hd), BF16),
                   jax.ShapeDtypeStruct((tb, SWA_HEADS * hd), F32)],
        scratch_shapes=[pltpu.VMEM((tb, hd), F32),
                        pltpu.VMEM((2 * tb, hd), F32),
                        pltpu.VMEM((2 * tb, hd), F32),
                        pltpu.VMEM((len(SWA_PATTERNS), tb, 1), F32),
                        pltpu.VMEM((len(SWA_PATTERNS), tb, 1), F32),
                        pltpu.VMEM((len(SWA_PATTERNS), tb, hd), F32)],
        compiler_params=_params("arbitrary", "arbitrary"),
        name="swa_prompt",
    )(proj, proj, proj, proj, proj, proj, cos, sin, cos, sin)


def _swa_sample_kernel(q_ref, k_ref, v_ref, g_ref, cos_ref, sin_ref, ka_ref, kb_ref, va_ref, vb_ref,
                       y_ref, ko_ref, kn_s, vn_s, *, seq):
    pad = SAMPLE_PAD
    nh = SWA_HEADS
    hd = SWA_HD
    nr = nh * pad
    cos = cos_ref[...]
    sin = sin_ref[...]
    q8 = _rot_partial(q_ref[0] * (SWA_HD ** -0.5), cos, sin)
    k8 = _rot_partial(k_ref[0], cos, sin)
    ko_ref[0] = k8

    def head_rows(x):
        return jnp.concatenate([x[:, h * hd:(h + 1) * hd] for h in range(nh)], axis=0)

    qrows = head_rows(q8).astype(BF16)
    kn_s[...] = jnp.zeros_like(kn_s)
    vn_s[...] = jnp.zeros_like(vn_s)
    kn_s[0:nr, :] = head_rows(k8).astype(BF16)
    vn_s[0:nr, :] = head_rows(v_ref[0]).astype(BF16)
    nn = kn_s.shape[0]

    ka = ka_ref[0]
    va = va_ref[0]
    nj, nrh, _ = kb_ref.shape[1:]
    kb = kb_ref[0].reshape(nj * nrh, hd)
    vb = vb_ref[0].reshape(nj * nrh, hd)
    na = ka.shape[0]
    s_a = _dot_nt(qrows, ka.astype(BF16))
    s_b = _dot_nt(qrows, kb.astype(BF16))
    s_n = _dot_nt(qrows, kn_s[...])

    (w1, d1), (w2, d2), (w3, d3) = SWA_PATTERNS

    def rows_cols(n):
        r = lax.broadcasted_iota(jnp.int32, (nr, n), 0)
        c = lax.broadcasted_iota(jnp.int32, (nr, n), 1)
        return _div(r, pad), _mod(r, pad), c

    hq, tq, c = rows_cols(na)
    dist = na // nh + tq - _div(c, nh)
    mult_a = (dist <= w1).astype(F32) + ((_mod(dist, d2) == 0) & (dist <= w2)).astype(F32)
    mult_a = jnp.where((hq == _mod(c, nh)) & (tq < seq), mult_a, 0.0)
    hq, tq, c = rows_cols(nj * nrh)
    mult_b = jnp.where((hq == _mod(c, nh)) & (tq == _mod(_div(c, nh), nrh // nh)), 1.0, 0.0)
    hq, tq, c = rows_cols(nn)
    tk = _mod(c, pad)
    mult_n = (tk <= tq).astype(F32) + 2.0 * (tk == tq).astype(F32)
    mult_n = jnp.where((hq == _div(c, pad)) & (tq < seq) & (tk < seq), mult_n, 0.0)

    def rowmax(s, mult):
        return jnp.max(jnp.where(mult > 0, s, NEG_INF), axis=-1, keepdims=True)

    m = jnp.maximum(jnp.maximum(rowmax(s_a, mult_a), rowmax(s_b, mult_b)), rowmax(s_n, mult_n))

    def probs(s, mult):
        return jnp.where(mult > 0, jnp.exp(s - m), 0.0) * mult

    p_a, p_b, p_n = probs(s_a, mult_a), probs(s_b, mult_b), probs(s_n, mult_n)
    l = (jnp.sum(p_a, axis=-1, keepdims=True) + jnp.sum(p_b, axis=-1, keepdims=True)
         + jnp.sum(p_n, axis=-1, keepdims=True))
    l = jnp.where(l > 0, l, 1.0)
    o = (_dot(p_a.astype(BF16), va.astype(BF16)) + _dot(p_b.astype(BF16), vb.astype(BF16))
         + _dot(p_n.astype(BF16), vn_s[...])) / l
    y = jnp.concatenate([o[h * pad:(h + 1) * pad] for h in range(nh)], axis=1)
    y_ref[0] = y * _silu(g_ref[0])


def _swa_sample(proj_pad, cache_k, cache_v, cos, sin, seq):
    nseq, pad, _ = proj_pad.shape
    buf = cache_k.shape[1]
    nh, hd = SWA_HEADS, SWA_HD
    w = nh * hd
    (w1, d1), (w2, d2), (w3, d3) = SWA_PATTERNS
    assert buf == w3 and d1 == 1 and seq <= d2 and w1 <= w2 and buf % w2 == 0 and buf % d3 == 0
    assert nh * pad <= SWA_BLOCK and (seq * nh) % 8 == 0
    flat = lambda a: a.reshape(nseq, buf * nh, hd)
    strided = lambda a: a.reshape(nseq, buf // d3, d3 * nh, hd)
    col = lambda off: pl.BlockSpec((1, pad, w), lambda b: (b, 0, off // w))
    tail = pl.BlockSpec((1, w2 * nh, hd), lambda b: (b, buf // w2 - 1, 0))
    resid = pl.BlockSpec((1, buf // d3, seq * nh, hd), lambda b: (b, 0, 0, 0))
    tab = pl.BlockSpec((pad, w), lambda b: (0, 0))
    out = pl.BlockSpec((1, pad, w), lambda b: (b, 0, 0))
    return pl.pallas_call(
        functools.partial(_swa_sample_kernel, seq=seq),
        grid=(nseq,),
        in_specs=[col(E_BQ), col(E_BK), col(E_BV), col(E_BG), tab, tab, tail, resid, tail, resid],
        out_specs=[out, out],
        out_shape=[jax.ShapeDtypeStruct((nseq, pad, w), F32),
                   jax.ShapeDtypeStruct((nseq, pad, w), F32)],
        scratch_shapes=[pltpu.VMEM((SWA_BLOCK, hd), BF16), pltpu.VMEM((SWA_BLOCK, hd), BF16)],
        compiler_params=_params("arbitrary"),
        name="swa_sample",
    )(proj_pad, proj_pad, proj_pad, proj_pad, cos, sin,
      flat(cache_k), strided(cache_k), flat(cache_v), strided(cache_v))


def _mem_attn_kernel(q_ref, g_ref, k_ref, v_ref, o_ref):
    for h in range(MEM_HEADS):
        hs = slice(h * MEM_HD, (h + 1) * MEM_HD)
        q = (q_ref[0, :, hs] * (MEM_HD ** -0.5)).astype(BF16)
        s = _dot_nt(q, k_ref[0, :, hs].astype(BF16))
        m = jnp.max(s, axis=-1, keepdims=True)
        p = jnp.exp(s - m)
        l = jnp.sum(p, axis=-1, keepdims=True)
        o = _dot(p.astype(BF16), v_ref[0, :, hs].astype(BF16)) / l
        o_ref[0, :, hs] = (o * _silu(g_ref[0, :, hs])).astype(o_ref.dtype)


def _mem_attn(proj, q_off, g_off, mem_k, k_off, mem_v, v_off, tq, out_dtype):
    b, t, _ = proj.shape
    w = MEM_W
    return pl.pallas_call(
        _mem_attn_kernel,
        grid=(b, t // tq),
        in_specs=[pl.BlockSpec((1, tq, w), lambda i, j: (i, j, q_off // w)),
                  pl.BlockSpec((1, tq, w), lambda i, j: (i, j, g_off // w)),
                  pl.BlockSpec((1, MEM_LEN, w), lambda i, j: (i, 0, k_off // w)),
                  pl.BlockSpec((1, MEM_LEN, w), lambda i, j: (i, 0, v_off // w))],
        out_specs=pl.BlockSpec((1, tq, w), lambda i, j: (i, j, 0)),
        out_shape=jax.ShapeDtypeStruct((b, t, w), out_dtype),
        compiler_params=_params("arbitrary", "arbitrary"),
        name="mem_attn",
    )(proj, proj, mem_k, mem_v)


def _gmlp_kernel(u_ref, v_ref, z_ref, ln_ref, w_ref, b_ref, y_ref, vo_ref):
    c = GMLP_CHUNK
    v = jax.nn.gelu(v_ref[...])
    mu = jnp.mean(v, axis=-1, keepdims=True)
    d = v - mu
    var = jnp.mean(d * d, axis=-1, keepdims=True)
    vn = d * lax.rsqrt(var + NORM_EPS) * ln_ref[...]
    vo_ref[...] = vn
    causal = (lax.broadcasted_iota(jnp.int32, (c, c), 0) >= lax.broadcasted_iota(jnp.int32, (c, c), 1))
    for g in range(GMLP_GROUPS):
        gs = slice(g * GMLP_GROUP, (g + 1) * GMLP_GROUP)
        wc = jnp.where(causal, w_ref[g], 0.0).astype(BF16)
        mixed = _dot(wc, vn[:, gs].astype(BF16)) + b_ref[:, gs]
        y_ref[:, gs] = (jax.nn.gelu(u_ref[:, gs]) * mixed * _silu(z_ref[:, gs])).astype(y_ref.dtype)


def _gmlp(proj, gmlp_ln, w_mix, bias_rows):
    t = proj.shape[0]
    c = GMLP_CHUNK
    w = GMLP_WIDTH
    col = lambda off: pl.BlockSpec((c, w), lambda n: (n, off // w))
    return pl.pallas_call(
        _gmlp_kernel,
        grid=(t // c,),
        in_specs=[col(O_U), col(O_V), col(O_Z),
                  pl.BlockSpec((1, w), lambda n: (0, 0)),
                  pl.BlockSpec(w_mix.shape, lambda n: (0, 0, 0)),
                  pl.BlockSpec((c, w), lambda n: (0, 0))],
        out_specs=[pl.BlockSpec((c, w), lambda n: (n, 0)),
                   pl.BlockSpec((c, w), lambda n: (0, 0))],
        out_shape=[jax.ShapeDtypeStruct((t, w), BF16),
                   jax.ShapeDtypeStruct((c, w), F32)],
        compiler_params=_params("arbitrary"),
        name="gmlp",
    )(proj, proj, proj, gmlp_ln.reshape(1, w), w_mix, bias_rows)


def _out_proj_kernel(*refs, n_in):
    a_refs, w_refs = refs[:n_in], refs[n_in:2 * n_in]
    g_ref, x_ref, o_ref = refs[2 * n_in:]
    acc = _dot(a_refs[0][...].astype(BF16), w_refs[0][...])
    for a_ref, w_ref in zip(a_refs[1:], w_refs[1:]):
        acc = acc + _dot(a_ref[...].astype(BF16), w_ref[...])
    ms = jnp.mean(acc * acc, axis=-1, keepdims=True)
    o_ref[...] = x_ref[...] + acc * lax.rsqrt(ms + NORM_EPS) * g_ref[...]


def _out_proj(acts, weights, g, x, tm):
    m, d = x.shape
    n_in = len(acts)
    return pl.pallas_call(
        functools.partial(_out_proj_kernel, n_in=n_in),
        grid=(m // tm,),
        in_specs=([pl.BlockSpec((tm, a.shape[1]), lambda i: (i, 0)) for a in acts]
                  + [pl.BlockSpec(w.shape, lambda i: (0, 0)) for w in weights]
                  + [pl.BlockSpec((1, d), lambda i: (0, 0)),
                     pl.BlockSpec((tm, d), lambda i: (i, 0))]),
        out_specs=pl.BlockSpec((tm, d), lambda i: (i, 0)),
        out_shape=jax.ShapeDtypeStruct((m, d), F32),
        compiler_params=_params("arbitrary"),
        name="out_proj",
    )(*acts, *weights, g.reshape(1, d), x)


def _split_rows(w, widths):
    out, start = [], 0
    for n in widths:
        out.append(w[start:start + n])
        start += n
    return out


def _pad_rows(a, nseq, seq):
    return jnp.pad(a.reshape(nseq, seq, a.shape[-1]), ((0, 0), (0, SAMPLE_PAD - seq), (0, 0)))


def kernel(x_prompt, x_sample, state_ret, cache_swa_k, cache_swa_v, cache_mem_k, cache_mem_v, mem_prompt,
           pre_norm, post_norm, mem_norm, w_mem_k, w_mem_v, w_in_even, ret_gn, w_out_even,
           w_in_odd, gmlp_ln, w_spatial, b_spatial, w_out_odd):
    bp, tp, d = x_prompt.shape
    nseq, seq, _ = x_sample.shape
    assert bp == 1 and seq <= SAMPLE_PAD and (nseq * seq) % GMLP_CHUNK == 0
    rows_s = nseq * seq
    hp = x_prompt.reshape(tp, d)
    hs = x_sample.reshape(rows_s, d)
    mem = mem_prompt.reshape(MEM_LEN, d)
    tm_p = 1024 if tp % 1024 == 0 else 512

    pos_p = jnp.arange(tp, dtype=jnp.int32)
    pos_s = PAST_LEN + jnp.arange(seq, dtype=jnp.int32)
    ret_cos_p, ret_sin_p = _rot_tables(pos_p, RET_DK, RET_THETA, RET_DK)
    ret_cos_s, ret_sin_s = _rot_tables(jnp.tile(pos_s, nseq), RET_DK, RET_THETA, RET_DK)
    swa_cos_p, swa_sin_p = _rot_tables(pos_p, ROPE_DIMS, ROPE_THETA, SWA_HD)
    pos_pad = jnp.concatenate([pos_s, jnp.zeros((SAMPLE_PAD - seq,), jnp.int32)])
    swa_cos_s, swa_sin_s = _rot_tables(pos_pad, ROPE_DIMS, ROPE_THETA, SWA_HD)
    swa_cos_s = jnp.tile(swa_cos_s, (1, SWA_HEADS))
    swa_sin_s = jnp.tile(swa_sin_s, (1, SWA_HEADS))

    w_mem0 = jnp.concatenate([w_mem_k[0], w_mem_v[0]], axis=1).astype(BF16)
    memkv0 = _norm_matmul(mem, mem_norm[0], w_mem0, MEM_LEN, 2 * MEM_W)
    w_in0 = w_in_even[0].astype(BF16)
    wo_a, wo_b, wo_m = _split_rows(w_out_even[0].astype(BF16),
                                   (RET_HEADS * RET_DV, SWA_HEADS * SWA_HD, MEM_W))

    proj_p = _norm_matmul(hp, pre_norm[0], w_in0, tm_p, 1024)
    ya_p, st_p = _ret_prompt(proj_p, ret_cos_p, ret_sin_p, ret_gn[0])
    yb_p, pk = _swa_prompt(proj_p, swa_cos_p, swa_sin_p)
    ym_p = _mem_attn(proj_p[None], E_MQ, E_MG, memkv0[None], 0, memkv0[None], MEM_W, 1024, BF16)[0]
    hp1 = _out_proj([ya_p, yb_p, ym_p], [wo_a, wo_b, wo_m], post_norm[0], hp, 256)
    buf_p = min(SWA_MAX_WINDOW, tp)
    pv = proj_p[tp - buf_p:, E_BV:E_BV + SWA_HEADS * SWA_HD]

    proj_s = _norm_matmul(hs, pre_norm[0], w_in0, rows_s, 1024)
    proj_s8 = _pad_rows(proj_s, nseq, seq)
    ya_s, st_s = _ret_sample(proj_s, state_ret[0], ret_cos_s, ret_sin_s, ret_gn[0], nseq, seq)
    yb_s8, sk8 = _swa_sample(proj_s8, cache_swa_k[0], cache_swa_v[0], swa_cos_s, swa_sin_s, seq)
    ck0 = cache_mem_k[0].reshape(nseq, MEM_LEN, MEM_W)
    cv0 = cache_mem_v[0].reshape(nseq, MEM_LEN, MEM_W)
    ym_s8 = _mem_attn(proj_s8, E_MQ, E_MG, ck0, 0, cv0, 0, SAMPLE_PAD, F32)
    yb_s = yb_s8[:, :seq].reshape(rows_s, -1)
    ym_s = ym_s8[:, :seq].reshape(rows_s, -1)
    hs1 = _out_proj([ya_s, yb_s, ym_s], [wo_a, wo_b, wo_m], post_norm[0], hs, rows_s)
    sk = sk8[:, :seq].reshape(nseq, seq, SWA_HEADS, SWA_HD)
    sv = proj_s[:, E_BV:E_BV + SWA_HEADS * SWA_HD].reshape(nseq, seq, SWA_HEADS, SWA_HD)

    w_mem1 = jnp.concatenate([w_mem_k[1], w_mem_v[1]], axis=1).astype(BF16)
    memkv1 = _norm_matmul(mem, mem_norm[1], w_mem1, MEM_LEN, 2 * MEM_W)
    w_in1 = w_in_odd[0].astype(BF16)
    wo_c, wo_m1 = _split_rows(w_out_odd[0].astype(BF16), (GMLP_WIDTH, MEM_W))
    c = GMLP_CHUNK
    bias_p = jnp.repeat(b_spatial[0].T, GMLP_GROUP, axis=1)
    w_mix_s = jax.vmap(lambda wg: jnp.kron(jnp.eye(c // seq, dtype=F32), wg[:seq, :seq]))(w_spatial[0])
    bias_s = jnp.tile(bias_p[:seq], (c // seq, 1))

    proj_p1 = _norm_matmul(hp1, pre_norm[1], w_in1, tm_p, 1024)
    yc_p, gv_p = _gmlp(proj_p1, gmlp_ln[0], w_spatial[0], bias_p)
    ym_p1 = _mem_attn(proj_p1[None], O_MQ, O_MG, memkv1[None], 0, memkv1[None], MEM_W, 1024, BF16)[0]
    hp2 = _out_proj([yc_p, ym_p1], [wo_c, wo_m1], post_norm[1], hp1, 256)

    proj_s1 = _norm_matmul(hs1, pre_norm[1], w_in1, rows_s, 1024)
    yc_s, gv_s = _gmlp(proj_s1, gmlp_ln[0], w_mix_s, bias_s)
    ck1 = cache_mem_k[1].reshape(nseq, MEM_LEN, MEM_W)
    cv1 = cache_mem_v[1].reshape(nseq, MEM_LEN, MEM_W)
    ym_s1 = _mem_attn(_pad_rows(proj_s1, nseq, seq), O_MQ, O_MG, ck1, 0, cv1, 0, SAMPLE_PAD, F32)
    ym_s1 = ym_s1[:, :seq].reshape(rows_s, -1)
    hs2 = _out_proj([yc_s, ym_s1], [wo_c, wo_m1], post_norm[1], hs1, rows_s)

    memkv = jnp.stack([memkv0, memkv1])
    p_mk = memkv[:, :, :MEM_W].reshape(2, bp, MEM_LEN, MEM_HEADS, MEM_HD)
    p_mv = memkv[:, :, MEM_W:].reshape(2, bp, MEM_LEN, MEM_HEADS, MEM_HD)
    return (hp2.reshape(bp, tp, d), hs2.reshape(nseq, seq, d),
            st_p[None, None],
            pk.reshape(1, bp, buf_p, SWA_HEADS, SWA_HD), pv.reshape(1, bp, buf_p, SWA_HEADS, SWA_HD),
            p_mk, p_mv,
            gv_p[None, None],
            st_s[None],
            sk[None], sv[None],
            gv_s.reshape(1, nseq, seq, d))
```

```python
import functools
import math

import jax
import jax.numpy as jnp
from jax import lax
from jax.experimental import pallas as pl
from jax.experimental.pallas import tpu as pltpu

F32 = jnp.float32
BF16 = jnp.bfloat16

D_MODEL = 2048
PAST_LEN = 8192
RET_HEADS = 8
RET_DK = 128
RET_DV = 256
RET_CHUNK = 128
RET_THETA = 10000.0
SWA_HEADS = 8
SWA_HD = 128
SWA_PATTERNS = ((128, 1), (512, 4), (2048, 16))
SWA_MAX_WINDOW = 2048
SWA_BLOCK = 128
ROPE_THETA = 500000.0
ROPE_DIMS = SWA_HD // 4
GMLP_CHUNK = 128
GMLP_WIDTH = D_MODEL
GMLP_GROUP = 128
GMLP_GROUPS = GMLP_WIDTH // GMLP_GROUP
MEM_LEN = 256
MEM_HEADS = 4
MEM_HD = 128
MEM_W = MEM_HEADS * MEM_HD
NORM_EPS = 1e-6
NEG_INF = -1e30
LOG2E = math.log2(math.e)

E_AQ = 0
E_AK = E_AQ + RET_HEADS * RET_DK
E_AV = E_AK + RET_HEADS * RET_DK
E_AG = E_AV + RET_HEADS * RET_DV
E_BQ = E_AG + RET_HEADS * RET_DV
E_BK = E_BQ + SWA_HEADS * SWA_HD
E_BV = E_BK + SWA_HEADS * SWA_HD
E_BG = E_BV + SWA_HEADS * SWA_HD
E_MQ = E_BG + SWA_HEADS * SWA_HD
E_MG = E_MQ + MEM_W
EVEN_IN = E_MG + MEM_W
O_U = 0
O_V = O_U + GMLP_WIDTH
O_Z = O_V + GMLP_WIDTH
O_MQ = O_Z + GMLP_WIDTH
O_MG = O_MQ + MEM_W
ODD_IN = O_MG + MEM_W

SWA_TILE = SWA_MAX_WINDOW
SAMPLE_PAD = 8
VMEM_LIMIT = 48 * 1024 * 1024


def _params(*sem):
    return pltpu.CompilerParams(dimension_semantics=sem, vmem_limit_bytes=VMEM_LIMIT)


def _dot(a, b):
    return jnp.dot(a, b, preferred_element_type=F32)


def _dot_nt(a, b):
    return lax.dot_general(a, b, (((1,), (1,)), ((), ())), preferred_element_type=F32)


def _silu(x):
    return x * jax.nn.sigmoid(x)


def _div(x, n):
    assert n & (n - 1) == 0
    return jnp.right_shift(x, n.bit_length() - 1)


def _mod(x, n):
    assert n & (n - 1) == 0
    return jnp.bitwise_and(x, n - 1)


def _rot_tables(pos, n_rot, theta, width):
    half = n_rot // 2
    inv = 1.0 / (theta ** (jnp.arange(half, dtype=F32) / half))
    ang = pos.astype(F32)[:, None] * inv[None, :]
    cos, sin = jnp.cos(ang), jnp.sin(ang)
    t = pos.shape[0]
    cos_t = jnp.concatenate([cos, cos, jnp.ones((t, width - n_rot), F32)], axis=-1)
    sin_t = jnp.concatenate([-sin, sin, jnp.zeros((t, width - n_rot), F32)], axis=-1)
    return cos_t, sin_t


def _rot_full(x, cos, sin):
    return x * cos + pltpu.roll(x, RET_DK // 2, 1) * sin


def _rot_partial(x, cos, sin):
    half = ROPE_DIMS // 2
    lane = _mod(lax.broadcasted_iota(jnp.int32, x.shape, 1), SWA_HD)
    width = x.shape[1]
    partner = jnp.where(lane < half, pltpu.roll(x, width - half, 1), pltpu.roll(x, half, 1))
    return x * cos + partner * sin


def _norm_matmul_kernel(x_ref, g_ref, w_ref, o_ref, xn_ref):
    @pl.when(pl.program_id(1) == 0)
    def _():
        x = x_ref[...]
        ms = jnp.mean(x * x, axis=-1, keepdims=True)
        xn_ref[...] = (x * lax.rsqrt(ms + NORM_EPS) * g_ref[...]).astype(BF16)

    o_ref[...] = _dot(xn_ref[...], w_ref[...]).astype(o_ref.dtype)


def _norm_matmul(x, g, w, tm, tn):
    m, d = x.shape
    n = w.shape[1]
    return pl.pallas_call(
        _norm_matmul_kernel,
        grid=(m // tm, n // tn),
        in_specs=[pl.BlockSpec((tm, d), lambda i, j: (i, 0)),
                  pl.BlockSpec((1, d), lambda i, j: (0, 0)),
                  pl.BlockSpec((d, tn), lambda i, j: (0, j))],
        out_specs=pl.BlockSpec((tm, tn), lambda i, j: (i, j)),
        out_shape=jax.ShapeDtypeStruct((m, n), F32),
        scratch_shapes=[pltpu.VMEM((tm, d), BF16)],
        compiler_params=_params("arbitrary", "arbitrary"),
        name="norm_matmul",
    )(x, g.reshape(1, d), w)


def _ret_tables(c, reps, pad=None):
    pad = c if pad is None else pad
    lg = jnp.log1p(-jnp.exp2(-5.0 - jnp.arange(RET_HEADS, dtype=F32)))
    idx = jnp.arange(c, dtype=F32)
    rel = idx[:, None] - idx[None, :]
    decay = jnp.where(rel >= 0, jnp.exp(lg[:, None, None] * jnp.maximum(rel, 0.0)), 0.0)
    q_decay = jnp.exp(lg[:, None] * (idx[None, :] + 1.0))
    k_decay = jnp.exp(lg[:, None] * (c - 1.0 - idx[None, :]))
    chunk_decay = jnp.exp(lg * c)
    decay = jnp.pad(decay, ((0, 0), (0, pad - c), (0, pad - c)))
    q_decay = jnp.pad(q_decay, ((0, 0), (0, pad - c)))
    k_decay = jnp.pad(k_decay, ((0, 0), (0, pad - c)))
    if reps > 1:
        eye = jnp.eye(reps, dtype=F32)
        decay = jax.vmap(lambda dm: jnp.kron(eye, dm))(decay)
        q_decay = jnp.tile(q_decay, (1, reps))
        k_decay = jnp.tile(k_decay, (1, reps))
    rows = pad * reps
    qd = jnp.broadcast_to(q_decay[:, :, None], (RET_HEADS, rows, RET_DK))
    kd = jnp.broadcast_to(k_decay[:, :, None], (RET_HEADS, rows, RET_DK))
    cd = jnp.broadcast_to(chunk_decay[:, None, None], (RET_HEADS, 1, RET_DV))
    return decay, qd, kd, cd


def _group_norm_gate(o, gn, g):
    mu = jnp.mean(o, axis=-1, keepdims=True)
    d = o - mu
    var = jnp.mean(d * d, axis=-1, keepdims=True)
    return d * lax.rsqrt(var + NORM_EPS) * gn * _silu(g)


def _ret_prompt_kernel(q_ref, k_ref, v_ref, g_ref, cos_ref, sin_ref, dec_ref, qd_ref, kd_ref, cd_ref, gn_ref,
                       y_ref, st_ref):
    @pl.when(pl.program_id(0) == 0)
    def _():
        st_ref[...] = jnp.zeros_like(st_ref)

    cos = cos_ref[...]
    sin = sin_ref[...]
    for h in range(RET_HEADS):
        ks = slice(h * RET_DK, (h + 1) * RET_DK)
        vs = slice(h * RET_DV, (h + 1) * RET_DV)
        q = _rot_full(q_ref[:, ks], cos, sin)
        k = _rot_full(k_ref[:, ks] * (RET_DK ** -0.5), cos, sin)
        v = v_ref[:, vs].astype(BF16)
        st = st_ref[h]
        s = _dot_nt(q.astype(BF16), k.astype(BF16)) * dec_ref[h]
        o = _dot(s.astype(BF16), v) + _dot((q * qd_ref[h]).astype(BF16), st.astype(BF16))
        st_ref[h] = cd_ref[h] * st + _dot((k * kd_ref[h]).T.astype(BF16), v)
        y_ref[:, vs] = _group_norm_gate(o, gn_ref[:, vs], g_ref[:, vs]).astype(y_ref.dtype)


def _ret_prompt(proj, cos, sin, ret_gn):
    t = proj.shape[0]
    c = RET_CHUNK
    dec, qd, kd, cd = _ret_tables(c, 1)
    qw = RET_HEADS * RET_DK
    vw = RET_HEADS * RET_DV
    full3 = lambda n: (0, 0, 0)
    return pl.pallas_call(
        _ret_prompt_kernel,
        grid=(t // c,),
        in_specs=[pl.BlockSpec((c, qw), lambda n: (n, E_AQ // qw)),
                  pl.BlockSpec((c, qw), lambda n: (n, E_AK // qw)),
                  pl.BlockSpec((c, vw), lambda n: (n, E_AV // vw)),
                  pl.BlockSpec((c, vw), lambda n: (n, E_AG // vw)),
                  pl.BlockSpec((c, RET_DK), lambda n: (n, 0)),
                  pl.BlockSpec((c, RET_DK), lambda n: (n, 0)),
                  pl.BlockSpec(dec.shape, full3),
                  pl.BlockSpec(qd.shape, full3),
                  pl.BlockSpec(kd.shape, full3),
                  pl.BlockSpec(cd.shape, full3),
                  pl.BlockSpec((1, vw), lambda n: (0, 0))],
        out_specs=[pl.BlockSpec((c, vw), lambda n: (n, 0)),
                   pl.BlockSpec((RET_HEADS, RET_DK, RET_DV), full3)],
        out_shape=[jax.ShapeDtypeStruct((t, vw), BF16),
                   jax.ShapeDtypeStruct((RET_HEADS, RET_DK, RET_DV), F32)],
        compiler_params=_params("arbitrary"),
        name="ret_prompt",
    )(proj, proj, proj, proj, cos, sin, dec, qd, kd, cd, ret_gn.reshape(1, vw))


def _ret_sample_kernel(q_ref, k_ref, v_ref, g_ref, cos_ref, sin_ref, dec_ref, qd_ref, kd_ref, cd_ref, gn_ref,
                       st_ref, y_ref, sto_ref, cross_ref, *, pad):
    b = pl.program_id(0)
    nb = pl.num_programs(0)
    win = RET_CHUNK

    @pl.when(b == 0)
    def _():
        cross_ref[...] = jnp.zeros_like(cross_ref)

    w0 = pl.multiple_of(jnp.right_shift(b * pad, win.bit_length() - 1) * win, win)
    wrows = pl.ds(w0, win)
    cos = cos_ref[wrows, :]
    sin = sin_ref[wrows, :]
    rows = lax.broadcasted_iota(jnp.int32, (win, RET_DK), 0) + w0
    mine = (rows >= b * pad) & (rows < (b + 1) * pad)
    for h in range(RET_HEADS):
        ks = slice(h * RET_DK, (h + 1) * RET_DK)
        vs = slice(h * RET_DV, (h + 1) * RET_DV)
        q = _rot_full(q_ref[wrows, ks], cos, sin)
        k = _rot_full(k_ref[wrows, ks] * (RET_DK ** -0.5), cos, sin)
        v = v_ref[wrows, vs].astype(BF16)
        st = st_ref[0, h]
        qm = jnp.where(mine, q * qd_ref[h, wrows, :], 0.0)
        km = jnp.where(mine, k * kd_ref[h, wrows, :], 0.0)
        cross_ref[wrows, vs] += _dot(qm.astype(BF16), st.astype(BF16))
        sto_ref[0, h] = cd_ref[h] * st + _dot(km.T.astype(BF16), v)

    @pl.when(b == nb - 1)
    def _():
        cos_all = cos_ref[...]
        sin_all = sin_ref[...]
        for h in range(RET_HEADS):
            ks = slice(h * RET_DK, (h + 1) * RET_DK)
            vs = slice(h * RET_DV, (h + 1) * RET_DV)
            q = _rot_full(q_ref[:, ks], cos_all, sin_all)
            k = _rot_full(k_ref[:, ks] * (RET_DK ** -0.5), cos_all, sin_all)
            v = v_ref[:, vs].astype(BF16)
            s = _dot_nt(q.astype(BF16), k.astype(BF16)) * dec_ref[h]
            o = _dot(s.astype(BF16), v) + cross_ref[:, vs]
            y_ref[:, vs] = _group_norm_gate(o, gn_ref[:, vs], g_ref[:, vs]).astype(y_ref.dtype)


def _ret_sample(proj, state, cos, sin, ret_gn, nseq, seq, pad):
    rows = nseq * pad
    c = math.gcd(seq, RET_CHUNK)
    assert c == seq, "sample chunk must cover the new tokens"
    assert rows % RET_CHUNK == 0 and RET_CHUNK % pad == 0
    dec, qd, kd, cd = _ret_tables(c, nseq, pad)
    qw = RET_HEADS * RET_DK
    vw = RET_HEADS * RET_DV
    full3 = lambda b: (0, 0, 0)
    st_spec = pl.BlockSpec((1, RET_HEADS, RET_DK, RET_DV), lambda b: (b, 0, 0, 0))
    return pl.pallas_call(
        functools.partial(_ret_sample_kernel, pad=pad),
        grid=(nseq,),
        in_specs=[pl.BlockSpec((rows, qw), lambda b: (0, E_AQ // qw)),
                  pl.BlockSpec((rows, qw), lambda b: (0, E_AK // qw)),
                  pl.BlockSpec((rows, vw), lambda b: (0, E_AV // vw)),
                  pl.BlockSpec((rows, vw), lambda b: (0, E_AG // vw)),
                  pl.BlockSpec((rows, RET_DK), lambda b: (0, 0)),
                  pl.BlockSpec((rows, RET_DK), lambda b: (0, 0)),
                  pl.BlockSpec(dec.shape, full3),
                  pl.BlockSpec(qd.shape, full3),
                  pl.BlockSpec(kd.shape, full3),
                  pl.BlockSpec(cd.shape, full3),
                  pl.BlockSpec((1, vw), lambda b: (0, 0)),
                  st_spec],
        out_specs=[pl.BlockSpec((rows, vw), lambda b: (0, 0)), st_spec],
        out_shape=[jax.ShapeDtypeStruct((rows, vw), F32),
                   jax.ShapeDtypeStruct(state.shape, F32)],
        scratch_shapes=[pltpu.VMEM((rows, vw), F32)],
        compiler_params=_params("arbitrary"),
        name="ret_sample",
    )(proj, proj, proj, proj, cos, sin, dec, qd, kd, cd, ret_gn.reshape(1, vw), state)


def _swa_prompt_kernel(q_ref, kc_ref, vc_ref, g_ref, cos_ref, sin_ref,
                       y_ref, ko_ref, q_s, k_s, v_s, m_s, l_s, acc_s):
    t = pl.program_id(1)
    nt = pl.num_programs(1)
    tb = SWA_TILE
    blk = SWA_BLOCK
    hd = SWA_HD
    cos = cos_ref[...]
    sin = sin_ref[...]

    @pl.when(t == 0)
    def _():
        k_s[0:tb, :] = jnp.zeros((tb, hd), F32)
        v_s[0:tb, :] = jnp.zeros((tb, hd), F32)

    @pl.when(t > 0)
    def _():
        k_s[0:tb, :] = k_s[tb:2 * tb, :]
        v_s[0:tb, :] = v_s[tb:2 * tb, :]

    q_s[...] = _rot_partial(q_ref[...] * (SWA_HD ** -0.5 * LOG2E), cos, sin)
    kr = _rot_partial(kc_ref[...], cos, sin)
    k_s[tb:2 * tb, :] = kr
    v_s[tb:2 * tb, :] = vc_ref[...]

    @pl.when(t == nt - 1)
    def _():
        ko_ref[...] = kr

    qi = lax.broadcasted_iota(jnp.int32, (blk, 2 * blk), 0)
    kj = lax.broadcasted_iota(jnp.int32, (blk, 2 * blk), 1)
    band = (kj >= qi) & (kj <= qi + blk)
    band_first = band & ((kj >= blk) | (t > 0))
    ones = jnp.ones((2 * blk, hd), BF16)

    def ds(start, size, stride):
        return pl.ds(start, size) if stride == 1 else pl.ds(start, size, stride=stride)

    for bi, (window, r) in enumerate(SWA_PATTERNS):
        assert window // r == blk
        for idx in range(tb // blk):
            c, n = idx % r, idx // r
            qs = n * blk * r + c
            ks = qs + tb - blk * r
            qrows = ds(qs, blk, r)
            krows = ds(ks, 2 * blk, r)
            valid = band_first if n == 0 else band
            s = _dot_nt(q_s[qrows, :].astype(BF16), k_s[krows, :].astype(BF16))
            s = jnp.where(valid, s, NEG_INF)
            m = jnp.max(s, axis=-1, keepdims=True)
            p = jnp.exp2(s - m).astype(BF16)
            pv = _dot(p, jnp.concatenate([v_s[krows, :].astype(BF16), ones], axis=1))
            m_s[bi, qrows, :] = jnp.broadcast_to(m, (blk, hd))
            l_s[bi, qrows, :] = pv[:, hd:]
            acc_s[bi, qrows, :] = pv[:, :hd]

    nbr = len(SWA_PATTERNS)
    ms = [m_s[i] for i in range(nbr)]
    mmax = functools.reduce(jnp.maximum, ms)
    wts = [jnp.exp2(mi - mmax) for mi in ms]
    num = sum(wts[i] * acc_s[i] for i in range(nbr))
    den = sum(wts[i] * l_s[i] for i in range(nbr))
    y_ref[...] = (num / den * _silu(g_ref[...])).astype(y_ref.dtype)


def _swa_prompt(proj, cos, sin):
    t = proj.shape[0]
    tb = SWA_TILE
    assert t % tb == 0
    hd = SWA_HD
    nbr = len(SWA_PATTERNS)
    cur = lambda off: pl.BlockSpec((tb, hd), lambda h, i: (i, off // hd + h))
    return pl.pallas_call(
        _swa_prompt_kernel,
        grid=(SWA_HEADS, t // tb),
        in_specs=[cur(E_BQ), cur(E_BK), cur(E_BV), cur(E_BG),
                  pl.BlockSpec((tb, hd), lambda h, i: (i, 0)),
                  pl.BlockSpec((tb, hd), lambda h, i: (i, 0))],
        out_specs=[pl.BlockSpec((tb, hd), lambda h, i: (i, h)),
                   pl.BlockSpec((tb, hd), lambda h, i: (0, h))],
        out_shape=[jax.ShapeDtypeStruct((t, SWA_HEADS * hd), BF16),
                   jax.ShapeDtypeStruct((tb, SWA_HEADS * hd), F32)],
        scratch_shapes=[pltpu.VMEM((tb, hd), F32),
                        pltpu.VMEM((2 * tb, hd), F32),
                        pltpu.VMEM((2 * tb, hd), F32),
                        pltpu.VMEM((nbr, tb, hd), F32),
                        pltpu.VMEM((nbr, tb, hd), F32),
                        pltpu.VMEM((nbr, tb, hd), F32)],
        compiler_params=_params("arbitrary", "arbitrary"),
        name="swa_prompt",
    )(proj, proj, proj, proj, cos, sin)


def _swa_sample_kernel(q_ref, k_ref, v_ref, g_ref, cos_ref, sin_ref, ka_ref, kb_ref, va_ref, vb_ref,
                       y_ref, ko_ref, kn_s, vn_s, *, seq):
    pad = SAMPLE_PAD
    nh = SWA_HEADS
    hd = SWA_HD
    nr = nh * pad
    cos = cos_ref[...]
    sin = sin_ref[...]
    q8 = _rot_partial(q_ref[0] * (SWA_HD ** -0.5), cos, sin)
    k8 = _rot_partial(k_ref[0], cos, sin)
    ko_ref[0] = k8

    def head_rows(x):
        return jnp.concatenate([x[:, h * hd:(h + 1) * hd] for h in range(nh)], axis=0)

    qrows = head_rows(q8).astype(BF16)
    kn_s[...] = jnp.zeros_like(kn_s)
    vn_s[...] = jnp.zeros_like(vn_s)
    kn_s[0:nr, :] = head_rows(k8).astype(BF16)
    vn_s[0:nr, :] = head_rows(v_ref[0]).astype(BF16)
    nn = kn_s.shape[0]

    ka = ka_ref[0]
    va = va_ref[0]
    nj, nrh, _ = kb_ref.shape[1:]
    kb = kb_ref[0].reshape(nj * nrh, hd)
    vb = vb_ref[0].reshape(nj * nrh, hd)
    na = ka.shape[0]
    s_a = _dot_nt(qrows, ka.astype(BF16))
    s_b = _dot_nt(qrows, kb.astype(BF16))
    s_n = _dot_nt(qrows, kn_s[...])

    (w1, d1), (w2, d2), (w3, d3) = SWA_PATTERNS

    def rows_cols(n):
        r = lax.broadcasted_iota(jnp.int32, (nr, n), 0)
        c = lax.broadcasted_iota(jnp.int32, (nr, n), 1)
        return _div(r, pad), _mod(r, pad), c

    hq, tq, c = rows_cols(na)
    dist = na // nh + tq - _div(c, nh)
    mult_a = (dist <= w1).astype(F32) + ((_mod(dist, d2) == 0) & (dist <= w2)).astype(F32)
    mult_a = jnp.where((hq == _mod(c, nh)) & (tq < seq), mult_a, 0.0)
    hq, tq, c = rows_cols(nj * nrh)
    mult_b = jnp.where((hq == _mod(c, nh)) & (tq == _mod(_div(c, nh), nrh // nh)), 1.0, 0.0)
    hq, tq, c = rows_cols(nn)
    tk = _mod(c, pad)
    mult_n = (tk <= tq).astype(F32) + 2.0 * (tk == tq).astype(F32)
    mult_n = jnp.where((hq == _div(c, pad)) & (tq < seq) & (tk < seq), mult_n, 0.0)

    def rowmax(s, mult):
        return jnp.max(jnp.where(mult > 0, s, NEG_INF), axis=-1, keepdims=True)

    m = jnp.maximum(jnp.maximum(rowmax(s_a, mult_a), rowmax(s_b, mult_b)), rowmax(s_n, mult_n))

    def probs(s, mult):
        return jnp.where(mult > 0, jnp.exp(s - m), 0.0) * mult

    p_a, p_b, p_n = probs(s_a, mult_a), probs(s_b, mult_b), probs(s_n, mult_n)
    l = (jnp.sum(p_a, axis=-1, keepdims=True) + jnp.sum(p_b, axis=-1, keepdims=True)
         + jnp.sum(p_n, axis=-1, keepdims=True))
    l = jnp.where(l > 0, l, 1.0)
    o = (_dot(p_a.astype(BF16), va.astype(BF16)) + _dot(p_b.astype(BF16), vb.astype(BF16))
         + _dot(p_n.astype(BF16), vn_s[...])) / l
    y = jnp.concatenate([o[h * pad:(h + 1) * pad] for h in range(nh)], axis=1)
    y_ref[0] = y * _silu(g_ref[0])


def _swa_sample(proj_pad, cache_k, cache_v, cos, sin, seq):
    nseq, pad, _ = proj_pad.shape
    buf = cache_k.shape[1]
    nh, hd = SWA_HEADS, SWA_HD
    w = nh * hd
    (w1, d1), (w2, d2), (w3, d3) = SWA_PATTERNS
    assert buf == w3 and d1 == 1 and seq <= d2 and w1 <= w2 and buf % w2 == 0 and buf % d3 == 0
    assert nh * pad <= SWA_BLOCK and (seq * nh) % 8 == 0
    flat = lambda a: a.reshape(nseq, buf * nh, hd)
    strided = lambda a: a.reshape(nseq, buf // d3, d3 * nh, hd)
    col = lambda off: pl.BlockSpec((1, pad, w), lambda b: (b, 0, off // w))
    tail = pl.BlockSpec((1, w2 * nh, hd), lambda b: (b, buf // w2 - 1, 0))
    resid = pl.BlockSpec((1, buf // d3, seq * nh, hd), lambda b: (b, 0, 0, 0))
    tab = pl.BlockSpec((pad, w), lambda b: (0, 0))
    out = pl.BlockSpec((1, pad, w), lambda b: (b, 0, 0))
    return pl.pallas_call(
        functools.partial(_swa_sample_kernel, seq=seq),
        grid=(nseq,),
        in_specs=[col(E_BQ), col(E_BK), col(E_BV), col(E_BG), tab, tab, tail, resid, tail, resid],
        out_specs=[out, out],
        out_shape=[jax.ShapeDtypeStruct((nseq, pad, w), F32),
                   jax.ShapeDtypeStruct((nseq, pad, w), F32)],
        scratch_shapes=[pltpu.VMEM((SWA_BLOCK, hd), BF16), pltpu.VMEM((SWA_BLOCK, hd), BF16)],
        compiler_params=_params("arbitrary"),
        name="swa_sample",
    )(proj_pad, proj_pad, proj_pad, proj_pad, cos, sin,
      flat(cache_k), strided(cache_k), flat(cache_v), strided(cache_v))


def _mem_attn_kernel(q_ref, g_ref, k_ref, v_ref, o_ref):
    ones = jnp.ones((MEM_LEN, MEM_HD), BF16)
    for h in range(MEM_HEADS):
        hs = slice(h * MEM_HD, (h + 1) * MEM_HD)
        q = (q_ref[0, :, hs] * (MEM_HD ** -0.5 * LOG2E)).astype(BF16)
        s = _dot_nt(q, k_ref[0, :, hs].astype(BF16))
        m = jnp.max(s, axis=-1, keepdims=True)
        p = jnp.exp2(s - m).astype(BF16)
        pv = _dot(p, jnp.concatenate([v_ref[0, :, hs].astype(BF16), ones], axis=1))
        o = pv[:, :MEM_HD] / pv[:, MEM_HD:]
        o_ref[0, :, hs] = (o * _silu(g_ref[0, :, hs])).astype(o_ref.dtype)


def _mem_attn_rows_kernel(q_ref, g_ref, k_ref, v_ref, o_ref):
    pad = q_ref.shape[1]
    nh, hd = MEM_HEADS, MEM_HD
    nr = nh * pad
    q = q_ref[0] * (MEM_HD ** -0.5)
    qrows = jnp.concatenate([q[:, h * hd:(h + 1) * hd] for h in range(nh)], axis=0).astype(BF16)
    s = _dot_nt(qrows, k_ref[0].astype(BF16))
    r = lax.broadcasted_iota(jnp.int32, s.shape, 0)
    c = lax.broadcasted_iota(jnp.int32, s.shape, 1)
    same_head = _div(r, pad) == _mod(c, nh)
    m = jnp.max(jnp.where(same_head, s, NEG_INF), axis=-1, keepdims=True)
    p = jnp.where(same_head, jnp.exp(s - m), 0.0)
    l = jnp.sum(p, axis=-1, keepdims=True)
    o = _dot(p.astype(BF16), v_ref[0].astype(BF16)) / l
    y = jnp.concatenate([o[h * pad:(h + 1) * pad] for h in range(nh)], axis=1)
    o_ref[0] = y * _silu(g_ref[0])


def _mem_attn_rows(proj_pad, q_off, g_off, mem_k, mem_v, layer):
    b, pad, _ = proj_pad.shape
    w = MEM_W
    rows = MEM_LEN * MEM_HEADS
    kv = pl.BlockSpec((1, rows, MEM_HD), lambda i: (layer * b + i, 0, 0))
    return pl.pallas_call(
        _mem_attn_rows_kernel,
        grid=(b,),
        in_specs=[pl.BlockSpec((1, pad, w), lambda i: (i, 0, q_off // w)),
                  pl.BlockSpec((1, pad, w), lambda i: (i, 0, g_off // w)),
                  kv, kv],
        out_specs=pl.BlockSpec((1, pad, w), lambda i: (i, 0, 0)),
        out_shape=jax.ShapeDtypeStruct((b, pad, w), F32),
        compiler_params=_params("arbitrary"),
        name="mem_attn_rows",
    )(proj_pad, proj_pad, mem_k.reshape(-1, rows, MEM_HD), mem_v.reshape(-1, rows, MEM_HD))


def _mem_attn(proj, q_off, g_off, mem_k, k_off, mem_v, v_off, tq, out_dtype):
    b, t, _ = proj.shape
    w = MEM_W
    return pl.pallas_call(
        _mem_attn_kernel,
        grid=(b, t // tq),
        in_specs=[pl.BlockSpec((1, tq, w), lambda i, j: (i, j, q_off // w)),
                  pl.BlockSpec((1, tq, w), lambda i, j: (i, j, g_off // w)),
                  pl.BlockSpec((1, MEM_LEN, w), lambda i, j: (i, 0, k_off // w)),
                  pl.BlockSpec((1, MEM_LEN, w), lambda i, j: (i, 0, v_off // w))],
        out_specs=pl.BlockSpec((1, tq, w), lambda i, j: (i, j, 0)),
        out_shape=jax.ShapeDtypeStruct((b, t, w), out_dtype),
        compiler_params=_params("arbitrary", "arbitrary"),
        name="mem_attn",
    )(proj, proj, mem_k, mem_v)


def _gmlp_kernel(u_ref, v_ref, z_ref, ln_ref, w_ref, b_ref, y_ref, vo_ref):
    c = GMLP_CHUNK
    v = jax.nn.gelu(v_ref[...])
    mu = jnp.mean(v, axis=-1, keepdims=True)
    d = v - mu
    var = jnp.mean(d * d, axis=-1, keepdims=True)
    vn = d * lax.rsqrt(var + NORM_EPS) * ln_ref[...]
    vo_ref[...] = vn
    causal = (lax.broadcasted_iota(jnp.int32, (c, c), 0) >= lax.broadcasted_iota(jnp.int32, (c, c), 1))
    for g in range(GMLP_GROUPS):
        gs = slice(g * GMLP_GROUP, (g + 1) * GMLP_GROUP)
        wc = jnp.where(causal, w_ref[g], 0.0).astype(BF16)
        mixed = _dot(wc, vn[:, gs].astype(BF16)) + b_ref[:, gs]
        y_ref[:, gs] = (jax.nn.gelu(u_ref[:, gs]) * mixed * _silu(z_ref[:, gs])).astype(y_ref.dtype)


def _gmlp(proj, gmlp_ln, w_mix, bias_rows, keep_all_v):
    t = proj.shape[0]
    c = GMLP_CHUNK
    w = GMLP_WIDTH
    col = lambda off: pl.BlockSpec((c, w), lambda n: (n, off // w))
    return pl.pallas_call(
        _gmlp_kernel,
        grid=(t // c,),
        in_specs=[col(O_U), col(O_V), col(O_Z),
                  pl.BlockSpec((1, w), lambda n: (0, 0)),
                  pl.BlockSpec(w_mix.shape, lambda n: (0, 0, 0)),
                  pl.BlockSpec((c, w), lambda n: (0, 0))],
        out_specs=[pl.BlockSpec((c, w), lambda n: (n, 0)),
                   pl.BlockSpec((c, w), (lambda n: (n, 0)) if keep_all_v else (lambda n: (0, 0)))],
        out_shape=[jax.ShapeDtypeStruct((t, w), BF16),
                   jax.ShapeDtypeStruct((t if keep_all_v else c, w), F32)],
        compiler_params=_params("arbitrary"),
        name="gmlp",
    )(proj, proj, proj, gmlp_ln.reshape(1, w), w_mix, bias_rows)


def _out_proj_kernel(*refs, n_in):
    a_refs, w_refs = refs[:n_in], refs[n_in:2 * n_in]
    g_ref, x_ref, o_ref = refs[2 * n_in:]
    acc = _dot(a_refs[0][...].astype(BF16), w_refs[0][...])
    for a_ref, w_ref in zip(a_refs[1:], w_refs[1:]):
        acc = acc + _dot(a_ref[...].astype(BF16), w_ref[...])
    ms = jnp.mean(acc * acc, axis=-1, keepdims=True)
    o_ref[...] = x_ref[...] + acc * lax.rsqrt(ms + NORM_EPS) * g_ref[...]


def _out_proj(acts, w, g, x, tm):
    m, d = x.shape
    n_in = len(acts)
    w_specs, off = [], 0
    for a in acts:
        k = a.shape[1]
        assert off % k == 0
        w_specs.append(pl.BlockSpec((k, d), functools.partial(lambda i, blk: (blk, 0), blk=off // k)))
        off += k
    assert off == w.shape[0]
    return pl.pallas_call(
        functools.partial(_out_proj_kernel, n_in=n_in),
        grid=(m // tm,),
        in_specs=([pl.BlockSpec((tm, a.shape[1]), lambda i: (i, 0)) for a in acts]
                  + w_specs
                  + [pl.BlockSpec((1, d), lambda i: (0, 0)),
                     pl.BlockSpec((tm, d), lambda i: (i, 0))]),
        out_specs=pl.BlockSpec((tm, d), lambda i: (i, 0)),
        out_shape=jax.ShapeDtypeStruct((m, d), F32),
        compiler_params=_params("arbitrary"),
        name="out_proj",
    )(*acts, *([w] * n_in), g.reshape(1, d), x)


def _pad_steps(a, pad):
    nseq, seq, n = a.shape
    return jnp.pad(a, ((0, 0), (0, pad - seq), (0, 0))).reshape(nseq * pad, n)


def kernel(x_prompt, x_sample, state_ret, cache_swa_k, cache_swa_v, cache_mem_k, cache_mem_v, mem_prompt,
           pre_norm, post_norm, mem_norm, w_mem_k, w_mem_v, w_in_even, ret_gn, w_out_even,
           w_in_odd, gmlp_ln, w_spatial, b_spatial, w_out_odd):
    bp, tp, d = x_prompt.shape
    nseq, seq, _ = x_sample.shape
    pad = SAMPLE_PAD
    rows_s = nseq * pad
    c = GMLP_CHUNK
    assert bp == 1 and seq <= pad and rows_s % c == 0 and c % pad == 0
    hp = x_prompt.reshape(tp, d)
    hs = _pad_steps(x_sample, pad)
    mem = mem_prompt.reshape(MEM_LEN, d)
    tm_p = 1024 if tp % 1024 == 0 else 512
    unpad = lambda a: a.reshape(nseq, pad, -1)[:, :seq]

    pos_p = jnp.arange(tp, dtype=jnp.int32)
    pos_s = PAST_LEN + jnp.arange(seq, dtype=jnp.int32)
    pos_pad = jnp.concatenate([pos_s, jnp.zeros((pad - seq,), jnp.int32)])
    ret_cos_p, ret_sin_p = _rot_tables(pos_p, RET_DK, RET_THETA, RET_DK)
    ret_cos_s, ret_sin_s = _rot_tables(jnp.tile(pos_pad, nseq), RET_DK, RET_THETA, RET_DK)
    swa_cos_p, swa_sin_p = _rot_tables(pos_p, ROPE_DIMS, ROPE_THETA, SWA_HD)
    swa_cos_s, swa_sin_s = _rot_tables(pos_pad, ROPE_DIMS, ROPE_THETA, SWA_HD)
    swa_cos_s = jnp.tile(swa_cos_s, (1, SWA_HEADS))
    swa_sin_s = jnp.tile(swa_sin_s, (1, SWA_HEADS))

    w_mem0 = jnp.concatenate([w_mem_k[0], w_mem_v[0]], axis=1).astype(BF16)
    memkv0 = _norm_matmul(mem, mem_norm[0], w_mem0, MEM_LEN, 2 * MEM_W)
    w_in0 = w_in_even[0].astype(BF16)
    w_out0 = w_out_even[0].astype(BF16)

    proj_p = _norm_matmul(hp, pre_norm[0], w_in0, tm_p, 1024)
    ya_p, st_p = _ret_prompt(proj_p, ret_cos_p, ret_sin_p, ret_gn[0])
    yb_p, pk = _swa_prompt(proj_p, swa_cos_p, swa_sin_p)
    ym_p = _mem_attn(proj_p[None], E_MQ, E_MG, memkv0[None], 0, memkv0[None], MEM_W, 1024, BF16)[0]
    hp1 = _out_proj([ya_p, yb_p, ym_p], w_out0, post_norm[0], hp, 256)
    buf_p = min(SWA_MAX_WINDOW, tp)
    pv = proj_p[tp - buf_p:, E_BV:E_BV + SWA_HEADS * SWA_HD]

    proj_s = _norm_matmul(hs, pre_norm[0], w_in0, rows_s, 1024)
    proj_s3 = proj_s.reshape(nseq, pad, EVEN_IN)
    ya_s, st_s = _ret_sample(proj_s, state_ret[0], ret_cos_s, ret_sin_s, ret_gn[0], nseq, seq, pad)
    yb_s, sk = _swa_sample(proj_s3, cache_swa_k[0], cache_swa_v[0], swa_cos_s, swa_sin_s, seq)
    ym_s = _mem_attn_rows(proj_s3, E_MQ, E_MG, cache_mem_k, cache_mem_v, 0)
    hs1 = _out_proj([ya_s, yb_s.reshape(rows_s, -1), ym_s.reshape(rows_s, -1)], w_out0, post_norm[0], hs, rows_s)
    sk = sk[:, :seq].reshape(nseq, seq, SWA_HEADS, SWA_HD)
    sv = unpad(proj_s[:, E_BV:E_BV + SWA_HEADS * SWA_HD]).reshape(nseq, seq, SWA_HEADS, SWA_HD)

    w_mem1 = jnp.concatenate([w_mem_k[1], w_mem_v[1]], axis=1).astype(BF16)
    memkv1 = _norm_matmul(mem, mem_norm[1], w_mem1, MEM_LEN, 2 * MEM_W)
    w_in1 = w_in_odd[0].astype(BF16)
    w_out1 = w_out_odd[0].astype(BF16)
    bias_p = jnp.repeat(b_spatial[0].T, GMLP_GROUP, axis=1)
    blockdiag = lambda wg: jnp.kron(jnp.eye(c // pad, dtype=F32),
                                    jnp.pad(wg[:seq, :seq], ((0, pad - seq), (0, pad - seq))))
    w_mix_s = jax.vmap(blockdiag)(w_spatial[0])
    bias_s = jnp.tile(jnp.pad(bias_p[:seq], ((0, pad - seq), (0, 0))), (c // pad, 1))

    proj_p1 = _norm_matmul(hp1, pre_norm[1], w_in1, tm_p, 1024)
    yc_p, gv_p = _gmlp(proj_p1, gmlp_ln[0], w_spatial[0], bias_p, False)
    ym_p1 = _mem_attn(proj_p1[None], O_MQ, O_MG, memkv1[None], 0, memkv1[None], MEM_W, 1024, BF16)[0]
    hp2 = _out_proj([yc_p, ym_p1], w_out1, post_norm[1], hp1, 256)

    proj_s1 = _norm_matmul(hs1, pre_norm[1], w_in1, rows_s, 1024)
    yc_s, gv_s = _gmlp(proj_s1, gmlp_ln[0], w_mix_s, bias_s, True)
    ym_s1 = _mem_attn_rows(proj_s1.reshape(nseq, pad, ODD_IN), O_MQ, O_MG, cache_mem_k, cache_mem_v, 1)
    hs2 = _out_proj([yc_s, ym_s1.reshape(rows_s, -1)], w_out1, post_norm[1], hs1, rows_s)

    memkv = jnp.stack([memkv0, memkv1])
    p_mk = memkv[:, :, :MEM_W].reshape(2, bp, MEM_LEN, MEM_HEADS, MEM_HD)
    p_mv = memkv[:, :, MEM_W:].reshape(2, bp, MEM_LEN, MEM_HEADS, MEM_HD)
    return (hp2.reshape(bp, tp, d), unpad(hs2),
            st_p[None, None],
            pk.reshape(1, bp, buf_p, SWA_HEADS, SWA_HD), pv.reshape(1, bp, buf_p, SWA_HEADS, SWA_HD),
            p_mk, p_mv,
            gv_p[None, None],
            st_s[None],
            sk[None], sv[None],
            unpad(gv_s)[None])
```

```python
import functools
import math

import jax
import jax.numpy as jnp
from jax import lax
from jax.experimental import pallas as pl
from jax.experimental.pallas import tpu as pltpu

F32 = jnp.float32
BF16 = jnp.bfloat16

D_MODEL = 2048
PAST_LEN = 8192
RET_HEADS = 8
RET_DK = 128
RET_DV = 256
RET_CHUNK = 128
RET_THETA = 10000.0
SWA_HEADS = 8
SWA_HD = 128
SWA_PATTERNS = ((128, 1), (512, 4), (2048, 16))
SWA_MAX_WINDOW = 2048
SWA_BLOCK = 128
ROPE_THETA = 500000.0
ROPE_DIMS = SWA_HD // 4
GMLP_CHUNK = 128
GMLP_WIDTH = D_MODEL
GMLP_GROUP = 128
GMLP_GROUPS = GMLP_WIDTH // GMLP_GROUP
MEM_LEN = 256
MEM_HEADS = 4
MEM_HD = 128
MEM_W = MEM_HEADS * MEM_HD
NORM_EPS = 1e-6
NEG_INF = -1e30
LOG2E = math.log2(math.e)

E_AQ = 0
E_AK = E_AQ + RET_HEADS * RET_DK
E_AV = E_AK + RET_HEADS * RET_DK
E_AG = E_AV + RET_HEADS * RET_DV
E_BQ = E_AG + RET_HEADS * RET_DV
E_BK = E_BQ + SWA_HEADS * SWA_HD
E_BV = E_BK + SWA_HEADS * SWA_HD
E_BG = E_BV + SWA_HEADS * SWA_HD
E_MQ = E_BG + SWA_HEADS * SWA_HD
E_MG = E_MQ + MEM_W
EVEN_IN = E_MG + MEM_W
O_U = 0
O_V = O_U + GMLP_WIDTH
O_Z = O_V + GMLP_WIDTH
O_MQ = O_Z + GMLP_WIDTH
O_MG = O_MQ + MEM_W
ODD_IN = O_MG + MEM_W

SWA_TILE = SWA_MAX_WINDOW
SAMPLE_PAD = 8
GMLP_STEP_CHUNKS = 2
RET_STEP_CHUNKS = 2
VMEM_LIMIT = 48 * 1024 * 1024


def _params(*sem):
    return pltpu.CompilerParams(dimension_semantics=sem, vmem_limit_bytes=VMEM_LIMIT)


def _dot(a, b):
    return jnp.dot(a, b, preferred_element_type=F32)


def _dot_nt(a, b):
    return lax.dot_general(a, b, (((1,), (1,)), ((), ())), preferred_element_type=F32)


def _silu(x):
    return x * jax.nn.sigmoid(x)


def _div(x, n):
    assert n & (n - 1) == 0
    return jnp.right_shift(x, n.bit_length() - 1)


def _mod(x, n):
    assert n & (n - 1) == 0
    return jnp.bitwise_and(x, n - 1)


def _rot_tables(pos, n_rot, theta, width):
    half = n_rot // 2
    inv = 1.0 / (theta ** (jnp.arange(half, dtype=F32) / half))
    ang = pos.astype(F32)[:, None] * inv[None, :]
    return _rot_assemble(jnp.cos(ang), jnp.sin(ang), width)


def _rot_tables_range(n, n_rot, theta, width, blk=128):
    assert n % blk == 0
    half = n_rot // 2
    inv = 1.0 / (theta ** (jnp.arange(half, dtype=F32) / half))
    hi = (jnp.arange(n // blk, dtype=F32) * blk)[:, None] * inv[None, :]
    lo = jnp.arange(blk, dtype=F32)[:, None] * inv[None, :]
    ch, sh, cl, sl = jnp.cos(hi)[:, None], jnp.sin(hi)[:, None], jnp.cos(lo)[None], jnp.sin(lo)[None]
    return _rot_assemble((ch * cl - sh * sl).reshape(n, half), (sh * cl + ch * sl).reshape(n, half), width)


def _rot_assemble(cos, sin, width):
    t, half = cos.shape
    cos_t = jnp.concatenate([cos, cos, jnp.ones((t, width - 2 * half), F32)], axis=-1)
    sin_t = jnp.concatenate([-sin, sin, jnp.zeros((t, width - 2 * half), F32)], axis=-1)
    return cos_t, sin_t


def _rot_full(x, cos, sin):
    return x * cos + pltpu.roll(x, RET_DK // 2, 1) * sin


def _rot_partial(x, cos, sin):
    half = ROPE_DIMS // 2
    lane = _mod(lax.broadcasted_iota(jnp.int32, x.shape, 1), SWA_HD)
    width = x.shape[1]
    partner = jnp.where(lane < half, pltpu.roll(x, width - half, 1), pltpu.roll(x, half, 1))
    return x * cos + partner * sin


def _norm_matmul_kernel(x_ref, g_ref, w_ref, o_ref, xn_ref):
    @pl.when(pl.program_id(1) == 0)
    def _():
        x = x_ref[...]
        ms = jnp.mean(x * x, axis=-1, keepdims=True)
        xn_ref[...] = (x * lax.rsqrt(ms + NORM_EPS) * g_ref[...]).astype(BF16)

    o_ref[...] = _dot(xn_ref[...], w_ref[...]).astype(o_ref.dtype)


def _norm_matmul(x, g, w, tm, tn):
    m, d = x.shape
    n = w.shape[1]
    return pl.pallas_call(
        _norm_matmul_kernel,
        grid=(m // tm, n // tn),
        in_specs=[pl.BlockSpec((tm, d), lambda i, j: (i, 0)),
                  pl.BlockSpec((1, d), lambda i, j: (0, 0)),
                  pl.BlockSpec((d, tn), lambda i, j: (0, j))],
        out_specs=pl.BlockSpec((tm, tn), lambda i, j: (i, j)),
        out_shape=jax.ShapeDtypeStruct((m, n), F32),
        scratch_shapes=[pltpu.VMEM((tm, d), BF16)],
        compiler_params=_params("arbitrary", "arbitrary"),
        name="norm_matmul",
    )(x, g.reshape(1, d), w)


def _norm_matmul_cast_kernel(x_ref, g_ref, w_ref, o_ref, wb_ref, xn_ref):
    @pl.when(pl.program_id(0) == 0)
    def _():
        x = x_ref[...]
        ms = jnp.mean(x * x, axis=-1, keepdims=True)
        xn_ref[...] = (x * lax.rsqrt(ms + NORM_EPS) * g_ref[...]).astype(BF16)

    wb = w_ref[...].astype(BF16)
    wb_ref[...] = wb
    o_ref[...] = _dot(xn_ref[...], wb).astype(o_ref.dtype)


def _norm_matmul_cast(x, g, w, tn):
    m, d = x.shape
    n = w.shape[1]
    return pl.pallas_call(
        _norm_matmul_cast_kernel,
        grid=(n // tn,),
        in_specs=[pl.BlockSpec((m, d), lambda j: (0, 0)),
                  pl.BlockSpec((1, d), lambda j: (0, 0)),
                  pl.BlockSpec((d, tn), lambda j: (0, j))],
        out_specs=[pl.BlockSpec((m, tn), lambda j: (0, j)),
                   pl.BlockSpec((d, tn), lambda j: (0, j))],
        out_shape=[jax.ShapeDtypeStruct((m, n), F32),
                   jax.ShapeDtypeStruct((d, n), BF16)],
        scratch_shapes=[pltpu.VMEM((m, d), BF16)],
        compiler_params=_params("arbitrary"),
        name="norm_matmul_cast",
    )(x, g.reshape(1, d), w)


def _ret_tables(c, reps, pad=None):
    pad = c if pad is None else pad
    rows = pad * reps
    lg = jnp.log1p(-jnp.exp2(-5.0 - jnp.arange(RET_HEADS, dtype=F32)))
    row = jnp.arange(rows, dtype=jnp.int32)
    step = (row % pad).astype(F32)
    live = (row % pad) < c
    rel = step[:, None] - step[None, :]
    keep = ((row // pad)[:, None] == (row // pad)[None, :]) & (rel >= 0) & live[:, None] & live[None, :]
    decay = jnp.where(keep[None], jnp.exp(lg[:, None, None] * jnp.maximum(rel, 0.0)[None]), 0.0)
    q_decay = jnp.where(live[None], jnp.exp(lg[:, None] * (step[None, :] + 1.0)), 0.0)
    k_decay = jnp.where(live[None], jnp.exp(lg[:, None] * (c - 1.0 - step[None, :])), 0.0)
    chunk_decay = jnp.exp(lg * c)
    qd = jnp.broadcast_to(q_decay[:, :, None], (RET_HEADS, rows, RET_DK))
    kd = jnp.broadcast_to(k_decay[:, :, None], (RET_HEADS, rows, RET_DK))
    cd = jnp.broadcast_to(chunk_decay[:, None, None], (RET_HEADS, 1, RET_DV))
    return decay, qd, kd, cd


def _group_norm_gate(o, gn, g):
    mu = jnp.mean(o, axis=-1, keepdims=True)
    d = o - mu
    var = jnp.mean(d * d, axis=-1, keepdims=True)
    return d * lax.rsqrt(var + NORM_EPS) * gn * _silu(g)


def _ret_prompt_kernel(q_ref, k_ref, v_ref, g_ref, cos_ref, sin_ref, dec_ref, qd_ref, kd_ref, cd_ref, gn_ref,
                       y_ref, st_ref):
    @pl.when(pl.program_id(0) == 0)
    def _():
        st_ref[...] = jnp.zeros_like(st_ref)

    for i in range(q_ref.shape[0] // RET_CHUNK):
        rs = slice(i * RET_CHUNK, (i + 1) * RET_CHUNK)
        cos = cos_ref[rs, :]
        sin = sin_ref[rs, :]
        for h in range(RET_HEADS):
            ks = slice(h * RET_DK, (h + 1) * RET_DK)
            vs = slice(h * RET_DV, (h + 1) * RET_DV)
            q = _rot_full(q_ref[rs, ks], cos, sin)
            k = _rot_full(k_ref[rs, ks] * (RET_DK ** -0.5), cos, sin)
            v = v_ref[rs, vs].astype(BF16)
            st = st_ref[h]
            s = _dot_nt(q.astype(BF16), k.astype(BF16)) * dec_ref[h]
            o = _dot(s.astype(BF16), v) + _dot((q * qd_ref[h]).astype(BF16), st.astype(BF16))
            st_ref[h] = cd_ref[h] * st + _dot((k * kd_ref[h]).T.astype(BF16), v)
            y_ref[rs, vs] = _group_norm_gate(o, gn_ref[:, vs], g_ref[rs, vs]).astype(y_ref.dtype)


def _ret_prompt(proj, cos, sin, ret_gn):
    t = proj.shape[0]
    dec, qd, kd, cd = _ret_tables(RET_CHUNK, 1)
    c = RET_STEP_CHUNKS * RET_CHUNK
    assert t % c == 0
    qw = RET_HEADS * RET_DK
    vw = RET_HEADS * RET_DV
    full3 = lambda n: (0, 0, 0)
    return pl.pallas_call(
        _ret_prompt_kernel,
        grid=(t // c,),
        in_specs=[pl.BlockSpec((c, qw), lambda n: (n, E_AQ // qw)),
                  pl.BlockSpec((c, qw), lambda n: (n, E_AK // qw)),
                  pl.BlockSpec((c, vw), lambda n: (n, E_AV // vw)),
                  pl.BlockSpec((c, vw), lambda n: (n, E_AG // vw)),
                  pl.BlockSpec((c, RET_DK), lambda n: (n, 0)),
                  pl.BlockSpec((c, RET_DK), lambda n: (n, 0)),
                  pl.BlockSpec(dec.shape, full3),
                  pl.BlockSpec(qd.shape, full3),
                  pl.BlockSpec(kd.shape, full3),
                  pl.BlockSpec(cd.shape, full3),
                  pl.BlockSpec((1, vw), lambda n: (0, 0))],
        out_specs=[pl.BlockSpec((c, vw), lambda n: (n, 0)),
                   pl.BlockSpec((RET_HEADS, RET_DK, RET_DV), full3)],
        out_shape=[jax.ShapeDtypeStruct((t, vw), BF16),
                   jax.ShapeDtypeStruct((RET_HEADS, RET_DK, RET_DV), F32)],
        compiler_params=_params("arbitrary"),
        name="ret_prompt",
    )(proj, proj, proj, proj, cos, sin, dec, qd, kd, cd, ret_gn.reshape(1, vw))


def _ret_sample_kernel(q_ref, k_ref, v_ref, g_ref, cos_ref, sin_ref, dec_ref, qd_ref, kd_ref, cd_ref, gn_ref,
                       st_ref, y_ref, sto_ref, cross_ref, *, pad):
    b = pl.program_id(0)
    nb = pl.num_programs(0)
    win = RET_CHUNK

    @pl.when(b == 0)
    def _():
        cross_ref[...] = jnp.zeros_like(cross_ref)

    w0 = pl.multiple_of(jnp.right_shift(b * pad, win.bit_length() - 1) * win, win)
    wrows = pl.ds(w0, win)
    cos = cos_ref[wrows, :]
    sin = sin_ref[wrows, :]
    rows = lax.broadcasted_iota(jnp.int32, (win, RET_DK), 0) + w0
    mine = (rows >= b * pad) & (rows < (b + 1) * pad)
    for h in range(RET_HEADS):
        ks = slice(h * RET_DK, (h + 1) * RET_DK)
        vs = slice(h * RET_DV, (h + 1) * RET_DV)
        q = _rot_full(q_ref[wrows, ks], cos, sin)
        k = _rot_full(k_ref[wrows, ks] * (RET_DK ** -0.5), cos, sin)
        v = v_ref[wrows, vs].astype(BF16)
        st = st_ref[0, h]
        qm = jnp.where(mine, q * qd_ref[h, wrows, :], 0.0)
        km = jnp.where(mine, k * kd_ref[h, wrows, :], 0.0)
        cross_ref[wrows, vs] += _dot(qm.astype(BF16), st.astype(BF16))
        sto_ref[0, h] = cd_ref[h] * st + _dot(km.T.astype(BF16), v)

    @pl.when(b == nb - 1)
    def _():
        cos_all = cos_ref[...]
        sin_all = sin_ref[...]
        for h in range(RET_HEADS):
            ks = slice(h * RET_DK, (h + 1) * RET_DK)
            vs = slice(h * RET_DV, (h + 1) * RET_DV)
            q = _rot_full(q_ref[:, ks], cos_all, sin_all)
            k = _rot_full(k_ref[:, ks] * (RET_DK ** -0.5), cos_all, sin_all)
            v = v_ref[:, vs].astype(BF16)
            s = _dot_nt(q.astype(BF16), k.astype(BF16)) * dec_ref[h]
            o = _dot(s.astype(BF16), v) + cross_ref[:, vs]
            y_ref[:, vs] = _group_norm_gate(o, gn_ref[:, vs], g_ref[:, vs]).astype(y_ref.dtype)


def _ret_sample(proj, state, cos, sin, ret_gn, nseq, seq, pad):
    rows = nseq * pad
    c = math.gcd(seq, RET_CHUNK)
    assert c == seq, "sample chunk must cover the new tokens"
    assert rows % RET_CHUNK == 0 and RET_CHUNK % pad == 0
    dec, qd, kd, cd = _ret_tables(c, nseq, pad)
    qw = RET_HEADS * RET_DK
    vw = RET_HEADS * RET_DV
    full3 = lambda b: (0, 0, 0)
    st_spec = pl.BlockSpec((1, RET_HEADS, RET_DK, RET_DV), lambda b: (b, 0, 0, 0))
    return pl.pallas_call(
        functools.partial(_ret_sample_kernel, pad=pad),
        grid=(nseq,),
        in_specs=[pl.BlockSpec((rows, qw), lambda b: (0, E_AQ // qw)),
                  pl.BlockSpec((rows, qw), lambda b: (0, E_AK // qw)),
                  pl.BlockSpec((rows, vw), lambda b: (0, E_AV // vw)),
                  pl.BlockSpec((rows, vw), lambda b: (0, E_AG // vw)),
                  pl.BlockSpec((rows, RET_DK), lambda b: (0, 0)),
                  pl.BlockSpec((rows, RET_DK), lambda b: (0, 0)),
                  pl.BlockSpec(dec.shape, full3),
                  pl.BlockSpec(qd.shape, full3),
                  pl.BlockSpec(kd.shape, full3),
                  pl.BlockSpec(cd.shape, full3),
                  pl.BlockSpec((1, vw), lambda b: (0, 0)),
                  st_spec],
        out_specs=[pl.BlockSpec((rows, vw), lambda b: (0, 0)), st_spec],
        out_shape=[jax.ShapeDtypeStruct((rows, vw), F32),
                   jax.ShapeDtypeStruct(state.shape, F32)],
        scratch_shapes=[pltpu.VMEM((rows, vw), F32)],
        compiler_params=_params("arbitrary"),
        name="ret_sample",
    )(proj, proj, proj, proj, cos, sin, dec, qd, kd, cd, ret_gn.reshape(1, vw), state)


def _swa_prompt_kernel(q_ref, kc_ref, vc_ref, g_ref, cos_ref, sin_ref,
                       y_ref, ko_ref, q_s, k_s, v_s, m_s, l_s, acc_s):
    t = pl.program_id(1)
    nt = pl.num_programs(1)
    tb = SWA_TILE
    blk = SWA_BLOCK
    hd = SWA_HD
    cos = cos_ref[...]
    sin = sin_ref[...]

    @pl.when(t == 0)
    def _():
        k_s[0:tb, :] = jnp.zeros((tb, hd), F32)
        v_s[0:tb, :] = jnp.zeros((tb, hd), F32)

    @pl.when(t > 0)
    def _():
        k_s[0:tb, :] = k_s[tb:2 * tb, :]
        v_s[0:tb, :] = v_s[tb:2 * tb, :]

    q_s[...] = _rot_partial(q_ref[...] * (SWA_HD ** -0.5 * LOG2E), cos, sin)
    kr = _rot_partial(kc_ref[...], cos, sin)
    k_s[tb:2 * tb, :] = kr
    v_s[tb:2 * tb, :] = vc_ref[...]

    @pl.when(t == nt - 1)
    def _():
        ko_ref[...] = kr

    qi = lax.broadcasted_iota(jnp.int32, (blk, 2 * blk), 0)
    kj = lax.broadcasted_iota(jnp.int32, (blk, 2 * blk), 1)
    band = (kj >= qi) & (kj <= qi + blk)
    band_first = band & ((kj >= blk) | (t > 0))
    ones = jnp.ones((2 * blk, hd), BF16)

    def ds(start, size, stride):
        return pl.ds(start, size) if stride == 1 else pl.ds(start, size, stride=stride)

    for bi, (window, r) in enumerate(SWA_PATTERNS):
        assert window // r == blk
        for idx in range(tb // blk):
            c, n = idx % r, idx // r
            qs = n * blk * r + c
            ks = qs + tb - blk * r
            qrows = ds(qs, blk, r)
            krows = ds(ks, 2 * blk, r)
            valid = band_first if n == 0 else band
            s = _dot_nt(q_s[qrows, :].astype(BF16), k_s[krows, :].astype(BF16))
            s = jnp.where(valid, s, NEG_INF)
            m = jnp.max(s, axis=-1, keepdims=True)
            p = jnp.exp2(s - m).astype(BF16)
            pv = _dot(p, jnp.concatenate([v_s[krows, :].astype(BF16), ones], axis=1))
            m_s[bi, qrows, :] = jnp.broadcast_to(m, (blk, hd))
            l_s[bi, qrows, :] = pv[:, hd:]
            acc_s[bi, qrows, :] = pv[:, :hd]

    nbr = len(SWA_PATTERNS)
    ms = [m_s[i] for i in range(nbr)]
    mmax = functools.reduce(jnp.maximum, ms)
    wts = [jnp.exp2(mi - mmax) for mi in ms]
    num = sum(wts[i] * acc_s[i] for i in range(nbr))
    den = sum(wts[i] * l_s[i] for i in range(nbr))
    y_ref[...] = (num / den * _silu(g_ref[...])).astype(y_ref.dtype)


def _swa_prompt(proj, cos, sin):
    t = proj.shape[0]
    tb = SWA_TILE
    assert t % tb == 0
    hd = SWA_HD
    nbr = len(SWA_PATTERNS)
    cur = lambda off: pl.BlockSpec((tb, hd), lambda h, i: (i, off // hd + h))
    return pl.pallas_call(
        _swa_prompt_kernel,
        grid=(SWA_HEADS, t // tb),
        in_specs=[cur(E_BQ), cur(E_BK), cur(E_BV), cur(E_BG),
                  pl.BlockSpec((tb, hd), lambda h, i: (i, 0)),
                  pl.BlockSpec((tb, hd), lambda h, i: (i, 0))],
        out_specs=[pl.BlockSpec((tb, hd), lambda h, i: (i, h)),
                   pl.BlockSpec((tb, hd), lambda h, i: (0, h))],
        out_shape=[jax.ShapeDtypeStruct((t, SWA_HEADS * hd), BF16),
                   jax.ShapeDtypeStruct((tb, SWA_HEADS * hd), F32)],
        scratch_shapes=[pltpu.VMEM((tb, hd), F32),
                        pltpu.VMEM((2 * tb, hd), F32),
                        pltpu.VMEM((2 * tb, hd), F32),
                        pltpu.VMEM((nbr, tb, hd), F32),
                        pltpu.VMEM((nbr, tb, hd), F32),
                        pltpu.VMEM((nbr, tb, hd), F32)],
        compiler_params=_params("arbitrary", "arbitrary"),
        name="swa_prompt",
    )(proj, proj, proj, proj, cos, sin)


def _swa_sample_kernel(q_ref, k_ref, v_ref, g_ref, cos_ref, sin_ref, ka_ref, kb_ref, va_ref, vb_ref,
                       y_ref, ko_ref, kn_s, vn_s, *, seq):
    pad = SAMPLE_PAD
    nh = SWA_HEADS
    hd = SWA_HD
    nr = nh * pad
    cos = cos_ref[...]
    sin = sin_ref[...]
    q8 = _rot_partial(q_ref[0] * (SWA_HD ** -0.5), cos, sin)
    k8 = _rot_partial(k_ref[0], cos, sin)
    ko_ref[0] = k8

    def head_rows(x):
        return jnp.concatenate([x[:, h * hd:(h + 1) * hd] for h in range(nh)], axis=0)

    qrows = head_rows(q8).astype(BF16)
    kn_s[...] = jnp.zeros_like(kn_s)
    vn_s[...] = jnp.zeros_like(vn_s)
    kn_s[0:nr, :] = head_rows(k8).astype(BF16)
    vn_s[0:nr, :] = head_rows(v_ref[0]).astype(BF16)
    nn = kn_s.shape[0]

    ka = ka_ref[0]
    va = va_ref[0]
    nj, nrh, _ = kb_ref.shape[1:]
    kb = kb_ref[0].reshape(nj * nrh, hd)
    vb = vb_ref[0].reshape(nj * nrh, hd)
    na = ka.shape[0]
    s_a = _dot_nt(qrows, ka.astype(BF16))
    s_b = _dot_nt(qrows, kb.astype(BF16))
    s_n = _dot_nt(qrows, kn_s[...])

    (w1, d1), (w2, d2), (w3, d3) = SWA_PATTERNS

    def rows_cols(n):
        r = lax.broadcasted_iota(jnp.int32, (nr, n), 0)
        c = lax.broadcasted_iota(jnp.int32, (nr, n), 1)
        return _div(r, pad), _mod(r, pad), c

    hq, tq, c = rows_cols(na)
    dist = na // nh + tq - _div(c, nh)
    mult_a = (dist <= w1).astype(F32) + ((_mod(dist, d2) == 0) & (dist <= w2)).astype(F32)
    mult_a = jnp.where((hq == _mod(c, nh)) & (tq < seq), mult_a, 0.0)
    hq, tq, c = rows_cols(nj * nrh)
    mult_b = jnp.where((hq == _mod(c, nh)) & (tq == _mod(_div(c, nh), nrh // nh)), 1.0, 0.0)
    hq, tq, c = rows_cols(nn)
    tk = _mod(c, pad)
    mult_n = (tk <= tq).astype(F32) + 2.0 * (tk == tq).astype(F32)
    mult_n = jnp.where((hq == _div(c, pad)) & (tq < seq) & (tk < seq), mult_n, 0.0)

    def rowmax(s, mult):
        return jnp.max(jnp.where(mult > 0, s, NEG_INF), axis=-1, keepdims=True)

    m = jnp.maximum(jnp.maximum(rowmax(s_a, mult_a), rowmax(s_b, mult_b)), rowmax(s_n, mult_n))

    def probs(s, mult):
        return jnp.where(mult > 0, jnp.exp(s - m), 0.0) * mult

    p_a, p_b, p_n = probs(s_a, mult_a), probs(s_b, mult_b), probs(s_n, mult_n)
    l = (jnp.sum(p_a, axis=-1, keepdims=True) + jnp.sum(p_b, axis=-1, keepdims=True)
         + jnp.sum(p_n, axis=-1, keepdims=True))
    l = jnp.where(l > 0, l, 1.0)
    o = (_dot(p_a.astype(BF16), va.astype(BF16)) + _dot(p_b.astype(BF16), vb.astype(BF16))
         + _dot(p_n.astype(BF16), vn_s[...])) / l
    y = jnp.concatenate([o[h * pad:(h + 1) * pad] for h in range(nh)], axis=1)
    y_ref[0] = y * _silu(g_ref[0])


def _swa_sample(proj_pad, cache_k, cache_v, cos, sin, seq):
    nseq, pad, _ = proj_pad.shape
    buf = cache_k.shape[1]
    nh, hd = SWA_HEADS, SWA_HD
    w = nh * hd
    (w1, d1), (w2, d2), (w3, d3) = SWA_PATTERNS
    assert buf == w3 and d1 == 1 and seq <= d2 and w1 <= w2 and buf % w2 == 0 and buf % d3 == 0
    assert nh * pad <= SWA_BLOCK and (seq * nh) % 8 == 0
    flat = lambda a: a.reshape(nseq, buf * nh, hd)
    strided = lambda a: a.reshape(nseq, buf // d3, d3 * nh, hd)
    col = lambda off: pl.BlockSpec((1, pad, w), lambda b: (b, 0, off // w))
    tail = pl.BlockSpec((1, w2 * nh, hd), lambda b: (b, buf // w2 - 1, 0))
    resid = pl.BlockSpec((1, buf // d3, seq * nh, hd), lambda b: (b, 0, 0, 0))
    tab = pl.BlockSpec((pad, w), lambda b: (0, 0))
    out = pl.BlockSpec((1, pad, w), lambda b: (b, 0, 0))
    return pl.pallas_call(
        functools.partial(_swa_sample_kernel, seq=seq),
        grid=(nseq,),
        in_specs=[col(E_BQ), col(E_BK), col(E_BV), col(E_BG), tab, tab, tail, resid, tail, resid],
        out_specs=[out, out],
        out_shape=[jax.ShapeDtypeStruct((nseq, pad, w), F32),
                   jax.ShapeDtypeStruct((nseq, pad, w), F32)],
        scratch_shapes=[pltpu.VMEM((SWA_BLOCK, hd), BF16), pltpu.VMEM((SWA_BLOCK, hd), BF16)],
        compiler_params=_params("arbitrary"),
        name="swa_sample",
    )(proj_pad, proj_pad, proj_pad, proj_pad, cos, sin,
      flat(cache_k), strided(cache_k), flat(cache_v), strided(cache_v))


def _mem_attn_kernel(q_ref, g_ref, k_ref, v_ref, o_ref):
    ones = jnp.ones((MEM_LEN, MEM_HD), BF16)
    for h in range(MEM_HEADS):
        hs = slice(h * MEM_HD, (h + 1) * MEM_HD)
        q = (q_ref[0, :, hs] * (MEM_HD ** -0.5 * LOG2E)).astype(BF16)
        s = _dot_nt(q, k_ref[0, :, hs].astype(BF16))
        m = jnp.max(s, axis=-1, keepdims=True)
        p = jnp.exp2(s - m).astype(BF16)
        pv = _dot(p, jnp.concatenate([v_ref[0, :, hs].astype(BF16), ones], axis=1))
        o = pv[:, :MEM_HD] / pv[:, MEM_HD:]
        o_ref[0, :, hs] = (o * _silu(g_ref[0, :, hs])).astype(o_ref.dtype)


def _mem_attn_rows_kernel(q_ref, g_ref, k_ref, v_ref, o_ref):
    pad = q_ref.shape[1]
    nh, hd = MEM_HEADS, MEM_HD
    nr = nh * pad
    q = q_ref[0] * (MEM_HD ** -0.5)
    qrows = jnp.concatenate([q[:, h * hd:(h + 1) * hd] for h in range(nh)], axis=0).astype(BF16)
    s = _dot_nt(qrows, k_ref[0].astype(BF16))
    r = lax.broadcasted_iota(jnp.int32, s.shape, 0)
    c = lax.broadcasted_iota(jnp.int32, s.shape, 1)
    same_head = _div(r, pad) == _mod(c, nh)
    m = jnp.max(jnp.where(same_head, s, NEG_INF), axis=-1, keepdims=True)
    p = jnp.where(same_head, jnp.exp(s - m), 0.0)
    l = jnp.sum(p, axis=-1, keepdims=True)
    o = _dot(p.astype(BF16), v_ref[0].astype(BF16)) / l
    y = jnp.concatenate([o[h * pad:(h + 1) * pad] for h in range(nh)], axis=1)
    o_ref[0] = y * _silu(g_ref[0])


def _mem_attn_rows(proj_pad, q_off, g_off, mem_k, mem_v, layer):
    b, pad, _ = proj_pad.shape
    w = MEM_W
    rows = MEM_LEN * MEM_HEADS
    kv = pl.BlockSpec((1, rows, MEM_HD), lambda i: (layer * b + i, 0, 0))
    return pl.pallas_call(
        _mem_attn_rows_kernel,
        grid=(b,),
        in_specs=[pl.BlockSpec((1, pad, w), lambda i: (i, 0, q_off // w)),
                  pl.BlockSpec((1, pad, w), lambda i: (i, 0, g_off // w)),
                  kv, kv],
        out_specs=pl.BlockSpec((1, pad, w), lambda i: (i, 0, 0)),
        out_shape=jax.ShapeDtypeStruct((b, pad, w), F32),
        compiler_params=_params("arbitrary"),
        name="mem_attn_rows",
    )(proj_pad, proj_pad, mem_k.reshape(-1, rows, MEM_HD), mem_v.reshape(-1, rows, MEM_HD))


def _mem_attn(proj, q_off, g_off, mem_k, k_off, mem_v, v_off, tq, out_dtype):
    b, t, _ = proj.shape
    w = MEM_W
    return pl.pallas_call(
        _mem_attn_kernel,
        grid=(b, t // tq),
        in_specs=[pl.BlockSpec((1, tq, w), lambda i, j: (i, j, q_off // w)),
                  pl.BlockSpec((1, tq, w), lambda i, j: (i, j, g_off // w)),
                  pl.BlockSpec((1, MEM_LEN, w), lambda i, j: (i, 0, k_off // w)),
                  pl.BlockSpec((1, MEM_LEN, w), lambda i, j: (i, 0, v_off // w))],
        out_specs=pl.BlockSpec((1, tq, w), lambda i, j: (i, j, 0)),
        out_shape=jax.ShapeDtypeStruct((b, t, w), out_dtype),
        compiler_params=_params("arbitrary", "arbitrary"),
        name="mem_attn",
    )(proj, proj, mem_k, mem_v)


def _gmlp_kernel(u_ref, v_ref, z_ref, ln_ref, w_ref, b_ref, y_ref, vo_ref):
    c = GMLP_CHUNK
    causal = (lax.broadcasted_iota(jnp.int32, (c, c), 0) >= lax.broadcasted_iota(jnp.int32, (c, c), 1))
    nchunks = u_ref.shape[0] // c
    keep_all = vo_ref.shape[0] == u_ref.shape[0]
    for i in range(nchunks):
        rs = slice(i * c, (i + 1) * c)
        v = jax.nn.gelu(v_ref[rs, :])
        mu = jnp.mean(v, axis=-1, keepdims=True)
        d = v - mu
        var = jnp.mean(d * d, axis=-1, keepdims=True)
        vn = d * lax.rsqrt(var + NORM_EPS) * ln_ref[...]
        if keep_all:
            vo_ref[rs, :] = vn
        elif i == nchunks - 1:
            vo_ref[...] = vn
        for g in range(GMLP_GROUPS):
            gs = slice(g * GMLP_GROUP, (g + 1) * GMLP_GROUP)
            wc = jnp.where(causal, w_ref[g], 0.0).astype(BF16)
            mixed = _dot(wc, vn[:, gs].astype(BF16)) + b_ref[:, gs]
            y_ref[rs, gs] = (jax.nn.gelu(u_ref[rs, gs]) * mixed * _silu(z_ref[rs, gs])).astype(y_ref.dtype)


def _gmlp(proj, gmlp_ln, w_mix, bias_rows, keep_all_v):
    t = proj.shape[0]
    c = GMLP_CHUNK
    w = GMLP_WIDTH
    rows = GMLP_STEP_CHUNKS * c
    assert t % rows == 0
    col = lambda off: pl.BlockSpec((rows, w), lambda n: (n, off // w))
    return pl.pallas_call(
        _gmlp_kernel,
        grid=(t // rows,),
        in_specs=[col(O_U), col(O_V), col(O_Z),
                  pl.BlockSpec((1, w), lambda n: (0, 0)),
                  pl.BlockSpec(w_mix.shape, lambda n: (0, 0, 0)),
                  pl.BlockSpec((c, w), lambda n: (0, 0))],
        out_specs=[pl.BlockSpec((rows, w), lambda n: (n, 0)),
                   pl.BlockSpec((rows, w), lambda n: (n, 0)) if keep_all_v
                   else pl.BlockSpec((c, w), lambda n: (0, 0))],
        out_shape=[jax.ShapeDtypeStruct((t, w), BF16),
                   jax.ShapeDtypeStruct((t if keep_all_v else c, w), F32)],
        compiler_params=_params("arbitrary"),
        name="gmlp",
    )(proj, proj, proj, gmlp_ln.reshape(1, w), w_mix, bias_rows)


def _out_proj_kernel(*refs, n_in):
    a_refs, w_refs = refs[:n_in], refs[n_in:2 * n_in]
    g_ref, x_ref, o_ref = refs[2 * n_in:]
    acc = _dot(a_refs[0][...].astype(BF16), w_refs[0][...])
    for a_ref, w_ref in zip(a_refs[1:], w_refs[1:]):
        acc = acc + _dot(a_ref[...].astype(BF16), w_ref[...])
    ms = jnp.mean(acc * acc, axis=-1, keepdims=True)
    o_ref[...] = x_ref[...] + acc * lax.rsqrt(ms + NORM_EPS) * g_ref[...]


def _out_proj(acts, w, g, x, tm):
    m, d = x.shape
    n_in = len(acts)
    w_specs, off = [], 0
    for a in acts:
        k = a.shape[1]
        assert off % k == 0
        w_specs.append(pl.BlockSpec((k, d), functools.partial(lambda i, blk: (blk, 0), blk=off // k)))
        off += k
    assert off == w.shape[0]
    return pl.pallas_call(
        functools.partial(_out_proj_kernel, n_in=n_in),
        grid=(m // tm,),
        in_specs=([pl.BlockSpec((tm, a.shape[1]), lambda i: (i, 0)) for a in acts]
                  + w_specs
                  + [pl.BlockSpec((1, d), lambda i: (0, 0)),
                     pl.BlockSpec((tm, d), lambda i: (i, 0))]),
        out_specs=pl.BlockSpec((tm, d), lambda i: (i, 0)),
        out_shape=jax.ShapeDtypeStruct((m, d), F32),
        compiler_params=_params("arbitrary"),
        name="out_proj",
    )(*acts, *([w] * n_in), g.reshape(1, d), x)


def _out_proj_cast_kernel(a_ref, w_ref, g_ref, x_ref, o_ref, wb_ref, acc_ref):
    k = pl.program_id(0)

    @pl.when(k == 0)
    def _():
        acc_ref[...] = jnp.zeros_like(acc_ref)

    wb = w_ref[...].astype(BF16)
    wb_ref[...] = wb
    acc_ref[...] += _dot(a_ref[...].astype(BF16), wb)

    @pl.when(k == pl.num_programs(0) - 1)
    def _():
        acc = acc_ref[...]
        ms = jnp.mean(acc * acc, axis=-1, keepdims=True)
        o_ref[...] = x_ref[...] + acc * lax.rsqrt(ms + NORM_EPS) * g_ref[...]


def _out_proj_cast(a, w, g, x, tk):
    m, d = x.shape
    kdim = w.shape[0]
    assert kdim % tk == 0 and a.shape == (m, kdim)
    return pl.pallas_call(
        _out_proj_cast_kernel,
        grid=(kdim // tk,),
        in_specs=[pl.BlockSpec((m, tk), lambda k: (0, k)),
                  pl.BlockSpec((tk, d), lambda k: (k, 0)),
                  pl.BlockSpec((1, d), lambda k: (0, 0)),
                  pl.BlockSpec((m, d), lambda k: (0, 0))],
        out_specs=[pl.BlockSpec((m, d), lambda k: (0, 0)),
                   pl.BlockSpec((tk, d), lambda k: (k, 0))],
        out_shape=[jax.ShapeDtypeStruct((m, d), F32),
                   jax.ShapeDtypeStruct((kdim, d), BF16)],
        scratch_shapes=[pltpu.VMEM((m, d), F32)],
        compiler_params=_params("arbitrary"),
        name="out_proj_cast",
    )(a, w, g.reshape(1, d), x)


def _pad_steps(a, pad):
    nseq, seq, n = a.shape
    return jnp.pad(a, ((0, 0), (0, pad - seq), (0, 0))).reshape(nseq * pad, n)


def kernel(x_prompt, x_sample, state_ret, cache_swa_k, cache_swa_v, cache_mem_k, cache_mem_v, mem_prompt,
           pre_norm, post_norm, mem_norm, w_mem_k, w_mem_v, w_in_even, ret_gn, w_out_even,
           w_in_odd, gmlp_ln, w_spatial, b_spatial, w_out_odd):
    bp, tp, d = x_prompt.shape
    nseq, seq, _ = x_sample.shape
    pad = SAMPLE_PAD
    rows_s = nseq * pad
    c = GMLP_CHUNK
    assert bp == 1 and seq <= pad and rows_s % c == 0 and c % pad == 0
    hp = x_prompt.reshape(tp, d)
    hs = _pad_steps(x_sample, pad)
    mem = mem_prompt.reshape(MEM_LEN, d)
    tm_p = 1024 if tp % 1024 == 0 else 512
    unpad = lambda a: a.reshape(nseq, pad, -1)[:, :seq]

    pos_s = PAST_LEN + jnp.arange(seq, dtype=jnp.int32)
    pos_pad = jnp.concatenate([pos_s, jnp.zeros((pad - seq,), jnp.int32)])
    ret_cos_p, ret_sin_p = _rot_tables_range(tp, RET_DK, RET_THETA, RET_DK)
    ret_cos_s, ret_sin_s = _rot_tables(jnp.tile(pos_pad, nseq), RET_DK, RET_THETA, RET_DK)
    swa_cos_p, swa_sin_p = _rot_tables_range(tp, ROPE_DIMS, ROPE_THETA, SWA_HD)
    swa_cos_s, swa_sin_s = _rot_tables(pos_pad, ROPE_DIMS, ROPE_THETA, SWA_HD)
    swa_cos_s = jnp.tile(swa_cos_s, (1, SWA_HEADS))
    swa_sin_s = jnp.tile(swa_sin_s, (1, SWA_HEADS))

    w_mem0 = jnp.concatenate([w_mem_k[0], w_mem_v[0]], axis=1).astype(BF16)
    memkv0 = _norm_matmul(mem, mem_norm[0], w_mem0, MEM_LEN, 2 * MEM_W)

    proj_s, w_in0 = _norm_matmul_cast(hs, pre_norm[0], w_in_even[0], 1024)
    proj_s3 = proj_s.reshape(nseq, pad, EVEN_IN)
    ya_s, st_s = _ret_sample(proj_s, state_ret[0], ret_cos_s, ret_sin_s, ret_gn[0], nseq, seq, pad)
    yb_s, sk = _swa_sample(proj_s3, cache_swa_k[0], cache_swa_v[0], swa_cos_s, swa_sin_s, seq)
    ym_s = _mem_attn_rows(proj_s3, E_MQ, E_MG, cache_mem_k, cache_mem_v, 0)
    y_s = jnp.concatenate([ya_s, yb_s.reshape(rows_s, -1), ym_s.reshape(rows_s, -1)], axis=1)
    hs1, w_out0 = _out_proj_cast(y_s, w_out_even[0], post_norm[0], hs, 512)
    sk = sk[:, :seq].reshape(nseq, seq, SWA_HEADS, SWA_HD)
    sv = unpad(proj_s[:, E_BV:E_BV + SWA_HEADS * SWA_HD]).reshape(nseq, seq, SWA_HEADS, SWA_HD)

    proj_p = _norm_matmul(hp, pre_norm[0], w_in0, tm_p, 1024)
    ya_p, st_p = _ret_prompt(proj_p, ret_cos_p, ret_sin_p, ret_gn[0])
    yb_p, pk = _swa_prompt(proj_p, swa_cos_p, swa_sin_p)
    ym_p = _mem_attn(proj_p[None], E_MQ, E_MG, memkv0[None], 0, memkv0[None], MEM_W, 1024, BF16)[0]
    hp1 = _out_proj([ya_p, yb_p, ym_p], w_out0, post_norm[0], hp, 256)
    buf_p = min(SWA_MAX_WINDOW, tp)
    pv = proj_p[tp - buf_p:, E_BV:E_BV + SWA_HEADS * SWA_HD]

    w_mem1 = jnp.concatenate([w_mem_k[1], w_mem_v[1]], axis=1).astype(BF16)
    memkv1 = _norm_matmul(mem, mem_norm[1], w_mem1, MEM_LEN, 2 * MEM_W)
    bias_p = jnp.repeat(b_spatial[0].T, GMLP_GROUP, axis=1)
    blockdiag = lambda wg: jnp.kron(jnp.eye(c // pad, dtype=F32),
                                    jnp.pad(wg[:seq, :seq], ((0, pad - seq), (0, pad - seq))))
    w_mix_s = jax.vmap(blockdiag)(w_spatial[0])
    bias_s = jnp.tile(jnp.pad(bias_p[:seq], ((0, pad - seq), (0, 0))), (c // pad, 1))

    proj_s1, w_in1 = _norm_matmul_cast(hs1, pre_norm[1], w_in_odd[0], 1024)
    yc_s, gv_s = _gmlp(proj_s1, gmlp_ln[0], w_mix_s, bias_s, True)
    ym_s1 = _mem_attn_rows(proj_s1.reshape(nseq, pad, ODD_IN), O_MQ, O_MG, cache_mem_k, cache_mem_v, 1)
    y_s1 = jnp.concatenate([yc_s.astype(F32), ym_s1.reshape(rows_s, -1)], axis=1)
    hs2, w_out1 = _out_proj_cast(y_s1, w_out_odd[0], post_norm[1], hs1, 512)

    proj_p1 = _norm_matmul(hp1, pre_norm[1], w_in1, tm_p, 1024)
    yc_p, gv_p = _gmlp(proj_p1, gmlp_ln[0], w_spatial[0], bias_p, False)
    ym_p1 = _mem_attn(proj_p1[None], O_MQ, O_MG, memkv1[None], 0, memkv1[None], MEM_W, 1024, BF16)[0]
    hp2 = _out_proj([yc_p, ym_p1], w_out1, post_norm[1], hp1, 256)

    memkv = jnp.stack([memkv0, memkv1])
    p_mk = memkv[:, :, :MEM_W].reshape(2, bp, MEM_LEN, MEM_HEADS, MEM_HD)
    p_mv = memkv[:, :, MEM_W:].reshape(2, bp, MEM_LEN, MEM_HEADS, MEM_HD)
    return (hp2.reshape(bp, tp, d), unpad(hs2),
            st_p[None, None],
            pk.reshape(1, bp, buf_p, SWA_HEADS, SWA_HD), pv.reshape(1, bp, buf_p, SWA_HEADS, SWA_HD),
            p_mk, p_mv,
            gv_p[None, None],
            st_s[None],
            sk[None], sv[None],
            unpad(gv_s)[None])
```

```python
import functools
import math

import jax
import jax.numpy as jnp
from jax import lax
from jax.experimental import pallas as pl
from jax.experimental.pallas import tpu as pltpu

F32 = jnp.float32
BF16 = jnp.bfloat16

D_MODEL = 2048
PAST_LEN = 8192
RET_HEADS = 8
RET_DK = 128
RET_DV = 256
RET_CHUNK = 128
RET_THETA = 10000.0
SWA_HEADS = 8
SWA_HD = 128
SWA_PATTERNS = ((128, 1), (512, 4), (2048, 16))
SWA_MAX_WINDOW = 2048
SWA_BLOCK = 128
ROPE_THETA = 500000.0
ROPE_DIMS = SWA_HD // 4
GMLP_CHUNK = 128
GMLP_WIDTH = D_MODEL
GMLP_GROUP = 128
GMLP_GROUPS = GMLP_WIDTH // GMLP_GROUP
MEM_LEN = 256
MEM_HEADS = 4
MEM_HD = 128
MEM_W = MEM_HEADS * MEM_HD
NORM_EPS = 1e-6
NEG_INF = -1e30
LOG2E = math.log2(math.e)

E_AQ = 0
E_AK = E_AQ + RET_HEADS * RET_DK
E_AV = E_AK + RET_HEADS * RET_DK
E_AG = E_AV + RET_HEADS * RET_DV
E_BQ = E_AG + RET_HEADS * RET_DV
E_BK = E_BQ + SWA_HEADS * SWA_HD
E_BV = E_BK + SWA_HEADS * SWA_HD
E_BG = E_BV + SWA_HEADS * SWA_HD
E_MQ = E_BG + SWA_HEADS * SWA_HD
E_MG = E_MQ + MEM_W
EVEN_IN = E_MG + MEM_W
O_U = 0
O_V = O_U + GMLP_WIDTH
O_Z = O_V + GMLP_WIDTH
O_MQ = O_Z + GMLP_WIDTH
O_MG = O_MQ + MEM_W
ODD_IN = O_MG + MEM_W

SWA_TILE = SWA_MAX_WINDOW
SAMPLE_PAD = 8
GMLP_STEP_CHUNKS = 2
RET_STEP_CHUNKS = 2
VMEM_LIMIT = 48 * 1024 * 1024


def _params(*sem):
    return pltpu.CompilerParams(dimension_semantics=sem, vmem_limit_bytes=VMEM_LIMIT)


def _dot(a, b):
    return jnp.dot(a, b, preferred_element_type=F32)


def _dot_nt(a, b):
    return lax.dot_general(a, b, (((1,), (1,)), ((), ())), preferred_element_type=F32)


def _silu(x):
    return x * jax.nn.sigmoid(x)


def _gelu(x):
    k = 2.0 * math.sqrt(2.0 / math.pi)
    return x * jax.nn.sigmoid(x * (k + (k * 0.044715) * (x * x)))


def _div(x, n):
    assert n & (n - 1) == 0
    return jnp.right_shift(x, n.bit_length() - 1)


def _mod(x, n):
    assert n & (n - 1) == 0
    return jnp.bitwise_and(x, n - 1)


def _rot_tables(pos, n_rot, theta, width):
    half = n_rot // 2
    inv = 1.0 / (theta ** (jnp.arange(half, dtype=F32) / half))
    ang = pos.astype(F32)[:, None] * inv[None, :]
    return _rot_assemble(jnp.cos(ang), jnp.sin(ang), width)


def _rot_tables_range(n, n_rot, theta, width, blk=128):
    assert n % blk == 0
    half = n_rot // 2
    inv = 1.0 / (theta ** (jnp.arange(half, dtype=F32) / half))
    hi = (jnp.arange(n // blk, dtype=F32) * blk)[:, None] * inv[None, :]
    lo = jnp.arange(blk, dtype=F32)[:, None] * inv[None, :]
    ch, sh, cl, sl = jnp.cos(hi)[:, None], jnp.sin(hi)[:, None], jnp.cos(lo)[None], jnp.sin(lo)[None]
    return _rot_assemble((ch * cl - sh * sl).reshape(n, half), (sh * cl + ch * sl).reshape(n, half), width)


def _rot_assemble(cos, sin, width):
    t, half = cos.shape
    cos_t = jnp.concatenate([cos, cos, jnp.ones((t, width - 2 * half), F32)], axis=-1)
    sin_t = jnp.concatenate([-sin, sin, jnp.zeros((t, width - 2 * half), F32)], axis=-1)
    return cos_t, sin_t


def _rot_full(x, cos, sin):
    return x * cos + pltpu.roll(x, RET_DK // 2, 1) * sin


def _rot_partial(x, cos, sin):
    half = ROPE_DIMS // 2
    lane = _mod(lax.broadcasted_iota(jnp.int32, x.shape, 1), SWA_HD)
    width = x.shape[1]
    partner = jnp.where(lane < half, pltpu.roll(x, width - half, 1), pltpu.roll(x, half, 1))
    return x * cos + partner * sin


def _norm_matmul_kernel(x_ref, g_ref, w_ref, o_ref, xn_ref):
    @pl.when(pl.program_id(1) == 0)
    def _():
        x = x_ref[...]
        ms = jnp.mean(x * x, axis=-1, keepdims=True)
        xn_ref[...] = (x * lax.rsqrt(ms + NORM_EPS) * g_ref[...]).astype(BF16)

    o_ref[...] = _dot(xn_ref[...], w_ref[...]).astype(o_ref.dtype)


def _norm_matmul(x, g, w, tm, tn):
    m, d = x.shape
    n = w.shape[1]
    return pl.pallas_call(
        _norm_matmul_kernel,
        grid=(m // tm, n // tn),
        in_specs=[pl.BlockSpec((tm, d), lambda i, j: (i, 0)),
                  pl.BlockSpec((1, d), lambda i, j: (0, 0)),
                  pl.BlockSpec((d, tn), lambda i, j: (0, j))],
        out_specs=pl.BlockSpec((tm, tn), lambda i, j: (i, j)),
        out_shape=jax.ShapeDtypeStruct((m, n), F32),
        scratch_shapes=[pltpu.VMEM((tm, d), BF16)],
        compiler_params=_params("arbitrary", "arbitrary"),
        name="norm_matmul",
    )(x, g.reshape(1, d), w)


def _norm_matmul_cast_kernel(x_ref, g_ref, w_ref, o_ref, wb_ref, xn_ref):
    @pl.when(pl.program_id(0) == 0)
    def _():
        x = x_ref[...]
        ms = jnp.mean(x * x, axis=-1, keepdims=True)
        xn_ref[...] = (x * lax.rsqrt(ms + NORM_EPS) * g_ref[...]).astype(BF16)

    wb = w_ref[...].astype(BF16)
    wb_ref[...] = wb
    o_ref[...] = _dot(xn_ref[...], wb).astype(o_ref.dtype)


def _norm_matmul_cast(x, g, w, tn):
    m, d = x.shape
    n = w.shape[1]
    return pl.pallas_call(
        _norm_matmul_cast_kernel,
        grid=(n // tn,),
        in_specs=[pl.BlockSpec((m, d), lambda j: (0, 0)),
                  pl.BlockSpec((1, d), lambda j: (0, 0)),
                  pl.BlockSpec((d, tn), lambda j: (0, j))],
        out_specs=[pl.BlockSpec((m, tn), lambda j: (0, j)),
                   pl.BlockSpec((d, tn), lambda j: (0, j))],
        out_shape=[jax.ShapeDtypeStruct((m, n), F32),
                   jax.ShapeDtypeStruct((d, n), BF16)],
        scratch_shapes=[pltpu.VMEM((m, d), BF16)],
        compiler_params=_params("arbitrary"),
        name="norm_matmul_cast",
    )(x, g.reshape(1, d), w)


def _ret_tables(c, reps, pad=None):
    pad = c if pad is None else pad
    rows = pad * reps
    lg = jnp.log1p(-jnp.exp2(-5.0 - jnp.arange(RET_HEADS, dtype=F32)))
    row = jnp.arange(rows, dtype=jnp.int32)
    step = (row % pad).astype(F32)
    live = (row % pad) < c
    rel = step[:, None] - step[None, :]
    keep = ((row // pad)[:, None] == (row // pad)[None, :]) & (rel >= 0) & live[:, None] & live[None, :]
    decay = jnp.where(keep[None], jnp.exp(lg[:, None, None] * jnp.maximum(rel, 0.0)[None]), 0.0)
    q_decay = jnp.where(live[None], jnp.exp(lg[:, None] * (step[None, :] + 1.0)), 0.0)
    k_decay = jnp.where(live[None], jnp.exp(lg[:, None] * (c - 1.0 - step[None, :])), 0.0)
    chunk_decay = jnp.exp(lg * c)
    qd = jnp.broadcast_to(q_decay[:, :, None], (RET_HEADS, rows, RET_DK))
    kd = jnp.broadcast_to(k_decay[:, :, None], (RET_HEADS, rows, RET_DK))
    cd = jnp.broadcast_to(chunk_decay[:, None, None], (RET_HEADS, 1, RET_DV))
    return decay, qd, kd, cd


def _group_norm_gate(o, gn, g):
    mu = jnp.mean(o, axis=-1, keepdims=True)
    d = o - mu
    var = jnp.mean(d * d, axis=-1, keepdims=True)
    return d * lax.rsqrt(var + NORM_EPS) * gn * _silu(g)


def _ret_prompt_kernel(q_ref, k_ref, v_ref, g_ref, cos_ref, sin_ref, dec_ref, qd_ref, kd_ref, cd_ref, gn_ref,
                       y_ref, st_ref):
    @pl.when(pl.program_id(0) == 0)
    def _():
        st_ref[...] = jnp.zeros_like(st_ref)

    for i in range(q_ref.shape[0] // RET_CHUNK):
        rs = slice(i * RET_CHUNK, (i + 1) * RET_CHUNK)
        cos = cos_ref[rs, :]
        sin = sin_ref[rs, :]
        for h in range(RET_HEADS):
            ks = slice(h * RET_DK, (h + 1) * RET_DK)
            vs = slice(h * RET_DV, (h + 1) * RET_DV)
            q = _rot_full(q_ref[rs, ks], cos, sin)
            k = _rot_full(k_ref[rs, ks] * (RET_DK ** -0.5), cos, sin)
            v = v_ref[rs, vs].astype(BF16)
            st = st_ref[h]
            s = _dot_nt(q.astype(BF16), k.astype(BF16)) * dec_ref[h]
            lhs = jnp.concatenate([s.astype(BF16), (q * qd_ref[h]).astype(BF16)], axis=1)
            o = _dot(lhs, jnp.concatenate([v, st.astype(BF16)], axis=0))
            st_ref[h] = cd_ref[h] * st + _dot((k * kd_ref[h]).T.astype(BF16), v)
            y_ref[rs, vs] = _group_norm_gate(o, gn_ref[:, vs], g_ref[rs, vs]).astype(y_ref.dtype)


def _ret_prompt(proj, cos, sin, ret_gn):
    t = proj.shape[0]
    dec, qd, kd, cd = _ret_tables(RET_CHUNK, 1)
    c = RET_STEP_CHUNKS * RET_CHUNK
    assert t % c == 0
    qw = RET_HEADS * RET_DK
    vw = RET_HEADS * RET_DV
    full3 = lambda n: (0, 0, 0)
    return pl.pallas_call(
        _ret_prompt_kernel,
        grid=(t // c,),
        in_specs=[pl.BlockSpec((c, qw), lambda n: (n, E_AQ // qw)),
                  pl.BlockSpec((c, qw), lambda n: (n, E_AK // qw)),
                  pl.BlockSpec((c, vw), lambda n: (n, E_AV // vw)),
                  pl.BlockSpec((c, vw), lambda n: (n, E_AG // vw)),
                  pl.BlockSpec((c, RET_DK), lambda n: (n, 0)),
                  pl.BlockSpec((c, RET_DK), lambda n: (n, 0)),
                  pl.BlockSpec(dec.shape, full3),
                  pl.BlockSpec(qd.shape, full3),
                  pl.BlockSpec(kd.shape, full3),
                  pl.BlockSpec(cd.shape, full3),
                  pl.BlockSpec((1, vw), lambda n: (0, 0))],
        out_specs=[pl.BlockSpec((c, vw), lambda n: (n, 0)),
                   pl.BlockSpec((RET_HEADS, RET_DK, RET_DV), full3)],
        out_shape=[jax.ShapeDtypeStruct((t, vw), BF16),
                   jax.ShapeDtypeStruct((RET_HEADS, RET_DK, RET_DV), F32)],
        compiler_params=_params("arbitrary"),
        name="ret_prompt",
    )(proj, proj, proj, proj, cos, sin, dec, qd, kd, cd, ret_gn.reshape(1, vw))


def _ret_sample_kernel(q_ref, k_ref, v_ref, g_ref, cos_ref, sin_ref, dec_ref, qd_ref, kd_ref, cd_ref, gn_ref,
                       st_ref, y_ref, sto_ref, cross_ref, *, pad):
    b = pl.program_id(0)
    nb = pl.num_programs(0)
    win = RET_CHUNK

    @pl.when(b == 0)
    def _():
        cross_ref[...] = jnp.zeros_like(cross_ref)

    w0 = pl.multiple_of(jnp.right_shift(b * pad, win.bit_length() - 1) * win, win)
    wrows = pl.ds(w0, win)
    cos = cos_ref[wrows, :]
    sin = sin_ref[wrows, :]
    rows = lax.broadcasted_iota(jnp.int32, (win, RET_DK), 0) + w0
    mine = (rows >= b * pad) & (rows < (b + 1) * pad)
    for h in range(RET_HEADS):
        ks = slice(h * RET_DK, (h + 1) * RET_DK)
        vs = slice(h * RET_DV, (h + 1) * RET_DV)
        q = _rot_full(q_ref[wrows, ks], cos, sin)
        k = _rot_full(k_ref[wrows, ks] * (RET_DK ** -0.5), cos, sin)
        v = v_ref[wrows, vs].astype(BF16)
        st = st_ref[0, h]
        qm = jnp.where(mine, q * qd_ref[h, wrows, :], 0.0)
        km = jnp.where(mine, k * kd_ref[h, wrows, :], 0.0)
        cross_ref[wrows, vs] += _dot(qm.astype(BF16), st.astype(BF16))
        sto_ref[0, h] = cd_ref[h] * st + _dot(km.T.astype(BF16), v)

    @pl.when(b == nb - 1)
    def _():
        cos_all = cos_ref[...]
        sin_all = sin_ref[...]
        for h in range(RET_HEADS):
            ks = slice(h * RET_DK, (h + 1) * RET_DK)
            vs = slice(h * RET_DV, (h + 1) * RET_DV)
            q = _rot_full(q_ref[:, ks], cos_all, sin_all)
            k = _rot_full(k_ref[:, ks] * (RET_DK ** -0.5), cos_all, sin_all)
            v = v_ref[:, vs].astype(BF16)
            s = _dot_nt(q.astype(BF16), k.astype(BF16)) * dec_ref[h]
            o = _dot(s.astype(BF16), v) + cross_ref[:, vs]
            y_ref[:, vs] = _group_norm_gate(o, gn_ref[:, vs], g_ref[:, vs]).astype(y_ref.dtype)


def _ret_sample(proj, state, cos, sin, ret_gn, nseq, seq, pad):
    rows = nseq * pad
    c = math.gcd(seq, RET_CHUNK)
    assert c == seq, "sample chunk must cover the new tokens"
    assert rows % RET_CHUNK == 0 and RET_CHUNK % pad == 0
    dec, qd, kd, cd = _ret_tables(c, nseq, pad)
    qw = RET_HEADS * RET_DK
    vw = RET_HEADS * RET_DV
    full3 = lambda b: (0, 0, 0)
    st_spec = pl.BlockSpec((1, RET_HEADS, RET_DK, RET_DV), lambda b: (b, 0, 0, 0))
    return pl.pallas_call(
        functools.partial(_ret_sample_kernel, pad=pad),
        grid=(nseq,),
        in_specs=[pl.BlockSpec((rows, qw), lambda b: (0, E_AQ // qw)),
                  pl.BlockSpec((rows, qw), lambda b: (0, E_AK // qw)),
                  pl.BlockSpec((rows, vw), lambda b: (0, E_AV // vw)),
                  pl.BlockSpec((rows, vw), lambda b: (0, E_AG // vw)),
                  pl.BlockSpec((rows, RET_DK), lambda b: (0, 0)),
                  pl.BlockSpec((rows, RET_DK), lambda b: (0, 0)),
                  pl.BlockSpec(dec.shape, full3),
                  pl.BlockSpec(qd.shape, full3),
                  pl.BlockSpec(kd.shape, full3),
                  pl.BlockSpec(cd.shape, full3),
                  pl.BlockSpec((1, vw), lambda b: (0, 0)),
                  st_spec],
        out_specs=[pl.BlockSpec((rows, vw), lambda b: (0, 0)), st_spec],
        out_shape=[jax.ShapeDtypeStruct((rows, vw), F32),
                   jax.ShapeDtypeStruct(state.shape, F32)],
        scratch_shapes=[pltpu.VMEM((rows, vw), F32)],
        compiler_params=_params("arbitrary"),
        name="ret_sample",
    )(proj, proj, proj, proj, cos, sin, dec, qd, kd, cd, ret_gn.reshape(1, vw), state)


def _swa_prompt_kernel(q_ref, kc_ref, vc_ref, g_ref, cos_ref, sin_ref,
                       y_ref, ko_ref, q_s, k_s, v_s, m_s, l_s, acc_s):
    t = pl.program_id(1)
    nt = pl.num_programs(1)
    tb = SWA_TILE
    blk = SWA_BLOCK
    hd = SWA_HD
    cos = cos_ref[...]
    sin = sin_ref[...]

    @pl.when(t == 0)
    def _():
        k_s[0:tb, :] = jnp.zeros((tb, hd), F32)
        v_s[0:tb, :] = jnp.zeros((tb, hd), F32)

    @pl.when(t > 0)
    def _():
        k_s[0:tb, :] = k_s[tb:2 * tb, :]
        v_s[0:tb, :] = v_s[tb:2 * tb, :]

    q_s[...] = _rot_partial(q_ref[...] * (SWA_HD ** -0.5 * LOG2E), cos, sin)
    kr = _rot_partial(kc_ref[...], cos, sin)
    k_s[tb:2 * tb, :] = kr
    v_s[tb:2 * tb, :] = vc_ref[...]

    @pl.when(t == nt - 1)
    def _():
        ko_ref[...] = kr

    qi = lax.broadcasted_iota(jnp.int32, (blk, 2 * blk), 0)
    kj = lax.broadcasted_iota(jnp.int32, (blk, 2 * blk), 1)
    band = (kj >= qi) & (kj <= qi + blk)
    band_first = band & ((kj >= blk) | (t > 0))
    ones = jnp.ones((2 * blk, hd), BF16)

    def ds(start, size, stride):
        return pl.ds(start, size) if stride == 1 else pl.ds(start, size, stride=stride)

    for bi, (window, r) in enumerate(SWA_PATTERNS):
        assert window // r == blk
        for idx in range(tb // blk):
            c, n = idx % r, idx // r
            qs = n * blk * r + c
            ks = qs + tb - blk * r
            qrows = ds(qs, blk, r)
            krows = ds(ks, 2 * blk, r)
            valid = band_first if n == 0 else band
            s = _dot_nt(q_s[qrows, :].astype(BF16), k_s[krows, :].astype(BF16))
            s = jnp.where(valid, s, NEG_INF)
            m = jnp.max(s, axis=-1, keepdims=True)
            p = jnp.exp2(s - m).astype(BF16)
            pv = _dot(p, jnp.concatenate([v_s[krows, :].astype(BF16), ones], axis=1))
            m_s[bi, qrows, :] = jnp.broadcast_to(m, (blk, hd))
            l_s[bi, qrows, :] = pv[:, hd:]
            acc_s[bi, qrows, :] = pv[:, :hd]

    nbr = len(SWA_PATTERNS)
    ms = [m_s[i] for i in range(nbr)]
    mmax = functools.reduce(jnp.maximum, ms)
    wts = [jnp.exp2(mi - mmax) for mi in ms]
    num = sum(wts[i] * acc_s[i] for i in range(nbr))
    den = sum(wts[i] * l_s[i] for i in range(nbr))
    y_ref[...] = (num / den * _silu(g_ref[...])).astype(y_ref.dtype)


def _swa_prompt(proj, cos, sin):
    t = proj.shape[0]
    tb = SWA_TILE
    assert t % tb == 0
    hd = SWA_HD
    nbr = len(SWA_PATTERNS)
    cur = lambda off: pl.BlockSpec((tb, hd), lambda h, i: (i, off // hd + h))
    return pl.pallas_call(
        _swa_prompt_kernel,
        grid=(SWA_HEADS, t // tb),
        in_specs=[cur(E_BQ), cur(E_BK), cur(E_BV), cur(E_BG),
                  pl.BlockSpec((tb, hd), lambda h, i: (i, 0)),
                  pl.BlockSpec((tb, hd), lambda h, i: (i, 0))],
        out_specs=[pl.BlockSpec((tb, hd), lambda h, i: (i, h)),
                   pl.BlockSpec((tb, hd), lambda h, i: (0, h))],
        out_shape=[jax.ShapeDtypeStruct((t, SWA_HEADS * hd), BF16),
                   jax.ShapeDtypeStruct((tb, SWA_HEADS * hd), F32)],
        scratch_shapes=[pltpu.VMEM((tb, hd), F32),
                        pltpu.VMEM((2 * tb, hd), F32),
                        pltpu.VMEM((2 * tb, hd), F32),
                        pltpu.VMEM((nbr, tb, hd), F32),
                        pltpu.VMEM((nbr, tb, hd), F32),
                        pltpu.VMEM((nbr, tb, hd), F32)],
        compiler_params=_params("arbitrary", "arbitrary"),
        name="swa_prompt",
    )(proj, proj, proj, proj, cos, sin)


def _swa_sample_kernel(q_ref, k_ref, v_ref, g_ref, cos_ref, sin_ref, ka_ref, kb_ref, va_ref, vb_ref,
                       y_ref, ko_ref, kn_s, vn_s, *, seq):
    pad = SAMPLE_PAD
    nh = SWA_HEADS
    hd = SWA_HD
    nr = nh * pad
    cos = cos_ref[...]
    sin = sin_ref[...]
    q8 = _rot_partial(q_ref[0] * (SWA_HD ** -0.5), cos, sin)
    k8 = _rot_partial(k_ref[0], cos, sin)
    ko_ref[0] = k8

    def head_rows(x):
        return jnp.concatenate([x[:, h * hd:(h + 1) * hd] for h in range(nh)], axis=0)

    qrows = head_rows(q8).astype(BF16)
    kn_s[...] = jnp.zeros_like(kn_s)
    vn_s[...] = jnp.zeros_like(vn_s)
    kn_s[0:nr, :] = head_rows(k8).astype(BF16)
    vn_s[0:nr, :] = head_rows(v_ref[0]).astype(BF16)
    nn = kn_s.shape[0]

    ka = ka_ref[0]
    va = va_ref[0]
    nj, nrh, _ = kb_ref.shape[1:]
    kb = kb_ref[0].reshape(nj * nrh, hd)
    vb = vb_ref[0].reshape(nj * nrh, hd)
    na = ka.shape[0]
    s_a = _dot_nt(qrows, ka.astype(BF16))
    s_b = _dot_nt(qrows, kb.astype(BF16))
    s_n = _dot_nt(qrows, kn_s[...])

    (w1, d1), (w2, d2), (w3, d3) = SWA_PATTERNS

    def rows_cols(n):
        r = lax.broadcasted_iota(jnp.int32, (nr, n), 0)
        c = lax.broadcasted_iota(jnp.int32, (nr, n), 1)
        return _div(r, pad), _mod(r, pad), c

    hq, tq, c = rows_cols(na)
    dist = na // nh + tq - _div(c, nh)
    mult_a = ((dist <= w1).astype(F32) + ((_mod(dist, d2) == 0) & (dist <= w2)).astype(F32)
              + ((_mod(dist, d3) == 0) & (dist <= w3)).astype(F32))
    mult_a = jnp.where((hq == _mod(c, nh)) & (tq < seq), mult_a, 0.0)
    hq, tq, c = rows_cols(nj * nrh)
    mult_b = jnp.where((hq == _mod(c, nh)) & (tq == _mod(_div(c, nh), nrh // nh)), 1.0, 0.0)
    hq, tq, c = rows_cols(nn)
    tk = _mod(c, pad)
    mult_n = (tk <= tq).astype(F32) + 2.0 * (tk == tq).astype(F32)
    mult_n = jnp.where((hq == _div(c, pad)) & (tq < seq) & (tk < seq), mult_n, 0.0)

    def rowmax(s, mult):
        return jnp.max(jnp.where(mult > 0, s, NEG_INF), axis=-1, keepdims=True)

    m = jnp.maximum(jnp.maximum(rowmax(s_a, mult_a), rowmax(s_b, mult_b)), rowmax(s_n, mult_n))

    def probs(s, mult):
        return jnp.where(mult > 0, jnp.exp(s - m), 0.0) * mult

    p_a, p_b, p_n = probs(s_a, mult_a), probs(s_b, mult_b), probs(s_n, mult_n)
    l = (jnp.sum(p_a, axis=-1, keepdims=True) + jnp.sum(p_b, axis=-1, keepdims=True)
         + jnp.sum(p_n, axis=-1, keepdims=True))
    l = jnp.where(l > 0, l, 1.0)
    o = (_dot(p_a.astype(BF16), va.astype(BF16)) + _dot(p_b.astype(BF16), vb.astype(BF16))
         + _dot(p_n.astype(BF16), vn_s[...])) / l
    y = jnp.concatenate([o[h * pad:(h + 1) * pad] for h in range(nh)], axis=1)
    y_ref[0] = y * _silu(g_ref[0])


def _swa_sample(proj_pad, cache_k, cache_v, cos, sin, seq):
    nseq, pad, _ = proj_pad.shape
    buf = cache_k.shape[1]
    nh, hd = SWA_HEADS, SWA_HD
    w = nh * hd
    (w1, d1), (w2, d2), (w3, d3) = SWA_PATTERNS
    assert buf == w3 and d1 == 1 and seq <= d2 and w1 <= w2 and buf % w2 == 0 and buf % d3 == 0
    assert nh * pad <= SWA_BLOCK and (seq * nh) % 8 == 0
    flat = lambda a: a.reshape(nseq, buf * nh, hd)
    strided = lambda a: a.reshape(nseq, buf // d3, d3 * nh, hd)
    col = lambda off: pl.BlockSpec((1, pad, w), lambda b: (b, 0, off // w))
    tail = pl.BlockSpec((1, w2 * nh, hd), lambda b: (b, buf // w2 - 1, 0))
    resid = pl.BlockSpec((1, (buf - w2) // d3, seq * nh, hd), lambda b: (b, 0, 0, 0))
    tab = pl.BlockSpec((pad, w), lambda b: (0, 0))
    out = pl.BlockSpec((1, pad, w), lambda b: (b, 0, 0))
    return pl.pallas_call(
        functools.partial(_swa_sample_kernel, seq=seq),
        grid=(nseq,),
        in_specs=[col(E_BQ), col(E_BK), col(E_BV), col(E_BG), tab, tab, tail, resid, tail, resid],
        out_specs=[out, out],
        out_shape=[jax.ShapeDtypeStruct((nseq, pad, w), F32),
                   jax.ShapeDtypeStruct((nseq, pad, w), F32)],
        scratch_shapes=[pltpu.VMEM((SWA_BLOCK, hd), BF16), pltpu.VMEM((SWA_BLOCK, hd), BF16)],
        compiler_params=_params("arbitrary"),
        name="swa_sample",
    )(proj_pad, proj_pad, proj_pad, proj_pad, cos, sin,
      flat(cache_k), strided(cache_k), flat(cache_v), strided(cache_v))


def _mem_attn_kernel(q_ref, g_ref, k_ref, v_ref, o_ref):
    ones = jnp.ones((MEM_LEN, MEM_HD), BF16)
    for h in range(MEM_HEADS):
        hs = slice(h * MEM_HD, (h + 1) * MEM_HD)
        q = (q_ref[0, :, hs] * (MEM_HD ** -0.5 * LOG2E)).astype(BF16)
        s = _dot_nt(q, k_ref[0, :, hs].astype(BF16))
        m = jnp.max(s, axis=-1, keepdims=True)
        p = jnp.exp2(s - m).astype(BF16)
        pv = _dot(p, jnp.concatenate([v_ref[0, :, hs].astype(BF16), ones], axis=1))
        o = pv[:, :MEM_HD] / pv[:, MEM_HD:]
        o_ref[0, :, hs] = (o * _silu(g_ref[0, :, hs])).astype(o_ref.dtype)


def _mem_attn_rows_kernel(q_ref, g_ref, k_ref, v_ref, o_ref):
    pad = q_ref.shape[1]
    nh, hd = MEM_HEADS, MEM_HD
    nr = nh * pad
    q = q_ref[0] * (MEM_HD ** -0.5)
    qrows = jnp.concatenate([q[:, h * hd:(h + 1) * hd] for h in range(nh)], axis=0).astype(BF16)
    s = _dot_nt(qrows, k_ref[0].astype(BF16))
    r = lax.broadcasted_iota(jnp.int32, s.shape, 0)
    c = lax.broadcasted_iota(jnp.int32, s.shape, 1)
    same_head = _div(r, pad) == _mod(c, nh)
    m = jnp.max(jnp.where(same_head, s, NEG_INF), axis=-1, keepdims=True)
    p = jnp.where(same_head, jnp.exp(s - m), 0.0)
    l = jnp.sum(p, axis=-1, keepdims=True)
    o = _dot(p.astype(BF16), v_ref[0].astype(BF16)) / l
    y = jnp.concatenate([o[h * pad:(h + 1) * pad] for h in range(nh)], axis=1)
    o_ref[0] = y * _silu(g_ref[0])


def _mem_attn_rows(proj_pad, q_off, g_off, mem_k, mem_v, layer):
    b, pad, _ = proj_pad.shape
    w = MEM_W
    rows = MEM_LEN * MEM_HEADS
    kv = pl.BlockSpec((1, rows, MEM_HD), lambda i: (layer * b + i, 0, 0))
    return pl.pallas_call(
        _mem_attn_rows_kernel,
        grid=(b,),
        in_specs=[pl.BlockSpec((1, pad, w), lambda i: (i, 0, q_off // w)),
                  pl.BlockSpec((1, pad, w), lambda i: (i, 0, g_off // w)),
                  kv, kv],
        out_specs=pl.BlockSpec((1, pad, w), lambda i: (i, 0, 0)),
        out_shape=jax.ShapeDtypeStruct((b, pad, w), F32),
        compiler_params=_params("arbitrary"),
        name="mem_attn_rows",
    )(proj_pad, proj_pad, mem_k.reshape(-1, rows, MEM_HD), mem_v.reshape(-1, rows, MEM_HD))


def _mem_attn(proj, q_off, g_off, mem_k, k_off, mem_v, v_off, tq, out_dtype):
    b, t, _ = proj.shape
    w = MEM_W
    return pl.pallas_call(
        _mem_attn_kernel,
        grid=(b, t // tq),
        in_specs=[pl.BlockSpec((1, tq, w), lambda i, j: (i, j, q_off // w)),
                  pl.BlockSpec((1, tq, w), lambda i, j: (i, j, g_off // w)),
                  pl.BlockSpec((1, MEM_LEN, w), lambda i, j: (i, 0, k_off // w)),
                  pl.BlockSpec((1, MEM_LEN, w), lambda i, j: (i, 0, v_off // w))],
        out_specs=pl.BlockSpec((1, tq, w), lambda i, j: (i, j, 0)),
        out_shape=jax.ShapeDtypeStruct((b, t, w), out_dtype),
        compiler_params=_params("arbitrary", "arbitrary"),
        name="mem_attn",
    )(proj, proj, mem_k, mem_v)


def _gmlp_kernel(u_ref, v_ref, z_ref, ln_ref, w_ref, b_ref, y_ref, vo_ref, wc_ref):
    c = GMLP_CHUNK

    @pl.when(pl.program_id(0) == 0)
    def _():
        causal = (lax.broadcasted_iota(jnp.int32, (c, c), 0) >= lax.broadcasted_iota(jnp.int32, (c, c), 1))
        for g in range(GMLP_GROUPS):
            wc_ref[g] = jnp.where(causal, w_ref[g], 0.0).astype(BF16)

    nchunks = u_ref.shape[0] // c
    keep_all = vo_ref.shape[0] == u_ref.shape[0]
    for i in range(nchunks):
        rs = slice(i * c, (i + 1) * c)
        v = _gelu(v_ref[rs, :])
        mu = jnp.mean(v, axis=-1, keepdims=True)
        d = v - mu
        var = jnp.mean(d * d, axis=-1, keepdims=True)
        vn = d * lax.rsqrt(var + NORM_EPS) * ln_ref[...]
        if keep_all:
            vo_ref[rs, :] = vn
        elif i == nchunks - 1:
            vo_ref[...] = vn
        for g in range(GMLP_GROUPS):
            gs = slice(g * GMLP_GROUP, (g + 1) * GMLP_GROUP)
            mixed = _dot(wc_ref[g], vn[:, gs].astype(BF16)) + b_ref[:, gs]
            y_ref[rs, gs] = (_gelu(u_ref[rs, gs]) * mixed * _silu(z_ref[rs, gs])).astype(y_ref.dtype)


def _gmlp(proj, gmlp_ln, w_mix, bias_rows, keep_all_v):
    t = proj.shape[0]
    c = GMLP_CHUNK
    w = GMLP_WIDTH
    rows = GMLP_STEP_CHUNKS * c
    assert t % rows == 0
    col = lambda off: pl.BlockSpec((rows, w), lambda n: (n, off // w))
    return pl.pallas_call(
        _gmlp_kernel,
        grid=(t // rows,),
        in_specs=[col(O_U), col(O_V), col(O_Z),
                  pl.BlockSpec((1, w), lambda n: (0, 0)),
                  pl.BlockSpec(w_mix.shape, lambda n: (0, 0, 0)),
                  pl.BlockSpec((c, w), lambda n: (0, 0))],
        out_specs=[pl.BlockSpec((rows, w), lambda n: (n, 0)),
                   pl.BlockSpec((rows, w), lambda n: (n, 0)) if keep_all_v
                   else pl.BlockSpec((c, w), lambda n: (0, 0))],
        out_shape=[jax.ShapeDtypeStruct((t, w), BF16),
                   jax.ShapeDtypeStruct((t if keep_all_v else c, w), F32)],
        scratch_shapes=[pltpu.VMEM(w_mix.shape, BF16)],
        compiler_params=_params("arbitrary"),
        name="gmlp",
    )(proj, proj, proj, gmlp_ln.reshape(1, w), w_mix, bias_rows)


def _out_proj_kernel(*refs, n_in):
    a_refs, w_refs = refs[:n_in], refs[n_in:2 * n_in]
    g_ref, x_ref, o_ref = refs[2 * n_in:]
    acc = _dot(a_refs[0][...].astype(BF16), w_refs[0][...])
    for a_ref, w_ref in zip(a_refs[1:], w_refs[1:]):
        acc = acc + _dot(a_ref[...].astype(BF16), w_ref[...])
    ms = jnp.mean(acc * acc, axis=-1, keepdims=True)
    o_ref[...] = x_ref[...] + acc * lax.rsqrt(ms + NORM_EPS) * g_ref[...]


def _out_proj(acts, w, g, x, tm):
    m, d = x.shape
    n_in = len(acts)
    w_specs, off = [], 0
    for a in acts:
        k = a.shape[1]
        assert off % k == 0
        w_specs.append(pl.BlockSpec((k, d), functools.partial(lambda i, blk: (blk, 0), blk=off // k)))
        off += k
    assert off == w.shape[0]
    return pl.pallas_call(
        functools.partial(_out_proj_kernel, n_in=n_in),
        grid=(m // tm,),
        in_specs=([pl.BlockSpec((tm, a.shape[1]), lambda i: (i, 0)) for a in acts]
                  + w_specs
                  + [pl.BlockSpec((1, d), lambda i: (0, 0)),
                     pl.BlockSpec((tm, d), lambda i: (i, 0))]),
        out_specs=pl.BlockSpec((tm, d), lambda i: (i, 0)),
        out_shape=jax.ShapeDtypeStruct((m, d), F32),
        compiler_params=_params("arbitrary"),
        name="out_proj",
    )(*acts, *([w] * n_in), g.reshape(1, d), x)


def _out_proj_cast_kernel(a_ref, w_ref, g_ref, x_ref, o_ref, wb_ref, acc_ref):
    k = pl.program_id(0)

    @pl.when(k == 0)
    def _():
        acc_ref[...] = jnp.zeros_like(acc_ref)

    wb = w_ref[...].astype(BF16)
    wb_ref[...] = wb
    acc_ref[...] += _dot(a_ref[...].astype(BF16), wb)

    @pl.when(k == pl.num_programs(0) - 1)
    def _():
        acc = acc_ref[...]
        ms = jnp.mean(acc * acc, axis=-1, keepdims=True)
        o_ref[...] = x_ref[...] + acc * lax.rsqrt(ms + NORM_EPS) * g_ref[...]


def _out_proj_cast(a, w, g, x, tk):
    m, d = x.shape
    kdim = w.shape[0]
    assert kdim % tk == 0 and a.shape == (m, kdim)
    return pl.pallas_call(
        _out_proj_cast_kernel,
        grid=(kdim // tk,),
        in_specs=[pl.BlockSpec((m, tk), lambda k: (0, k)),
                  pl.BlockSpec((tk, d), lambda k: (k, 0)),
                  pl.BlockSpec((1, d), lambda k: (0, 0)),
                  pl.BlockSpec((m, d), lambda k: (0, 0))],
        out_specs=[pl.BlockSpec((m, d), lambda k: (0, 0)),
                   pl.BlockSpec((tk, d), lambda k: (k, 0))],
        out_shape=[jax.ShapeDtypeStruct((m, d), F32),
                   jax.ShapeDtypeStruct((kdim, d), BF16)],
        scratch_shapes=[pltpu.VMEM((m, d), F32)],
        compiler_params=_params("arbitrary"),
        name="out_proj_cast",
    )(a, w, g.reshape(1, d), x)


def _pad_steps(a, pad):
    nseq, seq, n = a.shape
    return jnp.pad(a, ((0, 0), (0, pad - seq), (0, 0))).reshape(nseq * pad, n)


def kernel(x_prompt, x_sample, state_ret, cache_swa_k, cache_swa_v, cache_mem_k, cache_mem_v, mem_prompt,
           pre_norm, post_norm, mem_norm, w_mem_k, w_mem_v, w_in_even, ret_gn, w_out_even,
           w_in_odd, gmlp_ln, w_spatial, b_spatial, w_out_odd):
    bp, tp, d = x_prompt.shape
    nseq, seq, _ = x_sample.shape
    pad = SAMPLE_PAD
    rows_s = nseq * pad
    c = GMLP_CHUNK
    assert bp == 1 and seq <= pad and rows_s % c == 0 and c % pad == 0
    hp = x_prompt.reshape(tp, d)
    hs = _pad_steps(x_sample, pad)
    mem = mem_prompt.reshape(MEM_LEN, d)
    tm_p = 1024 if tp % 1024 == 0 else 512
    unpad = lambda a: a.reshape(nseq, pad, -1)[:, :seq]

    pos_s = PAST_LEN + jnp.arange(seq, dtype=jnp.int32)
    pos_pad = jnp.concatenate([pos_s, jnp.zeros((pad - seq,), jnp.int32)])
    ret_cos_p, ret_sin_p = _rot_tables_range(tp, RET_DK, RET_THETA, RET_DK)
    ret_cos_s, ret_sin_s = _rot_tables(jnp.tile(pos_pad, nseq), RET_DK, RET_THETA, RET_DK)
    swa_cos_p, swa_sin_p = _rot_tables_range(tp, ROPE_DIMS, ROPE_THETA, SWA_HD)
    swa_cos_s, swa_sin_s = _rot_tables(pos_pad, ROPE_DIMS, ROPE_THETA, SWA_HD)
    swa_cos_s = jnp.tile(swa_cos_s, (1, SWA_HEADS))
    swa_sin_s = jnp.tile(swa_sin_s, (1, SWA_HEADS))

    w_mem0 = jnp.concatenate([w_mem_k[0], w_mem_v[0]], axis=1).astype(BF16)
    memkv0 = _norm_matmul(mem, mem_norm[0], w_mem0, MEM_LEN, 2 * MEM_W)

    proj_s, w_in0 = _norm_matmul_cast(hs, pre_norm[0], w_in_even[0], 1024)
    proj_s3 = proj_s.reshape(nseq, pad, EVEN_IN)
    ya_s, st_s = _ret_sample(proj_s, state_ret[0], ret_cos_s, ret_sin_s, ret_gn[0], nseq, seq, pad)
    yb_s, sk = _swa_sample(proj_s3, cache_swa_k[0], cache_swa_v[0], swa_cos_s, swa_sin_s, seq)
    ym_s = _mem_attn_rows(proj_s3, E_MQ, E_MG, cache_mem_k, cache_mem_v, 0)
    y_s = jnp.concatenate([ya_s, yb_s.reshape(rows_s, -1), ym_s.reshape(rows_s, -1)], axis=1)
    hs1, w_out0 = _out_proj_cast(y_s, w_out_even[0], post_norm[0], hs, 512)
    sk = sk[:, :seq].reshape(nseq, seq, SWA_HEADS, SWA_HD)
    sv = unpad(proj_s[:, E_BV:E_BV + SWA_HEADS * SWA_HD]).reshape(nseq, seq, SWA_HEADS, SWA_HD)

    proj_p = _norm_matmul(hp, pre_norm[0], w_in0, tm_p, 1024)
    ya_p, st_p = _ret_prompt(proj_p, ret_cos_p, ret_sin_p, ret_gn[0])
    yb_p, pk = _swa_prompt(proj_p, swa_cos_p, swa_sin_p)
    ym_p = _mem_attn(proj_p[None], E_MQ, E_MG, memkv0[None], 0, memkv0[None], MEM_W, 1024, BF16)[0]
    hp1 = _out_proj([ya_p, yb_p, ym_p], w_out0, post_norm[0], hp, 256)
    buf_p = min(SWA_MAX_WINDOW, tp)
    pv = proj_p[tp - buf_p:, E_BV:E_BV + SWA_HEADS * SWA_HD]

    w_mem1 = jnp.concatenate([w_mem_k[1], w_mem_v[1]], axis=1).astype(BF16)
    memkv1 = _norm_matmul(mem, mem_norm[1], w_mem1, MEM_LEN, 2 * MEM_W)
    bias_p = jnp.repeat(b_spatial[0].T, GMLP_GROUP, axis=1)
    blockdiag = lambda wg: jnp.kron(jnp.eye(c // pad, dtype=F32),
                                    jnp.pad(wg[:seq, :seq], ((0, pad - seq), (0, pad - seq))))
    w_mix_s = jax.vmap(blockdiag)(w_spatial[0])
    bias_s = jnp.tile(jnp.pad(bias_p[:seq], ((0, pad - seq), (0, 0))), (c // pad, 1))

    proj_s1, w_in1 = _norm_matmul_cast(hs1, pre_norm[1], w_in_odd[0], 1024)
    yc_s, gv_s = _gmlp(proj_s1, gmlp_ln[0], w_mix_s, bias_s, True)
    ym_s1 = _mem_attn_rows(proj_s1.reshape(nseq, pad, ODD_IN), O_MQ, O_MG, cache_mem_k, cache_mem_v, 1)
    y_s1 = jnp.concatenate([yc_s.astype(F32), ym_s1.reshape(rows_s, -1)], axis=1)
    hs2, w_out1 = _out_proj_cast(y_s1, w_out_odd[0], post_norm[1], hs1, 512)

    proj_p1 = _norm_matmul(hp1, pre_norm[1], w_in1, tm_p, 1024)
    yc_p, gv_p = _gmlp(proj_p1, gmlp_ln[0], w_spatial[0], bias_p, False)
    ym_p1 = _mem_attn(proj_p1[None], O_MQ, O_MG, memkv1[None], 0, memkv1[None], MEM_W, 1024, BF16)[0]
    hp2 = _out_proj([yc_p, ym_p1], w_out1, post_norm[1], hp1, 256)

    memkv = jnp.stack([memkv0, memkv1])
    p_mk = memkv[:, :, :MEM_W].reshape(2, bp, MEM_LEN, MEM_HEADS, MEM_HD)
    p_mv = memkv[:, :, MEM_W:].reshape(2, bp, MEM_LEN, MEM_HEADS, MEM_HD)
    return (hp2.reshape(bp, tp, d), unpad(hs2),
            st_p[None, None],
            pk.reshape(1, bp, buf_p, SWA_HEADS, SWA_HD), pv.reshape(1, bp, buf_p, SWA_HEADS, SWA_HD),
            p_mk, p_mv,
            gv_p[None, None],
            st_s[None],
            sk[None], sv[None],
            unpad(gv_s)[None])
```

```python
import functools
import math

import jax
import jax.numpy as jnp
from jax import lax
from jax.experimental import pallas as pl
from jax.experimental.pallas import tpu as pltpu

F32 = jnp.float32
BF16 = jnp.bfloat16

D_MODEL = 2048
PAST_LEN = 8192
RET_HEADS = 8
RET_DK = 128
RET_DV = 256
RET_CHUNK = 128
RET_THETA = 10000.0
SWA_HEADS = 8
SWA_HD = 128
SWA_PATTERNS = ((128, 1), (512, 4), (2048, 16))
SWA_MAX_WINDOW = 2048
SWA_BLOCK = 128
ROPE_THETA = 500000.0
ROPE_DIMS = SWA_HD // 4
GMLP_CHUNK = 128
GMLP_WIDTH = D_MODEL
GMLP_GROUP = 128
GMLP_GROUPS = GMLP_WIDTH // GMLP_GROUP
MEM_LEN = 256
MEM_HEADS = 4
MEM_HD = 128
MEM_W = MEM_HEADS * MEM_HD
NORM_EPS = 1e-6
NEG_INF = -1e30
LOG2E = math.log2(math.e)

E_AQ = 0
E_AK = E_AQ + RET_HEADS * RET_DK
E_AV = E_AK + RET_HEADS * RET_DK
E_AG = E_AV + RET_HEADS * RET_DV
E_BQ = E_AG + RET_HEADS * RET_DV
E_BK = E_BQ + SWA_HEADS * SWA_HD
E_BV = E_BK + SWA_HEADS * SWA_HD
E_BG = E_BV + SWA_HEADS * SWA_HD
E_MQ = E_BG + SWA_HEADS * SWA_HD
E_MG = E_MQ + MEM_W
EVEN_IN = E_MG + MEM_W
O_U = 0
O_V = O_U + GMLP_WIDTH
O_Z = O_V + GMLP_WIDTH
O_MQ = O_Z + GMLP_WIDTH
O_MG = O_MQ + MEM_W
ODD_IN = O_MG + MEM_W

SWA_TILE = SWA_MAX_WINDOW
SAMPLE_PAD = 8
GMLP_STEP_CHUNKS = 2
RET_STEP_CHUNKS = 2
MEM_ROWS_STEP_SEQS = 2
VMEM_LIMIT = 48 * 1024 * 1024


def _params(*sem):
    return pltpu.CompilerParams(dimension_semantics=sem, vmem_limit_bytes=VMEM_LIMIT)


def _dot(a, b):
    return jnp.dot(a, b, preferred_element_type=F32)


def _dot_nt(a, b):
    return lax.dot_general(a, b, (((1,), (1,)), ((), ())), preferred_element_type=F32)


def _silu(x):
    return x * jax.nn.sigmoid(x)


def _gelu(x):
    k = 2.0 * math.sqrt(2.0 / math.pi)
    return x * jax.nn.sigmoid(x * (k + (k * 0.044715) * (x * x)))


def _div(x, n):
    assert n & (n - 1) == 0
    return jnp.right_shift(x, n.bit_length() - 1)


def _mod(x, n):
    assert n & (n - 1) == 0
    return jnp.bitwise_and(x, n - 1)


def _rot_tables(pos, n_rot, theta, width):
    half = n_rot // 2
    inv = 1.0 / (theta ** (jnp.arange(half, dtype=F32) / half))
    ang = pos.astype(F32)[:, None] * inv[None, :]
    return _rot_assemble(jnp.cos(ang), jnp.sin(ang), width)


def _rot_tables_range(n, n_rot, theta, width, blk=128):
    assert n % blk == 0
    half = n_rot // 2
    inv = 1.0 / (theta ** (jnp.arange(half, dtype=F32) / half))
    hi = (jnp.arange(n // blk, dtype=F32) * blk)[:, None] * inv[None, :]
    lo = jnp.arange(blk, dtype=F32)[:, None] * inv[None, :]
    ch, sh, cl, sl = jnp.cos(hi)[:, None], jnp.sin(hi)[:, None], jnp.cos(lo)[None], jnp.sin(lo)[None]
    return _rot_assemble((ch * cl - sh * sl).reshape(n, half), (sh * cl + ch * sl).reshape(n, half), width)


def _rot_assemble(cos, sin, width):
    t, half = cos.shape
    cos_t = jnp.concatenate([cos, cos, jnp.ones((t, width - 2 * half), F32)], axis=-1)
    sin_t = jnp.concatenate([-sin, sin, jnp.zeros((t, width - 2 * half), F32)], axis=-1)
    return cos_t, sin_t


def _rot_full(x, cos, sin):
    return x * cos + pltpu.roll(x, RET_DK // 2, 1) * sin


def _rot_partial(x, cos, sin):
    half = ROPE_DIMS // 2
    lane = _mod(lax.broadcasted_iota(jnp.int32, x.shape, 1), SWA_HD)
    width = x.shape[1]
    partner = jnp.where(lane < half, pltpu.roll(x, width - half, 1), pltpu.roll(x, half, 1))
    return x * cos + partner * sin


def _norm_matmul_kernel(x_ref, g_ref, w_ref, o_ref, xn_ref):
    @pl.when(pl.program_id(1) == 0)
    def _():
        x = x_ref[...]
        ms = jnp.mean(x * x, axis=-1, keepdims=True)
        xn_ref[...] = (x * lax.rsqrt(ms + NORM_EPS) * g_ref[...]).astype(BF16)

    o_ref[...] = _dot(xn_ref[...], w_ref[...]).astype(o_ref.dtype)


def _norm_matmul(x, g, w, tm, tn):
    m, d = x.shape
    n = w.shape[1]
    return pl.pallas_call(
        _norm_matmul_kernel,
        grid=(m // tm, n // tn),
        in_specs=[pl.BlockSpec((tm, d), lambda i, j: (i, 0)),
                  pl.BlockSpec((1, d), lambda i, j: (0, 0)),
                  pl.BlockSpec((d, tn), lambda i, j: (0, j))],
        out_specs=pl.BlockSpec((tm, tn), lambda i, j: (i, j)),
        out_shape=jax.ShapeDtypeStruct((m, n), F32),
        scratch_shapes=[pltpu.VMEM((tm, d), BF16)],
        compiler_params=_params("arbitrary", "arbitrary"),
        name="norm_matmul",
    )(x, g.reshape(1, d), w)


def _norm_matmul_cast_kernel(x_ref, g_ref, w_ref, o_ref, wb_ref, xn_ref):
    @pl.when(pl.program_id(0) == 0)
    def _():
        x = x_ref[...]
        ms = jnp.mean(x * x, axis=-1, keepdims=True)
        xn_ref[...] = (x * lax.rsqrt(ms + NORM_EPS) * g_ref[...]).astype(BF16)

    wb = w_ref[...].astype(BF16)
    wb_ref[...] = wb
    o_ref[...] = _dot(xn_ref[...], wb).astype(o_ref.dtype)


def _norm_matmul_cast(x, g, w, tn):
    m, d = x.shape
    n = w.shape[1]
    return pl.pallas_call(
        _norm_matmul_cast_kernel,
        grid=(n // tn,),
        in_specs=[pl.BlockSpec((m, d), lambda j: (0, 0)),
                  pl.BlockSpec((1, d), lambda j: (0, 0)),
                  pl.BlockSpec((d, tn), lambda j: (0, j))],
        out_specs=[pl.BlockSpec((m, tn), lambda j: (0, j)),
                   pl.BlockSpec((d, tn), lambda j: (0, j))],
        out_shape=[jax.ShapeDtypeStruct((m, n), F32),
                   jax.ShapeDtypeStruct((d, n), BF16)],
        scratch_shapes=[pltpu.VMEM((m, d), BF16)],
        compiler_params=_params("arbitrary"),
        name="norm_matmul_cast",
    )(x, g.reshape(1, d), w)


def _ret_tables(c, reps, pad=None):
    pad = c if pad is None else pad
    rows = pad * reps
    lg = jnp.log1p(-jnp.exp2(-5.0 - jnp.arange(RET_HEADS, dtype=F32)))
    row = jnp.arange(rows, dtype=jnp.int32)
    step = (row % pad).astype(F32)
    live = (row % pad) < c
    rel = step[:, None] - step[None, :]
    keep = ((row // pad)[:, None] == (row // pad)[None, :]) & (rel >= 0) & live[:, None] & live[None, :]
    decay = jnp.where(keep[None], jnp.exp(lg[:, None, None] * jnp.maximum(rel, 0.0)[None]), 0.0)
    q_decay = jnp.where(live[None], jnp.exp(lg[:, None] * (step[None, :] + 1.0)), 0.0)
    k_decay = jnp.where(live[None], jnp.exp(lg[:, None] * (c - 1.0 - step[None, :])), 0.0)
    chunk_decay = jnp.exp(lg * c)
    qd = jnp.broadcast_to(q_decay[:, :, None], (RET_HEADS, rows, RET_DK))
    kd = jnp.broadcast_to(k_decay[:, :, None], (RET_HEADS, rows, RET_DK))
    cd = jnp.broadcast_to(chunk_decay[:, None, None], (RET_HEADS, 1, RET_DV))
    return decay, qd, kd, cd


def _group_norm_gate(o, gn, g):
    mu = jnp.mean(o, axis=-1, keepdims=True)
    d = o - mu
    var = jnp.mean(d * d, axis=-1, keepdims=True)
    return d * lax.rsqrt(var + NORM_EPS) * gn * _silu(g)


def _ret_prompt_kernel(q_ref, k_ref, v_ref, g_ref, cos_ref, sin_ref, dec_ref, qd_ref, kd_ref, cd_ref, gn_ref,
                       y_ref, st_ref):
    @pl.when(pl.program_id(0) == 0)
    def _():
        st_ref[...] = jnp.zeros_like(st_ref)

    for i in range(q_ref.shape[0] // RET_CHUNK):
        rs = slice(i * RET_CHUNK, (i + 1) * RET_CHUNK)
        cos = cos_ref[rs, :]
        sin = sin_ref[rs, :]
        for h in range(RET_HEADS):
            ks = slice(h * RET_DK, (h + 1) * RET_DK)
            vs = slice(h * RET_DV, (h + 1) * RET_DV)
            q = _rot_full(q_ref[rs, ks], cos, sin)
            k = _rot_full(k_ref[rs, ks] * (RET_DK ** -0.5), cos, sin)
            v = v_ref[rs, vs].astype(BF16)
            st = st_ref[h]
            s = _dot_nt(q.astype(BF16), k.astype(BF16)) * dec_ref[h]
            lhs = jnp.concatenate([s.astype(BF16), (q * qd_ref[h]).astype(BF16)], axis=1)
            o = _dot(lhs, jnp.concatenate([v, st.astype(BF16)], axis=0))
            st_ref[h] = cd_ref[h] * st + _dot((k * kd_ref[h]).T.astype(BF16), v)
            y_ref[rs, vs] = _group_norm_gate(o, gn_ref[:, vs], g_ref[rs, vs]).astype(y_ref.dtype)


def _ret_prompt(proj, cos, sin, ret_gn):
    t = proj.shape[0]
    dec, qd, kd, cd = _ret_tables(RET_CHUNK, 1)
    c = RET_STEP_CHUNKS * RET_CHUNK
    assert t % c == 0
    qw = RET_HEADS * RET_DK
    vw = RET_HEADS * RET_DV
    full3 = lambda n: (0, 0, 0)
    return pl.pallas_call(
        _ret_prompt_kernel,
        grid=(t // c,),
        in_specs=[pl.BlockSpec((c, qw), lambda n: (n, E_AQ // qw)),
                  pl.BlockSpec((c, qw), lambda n: (n, E_AK // qw)),
                  pl.BlockSpec((c, vw), lambda n: (n, E_AV // vw)),
                  pl.BlockSpec((c, vw), lambda n: (n, E_AG // vw)),
                  pl.BlockSpec((c, RET_DK), lambda n: (n, 0)),
                  pl.BlockSpec((c, RET_DK), lambda n: (n, 0)),
                  pl.BlockSpec(dec.shape, full3),
                  pl.BlockSpec(qd.shape, full3),
                  pl.BlockSpec(kd.shape, full3),
                  pl.BlockSpec(cd.shape, full3),
                  pl.BlockSpec((1, vw), lambda n: (0, 0))],
        out_specs=[pl.BlockSpec((c, vw), lambda n: (n, 0)),
                   pl.BlockSpec((RET_HEADS, RET_DK, RET_DV), full3)],
        out_shape=[jax.ShapeDtypeStruct((t, vw), BF16),
                   jax.ShapeDtypeStruct((RET_HEADS, RET_DK, RET_DV), F32)],
        compiler_params=_params("arbitrary"),
        name="ret_prompt",
    )(proj, proj, proj, proj, cos, sin, dec, qd, kd, cd, ret_gn.reshape(1, vw))


def _ret_sample_kernel(q_ref, k_ref, v_ref, g_ref, cos_ref, sin_ref, dec_ref, qd_ref, kd_ref, cd_ref, gn_ref,
                       st_ref, y_ref, sto_ref, cross_ref, qs_ref, kt_ref, *, pad):
    b = pl.program_id(0)
    nb = pl.num_programs(0)
    win = RET_CHUNK

    @pl.when(b == 0)
    def _():
        cross_ref[...] = jnp.zeros_like(cross_ref)
        cos = cos_ref[...]
        sin = sin_ref[...]
        for h in range(RET_HEADS):
            ks = slice(h * RET_DK, (h + 1) * RET_DK)
            qs_ref[:, ks] = _rot_full(q_ref[:, ks], cos, sin) * qd_ref[h]
            kt_ref[h] = (_rot_full(k_ref[:, ks] * (RET_DK ** -0.5), cos, sin) * kd_ref[h]).T

    w0 = pl.multiple_of(jnp.right_shift(b * pad, win.bit_length() - 1) * win, win)
    wrows = pl.ds(w0, win)
    rows = lax.broadcasted_iota(jnp.int32, (win, RET_DK), 0) + w0
    cols = lax.broadcasted_iota(jnp.int32, (RET_DK, win), 1) + w0
    mine_r = (rows >= b * pad) & (rows < (b + 1) * pad)
    mine_c = (cols >= b * pad) & (cols < (b + 1) * pad)
    for h in range(RET_HEADS):
        ks = slice(h * RET_DK, (h + 1) * RET_DK)
        vs = slice(h * RET_DV, (h + 1) * RET_DV)
        st = st_ref[0, h]
        qm = jnp.where(mine_r, qs_ref[wrows, ks], 0.0)
        kmt = jnp.where(mine_c, kt_ref[h, :, wrows], 0.0)
        cross_ref[wrows, vs] += _dot(qm.astype(BF16), st.astype(BF16))
        sto_ref[0, h] = cd_ref[h] * st + _dot(kmt.astype(BF16), v_ref[wrows, vs].astype(BF16))

    @pl.when(b == nb - 1)
    def _():
        cos_all = cos_ref[...]
        sin_all = sin_ref[...]
        for h in range(RET_HEADS):
            ks = slice(h * RET_DK, (h + 1) * RET_DK)
            vs = slice(h * RET_DV, (h + 1) * RET_DV)
            q = _rot_full(q_ref[:, ks], cos_all, sin_all)
            k = _rot_full(k_ref[:, ks] * (RET_DK ** -0.5), cos_all, sin_all)
            v = v_ref[:, vs].astype(BF16)
            s = _dot_nt(q.astype(BF16), k.astype(BF16)) * dec_ref[h]
            o = _dot(s.astype(BF16), v) + cross_ref[:, vs]
            y_ref[:, vs] = _group_norm_gate(o, gn_ref[:, vs], g_ref[:, vs]).astype(y_ref.dtype)


def _ret_sample(proj, state, cos, sin, ret_gn, nseq, seq, pad):
    rows = nseq * pad
    c = math.gcd(seq, RET_CHUNK)
    assert c == seq, "sample chunk must cover the new tokens"
    assert rows % RET_CHUNK == 0 and RET_CHUNK % pad == 0
    dec, qd, kd, cd = _ret_tables(c, nseq, pad)
    qw = RET_HEADS * RET_DK
    vw = RET_HEADS * RET_DV
    full3 = lambda b: (0, 0, 0)
    st_spec = pl.BlockSpec((1, RET_HEADS, RET_DK, RET_DV), lambda b: (b, 0, 0, 0))
    return pl.pallas_call(
        functools.partial(_ret_sample_kernel, pad=pad),
        grid=(nseq,),
        in_specs=[pl.BlockSpec((rows, qw), lambda b: (0, E_AQ // qw)),
                  pl.BlockSpec((rows, qw), lambda b: (0, E_AK // qw)),
                  pl.BlockSpec((rows, vw), lambda b: (0, E_AV // vw)),
                  pl.BlockSpec((rows, vw), lambda b: (0, E_AG // vw)),
                  pl.BlockSpec((rows, RET_DK), lambda b: (0, 0)),
                  pl.BlockSpec((rows, RET_DK), lambda b: (0, 0)),
                  pl.BlockSpec(dec.shape, full3),
                  pl.BlockSpec(qd.shape, full3),
                  pl.BlockSpec(kd.shape, full3),
                  pl.BlockSpec(cd.shape, full3),
                  pl.BlockSpec((1, vw), lambda b: (0, 0)),
                  st_spec],
        out_specs=[pl.BlockSpec((rows, vw), lambda b: (0, 0)), st_spec],
        out_shape=[jax.ShapeDtypeStruct((rows, vw), F32),
                   jax.ShapeDtypeStruct(state.shape, F32)],
        scratch_shapes=[pltpu.VMEM((rows, vw), F32),
                        pltpu.VMEM((rows, qw), F32),
                        pltpu.VMEM((RET_HEADS, RET_DK, rows), F32)],
        compiler_params=_params("arbitrary"),
        name="ret_sample",
    )(proj, proj, proj, proj, cos, sin, dec, qd, kd, cd, ret_gn.reshape(1, vw), state)


def _swa_prompt_kernel(q_ref, kc_ref, vc_ref, g_ref, cos_ref, sin_ref,
                       y_ref, ko_ref, q_s, k_s, v_s, m_s, l_s, acc_s):
    t = pl.program_id(1)
    nt = pl.num_programs(1)
    tb = SWA_TILE
    blk = SWA_BLOCK
    hd = SWA_HD
    cos = cos_ref[...]
    sin = sin_ref[...]

    @pl.when(t == 0)
    def _():
        k_s[0:tb, :] = jnp.zeros((tb, hd), F32)
        v_s[0:tb, :] = jnp.zeros((tb, hd), F32)

    @pl.when(t > 0)
    def _():
        k_s[0:tb, :] = k_s[tb:2 * tb, :]
        v_s[0:tb, :] = v_s[tb:2 * tb, :]

    q_s[...] = _rot_partial(q_ref[...] * (SWA_HD ** -0.5 * LOG2E), cos, sin)
    kr = _rot_partial(kc_ref[...], cos, sin)
    k_s[tb:2 * tb, :] = kr
    v_s[tb:2 * tb, :] = vc_ref[...]

    @pl.when(t == nt - 1)
    def _():
        ko_ref[...] = kr

    qi = lax.broadcasted_iota(jnp.int32, (blk, 2 * blk), 0)
    kj = lax.broadcasted_iota(jnp.int32, (blk, 2 * blk), 1)
    band = (kj >= qi) & (kj <= qi + blk)
    band_first = band & ((kj >= blk) | (t > 0))
    ones = jnp.ones((2 * blk, hd), BF16)

    def ds(start, size, stride):
        return pl.ds(start, size) if stride == 1 else pl.ds(start, size, stride=stride)

    for bi, (window, r) in enumerate(SWA_PATTERNS):
        assert window // r == blk
        for idx in range(tb // blk):
            c, n = idx % r, idx // r
            qs = n * blk * r + c
            ks = qs + tb - blk * r
            qrows = ds(qs, blk, r)
            krows = ds(ks, 2 * blk, r)
            valid = band_first if n == 0 else band
            s = _dot_nt(q_s[qrows, :].astype(BF16), k_s[krows, :].astype(BF16))
            s = jnp.where(valid, s, NEG_INF)
            m = jnp.max(s, axis=-1, keepdims=True)
            p = jnp.exp2(s - m).astype(BF16)
            pv = _dot(p, jnp.concatenate([v_s[krows, :].astype(BF16), ones], axis=1))
            m_s[bi, qrows, :] = jnp.broadcast_to(m, (blk, hd))
            l_s[bi, qrows, :] = pv[:, hd:]
            acc_s[bi, qrows, :] = pv[:, :hd]

    nbr = len(SWA_PATTERNS)
    ms = [m_s[i] for i in range(nbr)]
    mmax = functools.reduce(jnp.maximum, ms)
    wts = [jnp.exp2(mi - mmax) for mi in ms]
    num = sum(wts[i] * acc_s[i] for i in range(nbr))
    den = sum(wts[i] * l_s[i] for i in range(nbr))
    y_ref[...] = (num / den * _silu(g_ref[...])).astype(y_ref.dtype)


def _swa_prompt(proj, cos, sin):
    t = proj.shape[0]
    tb = SWA_TILE
    assert t % tb == 0
    hd = SWA_HD
    nbr = len(SWA_PATTERNS)
    cur = lambda off: pl.BlockSpec((tb, hd), lambda h, i: (i, off // hd + h))
    return pl.pallas_call(
        _swa_prompt_kernel,
        grid=(SWA_HEADS, t // tb),
        in_specs=[cur(E_BQ), cur(E_BK), cur(E_BV), cur(E_BG),
                  pl.BlockSpec((tb, hd), lambda h, i: (i, 0)),
                  pl.BlockSpec((tb, hd), lambda h, i: (i, 0))],
        out_specs=[pl.BlockSpec((tb, hd), lambda h, i: (i, h)),
                   pl.BlockSpec((tb, hd), lambda h, i: (0, h))],
        out_shape=[jax.ShapeDtypeStruct((t, SWA_HEADS * hd), BF16),
                   jax.ShapeDtypeStruct((tb, SWA_HEADS * hd), F32)],
        scratch_shapes=[pltpu.VMEM((tb, hd), F32),
                        pltpu.VMEM((2 * tb, hd), F32),
                        pltpu.VMEM((2 * tb, hd), F32),
                        pltpu.VMEM((nbr, tb, hd), F32),
                        pltpu.VMEM((nbr, tb, hd), F32),
                        pltpu.VMEM((nbr, tb, hd), F32)],
        compiler_params=_params("arbitrary", "arbitrary"),
        name="swa_prompt",
    )(proj, proj, proj, proj, cos, sin)


def _swa_sample_kernel(q_ref, k_ref, v_ref, g_ref, cos_ref, sin_ref, ka_ref, kb_ref, va_ref, vb_ref,
                       y_ref, ko_ref, kn_s, vn_s, ba_s, bb_s, bn_s, *, seq):
    pad = SAMPLE_PAD
    nh = SWA_HEADS
    hd = SWA_HD
    nr = nh * pad
    nj, nrh, _ = kb_ref.shape[1:]
    na = ka_ref.shape[1]
    nb = nj * nrh
    nn = kn_s.shape[0]

    @pl.when(pl.program_id(0) == 0)
    def _():
        (w1, d1), (w2, d2), (w3, d3) = SWA_PATTERNS

        def rows_cols(n):
            r = lax.broadcasted_iota(jnp.int32, (nr, n), 0)
            c = lax.broadcasted_iota(jnp.int32, (nr, n), 1)
            return _div(r, pad), _mod(r, pad), c

        def log2_mult(mult):
            return jnp.where(mult > 0, jnp.log2(jnp.maximum(mult, 1.0)), NEG_INF)

        hq, tq, c = rows_cols(na)
        dist = na // nh + tq - _div(c, nh)
        mult_a = ((dist <= w1).astype(F32) + ((_mod(dist, d2) == 0) & (dist <= w2)).astype(F32)
                  + ((_mod(dist, d3) == 0) & (dist <= w3)).astype(F32))
        ba_s[...] = log2_mult(jnp.where((hq == _mod(c, nh)) & (tq < seq), mult_a, 0.0))
        hq, tq, c = rows_cols(nb)
        bb_s[...] = log2_mult(jnp.where((hq == _mod(c, nh)) & (tq == _mod(_div(c, nh), nrh // nh)), 1.0, 0.0))
        hq, tq, c = rows_cols(nn)
        tk = _mod(c, pad)
        mult_n = (tk <= tq).astype(F32) + 2.0 * (tk == tq).astype(F32)
        bn_s[...] = log2_mult(jnp.where((hq == _div(c, pad)) & (tq < seq) & (tk < seq), mult_n, 0.0))
        kn_s[...] = jnp.zeros_like(kn_s)
        vn_s[:, 0:hd] = jnp.zeros((nn, hd), BF16)
        vn_s[:, hd:2 * hd] = jnp.ones((nn, hd), BF16)

    cos = cos_ref[...]
    sin = sin_ref[...]
    q8 = _rot_partial(q_ref[0] * (SWA_HD ** -0.5 * LOG2E), cos, sin)
    k8 = _rot_partial(k_ref[0], cos, sin)
    ko_ref[0] = k8

    def head_rows(x):
        return jnp.concatenate([x[:, h * hd:(h + 1) * hd] for h in range(nh)], axis=0)

    qrows = head_rows(q8).astype(BF16)
    kn_s[0:nr, :] = head_rows(k8).astype(BF16)
    vn_s[0:nr, 0:hd] = head_rows(v_ref[0]).astype(BF16)

    kb = kb_ref[0].reshape(nb, hd)
    vb = vb_ref[0].reshape(nb, hd)
    s_a = _dot_nt(qrows, ka_ref[0].astype(BF16)) + ba_s[...]
    s_b = _dot_nt(qrows, kb.astype(BF16)) + bb_s[...]
    s_n = _dot_nt(qrows, kn_s[...]) + bn_s[...]
    rowmax = lambda s: jnp.max(s, axis=-1, keepdims=True)
    m = jnp.maximum(jnp.maximum(rowmax(s_a), rowmax(s_b)), rowmax(s_n))
    ones_a = jnp.ones((na, hd), BF16)
    ones_b = jnp.ones((nb, hd), BF16)
    ol = (_dot(jnp.exp2(s_a - m).astype(BF16), jnp.concatenate([va_ref[0].astype(BF16), ones_a], axis=1))
          + _dot(jnp.exp2(s_b - m).astype(BF16), jnp.concatenate([vb.astype(BF16), ones_b], axis=1))
          + _dot(jnp.exp2(s_n - m).astype(BF16), vn_s[...]))
    o = ol[:, :hd] / ol[:, hd:]
    y = jnp.concatenate([o[h * pad:(h + 1) * pad] for h in range(nh)], axis=1)
    y_ref[0] = y * _silu(g_ref[0])


def _swa_sample(proj_pad, cache_k, cache_v, cos, sin, seq):
    nseq, pad, _ = proj_pad.shape
    buf = cache_k.shape[1]
    nh, hd = SWA_HEADS, SWA_HD
    w = nh * hd
    (w1, d1), (w2, d2), (w3, d3) = SWA_PATTERNS
    assert buf == w3 and d1 == 1 and seq <= d2 and w1 <= w2 and buf % w2 == 0 and buf % d3 == 0
    assert nh * pad <= SWA_BLOCK and (seq * nh) % 8 == 0
    flat = lambda a: a.reshape(nseq, buf * nh, hd)
    strided = lambda a: a.reshape(nseq, buf // d3, d3 * nh, hd)
    col = lambda off: pl.BlockSpec((1, pad, w), lambda b: (b, 0, off // w))
    tail = pl.BlockSpec((1, w2 * nh, hd), lambda b: (b, buf // w2 - 1, 0))
    resid = pl.BlockSpec((1, (buf - w2) // d3, seq * nh, hd), lambda b: (b, 0, 0, 0))
    tab = pl.BlockSpec((pad, w), lambda b: (0, 0))
    out = pl.BlockSpec((1, pad, w), lambda b: (b, 0, 0))
    return pl.pallas_call(
        functools.partial(_swa_sample_kernel, seq=seq),
        grid=(nseq,),
        in_specs=[col(E_BQ), col(E_BK), col(E_BV), col(E_BG), tab, tab, tail, resid, tail, resid],
        out_specs=[out, out],
        out_shape=[jax.ShapeDtypeStruct((nseq, pad, w), F32),
                   jax.ShapeDtypeStruct((nseq, pad, w), F32)],
        scratch_shapes=[pltpu.VMEM((SWA_BLOCK, hd), BF16), pltpu.VMEM((SWA_BLOCK, 2 * hd), BF16),
                        pltpu.VMEM((nh * pad, w2 * nh), F32),
                        pltpu.VMEM((nh * pad, (buf - w2) // d3 * seq * nh), F32),
                        pltpu.VMEM((nh * pad, SWA_BLOCK), F32)],
        compiler_params=_params("arbitrary"),
        name="swa_sample",
    )(proj_pad, proj_pad, proj_pad, proj_pad, cos, sin,
      flat(cache_k), strided(cache_k), flat(cache_v), strided(cache_v))


def _mem_attn_kernel(q_ref, g_ref, k_ref, v_ref, o_ref):
    ones = jnp.ones((MEM_LEN, MEM_HD), BF16)
    for h in range(MEM_HEADS):
        hs = slice(h * MEM_HD, (h + 1) * MEM_HD)
        q = (q_ref[0, :, hs] * (MEM_HD ** -0.5 * LOG2E)).astype(BF16)
        s = _dot_nt(q, k_ref[0, :, hs].astype(BF16))
        m = jnp.max(s, axis=-1, keepdims=True)
        p = jnp.exp2(s - m).astype(BF16)
        pv = _dot(p, jnp.concatenate([v_ref[0, :, hs].astype(BF16), ones], axis=1))
        o = pv[:, :MEM_HD] / pv[:, MEM_HD:]
        o_ref[0, :, hs] = (o * _silu(g_ref[0, :, hs])).astype(o_ref.dtype)


def _mem_attn_rows_kernel(q_ref, g_ref, k_ref, v_ref, o_ref, bias_s):
    nseq, pad, _ = q_ref.shape
    nh, hd = MEM_HEADS, MEM_HD

    @pl.when(pl.program_id(0) == 0)
    def _():
        r = lax.broadcasted_iota(jnp.int32, bias_s.shape, 0)
        c = lax.broadcasted_iota(jnp.int32, bias_s.shape, 1)
        bias_s[...] = jnp.where(_div(r, pad) == _mod(c, nh), 0.0, NEG_INF)

    ones = jnp.ones((k_ref.shape[1], hd), BF16)
    for i in range(nseq):
        q = q_ref[i] * (MEM_HD ** -0.5 * LOG2E)
        qrows = jnp.concatenate([q[:, h * hd:(h + 1) * hd] for h in range(nh)], axis=0).astype(BF16)
        s = _dot_nt(qrows, k_ref[i].astype(BF16)) + bias_s[...]
        m = jnp.max(s, axis=-1, keepdims=True)
        p = jnp.exp2(s - m).astype(BF16)
        ol = _dot(p, jnp.concatenate([v_ref[i].astype(BF16), ones], axis=1))
        o = ol[:, :hd] / ol[:, hd:]
        y = jnp.concatenate([o[h * pad:(h + 1) * pad] for h in range(nh)], axis=1)
        o_ref[i] = y * _silu(g_ref[i])


def _mem_attn_rows(proj_pad, q_off, g_off, mem_k, mem_v, layer):
    b, pad, _ = proj_pad.shape
    w = MEM_W
    rows = MEM_LEN * MEM_HEADS
    n = MEM_ROWS_STEP_SEQS
    assert b % n == 0
    kv = pl.BlockSpec((n, rows, MEM_HD), lambda i: (layer * (b // n) + i, 0, 0))
    return pl.pallas_call(
        _mem_attn_rows_kernel,
        grid=(b // n,),
        in_specs=[pl.BlockSpec((n, pad, w), lambda i: (i, 0, q_off // w)),
                  pl.BlockSpec((n, pad, w), lambda i: (i, 0, g_off // w)),
                  kv, kv],
        out_specs=pl.BlockSpec((n, pad, w), lambda i: (i, 0, 0)),
        out_shape=jax.ShapeDtypeStruct((b, pad, w), F32),
        scratch_shapes=[pltpu.VMEM((MEM_HEADS * pad, rows), F32)],
        compiler_params=_params("arbitrary"),
        name="mem_attn_rows",
    )(proj_pad, proj_pad, mem_k.reshape(-1, rows, MEM_HD), mem_v.reshape(-1, rows, MEM_HD))


def _mem_attn(proj, q_off, g_off, mem_k, k_off, mem_v, v_off, tq, out_dtype):
    b, t, _ = proj.shape
    w = MEM_W
    return pl.pallas_call(
        _mem_attn_kernel,
        grid=(b, t // tq),
        in_specs=[pl.BlockSpec((1, tq, w), lambda i, j: (i, j, q_off // w)),
                  pl.BlockSpec((1, tq, w), lambda i, j: (i, j, g_off // w)),
                  pl.BlockSpec((1, MEM_LEN, w), lambda i, j: (i, 0, k_off // w)),
                  pl.BlockSpec((1, MEM_LEN, w), lambda i, j: (i, 0, v_off // w))],
        out_specs=pl.BlockSpec((1, tq, w), lambda i, j: (i, j, 0)),
        out_shape=jax.ShapeDtypeStruct((b, t, w), out_dtype),
        compiler_params=_params("arbitrary", "arbitrary"),
        name="mem_attn",
    )(proj, proj, mem_k, mem_v)


def _gmlp_kernel(u_ref, v_ref, z_ref, ln_ref, w_ref, b_ref, y_ref, vo_ref, wc_ref):
    c = GMLP_CHUNK

    @pl.when(pl.program_id(0) == 0)
    def _():
        causal = (lax.broadcasted_iota(jnp.int32, (c, c), 0) >= lax.broadcasted_iota(jnp.int32, (c, c), 1))
        for g in range(GMLP_GROUPS):
            wc_ref[g] = jnp.where(causal, w_ref[g], 0.0).astype(BF16)

    nchunks = u_ref.shape[0] // c
    keep_all = vo_ref.shape[0] == u_ref.shape[0]
    for i in range(nchunks):
        rs = slice(i * c, (i + 1) * c)
        v = _gelu(v_ref[rs, :])
        mu = jnp.mean(v, axis=-1, keepdims=True)
        d = v - mu
        var = jnp.mean(d * d, axis=-1, keepdims=True)
        vn = d * lax.rsqrt(var + NORM_EPS) * ln_ref[...]
        if keep_all:
            vo_ref[rs, :] = vn
        elif i == nchunks - 1:
            vo_ref[...] = vn
        for g in range(GMLP_GROUPS):
            gs = slice(g * GMLP_GROUP, (g + 1) * GMLP_GROUP)
            mixed = _dot(wc_ref[g], vn[:, gs].astype(BF16)) + b_ref[:, gs]
            y_ref[rs, gs] = (_gelu(u_ref[rs, gs]) * mixed * _silu(z_ref[rs, gs])).astype(y_ref.dtype)


def _gmlp(proj, gmlp_ln, w_mix, bias_rows, keep_all_v):
    t = proj.shape[0]
    c = GMLP_CHUNK
    w = GMLP_WIDTH
    rows = GMLP_STEP_CHUNKS * c
    assert t % rows == 0
    col = lambda off: pl.BlockSpec((rows, w), lambda n: (n, off // w))
    return pl.pallas_call(
        _gmlp_kernel,
        grid=(t // rows,),
        in_specs=[col(O_U), col(O_V), col(O_Z),
                  pl.BlockSpec((1, w), lambda n: (0, 0)),
                  pl.BlockSpec(w_mix.shape, lambda n: (0, 0, 0)),
                  pl.BlockSpec((c, w), lambda n: (0, 0))],
        out_specs=[pl.BlockSpec((rows, w), lambda n: (n, 0)),
                   pl.BlockSpec((rows, w), lambda n: (n, 0)) if keep_all_v
                   else pl.BlockSpec((c, w), lambda n: (0, 0))],
        out_shape=[jax.ShapeDtypeStruct((t, w), BF16),
                   jax.ShapeDtypeStruct((t if keep_all_v else c, w), F32)],
        scratch_shapes=[pltpu.VMEM(w_mix.shape, BF16)],
        compiler_params=_params("arbitrary"),
        name="gmlp",
    )(proj, proj, proj, gmlp_ln.reshape(1, w), w_mix, bias_rows)


def _out_proj_kernel(*refs, n_in):
    a_refs, w_refs = refs[:n_in], refs[n_in:2 * n_in]
    g_ref, x_ref, o_ref = refs[2 * n_in:]
    acc = _dot(a_refs[0][...].astype(BF16), w_refs[0][...])
    for a_ref, w_ref in zip(a_refs[1:], w_refs[1:]):
        acc = acc + _dot(a_ref[...].astype(BF16), w_ref[...])
    ms = jnp.mean(acc * acc, axis=-1, keepdims=True)
    o_ref[...] = x_ref[...] + acc * lax.rsqrt(ms + NORM_EPS) * g_ref[...]


def _out_proj(acts, w, g, x, tm):
    m, d = x.shape
    n_in = len(acts)
    w_specs, off = [], 0
    for a in acts:
        k = a.shape[1]
        assert off % k == 0
        w_specs.append(pl.BlockSpec((k, d), functools.partial(lambda i, blk: (blk, 0), blk=off // k)))
        off += k
    assert off == w.shape[0]
    return pl.pallas_call(
        functools.partial(_out_proj_kernel, n_in=n_in),
        grid=(m // tm,),
        in_specs=([pl.BlockSpec((tm, a.shape[1]), lambda i: (i, 0)) for a in acts]
                  + w_specs
                  + [pl.BlockSpec((1, d), lambda i: (0, 0)),
                     pl.BlockSpec((tm, d), lambda i: (i, 0))]),
        out_specs=pl.BlockSpec((tm, d), lambda i: (i, 0)),
        out_shape=jax.ShapeDtypeStruct((m, d), F32),
        compiler_params=_params("arbitrary"),
        name="out_proj",
    )(*acts, *([w] * n_in), g.reshape(1, d), x)


def _out_proj_cast_kernel(a_ref, w_ref, g_ref, x_ref, o_ref, wb_ref):
    k = pl.program_id(0)

    @pl.when(k == 0)
    def _():
        o_ref[...] = jnp.zeros_like(o_ref)

    wb = w_ref[...].astype(BF16)
    wb_ref[...] = wb
    o_ref[...] += _dot(a_ref[...].astype(BF16), wb)

    @pl.when(k == pl.num_programs(0) - 1)
    def _():
        acc = o_ref[...]
        ms = jnp.mean(acc * acc, axis=-1, keepdims=True)
        o_ref[...] = x_ref[...] + acc * lax.rsqrt(ms + NORM_EPS) * g_ref[...]


def _out_proj_cast(a, w, g, x, tk):
    m, d = x.shape
    kdim = w.shape[0]
    assert kdim % tk == 0 and a.shape == (m, kdim)
    return pl.pallas_call(
        _out_proj_cast_kernel,
        grid=(kdim // tk,),
        in_specs=[pl.BlockSpec((m, tk), lambda k: (0, k)),
                  pl.BlockSpec((tk, d), lambda k: (k, 0)),
                  pl.BlockSpec((1, d), lambda k: (0, 0)),
                  pl.BlockSpec((m, d), lambda k: (0, 0))],
        out_specs=[pl.BlockSpec((m, d), lambda k: (0, 0)),
                   pl.BlockSpec((tk, d), lambda k: (k, 0))],
        out_shape=[jax.ShapeDtypeStruct((m, d), F32),
                   jax.ShapeDtypeStruct((kdim, d), BF16)],
        compiler_params=_params("arbitrary"),
        name="out_proj_cast",
    )(a, w, g.reshape(1, d), x)


def _pad_steps(a, pad):
    nseq, seq, n = a.shape
    return jnp.pad(a, ((0, 0), (0, pad - seq), (0, 0))).reshape(nseq * pad, n)


def kernel(x_prompt, x_sample, state_ret, cache_swa_k, cache_swa_v, cache_mem_k, cache_mem_v, mem_prompt,
           pre_norm, post_norm, mem_norm, w_mem_k, w_mem_v, w_in_even, ret_gn, w_out_even,
           w_in_odd, gmlp_ln, w_spatial, b_spatial, w_out_odd):
    bp, tp, d = x_prompt.shape
    nseq, seq, _ = x_sample.shape
    pad = SAMPLE_PAD
    rows_s = nseq * pad
    c = GMLP_CHUNK
    assert bp == 1 and seq <= pad and rows_s % c == 0 and c % pad == 0
    hp = x_prompt.reshape(tp, d)
    hs = _pad_steps(x_sample, pad)
    mem = mem_prompt.reshape(MEM_LEN, d)
    tm_p = 1024 if tp % 1024 == 0 else 512
    unpad = lambda a: a.reshape(nseq, pad, -1)[:, :seq]

    pos_s = PAST_LEN + jnp.arange(seq, dtype=jnp.int32)
    pos_pad = jnp.concatenate([pos_s, jnp.zeros((pad - seq,), jnp.int32)])
    ret_cos_p, ret_sin_p = _rot_tables_range(tp, RET_DK, RET_THETA, RET_DK)
    ret_cos_s, ret_sin_s = _rot_tables(jnp.tile(pos_pad, nseq), RET_DK, RET_THETA, RET_DK)
    swa_cos_p, swa_sin_p = _rot_tables_range(tp, ROPE_DIMS, ROPE_THETA, SWA_HD)
    swa_cos_s, swa_sin_s = _rot_tables(pos_pad, ROPE_DIMS, ROPE_THETA, SWA_HD)
    swa_cos_s = jnp.tile(swa_cos_s, (1, SWA_HEADS))
    swa_sin_s = jnp.tile(swa_sin_s, (1, SWA_HEADS))

    w_mem0 = jnp.concatenate([w_mem_k[0], w_mem_v[0]], axis=1).astype(BF16)
    memkv0 = _norm_matmul(mem, mem_norm[0], w_mem0, MEM_LEN, 2 * MEM_W)

    proj_s, w_in0 = _norm_matmul_cast(hs, pre_norm[0], w_in_even[0], 1024)
    proj_s3 = proj_s.reshape(nseq, pad, EVEN_IN)
    ya_s, st_s = _ret_sample(proj_s, state_ret[0], ret_cos_s, ret_sin_s, ret_gn[0], nseq, seq, pad)
    yb_s, sk = _swa_sample(proj_s3, cache_swa_k[0], cache_swa_v[0], swa_cos_s, swa_sin_s, seq)
    ym_s = _mem_attn_rows(proj_s3, E_MQ, E_MG, cache_mem_k, cache_mem_v, 0)
    y_s = jnp.concatenate([ya_s, yb_s.reshape(rows_s, -1), ym_s.reshape(rows_s, -1)], axis=1)
    hs1, w_out0 = _out_proj_cast(y_s, w_out_even[0], post_norm[0], hs, 512)
    sk = sk[:, :seq].reshape(nseq, seq, SWA_HEADS, SWA_HD)
    sv = unpad(proj_s[:, E_BV:E_BV + SWA_HEADS * SWA_HD]).reshape(nseq, seq, SWA_HEADS, SWA_HD)

    proj_p = _norm_matmul(hp, pre_norm[0], w_in0, tm_p, 1024)
    ya_p, st_p = _ret_prompt(proj_p, ret_cos_p, ret_sin_p, ret_gn[0])
    yb_p, pk = _swa_prompt(proj_p, swa_cos_p, swa_sin_p)
    ym_p = _mem_attn(proj_p[None], E_MQ, E_MG, memkv0[None], 0, memkv0[None], MEM_W, 1024, BF16)[0]
    hp1 = _out_proj([ya_p, yb_p, ym_p], w_out0, post_norm[0], hp, 256)
    buf_p = min(SWA_MAX_WINDOW, tp)
    pv = proj_p[tp - buf_p:, E_BV:E_BV + SWA_HEADS * SWA_HD]

    w_mem1 = jnp.concatenate([w_mem_k[1], w_mem_v[1]], axis=1).astype(BF16)
    memkv1 = _norm_matmul(mem, mem_norm[1], w_mem1, MEM_LEN, 2 * MEM_W)
    bias_p = jnp.repeat(b_spatial[0].T, GMLP_GROUP, axis=1)
    blockdiag = lambda wg: jnp.kron(jnp.eye(c // pad, dtype=F32),
                                    jnp.pad(wg[:seq, :seq], ((0, pad - seq), (0, pad - seq))))
    w_mix_s = jax.vmap(blockdiag)(w_spatial[0])
    bias_s = jnp.tile(jnp.pad(bias_p[:seq], ((0, pad - seq), (0, 0))), (c // pad, 1))

    proj_s1, w_in1 = _norm_matmul_cast(hs1, pre_norm[1], w_in_odd[0], 1024)
    yc_s, gv_s = _gmlp(proj_s1, gmlp_ln[0], w_mix_s, bias_s, True)
    ym_s1 = _mem_attn_rows(proj_s1.reshape(nseq, pad, ODD_IN), O_MQ, O_MG, cache_mem_k, cache_mem_v, 1)
    y_s1 = jnp.concatenate([yc_s.astype(F32), ym_s1.reshape(rows_s, -1)], axis=1)
    hs2, w_out1 = _out_proj_cast(y_s1, w_out_odd[0], post_norm[1], hs1, 512)

    proj_p1 = _norm_matmul(hp1, pre_norm[1], w_in1, tm_p, 1024)
    yc_p, gv_p = _gmlp(proj_p1, gmlp_ln[0], w_spatial[0], bias_p, False)
    ym_p1 = _mem_attn(proj_p1[None], O_MQ, O_MG, memkv1[None], 0, memkv1[None], MEM_W, 1024, BF16)[0]
    hp2 = _out_proj([yc_p, ym_p1], w_out1, post_norm[1], hp1, 256)

    memkv = jnp.stack([memkv0, memkv1])
    p_mk = memkv[:, :, :MEM_W].reshape(2, bp, MEM_LEN, MEM_HEADS, MEM_HD)
    p_mv = memkv[:, :, MEM_W:].reshape(2, bp, MEM_LEN, MEM_HEADS, MEM_HD)
    return (hp2.reshape(bp, tp, d), unpad(hs2),
            st_p[None, None],
            pk.reshape(1, bp, buf_p, SWA_HEADS, SWA_HD), pv.reshape(1, bp, buf_p, SWA_HEADS, SWA_HD),
            p_mk, p_mv,
            gv_p[None, None],
            st_s[None],
            sk[None], sv[None],
            unpad(gv_s)[None])
```

```python
import functools
import math

import jax
import jax.numpy as jnp
from jax import lax
from jax.experimental import pallas as pl
from jax.experimental.pallas import tpu as pltpu

F32 = jnp.float32
BF16 = jnp.bfloat16

D_MODEL = 2048
PAST_LEN = 8192
RET_HEADS = 8
RET_DK = 128
RET_DV = 256
RET_CHUNK = 128
RET_THETA = 10000.0
SWA_HEADS = 8
SWA_HD = 128
SWA_PATTERNS = ((128, 1), (512, 4), (2048, 16))
SWA_MAX_WINDOW = 2048
SWA_BLOCK = 128
ROPE_THETA = 500000.0
ROPE_DIMS = SWA_HD // 4
GMLP_CHUNK = 128
GMLP_WIDTH = D_MODEL
GMLP_GROUP = 128
GMLP_GROUPS = GMLP_WIDTH // GMLP_GROUP
MEM_LEN = 256
MEM_HEADS = 4
MEM_HD = 128
MEM_W = MEM_HEADS * MEM_HD
NORM_EPS = 1e-6
NEG_INF = -1e30
LOG2E = math.log2(math.e)

E_AQ = 0
E_AK = E_AQ + RET_HEADS * RET_DK
E_AV = E_AK + RET_HEADS * RET_DK
E_AG = E_AV + RET_HEADS * RET_DV
E_BQ = E_AG + RET_HEADS * RET_DV
E_BK = E_BQ + SWA_HEADS * SWA_HD
E_BV = E_BK + SWA_HEADS * SWA_HD
E_BG = E_BV + SWA_HEADS * SWA_HD
E_MQ = E_BG + SWA_HEADS * SWA_HD
E_MG = E_MQ + MEM_W
EVEN_IN = E_MG + MEM_W
O_U = 0
O_V = O_U + GMLP_WIDTH
O_Z = O_V + GMLP_WIDTH
O_MQ = O_Z + GMLP_WIDTH
O_MG = O_MQ + MEM_W
ODD_IN = O_MG + MEM_W

SWA_TILE = SWA_MAX_WINDOW
SAMPLE_PAD = 8
GMLP_STEP_CHUNKS = 2
RET_STEP_CHUNKS = 2
MEM_ROWS_STEP_SEQS = 2
RET_STEP_SEQS = 2
OUT_PROJ_SUB_ROWS = 256
VMEM_LIMIT = 48 * 1024 * 1024


def _params(*sem):
    return pltpu.CompilerParams(dimension_semantics=sem, vmem_limit_bytes=VMEM_LIMIT)


def _dot(a, b):
    return jnp.dot(a, b, preferred_element_type=F32)


def _dot_nt(a, b):
    return lax.dot_general(a, b, (((1,), (1,)), ((), ())), preferred_element_type=F32)


def _silu(x):
    return x * jax.nn.sigmoid(x)


def _gelu(x):
    k = 2.0 * math.sqrt(2.0 / math.pi)
    return x * jax.nn.sigmoid(x * (k + (k * 0.044715) * (x * x)))


def _div(x, n):
    assert n & (n - 1) == 0
    return jnp.right_shift(x, n.bit_length() - 1)


def _mod(x, n):
    assert n & (n - 1) == 0
    return jnp.bitwise_and(x, n - 1)


def _rot_tables(pos, n_rot, theta, width):
    half = n_rot // 2
    inv = 1.0 / (theta ** (jnp.arange(half, dtype=F32) / half))
    ang = pos.astype(F32)[:, None] * inv[None, :]
    return _rot_assemble(jnp.cos(ang), jnp.sin(ang), width)


def _rot_tables_range(n, n_rot, theta, width, blk=128):
    assert n % blk == 0
    half = n_rot // 2
    inv = 1.0 / (theta ** (jnp.arange(half, dtype=F32) / half))
    hi = (jnp.arange(n // blk, dtype=F32) * blk)[:, None] * inv[None, :]
    lo = jnp.arange(blk, dtype=F32)[:, None] * inv[None, :]
    ch, sh, cl, sl = jnp.cos(hi)[:, None], jnp.sin(hi)[:, None], jnp.cos(lo)[None], jnp.sin(lo)[None]
    return _rot_assemble((ch * cl - sh * sl).reshape(n, half), (sh * cl + ch * sl).reshape(n, half), width)


def _rot_assemble(cos, sin, width):
    t, half = cos.shape
    cos_t = jnp.concatenate([cos, cos, jnp.ones((t, width - 2 * half), F32)], axis=-1)
    sin_t = jnp.concatenate([-sin, sin, jnp.zeros((t, width - 2 * half), F32)], axis=-1)
    return cos_t, sin_t


def _rot_full(x, cos, sin):
    return x * cos + pltpu.roll(x, RET_DK // 2, 1) * sin


def _rot_partial(x, cos, sin):
    half = ROPE_DIMS // 2
    lane = _mod(lax.broadcasted_iota(jnp.int32, x.shape, 1), SWA_HD)
    width = x.shape[1]
    partner = jnp.where(lane < half, pltpu.roll(x, width - half, 1), pltpu.roll(x, half, 1))
    return x * cos + partner * sin


def _norm_matmul_kernel(x_ref, g_ref, w_ref, o_ref, xn_ref):
    @pl.when(pl.program_id(1) == 0)
    def _():
        x = x_ref[...]
        ms = jnp.mean(x * x, axis=-1, keepdims=True)
        xn_ref[...] = (x * lax.rsqrt(ms + NORM_EPS) * g_ref[...]).astype(BF16)

    o_ref[...] = _dot(xn_ref[...], w_ref[...]).astype(o_ref.dtype)


def _norm_matmul(x, g, w, tm, tn):
    m, d = x.shape
    n = w.shape[1]
    return pl.pallas_call(
        _norm_matmul_kernel,
        grid=(m // tm, n // tn),
        in_specs=[pl.BlockSpec((tm, d), lambda i, j: (i, 0)),
                  pl.BlockSpec((1, d), lambda i, j: (0, 0)),
                  pl.BlockSpec((d, tn), lambda i, j: (0, j))],
        out_specs=pl.BlockSpec((tm, tn), lambda i, j: (i, j)),
        out_shape=jax.ShapeDtypeStruct((m, n), F32),
        scratch_shapes=[pltpu.VMEM((tm, d), BF16)],
        compiler_params=_params("arbitrary", "arbitrary"),
        name="norm_matmul",
    )(x, g.reshape(1, d), w)


def _norm_matmul_cast_kernel(x_ref, g_ref, w_ref, o_ref, wb_ref, xn_ref):
    @pl.when(pl.program_id(0) == 0)
    def _():
        x = x_ref[...]
        ms = jnp.mean(x * x, axis=-1, keepdims=True)
        xn_ref[...] = (x * lax.rsqrt(ms + NORM_EPS) * g_ref[...]).astype(BF16)

    wb = w_ref[...].astype(BF16)
    wb_ref[...] = wb
    o_ref[...] = _dot(xn_ref[...], wb).astype(o_ref.dtype)


def _norm_matmul_cast(x, g, w, tn):
    m, d = x.shape
    n = w.shape[1]
    return pl.pallas_call(
        _norm_matmul_cast_kernel,
        grid=(n // tn,),
        in_specs=[pl.BlockSpec((m, d), lambda j: (0, 0)),
                  pl.BlockSpec((1, d), lambda j: (0, 0)),
                  pl.BlockSpec((d, tn), lambda j: (0, j))],
        out_specs=[pl.BlockSpec((m, tn), lambda j: (0, j)),
                   pl.BlockSpec((d, tn), lambda j: (0, j))],
        out_shape=[jax.ShapeDtypeStruct((m, n), F32),
                   jax.ShapeDtypeStruct((d, n), BF16)],
        scratch_shapes=[pltpu.VMEM((m, d), BF16)],
        compiler_params=_params("arbitrary"),
        name="norm_matmul_cast",
    )(x, g.reshape(1, d), w)


def _ret_tables(c, reps, pad=None):
    pad = c if pad is None else pad
    rows = pad * reps
    lg = jnp.log1p(-jnp.exp2(-5.0 - jnp.arange(RET_HEADS, dtype=F32)))
    row = jnp.arange(rows, dtype=jnp.int32)
    step = (row % pad).astype(F32)
    live = (row % pad) < c
    rel = step[:, None] - step[None, :]
    keep = ((row // pad)[:, None] == (row // pad)[None, :]) & (rel >= 0) & live[:, None] & live[None, :]
    decay = jnp.where(keep[None], jnp.exp(lg[:, None, None] * jnp.maximum(rel, 0.0)[None]), 0.0)
    q_decay = jnp.where(live[None], jnp.exp(lg[:, None] * (step[None, :] + 1.0)), 0.0)
    k_decay = jnp.where(live[None], jnp.exp(lg[:, None] * (c - 1.0 - step[None, :])), 0.0)
    chunk_decay = jnp.exp(lg * c)
    qd = jnp.broadcast_to(q_decay[:, :, None], (RET_HEADS, rows, RET_DK))
    kd = jnp.broadcast_to(k_decay[:, :, None], (RET_HEADS, rows, RET_DK))
    cd = jnp.broadcast_to(chunk_decay[:, None, None], (RET_HEADS, 1, RET_DV))
    return decay, qd, kd, cd


def _group_norm_gate(o, gn, g):
    mu = jnp.mean(o, axis=-1, keepdims=True)
    d = o - mu
    var = jnp.mean(d * d, axis=-1, keepdims=True)
    return d * lax.rsqrt(var + NORM_EPS) * gn * _silu(g)


def _ret_prompt_kernel(q_ref, k_ref, v_ref, g_ref, cos_ref, sin_ref, dec_ref, qd_ref, kd_ref, cd_ref, gn_ref,
                       y_ref, st_ref):
    @pl.when(pl.program_id(0) == 0)
    def _():
        st_ref[...] = jnp.zeros_like(st_ref)

    for i in range(q_ref.shape[0] // RET_CHUNK):
        rs = slice(i * RET_CHUNK, (i + 1) * RET_CHUNK)
        cos = cos_ref[rs, :]
        sin = sin_ref[rs, :]
        for h in range(RET_HEADS):
            ks = slice(h * RET_DK, (h + 1) * RET_DK)
            vs = slice(h * RET_DV, (h + 1) * RET_DV)
            q = _rot_full(q_ref[rs, ks], cos, sin)
            k = _rot_full(k_ref[rs, ks] * (RET_DK ** -0.5), cos, sin)
            v = v_ref[rs, vs].astype(BF16)
            st = st_ref[h]
            s = _dot_nt(q.astype(BF16), k.astype(BF16)) * dec_ref[h]
            lhs = jnp.concatenate([s.astype(BF16), (q * qd_ref[h]).astype(BF16)], axis=1)
            o = _dot(lhs, jnp.concatenate([v, st.astype(BF16)], axis=0))
            st_ref[h] = cd_ref[h] * st + _dot((k * kd_ref[h]).T.astype(BF16), v)
            y_ref[rs, vs] = _group_norm_gate(o, gn_ref[:, vs], g_ref[rs, vs]).astype(y_ref.dtype)


def _ret_prompt(proj, cos, sin, ret_gn):
    t = proj.shape[0]
    dec, qd, kd, cd = _ret_tables(RET_CHUNK, 1)
    c = RET_STEP_CHUNKS * RET_CHUNK
    assert t % c == 0
    qw = RET_HEADS * RET_DK
    vw = RET_HEADS * RET_DV
    full3 = lambda n: (0, 0, 0)
    return pl.pallas_call(
        _ret_prompt_kernel,
        grid=(t // c,),
        in_specs=[pl.BlockSpec((c, qw), lambda n: (n, E_AQ // qw)),
                  pl.BlockSpec((c, qw), lambda n: (n, E_AK // qw)),
                  pl.BlockSpec((c, vw), lambda n: (n, E_AV // vw)),
                  pl.BlockSpec((c, vw), lambda n: (n, E_AG // vw)),
                  pl.BlockSpec((c, RET_DK), lambda n: (n, 0)),
                  pl.BlockSpec((c, RET_DK), lambda n: (n, 0)),
                  pl.BlockSpec(dec.shape, full3),
                  pl.BlockSpec(qd.shape, full3),
                  pl.BlockSpec(kd.shape, full3),
                  pl.BlockSpec(cd.shape, full3),
                  pl.BlockSpec((1, vw), lambda n: (0, 0))],
        out_specs=[pl.BlockSpec((c, vw), lambda n: (n, 0)),
                   pl.BlockSpec((RET_HEADS, RET_DK, RET_DV), full3)],
        out_shape=[jax.ShapeDtypeStruct((t, vw), BF16),
                   jax.ShapeDtypeStruct((RET_HEADS, RET_DK, RET_DV), F32)],
        compiler_params=_params("arbitrary"),
        name="ret_prompt",
    )(proj, proj, proj, proj, cos, sin, dec, qd, kd, cd, ret_gn.reshape(1, vw))


def _ret_sample_kernel(q_ref, k_ref, v_ref, g_ref, cos_ref, sin_ref, dec_ref, qd_ref, kd_ref, cd_ref, gn_ref,
                       st_ref, y_ref, sto_ref, cross_ref, qs_ref, kt_ref, *, pad):
    b = pl.program_id(0)
    nb = pl.num_programs(0)
    win = RET_CHUNK

    @pl.when(b == 0)
    def _():
        cross_ref[...] = jnp.zeros_like(cross_ref)
        cos = cos_ref[...]
        sin = sin_ref[...]
        for h in range(RET_HEADS):
            ks = slice(h * RET_DK, (h + 1) * RET_DK)
            qs_ref[:, ks] = _rot_full(q_ref[:, ks], cos, sin) * qd_ref[h]
            kt_ref[h] = (_rot_full(k_ref[:, ks] * (RET_DK ** -0.5), cos, sin) * kd_ref[h]).T

    for i in range(st_ref.shape[0]):
        sq = b * st_ref.shape[0] + i
        w0 = pl.multiple_of(jnp.right_shift(sq * pad, win.bit_length() - 1) * win, win)
        wrows = pl.ds(w0, win)
        rows = lax.broadcasted_iota(jnp.int32, (win, RET_DK), 0) + w0
        cols = lax.broadcasted_iota(jnp.int32, (RET_DK, win), 1) + w0
        mine_r = (rows >= sq * pad) & (rows < (sq + 1) * pad)
        mine_c = (cols >= sq * pad) & (cols < (sq + 1) * pad)
        for h in range(RET_HEADS):
            ks = slice(h * RET_DK, (h + 1) * RET_DK)
            vs = slice(h * RET_DV, (h + 1) * RET_DV)
            st = st_ref[i, h]
            qm = jnp.where(mine_r, qs_ref[wrows, ks], 0.0)
            kmt = jnp.where(mine_c, kt_ref[h, :, wrows], 0.0)
            cross_ref[wrows, vs] += _dot(qm.astype(BF16), st.astype(BF16))
            sto_ref[i, h] = cd_ref[h] * st + _dot(kmt.astype(BF16), v_ref[wrows, vs].astype(BF16))

    @pl.when(b == nb - 1)
    def _():
        cos_all = cos_ref[...]
        sin_all = sin_ref[...]
        for h in range(RET_HEADS):
            ks = slice(h * RET_DK, (h + 1) * RET_DK)
            vs = slice(h * RET_DV, (h + 1) * RET_DV)
            q = _rot_full(q_ref[:, ks], cos_all, sin_all)
            k = _rot_full(k_ref[:, ks] * (RET_DK ** -0.5), cos_all, sin_all)
            v = v_ref[:, vs].astype(BF16)
            s = _dot_nt(q.astype(BF16), k.astype(BF16)) * dec_ref[h]
            o = _dot(s.astype(BF16), v) + cross_ref[:, vs]
            y_ref[:, vs] = _group_norm_gate(o, gn_ref[:, vs], g_ref[:, vs]).astype(y_ref.dtype)


def _ret_sample(proj, state, cos, sin, ret_gn, nseq, seq, pad):
    rows = nseq * pad
    c = math.gcd(seq, RET_CHUNK)
    assert c == seq, "sample chunk must cover the new tokens"
    assert rows % RET_CHUNK == 0 and RET_CHUNK % pad == 0
    dec, qd, kd, cd = _ret_tables(c, nseq, pad)
    qw = RET_HEADS * RET_DK
    vw = RET_HEADS * RET_DV
    full3 = lambda b: (0, 0, 0)
    n = RET_STEP_SEQS
    assert nseq % n == 0
    st_spec = pl.BlockSpec((n, RET_HEADS, RET_DK, RET_DV), lambda b: (b, 0, 0, 0))
    return pl.pallas_call(
        functools.partial(_ret_sample_kernel, pad=pad),
        grid=(nseq // n,),
        in_specs=[pl.BlockSpec((rows, qw), lambda b: (0, E_AQ // qw)),
                  pl.BlockSpec((rows, qw), lambda b: (0, E_AK // qw)),
                  pl.BlockSpec((rows, vw), lambda b: (0, E_AV // vw)),
                  pl.BlockSpec((rows, vw), lambda b: (0, E_AG // vw)),
                  pl.BlockSpec((rows, RET_DK), lambda b: (0, 0)),
                  pl.BlockSpec((rows, RET_DK), lambda b: (0, 0)),
                  pl.BlockSpec(dec.shape, full3),
                  pl.BlockSpec(qd.shape, full3),
                  pl.BlockSpec(kd.shape, full3),
                  pl.BlockSpec(cd.shape, full3),
                  pl.BlockSpec((1, vw), lambda b: (0, 0)),
                  st_spec],
        out_specs=[pl.BlockSpec((rows, vw), lambda b: (0, 0)), st_spec],
        out_shape=[jax.ShapeDtypeStruct((rows, vw), F32),
                   jax.ShapeDtypeStruct(state.shape, F32)],
        scratch_shapes=[pltpu.VMEM((rows, vw), F32),
                        pltpu.VMEM((rows, qw), F32),
                        pltpu.VMEM((RET_HEADS, RET_DK, rows), F32)],
        compiler_params=_params("arbitrary"),
        name="ret_sample",
    )(proj, proj, proj, proj, cos, sin, dec, qd, kd, cd, ret_gn.reshape(1, vw), state)


def _swa_prompt_kernel(q_ref, kc_ref, vc_ref, g_ref, cos_ref, sin_ref,
                       y_ref, ko_ref, q_s, k_s, v_s, m_s, l_s, acc_s):
    t = pl.program_id(1)
    nt = pl.num_programs(1)
    tb = SWA_TILE
    blk = SWA_BLOCK
    hd = SWA_HD
    cos = cos_ref[...]
    sin = sin_ref[...]

    @pl.when(t == 0)
    def _():
        k_s[0:tb, :] = jnp.zeros((tb, hd), F32)
        v_s[0:tb, :] = jnp.zeros((tb, hd), F32)

    @pl.when(t > 0)
    def _():
        k_s[0:tb, :] = k_s[tb:2 * tb, :]
        v_s[0:tb, :] = v_s[tb:2 * tb, :]

    q_s[...] = _rot_partial(q_ref[...] * (SWA_HD ** -0.5 * LOG2E), cos, sin)
    kr = _rot_partial(kc_ref[...], cos, sin)
    k_s[tb:2 * tb, :] = kr
    v_s[tb:2 * tb, :] = vc_ref[...]

    @pl.when(t == nt - 1)
    def _():
        ko_ref[...] = kr

    qi = lax.broadcasted_iota(jnp.int32, (blk, 2 * blk), 0)
    kj = lax.broadcasted_iota(jnp.int32, (blk, 2 * blk), 1)
    band = (kj >= qi) & (kj <= qi + blk)
    band_first = band & ((kj >= blk) | (t > 0))
    ones = jnp.ones((2 * blk, hd), BF16)

    def ds(start, size, stride):
        return pl.ds(start, size) if stride == 1 else pl.ds(start, size, stride=stride)

    for bi, (window, r) in enumerate(SWA_PATTERNS):
        assert window // r == blk
        for idx in range(tb // blk):
            c, n = idx % r, idx // r
            qs = n * blk * r + c
            ks = qs + tb - blk * r
            qrows = ds(qs, blk, r)
            krows = ds(ks, 2 * blk, r)
            valid = band_first if n == 0 else band
            s = _dot_nt(q_s[qrows, :].astype(BF16), k_s[krows, :].astype(BF16))
            s = jnp.where(valid, s, NEG_INF)
            m = jnp.max(s, axis=-1, keepdims=True)
            p = jnp.exp2(s - m).astype(BF16)
            pv = _dot(p, jnp.concatenate([v_s[krows, :].astype(BF16), ones], axis=1))
            m_s[bi, qrows, :] = jnp.broadcast_to(m, (blk, hd))
            l_s[bi, qrows, :] = pv[:, hd:]
            acc_s[bi, qrows, :] = pv[:, :hd]

    nbr = len(SWA_PATTERNS)
    ms = [m_s[i] for i in range(nbr)]
    mmax = functools.reduce(jnp.maximum, ms)
    wts = [jnp.exp2(mi - mmax) for mi in ms]
    num = sum(wts[i] * acc_s[i] for i in range(nbr))
    den = sum(wts[i] * l_s[i] for i in range(nbr))
    y_ref[...] = (num / den * _silu(g_ref[...])).astype(y_ref.dtype)


def _swa_prompt(proj, cos, sin):
    t = proj.shape[0]
    tb = SWA_TILE
    assert t % tb == 0
    hd = SWA_HD
    nbr = len(SWA_PATTERNS)
    cur = lambda off: pl.BlockSpec((tb, hd), lambda h, i: (i, off // hd + h))
    return pl.pallas_call(
        _swa_prompt_kernel,
        grid=(SWA_HEADS, t // tb),
        in_specs=[cur(E_BQ), cur(E_BK), cur(E_BV), cur(E_BG),
                  pl.BlockSpec((tb, hd), lambda h, i: (i, 0)),
                  pl.BlockSpec((tb, hd), lambda h, i: (i, 0))],
        out_specs=[pl.BlockSpec((tb, hd), lambda h, i: (i, h)),
                   pl.BlockSpec((tb, hd), lambda h, i: (0, h))],
        out_shape=[jax.ShapeDtypeStruct((t, SWA_HEADS * hd), BF16),
                   jax.ShapeDtypeStruct((tb, SWA_HEADS * hd), F32)],
        scratch_shapes=[pltpu.VMEM((tb, hd), F32),
                        pltpu.VMEM((2 * tb, hd), F32),
                        pltpu.VMEM((2 * tb, hd), F32),
                        pltpu.VMEM((nbr, tb, hd), F32),
                        pltpu.VMEM((nbr, tb, hd), F32),
                        pltpu.VMEM((nbr, tb, hd), F32)],
        compiler_params=_params("arbitrary", "arbitrary"),
        name="swa_prompt",
    )(proj, proj, proj, proj, cos, sin)


def _swa_sample_kernel(q_ref, k_ref, v_ref, g_ref, cos_ref, sin_ref, ka_ref, kb_ref, va_ref, vb_ref,
                       y_ref, ko_ref, kn_s, vn_s, ba_s, bb_s, bn_s, *, seq):
    pad = SAMPLE_PAD
    nh = SWA_HEADS
    hd = SWA_HD
    nr = nh * pad
    nj, nrh, _ = kb_ref.shape[1:]
    na = ka_ref.shape[1]
    nb = nj * nrh
    nn = kn_s.shape[0]

    @pl.when(pl.program_id(0) == 0)
    def _():
        (w1, d1), (w2, d2), (w3, d3) = SWA_PATTERNS

        def rows_cols(n):
            r = lax.broadcasted_iota(jnp.int32, (nr, n), 0)
            c = lax.broadcasted_iota(jnp.int32, (nr, n), 1)
            return _div(r, pad), _mod(r, pad), c

        def log2_mult(mult):
            return jnp.where(mult > 0, jnp.log2(jnp.maximum(mult, 1.0)), NEG_INF)

        hq, tq, c = rows_cols(na)
        dist = na // nh + tq - _div(c, nh)
        mult_a = ((dist <= w1).astype(F32) + ((_mod(dist, d2) == 0) & (dist <= w2)).astype(F32)
                  + ((_mod(dist, d3) == 0) & (dist <= w3)).astype(F32))
        ba_s[...] = log2_mult(jnp.where((hq == _mod(c, nh)) & (tq < seq), mult_a, 0.0))
        hq, tq, c = rows_cols(nb)
        bb_s[...] = log2_mult(jnp.where((hq == _mod(c, nh)) & (tq == _mod(_div(c, nh), nrh // nh)), 1.0, 0.0))
        hq, tq, c = rows_cols(nn)
        tk = _mod(c, pad)
        mult_n = (tk <= tq).astype(F32) + 2.0 * (tk == tq).astype(F32)
        bn_s[...] = log2_mult(jnp.where((hq == _div(c, pad)) & (tq < seq) & (tk < seq), mult_n, 0.0))
        kn_s[...] = jnp.zeros_like(kn_s)
        vn_s[...] = jnp.zeros_like(vn_s)

    cos = cos_ref[...]
    sin = sin_ref[...]
    q8 = _rot_partial(q_ref[0] * (SWA_HD ** -0.5 * LOG2E), cos, sin)
    k8 = _rot_partial(k_ref[0], cos, sin)
    ko_ref[0] = k8

    def head_rows(x):
        return jnp.concatenate([x[:, h * hd:(h + 1) * hd] for h in range(nh)], axis=0)

    qrows = head_rows(q8).astype(BF16)
    kn_s[0:nr, :] = head_rows(k8).astype(BF16)
    vn_s[0:nr, :] = head_rows(v_ref[0]).astype(BF16)

    kb = kb_ref[0].reshape(nb, hd)
    vb = vb_ref[0].reshape(nb, hd)
    s_a = _dot_nt(qrows, ka_ref[0].astype(BF16)) + ba_s[...]
    s_b = _dot_nt(qrows, kb.astype(BF16)) + bb_s[...]
    s_n = _dot_nt(qrows, kn_s[...]) + bn_s[...]
    rowmax = lambda s: jnp.max(s, axis=-1, keepdims=True)
    m = jnp.maximum(jnp.maximum(rowmax(s_a), rowmax(s_b)), rowmax(s_n))
    p_a, p_b, p_n = jnp.exp2(s_a - m), jnp.exp2(s_b - m), jnp.exp2(s_n - m)
    rowsum = lambda p: jnp.sum(p, axis=-1, keepdims=True)
    o = (_dot(p_a.astype(BF16), va_ref[0].astype(BF16)) + _dot(p_b.astype(BF16), vb.astype(BF16))
         + _dot(p_n.astype(BF16), vn_s[...])) / (rowsum(p_a) + rowsum(p_b) + rowsum(p_n))
    y = jnp.concatenate([o[h * pad:(h + 1) * pad] for h in range(nh)], axis=1)
    y_ref[0] = y * _silu(g_ref[0])


def _swa_sample(proj_pad, cache_k, cache_v, cos, sin, seq):
    nseq, pad, _ = proj_pad.shape
    buf = cache_k.shape[1]
    nh, hd = SWA_HEADS, SWA_HD
    w = nh * hd
    (w1, d1), (w2, d2), (w3, d3) = SWA_PATTERNS
    assert buf == w3 and d1 == 1 and seq <= d2 and w1 <= w2 and buf % w2 == 0 and buf % d3 == 0
    assert nh * pad <= SWA_BLOCK and (seq * nh) % 8 == 0
    flat = lambda a: a.reshape(nseq, buf * nh, hd)
    strided = lambda a: a.reshape(nseq, buf // d3, d3 * nh, hd)
    col = lambda off: pl.BlockSpec((1, pad, w), lambda b: (b, 0, off // w))
    tail = pl.BlockSpec((1, w2 * nh, hd), lambda b: (b, buf // w2 - 1, 0))
    resid = pl.BlockSpec((1, (buf - w2) // d3, seq * nh, hd), lambda b: (b, 0, 0, 0))
    tab = pl.BlockSpec((pad, w), lambda b: (0, 0))
    out = pl.BlockSpec((1, pad, w), lambda b: (b, 0, 0))
    return pl.pallas_call(
        functools.partial(_swa_sample_kernel, seq=seq),
        grid=(nseq,),
        in_specs=[col(E_BQ), col(E_BK), col(E_BV), col(E_BG), tab, tab, tail, resid, tail, resid],
        out_specs=[out, out],
        out_shape=[jax.ShapeDtypeStruct((nseq, pad, w), F32),
                   jax.ShapeDtypeStruct((nseq, pad, w), F32)],
        scratch_shapes=[pltpu.VMEM((SWA_BLOCK, hd), BF16), pltpu.VMEM((SWA_BLOCK, hd), BF16),
                        pltpu.VMEM((nh * pad, w2 * nh), F32),
                        pltpu.VMEM((nh * pad, (buf - w2) // d3 * seq * nh), F32),
                        pltpu.VMEM((nh * pad, SWA_BLOCK), F32)],
        compiler_params=_params("arbitrary"),
        name="swa_sample",
    )(proj_pad, proj_pad, proj_pad, proj_pad, cos, sin,
      flat(cache_k), strided(cache_k), flat(cache_v), strided(cache_v))


def _mem_attn_kernel(q_ref, g_ref, k_ref, v_ref, o_ref):
    ones = jnp.ones((MEM_LEN, MEM_HD), BF16)
    for h in range(MEM_HEADS):
        hs = slice(h * MEM_HD, (h + 1) * MEM_HD)
        q = (q_ref[0, :, hs] * (MEM_HD ** -0.5 * LOG2E)).astype(BF16)
        s = _dot_nt(q, k_ref[0, :, hs].astype(BF16))
        m = jnp.max(s, axis=-1, keepdims=True)
        p = jnp.exp2(s - m).astype(BF16)
        pv = _dot(p, jnp.concatenate([v_ref[0, :, hs].astype(BF16), ones], axis=1))
        o = pv[:, :MEM_HD] / pv[:, MEM_HD:]
        o_ref[0, :, hs] = (o * _silu(g_ref[0, :, hs])).astype(o_ref.dtype)


def _mem_attn_rows_kernel(q_ref, g_ref, k_ref, v_ref, o_ref, bias_s):
    nseq, pad, _ = q_ref.shape
    nh, hd = MEM_HEADS, MEM_HD

    @pl.when(pl.program_id(0) == 0)
    def _():
        r = lax.broadcasted_iota(jnp.int32, bias_s.shape, 0)
        c = lax.broadcasted_iota(jnp.int32, bias_s.shape, 1)
        bias_s[...] = jnp.where(_div(r, pad) == _mod(c, nh), 0.0, NEG_INF)

    for i in range(nseq):
        q = q_ref[i] * (MEM_HD ** -0.5 * LOG2E)
        qrows = jnp.concatenate([q[:, h * hd:(h + 1) * hd] for h in range(nh)], axis=0).astype(BF16)
        s = _dot_nt(qrows, k_ref[i].astype(BF16)) + bias_s[...]
        m = jnp.max(s, axis=-1, keepdims=True)
        p = jnp.exp2(s - m)
        o = _dot(p.astype(BF16), v_ref[i].astype(BF16)) / jnp.sum(p, axis=-1, keepdims=True)
        y = jnp.concatenate([o[h * pad:(h + 1) * pad] for h in range(nh)], axis=1)
        o_ref[i] = y * _silu(g_ref[i])


def _mem_attn_rows(proj_pad, q_off, g_off, mem_k, mem_v, layer):
    b, pad, _ = proj_pad.shape
    w = MEM_W
    rows = MEM_LEN * MEM_HEADS
    n = MEM_ROWS_STEP_SEQS
    assert b % n == 0
    kv = pl.BlockSpec((n, rows, MEM_HD), lambda i: (layer * (b // n) + i, 0, 0))
    return pl.pallas_call(
        _mem_attn_rows_kernel,
        grid=(b // n,),
        in_specs=[pl.BlockSpec((n, pad, w), lambda i: (i, 0, q_off // w)),
                  pl.BlockSpec((n, pad, w), lambda i: (i, 0, g_off // w)),
                  kv, kv],
        out_specs=pl.BlockSpec((n, pad, w), lambda i: (i, 0, 0)),
        out_shape=jax.ShapeDtypeStruct((b, pad, w), F32),
        scratch_shapes=[pltpu.VMEM((MEM_HEADS * pad, rows), F32)],
        compiler_params=_params("arbitrary"),
        name="mem_attn_rows",
    )(proj_pad, proj_pad, mem_k.reshape(-1, rows, MEM_HD), mem_v.reshape(-1, rows, MEM_HD))


def _mem_attn(proj, q_off, g_off, mem_k, k_off, mem_v, v_off, tq, out_dtype):
    b, t, _ = proj.shape
    w = MEM_W
    return pl.pallas_call(
        _mem_attn_kernel,
        grid=(b, t // tq),
        in_specs=[pl.BlockSpec((1, tq, w), lambda i, j: (i, j, q_off // w)),
                  pl.BlockSpec((1, tq, w), lambda i, j: (i, j, g_off // w)),
                  pl.BlockSpec((1, MEM_LEN, w), lambda i, j: (i, 0, k_off // w)),
                  pl.BlockSpec((1, MEM_LEN, w), lambda i, j: (i, 0, v_off // w))],
        out_specs=pl.BlockSpec((1, tq, w), lambda i, j: (i, j, 0)),
        out_shape=jax.ShapeDtypeStruct((b, t, w), out_dtype),
        compiler_params=_params("arbitrary", "arbitrary"),
        name="mem_attn",
    )(proj, proj, mem_k, mem_v)


def _gmlp_kernel(u_ref, v_ref, z_ref, ln_ref, w_ref, b_ref, y_ref, vo_ref, wc_ref):
    c = GMLP_CHUNK

    @pl.when(pl.program_id(0) == 0)
    def _():
        causal = (lax.broadcasted_iota(jnp.int32, (c, c), 0) >= lax.broadcasted_iota(jnp.int32, (c, c), 1))
        for g in range(GMLP_GROUPS):
            wc_ref[g] = jnp.where(causal, w_ref[g], 0.0).astype(BF16)

    nchunks = u_ref.shape[0] // c
    keep_all = vo_ref.shape[0] == u_ref.shape[0]
    for i in range(nchunks):
        rs = slice(i * c, (i + 1) * c)
        v = _gelu(v_ref[rs, :])
        mu = jnp.mean(v, axis=-1, keepdims=True)
        d = v - mu
        var = jnp.mean(d * d, axis=-1, keepdims=True)
        vn = d * lax.rsqrt(var + NORM_EPS) * ln_ref[...]
        if keep_all:
            vo_ref[rs, :] = vn
        elif i == nchunks - 1:
            vo_ref[...] = vn
        for g in range(GMLP_GROUPS):
            gs = slice(g * GMLP_GROUP, (g + 1) * GMLP_GROUP)
            mixed = _dot(wc_ref[g], vn[:, gs].astype(BF16)) + b_ref[:, gs]
            y_ref[rs, gs] = (_gelu(u_ref[rs, gs]) * mixed * _silu(z_ref[rs, gs])).astype(y_ref.dtype)


def _gmlp(proj, gmlp_ln, w_mix, bias_rows, keep_all_v):
    t = proj.shape[0]
    c = GMLP_CHUNK
    w = GMLP_WIDTH
    rows = GMLP_STEP_CHUNKS * c
    assert t % rows == 0
    col = lambda off: pl.BlockSpec((rows, w), lambda n: (n, off // w))
    return pl.pallas_call(
        _gmlp_kernel,
        grid=(t // rows,),
        in_specs=[col(O_U), col(O_V), col(O_Z),
                  pl.BlockSpec((1, w), lambda n: (0, 0)),
                  pl.BlockSpec(w_mix.shape, lambda n: (0, 0, 0)),
                  pl.BlockSpec((c, w), lambda n: (0, 0))],
        out_specs=[pl.BlockSpec((rows, w), lambda n: (n, 0)),
                   pl.BlockSpec((rows, w), lambda n: (n, 0)) if keep_all_v
                   else pl.BlockSpec((c, w), lambda n: (0, 0))],
        out_shape=[jax.ShapeDtypeStruct((t, w), BF16),
                   jax.ShapeDtypeStruct((t if keep_all_v else c, w), F32)],
        scratch_shapes=[pltpu.VMEM(w_mix.shape, BF16)],
        compiler_params=_params("arbitrary"),
        name="gmlp",
    )(proj, proj, proj, gmlp_ln.reshape(1, w), w_mix, bias_rows)


def _out_proj_kernel(*refs, n_in):
    a_refs, w_refs = refs[:n_in], refs[n_in:2 * n_in]
    g_ref, x_ref, o_ref = refs[2 * n_in:]
    sub = min(OUT_PROJ_SUB_ROWS, o_ref.shape[0])
    for r0 in range(0, o_ref.shape[0], sub):
        rs = slice(r0, r0 + sub)
        acc = _dot(a_refs[0][rs, :].astype(BF16), w_refs[0][...])
        for a_ref, w_ref in zip(a_refs[1:], w_refs[1:]):
            acc = acc + _dot(a_ref[rs, :].astype(BF16), w_ref[...])
        ms = jnp.mean(acc * acc, axis=-1, keepdims=True)
        o_ref[rs, :] = x_ref[rs, :] + acc * lax.rsqrt(ms + NORM_EPS) * g_ref[...]


def _out_proj(acts, w, g, x, tm):
    m, d = x.shape
    n_in = len(acts)
    w_specs, off = [], 0
    for a in acts:
        k = a.shape[1]
        assert off % k == 0
        w_specs.append(pl.BlockSpec((k, d), functools.partial(lambda i, blk: (blk, 0), blk=off // k),
                                    pipeline_mode=pl.Buffered(1)))
        off += k
    assert off == w.shape[0]
    return pl.pallas_call(
        functools.partial(_out_proj_kernel, n_in=n_in),
        grid=(m // tm,),
        in_specs=([pl.BlockSpec((tm, a.shape[1]), lambda i: (i, 0)) for a in acts]
                  + w_specs
                  + [pl.BlockSpec((1, d), lambda i: (0, 0)),
                     pl.BlockSpec((tm, d), lambda i: (i, 0))]),
        out_specs=pl.BlockSpec((tm, d), lambda i: (i, 0)),
        out_shape=jax.ShapeDtypeStruct((m, d), F32),
        compiler_params=_params("arbitrary"),
        name="out_proj",
    )(*acts, *([w] * n_in), g.reshape(1, d), x)


def _out_proj_cast_kernel(a_ref, w_ref, g_ref, x_ref, o_ref, wb_ref):
    k = pl.program_id(0)

    @pl.when(k == 0)
    def _():
        o_ref[...] = jnp.zeros_like(o_ref)

    wb = w_ref[...].astype(BF16)
    wb_ref[...] = wb
    o_ref[...] += _dot(a_ref[...].astype(BF16), wb)

    @pl.when(k == pl.num_programs(0) - 1)
    def _():
        acc = o_ref[...]
        ms = jnp.mean(acc * acc, axis=-1, keepdims=True)
        o_ref[...] = x_ref[...] + acc * lax.rsqrt(ms + NORM_EPS) * g_ref[...]


def _out_proj_cast(a, w, g, x, tk):
    m, d = x.shape
    kdim = w.shape[0]
    assert kdim % tk == 0 and a.shape == (m, kdim)
    return pl.pallas_call(
        _out_proj_cast_kernel,
        grid=(kdim // tk,),
        in_specs=[pl.BlockSpec((m, tk), lambda k: (0, k)),
                  pl.BlockSpec((tk, d), lambda k: (k, 0)),
                  pl.BlockSpec((1, d), lambda k: (0, 0)),
                  pl.BlockSpec((m, d), lambda k: (0, 0))],
        out_specs=[pl.BlockSpec((m, d), lambda k: (0, 0)),
                   pl.BlockSpec((tk, d), lambda k: (k, 0))],
        out_shape=[jax.ShapeDtypeStruct((m, d), F32),
                   jax.ShapeDtypeStruct((kdim, d), BF16)],
        compiler_params=_params("arbitrary"),
        name="out_proj_cast",
    )(a, w, g.reshape(1, d), x)


def _pad_steps(a, pad):
    nseq, seq, n = a.shape
    return jnp.pad(a, ((0, 0), (0, pad - seq), (0, 0))).reshape(nseq * pad, n)


def kernel(x_prompt, x_sample, state_ret, cache_swa_k, cache_swa_v, cache_mem_k, cache_mem_v, mem_prompt,
           pre_norm, post_norm, mem_norm, w_mem_k, w_mem_v, w_in_even, ret_gn, w_out_even,
           w_in_odd, gmlp_ln, w_spatial, b_spatial, w_out_odd):
    bp, tp, d = x_prompt.shape
    nseq, seq, _ = x_sample.shape
    pad = SAMPLE_PAD
    rows_s = nseq * pad
    c = GMLP_CHUNK
    assert bp == 1 and seq <= pad and rows_s % c == 0 and c % pad == 0
    hp = x_prompt.reshape(tp, d)
    hs = _pad_steps(x_sample, pad)
    mem = mem_prompt.reshape(MEM_LEN, d)
    tm_p = 1024 if tp % 1024 == 0 else 512
    unpad = lambda a: a.reshape(nseq, pad, -1)[:, :seq]

    pos_s = PAST_LEN + jnp.arange(seq, dtype=jnp.int32)
    pos_pad = jnp.concatenate([pos_s, jnp.zeros((pad - seq,), jnp.int32)])
    ret_cos_p, ret_sin_p = _rot_tables_range(tp, RET_DK, RET_THETA, RET_DK)
    ret_cos_s, ret_sin_s = _rot_tables(jnp.tile(pos_pad, nseq), RET_DK, RET_THETA, RET_DK)
    swa_cos_p, swa_sin_p = _rot_tables_range(tp, ROPE_DIMS, ROPE_THETA, SWA_HD)
    swa_cos_s, swa_sin_s = _rot_tables(pos_pad, ROPE_DIMS, ROPE_THETA, SWA_HD)
    swa_cos_s = jnp.tile(swa_cos_s, (1, SWA_HEADS))
    swa_sin_s = jnp.tile(swa_sin_s, (1, SWA_HEADS))

    w_mem0 = jnp.concatenate([w_mem_k[0], w_mem_v[0]], axis=1).astype(BF16)
    memkv0 = _norm_matmul(mem, mem_norm[0], w_mem0, MEM_LEN, 2 * MEM_W)

    proj_s, w_in0 = _norm_matmul_cast(hs, pre_norm[0], w_in_even[0], 1024)
    proj_s3 = proj_s.reshape(nseq, pad, EVEN_IN)
    ya_s, st_s = _ret_sample(proj_s, state_ret[0], ret_cos_s, ret_sin_s, ret_gn[0], nseq, seq, pad)
    yb_s, sk = _swa_sample(proj_s3, cache_swa_k[0], cache_swa_v[0], swa_cos_s, swa_sin_s, seq)
    ym_s = _mem_attn_rows(proj_s3, E_MQ, E_MG, cache_mem_k, cache_mem_v, 0)
    y_s = jnp.concatenate([ya_s, yb_s.reshape(rows_s, -1), ym_s.reshape(rows_s, -1)], axis=1)
    hs1, w_out0 = _out_proj_cast(y_s, w_out_even[0], post_norm[0], hs, 512)
    sk = sk[:, :seq].reshape(nseq, seq, SWA_HEADS, SWA_HD)
    sv = unpad(proj_s[:, E_BV:E_BV + SWA_HEADS * SWA_HD]).reshape(nseq, seq, SWA_HEADS, SWA_HD)

    proj_p = _norm_matmul(hp, pre_norm[0], w_in0, tm_p, 1024)
    ya_p, st_p = _ret_prompt(proj_p, ret_cos_p, ret_sin_p, ret_gn[0])
    yb_p, pk = _swa_prompt(proj_p, swa_cos_p, swa_sin_p)
    ym_p = _mem_attn(proj_p[None], E_MQ, E_MG, memkv0[None], 0, memkv0[None], MEM_W, 1024, BF16)[0]
    hp1 = _out_proj([ya_p, yb_p, ym_p], w_out0, post_norm[0], hp, 512)
    buf_p = min(SWA_MAX_WINDOW, tp)
    pv = proj_p[tp - buf_p:, E_BV:E_BV + SWA_HEADS * SWA_HD]

    w_mem1 = jnp.concatenate([w_mem_k[1], w_mem_v[1]], axis=1).astype(BF16)
    memkv1 = _norm_matmul(mem, mem_norm[1], w_mem1, MEM_LEN, 2 * MEM_W)
    bias_p = jnp.repeat(b_spatial[0].T, GMLP_GROUP, axis=1)
    blockdiag = lambda wg: jnp.kron(jnp.eye(c // pad, dtype=F32),
                                    jnp.pad(wg[:seq, :seq], ((0, pad - seq), (0, pad - seq))))
    w_mix_s = jax.vmap(blockdiag)(w_spatial[0])
    bias_s = jnp.tile(jnp.pad(bias_p[:seq], ((0, pad - seq), (0, 0))), (c // pad, 1))

    proj_s1, w_in1 = _norm_matmul_cast(hs1, pre_norm[1], w_in_odd[0], 1024)
    yc_s, gv_s = _gmlp(proj_s1, gmlp_ln[0], w_mix_s, bias_s, True)
    ym_s1 = _mem_attn_rows(proj_s1.reshape(nseq, pad, ODD_IN), O_MQ, O_MG, cache_mem_k, cache_mem_v, 1)
    y_s1 = jnp.concatenate([yc_s.astype(F32), ym_s1.reshape(rows_s, -1)], axis=1)
    hs2, w_out1 = _out_proj_cast(y_s1, w_out_odd[0], post_norm[1], hs1, 512)

    proj_p1 = _norm_matmul(hp1, pre_norm[1], w_in1, tm_p, 1024)
    yc_p, gv_p = _gmlp(proj_p1, gmlp_ln[0], w_spatial[0], bias_p, False)
    ym_p1 = _mem_attn(proj_p1[None], O_MQ, O_MG, memkv1[None], 0, memkv1[None], MEM_W, 1024, BF16)[0]
    hp2 = _out_proj([yc_p, ym_p1], w_out1, post_norm[1], hp1, 512)

    memkv = jnp.stack([memkv0, memkv1])
    p_mk = memkv[:, :, :MEM_W].reshape(2, bp, MEM_LEN, MEM_HEADS, MEM_HD)
    p_mv = memkv[:, :, MEM_W:].reshape(2, bp, MEM_LEN, MEM_HEADS, MEM_HD)
    return (hp2.reshape(bp, tp, d), unpad(hs2),
            st_p[None, None],
            pk.reshape(1, bp, buf_p, SWA_HEADS, SWA_HD), pv.reshape(1, bp, buf_p, SWA_HEADS, SWA_HD),
            p_mk, p_mv,
            gv_p[None, None],
            st_s[None],
            sk[None], sv[None],
            unpad(gv_s)[None])
```

```python
import functools
import math

import jax
import jax.numpy as jnp
from jax import lax
from jax.experimental import pallas as pl
from jax.experimental.pallas import tpu as pltpu

F32 = jnp.float32
BF16 = jnp.bfloat16

D_MODEL = 2048
PAST_LEN = 8192
RET_HEADS = 8
RET_DK = 128
RET_DV = 256
RET_CHUNK = 128
RET_THETA = 10000.0
SWA_HEADS = 8
SWA_HD = 128
SWA_PATTERNS = ((128, 1), (512, 4), (2048, 16))
SWA_MAX_WINDOW = 2048
SWA_BLOCK = 128
ROPE_THETA = 500000.0
ROPE_DIMS = SWA_HD // 4
GMLP_CHUNK = 128
GMLP_WIDTH = D_MODEL
GMLP_GROUP = 128
GMLP_GROUPS = GMLP_WIDTH // GMLP_GROUP
MEM_LEN = 256
MEM_HEADS = 4
MEM_HD = 128
MEM_W = MEM_HEADS * MEM_HD
NORM_EPS = 1e-6
NEG_INF = -1e30
LOG2E = math.log2(math.e)

E_AQ = 0
E_AK = E_AQ + RET_HEADS * RET_DK
E_AV = E_AK + RET_HEADS * RET_DK
E_AG = E_AV + RET_HEADS * RET_DV
E_BQ = E_AG + RET_HEADS * RET_DV
E_BK = E_BQ + SWA_HEADS * SWA_HD
E_BV = E_BK + SWA_HEADS * SWA_HD
E_BG = E_BV + SWA_HEADS * SWA_HD
E_MQ = E_BG + SWA_HEADS * SWA_HD
E_MG = E_MQ + MEM_W
EVEN_IN = E_MG + MEM_W
O_U = 0
O_V = O_U + GMLP_WIDTH
O_Z = O_V + GMLP_WIDTH
O_MQ = O_Z + GMLP_WIDTH
O_MG = O_MQ + MEM_W
ODD_IN = O_MG + MEM_W

SWA_TILE = SWA_MAX_WINDOW
SAMPLE_PAD = 8
GMLP_STEP_CHUNKS = 2
RET_STEP_CHUNKS = 2
MEM_ROWS_STEP_SEQS = 2
RET_STEP_SEQS = 2
OUT_PROJ_SUB_ROWS = 256
VMEM_LIMIT = 48 * 1024 * 1024


def _params(*sem):
    return pltpu.CompilerParams(dimension_semantics=sem, vmem_limit_bytes=VMEM_LIMIT)


def _dot(a, b):
    return jnp.dot(a, b, preferred_element_type=F32)


def _dot_nt(a, b):
    return lax.dot_general(a, b, (((1,), (1,)), ((), ())), preferred_element_type=F32)


def _silu(x):
    return x * jax.nn.sigmoid(x)


def _gelu(x):
    k = 2.0 * math.sqrt(2.0 / math.pi)
    return x * jax.nn.sigmoid(x * (k + (k * 0.044715) * (x * x)))


def _div(x, n):
    assert n & (n - 1) == 0
    return jnp.right_shift(x, n.bit_length() - 1)


def _mod(x, n):
    assert n & (n - 1) == 0
    return jnp.bitwise_and(x, n - 1)


def _rot_tables(pos, n_rot, theta, width):
    half = n_rot // 2
    inv = 1.0 / (theta ** (jnp.arange(half, dtype=F32) / half))
    ang = pos.astype(F32)[:, None] * inv[None, :]
    return _rot_assemble(jnp.cos(ang), jnp.sin(ang), width)


def _rot_tables_range(n, n_rot, theta, width, blk=128):
    assert n % blk == 0
    half = n_rot // 2
    inv = 1.0 / (theta ** (jnp.arange(half, dtype=F32) / half))
    hi = (jnp.arange(n // blk, dtype=F32) * blk)[:, None] * inv[None, :]
    lo = jnp.arange(blk, dtype=F32)[:, None] * inv[None, :]
    ch, sh, cl, sl = jnp.cos(hi)[:, None], jnp.sin(hi)[:, None], jnp.cos(lo)[None], jnp.sin(lo)[None]
    return _rot_assemble((ch * cl - sh * sl).reshape(n, half), (sh * cl + ch * sl).reshape(n, half), width)


def _rot_assemble(cos, sin, width):
    t, half = cos.shape
    cos_t = jnp.concatenate([cos, cos, jnp.ones((t, width - 2 * half), F32)], axis=-1)
    sin_t = jnp.concatenate([-sin, sin, jnp.zeros((t, width - 2 * half), F32)], axis=-1)
    return cos_t, sin_t


def _rot_full(x, cos, sin):
    return x * cos + pltpu.roll(x, RET_DK // 2, 1) * sin


def _rot_partial(x, cos, sin):
    half = ROPE_DIMS // 2
    lane = _mod(lax.broadcasted_iota(jnp.int32, x.shape, 1), SWA_HD)
    width = x.shape[1]
    partner = jnp.where(lane < half, pltpu.roll(x, width - half, 1), pltpu.roll(x, half, 1))
    return x * cos + partner * sin


def _norm_matmul_kernel(x_ref, g_ref, w_ref, o_ref, *rest, side):
    @pl.when(pl.program_id(1) == 0)
    def _():
        x = x_ref[...]
        ms = jnp.mean(x * x, axis=-1, keepdims=True)
        rest[-1][...] = (x * lax.rsqrt(ms + NORM_EPS) * g_ref[...]).astype(BF16)

    acc = _dot(rest[-1][...], w_ref[...])
    o_ref[...] = acc.astype(o_ref.dtype)
    if side is not None:
        i0, j0, nj = side
        i, j = pl.program_id(0), pl.program_id(1)

        @pl.when((i >= i0) & (j >= j0) & (j < j0 + nj))
        def _():
            rest[0][...] = acc


def _norm_matmul(x, g, w, tm, tn, out_dtype=F32, side=None):
    m, d = x.shape
    n = w.shape[1]
    out_specs = [pl.BlockSpec((tm, tn), lambda i, j: (i, j))]
    out_shape = [jax.ShapeDtypeStruct((m, n), out_dtype)]
    tiles = None
    if side is not None:
        row0, col0, ncols = side
        assert row0 % tm == 0 and col0 % tn == 0 and ncols % tn == 0
        i0, j0, nj = row0 // tm, col0 // tn, ncols // tn
        tiles = (i0, j0, nj)
        out_specs.append(pl.BlockSpec(
            (tm, tn), lambda i, j: (jnp.maximum(i - i0, 0), jnp.where(i >= i0, jnp.clip(j - j0, 0, nj - 1), 0))))
        out_shape.append(jax.ShapeDtypeStruct((m - row0, ncols), F32))
    res = pl.pallas_call(
        functools.partial(_norm_matmul_kernel, side=tiles),
        grid=(m // tm, n // tn),
        in_specs=[pl.BlockSpec((tm, d), lambda i, j: (i, 0)),
                  pl.BlockSpec((1, d), lambda i, j: (0, 0)),
                  pl.BlockSpec((d, tn), lambda i, j: (0, j))],
        out_specs=out_specs,
        out_shape=out_shape,
        scratch_shapes=[pltpu.VMEM((tm, d), BF16)],
        compiler_params=_params("arbitrary", "arbitrary"),
        name="norm_matmul",
    )(x, g.reshape(1, d), w)
    return res if side is not None else res[0]


def _norm_matmul_cast_kernel(x_ref, g_ref, w_ref, o_ref, wb_ref, xn_ref):
    @pl.when(pl.program_id(0) == 0)
    def _():
        x = x_ref[...]
        ms = jnp.mean(x * x, axis=-1, keepdims=True)
        xn_ref[...] = (x * lax.rsqrt(ms + NORM_EPS) * g_ref[...]).astype(BF16)

    wb = w_ref[...].astype(BF16)
    wb_ref[...] = wb
    o_ref[...] = _dot(xn_ref[...], wb).astype(o_ref.dtype)


def _norm_matmul_cast(x, g, w, tn):
    m, d = x.shape
    n = w.shape[1]
    return pl.pallas_call(
        _norm_matmul_cast_kernel,
        grid=(n // tn,),
        in_specs=[pl.BlockSpec((m, d), lambda j: (0, 0)),
                  pl.BlockSpec((1, d), lambda j: (0, 0)),
                  pl.BlockSpec((d, tn), lambda j: (0, j))],
        out_specs=[pl.BlockSpec((m, tn), lambda j: (0, j)),
                   pl.BlockSpec((d, tn), lambda j: (0, j))],
        out_shape=[jax.ShapeDtypeStruct((m, n), F32),
                   jax.ShapeDtypeStruct((d, n), BF16)],
        scratch_shapes=[pltpu.VMEM((m, d), BF16)],
        compiler_params=_params("arbitrary"),
        name="norm_matmul_cast",
    )(x, g.reshape(1, d), w)


def _ret_tables(c, reps, pad=None):
    pad = c if pad is None else pad
    rows = pad * reps
    lg = jnp.log1p(-jnp.exp2(-5.0 - jnp.arange(RET_HEADS, dtype=F32)))
    row = jnp.arange(rows, dtype=jnp.int32)
    step = (row % pad).astype(F32)
    live = (row % pad) < c
    rel = step[:, None] - step[None, :]
    keep = ((row // pad)[:, None] == (row // pad)[None, :]) & (rel >= 0) & live[:, None] & live[None, :]
    decay = jnp.where(keep[None], jnp.exp(lg[:, None, None] * jnp.maximum(rel, 0.0)[None]), 0.0)
    q_decay = jnp.where(live[None], jnp.exp(lg[:, None] * (step[None, :] + 1.0)), 0.0)
    k_decay = jnp.where(live[None], jnp.exp(lg[:, None] * (c - 1.0 - step[None, :])), 0.0)
    chunk_decay = jnp.exp(lg * c)
    qd = jnp.broadcast_to(q_decay[:, :, None], (RET_HEADS, rows, RET_DK))
    kd = jnp.broadcast_to(k_decay[:, :, None], (RET_HEADS, rows, RET_DK))
    cd = jnp.broadcast_to(chunk_decay[:, None, None], (RET_HEADS, 1, RET_DV))
    return decay, qd, kd, cd


def _group_norm_gate(o, gn, g):
    mu = jnp.mean(o, axis=-1, keepdims=True)
    d = o - mu
    var = jnp.mean(d * d, axis=-1, keepdims=True)
    return d * lax.rsqrt(var + NORM_EPS) * gn * _silu(g.astype(F32))


def _ret_prompt_kernel(q_ref, k_ref, v_ref, g_ref, cos_ref, sin_ref, dec_ref, qd_ref, kd_ref, cd_ref, gn_ref,
                       y_ref, st_ref):
    @pl.when(pl.program_id(0) == 0)
    def _():
        st_ref[...] = jnp.zeros_like(st_ref)

    for i in range(q_ref.shape[0] // RET_CHUNK):
        rs = slice(i * RET_CHUNK, (i + 1) * RET_CHUNK)
        cos = cos_ref[rs, :]
        sin = sin_ref[rs, :]
        for h in range(RET_HEADS):
            ks = slice(h * RET_DK, (h + 1) * RET_DK)
            vs = slice(h * RET_DV, (h + 1) * RET_DV)
            q = _rot_full(q_ref[rs, ks].astype(F32), cos, sin)
            k = _rot_full(k_ref[rs, ks].astype(F32) * (RET_DK ** -0.5), cos, sin)
            v = v_ref[rs, vs].astype(BF16)
            st = st_ref[h]
            s = _dot_nt(q.astype(BF16), k.astype(BF16)) * dec_ref[h]
            lhs = jnp.concatenate([s.astype(BF16), (q * qd_ref[h]).astype(BF16)], axis=1)
            o = _dot(lhs, jnp.concatenate([v, st.astype(BF16)], axis=0))
            st_ref[h] = cd_ref[h] * st + _dot((k * kd_ref[h]).T.astype(BF16), v)
            y_ref[rs, vs] = _group_norm_gate(o, gn_ref[:, vs], g_ref[rs, vs]).astype(y_ref.dtype)


def _ret_prompt(proj, cos, sin, ret_gn):
    t = proj.shape[0]
    dec, qd, kd, cd = _ret_tables(RET_CHUNK, 1)
    c = RET_STEP_CHUNKS * RET_CHUNK
    assert t % c == 0
    qw = RET_HEADS * RET_DK
    vw = RET_HEADS * RET_DV
    full3 = lambda n: (0, 0, 0)
    return pl.pallas_call(
        _ret_prompt_kernel,
        grid=(t // c,),
        in_specs=[pl.BlockSpec((c, qw), lambda n: (n, E_AQ // qw)),
                  pl.BlockSpec((c, qw), lambda n: (n, E_AK // qw)),
                  pl.BlockSpec((c, vw), lambda n: (n, E_AV // vw)),
                  pl.BlockSpec((c, vw), lambda n: (n, E_AG // vw)),
                  pl.BlockSpec((c, RET_DK), lambda n: (n, 0)),
                  pl.BlockSpec((c, RET_DK), lambda n: (n, 0)),
                  pl.BlockSpec(dec.shape, full3),
                  pl.BlockSpec(qd.shape, full3),
                  pl.BlockSpec(kd.shape, full3),
                  pl.BlockSpec(cd.shape, full3),
                  pl.BlockSpec((1, vw), lambda n: (0, 0))],
        out_specs=[pl.BlockSpec((c, vw), lambda n: (n, 0)),
                   pl.BlockSpec((RET_HEADS, RET_DK, RET_DV), full3)],
        out_shape=[jax.ShapeDtypeStruct((t, vw), BF16),
                   jax.ShapeDtypeStruct((RET_HEADS, RET_DK, RET_DV), F32)],
        compiler_params=_params("arbitrary"),
        name="ret_prompt",
    )(proj, proj, proj, proj, cos, sin, dec, qd, kd, cd, ret_gn.reshape(1, vw))


def _ret_sample_kernel(q_ref, k_ref, v_ref, g_ref, cos_ref, sin_ref, dec_ref, qd_ref, kd_ref, cd_ref, gn_ref,
                       st_ref, y_ref, sto_ref, cross_ref, qs_ref, kt_ref, *, pad):
    b = pl.program_id(0)
    nb = pl.num_programs(0)
    win = RET_CHUNK

    @pl.when(b == 0)
    def _():
        cross_ref[...] = jnp.zeros_like(cross_ref)
        cos = cos_ref[...]
        sin = sin_ref[...]
        for h in range(RET_HEADS):
            ks = slice(h * RET_DK, (h + 1) * RET_DK)
            qs_ref[:, ks] = _rot_full(q_ref[:, ks], cos, sin) * qd_ref[h]
            kt_ref[h] = (_rot_full(k_ref[:, ks] * (RET_DK ** -0.5), cos, sin) * kd_ref[h]).T

    for i in range(st_ref.shape[0]):
        sq = b * st_ref.shape[0] + i
        w0 = pl.multiple_of(jnp.right_shift(sq * pad, win.bit_length() - 1) * win, win)
        wrows = pl.ds(w0, win)
        rows = lax.broadcasted_iota(jnp.int32, (win, RET_DK), 0) + w0
        cols = lax.broadcasted_iota(jnp.int32, (RET_DK, win), 1) + w0
        mine_r = (rows >= sq * pad) & (rows < (sq + 1) * pad)
        mine_c = (cols >= sq * pad) & (cols < (sq + 1) * pad)
        for h in range(RET_HEADS):
            ks = slice(h * RET_DK, (h + 1) * RET_DK)
            vs = slice(h * RET_DV, (h + 1) * RET_DV)
            st = st_ref[i, h]
            qm = jnp.where(mine_r, qs_ref[wrows, ks], 0.0)
            kmt = jnp.where(mine_c, kt_ref[h, :, wrows], 0.0)
            cross_ref[wrows, vs] += _dot(qm.astype(BF16), st.astype(BF16))
            sto_ref[i, h] = cd_ref[h] * st + _dot(kmt.astype(BF16), v_ref[wrows, vs].astype(BF16))

    @pl.when(b == nb - 1)
    def _():
        cos_all = cos_ref[...]
        sin_all = sin_ref[...]
        for h in range(RET_HEADS):
            ks = slice(h * RET_DK, (h + 1) * RET_DK)
            vs = slice(h * RET_DV, (h + 1) * RET_DV)
            q = _rot_full(q_ref[:, ks], cos_all, sin_all)
            k = _rot_full(k_ref[:, ks] * (RET_DK ** -0.5), cos_all, sin_all)
            v = v_ref[:, vs].astype(BF16)
            s = _dot_nt(q.astype(BF16), k.astype(BF16)) * dec_ref[h]
            o = _dot(s.astype(BF16), v) + cross_ref[:, vs]
            y_ref[:, vs] = _group_norm_gate(o, gn_ref[:, vs], g_ref[:, vs]).astype(y_ref.dtype)


def _ret_sample(proj, state, cos, sin, ret_gn, nseq, seq, pad):
    rows = nseq * pad
    c = math.gcd(seq, RET_CHUNK)
    assert c == seq, "sample chunk must cover the new tokens"
    assert rows % RET_CHUNK == 0 and RET_CHUNK % pad == 0
    dec, qd, kd, cd = _ret_tables(c, nseq, pad)
    qw = RET_HEADS * RET_DK
    vw = RET_HEADS * RET_DV
    full3 = lambda b: (0, 0, 0)
    n = RET_STEP_SEQS
    assert nseq % n == 0
    st_spec = pl.BlockSpec((n, RET_HEADS, RET_DK, RET_DV), lambda b: (b, 0, 0, 0))
    return pl.pallas_call(
        functools.partial(_ret_sample_kernel, pad=pad),
        grid=(nseq // n,),
        in_specs=[pl.BlockSpec((rows, qw), lambda b: (0, E_AQ // qw)),
                  pl.BlockSpec((rows, qw), lambda b: (0, E_AK // qw)),
                  pl.BlockSpec((rows, vw), lambda b: (0, E_AV // vw)),
                  pl.BlockSpec((rows, vw), lambda b: (0, E_AG // vw)),
                  pl.BlockSpec((rows, RET_DK), lambda b: (0, 0)),
                  pl.BlockSpec((rows, RET_DK), lambda b: (0, 0)),
                  pl.BlockSpec(dec.shape, full3),
                  pl.BlockSpec(qd.shape, full3),
                  pl.BlockSpec(kd.shape, full3),
                  pl.BlockSpec(cd.shape, full3),
                  pl.BlockSpec((1, vw), lambda b: (0, 0)),
                  st_spec],
        out_specs=[pl.BlockSpec((rows, vw), lambda b: (0, 0)), st_spec],
        out_shape=[jax.ShapeDtypeStruct((rows, vw), F32),
                   jax.ShapeDtypeStruct(state.shape, F32)],
        scratch_shapes=[pltpu.VMEM((rows, vw), F32),
                        pltpu.VMEM((rows, qw), F32),
                        pltpu.VMEM((RET_HEADS, RET_DK, rows), F32)],
        compiler_params=_params("arbitrary"),
        name="ret_sample",
    )(proj, proj, proj, proj, cos, sin, dec, qd, kd, cd, ret_gn.reshape(1, vw), state)


def _swa_prompt_kernel(q_ref, kc_ref, vc_ref, g_ref, kf_ref, cos_ref, sin_ref,
                       y_ref, ko_ref, q_s, k_s, v_s, m_s, l_s, acc_s):
    t = pl.program_id(1)
    nt = pl.num_programs(1)
    tb = SWA_TILE
    blk = SWA_BLOCK
    hd = SWA_HD
    cos = cos_ref[...]
    sin = sin_ref[...]

    @pl.when(t == 0)
    def _():
        k_s[0:tb, :] = jnp.zeros((tb, hd), F32)
        v_s[0:tb, :] = jnp.zeros((tb, hd), F32)

    @pl.when(t > 0)
    def _():
        k_s[0:tb, :] = k_s[tb:2 * tb, :]
        v_s[0:tb, :] = v_s[tb:2 * tb, :]

    q_s[...] = _rot_partial(q_ref[...].astype(F32) * (SWA_HD ** -0.5 * LOG2E), cos, sin)
    k_s[tb:2 * tb, :] = _rot_partial(kc_ref[...].astype(F32), cos, sin)
    v_s[tb:2 * tb, :] = vc_ref[...].astype(F32)

    @pl.when(t == nt - 1)
    def _():
        ko_ref[...] = _rot_partial(kf_ref[...], cos, sin)

    qi = lax.broadcasted_iota(jnp.int32, (blk, 2 * blk), 0)
    kj = lax.broadcasted_iota(jnp.int32, (blk, 2 * blk), 1)
    band = (kj >= qi) & (kj <= qi + blk)
    band_first = band & ((kj >= blk) | (t > 0))
    ones = jnp.ones((2 * blk, hd), BF16)

    def ds(start, size, stride):
        return pl.ds(start, size) if stride == 1 else pl.ds(start, size, stride=stride)

    for bi, (window, r) in enumerate(SWA_PATTERNS):
        assert window // r == blk
        for idx in range(tb // blk):
            c, n = idx % r, idx // r
            qs = n * blk * r + c
            ks = qs + tb - blk * r
            qrows = ds(qs, blk, r)
            krows = ds(ks, 2 * blk, r)
            valid = band_first if n == 0 else band
            s = _dot_nt(q_s[qrows, :].astype(BF16), k_s[krows, :].astype(BF16))
            s = jnp.where(valid, s, NEG_INF)
            m = jnp.max(s, axis=-1, keepdims=True)
            p = jnp.exp2(s - m).astype(BF16)
            pv = _dot(p, jnp.concatenate([v_s[krows, :].astype(BF16), ones], axis=1))
            m_s[bi, qrows, :] = jnp.broadcast_to(m, (blk, hd))
            l_s[bi, qrows, :] = pv[:, hd:]
            acc_s[bi, qrows, :] = pv[:, :hd]

    nbr = len(SWA_PATTERNS)
    ms = [m_s[i] for i in range(nbr)]
    mmax = functools.reduce(jnp.maximum, ms)
    wts = [jnp.exp2(mi - mmax) for mi in ms]
    num = sum(wts[i] * acc_s[i] for i in range(nbr))
    den = sum(wts[i] * l_s[i] for i in range(nbr))
    y_ref[...] = (num / den * _silu(g_ref[...].astype(F32))).astype(y_ref.dtype)


def _swa_prompt(proj, k_last, cos, sin):
    t = proj.shape[0]
    tb = SWA_TILE
    assert t % tb == 0
    hd = SWA_HD
    nbr = len(SWA_PATTERNS)
    cur = lambda off: pl.BlockSpec((tb, hd), lambda h, i: (i, off // hd + h))
    return pl.pallas_call(
        _swa_prompt_kernel,
        grid=(SWA_HEADS, t // tb),
        in_specs=[cur(E_BQ), cur(E_BK), cur(E_BV), cur(E_BG),
                  pl.BlockSpec((tb, hd), lambda h, i: (0, h)),
                  pl.BlockSpec((tb, hd), lambda h, i: (i, 0)),
                  pl.BlockSpec((tb, hd), lambda h, i: (i, 0))],
        out_specs=[pl.BlockSpec((tb, hd), lambda h, i: (i, h)),
                   pl.BlockSpec((tb, hd), lambda h, i: (0, h))],
        out_shape=[jax.ShapeDtypeStruct((t, SWA_HEADS * hd), BF16),
                   jax.ShapeDtypeStruct((tb, SWA_HEADS * hd), F32)],
        scratch_shapes=[pltpu.VMEM((tb, hd), F32),
                        pltpu.VMEM((2 * tb, hd), F32),
                        pltpu.VMEM((2 * tb, hd), F32),
                        pltpu.VMEM((nbr, tb, hd), F32),
                        pltpu.VMEM((nbr, tb, hd), F32),
                        pltpu.VMEM((nbr, tb, hd), F32)],
        compiler_params=_params("arbitrary", "arbitrary"),
        name="swa_prompt",
    )(proj, proj, proj, proj, k_last, cos, sin)


def _swa_sample_kernel(q_ref, k_ref, v_ref, g_ref, cos_ref, sin_ref, ka_ref, kb_ref, va_ref, vb_ref,
                       y_ref, ko_ref, kn_s, vn_s, ba_s, bb_s, bn_s, *, seq):
    pad = SAMPLE_PAD
    nh = SWA_HEADS
    hd = SWA_HD
    nr = nh * pad
    nj, nrh, _ = kb_ref.shape[1:]
    na = ka_ref.shape[1]
    nb = nj * nrh
    nn = kn_s.shape[0]

    @pl.when(pl.program_id(0) == 0)
    def _():
        (w1, d1), (w2, d2), (w3, d3) = SWA_PATTERNS

        def rows_cols(n):
            r = lax.broadcasted_iota(jnp.int32, (nr, n), 0)
            c = lax.broadcasted_iota(jnp.int32, (nr, n), 1)
            return _div(r, pad), _mod(r, pad), c

        def log2_mult(mult):
            return jnp.where(mult > 0, jnp.log2(jnp.maximum(mult, 1.0)), NEG_INF)

        hq, tq, c = rows_cols(na)
        dist = na // nh + tq - _div(c, nh)
        mult_a = ((dist <= w1).astype(F32) + ((_mod(dist, d2) == 0) & (dist <= w2)).astype(F32)
                  + ((_mod(dist, d3) == 0) & (dist <= w3)).astype(F32))
        ba_s[...] = log2_mult(jnp.where((hq == _mod(c, nh)) & (tq < seq), mult_a, 0.0))
        hq, tq, c = rows_cols(nb)
        bb_s[...] = log2_mult(jnp.where((hq == _mod(c, nh)) & (tq == _mod(_div(c, nh), nrh // nh)), 1.0, 0.0))
        hq, tq, c = rows_cols(nn)
        tk = _mod(c, pad)
        mult_n = (tk <= tq).astype(F32) + 2.0 * (tk == tq).astype(F32)
        bn_s[...] = log2_mult(jnp.where((hq == _div(c, pad)) & (tq < seq) & (tk < seq), mult_n, 0.0))
        kn_s[...] = jnp.zeros_like(kn_s)
        vn_s[...] = jnp.zeros_like(vn_s)

    cos = cos_ref[...]
    sin = sin_ref[...]
    q8 = _rot_partial(q_ref[0] * (SWA_HD ** -0.5 * LOG2E), cos, sin)
    k8 = _rot_partial(k_ref[0], cos, sin)
    ko_ref[0] = k8

    def head_rows(x):
        return jnp.concatenate([x[:, h * hd:(h + 1) * hd] for h in range(nh)], axis=0)

    qrows = head_rows(q8).astype(BF16)
    kn_s[0:nr, :] = head_rows(k8).astype(BF16)
    vn_s[0:nr, :] = head_rows(v_ref[0]).astype(BF16)

    kb = kb_ref[0].reshape(nb, hd)
    vb = vb_ref[0].reshape(nb, hd)
    s_a = _dot_nt(qrows, ka_ref[0].astype(BF16)) + ba_s[...]
    s_b = _dot_nt(qrows, kb.astype(BF16)) + bb_s[...]
    s_n = _dot_nt(qrows, kn_s[...]) + bn_s[...]
    rowmax = lambda s: jnp.max(s, axis=-1, keepdims=True)
    m = jnp.maximum(jnp.maximum(rowmax(s_a), rowmax(s_b)), rowmax(s_n))
    p_a, p_b, p_n = jnp.exp2(s_a - m), jnp.exp2(s_b - m), jnp.exp2(s_n - m)
    rowsum = lambda p: jnp.sum(p, axis=-1, keepdims=True)
    o = (_dot(p_a.astype(BF16), va_ref[0].astype(BF16)) + _dot(p_b.astype(BF16), vb.astype(BF16))
         + _dot(p_n.astype(BF16), vn_s[...])) / (rowsum(p_a) + rowsum(p_b) + rowsum(p_n))
    y = jnp.concatenate([o[h * pad:(h + 1) * pad] for h in range(nh)], axis=1)
    y_ref[0] = y * _silu(g_ref[0])


def _swa_sample(proj_pad, cache_k, cache_v, cos, sin, seq):
    nseq, pad, _ = proj_pad.shape
    buf = cache_k.shape[1]
    nh, hd = SWA_HEADS, SWA_HD
    w = nh * hd
    (w1, d1), (w2, d2), (w3, d3) = SWA_PATTERNS
    assert buf == w3 and d1 == 1 and seq <= d2 and w1 <= w2 and buf % w2 == 0 and buf % d3 == 0
    assert nh * pad <= SWA_BLOCK and (seq * nh) % 8 == 0
    flat = lambda a: a.reshape(nseq, buf * nh, hd)
    strided = lambda a: a.reshape(nseq, buf // d3, d3 * nh, hd)
    col = lambda off: pl.BlockSpec((1, pad, w), lambda b: (b, 0, off // w))
    tail = pl.BlockSpec((1, w2 * nh, hd), lambda b: (b, buf // w2 - 1, 0))
    resid = pl.BlockSpec((1, (buf - w2) // d3, seq * nh, hd), lambda b: (b, 0, 0, 0))
    tab = pl.BlockSpec((pad, w), lambda b: (0, 0))
    out = pl.BlockSpec((1, pad, w), lambda b: (b, 0, 0))
    return pl.pallas_call(
        functools.partial(_swa_sample_kernel, seq=seq),
        grid=(nseq,),
        in_specs=[col(E_BQ), col(E_BK), col(E_BV), col(E_BG), tab, tab, tail, resid, tail, resid],
        out_specs=[out, out],
        out_shape=[jax.ShapeDtypeStruct((nseq, pad, w), F32),
                   jax.ShapeDtypeStruct((nseq, pad, w), F32)],
        scratch_shapes=[pltpu.VMEM((SWA_BLOCK, hd), BF16), pltpu.VMEM((SWA_BLOCK, hd), BF16),
                        pltpu.VMEM((nh * pad, w2 * nh), F32),
                        pltpu.VMEM((nh * pad, (buf - w2) // d3 * seq * nh), F32),
                        pltpu.VMEM((nh * pad, SWA_BLOCK), F32)],
        compiler_params=_params("arbitrary"),
        name="swa_sample",
    )(proj_pad, proj_pad, proj_pad, proj_pad, cos, sin,
      flat(cache_k), strided(cache_k), flat(cache_v), strided(cache_v))


def _mem_attn_kernel(q_ref, g_ref, k_ref, v_ref, o_ref):
    ones = jnp.ones((MEM_LEN, MEM_HD), BF16)
    for h in range(MEM_HEADS):
        hs = slice(h * MEM_HD, (h + 1) * MEM_HD)
        q = (q_ref[0, :, hs].astype(F32) * (MEM_HD ** -0.5 * LOG2E)).astype(BF16)
        s = _dot_nt(q, k_ref[0, :, hs].astype(BF16))
        m = jnp.max(s, axis=-1, keepdims=True)
        p = jnp.exp2(s - m).astype(BF16)
        pv = _dot(p, jnp.concatenate([v_ref[0, :, hs].astype(BF16), ones], axis=1))
        o = pv[:, :MEM_HD] / pv[:, MEM_HD:]
        o_ref[0, :, hs] = (o * _silu(g_ref[0, :, hs].astype(F32))).astype(o_ref.dtype)


def _mem_attn_rows_kernel(q_ref, g_ref, k_ref, v_ref, o_ref, bias_s):
    nseq, pad, _ = q_ref.shape
    nh, hd = MEM_HEADS, MEM_HD

    @pl.when(pl.program_id(0) == 0)
    def _():
        r = lax.broadcasted_iota(jnp.int32, bias_s.shape, 0)
        c = lax.broadcasted_iota(jnp.int32, bias_s.shape, 1)
        bias_s[...] = jnp.where(_div(r, pad) == _mod(c, nh), 0.0, NEG_INF)

    for i in range(nseq):
        q = q_ref[i] * (MEM_HD ** -0.5 * LOG2E)
        qrows = jnp.concatenate([q[:, h * hd:(h + 1) * hd] for h in range(nh)], axis=0).astype(BF16)
        s = _dot_nt(qrows, k_ref[i].astype(BF16)) + bias_s[...]
        m = jnp.max(s, axis=-1, keepdims=True)
        p = jnp.exp2(s - m)
        o = _dot(p.astype(BF16), v_ref[i].astype(BF16)) / jnp.sum(p, axis=-1, keepdims=True)
        y = jnp.concatenate([o[h * pad:(h + 1) * pad] for h in range(nh)], axis=1)
        o_ref[i] = y * _silu(g_ref[i])


def _mem_attn_rows(proj_pad, q_off, g_off, mem_k, mem_v, layer):
    b, pad, _ = proj_pad.shape
    w = MEM_W
    rows = MEM_LEN * MEM_HEADS
    n = MEM_ROWS_STEP_SEQS
    assert b % n == 0
    kv = pl.BlockSpec((n, rows, MEM_HD), lambda i: (layer * (b // n) + i, 0, 0))
    return pl.pallas_call(
        _mem_attn_rows_kernel,
        grid=(b // n,),
        in_specs=[pl.BlockSpec((n, pad, w), lambda i: (i, 0, q_off // w)),
                  pl.BlockSpec((n, pad, w), lambda i: (i, 0, g_off // w)),
                  kv, kv],
        out_specs=pl.BlockSpec((n, pad, w), lambda i: (i, 0, 0)),
        out_shape=jax.ShapeDtypeStruct((b, pad, w), F32),
        scratch_shapes=[pltpu.VMEM((MEM_HEADS * pad, rows), F32)],
        compiler_params=_params("arbitrary"),
        name="mem_attn_rows",
    )(proj_pad, proj_pad, mem_k.reshape(-1, rows, MEM_HD), mem_v.reshape(-1, rows, MEM_HD))


def _mem_attn(proj, q_off, g_off, mem_k, k_off, mem_v, v_off, tq, out_dtype):
    b, t, _ = proj.shape
    w = MEM_W
    return pl.pallas_call(
        _mem_attn_kernel,
        grid=(b, t // tq),
        in_specs=[pl.BlockSpec((1, tq, w), lambda i, j: (i, j, q_off // w)),
                  pl.BlockSpec((1, tq, w), lambda i, j: (i, j, g_off // w)),
                  pl.BlockSpec((1, MEM_LEN, w), lambda i, j: (i, 0, k_off // w)),
                  pl.BlockSpec((1, MEM_LEN, w), lambda i, j: (i, 0, v_off // w))],
        out_specs=pl.BlockSpec((1, tq, w), lambda i, j: (i, j, 0)),
        out_shape=jax.ShapeDtypeStruct((b, t, w), out_dtype),
        compiler_params=_params("arbitrary", "arbitrary"),
        name="mem_attn",
    )(proj, proj, mem_k, mem_v)


def _gmlp_kernel(u_ref, v_ref, z_ref, ln_ref, w_ref, b_ref, y_ref, vo_ref, wc_ref):
    c = GMLP_CHUNK

    @pl.when(pl.program_id(0) == 0)
    def _():
        causal = (lax.broadcasted_iota(jnp.int32, (c, c), 0) >= lax.broadcasted_iota(jnp.int32, (c, c), 1))
        for g in range(GMLP_GROUPS):
            wc_ref[g] = jnp.where(causal, w_ref[g], 0.0).astype(BF16)

    nchunks = u_ref.shape[0] // c
    keep_all = vo_ref.shape[0] == u_ref.shape[0]
    for i in range(nchunks):
        rs = slice(i * c, (i + 1) * c)
        v = _gelu(v_ref[rs, :].astype(F32))
        mu = jnp.mean(v, axis=-1, keepdims=True)
        d = v - mu
        var = jnp.mean(d * d, axis=-1, keepdims=True)
        vn = d * lax.rsqrt(var + NORM_EPS) * ln_ref[...]
        if keep_all:
            vo_ref[rs, :] = vn
        elif i == nchunks - 1:
            vo_ref[...] = vn
        for g in range(GMLP_GROUPS):
            gs = slice(g * GMLP_GROUP, (g + 1) * GMLP_GROUP)
            mixed = _dot(wc_ref[g], vn[:, gs].astype(BF16)) + b_ref[:, gs]
            y_ref[rs, gs] = (_gelu(u_ref[rs, gs].astype(F32)) * mixed
                             * _silu(z_ref[rs, gs].astype(F32))).astype(y_ref.dtype)


def _gmlp(proj, gmlp_ln, w_mix, bias_rows, keep_all_v):
    t = proj.shape[0]
    c = GMLP_CHUNK
    w = GMLP_WIDTH
    rows = GMLP_STEP_CHUNKS * c
    assert t % rows == 0
    col = lambda off: pl.BlockSpec((rows, w), lambda n: (n, off // w))
    return pl.pallas_call(
        _gmlp_kernel,
        grid=(t // rows,),
        in_specs=[col(O_U), col(O_V), col(O_Z),
                  pl.BlockSpec((1, w), lambda n: (0, 0)),
                  pl.BlockSpec(w_mix.shape, lambda n: (0, 0, 0)),
                  pl.BlockSpec((c, w), lambda n: (0, 0))],
        out_specs=[pl.BlockSpec((rows, w), lambda n: (n, 0)),
                   pl.BlockSpec((rows, w), lambda n: (n, 0)) if keep_all_v
                   else pl.BlockSpec((c, w), lambda n: (0, 0))],
        out_shape=[jax.ShapeDtypeStruct((t, w), BF16),
                   jax.ShapeDtypeStruct((t if keep_all_v else c, w), F32)],
        scratch_shapes=[pltpu.VMEM(w_mix.shape, BF16)],
        compiler_params=_params("arbitrary"),
        name="gmlp",
    )(proj, proj, proj, gmlp_ln.reshape(1, w), w_mix, bias_rows)


def _out_proj_kernel(*refs, n_in):
    a_refs, w_refs = refs[:n_in], refs[n_in:2 * n_in]
    g_ref, x_ref, o_ref = refs[2 * n_in:]
    sub = min(OUT_PROJ_SUB_ROWS, o_ref.shape[0])
    for r0 in range(0, o_ref.shape[0], sub):
        rs = slice(r0, r0 + sub)
        acc = _dot(a_refs[0][rs, :].astype(BF16), w_refs[0][...])
        for a_ref, w_ref in zip(a_refs[1:], w_refs[1:]):
            acc = acc + _dot(a_ref[rs, :].astype(BF16), w_ref[...])
        ms = jnp.mean(acc * acc, axis=-1, keepdims=True)
        o_ref[rs, :] = x_ref[rs, :] + acc * lax.rsqrt(ms + NORM_EPS) * g_ref[...]


def _out_proj(acts, w, g, x, tm):
    m, d = x.shape
    n_in = len(acts)
    w_specs, off = [], 0
    for a in acts:
        k = a.shape[1]
        assert off % k == 0
        w_specs.append(pl.BlockSpec((k, d), functools.partial(lambda i, blk: (blk, 0), blk=off // k),
                                    pipeline_mode=pl.Buffered(1)))
        off += k
    assert off == w.shape[0]
    return pl.pallas_call(
        functools.partial(_out_proj_kernel, n_in=n_in),
        grid=(m // tm,),
        in_specs=([pl.BlockSpec((tm, a.shape[1]), lambda i: (i, 0)) for a in acts]
                  + w_specs
                  + [pl.BlockSpec((1, d), lambda i: (0, 0)),
                     pl.BlockSpec((tm, d), lambda i: (i, 0))]),
        out_specs=pl.BlockSpec((tm, d), lambda i: (i, 0)),
        out_shape=jax.ShapeDtypeStruct((m, d), F32),
        compiler_params=_params("arbitrary"),
        name="out_proj",
    )(*acts, *([w] * n_in), g.reshape(1, d), x)


def _out_proj_cast_kernel(a_ref, w_ref, g_ref, x_ref, o_ref, wb_ref):
    k = pl.program_id(0)

    @pl.when(k == 0)
    def _():
        o_ref[...] = jnp.zeros_like(o_ref)

    wb = w_ref[...].astype(BF16)
    wb_ref[...] = wb
    o_ref[...] += _dot(a_ref[...].astype(BF16), wb)

    @pl.when(k == pl.num_programs(0) - 1)
    def _():
        acc = o_ref[...]
        ms = jnp.mean(acc * acc, axis=-1, keepdims=True)
        o_ref[...] = x_ref[...] + acc * lax.rsqrt(ms + NORM_EPS) * g_ref[...]


def _out_proj_cast(a, w, g, x, tk):
    m, d = x.shape
    kdim = w.shape[0]
    assert kdim % tk == 0 and a.shape == (m, kdim)
    return pl.pallas_call(
        _out_proj_cast_kernel,
        grid=(kdim // tk,),
        in_specs=[pl.BlockSpec((m, tk), lambda k: (0, k)),
                  pl.BlockSpec((tk, d), lambda k: (k, 0)),
                  pl.BlockSpec((1, d), lambda k: (0, 0)),
                  pl.BlockSpec((m, d), lambda k: (0, 0))],
        out_specs=[pl.BlockSpec((m, d), lambda k: (0, 0)),
                   pl.BlockSpec((tk, d), lambda k: (k, 0))],
        out_shape=[jax.ShapeDtypeStruct((m, d), F32),
                   jax.ShapeDtypeStruct((kdim, d), BF16)],
        compiler_params=_params("arbitrary"),
        name="out_proj_cast",
    )(a, w, g.reshape(1, d), x)


def _pad_steps(a, pad):
    nseq, seq, n = a.shape
    return jnp.pad(a, ((0, 0), (0, pad - seq), (0, 0))).reshape(nseq * pad, n)


def kernel(x_prompt, x_sample, state_ret, cache_swa_k, cache_swa_v, cache_mem_k, cache_mem_v, mem_prompt,
           pre_norm, post_norm, mem_norm, w_mem_k, w_mem_v, w_in_even, ret_gn, w_out_even,
           w_in_odd, gmlp_ln, w_spatial, b_spatial, w_out_odd):
    bp, tp, d = x_prompt.shape
    nseq, seq, _ = x_sample.shape
    pad = SAMPLE_PAD
    rows_s = nseq * pad
    c = GMLP_CHUNK
    assert bp == 1 and seq <= pad and rows_s % c == 0 and c % pad == 0
    hp = x_prompt.reshape(tp, d)
    hs = _pad_steps(x_sample, pad)
    mem = mem_prompt.reshape(MEM_LEN, d)
    tm_p = 1024 if tp % 1024 == 0 else 512
    unpad = lambda a: a.reshape(nseq, pad, -1)[:, :seq]

    pos_s = PAST_LEN + jnp.arange(seq, dtype=jnp.int32)
    pos_pad = jnp.concatenate([pos_s, jnp.zeros((pad - seq,), jnp.int32)])
    ret_cos_p, ret_sin_p = _rot_tables_range(tp, RET_DK, RET_THETA, RET_DK)
    ret_cos_s, ret_sin_s = _rot_tables(jnp.tile(pos_pad, nseq), RET_DK, RET_THETA, RET_DK)
    swa_cos_p, swa_sin_p = _rot_tables_range(tp, ROPE_DIMS, ROPE_THETA, SWA_HD)
    swa_cos_s, swa_sin_s = _rot_tables(pos_pad, ROPE_DIMS, ROPE_THETA, SWA_HD)
    swa_cos_s = jnp.tile(swa_cos_s, (1, SWA_HEADS))
    swa_sin_s = jnp.tile(swa_sin_s, (1, SWA_HEADS))

    w_mem0 = jnp.concatenate([w_mem_k[0], w_mem_v[0]], axis=1).astype(BF16)
    memkv0 = _norm_matmul(mem, mem_norm[0], w_mem0, MEM_LEN, 2 * MEM_W)

    proj_s, w_in0 = _norm_matmul_cast(hs, pre_norm[0], w_in_even[0], 1024)
    proj_s3 = proj_s.reshape(nseq, pad, EVEN_IN)
    ya_s, st_s = _ret_sample(proj_s, state_ret[0], ret_cos_s, ret_sin_s, ret_gn[0], nseq, seq, pad)
    yb_s, sk = _swa_sample(proj_s3, cache_swa_k[0], cache_swa_v[0], swa_cos_s, swa_sin_s, seq)
    ym_s = _mem_attn_rows(proj_s3, E_MQ, E_MG, cache_mem_k, cache_mem_v, 0)
    y_s = jnp.concatenate([ya_s, yb_s.reshape(rows_s, -1), ym_s.reshape(rows_s, -1)], axis=1)
    hs1, w_out0 = _out_proj_cast(y_s, w_out_even[0], post_norm[0], hs, 512)
    sk = sk[:, :seq].reshape(nseq, seq, SWA_HEADS, SWA_HD)
    sv = unpad(proj_s[:, E_BV:E_BV + SWA_HEADS * SWA_HD]).reshape(nseq, seq, SWA_HEADS, SWA_HD)

    buf_p = min(SWA_MAX_WINDOW, tp)
    proj_p, kv_last = _norm_matmul(hp, pre_norm[0], w_in0, tm_p, 1024, BF16, (tp - buf_p, E_BK, E_BG - E_BK))
    ya_p, st_p = _ret_prompt(proj_p, ret_cos_p, ret_sin_p, ret_gn[0])
    yb_p, pk = _swa_prompt(proj_p, kv_last, swa_cos_p, swa_sin_p)
    ym_p = _mem_attn(proj_p[None], E_MQ, E_MG, memkv0[None], 0, memkv0[None], MEM_W, 1024, BF16)[0]
    hp1 = _out_proj([ya_p, yb_p, ym_p], w_out0, post_norm[0], hp, 512)
    pv = kv_last[:, E_BV - E_BK:]

    w_mem1 = jnp.concatenate([w_mem_k[1], w_mem_v[1]], axis=1).astype(BF16)
    memkv1 = _norm_matmul(mem, mem_norm[1], w_mem1, MEM_LEN, 2 * MEM_W)
    bias_p = jnp.repeat(b_spatial[0].T, GMLP_GROUP, axis=1)
    blockdiag = lambda wg: jnp.kron(jnp.eye(c // pad, dtype=F32),
                                    jnp.pad(wg[:seq, :seq], ((0, pad - seq), (0, pad - seq))))
    w_mix_s = jax.vmap(blockdiag)(w_spatial[0])
    bias_s = jnp.tile(jnp.pad(bias_p[:seq], ((0, pad - seq), (0, 0))), (c // pad, 1))

    proj_s1, w_in1 = _norm_matmul_cast(hs1, pre_norm[1], w_in_odd[0], 1024)
    yc_s, gv_s = _gmlp(proj_s1, gmlp_ln[0], w_mix_s, bias_s, True)
    ym_s1 = _mem_attn_rows(proj_s1.reshape(nseq, pad, ODD_IN), O_MQ, O_MG, cache_mem_k, cache_mem_v, 1)
    y_s1 = jnp.concatenate([yc_s.astype(F32), ym_s1.reshape(rows_s, -1)], axis=1)
    hs2, w_out1 = _out_proj_cast(y_s1, w_out_odd[0], post_norm[1], hs1, 512)

    proj_p1 = _norm_matmul(hp1, pre_norm[1], w_in1, tm_p, 1024, BF16)
    yc_p, gv_p = _gmlp(proj_p1, gmlp_ln[0], w_spatial[0], bias_p, False)
    ym_p1 = _mem_attn(proj_p1[None], O_MQ, O_MG, memkv1[None], 0, memkv1[None], MEM_W, 1024, BF16)[0]
    hp2 = _out_proj([yc_p, ym_p1], w_out1, post_norm[1], hp1, 512)

    memkv = jnp.stack([memkv0, memkv1])
    p_mk = memkv[:, :, :MEM_W].reshape(2, bp, MEM_LEN, MEM_HEADS, MEM_HD)
    p_mv = memkv[:, :, MEM_W:].reshape(2, bp, MEM_LEN, MEM_HEADS, MEM_HD)
    return (hp2.reshape(bp, tp, d), unpad(hs2),
            st_p[None, None],
            pk.reshape(1, bp, buf_p, SWA_HEADS, SWA_HD), pv.reshape(1, bp, buf_p, SWA_HEADS, SWA_HD),
            p_mk, p_mv,
            gv_p[None, None],
            st_s[None],
            sk[None], sv[None],
            unpad(gv_s)[None])
```

```python
import functools
import math

import jax
import jax.numpy as jnp
import numpy as np
from jax import lax
from jax.experimental import pallas as pl
from jax.experimental.pallas import tpu as pltpu

F32 = jnp.float32
BF16 = jnp.bfloat16

D_MODEL = 2048
PAST_LEN = 8192
RET_HEADS = 8
RET_DK = 128
RET_DV = 256
RET_CHUNK = 128
RET_THETA = 10000.0
SWA_HEADS = 8
SWA_HD = 128
SWA_PATTERNS = ((128, 1), (512, 4), (2048, 16))
SWA_MAX_WINDOW = 2048
SWA_BLOCK = 128
ROPE_THETA = 500000.0
ROPE_DIMS = SWA_HD // 4
GMLP_CHUNK = 128
GMLP_WIDTH = D_MODEL
GMLP_GROUP = 128
GMLP_GROUPS = GMLP_WIDTH // GMLP_GROUP
MEM_LEN = 256
MEM_HEADS = 4
MEM_HD = 128
MEM_W = MEM_HEADS * MEM_HD
NORM_EPS = 1e-6
NEG_INF = -1e30
LOG2E = math.log2(math.e)

E_AQ = 0
E_AK = E_AQ + RET_HEADS * RET_DK
E_AV = E_AK + RET_HEADS * RET_DK
E_AG = E_AV + RET_HEADS * RET_DV
E_BQ = E_AG + RET_HEADS * RET_DV
E_BK = E_BQ + SWA_HEADS * SWA_HD
E_BV = E_BK + SWA_HEADS * SWA_HD
E_BG = E_BV + SWA_HEADS * SWA_HD
E_MQ = E_BG + SWA_HEADS * SWA_HD
E_MG = E_MQ + MEM_W
EVEN_IN = E_MG + MEM_W
O_U = 0
O_V = O_U + GMLP_WIDTH
O_Z = O_V + GMLP_WIDTH
O_MQ = O_Z + GMLP_WIDTH
O_MG = O_MQ + MEM_W
ODD_IN = O_MG + MEM_W

SWA_TILE = SWA_MAX_WINDOW
SAMPLE_PAD = 8
GMLP_STEP_CHUNKS = 2
RET_PROMPT_CHUNK = 256
RET_STEP_CHUNKS = 1
MEM_ROWS_STEP_SEQS = 2
RET_STEP_SEQS = 2
OUT_PROJ_SUB_ROWS = 256
NORM_SUB_ROWS = 256
VMEM_LIMIT = 48 * 1024 * 1024


def _params(*sem):
    return pltpu.CompilerParams(dimension_semantics=sem, vmem_limit_bytes=VMEM_LIMIT)


def _dot(a, b):
    return jnp.dot(a, b, preferred_element_type=F32)


def _dot_nt(a, b):
    return lax.dot_general(a, b, (((1,), (1,)), ((), ())), preferred_element_type=F32)


def _silu(x):
    return x / (1.0 + jnp.exp2(x * (-LOG2E)))


def _gelu(x):
    a = -2.0 * math.sqrt(2.0 / math.pi) * LOG2E
    return x / (1.0 + jnp.exp2(x * (a + (a * 0.044715) * (x * x))))


def _div(x, n):
    assert n & (n - 1) == 0
    return jnp.right_shift(x, n.bit_length() - 1)


def _mod(x, n):
    assert n & (n - 1) == 0
    return jnp.bitwise_and(x, n - 1)


def _rot_lanes(n_rot, theta, width):
    half = n_rot // 2
    inv = 1.0 / (theta ** (np.arange(half, dtype=np.float64) / half))
    rest = np.zeros(width - n_rot)
    return np.concatenate([inv, inv, rest]), np.concatenate([-np.ones(half), np.ones(half), rest])


def _rot_tables(pos, n_rot, theta, width):
    f, sgn = _rot_lanes(n_rot, theta, width)
    ang = np.asarray(pos, np.float64)[:, None] * f[None, :]
    return np.cos(ang).astype(np.float32), (sgn * np.sin(ang)).astype(np.float32)


def _rot_tables_range(n, n_rot, theta, width, blk=128):
    assert n % blk == 0
    f, sgn = _rot_lanes(n_rot, theta, width)
    hi = (np.arange(n // blk, dtype=np.float64) * blk)[:, None] * f[None, :]
    lo = np.arange(blk, dtype=np.float64)[:, None] * f[None, :]
    dev = lambda a: jnp.asarray(a, F32)
    ch, sh, cl, sl = dev(np.cos(hi)[:, None]), dev(np.sin(hi)[:, None]), dev(np.cos(lo)[None]), dev(np.sin(lo)[None])
    chs, shs = dev((sgn * np.cos(hi))[:, None]), dev((sgn * np.sin(hi))[:, None])
    return (ch * cl - sh * sl).reshape(n, width), (shs * cl + chs * sl).reshape(n, width)


def _rot_full(x, cos, sin):
    return x * cos + pltpu.roll(x, RET_DK // 2, 1) * sin


def _rot_partial(x, cos, sin):
    half = ROPE_DIMS // 2
    lane = _mod(lax.broadcasted_iota(jnp.int32, x.shape, 1), SWA_HD)
    width = x.shape[1]
    partner = jnp.where(lane < half, pltpu.roll(x, width - half, 1), pltpu.roll(x, half, 1))
    return x * cos + partner * sin


def _norm_matmul_kernel(x_ref, g_ref, w_ref, o_ref, *rest, side):
    xn_ref = rest[-1]
    tm = x_ref.shape[0]
    sub = min(NORM_SUB_ROWS, tm)

    @pl.when(pl.program_id(1) == 0)
    def _():
        for r0 in range(0, tm, sub):
            rs = slice(r0, r0 + sub)
            x = x_ref[rs, :]
            ms = jnp.mean(x * x, axis=-1, keepdims=True)
            xn = (x * lax.rsqrt(ms + NORM_EPS) * g_ref[...]).astype(BF16)
            xn_ref[rs, :] = xn
            o_ref[rs, :] = _dot(xn, w_ref[...]).astype(o_ref.dtype)

    @pl.when(pl.program_id(1) != 0)
    def _():
        acc = _dot(xn_ref[...], w_ref[...])
        o_ref[...] = acc.astype(o_ref.dtype)
        if side is not None:
            i0, j0, nj = side
            i, j = pl.program_id(0), pl.program_id(1)

            @pl.when((i >= i0) & (j >= j0) & (j < j0 + nj))
            def _():
                rest[0][...] = acc


def _norm_matmul(x, g, w, tm, tn, out_dtype=F32, side=None):
    m, d = x.shape
    n = w.shape[1]
    out_specs = [pl.BlockSpec((tm, tn), lambda i, j: (i, j))]
    out_shape = [jax.ShapeDtypeStruct((m, n), out_dtype)]
    tiles = None
    if side is not None:
        row0, col0, ncols = side
        assert row0 % tm == 0 and col0 % tn == 0 and ncols % tn == 0 and col0 >= tn
        i0, j0, nj = row0 // tm, col0 // tn, ncols // tn
        tiles = (i0, j0, nj)
        out_specs.append(pl.BlockSpec(
            (tm, tn), lambda i, j: (jnp.maximum(i - i0, 0), jnp.where(i >= i0, jnp.clip(j - j0, 0, nj - 1), 0))))
        out_shape.append(jax.ShapeDtypeStruct((m - row0, ncols), F32))
    res = pl.pallas_call(
        functools.partial(_norm_matmul_kernel, side=tiles),
        grid=(m // tm, n // tn),
        in_specs=[pl.BlockSpec((tm, d), lambda i, j: (i, 0)),
                  pl.BlockSpec((1, d), lambda i, j: (0, 0)),
                  pl.BlockSpec((d, tn), lambda i, j: (0, j))],
        out_specs=out_specs,
        out_shape=out_shape,
        scratch_shapes=[pltpu.VMEM((tm, d), BF16)],
        compiler_params=_params("arbitrary", "arbitrary"),
        name="norm_matmul",
    )(x, g.reshape(1, d), w)
    return res if side is not None else res[0]


def _norm_matmul_cast_kernel(x_ref, g_ref, w_ref, o_ref, wb_ref, xn_ref):
    @pl.when(pl.program_id(0) == 0)
    def _():
        x = x_ref[...]
        ms = jnp.mean(x * x, axis=-1, keepdims=True)
        xn_ref[...] = (x * lax.rsqrt(ms + NORM_EPS) * g_ref[...]).astype(BF16)

    wb = w_ref[...].astype(BF16)
    wb_ref[...] = wb
    o_ref[...] = _dot(xn_ref[...], wb).astype(o_ref.dtype)


def _norm_matmul_cast(x, g, w, tn):
    m, d = x.shape
    n = w.shape[1]
    return pl.pallas_call(
        _norm_matmul_cast_kernel,
        grid=(n // tn,),
        in_specs=[pl.BlockSpec((m, d), lambda j: (0, 0)),
                  pl.BlockSpec((1, d), lambda j: (0, 0)),
                  pl.BlockSpec((d, tn), lambda j: (0, j))],
        out_specs=[pl.BlockSpec((m, tn), lambda j: (0, j)),
                   pl.BlockSpec((d, tn), lambda j: (0, j))],
        out_shape=[jax.ShapeDtypeStruct((m, n), F32),
                   jax.ShapeDtypeStruct((d, n), BF16)],
        scratch_shapes=[pltpu.VMEM((m, d), BF16)],
        compiler_params=_params("arbitrary"),
        name="norm_matmul_cast",
    )(x, g.reshape(1, d), w)


def _ret_tables(c, reps, pad=None):
    pad = c if pad is None else pad
    rows = pad * reps
    lg = np.log1p(-np.exp2(-5.0 - np.arange(RET_HEADS, dtype=np.float64)))
    row = np.arange(rows)
    step = (row % pad).astype(np.float64)
    live = (row % pad) < c
    rel = step[:, None] - step[None, :]
    keep = ((row // pad)[:, None] == (row // pad)[None, :]) & (rel >= 0) & live[:, None] & live[None, :]
    decay = np.where(keep[None], np.exp(lg[:, None, None] * np.maximum(rel, 0.0)[None]), 0.0)
    q_decay = np.where(live[None], np.exp(lg[:, None] * (step[None, :] + 1.0)), 0.0)
    k_decay = np.where(live[None], np.exp(lg[:, None] * (c - 1.0 - step[None, :])), 0.0)
    chunk_decay = np.exp(lg * c)
    f32 = lambda a, shape: np.ascontiguousarray(np.broadcast_to(a, shape), dtype=np.float32)
    qd = f32(q_decay[:, :, None], (RET_HEADS, rows, RET_DK))
    kd = f32(k_decay[:, :, None], (RET_HEADS, rows, RET_DK))
    cd = f32(chunk_decay[:, None, None], (RET_HEADS, 1, RET_DV))
    return decay.astype(np.float32), qd, kd, cd


def _group_norm_gate(o, gn, g):
    mu = jnp.mean(o, axis=-1, keepdims=True)
    d = o - mu
    var = jnp.mean(d * d, axis=-1, keepdims=True)
    return d * lax.rsqrt(var + NORM_EPS) * gn * _silu(g.astype(F32))


def _ret_prompt_kernel(q_ref, k_ref, v_ref, g_ref, cos_ref, sin_ref, dec_ref, qd_ref, kd_ref, cd_ref, gn_ref,
                       y_ref, st_ref):
    @pl.when(pl.program_id(0) == 0)
    def _():
        st_ref[...] = jnp.zeros_like(st_ref)

    c = dec_ref.shape[1]
    for i in range(q_ref.shape[0] // c):
        rs = slice(i * c, (i + 1) * c)
        cos = cos_ref[rs, :]
        sin = sin_ref[rs, :]
        for h in range(RET_HEADS):
            ks = slice(h * RET_DK, (h + 1) * RET_DK)
            vs = slice(h * RET_DV, (h + 1) * RET_DV)
            q = _rot_full(q_ref[rs, ks].astype(F32), cos, sin)
            k = _rot_full(k_ref[rs, ks].astype(F32) * (RET_DK ** -0.5), cos, sin)
            v = v_ref[rs, vs].astype(BF16)
            st = st_ref[h]
            s = _dot_nt(q.astype(BF16), k.astype(BF16)) * dec_ref[h]
            lhs = jnp.concatenate([s.astype(BF16), (q * qd_ref[h]).astype(BF16)], axis=1)
            o = _dot(lhs, jnp.concatenate([v, st.astype(BF16)], axis=0))
            st_ref[h] = cd_ref[h] * st + _dot((k * kd_ref[h]).T.astype(BF16), v)
            y_ref[rs, vs] = _group_norm_gate(o, gn_ref[:, vs], g_ref[rs, vs]).astype(y_ref.dtype)


def _ret_prompt(proj, cos, sin, ret_gn):
    t = proj.shape[0]
    dec, qd, kd, cd = _ret_tables(RET_PROMPT_CHUNK, 1)
    c = RET_STEP_CHUNKS * RET_PROMPT_CHUNK
    assert t % c == 0
    qw = RET_HEADS * RET_DK
    vw = RET_HEADS * RET_DV
    full3 = lambda n: (0, 0, 0)
    return pl.pallas_call(
        _ret_prompt_kernel,
        grid=(t // c,),
        in_specs=[pl.BlockSpec((c, qw), lambda n: (n, E_AQ // qw)),
                  pl.BlockSpec((c, qw), lambda n: (n, E_AK // qw)),
                  pl.BlockSpec((c, vw), lambda n: (n, E_AV // vw)),
                  pl.BlockSpec((c, vw), lambda n: (n, E_AG // vw)),
                  pl.BlockSpec((c, RET_DK), lambda n: (n, 0)),
                  pl.BlockSpec((c, RET_DK), lambda n: (n, 0)),
                  pl.BlockSpec(dec.shape, full3),
                  pl.BlockSpec(qd.shape, full3),
                  pl.BlockSpec(kd.shape, full3),
                  pl.BlockSpec(cd.shape, full3),
                  pl.BlockSpec((1, vw), lambda n: (0, 0))],
        out_specs=[pl.BlockSpec((c, vw), lambda n: (n, 0)),
                   pl.BlockSpec((RET_HEADS, RET_DK, RET_DV), full3)],
        out_shape=[jax.ShapeDtypeStruct((t, vw), BF16),
                   jax.ShapeDtypeStruct((RET_HEADS, RET_DK, RET_DV), F32)],
        compiler_params=_params("arbitrary"),
        name="ret_prompt",
    )(proj, proj, proj, proj, cos, sin, dec, qd, kd, cd, ret_gn.reshape(1, vw))


def _ret_sample_kernel(q_ref, k_ref, v_ref, g_ref, cos_ref, sin_ref, dec_ref, qd_ref, kd_ref, cd_ref, gn_ref,
                       st_ref, y_ref, sto_ref, cross_ref, qs_ref, kt_ref, *, pad):
    b = pl.program_id(0)
    nb = pl.num_programs(0)
    win = RET_CHUNK

    @pl.when(b == 0)
    def _():
        cross_ref[...] = jnp.zeros_like(cross_ref)
        cos = cos_ref[...]
        sin = sin_ref[...]
        for h in range(RET_HEADS):
            ks = slice(h * RET_DK, (h + 1) * RET_DK)
            qs_ref[:, ks] = _rot_full(q_ref[:, ks], cos, sin) * qd_ref[h]
            kt_ref[h] = (_rot_full(k_ref[:, ks] * (RET_DK ** -0.5), cos, sin) * kd_ref[h]).T

    for i in range(st_ref.shape[0]):
        sq = b * st_ref.shape[0] + i
        w0 = pl.multiple_of(jnp.right_shift(sq * pad, win.bit_length() - 1) * win, win)
        wrows = pl.ds(w0, win)
        rows = lax.broadcasted_iota(jnp.int32, (win, RET_DK), 0) + w0
        cols = lax.broadcasted_iota(jnp.int32, (RET_DK, win), 1) + w0
        mine_r = (rows >= sq * pad) & (rows < (sq + 1) * pad)
        mine_c = (cols >= sq * pad) & (cols < (sq + 1) * pad)
        for h in range(RET_HEADS):
            ks = slice(h * RET_DK, (h + 1) * RET_DK)
            vs = slice(h * RET_DV, (h + 1) * RET_DV)
            st = st_ref[i, h]
            qm = jnp.where(mine_r, qs_ref[wrows, ks], 0.0)
            kmt = jnp.where(mine_c, kt_ref[h, :, wrows], 0.0)
            cross_ref[wrows, vs] += _dot(qm.astype(BF16), st.astype(BF16))
            sto_ref[i, h] = cd_ref[h] * st + _dot(kmt.astype(BF16), v_ref[wrows, vs].astype(BF16))

    @pl.when(b == nb - 1)
    def _():
        cos_all = cos_ref[...]
        sin_all = sin_ref[...]
        for h in range(RET_HEADS):
            ks = slice(h * RET_DK, (h + 1) * RET_DK)
            vs = slice(h * RET_DV, (h + 1) * RET_DV)
            q = _rot_full(q_ref[:, ks], cos_all, sin_all)
            k = _rot_full(k_ref[:, ks] * (RET_DK ** -0.5), cos_all, sin_all)
            v = v_ref[:, vs].astype(BF16)
            s = _dot_nt(q.astype(BF16), k.astype(BF16)) * dec_ref[h]
            o = _dot(s.astype(BF16), v) + cross_ref[:, vs]
            y_ref[:, vs] = _group_norm_gate(o, gn_ref[:, vs], g_ref[:, vs]).astype(y_ref.dtype)


def _ret_sample(proj, state, cos, sin, ret_gn, nseq, seq, pad):
    rows = nseq * pad
    c = math.gcd(seq, RET_CHUNK)
    assert c == seq, "sample chunk must cover the new tokens"
    assert rows % RET_CHUNK == 0 and RET_CHUNK % pad == 0
    dec, qd, kd, cd = _ret_tables(c, nseq, pad)
    qw = RET_HEADS * RET_DK
    vw = RET_HEADS * RET_DV
    full3 = lambda b: (0, 0, 0)
    n = RET_STEP_SEQS
    assert nseq % n == 0
    st_spec = pl.BlockSpec((n, RET_HEADS, RET_DK, RET_DV), lambda b: (b, 0, 0, 0))
    return pl.pallas_call(
        functools.partial(_ret_sample_kernel, pad=pad),
        grid=(nseq // n,),
        in_specs=[pl.BlockSpec((rows, qw), lambda b: (0, E_AQ // qw)),
                  pl.BlockSpec((rows, qw), lambda b: (0, E_AK // qw)),
                  pl.BlockSpec((rows, vw), lambda b: (0, E_AV // vw)),
                  pl.BlockSpec((rows, vw), lambda b: (0, E_AG // vw)),
                  pl.BlockSpec((rows, RET_DK), lambda b: (0, 0)),
                  pl.BlockSpec((rows, RET_DK), lambda b: (0, 0)),
                  pl.BlockSpec(dec.shape, full3),
                  pl.BlockSpec(qd.shape, full3),
                  pl.BlockSpec(kd.shape, full3),
                  pl.BlockSpec(cd.shape, full3),
                  pl.BlockSpec((1, vw), lambda b: (0, 0)),
                  st_spec],
        out_specs=[pl.BlockSpec((rows, vw), lambda b: (0, 0)), st_spec],
        out_shape=[jax.ShapeDtypeStruct((rows, vw), F32),
                   jax.ShapeDtypeStruct(state.shape, F32)],
        scratch_shapes=[pltpu.VMEM((rows, vw), F32),
                        pltpu.VMEM((rows, qw), F32),
                        pltpu.VMEM((RET_HEADS, RET_DK, rows), F32)],
        compiler_params=_params("arbitrary"),
        name="ret_sample",
    )(proj, proj, proj, proj, cos, sin, dec, qd, kd, cd, ret_gn.reshape(1, vw), state)


def _swa_prompt_kernel(q_ref, kc_ref, vc_ref, g_ref, kf_ref, cos_ref, sin_ref,
                       y_ref, ko_ref, q_s, k_s, v_s, m_s, l_s, acc_s):
    t = pl.program_id(1)
    nt = pl.num_programs(1)
    tb = SWA_TILE
    blk = SWA_BLOCK
    hd = SWA_HD
    cos = cos_ref[...]
    sin = sin_ref[...]

    @pl.when(t == 0)
    def _():
        k_s[0:tb, :] = jnp.zeros((tb, hd), F32)
        v_s[0:tb, :] = jnp.zeros((tb, hd), F32)

    @pl.when(t > 0)
    def _():
        k_s[0:tb, :] = k_s[tb:2 * tb, :]
        v_s[0:tb, :] = v_s[tb:2 * tb, :]

    q_s[...] = _rot_partial(q_ref[...].astype(F32) * (SWA_HD ** -0.5 * LOG2E), cos, sin)
    k_s[tb:2 * tb, :] = _rot_partial(kc_ref[...].astype(F32), cos, sin)
    v_s[tb:2 * tb, :] = vc_ref[...].astype(F32)

    @pl.when(t == nt - 1)
    def _():
        ko_ref[...] = _rot_partial(kf_ref[...], cos, sin)

    qi = lax.broadcasted_iota(jnp.int32, (blk, 2 * blk), 0)
    kj = lax.broadcasted_iota(jnp.int32, (blk, 2 * blk), 1)
    band = (kj >= qi) & (kj <= qi + blk)
    band_first = band & ((kj >= blk) | (t > 0))
    ones = jnp.ones((2 * blk, hd), BF16)

    def ds(start, size, stride):
        return pl.ds(start, size) if stride == 1 else pl.ds(start, size, stride=stride)

    for bi, (window, r) in enumerate(SWA_PATTERNS):
        assert window // r == blk
        for idx in range(tb // blk):
            c, n = idx % r, idx // r
            qs = n * blk * r + c
            ks = qs + tb - blk * r
            qrows = ds(qs, blk, r)
            krows = ds(ks, 2 * blk, r)
            valid = band_first if n == 0 else band
            s = _dot_nt(q_s[qrows, :].astype(BF16), k_s[krows, :].astype(BF16))
            s = jnp.where(valid, s, NEG_INF)
            m = jnp.max(s, axis=-1, keepdims=True)
            p = jnp.exp2(s - m).astype(BF16)
            pv = _dot(p, jnp.concatenate([v_s[krows, :].astype(BF16), ones], axis=1))
            m_s[bi, qrows, :] = jnp.broadcast_to(m, (blk, hd))
            l_s[bi, qrows, :] = pv[:, hd:]
            acc_s[bi, qrows, :] = pv[:, :hd]

    nbr = len(SWA_PATTERNS)
    ms = [m_s[i] for i in range(nbr)]
    mmax = functools.reduce(jnp.maximum, ms)
    wts = [jnp.exp2(mi - mmax) for mi in ms]
    num = sum(wts[i] * acc_s[i] for i in range(nbr))
    den = sum(wts[i] * l_s[i] for i in range(nbr))
    y_ref[...] = (num / den * _silu(g_ref[...].astype(F32))).astype(y_ref.dtype)


def _swa_prompt(proj, k_last, cos, sin):
    t = proj.shape[0]
    tb = SWA_TILE
    assert t % tb == 0
    hd = SWA_HD
    nbr = len(SWA_PATTERNS)
    cur = lambda off: pl.BlockSpec((tb, hd), lambda h, i: (i, off // hd + h))
    return pl.pallas_call(
        _swa_prompt_kernel,
        grid=(SWA_HEADS, t // tb),
        in_specs=[cur(E_BQ), cur(E_BK), cur(E_BV), cur(E_BG),
                  pl.BlockSpec((tb, hd), lambda h, i: (0, h)),
                  pl.BlockSpec((tb, hd), lambda h, i: (i, 0)),
                  pl.BlockSpec((tb, hd), lambda h, i: (i, 0))],
        out_specs=[pl.BlockSpec((tb, hd), lambda h, i: (i, h)),
                   pl.BlockSpec((tb, hd), lambda h, i: (0, h))],
        out_shape=[jax.ShapeDtypeStruct((t, SWA_HEADS * hd), BF16),
                   jax.ShapeDtypeStruct((tb, SWA_HEADS * hd), F32)],
        scratch_shapes=[pltpu.VMEM((tb, hd), F32),
                        pltpu.VMEM((2 * tb, hd), F32),
                        pltpu.VMEM((2 * tb, hd), F32),
                        pltpu.VMEM((nbr, tb, hd), F32),
                        pltpu.VMEM((nbr, tb, hd), F32),
                        pltpu.VMEM((nbr, tb, hd), F32)],
        compiler_params=_params("arbitrary", "arbitrary"),
        name="swa_prompt",
    )(proj, proj, proj, proj, k_last, cos, sin)


def _swa_sample_kernel(q_ref, k_ref, v_ref, g_ref, cos_ref, sin_ref, ka_ref, kb_ref, va_ref, vb_ref,
                       y_ref, ko_ref, kn_s, vn_s, ba_s, bb_s, bn_s, *, seq):
    pad = SAMPLE_PAD
    nh = SWA_HEADS
    hd = SWA_HD
    nr = nh * pad
    nj, nrh, _ = kb_ref.shape[1:]
    na = ka_ref.shape[1]
    nb = nj * nrh
    nn = kn_s.shape[0]

    @pl.when(pl.program_id(0) == 0)
    def _():
        (w1, d1), (w2, d2), (w3, d3) = SWA_PATTERNS

        def rows_cols(n):
            r = lax.broadcasted_iota(jnp.int32, (nr, n), 0)
            c = lax.broadcasted_iota(jnp.int32, (nr, n), 1)
            return _div(r, pad), _mod(r, pad), c

        def log2_mult(mult):
            return jnp.where(mult > 0, jnp.log2(jnp.maximum(mult, 1.0)), NEG_INF)

        hq, tq, c = rows_cols(na)
        dist = na // nh + tq - _div(c, nh)
        mult_a = ((dist <= w1).astype(F32) + ((_mod(dist, d2) == 0) & (dist <= w2)).astype(F32)
                  + ((_mod(dist, d3) == 0) & (dist <= w3)).astype(F32))
        ba_s[...] = log2_mult(jnp.where((hq == _mod(c, nh)) & (tq < seq), mult_a, 0.0))
        hq, tq, c = rows_cols(nb)
        bb_s[...] = log2_mult(jnp.where((hq == _mod(c, nh)) & (tq == _mod(_div(c, nh), nrh // nh)), 1.0, 0.0))
        hq, tq, c = rows_cols(nn)
        tk = _mod(c, pad)
        mult_n = (tk <= tq).astype(F32) + 2.0 * (tk == tq).astype(F32)
        bn_s[...] = log2_mult(jnp.where((hq == _div(c, pad)) & (tq < seq) & (tk < seq), mult_n, 0.0))
        kn_s[...] = jnp.zeros_like(kn_s)
        vn_s[...] = jnp.zeros_like(vn_s)

    cos = cos_ref[...]
    sin = sin_ref[...]
    q8 = _rot_partial(q_ref[0] * (SWA_HD ** -0.5 * LOG2E), cos, sin)
    k8 = _rot_partial(k_ref[0], cos, sin)
    ko_ref[0] = k8

    def head_rows(x):
        return jnp.concatenate([x[:, h * hd:(h + 1) * hd] for h in range(nh)], axis=0)

    qrows = head_rows(q8).astype(BF16)
    kn_s[0:nr, :] = head_rows(k8).astype(BF16)
    vn_s[0:nr, :] = head_rows(v_ref[0]).astype(BF16)

    kb = kb_ref[0].reshape(nb, hd)
    vb = vb_ref[0].reshape(nb, hd)
    s_a = _dot_nt(qrows, ka_ref[0].astype(BF16)) + ba_s[...]
    s_b = _dot_nt(qrows, kb.astype(BF16)) + bb_s[...]
    s_n = _dot_nt(qrows, kn_s[...]) + bn_s[...]
    rowmax = lambda s: jnp.max(s, axis=-1, keepdims=True)
    m = jnp.maximum(jnp.maximum(rowmax(s_a), rowmax(s_b)), rowmax(s_n))
    p_a, p_b, p_n = jnp.exp2(s_a - m), jnp.exp2(s_b - m), jnp.exp2(s_n - m)
    rowsum = lambda p: jnp.sum(p, axis=-1, keepdims=True)
    o = (_dot(p_a.astype(BF16), va_ref[0].astype(BF16)) + _dot(p_b.astype(BF16), vb.astype(BF16))
         + _dot(p_n.astype(BF16), vn_s[...])) / (rowsum(p_a) + rowsum(p_b) + rowsum(p_n))
    y = jnp.concatenate([o[h * pad:(h + 1) * pad] for h in range(nh)], axis=1)
    y_ref[0] = y * _silu(g_ref[0])


def _swa_sample(proj_pad, cache_k, cache_v, cos, sin, seq):
    nseq, pad, _ = proj_pad.shape
    buf = cache_k.shape[1]
    nh, hd = SWA_HEADS, SWA_HD
    w = nh * hd
    (w1, d1), (w2, d2), (w3, d3) = SWA_PATTERNS
    assert buf == w3 and d1 == 1 and seq <= d2 and w1 <= w2 and buf % w2 == 0 and buf % d3 == 0
    assert nh * pad <= SWA_BLOCK and (seq * nh) % 8 == 0
    flat = lambda a: a.reshape(nseq, buf * nh, hd)
    strided = lambda a: a.reshape(nseq, buf // d3, d3 * nh, hd)
    col = lambda off: pl.BlockSpec((1, pad, w), lambda b: (b, 0, off // w))
    tail = pl.BlockSpec((1, w2 * nh, hd), lambda b: (b, buf // w2 - 1, 0))
    resid = pl.BlockSpec((1, (buf - w2) // d3, seq * nh, hd), lambda b: (b, 0, 0, 0))
    tab = pl.BlockSpec((pad, w), lambda b: (0, 0))
    out = pl.BlockSpec((1, pad, w), lambda b: (b, 0, 0))
    return pl.pallas_call(
        functools.partial(_swa_sample_kernel, seq=seq),
        grid=(nseq,),
        in_specs=[col(E_BQ), col(E_BK), col(E_BV), col(E_BG), tab, tab, tail, resid, tail, resid],
        out_specs=[out, out],
        out_shape=[jax.ShapeDtypeStruct((nseq, pad, w), F32),
                   jax.ShapeDtypeStruct((nseq, pad, w), F32)],
        scratch_shapes=[pltpu.VMEM((SWA_BLOCK, hd), BF16), pltpu.VMEM((SWA_BLOCK, hd), BF16),
                        pltpu.VMEM((nh * pad, w2 * nh), F32),
                        pltpu.VMEM((nh * pad, (buf - w2) // d3 * seq * nh), F32),
                        pltpu.VMEM((nh * pad, SWA_BLOCK), F32)],
        compiler_params=_params("arbitrary"),
        name="swa_sample",
    )(proj_pad, proj_pad, proj_pad, proj_pad, cos, sin,
      flat(cache_k), strided(cache_k), flat(cache_v), strided(cache_v))


def _mem_attn_kernel(q_ref, g_ref, k_ref, v_ref, o_ref):
    ones = jnp.ones((MEM_LEN, MEM_HD), BF16)
    for h in range(MEM_HEADS):
        hs = slice(h * MEM_HD, (h + 1) * MEM_HD)
        q = (q_ref[0, :, hs].astype(F32) * (MEM_HD ** -0.5 * LOG2E)).astype(BF16)
        s = _dot_nt(q, k_ref[0, :, hs].astype(BF16))
        m = jnp.max(s, axis=-1, keepdims=True)
        p = jnp.exp2(s - m).astype(BF16)
        pv = _dot(p, jnp.concatenate([v_ref[0, :, hs].astype(BF16), ones], axis=1))
        o = pv[:, :MEM_HD] / pv[:, MEM_HD:]
        o_ref[0, :, hs] = (o * _silu(g_ref[0, :, hs].astype(F32))).astype(o_ref.dtype)


def _mem_attn_rows_kernel(q_ref, g_ref, k_ref, v_ref, o_ref, bias_s):
    nseq, pad, _ = q_ref.shape
    nh, hd = MEM_HEADS, MEM_HD

    @pl.when(pl.program_id(0) == 0)
    def _():
        r = lax.broadcasted_iota(jnp.int32, bias_s.shape, 0)
        c = lax.broadcasted_iota(jnp.int32, bias_s.shape, 1)
        bias_s[...] = jnp.where(_div(r, pad) == _mod(c, nh), 0.0, NEG_INF)

    for i in range(nseq):
        q = q_ref[i] * (MEM_HD ** -0.5 * LOG2E)
        qrows = jnp.concatenate([q[:, h * hd:(h + 1) * hd] for h in range(nh)], axis=0).astype(BF16)
        s = _dot_nt(qrows, k_ref[i].astype(BF16)) + bias_s[...]
        m = jnp.max(s, axis=-1, keepdims=True)
        p = jnp.exp2(s - m)
        o = _dot(p.astype(BF16), v_ref[i].astype(BF16)) / jnp.sum(p, axis=-1, keepdims=True)
        y = jnp.concatenate([o[h * pad:(h + 1) * pad] for h in range(nh)], axis=1)
        o_ref[i] = y * _silu(g_ref[i])


def _mem_attn_rows(proj_pad, q_off, g_off, mem_k, mem_v, layer):
    b, pad, _ = proj_pad.shape
    w = MEM_W
    rows = MEM_LEN * MEM_HEADS
    n = MEM_ROWS_STEP_SEQS
    assert b % n == 0
    kv = pl.BlockSpec((n, rows, MEM_HD), lambda i: (layer * (b // n) + i, 0, 0))
    return pl.pallas_call(
        _mem_attn_rows_kernel,
        grid=(b // n,),
        in_specs=[pl.BlockSpec((n, pad, w), lambda i: (i, 0, q_off // w)),
                  pl.BlockSpec((n, pad, w), lambda i: (i, 0, g_off // w)),
                  kv, kv],
        out_specs=pl.BlockSpec((n, pad, w), lambda i: (i, 0, 0)),
        out_shape=jax.ShapeDtypeStruct((b, pad, w), F32),
        scratch_shapes=[pltpu.VMEM((MEM_HEADS * pad, rows), F32)],
        compiler_params=_params("arbitrary"),
        name="mem_attn_rows",
    )(proj_pad, proj_pad, mem_k.reshape(-1, rows, MEM_HD), mem_v.reshape(-1, rows, MEM_HD))


def _mem_attn(proj, q_off, g_off, mem_k, k_off, mem_v, v_off, tq, out_dtype):
    b, t, _ = proj.shape
    w = MEM_W
    return pl.pallas_call(
        _mem_attn_kernel,
        grid=(b, t // tq),
        in_specs=[pl.BlockSpec((1, tq, w), lambda i, j: (i, j, q_off // w)),
                  pl.BlockSpec((1, tq, w), lambda i, j: (i, j, g_off // w)),
                  pl.BlockSpec((1, MEM_LEN, w), lambda i, j: (i, 0, k_off // w)),
                  pl.BlockSpec((1, MEM_LEN, w), lambda i, j: (i, 0, v_off // w))],
        out_specs=pl.BlockSpec((1, tq, w), lambda i, j: (i, j, 0)),
        out_shape=jax.ShapeDtypeStruct((b, t, w), out_dtype),
        compiler_params=_params("arbitrary", "arbitrary"),
        name="mem_attn",
    )(proj, proj, mem_k, mem_v)


def _gmlp_kernel(u_ref, v_ref, z_ref, ln_ref, w_ref, b_ref, y_ref, vo_ref, wc_ref):
    c = GMLP_CHUNK

    @pl.when(pl.program_id(0) == 0)
    def _():
        causal = (lax.broadcasted_iota(jnp.int32, (c, c), 0) >= lax.broadcasted_iota(jnp.int32, (c, c), 1))
        for g in range(GMLP_GROUPS):
            wc_ref[g] = jnp.where(causal, w_ref[g], 0.0).astype(BF16)

    nchunks = u_ref.shape[0] // c
    keep_all = vo_ref.shape[0] == u_ref.shape[0]
    for i in range(nchunks):
        rs = slice(i * c, (i + 1) * c)
        v = _gelu(v_ref[rs, :].astype(F32))
        mu = jnp.mean(v, axis=-1, keepdims=True)
        d = v - mu
        var = jnp.mean(d * d, axis=-1, keepdims=True)
        vn = d * lax.rsqrt(var + NORM_EPS) * ln_ref[...]
        if keep_all:
            vo_ref[rs, :] = vn
        elif i == nchunks - 1:
            vo_ref[...] = vn
        for g in range(GMLP_GROUPS):
            gs = slice(g * GMLP_GROUP, (g + 1) * GMLP_GROUP)
            mixed = _dot(wc_ref[g], vn[:, gs].astype(BF16)) + b_ref[:, gs]
            y_ref[rs, gs] = (_gelu(u_ref[rs, gs].astype(F32)) * mixed
                             * _silu(z_ref[rs, gs].astype(F32))).astype(y_ref.dtype)


def _gmlp(proj, gmlp_ln, w_mix, bias_rows, keep_all_v):
    t = proj.shape[0]
    c = GMLP_CHUNK
    w = GMLP_WIDTH
    rows = GMLP_STEP_CHUNKS * c
    assert t % rows == 0
    col = lambda off: pl.BlockSpec((rows, w), lambda n: (n, off // w))
    return pl.pallas_call(
        _gmlp_kernel,
        grid=(t // rows,),
        in_specs=[col(O_U), col(O_V), col(O_Z),
                  pl.BlockSpec((1, w), lambda n: (0, 0)),
                  pl.BlockSpec(w_mix.shape, lambda n: (0, 0, 0)),
                  pl.BlockSpec((c, w), lambda n: (0, 0))],
        out_specs=[pl.BlockSpec((rows, w), lambda n: (n, 0)),
                   pl.BlockSpec((rows, w), lambda n: (n, 0)) if keep_all_v
                   else pl.BlockSpec((c, w), lambda n: (0, 0))],
        out_shape=[jax.ShapeDtypeStruct((t, w), BF16),
                   jax.ShapeDtypeStruct((t if keep_all_v else c, w), F32)],
        scratch_shapes=[pltpu.VMEM(w_mix.shape, BF16)],
        compiler_params=_params("arbitrary"),
        name="gmlp",
    )(proj, proj, proj, gmlp_ln.reshape(1, w), w_mix, bias_rows)


def _out_proj_kernel(*refs, n_in):
    a_refs, w_refs = refs[:n_in], refs[n_in:2 * n_in]
    g_ref, x_ref, o_ref = refs[2 * n_in:]
    sub = min(OUT_PROJ_SUB_ROWS, o_ref.shape[0])
    for r0 in range(0, o_ref.shape[0], sub):
        rs = slice(r0, r0 + sub)
        acc = _dot(a_refs[0][rs, :].astype(BF16), w_refs[0][...])
        for a_ref, w_ref in zip(a_refs[1:], w_refs[1:]):
            acc = acc + _dot(a_ref[rs, :].astype(BF16), w_ref[...])
        ms = jnp.mean(acc * acc, axis=-1, keepdims=True)
        o_ref[rs, :] = x_ref[rs, :] + acc * lax.rsqrt(ms + NORM_EPS) * g_ref[...]


def _out_proj(acts, w, g, x, tm):
    m, d = x.shape
    n_in = len(acts)
    w_specs, off = [], 0
    for a in acts:
        k = a.shape[1]
        assert off % k == 0
        w_specs.append(pl.BlockSpec((k, d), functools.partial(lambda i, blk: (blk, 0), blk=off // k),
                                    pipeline_mode=pl.Buffered(1)))
        off += k
    assert off == w.shape[0]
    return pl.pallas_call(
        functools.partial(_out_proj_kernel, n_in=n_in),
        grid=(m // tm,),
        in_specs=([pl.BlockSpec((tm, a.shape[1]), lambda i: (i, 0)) for a in acts]
                  + w_specs
                  + [pl.BlockSpec((1, d), lambda i: (0, 0)),
                     pl.BlockSpec((tm, d), lambda i: (i, 0))]),
        out_specs=pl.BlockSpec((tm, d), lambda i: (i, 0)),
        out_shape=jax.ShapeDtypeStruct((m, d), F32),
        compiler_params=_params("arbitrary"),
        name="out_proj",
    )(*acts, *([w] * n_in), g.reshape(1, d), x)


def _out_proj_cast_kernel(a_ref, w_ref, g_ref, x_ref, o_ref, wb_ref):
    k = pl.program_id(0)

    @pl.when(k == 0)
    def _():
        o_ref[...] = jnp.zeros_like(o_ref)

    wb = w_ref[...].astype(BF16)
    wb_ref[...] = wb
    o_ref[...] += _dot(a_ref[...].astype(BF16), wb)

    @pl.when(k == pl.num_programs(0) - 1)
    def _():
        acc = o_ref[...]
        ms = jnp.mean(acc * acc, axis=-1, keepdims=True)
        o_ref[...] = x_ref[...] + acc * lax.rsqrt(ms + NORM_EPS) * g_ref[...]


def _out_proj_cast(a, w, g, x, tk):
    m, d = x.shape
    kdim = w.shape[0]
    assert kdim % tk == 0 and a.shape == (m, kdim)
    return pl.pallas_call(
        _out_proj_cast_kernel,
        grid=(kdim // tk,),
        in_specs=[pl.BlockSpec((m, tk), lambda k: (0, k)),
                  pl.BlockSpec((tk, d), lambda k: (k, 0)),
                  pl.BlockSpec((1, d), lambda k: (0, 0)),
                  pl.BlockSpec((m, d), lambda k: (0, 0))],
        out_specs=[pl.BlockSpec((m, d), lambda k: (0, 0)),
                   pl.BlockSpec((tk, d), lambda k: (k, 0))],
        out_shape=[jax.ShapeDtypeStruct((m, d), F32),
                   jax.ShapeDtypeStruct((kdim, d), BF16)],
        compiler_params=_params("arbitrary"),
        name="out_proj_cast",
    )(a, w, g.reshape(1, d), x)


def _pad_steps(a, pad):
    nseq, seq, n = a.shape
    return jnp.pad(a, ((0, 0), (0, pad - seq), (0, 0))).reshape(nseq * pad, n)


def kernel(x_prompt, x_sample, state_ret, cache_swa_k, cache_swa_v, cache_mem_k, cache_mem_v, mem_prompt,
           pre_norm, post_norm, mem_norm, w_mem_k, w_mem_v, w_in_even, ret_gn, w_out_even,
           w_in_odd, gmlp_ln, w_spatial, b_spatial, w_out_odd):
    bp, tp, d = x_prompt.shape
    nseq, seq, _ = x_sample.shape
    pad = SAMPLE_PAD
    rows_s = nseq * pad
    c = GMLP_CHUNK
    assert bp == 1 and seq <= pad and rows_s % c == 0 and c % pad == 0
    hp = x_prompt.reshape(tp, d)
    hs = _pad_steps(x_sample, pad)
    mem = mem_prompt.reshape(MEM_LEN, d)
    tm_p = 1024 if tp % 1024 == 0 else 512
    unpad = lambda a: a.reshape(nseq, pad, -1)[:, :seq]

    pos_pad = np.concatenate([PAST_LEN + np.arange(seq), np.zeros((pad - seq,), np.int64)])
    ret_cos_p, ret_sin_p = _rot_tables_range(tp, RET_DK, RET_THETA, RET_DK)
    ret_cos_s, ret_sin_s = _rot_tables(np.tile(pos_pad, nseq), RET_DK, RET_THETA, RET_DK)
    swa_cos_p, swa_sin_p = _rot_tables_range(tp, ROPE_DIMS, ROPE_THETA, SWA_HD)
    swa_cos_s, swa_sin_s = _rot_tables(pos_pad, ROPE_DIMS, ROPE_THETA, SWA_HD)
    swa_cos_s = np.tile(swa_cos_s, (1, SWA_HEADS))
    swa_sin_s = np.tile(swa_sin_s, (1, SWA_HEADS))

    w_mem0 = jnp.concatenate([w_mem_k[0], w_mem_v[0]], axis=1).astype(BF16)
    memkv0 = _norm_matmul(mem, mem_norm[0], w_mem0, MEM_LEN, 2 * MEM_W)

    proj_s, w_in0 = _norm_matmul_cast(hs, pre_norm[0], w_in_even[0], 1024)
    proj_s3 = proj_s.reshape(nseq, pad, EVEN_IN)
    ya_s, st_s = _ret_sample(proj_s, state_ret[0], ret_cos_s, ret_sin_s, ret_gn[0], nseq, seq, pad)
    yb_s, sk = _swa_sample(proj_s3, cache_swa_k[0], cache_swa_v[0], swa_cos_s, swa_sin_s, seq)
    ym_s = _mem_attn_rows(proj_s3, E_MQ, E_MG, cache_mem_k, cache_mem_v, 0)
    y_s = jnp.concatenate([ya_s, yb_s.reshape(rows_s, -1), ym_s.reshape(rows_s, -1)], axis=1)
    hs1, w_out0 = _out_proj_cast(y_s, w_out_even[0], post_norm[0], hs, 512)
    sk = sk[:, :seq].reshape(nseq, seq, SWA_HEADS, SWA_HD)
    sv = unpad(proj_s[:, E_BV:E_BV + SWA_HEADS * SWA_HD]).reshape(nseq, seq, SWA_HEADS, SWA_HD)

    buf_p = min(SWA_MAX_WINDOW, tp)
    proj_p, kv_last = _norm_matmul(hp, pre_norm[0], w_in0, tm_p, 1024, BF16, (tp - buf_p, E_BK, E_BG - E_BK))
    ya_p, st_p = _ret_prompt(proj_p, ret_cos_p, ret_sin_p, ret_gn[0])
    yb_p, pk = _swa_prompt(proj_p, kv_last, swa_cos_p, swa_sin_p)
    ym_p = _mem_attn(proj_p[None], E_MQ, E_MG, memkv0[None], 0, memkv0[None], MEM_W, 1024, BF16)[0]
    hp1 = _out_proj([ya_p, yb_p, ym_p], w_out0, post_norm[0], hp, 512)
    pv = kv_last[:, E_BV - E_BK:]

    w_mem1 = jnp.concatenate([w_mem_k[1], w_mem_v[1]], axis=1).astype(BF16)
    memkv1 = _norm_matmul(mem, mem_norm[1], w_mem1, MEM_LEN, 2 * MEM_W)
    bias_p = jnp.repeat(b_spatial[0].T, GMLP_GROUP, axis=1)
    blockdiag = lambda wg: jnp.kron(jnp.eye(c // pad, dtype=F32),
                                    jnp.pad(wg[:seq, :seq], ((0, pad - seq), (0, pad - seq))))
    w_mix_s = jax.vmap(blockdiag)(w_spatial[0])
    bias_s = jnp.tile(jnp.pad(bias_p[:seq], ((0, pad - seq), (0, 0))), (c // pad, 1))

    proj_s1, w_in1 = _norm_matmul_cast(hs1, pre_norm[1], w_in_odd[0], 1024)
    yc_s, gv_s = _gmlp(proj_s1, gmlp_ln[0], w_mix_s, bias_s, True)
    ym_s1 = _mem_attn_rows(proj_s1.reshape(nseq, pad, ODD_IN), O_MQ, O_MG, cache_mem_k, cache_mem_v, 1)
    y_s1 = jnp.concatenate([yc_s.astype(F32), ym_s1.reshape(rows_s, -1)], axis=1)
    hs2, w_out1 = _out_proj_cast(y_s1, w_out_odd[0], post_norm[1], hs1, 512)

    proj_p1 = _norm_matmul(hp1, pre_norm[1], w_in1, tm_p, 1024, BF16)
    yc_p, gv_p = _gmlp(proj_p1, gmlp_ln[0], w_spatial[0], bias_p, False)
    ym_p1 = _mem_attn(proj_p1[None], O_MQ, O_MG, memkv1[None], 0, memkv1[None], MEM_W, 1024, BF16)[0]
    hp2 = _out_proj([yc_p, ym_p1], w_out1, post_norm[1], hp1, 512)

    memkv = jnp.stack([memkv0, memkv1])
    p_mk = memkv[:, :, :MEM_W].reshape(2, bp, MEM_LEN, MEM_HEADS, MEM_HD)
    p_mv = memkv[:, :, MEM_W:].reshape(2, bp, MEM_LEN, MEM_HEADS, MEM_HD)
    return (hp2.reshape(bp, tp, d), unpad(hs2),
            st_p[None, None],
            pk.reshape(1, bp, buf_p, SWA_HEADS, SWA_HD), pv.reshape(1, bp, buf_p, SWA_HEADS, SWA_HD),
            p_mk, p_mv,
            gv_p[None, None],
            st_s[None],
            sk[None], sv[None],
            unpad(gv_s)[None])
```

```python
import functools
import math

import jax
import jax.numpy as jnp
import numpy as np
from jax import lax
from jax.experimental import pallas as pl
from jax.experimental.pallas import tpu as pltpu

F32 = jnp.float32
BF16 = jnp.bfloat16

D_MODEL = 2048
PAST_LEN = 8192
RET_HEADS = 8
RET_DK = 128
RET_DV = 256
RET_CHUNK = 128
RET_THETA = 10000.0
SWA_HEADS = 8
SWA_HD = 128
SWA_PATTERNS = ((128, 1), (512, 4), (2048, 16))
SWA_MAX_WINDOW = 2048
SWA_BLOCK = 128
ROPE_THETA = 500000.0
ROPE_DIMS = SWA_HD // 4
GMLP_CHUNK = 128
GMLP_WIDTH = D_MODEL
GMLP_GROUP = 128
GMLP_GROUPS = GMLP_WIDTH // GMLP_GROUP
MEM_LEN = 256
MEM_HEADS = 4
MEM_HD = 128
MEM_W = MEM_HEADS * MEM_HD
NORM_EPS = 1e-6
NEG_INF = -1e30
LOG2E = math.log2(math.e)

E_AQ = 0
E_AK = E_AQ + RET_HEADS * RET_DK
E_AV = E_AK + RET_HEADS * RET_DK
E_AG = E_AV + RET_HEADS * RET_DV
E_BQ = E_AG + RET_HEADS * RET_DV
E_BK = E_BQ + SWA_HEADS * SWA_HD
E_BV = E_BK + SWA_HEADS * SWA_HD
E_BG = E_BV + SWA_HEADS * SWA_HD
E_MQ = E_BG + SWA_HEADS * SWA_HD
E_MG = E_MQ + MEM_W
EVEN_IN = E_MG + MEM_W
O_U = 0
O_V = O_U + GMLP_WIDTH
O_Z = O_V + GMLP_WIDTH
O_MQ = O_Z + GMLP_WIDTH
O_MG = O_MQ + MEM_W
ODD_IN = O_MG + MEM_W

SWA_TILE = SWA_MAX_WINDOW
SAMPLE_PAD = 8
GMLP_STEP_CHUNKS = 2
SWA_GROUP = 1
RET_PROMPT_CHUNK = 256
RET_STEP_CHUNKS = 1
MEM_ROWS_STEP_SEQS = 2
RET_STEP_SEQS = 2
OUT_PROJ_SUB_ROWS = 256
NORM_SUB_ROWS = 256
VMEM_LIMIT = 56 * 1024 * 1024


def _params(*sem):
    return pltpu.CompilerParams(dimension_semantics=sem, vmem_limit_bytes=VMEM_LIMIT)


def _dot(a, b):
    return jnp.dot(a, b, preferred_element_type=F32)


def _dot_nt(a, b):
    return lax.dot_general(a, b, (((1,), (1,)), ((), ())), preferred_element_type=F32)


def _silu(x):
    return x / (1.0 + jnp.exp2(x * (-LOG2E)))


def _gelu(x):
    a = -2.0 * math.sqrt(2.0 / math.pi) * LOG2E
    return x / (1.0 + jnp.exp2(x * (a + (a * 0.044715) * (x * x))))


def _div(x, n):
    assert n & (n - 1) == 0
    return jnp.right_shift(x, n.bit_length() - 1)


def _mod(x, n):
    assert n & (n - 1) == 0
    return jnp.bitwise_and(x, n - 1)


def _rot_lanes(n_rot, theta, width):
    half = n_rot // 2
    inv = 1.0 / (theta ** (np.arange(half, dtype=np.float64) / half))
    rest = np.zeros(width - n_rot)
    return np.concatenate([inv, inv, rest]), np.concatenate([-np.ones(half), np.ones(half), rest])


def _rot_tables(pos, n_rot, theta, width):
    f, sgn = _rot_lanes(n_rot, theta, width)
    ang = np.asarray(pos, np.float64)[:, None] * f[None, :]
    return np.cos(ang).astype(np.float32), (sgn * np.sin(ang)).astype(np.float32)


def _rot_tables_range(n, n_rot, theta, width, blk=128):
    assert n % blk == 0
    f, sgn = _rot_lanes(n_rot, theta, width)
    hi = (np.arange(n // blk, dtype=np.float64) * blk)[:, None] * f[None, :]
    lo = np.arange(blk, dtype=np.float64)[:, None] * f[None, :]
    dev = lambda a: jnp.asarray(a, F32)
    ch, sh, cl, sl = dev(np.cos(hi)[:, None]), dev(np.sin(hi)[:, None]), dev(np.cos(lo)[None]), dev(np.sin(lo)[None])
    chs, shs = dev((sgn * np.cos(hi))[:, None]), dev((sgn * np.sin(hi))[:, None])
    return (ch * cl - sh * sl).reshape(n, width), (shs * cl + chs * sl).reshape(n, width)


def _rot_full(x, cos, sin):
    return x * cos + pltpu.roll(x, RET_DK // 2, 1) * sin


def _rot_partial(x, cos, sin):
    half = ROPE_DIMS // 2
    lane = _mod(lax.broadcasted_iota(jnp.int32, x.shape, 1), SWA_HD)
    width = x.shape[1]
    partner = jnp.where(lane < half, pltpu.roll(x, width - half, 1), pltpu.roll(x, half, 1))
    return x * cos + partner * sin


def _norm_matmul_kernel(x_ref, g_ref, w_ref, *rest, side, cast_cols):
    nc = len(cast_cols)
    cast_in, o_ref, rest = rest[:nc], rest[nc], rest[nc + 1:]
    cast_out, xn_ref = rest[len(rest) - 1 - nc:len(rest) - 1], rest[-1]
    tm = x_ref.shape[0]
    sub = min(NORM_SUB_ROWS, tm)

    for src, dst, ncol in zip(cast_in, cast_out, cast_cols):
        @pl.when(pl.program_id(1) < ncol)
        def _(src=src, dst=dst):
            dst[...] = src[...].astype(BF16)

    @pl.when(pl.program_id(1) == 0)
    def _():
        for r0 in range(0, tm, sub):
            rs = slice(r0, r0 + sub)
            x = x_ref[rs, :]
            ms = jnp.mean(x * x, axis=-1, keepdims=True)
            xn = (x * lax.rsqrt(ms + NORM_EPS) * g_ref[...]).astype(BF16)
            xn_ref[rs, :] = xn
            o_ref[rs, :] = _dot(xn, w_ref[...]).astype(o_ref.dtype)

    @pl.when(pl.program_id(1) != 0)
    def _():
        acc = _dot(xn_ref[...], w_ref[...])
        o_ref[...] = acc.astype(o_ref.dtype)
        if side is not None:
            i0, j0, nj = side
            i, j = pl.program_id(0), pl.program_id(1)

            @pl.when((i >= i0) & (j >= j0) & (j < j0 + nj))
            def _():
                rest[0][...] = acc


def _norm_matmul(x, g, w, tm, tn, out_dtype=F32, side=None, casts=()):
    m, d = x.shape
    n = w.shape[1]
    ni, nj_grid = m // tm, n // tn
    out_specs = [pl.BlockSpec((tm, tn), lambda i, j: (i, j))]
    out_shape = [jax.ShapeDtypeStruct((m, n), out_dtype)]
    tiles = None
    if side is not None:
        row0, col0, ncols = side
        assert row0 % tm == 0 and col0 % tn == 0 and ncols % tn == 0 and col0 >= tn
        i0, j0, nj = row0 // tm, col0 // tn, ncols // tn
        tiles = (i0, j0, nj)
        out_specs.append(pl.BlockSpec(
            (tm, tn), lambda i, j: (jnp.maximum(i - i0, 0), jnp.where(i >= i0, jnp.clip(j - j0, 0, nj - 1), 0))))
        out_shape.append(jax.ShapeDtypeStruct((m - row0, ncols), F32))
    cast_specs, cast_cols = [], []
    for a, bc in casts:
        r, c = a.shape
        assert r % ni == 0 and (r // ni) % 16 == 0 and c % bc == 0 and c // bc <= nj_grid
        ncol = c // bc
        spec = pl.BlockSpec((r // ni, bc), functools.partial(lambda i, j, last: (i, jnp.minimum(j, last)), last=ncol - 1))
        cast_specs.append(spec)
        cast_cols.append(ncol)
        out_specs.append(spec)
        out_shape.append(jax.ShapeDtypeStruct((r, c), BF16))
    res = pl.pallas_call(
        functools.partial(_norm_matmul_kernel, side=tiles, cast_cols=tuple(cast_cols)),
        grid=(ni, nj_grid),
        in_specs=[pl.BlockSpec((tm, d), lambda i, j: (i, 0)),
                  pl.BlockSpec((1, d), lambda i, j: (0, 0)),
                  pl.BlockSpec((d, tn), lambda i, j: (0, j))] + cast_specs,
        out_specs=out_specs,
        out_shape=out_shape,
        scratch_shapes=[pltpu.VMEM((tm, d), BF16)],
        compiler_params=_params("arbitrary", "arbitrary"),
        name="norm_matmul",
    )(x, g.reshape(1, d), w, *[a for a, _ in casts])
    return res if len(res) > 1 else res[0]


def _norm_matmul_cast_kernel(x_ref, g_ref, w_ref, o_ref, wb_ref, xn_ref):
    @pl.when(pl.program_id(0) == 0)
    def _():
        x = x_ref[...]
        ms = jnp.mean(x * x, axis=-1, keepdims=True)
        xn_ref[...] = (x * lax.rsqrt(ms + NORM_EPS) * g_ref[...]).astype(BF16)

    wb = w_ref[...].astype(BF16)
    wb_ref[...] = wb
    o_ref[...] = _dot(xn_ref[...], wb).astype(o_ref.dtype)


def _norm_matmul_cast(x, g, w, tn):
    m, d = x.shape
    n = w.shape[1]
    return pl.pallas_call(
        _norm_matmul_cast_kernel,
        grid=(n // tn,),
        in_specs=[pl.BlockSpec((m, d), lambda j: (0, 0)),
                  pl.BlockSpec((1, d), lambda j: (0, 0)),
                  pl.BlockSpec((d, tn), lambda j: (0, j))],
        out_specs=[pl.BlockSpec((m, tn), lambda j: (0, j)),
                   pl.BlockSpec((d, tn), lambda j: (0, j))],
        out_shape=[jax.ShapeDtypeStruct((m, n), F32),
                   jax.ShapeDtypeStruct((d, n), BF16)],
        scratch_shapes=[pltpu.VMEM((m, d), BF16)],
        compiler_params=_params("arbitrary"),
        name="norm_matmul_cast",
    )(x, g.reshape(1, d), w)


def _ret_tables(c, reps, pad=None):
    pad = c if pad is None else pad
    rows = pad * reps
    lg = np.log1p(-np.exp2(-5.0 - np.arange(RET_HEADS, dtype=np.float64)))
    row = np.arange(rows)
    step = (row % pad).astype(np.float64)
    live = (row % pad) < c
    rel = step[:, None] - step[None, :]
    keep = ((row // pad)[:, None] == (row // pad)[None, :]) & (rel >= 0) & live[:, None] & live[None, :]
    decay = np.where(keep[None], np.exp(lg[:, None, None] * np.maximum(rel, 0.0)[None]), 0.0)
    q_decay = np.where(live[None], np.exp(lg[:, None] * (step[None, :] + 1.0)), 0.0)
    k_decay = np.where(live[None], np.exp(lg[:, None] * (c - 1.0 - step[None, :])), 0.0)
    chunk_decay = np.exp(lg * c)
    f32 = lambda a, shape: np.ascontiguousarray(np.broadcast_to(a, shape), dtype=np.float32)
    qd = f32(q_decay[:, :, None], (RET_HEADS, rows, RET_DK))
    kd = f32(k_decay[:, :, None], (RET_HEADS, rows, RET_DK))
    cd = f32(chunk_decay[:, None, None], (RET_HEADS, 1, RET_DV))
    return decay.astype(np.float32), qd, kd, cd


def _group_norm_gate(o, gn, g):
    mu = jnp.mean(o, axis=-1, keepdims=True)
    d = o - mu
    var = jnp.mean(d * d, axis=-1, keepdims=True)
    return d * lax.rsqrt(var + NORM_EPS) * gn * _silu(g.astype(F32))


def _ret_prompt_kernel(q_ref, k_ref, v_ref, g_ref, cos_ref, sin_ref, dec_ref, qd_ref, kd_ref, cd_ref, gn_ref,
                       y_ref, st_ref):
    @pl.when(pl.program_id(0) == 0)
    def _():
        st_ref[...] = jnp.zeros_like(st_ref)

    c = dec_ref.shape[1]
    for i in range(q_ref.shape[0] // c):
        rs = slice(i * c, (i + 1) * c)
        cos = cos_ref[rs, :]
        sin = sin_ref[rs, :]
        for h in range(RET_HEADS):
            ks = slice(h * RET_DK, (h + 1) * RET_DK)
            vs = slice(h * RET_DV, (h + 1) * RET_DV)
            q = _rot_full(q_ref[rs, ks].astype(F32), cos, sin)
            k = _rot_full(k_ref[rs, ks].astype(F32) * (RET_DK ** -0.5), cos, sin)
            v = v_ref[rs, vs].astype(BF16)
            st = st_ref[h]
            s = _dot_nt(q.astype(BF16), k.astype(BF16)) * dec_ref[h]
            lhs = jnp.concatenate([s.astype(BF16), (q * qd_ref[h]).astype(BF16)], axis=1)
            o = _dot(lhs, jnp.concatenate([v, st.astype(BF16)], axis=0))
            st_ref[h] = cd_ref[h] * st + _dot((k * kd_ref[h]).T.astype(BF16), v)
            y_ref[rs, vs] = _group_norm_gate(o, gn_ref[:, vs], g_ref[rs, vs]).astype(y_ref.dtype)


def _ret_prompt(proj, cos, sin, ret_gn):
    t = proj.shape[0]
    dec, qd, kd, cd = _ret_tables(RET_PROMPT_CHUNK, 1)
    c = RET_STEP_CHUNKS * RET_PROMPT_CHUNK
    assert t % c == 0
    qw = RET_HEADS * RET_DK
    vw = RET_HEADS * RET_DV
    full3 = lambda n: (0, 0, 0)
    return pl.pallas_call(
        _ret_prompt_kernel,
        grid=(t // c,),
        in_specs=[pl.BlockSpec((c, qw), lambda n: (n, E_AQ // qw)),
                  pl.BlockSpec((c, qw), lambda n: (n, E_AK // qw)),
                  pl.BlockSpec((c, vw), lambda n: (n, E_AV // vw)),
                  pl.BlockSpec((c, vw), lambda n: (n, E_AG // vw)),
                  pl.BlockSpec((c, RET_DK), lambda n: (n, 0)),
                  pl.BlockSpec((c, RET_DK), lambda n: (n, 0)),
                  pl.BlockSpec(dec.shape, full3),
                  pl.BlockSpec(qd.shape, full3),
                  pl.BlockSpec(kd.shape, full3),
                  pl.BlockSpec(cd.shape, full3),
                  pl.BlockSpec((1, vw), lambda n: (0, 0))],
        out_specs=[pl.BlockSpec((c, vw), lambda n: (n, 0)),
                   pl.BlockSpec((RET_HEADS, RET_DK, RET_DV), full3)],
        out_shape=[jax.ShapeDtypeStruct((t, vw), BF16),
                   jax.ShapeDtypeStruct((RET_HEADS, RET_DK, RET_DV), F32)],
        compiler_params=_params("arbitrary"),
        name="ret_prompt",
    )(proj, proj, proj, proj, cos, sin, dec, qd, kd, cd, ret_gn.reshape(1, vw))


def _ret_sample_kernel(q_ref, k_ref, v_ref, g_ref, cos_ref, sin_ref, dec_ref, qd_ref, kd_ref, cd_ref, gn_ref,
                       st_ref, y_ref, sto_ref, cross_ref, qs_ref, kt_ref, *, pad):
    b = pl.program_id(0)
    nb = pl.num_programs(0)
    win = RET_CHUNK

    @pl.when(b == 0)
    def _():
        cross_ref[...] = jnp.zeros_like(cross_ref)
        cos = cos_ref[...]
        sin = sin_ref[...]
        for h in range(RET_HEADS):
            ks = slice(h * RET_DK, (h + 1) * RET_DK)
            qs_ref[:, ks] = _rot_full(q_ref[:, ks], cos, sin) * qd_ref[h]
            kt_ref[h] = (_rot_full(k_ref[:, ks] * (RET_DK ** -0.5), cos, sin) * kd_ref[h]).T

    for i in range(st_ref.shape[0]):
        sq = b * st_ref.shape[0] + i
        w0 = pl.multiple_of(jnp.right_shift(sq * pad, win.bit_length() - 1) * win, win)
        wrows = pl.ds(w0, win)
        rows = lax.broadcasted_iota(jnp.int32, (win, RET_DK), 0) + w0
        cols = lax.broadcasted_iota(jnp.int32, (RET_DK, win), 1) + w0
        mine_r = (rows >= sq * pad) & (rows < (sq + 1) * pad)
        mine_c = (cols >= sq * pad) & (cols < (sq + 1) * pad)
        for h in range(RET_HEADS):
            ks = slice(h * RET_DK, (h + 1) * RET_DK)
            vs = slice(h * RET_DV, (h + 1) * RET_DV)
            st = st_ref[i, h]
            qm = jnp.where(mine_r, qs_ref[wrows, ks], 0.0)
            kmt = jnp.where(mine_c, kt_ref[h, :, wrows], 0.0)
            cross_ref[wrows, vs] += _dot(qm.astype(BF16), st.astype(BF16))
            sto_ref[i, h] = cd_ref[h] * st + _dot(kmt.astype(BF16), v_ref[wrows, vs].astype(BF16))

    @pl.when(b == nb - 1)
    def _():
        cos_all = cos_ref[...]
        sin_all = sin_ref[...]
        for h in range(RET_HEADS):
            ks = slice(h * RET_DK, (h + 1) * RET_DK)
            vs = slice(h * RET_DV, (h + 1) * RET_DV)
            q = _rot_full(q_ref[:, ks], cos_all, sin_all)
            k = _rot_full(k_ref[:, ks] * (RET_DK ** -0.5), cos_all, sin_all)
            v = v_ref[:, vs].astype(BF16)
            s = _dot_nt(q.astype(BF16), k.astype(BF16)) * dec_ref[h]
            o = _dot(s.astype(BF16), v) + cross_ref[:, vs]
            y_ref[:, vs] = _group_norm_gate(o, gn_ref[:, vs], g_ref[:, vs]).astype(y_ref.dtype)


def _ret_sample(proj, state, cos, sin, ret_gn, nseq, seq, pad):
    rows = nseq * pad
    c = math.gcd(seq, RET_CHUNK)
    assert c == seq, "sample chunk must cover the new tokens"
    assert rows % RET_CHUNK == 0 and RET_CHUNK % pad == 0
    dec, qd, kd, cd = _ret_tables(c, nseq, pad)
    qw = RET_HEADS * RET_DK
    vw = RET_HEADS * RET_DV
    full3 = lambda b: (0, 0, 0)
    n = RET_STEP_SEQS
    assert nseq % n == 0
    st_spec = pl.BlockSpec((n, RET_HEADS, RET_DK, RET_DV), lambda b: (b, 0, 0, 0))
    return pl.pallas_call(
        functools.partial(_ret_sample_kernel, pad=pad),
        grid=(nseq // n,),
        in_specs=[pl.BlockSpec((rows, qw), lambda b: (0, E_AQ // qw)),
                  pl.BlockSpec((rows, qw), lambda b: (0, E_AK // qw)),
                  pl.BlockSpec((rows, vw), lambda b: (0, E_AV // vw)),
                  pl.BlockSpec((rows, vw), lambda b: (0, E_AG // vw)),
                  pl.BlockSpec((rows, RET_DK), lambda b: (0, 0)),
                  pl.BlockSpec((rows, RET_DK), lambda b: (0, 0)),
                  pl.BlockSpec(dec.shape, full3),
                  pl.BlockSpec(qd.shape, full3),
                  pl.BlockSpec(kd.shape, full3),
                  pl.BlockSpec(cd.shape, full3),
                  pl.BlockSpec((1, vw), lambda b: (0, 0)),
                  st_spec],
        out_specs=[pl.BlockSpec((rows, vw), lambda b: (0, 0)), st_spec],
        out_shape=[jax.ShapeDtypeStruct((rows, vw), F32),
                   jax.ShapeDtypeStruct(state.shape, F32)],
        scratch_shapes=[pltpu.VMEM((rows, vw), F32),
                        pltpu.VMEM((rows, qw), F32),
                        pltpu.VMEM((RET_HEADS, RET_DK, rows), F32)],
        compiler_params=_params("arbitrary"),
        name="ret_sample",
    )(proj, proj, proj, proj, cos, sin, dec, qd, kd, cd, ret_gn.reshape(1, vw), state)


def _swa_prompt_kernel(q_ref, kc_ref, vc_ref, g_ref, kf_ref, cos_ref, sin_ref,
                       y_ref, ko_ref, q_s, k_s, v_s, m_s, l_s, acc_s):
    t = pl.program_id(1)
    nt = pl.num_programs(1)
    tb = SWA_TILE
    blk = SWA_BLOCK
    hd = SWA_HD
    cos = cos_ref[...]
    sin = sin_ref[...]

    @pl.when(t == 0)
    def _():
        k_s[0:tb, :] = jnp.zeros((tb, hd), F32)
        v_s[0:tb, :] = jnp.zeros((tb, hd), F32)

    @pl.when(t > 0)
    def _():
        k_s[0:tb, :] = k_s[tb:2 * tb, :]
        v_s[0:tb, :] = v_s[tb:2 * tb, :]

    q_s[...] = _rot_partial(q_ref[...].astype(F32) * (SWA_HD ** -0.5 * LOG2E), cos, sin)
    k_s[tb:2 * tb, :] = _rot_partial(kc_ref[...].astype(F32), cos, sin)
    v_s[tb:2 * tb, :] = vc_ref[...].astype(F32)

    @pl.when(t == nt - 1)
    def _():
        ko_ref[...] = _rot_partial(kf_ref[...], cos, sin)

    def ds(start, size, stride):
        return pl.ds(start, size) if stride == 1 else pl.ds(start, size, stride=stride)

    def band(g):
        qi = lax.broadcasted_iota(jnp.int32, (g * blk, (g + 1) * blk), 0)
        kj = lax.broadcasted_iota(jnp.int32, (g * blk, (g + 1) * blk), 1)
        ok = (kj >= qi) & (kj <= qi + blk)
        return ok, ok & ((kj >= blk) | (t > 0))

    for bi, (window, r) in enumerate(SWA_PATTERNS):
        assert window // r == blk
        per_class = tb // (blk * r)
        g = SWA_GROUP if per_class % SWA_GROUP == 0 else 1
        ok, ok_first = band(g)
        ones = jnp.ones(((g + 1) * blk, hd), BF16)
        for c in range(r):
            for n in range(0, per_class, g):
                qs = n * blk * r + c
                ks = qs + tb - blk * r
                qrows = ds(qs, g * blk, r)
                krows = ds(ks, (g + 1) * blk, r)
                s = _dot_nt(q_s[qrows, :].astype(BF16), k_s[krows, :].astype(BF16))
                s = jnp.where(ok_first if n == 0 else ok, s, NEG_INF)
                m = jnp.max(s, axis=-1, keepdims=True)
                p = jnp.exp2(s - m).astype(BF16)
                pv = _dot(p, jnp.concatenate([v_s[krows, :].astype(BF16), ones], axis=1))
                m_s[bi, qrows, :] = jnp.broadcast_to(m, (g * blk, hd))
                l_s[bi, qrows, :] = pv[:, hd:]
                acc_s[bi, qrows, :] = pv[:, :hd]

    nbr = len(SWA_PATTERNS)
    ms = [m_s[i] for i in range(nbr)]
    mmax = functools.reduce(jnp.maximum, ms)
    wts = [jnp.exp2(mi - mmax) for mi in ms]
    num = sum(wts[i] * acc_s[i] for i in range(nbr))
    den = sum(wts[i] * l_s[i] for i in range(nbr))
    y_ref[...] = (num / den * _silu(g_ref[...].astype(F32))).astype(y_ref.dtype)


def _swa_prompt(proj, k_last, cos, sin):
    t = proj.shape[0]
    tb = SWA_TILE
    assert t % tb == 0
    hd = SWA_HD
    nbr = len(SWA_PATTERNS)
    cur = lambda off: pl.BlockSpec((tb, hd), lambda h, i: (i, off // hd + h))
    return pl.pallas_call(
        _swa_prompt_kernel,
        grid=(SWA_HEADS, t // tb),
        in_specs=[cur(E_BQ), cur(E_BK), cur(E_BV), cur(E_BG),
                  pl.BlockSpec((tb, hd), lambda h, i: (0, h)),
                  pl.BlockSpec((tb, hd), lambda h, i: (i, 0)),
                  pl.BlockSpec((tb, hd), lambda h, i: (i, 0))],
        out_specs=[pl.BlockSpec((tb, hd), lambda h, i: (i, h)),
                   pl.BlockSpec((tb, hd), lambda h, i: (0, h))],
        out_shape=[jax.ShapeDtypeStruct((t, SWA_HEADS * hd), BF16),
                   jax.ShapeDtypeStruct((tb, SWA_HEADS * hd), F32)],
        scratch_shapes=[pltpu.VMEM((tb, hd), F32),
                        pltpu.VMEM((2 * tb, hd), F32),
                        pltpu.VMEM((2 * tb, hd), F32),
                        pltpu.VMEM((nbr, tb, hd), F32),
                        pltpu.VMEM((nbr, tb, hd), F32),
                        pltpu.VMEM((nbr, tb, hd), F32)],
        compiler_params=_params("arbitrary", "arbitrary"),
        name="swa_prompt",
    )(proj, proj, proj, proj, k_last, cos, sin)


def _swa_sample_kernel(q_ref, k_ref, v_ref, g_ref, cos_ref, sin_ref, ka_ref, kb_ref, va_ref, vb_ref,
                       y_ref, ko_ref, kn_s, vn_s, ba_s, bb_s, bn_s, *, seq):
    pad = SAMPLE_PAD
    nh = SWA_HEADS
    hd = SWA_HD
    nr = nh * pad
    nj, nrh, _ = kb_ref.shape[1:]
    na = ka_ref.shape[1]
    nb = nj * nrh
    nn = kn_s.shape[0]

    @pl.when(pl.program_id(0) == 0)
    def _():
        (w1, d1), (w2, d2), (w3, d3) = SWA_PATTERNS

        def rows_cols(n):
            r = lax.broadcasted_iota(jnp.int32, (nr, n), 0)
            c = lax.broadcasted_iota(jnp.int32, (nr, n), 1)
            return _div(r, pad), _mod(r, pad), c

        def log2_mult(mult):
            return jnp.where(mult > 0, jnp.log2(jnp.maximum(mult, 1.0)), NEG_INF)

        hq, tq, c = rows_cols(na)
        dist = na // nh + tq - _div(c, nh)
        mult_a = ((dist <= w1).astype(F32) + ((_mod(dist, d2) == 0) & (dist <= w2)).astype(F32)
                  + ((_mod(dist, d3) == 0) & (dist <= w3)).astype(F32))
        ba_s[...] = log2_mult(jnp.where((hq == _mod(c, nh)) & (tq < seq), mult_a, 0.0))
        hq, tq, c = rows_cols(nb)
        bb_s[...] = log2_mult(jnp.where((hq == _mod(c, nh)) & (tq == _mod(_div(c, nh), nrh // nh)), 1.0, 0.0))
        hq, tq, c = rows_cols(nn)
        tk = _mod(c, pad)
        mult_n = (tk <= tq).astype(F32) + 2.0 * (tk == tq).astype(F32)
        bn_s[...] = log2_mult(jnp.where((hq == _div(c, pad)) & (tq < seq) & (tk < seq), mult_n, 0.0))
        kn_s[...] = jnp.zeros_like(kn_s)
        vn_s[...] = jnp.zeros_like(vn_s)

    cos = cos_ref[...]
    sin = sin_ref[...]
    q8 = _rot_partial(q_ref[0] * (SWA_HD ** -0.5 * LOG2E), cos, sin)
    k8 = _rot_partial(k_ref[0], cos, sin)
    ko_ref[0] = k8

    def head_rows(x):
        return jnp.concatenate([x[:, h * hd:(h + 1) * hd] for h in range(nh)], axis=0)

    qrows = head_rows(q8).astype(BF16)
    kn_s[0:nr, :] = head_rows(k8).astype(BF16)
    vn_s[0:nr, :] = head_rows(v_ref[0]).astype(BF16)

    kb = kb_ref[0].reshape(nb, hd)
    vb = vb_ref[0].reshape(nb, hd)
    s_a = _dot_nt(qrows, ka_ref[0].astype(BF16)) + ba_s[...]
    s_b = _dot_nt(qrows, kb.astype(BF16)) + bb_s[...]
    s_n = _dot_nt(qrows, kn_s[...]) + bn_s[...]
    rowmax = lambda s: jnp.max(s, axis=-1, keepdims=True)
    m = jnp.maximum(jnp.maximum(rowmax(s_a), rowmax(s_b)), rowmax(s_n))
    p_a, p_b, p_n = jnp.exp2(s_a - m), jnp.exp2(s_b - m), jnp.exp2(s_n - m)
    rowsum = lambda p: jnp.sum(p, axis=-1, keepdims=True)
    o = (_dot(p_a.astype(BF16), va_ref[0].astype(BF16)) + _dot(p_b.astype(BF16), vb.astype(BF16))
         + _dot(p_n.astype(BF16), vn_s[...])) / (rowsum(p_a) + rowsum(p_b) + rowsum(p_n))
    y = jnp.concatenate([o[h * pad:(h + 1) * pad] for h in range(nh)], axis=1)
    y_ref[0] = y * _silu(g_ref[0])


def _swa_sample(proj_pad, cache_k, cache_v, cos, sin, seq):
    nseq, pad, _ = proj_pad.shape
    buf = cache_k.shape[1]
    nh, hd = SWA_HEADS, SWA_HD
    w = nh * hd
    (w1, d1), (w2, d2), (w3, d3) = SWA_PATTERNS
    assert buf == w3 and d1 == 1 and seq <= d2 and w1 <= w2 and buf % w2 == 0 and buf % d3 == 0
    assert nh * pad <= SWA_BLOCK and (seq * nh) % 8 == 0
    flat = lambda a: a.reshape(nseq, buf * nh, hd)
    strided = lambda a: a.reshape(nseq, buf // d3, d3 * nh, hd)
    col = lambda off: pl.BlockSpec((1, pad, w), lambda b: (b, 0, off // w))
    tail = pl.BlockSpec((1, w2 * nh, hd), lambda b: (b, buf // w2 - 1, 0))
    resid = pl.BlockSpec((1, (buf - w2) // d3, seq * nh, hd), lambda b: (b, 0, 0, 0))
    tab = pl.BlockSpec((pad, w), lambda b: (0, 0))
    out = pl.BlockSpec((1, pad, w), lambda b: (b, 0, 0))
    return pl.pallas_call(
        functools.partial(_swa_sample_kernel, seq=seq),
        grid=(nseq,),
        in_specs=[col(E_BQ), col(E_BK), col(E_BV), col(E_BG), tab, tab, tail, resid, tail, resid],
        out_specs=[out, out],
        out_shape=[jax.ShapeDtypeStruct((nseq, pad, w), F32),
                   jax.ShapeDtypeStruct((nseq, pad, w), F32)],
        scratch_shapes=[pltpu.VMEM((SWA_BLOCK, hd), BF16), pltpu.VMEM((SWA_BLOCK, hd), BF16),
                        pltpu.VMEM((nh * pad, w2 * nh), F32),
                        pltpu.VMEM((nh * pad, (buf - w2) // d3 * seq * nh), F32),
                        pltpu.VMEM((nh * pad, SWA_BLOCK), F32)],
        compiler_params=_params("arbitrary"),
        name="swa_sample",
    )(proj_pad, proj_pad, proj_pad, proj_pad, cos, sin,
      flat(cache_k), strided(cache_k), flat(cache_v), strided(cache_v))


def _mem_attn_kernel(q_ref, g_ref, k_ref, v_ref, o_ref):
    ones = jnp.ones((MEM_LEN, MEM_HD), BF16)
    for h in range(MEM_HEADS):
        hs = slice(h * MEM_HD, (h + 1) * MEM_HD)
        q = (q_ref[0, :, hs].astype(F32) * (MEM_HD ** -0.5 * LOG2E)).astype(BF16)
        s = _dot_nt(q, k_ref[0, :, hs].astype(BF16))
        m = jnp.max(s, axis=-1, keepdims=True)
        p = jnp.exp2(s - m).astype(BF16)
        pv = _dot(p, jnp.concatenate([v_ref[0, :, hs].astype(BF16), ones], axis=1))
        o = pv[:, :MEM_HD] / pv[:, MEM_HD:]
        o_ref[0, :, hs] = (o * _silu(g_ref[0, :, hs].astype(F32))).astype(o_ref.dtype)


def _mem_attn_rows_kernel(q_ref, g_ref, k_ref, v_ref, o_ref, bias_s):
    nseq, pad, _ = q_ref.shape
    nh, hd = MEM_HEADS, MEM_HD

    @pl.when(pl.program_id(0) == 0)
    def _():
        r = lax.broadcasted_iota(jnp.int32, bias_s.shape, 0)
        c = lax.broadcasted_iota(jnp.int32, bias_s.shape, 1)
        bias_s[...] = jnp.where(_div(r, pad) == _mod(c, nh), 0.0, NEG_INF)

    for i in range(nseq):
        q = q_ref[i] * (MEM_HD ** -0.5 * LOG2E)
        qrows = jnp.concatenate([q[:, h * hd:(h + 1) * hd] for h in range(nh)], axis=0).astype(BF16)
        s = _dot_nt(qrows, k_ref[i].astype(BF16)) + bias_s[...]
        m = jnp.max(s, axis=-1, keepdims=True)
        p = jnp.exp2(s - m)
        o = _dot(p.astype(BF16), v_ref[i].astype(BF16)) / jnp.sum(p, axis=-1, keepdims=True)
        y = jnp.concatenate([o[h * pad:(h + 1) * pad] for h in range(nh)], axis=1)
        o_ref[i] = y * _silu(g_ref[i])


def _mem_attn_rows(proj_pad, q_off, g_off, mem_k, mem_v, layer):
    b, pad, _ = proj_pad.shape
    w = MEM_W
    rows = MEM_LEN * MEM_HEADS
    n = MEM_ROWS_STEP_SEQS
    assert b % n == 0
    kv = pl.BlockSpec((n, rows, MEM_HD), lambda i: (layer * (b // n) + i, 0, 0))
    return pl.pallas_call(
        _mem_attn_rows_kernel,
        grid=(b // n,),
        in_specs=[pl.BlockSpec((n, pad, w), lambda i: (i, 0, q_off // w)),
                  pl.BlockSpec((n, pad, w), lambda i: (i, 0, g_off // w)),
                  kv, kv],
        out_specs=pl.BlockSpec((n, pad, w), lambda i: (i, 0, 0)),
        out_shape=jax.ShapeDtypeStruct((b, pad, w), F32),
        scratch_shapes=[pltpu.VMEM((MEM_HEADS * pad, rows), F32)],
        compiler_params=_params("arbitrary"),
        name="mem_attn_rows",
    )(proj_pad, proj_pad, mem_k.reshape(-1, rows, MEM_HD), mem_v.reshape(-1, rows, MEM_HD))


def _mem_attn(proj, q_off, g_off, mem_k, k_off, mem_v, v_off, tq, out_dtype):
    b, t, _ = proj.shape
    w = MEM_W
    return pl.pallas_call(
        _mem_attn_kernel,
        grid=(b, t // tq),
        in_specs=[pl.BlockSpec((1, tq, w), lambda i, j: (i, j, q_off // w)),
                  pl.BlockSpec((1, tq, w), lambda i, j: (i, j, g_off // w)),
                  pl.BlockSpec((1, MEM_LEN, w), lambda i, j: (i, 0, k_off // w)),
                  pl.BlockSpec((1, MEM_LEN, w), lambda i, j: (i, 0, v_off // w))],
        out_specs=pl.BlockSpec((1, tq, w), lambda i, j: (i, j, 0)),
        out_shape=jax.ShapeDtypeStruct((b, t, w), out_dtype),
        compiler_params=_params("arbitrary", "arbitrary"),
        name="mem_attn",
    )(proj, proj, mem_k, mem_v)


def _gmlp_kernel(u_ref, v_ref, z_ref, ln_ref, w_ref, b_ref, y_ref, vo_ref, wc_ref):
    c = GMLP_CHUNK

    @pl.when(pl.program_id(0) == 0)
    def _():
        causal = (lax.broadcasted_iota(jnp.int32, (c, c), 0) >= lax.broadcasted_iota(jnp.int32, (c, c), 1))
        for g in range(GMLP_GROUPS):
            wc_ref[g] = jnp.where(causal, w_ref[g], 0.0).astype(BF16)

    nchunks = u_ref.shape[0] // c
    keep_all = vo_ref.shape[0] == u_ref.shape[0]
    for i in range(nchunks):
        rs = slice(i * c, (i + 1) * c)
        v = _gelu(v_ref[rs, :].astype(F32))
        mu = jnp.mean(v, axis=-1, keepdims=True)
        d = v - mu
        var = jnp.mean(d * d, axis=-1, keepdims=True)
        vn = d * lax.rsqrt(var + NORM_EPS) * ln_ref[...]
        if keep_all:
            vo_ref[rs, :] = vn
        elif i == nchunks - 1:
            vo_ref[...] = vn
        for g in range(GMLP_GROUPS):
            gs = slice(g * GMLP_GROUP, (g + 1) * GMLP_GROUP)
            mixed = _dot(wc_ref[g], vn[:, gs].astype(BF16)) + b_ref[:, gs]
            y_ref[rs, gs] = (_gelu(u_ref[rs, gs].astype(F32)) * mixed
                             * _silu(z_ref[rs, gs].astype(F32))).astype(y_ref.dtype)


def _gmlp(proj, gmlp_ln, w_mix, bias_rows, keep_all_v):
    t = proj.shape[0]
    c = GMLP_CHUNK
    w = GMLP_WIDTH
    rows = GMLP_STEP_CHUNKS * c
    assert t % rows == 0
    col = lambda off: pl.BlockSpec((rows, w), lambda n: (n, off // w))
    return pl.pallas_call(
        _gmlp_kernel,
        grid=(t // rows,),
        in_specs=[col(O_U), col(O_V), col(O_Z),
                  pl.BlockSpec((1, w), lambda n: (0, 0)),
                  pl.BlockSpec(w_mix.shape, lambda n: (0, 0, 0)),
                  pl.BlockSpec((c, w), lambda n: (0, 0))],
        out_specs=[pl.BlockSpec((rows, w), lambda n: (n, 0)),
                   pl.BlockSpec((rows, w), lambda n: (n, 0)) if keep_all_v
                   else pl.BlockSpec((c, w), lambda n: (0, 0))],
        out_shape=[jax.ShapeDtypeStruct((t, w), BF16),
                   jax.ShapeDtypeStruct((t if keep_all_v else c, w), F32)],
        scratch_shapes=[pltpu.VMEM(w_mix.shape, BF16)],
        compiler_params=_params("arbitrary"),
        name="gmlp",
    )(proj, proj, proj, gmlp_ln.reshape(1, w), w_mix, bias_rows)


def _out_proj_kernel(*refs, n_in):
    a_refs, w_refs = refs[:n_in], refs[n_in:2 * n_in]
    g_ref, x_ref, o_ref = refs[2 * n_in:]
    sub = min(OUT_PROJ_SUB_ROWS, o_ref.shape[0])
    for r0 in range(0, o_ref.shape[0], sub):
        rs = slice(r0, r0 + sub)
        acc = _dot(a_refs[0][rs, :].astype(BF16), w_refs[0][...])
        for a_ref, w_ref in zip(a_refs[1:], w_refs[1:]):
            acc = acc + _dot(a_ref[rs, :].astype(BF16), w_ref[...])
        ms = jnp.mean(acc * acc, axis=-1, keepdims=True)
        o_ref[rs, :] = x_ref[rs, :] + acc * lax.rsqrt(ms + NORM_EPS) * g_ref[...]


def _out_proj(acts, w, g, x, tm):
    m, d = x.shape
    n_in = len(acts)
    w_specs, off = [], 0
    for a in acts:
        k = a.shape[1]
        assert off % k == 0
        w_specs.append(pl.BlockSpec((k, d), functools.partial(lambda i, blk: (blk, 0), blk=off // k),
                                    pipeline_mode=pl.Buffered(1)))
        off += k
    assert off == w.shape[0]
    return pl.pallas_call(
        functools.partial(_out_proj_kernel, n_in=n_in),
        grid=(m // tm,),
        in_specs=([pl.BlockSpec((tm, a.shape[1]), lambda i: (i, 0)) for a in acts]
                  + w_specs
                  + [pl.BlockSpec((1, d), lambda i: (0, 0)),
                     pl.BlockSpec((tm, d), lambda i: (i, 0))]),
        out_specs=pl.BlockSpec((tm, d), lambda i: (i, 0)),
        out_shape=jax.ShapeDtypeStruct((m, d), F32),
        compiler_params=_params("arbitrary"),
        name="out_proj",
    )(*acts, *([w] * n_in), g.reshape(1, d), x)


def _pad_steps(a, pad):
    nseq, seq, n = a.shape
    return jnp.pad(a, ((0, 0), (0, pad - seq), (0, 0))).reshape(nseq * pad, n)


def kernel(x_prompt, x_sample, state_ret, cache_swa_k, cache_swa_v, cache_mem_k, cache_mem_v, mem_prompt,
           pre_norm, post_norm, mem_norm, w_mem_k, w_mem_v, w_in_even, ret_gn, w_out_even,
           w_in_odd, gmlp_ln, w_spatial, b_spatial, w_out_odd):
    bp, tp, d = x_prompt.shape
    nseq, seq, _ = x_sample.shape
    pad = SAMPLE_PAD
    rows_s = nseq * pad
    c = GMLP_CHUNK
    assert bp == 1 and seq <= pad and rows_s % c == 0 and c % pad == 0
    hp = x_prompt.reshape(tp, d)
    hs = _pad_steps(x_sample, pad)
    mem = mem_prompt.reshape(MEM_LEN, d)
    tm_p = 1024 if tp % 1024 == 0 else 512
    unpad = lambda a: a.reshape(nseq, pad, -1)[:, :seq]

    pos_pad = np.concatenate([PAST_LEN + np.arange(seq), np.zeros((pad - seq,), np.int64)])
    ret_cos_p, ret_sin_p = _rot_tables_range(tp, RET_DK, RET_THETA, RET_DK)
    ret_cos_s, ret_sin_s = _rot_tables(np.tile(pos_pad, nseq), RET_DK, RET_THETA, RET_DK)
    swa_cos_p, swa_sin_p = _rot_tables_range(tp, ROPE_DIMS, ROPE_THETA, SWA_HD)
    swa_cos_s, swa_sin_s = _rot_tables(pos_pad, ROPE_DIMS, ROPE_THETA, SWA_HD)
    swa_cos_s = np.tile(swa_cos_s, (1, SWA_HEADS))
    swa_sin_s = np.tile(swa_sin_s, (1, SWA_HEADS))

    w_mem0 = jnp.concatenate([w_mem_k[0], w_mem_v[0]], axis=1).astype(BF16)
    memkv0 = _norm_matmul(mem, mem_norm[0], w_mem0, MEM_LEN, 2 * MEM_W)

    proj_s, w_in0 = _norm_matmul_cast(hs, pre_norm[0], w_in_even[0], 1024)
    proj_s3 = proj_s.reshape(nseq, pad, EVEN_IN)
    ya_s, st_s = _ret_sample(proj_s, state_ret[0], ret_cos_s, ret_sin_s, ret_gn[0], nseq, seq, pad)
    yb_s, sk = _swa_sample(proj_s3, cache_swa_k[0], cache_swa_v[0], swa_cos_s, swa_sin_s, seq)
    ym_s = _mem_attn_rows(proj_s3, E_MQ, E_MG, cache_mem_k, cache_mem_v, 0)
    sk = sk[:, :seq].reshape(nseq, seq, SWA_HEADS, SWA_HD)
    sv = unpad(proj_s[:, E_BV:E_BV + SWA_HEADS * SWA_HD]).reshape(nseq, seq, SWA_HEADS, SWA_HD)

    buf_p = min(SWA_MAX_WINDOW, tp)
    proj_p, kv_last, w_out0, w_in1, w_out1 = _norm_matmul(
        hp, pre_norm[0], w_in0, tm_p, 1024, BF16, (tp - buf_p, E_BK, E_BG - E_BK),
        casts=[(w_out_even[0], 256), (w_in_odd[0], 1024), (w_out_odd[0], 256)])
    hs1 = _out_proj([ya_s, yb_s.reshape(rows_s, -1), ym_s.reshape(rows_s, -1)], w_out0, post_norm[0], hs, rows_s)
    ya_p, st_p = _ret_prompt(proj_p, ret_cos_p, ret_sin_p, ret_gn[0])
    yb_p, pk = _swa_prompt(proj_p, kv_last, swa_cos_p, swa_sin_p)
    ym_p = _mem_attn(proj_p[None], E_MQ, E_MG, memkv0[None], 0, memkv0[None], MEM_W, 1024, BF16)[0]
    hp1 = _out_proj([ya_p, yb_p, ym_p], w_out0, post_norm[0], hp, 512)
    pv = kv_last[:, E_BV - E_BK:]

    w_mem1 = jnp.concatenate([w_mem_k[1], w_mem_v[1]], axis=1).astype(BF16)
    memkv1 = _norm_matmul(mem, mem_norm[1], w_mem1, MEM_LEN, 2 * MEM_W)
    bias_p = jnp.repeat(b_spatial[0].T, GMLP_GROUP, axis=1)
    blockdiag = lambda wg: jnp.kron(jnp.eye(c // pad, dtype=F32),
                                    jnp.pad(wg[:seq, :seq], ((0, pad - seq), (0, pad - seq))))
    w_mix_s = jax.vmap(blockdiag)(w_spatial[0])
    bias_s = jnp.tile(jnp.pad(bias_p[:seq], ((0, pad - seq), (0, 0))), (c // pad, 1))

    proj_s1 = _norm_matmul(hs1, pre_norm[1], w_in1, rows_s, 1024)
    yc_s, gv_s = _gmlp(proj_s1, gmlp_ln[0], w_mix_s, bias_s, True)
    ym_s1 = _mem_attn_rows(proj_s1.reshape(nseq, pad, ODD_IN), O_MQ, O_MG, cache_mem_k, cache_mem_v, 1)
    hs2 = _out_proj([yc_s, ym_s1.reshape(rows_s, -1)], w_out1, post_norm[1], hs1, rows_s)

    proj_p1 = _norm_matmul(hp1, pre_norm[1], w_in1, tm_p, 1024, BF16)
    yc_p, gv_p = _gmlp(proj_p1, gmlp_ln[0], w_spatial[0], bias_p, False)
    ym_p1 = _mem_attn(proj_p1[None], O_MQ, O_MG, memkv1[None], 0, memkv1[None], MEM_W, 1024, BF16)[0]
    hp2 = _out_proj([yc_p, ym_p1], w_out1, post_norm[1], hp1, 512)

    memkv = jnp.stack([memkv0, memkv1])
    p_mk = memkv[:, :, :MEM_W].reshape(2, bp, MEM_LEN, MEM_HEADS, MEM_HD)
    p_mv = memkv[:, :, MEM_W:].reshape(2, bp, MEM_LEN, MEM_HEADS, MEM_HD)
    return (hp2.reshape(bp, tp, d), unpad(hs2),
            st_p[None, None],
            pk.reshape(1, bp, buf_p, SWA_HEADS, SWA_HD), pv.reshape(1, bp, buf_p, SWA_HEADS, SWA_HD),
            p_mk, p_mv,
            gv_p[None, None],
            st_s[None],
            sk[None], sv[None],
            unpad(gv_s)[None])
```

```python
import functools
import math

import jax
import jax.numpy as jnp
import numpy as np
from jax import lax
from jax.experimental import pallas as pl
from jax.experimental.pallas import tpu as pltpu

F32 = jnp.float32
BF16 = jnp.bfloat16

D_MODEL = 2048
PAST_LEN = 8192
RET_HEADS = 8
RET_DK = 128
RET_DV = 256
RET_CHUNK = 128
RET_THETA = 10000.0
SWA_HEADS = 8
SWA_HD = 128
SWA_PATTERNS = ((128, 1), (512, 4), (2048, 16))
SWA_MAX_WINDOW = 2048
SWA_BLOCK = 128
ROPE_THETA = 500000.0
ROPE_DIMS = SWA_HD // 4
GMLP_CHUNK = 128
GMLP_WIDTH = D_MODEL
GMLP_GROUP = 128
GMLP_GROUPS = GMLP_WIDTH // GMLP_GROUP
MEM_LEN = 256
MEM_HEADS = 4
MEM_HD = 128
MEM_W = MEM_HEADS * MEM_HD
NORM_EPS = 1e-6
NEG_INF = -1e30
LOG2E = math.log2(math.e)

E_AQ = 0
E_AK = E_AQ + RET_HEADS * RET_DK
E_AV = E_AK + RET_HEADS * RET_DK
E_AG = E_AV + RET_HEADS * RET_DV
E_BQ = E_AG + RET_HEADS * RET_DV
E_BK = E_BQ + SWA_HEADS * SWA_HD
E_BV = E_BK + SWA_HEADS * SWA_HD
E_BG = E_BV + SWA_HEADS * SWA_HD
E_MQ = E_BG + SWA_HEADS * SWA_HD
E_MG = E_MQ + MEM_W
EVEN_IN = E_MG + MEM_W
O_U = 0
O_V = O_U + GMLP_WIDTH
O_Z = O_V + GMLP_WIDTH
O_MQ = O_Z + GMLP_WIDTH
O_MG = O_MQ + MEM_W
ODD_IN = O_MG + MEM_W

SWA_TILE = SWA_MAX_WINDOW
SAMPLE_PAD = 8
GMLP_STEP_CHUNKS = 2
SWA_GROUP = 1
RET_PROMPT_CHUNK = 256
RET_STEP_CHUNKS = 1
MEM_ROWS_STEP_SEQS = 2
RET_STEP_SEQS = 2
OUT_PROJ_SUB_ROWS = 256
NORM_SUB_ROWS = 256
VMEM_LIMIT = 56 * 1024 * 1024


def _params(*sem):
    return pltpu.CompilerParams(dimension_semantics=sem, vmem_limit_bytes=VMEM_LIMIT)


def _dot(a, b):
    return jnp.dot(a, b, preferred_element_type=F32)


def _dot_nt(a, b):
    return lax.dot_general(a, b, (((1,), (1,)), ((), ())), preferred_element_type=F32)


def _silu(x):
    return x / (1.0 + jnp.exp2(x * (-LOG2E)))


def _gelu(x):
    a = -2.0 * math.sqrt(2.0 / math.pi) * LOG2E
    return x / (1.0 + jnp.exp2(x * (a + (a * 0.044715) * (x * x))))


def _div(x, n):
    assert n & (n - 1) == 0
    return jnp.right_shift(x, n.bit_length() - 1)


def _mod(x, n):
    assert n & (n - 1) == 0
    return jnp.bitwise_and(x, n - 1)


def _rot_lanes(n_rot, theta, width):
    half = n_rot // 2
    inv = 1.0 / (theta ** (np.arange(half, dtype=np.float64) / half))
    rest = np.zeros(width - n_rot)
    return np.concatenate([inv, inv, rest]), np.concatenate([-np.ones(half), np.ones(half), rest])


def _rot_tables(pos, n_rot, theta, width):
    f, sgn = _rot_lanes(n_rot, theta, width)
    ang = np.asarray(pos, np.float64)[:, None] * f[None, :]
    return np.cos(ang).astype(np.float32), (sgn * np.sin(ang)).astype(np.float32)


def _rot_tables_range(n, n_rot, theta, width, blk=128):
    assert n % blk == 0
    f, sgn = _rot_lanes(n_rot, theta, width)
    hi = (np.arange(n // blk, dtype=np.float64) * blk)[:, None] * f[None, :]
    lo = np.arange(blk, dtype=np.float64)[:, None] * f[None, :]
    dev = lambda a: jnp.asarray(a, F32)
    ch, sh, cl, sl = dev(np.cos(hi)[:, None]), dev(np.sin(hi)[:, None]), dev(np.cos(lo)[None]), dev(np.sin(lo)[None])
    chs, shs = dev((sgn * np.cos(hi))[:, None]), dev((sgn * np.sin(hi))[:, None])
    return (ch * cl - sh * sl).reshape(n, width), (shs * cl + chs * sl).reshape(n, width)


def _rot_full(x, cos, sin):
    return x * cos + pltpu.roll(x, RET_DK // 2, 1) * sin


def _rot_partial(x, cos, sin):
    half = ROPE_DIMS // 2
    lane = _mod(lax.broadcasted_iota(jnp.int32, x.shape, 1), SWA_HD)
    width = x.shape[1]
    partner = jnp.where(lane < half, pltpu.roll(x, width - half, 1), pltpu.roll(x, half, 1))
    return x * cos + partner * sin


def _norm_matmul_kernel(x_ref, g_ref, w_ref, *rest, side, cast_cols, rot):
    if rot is not None:
        cos_ref, sin_ref, rest = rest[0], rest[1], rest[2:]
    nc = len(cast_cols)
    cast_in, o_ref, rest = rest[:nc], rest[nc], rest[nc + 1:]
    cast_out, xn_ref = rest[len(rest) - 1 - nc:len(rest) - 1], rest[-1]
    tm = x_ref.shape[0]
    sub = min(NORM_SUB_ROWS, tm)

    for src, dst, ncol in zip(cast_in, cast_out, cast_cols):
        @pl.when(pl.program_id(1) < ncol)
        def _(src=src, dst=dst):
            dst[...] = src[...].astype(BF16)

    @pl.when(pl.program_id(1) == 0)
    def _():
        for r0 in range(0, tm, sub):
            rs = slice(r0, r0 + sub)
            x = x_ref[rs, :]
            ms = jnp.mean(x * x, axis=-1, keepdims=True)
            xn = (x * lax.rsqrt(ms + NORM_EPS) * g_ref[...]).astype(BF16)
            xn_ref[rs, :] = xn
            o_ref[rs, :] = _dot(xn, w_ref[...]).astype(o_ref.dtype)

    def tail(epilogue):
        acc = epilogue(_dot(xn_ref[...], w_ref[...]))
        o_ref[...] = acc.astype(o_ref.dtype)
        if side is not None:
            i0, j0, nj = side
            i, j = pl.program_id(0), pl.program_id(1)

            @pl.when((i >= i0) & (j >= j0) & (j < j0 + nj))
            def _():
                rest[0][...] = acc

    j = pl.program_id(1)
    if rot is None:
        pl.when(j != 0)(lambda: tail(lambda acc: acc))
    else:
        jq, jk, q_scale = rot
        nrep = o_ref.shape[1] // cos_ref.shape[1]
        rope = lambda acc: _rot_partial(acc, jnp.tile(cos_ref[...], (1, nrep)), jnp.tile(sin_ref[...], (1, nrep)))
        pl.when(j == jq)(lambda: tail(lambda acc: rope(acc * q_scale)))
        pl.when(j == jk)(lambda: tail(rope))
        pl.when((j != 0) & (j != jq) & (j != jk))(lambda: tail(lambda acc: acc))


def _norm_matmul(x, g, w, tm, tn, out_dtype=F32, side=None, casts=(), rot=None):
    m, d = x.shape
    n = w.shape[1]
    ni, nj_grid = m // tm, n // tn
    out_specs = [pl.BlockSpec((tm, tn), lambda i, j: (i, j))]
    out_shape = [jax.ShapeDtypeStruct((m, n), out_dtype)]
    tiles = None
    if side is not None:
        row0, col0, ncols = side
        assert row0 % tm == 0 and col0 % tn == 0 and ncols % tn == 0 and col0 >= tn
        i0, j0, nj = row0 // tm, col0 // tn, ncols // tn
        tiles = (i0, j0, nj)
        out_specs.append(pl.BlockSpec(
            (tm, tn), lambda i, j: (jnp.maximum(i - i0, 0), jnp.where(i >= i0, jnp.clip(j - j0, 0, nj - 1), 0))))
        out_shape.append(jax.ShapeDtypeStruct((m - row0, ncols), F32))
    rot_tiles, rot_specs, rot_args = None, [], []
    if rot is not None:
        cos, sin, q_col, k_col, q_scale = rot
        assert q_col % tn == 0 and k_col % tn == 0 and q_col >= tn and k_col >= tn and tn % cos.shape[1] == 0
        rot_tiles = (q_col // tn, k_col // tn, q_scale)
        rot_specs = [pl.BlockSpec((tm, cos.shape[1]), lambda i, j: (i, 0))] * 2
        rot_args = [cos, sin]
    cast_specs, cast_cols = [], []
    for a, bc in casts:
        r, c = a.shape
        assert r % ni == 0 and (r // ni) % 16 == 0 and c % bc == 0 and c // bc <= nj_grid
        ncol = c // bc
        spec = pl.BlockSpec((r // ni, bc), functools.partial(lambda i, j, last: (i, jnp.minimum(j, last)), last=ncol - 1))
        cast_specs.append(spec)
        cast_cols.append(ncol)
        out_specs.append(spec)
        out_shape.append(jax.ShapeDtypeStruct((r, c), BF16))
    res = pl.pallas_call(
        functools.partial(_norm_matmul_kernel, side=tiles, cast_cols=tuple(cast_cols), rot=rot_tiles),
        grid=(ni, nj_grid),
        in_specs=[pl.BlockSpec((tm, d), lambda i, j: (i, 0)),
                  pl.BlockSpec((1, d), lambda i, j: (0, 0)),
                  pl.BlockSpec((d, tn), lambda i, j: (0, j))] + rot_specs + cast_specs,
        out_specs=out_specs,
        out_shape=out_shape,
        scratch_shapes=[pltpu.VMEM((tm, d), BF16)],
        compiler_params=_params("arbitrary", "arbitrary"),
        name="norm_matmul",
    )(x, g.reshape(1, d), w, *rot_args, *[a for a, _ in casts])
    return res if len(res) > 1 else res[0]


def _norm_matmul_cast_kernel(x_ref, g_ref, w_ref, o_ref, wb_ref, xn_ref):
    @pl.when(pl.program_id(0) == 0)
    def _():
        x = x_ref[...]
        ms = jnp.mean(x * x, axis=-1, keepdims=True)
        xn_ref[...] = (x * lax.rsqrt(ms + NORM_EPS) * g_ref[...]).astype(BF16)

    wb = w_ref[...].astype(BF16)
    wb_ref[...] = wb
    o_ref[...] = _dot(xn_ref[...], wb).astype(o_ref.dtype)


def _norm_matmul_cast(x, g, w, tn):
    m, d = x.shape
    n = w.shape[1]
    return pl.pallas_call(
        _norm_matmul_cast_kernel,
        grid=(n // tn,),
        in_specs=[pl.BlockSpec((m, d), lambda j: (0, 0)),
                  pl.BlockSpec((1, d), lambda j: (0, 0)),
                  pl.BlockSpec((d, tn), lambda j: (0, j))],
        out_specs=[pl.BlockSpec((m, tn), lambda j: (0, j)),
                   pl.BlockSpec((d, tn), lambda j: (0, j))],
        out_shape=[jax.ShapeDtypeStruct((m, n), F32),
                   jax.ShapeDtypeStruct((d, n), BF16)],
        scratch_shapes=[pltpu.VMEM((m, d), BF16)],
        compiler_params=_params("arbitrary"),
        name="norm_matmul_cast",
    )(x, g.reshape(1, d), w)


def _ret_tables(c, reps, pad=None):
    pad = c if pad is None else pad
    rows = pad * reps
    lg = np.log1p(-np.exp2(-5.0 - np.arange(RET_HEADS, dtype=np.float64)))
    row = np.arange(rows)
    step = (row % pad).astype(np.float64)
    live = (row % pad) < c
    rel = step[:, None] - step[None, :]
    keep = ((row // pad)[:, None] == (row // pad)[None, :]) & (rel >= 0) & live[:, None] & live[None, :]
    decay = np.where(keep[None], np.exp(lg[:, None, None] * np.maximum(rel, 0.0)[None]), 0.0)
    q_decay = np.where(live[None], np.exp(lg[:, None] * (step[None, :] + 1.0)), 0.0)
    k_decay = np.where(live[None], np.exp(lg[:, None] * (c - 1.0 - step[None, :])), 0.0)
    chunk_decay = np.exp(lg * c)
    f32 = lambda a, shape: np.ascontiguousarray(np.broadcast_to(a, shape), dtype=np.float32)
    qd = f32(q_decay[:, :, None], (RET_HEADS, rows, RET_DK))
    kd = f32(k_decay[:, :, None], (RET_HEADS, rows, RET_DK))
    cd = f32(chunk_decay[:, None, None], (RET_HEADS, 1, RET_DV))
    return decay.astype(np.float32), qd, kd, cd


def _group_norm_gate(o, gn, g):
    mu = jnp.mean(o, axis=-1, keepdims=True)
    d = o - mu
    var = jnp.mean(d * d, axis=-1, keepdims=True)
    return d * lax.rsqrt(var + NORM_EPS) * gn * _silu(g.astype(F32))


def _ret_prompt_kernel(q_ref, k_ref, v_ref, g_ref, cos_ref, sin_ref, dec_ref, qd_ref, kd_ref, cd_ref, gn_ref,
                       y_ref, st_ref):
    @pl.when(pl.program_id(0) == 0)
    def _():
        st_ref[...] = jnp.zeros_like(st_ref)

    c = dec_ref.shape[1]
    for i in range(q_ref.shape[0] // c):
        rs = slice(i * c, (i + 1) * c)
        cos = cos_ref[rs, :]
        sin = sin_ref[rs, :]
        for h in range(RET_HEADS):
            ks = slice(h * RET_DK, (h + 1) * RET_DK)
            vs = slice(h * RET_DV, (h + 1) * RET_DV)
            q = _rot_full(q_ref[rs, ks].astype(F32), cos, sin)
            k = _rot_full(k_ref[rs, ks].astype(F32) * (RET_DK ** -0.5), cos, sin)
            v = v_ref[rs, vs].astype(BF16)
            st = st_ref[h]
            s = _dot_nt(q.astype(BF16), k.astype(BF16)) * dec_ref[h]
            lhs = jnp.concatenate([s.astype(BF16), (q * qd_ref[h]).astype(BF16)], axis=1)
            o = _dot(lhs, jnp.concatenate([v, st.astype(BF16)], axis=0))
            st_ref[h] = cd_ref[h] * st + _dot((k * kd_ref[h]).T.astype(BF16), v)
            y_ref[rs, vs] = _group_norm_gate(o, gn_ref[:, vs], g_ref[rs, vs]).astype(y_ref.dtype)


def _ret_prompt(proj, cos, sin, ret_gn):
    t = proj.shape[0]
    dec, qd, kd, cd = _ret_tables(RET_PROMPT_CHUNK, 1)
    c = RET_STEP_CHUNKS * RET_PROMPT_CHUNK
    assert t % c == 0
    qw = RET_HEADS * RET_DK
    vw = RET_HEADS * RET_DV
    full3 = lambda n: (0, 0, 0)
    return pl.pallas_call(
        _ret_prompt_kernel,
        grid=(t // c,),
        in_specs=[pl.BlockSpec((c, qw), lambda n: (n, E_AQ // qw)),
                  pl.BlockSpec((c, qw), lambda n: (n, E_AK // qw)),
                  pl.BlockSpec((c, vw), lambda n: (n, E_AV // vw)),
                  pl.BlockSpec((c, vw), lambda n: (n, E_AG // vw)),
                  pl.BlockSpec((c, RET_DK), lambda n: (n, 0)),
                  pl.BlockSpec((c, RET_DK), lambda n: (n, 0)),
                  pl.BlockSpec(dec.shape, full3),
                  pl.BlockSpec(qd.shape, full3),
                  pl.BlockSpec(kd.shape, full3),
                  pl.BlockSpec(cd.shape, full3),
                  pl.BlockSpec((1, vw), lambda n: (0, 0))],
        out_specs=[pl.BlockSpec((c, vw), lambda n: (n, 0)),
                   pl.BlockSpec((RET_HEADS, RET_DK, RET_DV), full3)],
        out_shape=[jax.ShapeDtypeStruct((t, vw), BF16),
                   jax.ShapeDtypeStruct((RET_HEADS, RET_DK, RET_DV), F32)],
        compiler_params=_params("arbitrary"),
        name="ret_prompt",
    )(proj, proj, proj, proj, cos, sin, dec, qd, kd, cd, ret_gn.reshape(1, vw))


def _ret_sample_kernel(q_ref, k_ref, v_ref, g_ref, cos_ref, sin_ref, dec_ref, qd_ref, kd_ref, cd_ref, gn_ref,
                       st_ref, y_ref, sto_ref, cross_ref, qs_ref, kt_ref, *, pad):
    b = pl.program_id(0)
    nb = pl.num_programs(0)
    win = RET_CHUNK

    @pl.when(b == 0)
    def _():
        cross_ref[...] = jnp.zeros_like(cross_ref)
        cos = cos_ref[...]
        sin = sin_ref[...]
        for h in range(RET_HEADS):
            ks = slice(h * RET_DK, (h + 1) * RET_DK)
            qs_ref[:, ks] = _rot_full(q_ref[:, ks], cos, sin) * qd_ref[h]
            kt_ref[h] = (_rot_full(k_ref[:, ks] * (RET_DK ** -0.5), cos, sin) * kd_ref[h]).T

    for i in range(st_ref.shape[0]):
        sq = b * st_ref.shape[0] + i
        w0 = pl.multiple_of(jnp.right_shift(sq * pad, win.bit_length() - 1) * win, win)
        wrows = pl.ds(w0, win)
        rows = lax.broadcasted_iota(jnp.int32, (win, RET_DK), 0) + w0
        cols = lax.broadcasted_iota(jnp.int32, (RET_DK, win), 1) + w0
        mine_r = (rows >= sq * pad) & (rows < (sq + 1) * pad)
        mine_c = (cols >= sq * pad) & (cols < (sq + 1) * pad)
        for h in range(RET_HEADS):
            ks = slice(h * RET_DK, (h + 1) * RET_DK)
            vs = slice(h * RET_DV, (h + 1) * RET_DV)
            st = st_ref[i, h]
            qm = jnp.where(mine_r, qs_ref[wrows, ks], 0.0)
            kmt = jnp.where(mine_c, kt_ref[h, :, wrows], 0.0)
            cross_ref[wrows, vs] += _dot(qm.astype(BF16), st.astype(BF16))
            sto_ref[i, h] = cd_ref[h] * st + _dot(kmt.astype(BF16), v_ref[wrows, vs].astype(BF16))

    @pl.when(b == nb - 1)
    def _():
        cos_all = cos_ref[...]
        sin_all = sin_ref[...]
        for h in range(RET_HEADS):
            ks = slice(h * RET_DK, (h + 1) * RET_DK)
            vs = slice(h * RET_DV, (h + 1) * RET_DV)
            q = _rot_full(q_ref[:, ks], cos_all, sin_all)
            k = _rot_full(k_ref[:, ks] * (RET_DK ** -0.5), cos_all, sin_all)
            v = v_ref[:, vs].astype(BF16)
            s = _dot_nt(q.astype(BF16), k.astype(BF16)) * dec_ref[h]
            o = _dot(s.astype(BF16), v) + cross_ref[:, vs]
            y_ref[:, vs] = _group_norm_gate(o, gn_ref[:, vs], g_ref[:, vs]).astype(y_ref.dtype)


def _ret_sample(proj, state, cos, sin, ret_gn, nseq, seq, pad):
    rows = nseq * pad
    c = math.gcd(seq, RET_CHUNK)
    assert c == seq, "sample chunk must cover the new tokens"
    assert rows % RET_CHUNK == 0 and RET_CHUNK % pad == 0
    dec, qd, kd, cd = _ret_tables(c, nseq, pad)
    qw = RET_HEADS * RET_DK
    vw = RET_HEADS * RET_DV
    full3 = lambda b: (0, 0, 0)
    n = RET_STEP_SEQS
    assert nseq % n == 0
    st_spec = pl.BlockSpec((n, RET_HEADS, RET_DK, RET_DV), lambda b: (b, 0, 0, 0))
    return pl.pallas_call(
        functools.partial(_ret_sample_kernel, pad=pad),
        grid=(nseq // n,),
        in_specs=[pl.BlockSpec((rows, qw), lambda b: (0, E_AQ // qw)),
                  pl.BlockSpec((rows, qw), lambda b: (0, E_AK // qw)),
                  pl.BlockSpec((rows, vw), lambda b: (0, E_AV // vw)),
                  pl.BlockSpec((rows, vw), lambda b: (0, E_AG // vw)),
                  pl.BlockSpec((rows, RET_DK), lambda b: (0, 0)),
                  pl.BlockSpec((rows, RET_DK), lambda b: (0, 0)),
                  pl.BlockSpec(dec.shape, full3),
                  pl.BlockSpec(qd.shape, full3),
                  pl.BlockSpec(kd.shape, full3),
                  pl.BlockSpec(cd.shape, full3),
                  pl.BlockSpec((1, vw), lambda b: (0, 0)),
                  st_spec],
        out_specs=[pl.BlockSpec((rows, vw), lambda b: (0, 0)), st_spec],
        out_shape=[jax.ShapeDtypeStruct((rows, vw), F32),
                   jax.ShapeDtypeStruct(state.shape, F32)],
        scratch_shapes=[pltpu.VMEM((rows, vw), F32),
                        pltpu.VMEM((rows, qw), F32),
                        pltpu.VMEM((RET_HEADS, RET_DK, rows), F32)],
        compiler_params=_params("arbitrary"),
        name="ret_sample",
    )(proj, proj, proj, proj, cos, sin, dec, qd, kd, cd, ret_gn.reshape(1, vw), state)


def _swa_prompt_kernel(q_ref, kc_ref, vc_ref, g_ref, y_ref, q_s, k_s, v_s, m_s, l_s, acc_s):
    t = pl.program_id(1)
    tb = SWA_TILE
    blk = SWA_BLOCK
    hd = SWA_HD

    @pl.when(t == 0)
    def _():
        k_s[0:tb, :] = jnp.zeros((tb, hd), F32)
        v_s[0:tb, :] = jnp.zeros((tb, hd), F32)

    @pl.when(t > 0)
    def _():
        k_s[0:tb, :] = k_s[tb:2 * tb, :]
        v_s[0:tb, :] = v_s[tb:2 * tb, :]

    q_s[...] = q_ref[...].astype(F32)
    k_s[tb:2 * tb, :] = kc_ref[...].astype(F32)
    v_s[tb:2 * tb, :] = vc_ref[...].astype(F32)

    def ds(start, size, stride):
        return pl.ds(start, size) if stride == 1 else pl.ds(start, size, stride=stride)

    def band(g):
        qi = lax.broadcasted_iota(jnp.int32, (g * blk, (g + 1) * blk), 0)
        kj = lax.broadcasted_iota(jnp.int32, (g * blk, (g + 1) * blk), 1)
        ok = (kj >= qi) & (kj <= qi + blk)
        return ok, ok & ((kj >= blk) | (t > 0))

    for bi, (window, r) in enumerate(SWA_PATTERNS):
        assert window // r == blk
        per_class = tb // (blk * r)
        g = SWA_GROUP if per_class % SWA_GROUP == 0 else 1
        ok, ok_first = band(g)
        ones = jnp.ones(((g + 1) * blk, hd), BF16)
        for c in range(r):
            for n in range(0, per_class, g):
                qs = n * blk * r + c
                ks = qs + tb - blk * r
                qrows = ds(qs, g * blk, r)
                krows = ds(ks, (g + 1) * blk, r)
                s = _dot_nt(q_s[qrows, :].astype(BF16), k_s[krows, :].astype(BF16))
                s = jnp.where(ok_first if n == 0 else ok, s, NEG_INF)
                m = jnp.max(s, axis=-1, keepdims=True)
                p = jnp.exp2(s - m).astype(BF16)
                pv = _dot(p, jnp.concatenate([v_s[krows, :].astype(BF16), ones], axis=1))
                m_s[bi, qrows, :] = jnp.broadcast_to(m, (g * blk, hd))
                l_s[bi, qrows, :] = pv[:, hd:]
                acc_s[bi, qrows, :] = pv[:, :hd]

    nbr = len(SWA_PATTERNS)
    ms = [m_s[i] for i in range(nbr)]
    mmax = functools.reduce(jnp.maximum, ms)
    wts = [jnp.exp2(mi - mmax) for mi in ms]
    num = sum(wts[i] * acc_s[i] for i in range(nbr))
    den = sum(wts[i] * l_s[i] for i in range(nbr))
    y_ref[...] = (num / den * _silu(g_ref[...].astype(F32))).astype(y_ref.dtype)


def _swa_prompt(proj):
    t = proj.shape[0]
    tb = SWA_TILE
    assert t % tb == 0
    hd = SWA_HD
    nbr = len(SWA_PATTERNS)
    cur = lambda off: pl.BlockSpec((tb, hd), lambda h, i: (i, off // hd + h))
    return pl.pallas_call(
        _swa_prompt_kernel,
        grid=(SWA_HEADS, t // tb),
        in_specs=[cur(E_BQ), cur(E_BK), cur(E_BV), cur(E_BG)],
        out_specs=pl.BlockSpec((tb, hd), lambda h, i: (i, h)),
        out_shape=jax.ShapeDtypeStruct((t, SWA_HEADS * hd), BF16),
        scratch_shapes=[pltpu.VMEM((tb, hd), F32),
                        pltpu.VMEM((2 * tb, hd), F32),
                        pltpu.VMEM((2 * tb, hd), F32),
                        pltpu.VMEM((nbr, tb, hd), F32),
                        pltpu.VMEM((nbr, tb, hd), F32),
                        pltpu.VMEM((nbr, tb, hd), F32)],
        compiler_params=_params("arbitrary", "arbitrary"),
        name="swa_prompt",
    )(proj, proj, proj, proj)


def _swa_sample_kernel(q_ref, k_ref, v_ref, g_ref, cos_ref, sin_ref, ka_ref, kb_ref, va_ref, vb_ref,
                       y_ref, ko_ref, kn_s, vn_s, ba_s, bb_s, bn_s, *, seq):
    pad = SAMPLE_PAD
    nh = SWA_HEADS
    hd = SWA_HD
    nr = nh * pad
    nj, nrh, _ = kb_ref.shape[1:]
    na = ka_ref.shape[1]
    nb = nj * nrh
    nn = kn_s.shape[0]

    @pl.when(pl.program_id(0) == 0)
    def _():
        (w1, d1), (w2, d2), (w3, d3) = SWA_PATTERNS

        def rows_cols(n):
            r = lax.broadcasted_iota(jnp.int32, (nr, n), 0)
            c = lax.broadcasted_iota(jnp.int32, (nr, n), 1)
            return _div(r, pad), _mod(r, pad), c

        def log2_mult(mult):
            return jnp.where(mult > 0, jnp.log2(jnp.maximum(mult, 1.0)), NEG_INF)

        hq, tq, c = rows_cols(na)
        dist = na // nh + tq - _div(c, nh)
        mult_a = ((dist <= w1).astype(F32) + ((_mod(dist, d2) == 0) & (dist <= w2)).astype(F32)
                  + ((_mod(dist, d3) == 0) & (dist <= w3)).astype(F32))
        ba_s[...] = log2_mult(jnp.where((hq == _mod(c, nh)) & (tq < seq), mult_a, 0.0))
        hq, tq, c = rows_cols(nb)
        bb_s[...] = log2_mult(jnp.where((hq == _mod(c, nh)) & (tq == _mod(_div(c, nh), nrh // nh)), 1.0, 0.0))
        hq, tq, c = rows_cols(nn)
        tk = _mod(c, pad)
        mult_n = (tk <= tq).astype(F32) + 2.0 * (tk == tq).astype(F32)
        bn_s[...] = log2_mult(jnp.where((hq == _div(c, pad)) & (tq < seq) & (tk < seq), mult_n, 0.0))
        kn_s[...] = jnp.zeros_like(kn_s)
        vn_s[...] = jnp.zeros_like(vn_s)

    cos = cos_ref[...]
    sin = sin_ref[...]
    q8 = _rot_partial(q_ref[0] * (SWA_HD ** -0.5 * LOG2E), cos, sin)
    k8 = _rot_partial(k_ref[0], cos, sin)
    ko_ref[0] = k8

    def head_rows(x):
        return jnp.concatenate([x[:, h * hd:(h + 1) * hd] for h in range(nh)], axis=0)

    qrows = head_rows(q8).astype(BF16)
    kn_s[0:nr, :] = head_rows(k8).astype(BF16)
    vn_s[0:nr, :] = head_rows(v_ref[0]).astype(BF16)

    kb = kb_ref[0].reshape(nb, hd)
    vb = vb_ref[0].reshape(nb, hd)
    s_a = _dot_nt(qrows, ka_ref[0].astype(BF16)) + ba_s[...]
    s_b = _dot_nt(qrows, kb.astype(BF16)) + bb_s[...]
    s_n = _dot_nt(qrows, kn_s[...]) + bn_s[...]
    rowmax = lambda s: jnp.max(s, axis=-1, keepdims=True)
    m = jnp.maximum(jnp.maximum(rowmax(s_a), rowmax(s_b)), rowmax(s_n))
    p_a, p_b, p_n = jnp.exp2(s_a - m), jnp.exp2(s_b - m), jnp.exp2(s_n - m)
    rowsum = lambda p: jnp.sum(p, axis=-1, keepdims=True)
    o = (_dot(p_a.astype(BF16), va_ref[0].astype(BF16)) + _dot(p_b.astype(BF16), vb.astype(BF16))
         + _dot(p_n.astype(BF16), vn_s[...])) / (rowsum(p_a) + rowsum(p_b) + rowsum(p_n))
    y = jnp.concatenate([o[h * pad:(h + 1) * pad] for h in range(nh)], axis=1)
    y_ref[0] = y * _silu(g_ref[0])


def _swa_sample(proj_pad, cache_k, cache_v, cos, sin, seq):
    nseq, pad, _ = proj_pad.shape
    buf = cache_k.shape[1]
    nh, hd = SWA_HEADS, SWA_HD
    w = nh * hd
    (w1, d1), (w2, d2), (w3, d3) = SWA_PATTERNS
    assert buf == w3 and d1 == 1 and seq <= d2 and w1 <= w2 and buf % w2 == 0 and buf % d3 == 0
    assert nh * pad <= SWA_BLOCK and (seq * nh) % 8 == 0
    flat = lambda a: a.reshape(nseq, buf * nh, hd)
    strided = lambda a: a.reshape(nseq, buf // d3, d3 * nh, hd)
    col = lambda off: pl.BlockSpec((1, pad, w), lambda b: (b, 0, off // w))
    tail = pl.BlockSpec((1, w2 * nh, hd), lambda b: (b, buf // w2 - 1, 0))
    resid = pl.BlockSpec((1, (buf - w2) // d3, seq * nh, hd), lambda b: (b, 0, 0, 0))
    tab = pl.BlockSpec((pad, w), lambda b: (0, 0))
    out = pl.BlockSpec((1, pad, w), lambda b: (b, 0, 0))
    return pl.pallas_call(
        functools.partial(_swa_sample_kernel, seq=seq),
        grid=(nseq,),
        in_specs=[col(E_BQ), col(E_BK), col(E_BV), col(E_BG), tab, tab, tail, resid, tail, resid],
        out_specs=[out, out],
        out_shape=[jax.ShapeDtypeStruct((nseq, pad, w), F32),
                   jax.ShapeDtypeStruct((nseq, pad, w), F32)],
        scratch_shapes=[pltpu.VMEM((SWA_BLOCK, hd), BF16), pltpu.VMEM((SWA_BLOCK, hd), BF16),
                        pltpu.VMEM((nh * pad, w2 * nh), F32),
                        pltpu.VMEM((nh * pad, (buf - w2) // d3 * seq * nh), F32),
                        pltpu.VMEM((nh * pad, SWA_BLOCK), F32)],
        compiler_params=_params("arbitrary"),
        name="swa_sample",
    )(proj_pad, proj_pad, proj_pad, proj_pad, cos, sin,
      flat(cache_k), strided(cache_k), flat(cache_v), strided(cache_v))


def _mem_attn_kernel(q_ref, g_ref, k_ref, v_ref, o_ref):
    ones = jnp.ones((MEM_LEN, MEM_HD), BF16)
    for h in range(MEM_HEADS):
        hs = slice(h * MEM_HD, (h + 1) * MEM_HD)
        q = (q_ref[0, :, hs].astype(F32) * (MEM_HD ** -0.5 * LOG2E)).astype(BF16)
        s = _dot_nt(q, k_ref[0, :, hs].astype(BF16))
        m = jnp.max(s, axis=-1, keepdims=True)
        p = jnp.exp2(s - m).astype(BF16)
        pv = _dot(p, jnp.concatenate([v_ref[0, :, hs].astype(BF16), ones], axis=1))
        o = pv[:, :MEM_HD] / pv[:, MEM_HD:]
        o_ref[0, :, hs] = (o * _silu(g_ref[0, :, hs].astype(F32))).astype(o_ref.dtype)


def _mem_attn_rows_kernel(q_ref, g_ref, k_ref, v_ref, o_ref, bias_s):
    nseq, pad, _ = q_ref.shape
    nh, hd = MEM_HEADS, MEM_HD

    @pl.when(pl.program_id(0) == 0)
    def _():
        r = lax.broadcasted_iota(jnp.int32, bias_s.shape, 0)
        c = lax.broadcasted_iota(jnp.int32, bias_s.shape, 1)
        bias_s[...] = jnp.where(_div(r, pad) == _mod(c, nh), 0.0, NEG_INF)

    for i in range(nseq):
        q = q_ref[i] * (MEM_HD ** -0.5 * LOG2E)
        qrows = jnp.concatenate([q[:, h * hd:(h + 1) * hd] for h in range(nh)], axis=0).astype(BF16)
        s = _dot_nt(qrows, k_ref[i].astype(BF16)) + bias_s[...]
        m = jnp.max(s, axis=-1, keepdims=True)
        p = jnp.exp2(s - m)
        o = _dot(p.astype(BF16), v_ref[i].astype(BF16)) / jnp.sum(p, axis=-1, keepdims=True)
        y = jnp.concatenate([o[h * pad:(h + 1) * pad] for h in range(nh)], axis=1)
        o_ref[i] = y * _silu(g_ref[i])


def _mem_attn_rows(proj_pad, q_off, g_off, mem_k, mem_v, layer):
    b, pad, _ = proj_pad.shape
    w = MEM_W
    rows = MEM_LEN * MEM_HEADS
    n = MEM_ROWS_STEP_SEQS
    assert b % n == 0
    kv = pl.BlockSpec((n, rows, MEM_HD), lambda i: (layer * (b // n) + i, 0, 0))
    return pl.pallas_call(
        _mem_attn_rows_kernel,
        grid=(b // n,),
        in_specs=[pl.BlockSpec((n, pad, w), lambda i: (i, 0, q_off // w)),
                  pl.BlockSpec((n, pad, w), lambda i: (i, 0, g_off // w)),
                  kv, kv],
        out_specs=pl.BlockSpec((n, pad, w), lambda i: (i, 0, 0)),
        out_shape=jax.ShapeDtypeStruct((b, pad, w), F32),
        scratch_shapes=[pltpu.VMEM((MEM_HEADS * pad, rows), F32)],
        compiler_params=_params("arbitrary"),
        name="mem_attn_rows",
    )(proj_pad, proj_pad, mem_k.reshape(-1, rows, MEM_HD), mem_v.reshape(-1, rows, MEM_HD))


def _mem_attn(proj, q_off, g_off, mem_k, k_off, mem_v, v_off, tq, out_dtype):
    b, t, _ = proj.shape
    w = MEM_W
    return pl.pallas_call(
        _mem_attn_kernel,
        grid=(b, t // tq),
        in_specs=[pl.BlockSpec((1, tq, w), lambda i, j: (i, j, q_off // w)),
                  pl.BlockSpec((1, tq, w), lambda i, j: (i, j, g_off // w)),
                  pl.BlockSpec((1, MEM_LEN, w), lambda i, j: (i, 0, k_off // w)),
                  pl.BlockSpec((1, MEM_LEN, w), lambda i, j: (i, 0, v_off // w))],
        out_specs=pl.BlockSpec((1, tq, w), lambda i, j: (i, j, 0)),
        out_shape=jax.ShapeDtypeStruct((b, t, w), out_dtype),
        compiler_params=_params("arbitrary", "arbitrary"),
        name="mem_attn",
    )(proj, proj, mem_k, mem_v)


def _gmlp_kernel(u_ref, v_ref, z_ref, ln_ref, w_ref, b_ref, y_ref, vo_ref, wc_ref):
    c = GMLP_CHUNK

    @pl.when(pl.program_id(0) == 0)
    def _():
        causal = (lax.broadcasted_iota(jnp.int32, (c, c), 0) >= lax.broadcasted_iota(jnp.int32, (c, c), 1))
        for g in range(GMLP_GROUPS):
            wc_ref[g] = jnp.where(causal, w_ref[g], 0.0).astype(BF16)

    nchunks = u_ref.shape[0] // c
    keep_all = vo_ref.shape[0] == u_ref.shape[0]
    for i in range(nchunks):
        rs = slice(i * c, (i + 1) * c)
        v = _gelu(v_ref[rs, :].astype(F32))
        mu = jnp.mean(v, axis=-1, keepdims=True)
        d = v - mu
        var = jnp.mean(d * d, axis=-1, keepdims=True)
        vn = d * lax.rsqrt(var + NORM_EPS) * ln_ref[...]
        if keep_all:
            vo_ref[rs, :] = vn
        elif i == nchunks - 1:
            vo_ref[...] = vn
        for g in range(GMLP_GROUPS):
            gs = slice(g * GMLP_GROUP, (g + 1) * GMLP_GROUP)
            mixed = _dot(wc_ref[g], vn[:, gs].astype(BF16)) + b_ref[:, gs]
            y_ref[rs, gs] = (_gelu(u_ref[rs, gs].astype(F32)) * mixed
                             * _silu(z_ref[rs, gs].astype(F32))).astype(y_ref.dtype)


def _gmlp(proj, gmlp_ln, w_mix, bias_rows, keep_all_v):
    t = proj.shape[0]
    c = GMLP_CHUNK
    w = GMLP_WIDTH
    rows = GMLP_STEP_CHUNKS * c
    assert t % rows == 0
    col = lambda off: pl.BlockSpec((rows, w), lambda n: (n, off // w))
    return pl.pallas_call(
        _gmlp_kernel,
        grid=(t // rows,),
        in_specs=[col(O_U), col(O_V), col(O_Z),
                  pl.BlockSpec((1, w), lambda n: (0, 0)),
                  pl.BlockSpec(w_mix.shape, lambda n: (0, 0, 0)),
                  pl.BlockSpec((c, w), lambda n: (0, 0))],
        out_specs=[pl.BlockSpec((rows, w), lambda n: (n, 0)),
                   pl.BlockSpec((rows, w), lambda n: (n, 0)) if keep_all_v
                   else pl.BlockSpec((c, w), lambda n: (0, 0))],
        out_shape=[jax.ShapeDtypeStruct((t, w), BF16),
                   jax.ShapeDtypeStruct((t if keep_all_v else c, w), F32)],
        scratch_shapes=[pltpu.VMEM(w_mix.shape, BF16)],
        compiler_params=_params("arbitrary"),
        name="gmlp",
    )(proj, proj, proj, gmlp_ln.reshape(1, w), w_mix, bias_rows)


def _out_proj_kernel(*refs, n_in):
    a_refs, w_refs = refs[:n_in], refs[n_in:2 * n_in]
    g_ref, x_ref, o_ref = refs[2 * n_in:]
    sub = min(OUT_PROJ_SUB_ROWS, o_ref.shape[0])
    for r0 in range(0, o_ref.shape[0], sub):
        rs = slice(r0, r0 + sub)
        acc = _dot(a_refs[0][rs, :].astype(BF16), w_refs[0][...])
        for a_ref, w_ref in zip(a_refs[1:], w_refs[1:]):
            acc = acc + _dot(a_ref[rs, :].astype(BF16), w_ref[...])
        ms = jnp.mean(acc * acc, axis=-1, keepdims=True)
        o_ref[rs, :] = x_ref[rs, :] + acc * lax.rsqrt(ms + NORM_EPS) * g_ref[...]


def _out_proj(acts, w, g, x, tm):
    m, d = x.shape
    n_in = len(acts)
    w_specs, off = [], 0
    for a in acts:
        k = a.shape[1]
        assert off % k == 0
        w_specs.append(pl.BlockSpec((k, d), functools.partial(lambda i, blk: (blk, 0), blk=off // k),
                                    pipeline_mode=pl.Buffered(1)))
        off += k
    assert off == w.shape[0]
    return pl.pallas_call(
        functools.partial(_out_proj_kernel, n_in=n_in),
        grid=(m // tm,),
        in_specs=([pl.BlockSpec((tm, a.shape[1]), lambda i: (i, 0)) for a in acts]
                  + w_specs
                  + [pl.BlockSpec((1, d), lambda i: (0, 0)),
                     pl.BlockSpec((tm, d), lambda i: (i, 0))]),
        out_specs=pl.BlockSpec((tm, d), lambda i: (i, 0)),
        out_shape=jax.ShapeDtypeStruct((m, d), F32),
        compiler_params=_params("arbitrary"),
        name="out_proj",
    )(*acts, *([w] * n_in), g.reshape(1, d), x)


def _pad_steps(a, pad):
    nseq, seq, n = a.shape
    return jnp.pad(a, ((0, 0), (0, pad - seq), (0, 0))).reshape(nseq * pad, n)


def kernel(x_prompt, x_sample, state_ret, cache_swa_k, cache_swa_v, cache_mem_k, cache_mem_v, mem_prompt,
           pre_norm, post_norm, mem_norm, w_mem_k, w_mem_v, w_in_even, ret_gn, w_out_even,
           w_in_odd, gmlp_ln, w_spatial, b_spatial, w_out_odd):
    bp, tp, d = x_prompt.shape
    nseq, seq, _ = x_sample.shape
    pad = SAMPLE_PAD
    rows_s = nseq * pad
    c = GMLP_CHUNK
    assert bp == 1 and seq <= pad and rows_s % c == 0 and c % pad == 0
    hp = x_prompt.reshape(tp, d)
    hs = _pad_steps(x_sample, pad)
    mem = mem_prompt.reshape(MEM_LEN, d)
    tm_p = 1024 if tp % 1024 == 0 else 512
    unpad = lambda a: a.reshape(nseq, pad, -1)[:, :seq]

    pos_pad = np.concatenate([PAST_LEN + np.arange(seq), np.zeros((pad - seq,), np.int64)])
    ret_cos_p, ret_sin_p = _rot_tables_range(tp, RET_DK, RET_THETA, RET_DK)
    ret_cos_s, ret_sin_s = _rot_tables(np.tile(pos_pad, nseq), RET_DK, RET_THETA, RET_DK)
    swa_cos_p, swa_sin_p = _rot_tables_range(tp, ROPE_DIMS, ROPE_THETA, SWA_HD)
    swa_cos_s, swa_sin_s = _rot_tables(pos_pad, ROPE_DIMS, ROPE_THETA, SWA_HD)
    swa_cos_s = np.tile(swa_cos_s, (1, SWA_HEADS))
    swa_sin_s = np.tile(swa_sin_s, (1, SWA_HEADS))

    w_mem0 = jnp.concatenate([w_mem_k[0], w_mem_v[0]], axis=1).astype(BF16)
    memkv0 = _norm_matmul(mem, mem_norm[0], w_mem0, MEM_LEN, 2 * MEM_W)

    proj_s, w_in0 = _norm_matmul_cast(hs, pre_norm[0], w_in_even[0], 1024)
    proj_s3 = proj_s.reshape(nseq, pad, EVEN_IN)
    ya_s, st_s = _ret_sample(proj_s, state_ret[0], ret_cos_s, ret_sin_s, ret_gn[0], nseq, seq, pad)
    yb_s, sk = _swa_sample(proj_s3, cache_swa_k[0], cache_swa_v[0], swa_cos_s, swa_sin_s, seq)
    ym_s = _mem_attn_rows(proj_s3, E_MQ, E_MG, cache_mem_k, cache_mem_v, 0)
    sk = sk[:, :seq].reshape(nseq, seq, SWA_HEADS, SWA_HD)
    sv = unpad(proj_s[:, E_BV:E_BV + SWA_HEADS * SWA_HD]).reshape(nseq, seq, SWA_HEADS, SWA_HD)

    buf_p = min(SWA_MAX_WINDOW, tp)
    proj_p, kv_last, w_out0, w_in1, w_out1 = _norm_matmul(
        hp, pre_norm[0], w_in0, tm_p, 1024, BF16, (tp - buf_p, E_BK, E_BG - E_BK),
        casts=[(w_out_even[0], 256), (w_in_odd[0], 1024), (w_out_odd[0], 256)],
        rot=(swa_cos_p, swa_sin_p, E_BQ, E_BK, SWA_HD ** -0.5 * LOG2E))
    hs1 = _out_proj([ya_s, yb_s.reshape(rows_s, -1), ym_s.reshape(rows_s, -1)], w_out0, post_norm[0], hs, rows_s)
    ya_p, st_p = _ret_prompt(proj_p, ret_cos_p, ret_sin_p, ret_gn[0])
    yb_p = _swa_prompt(proj_p)
    pk = kv_last[:, :E_BV - E_BK]
    ym_p = _mem_attn(proj_p[None], E_MQ, E_MG, memkv0[None], 0, memkv0[None], MEM_W, 1024, BF16)[0]
    hp1 = _out_proj([ya_p, yb_p, ym_p], w_out0, post_norm[0], hp, 512)
    pv = kv_last[:, E_BV - E_BK:]

    w_mem1 = jnp.concatenate([w_mem_k[1], w_mem_v[1]], axis=1).astype(BF16)
    memkv1 = _norm_matmul(mem, mem_norm[1], w_mem1, MEM_LEN, 2 * MEM_W)
    bias_p = jnp.repeat(b_spatial[0].T, GMLP_GROUP, axis=1)
    blockdiag = lambda wg: jnp.kron(jnp.eye(c // pad, dtype=F32),
                                    jnp.pad(wg[:seq, :seq], ((0, pad - seq), (0, pad - seq))))
    w_mix_s = jax.vmap(blockdiag)(w_spatial[0])
    bias_s = jnp.tile(jnp.pad(bias_p[:seq], ((0, pad - seq), (0, 0))), (c // pad, 1))

    proj_s1 = _norm_matmul(hs1, pre_norm[1], w_in1, rows_s, 1024)
    yc_s, gv_s = _gmlp(proj_s1, gmlp_ln[0], w_mix_s, bias_s, True)
    ym_s1 = _mem_attn_rows(proj_s1.reshape(nseq, pad, ODD_IN), O_MQ, O_MG, cache_mem_k, cache_mem_v, 1)
    hs2 = _out_proj([yc_s, ym_s1.reshape(rows_s, -1)], w_out1, post_norm[1], hs1, rows_s)

    proj_p1 = _norm_matmul(hp1, pre_norm[1], w_in1, tm_p, 1024, BF16)
    yc_p, gv_p = _gmlp(proj_p1, gmlp_ln[0], w_spatial[0], bias_p, False)
    ym_p1 = _mem_attn(proj_p1[None], O_MQ, O_MG, memkv1[None], 0, memkv1[None], MEM_W, 1024, BF16)[0]
    hp2 = _out_proj([yc_p, ym_p1], w_out1, post_norm[1], hp1, 512)

    memkv = jnp.stack([memkv0, memkv1])
    p_mk = memkv[:, :, :MEM_W].reshape(2, bp, MEM_LEN, MEM_HEADS, MEM_HD)
    p_mv = memkv[:, :, MEM_W:].reshape(2, bp, MEM_LEN, MEM_HEADS, MEM_HD)
    return (hp2.reshape(bp, tp, d), unpad(hs2),
            st_p[None, None],
            pk.reshape(1, bp, buf_p, SWA_HEADS, SWA_HD), pv.reshape(1, bp, buf_p, SWA_HEADS, SWA_HD),
            p_mk, p_mv,
            gv_p[None, None],
            st_s[None],
            sk[None], sv[None],
            unpad(gv_s)[None])
```

```python
import functools
import math

import jax
import jax.numpy as jnp
import numpy as np
from jax import lax
from jax.experimental import pallas as pl
from jax.experimental.pallas import tpu as pltpu

F32 = jnp.float32
BF16 = jnp.bfloat16

D_MODEL = 2048
PAST_LEN = 8192
RET_HEADS = 8
RET_DK = 128
RET_DV = 256
RET_CHUNK = 128
RET_THETA = 10000.0
SWA_HEADS = 8
SWA_HD = 128
SWA_PATTERNS = ((128, 1), (512, 4), (2048, 16))
SWA_MAX_WINDOW = 2048
SWA_BLOCK = 128
ROPE_THETA = 500000.0
ROPE_DIMS = SWA_HD // 4
GMLP_CHUNK = 128
GMLP_WIDTH = D_MODEL
GMLP_GROUP = 128
GMLP_GROUPS = GMLP_WIDTH // GMLP_GROUP
MEM_LEN = 256
MEM_HEADS = 4
MEM_HD = 128
MEM_W = MEM_HEADS * MEM_HD
NORM_EPS = 1e-6
NEG_INF = -1e30
LOG2E = math.log2(math.e)

E_AQ = 0
E_AK = E_AQ + RET_HEADS * RET_DK
E_AV = E_AK + RET_HEADS * RET_DK
E_AG = E_AV + RET_HEADS * RET_DV
E_BQ = E_AG + RET_HEADS * RET_DV
E_BK = E_BQ + SWA_HEADS * SWA_HD
E_BV = E_BK + SWA_HEADS * SWA_HD
E_BG = E_BV + SWA_HEADS * SWA_HD
E_MQ = E_BG + SWA_HEADS * SWA_HD
E_MG = E_MQ + MEM_W
EVEN_IN = E_MG + MEM_W
O_U = 0
O_V = O_U + GMLP_WIDTH
O_Z = O_V + GMLP_WIDTH
O_MQ = O_Z + GMLP_WIDTH
O_MG = O_MQ + MEM_W
ODD_IN = O_MG + MEM_W

SWA_TILE = SWA_MAX_WINDOW
SAMPLE_PAD = 8
GMLP_STEP_CHUNKS = 2
SWA_GROUP = 1
RET_PROMPT_CHUNK = 256
RET_STEP_CHUNKS = 1
MEM_ROWS_STEP_SEQS = 2
RET_STEP_SEQS = 2
OUT_PROJ_SUB_ROWS = 256
NORM_SUB_ROWS = 256
VMEM_LIMIT = 56 * 1024 * 1024


def _params(*sem):
    return pltpu.CompilerParams(dimension_semantics=sem, vmem_limit_bytes=VMEM_LIMIT)


def _dot(a, b):
    return jnp.dot(a, b, preferred_element_type=F32)


def _dot_nt(a, b):
    return lax.dot_general(a, b, (((1,), (1,)), ((), ())), preferred_element_type=F32)


def _silu(x):
    return x / (1.0 + jnp.exp2(x * (-LOG2E)))


def _gelu(x):
    a = -2.0 * math.sqrt(2.0 / math.pi) * LOG2E
    return x / (1.0 + jnp.exp2(x * (a + (a * 0.044715) * (x * x))))


def _div(x, n):
    assert n & (n - 1) == 0
    return jnp.right_shift(x, n.bit_length() - 1)


def _mod(x, n):
    assert n & (n - 1) == 0
    return jnp.bitwise_and(x, n - 1)


def _rot_lanes(n_rot, theta, width):
    half = n_rot // 2
    inv = 1.0 / (theta ** (np.arange(half, dtype=np.float64) / half))
    rest = np.zeros(width - n_rot)
    return np.concatenate([inv, inv, rest]), np.concatenate([-np.ones(half), np.ones(half), rest])


def _rot_tables(pos, n_rot, theta, width):
    f, sgn = _rot_lanes(n_rot, theta, width)
    ang = np.asarray(pos, np.float64)[:, None] * f[None, :]
    return np.cos(ang).astype(np.float32), (sgn * np.sin(ang)).astype(np.float32)


def _rot_tables_range(n, n_rot, theta, width, blk=128):
    assert n % blk == 0
    f, sgn = _rot_lanes(n_rot, theta, width)
    hi = (np.arange(n // blk, dtype=np.float64) * blk)[:, None] * f[None, :]
    lo = np.arange(blk, dtype=np.float64)[:, None] * f[None, :]
    dev = lambda a: jnp.asarray(a, F32)
    ch, sh, cl, sl = dev(np.cos(hi)[:, None]), dev(np.sin(hi)[:, None]), dev(np.cos(lo)[None]), dev(np.sin(lo)[None])
    chs, shs = dev((sgn * np.cos(hi))[:, None]), dev((sgn * np.sin(hi))[:, None])
    return (ch * cl - sh * sl).reshape(n, width), (shs * cl + chs * sl).reshape(n, width)


def _rot_full(x, cos, sin):
    return x * cos + pltpu.roll(x, RET_DK // 2, 1) * sin


def _rot_partial(x, cos, sin):
    half = ROPE_DIMS // 2
    lane = _mod(lax.broadcasted_iota(jnp.int32, x.shape, 1), SWA_HD)
    width = x.shape[1]
    partner = jnp.where(lane < half, pltpu.roll(x, width - half, 1), pltpu.roll(x, half, 1))
    return x * cos + partner * sin


def _norm_matmul_kernel(x_ref, g_ref, w_ref, *rest, side, cast_cols, rot, acts, ntiles):
    if rot is not None:
        cos_ref, sin_ref, rest = rest[0], rest[1], rest[2:]
    nc = len(cast_cols)
    cast_in, o_ref, rest = rest[:nc], rest[nc], rest[nc + 1:]
    cast_out, xn_ref = rest[len(rest) - 1 - nc:len(rest) - 1], rest[-1]
    tm = x_ref.shape[0]
    sub = min(NORM_SUB_ROWS, tm)

    for src, dst, ncol in zip(cast_in, cast_out, cast_cols):
        @pl.when(pl.program_id(1) < ncol)
        def _(src=src, dst=dst):
            dst[...] = src[...].astype(BF16)

    kind = {}
    if rot is not None:
        jq, jk, q_scale = rot
        kind[jq], kind[jk] = "rope_q", "rope_k"
    for lo, hi, name in acts:
        kind.update({t: name for t in range(lo, hi)})

    def epilogue(name):
        if name in ("rope_q", "rope_k"):
            nrep = o_ref.shape[1] // cos_ref.shape[1]
            cos, sin = jnp.tile(cos_ref[...], (1, nrep)), jnp.tile(sin_ref[...], (1, nrep))
            return (lambda acc: _rot_partial(acc * q_scale, cos, sin)) if name == "rope_q" else (
                lambda acc: _rot_partial(acc, cos, sin))
        return {"gelu": _gelu, "silu": _silu, None: lambda acc: acc}[name]

    j = pl.program_id(1)

    @pl.when(j == 0)
    def _():
        first = epilogue(kind.get(0))
        for r0 in range(0, tm, sub):
            rs = slice(r0, r0 + sub)
            x = x_ref[rs, :]
            ms = jnp.mean(x * x, axis=-1, keepdims=True)
            xn = (x * lax.rsqrt(ms + NORM_EPS) * g_ref[...]).astype(BF16)
            xn_ref[rs, :] = xn
            o_ref[rs, :] = first(_dot(xn, w_ref[...])).astype(o_ref.dtype)

    def tail(name):
        acc = epilogue(name)(_dot(xn_ref[...], w_ref[...]))
        o_ref[...] = acc.astype(o_ref.dtype)
        if side is not None:
            i0, j0, nj = side
            i = pl.program_id(0)

            @pl.when((i >= i0) & (j >= j0) & (j < j0 + nj))
            def _():
                rest[0][...] = acc

    for name in sorted({kind.get(t) for t in range(1, ntiles)}, key=str):
        tiles = [t for t in range(1, ntiles) if kind.get(t) == name]
        pl.when(functools.reduce(jnp.logical_or, [j == t for t in tiles]))(functools.partial(tail, name))


def _norm_matmul(x, g, w, tm, tn, out_dtype=F32, side=None, casts=(), rot=None, acts=()):
    m, d = x.shape
    n = w.shape[1]
    ni, nj_grid = m // tm, n // tn
    out_specs = [pl.BlockSpec((tm, tn), lambda i, j: (i, j))]
    out_shape = [jax.ShapeDtypeStruct((m, n), out_dtype)]
    tiles = None
    if side is not None:
        row0, col0, ncols = side
        assert row0 % tm == 0 and col0 % tn == 0 and ncols % tn == 0 and col0 >= tn
        i0, j0, nj = row0 // tm, col0 // tn, ncols // tn
        tiles = (i0, j0, nj)
        out_specs.append(pl.BlockSpec(
            (tm, tn), lambda i, j: (jnp.maximum(i - i0, 0), jnp.where(i >= i0, jnp.clip(j - j0, 0, nj - 1), 0))))
        out_shape.append(jax.ShapeDtypeStruct((m - row0, ncols), F32))
    rot_tiles, rot_specs, rot_args = None, [], []
    if rot is not None:
        cos, sin, q_col, k_col, q_scale = rot
        assert q_col % tn == 0 and k_col % tn == 0 and q_col >= tn and k_col >= tn and tn % cos.shape[1] == 0
        rot_tiles = (q_col // tn, k_col // tn, q_scale)
        rot_specs = [pl.BlockSpec((tm, cos.shape[1]), lambda i, j: (i, 0))] * 2
        rot_args = [cos, sin]
    assert all(lo % tn == 0 and hi % tn == 0 for lo, hi, _ in acts)
    act_tiles = tuple((lo // tn, hi // tn, name) for lo, hi, name in acts)
    cast_specs, cast_cols = [], []
    for a, bc in casts:
        r, c = a.shape
        assert r % ni == 0 and (r // ni) % 16 == 0 and c % bc == 0 and c // bc <= nj_grid
        ncol = c // bc
        spec = pl.BlockSpec((r // ni, bc), functools.partial(lambda i, j, last: (i, jnp.minimum(j, last)), last=ncol - 1))
        cast_specs.append(spec)
        cast_cols.append(ncol)
        out_specs.append(spec)
        out_shape.append(jax.ShapeDtypeStruct((r, c), BF16))
    res = pl.pallas_call(
        functools.partial(_norm_matmul_kernel, side=tiles, cast_cols=tuple(cast_cols), rot=rot_tiles,
                          acts=act_tiles, ntiles=nj_grid),
        grid=(ni, nj_grid),
        in_specs=[pl.BlockSpec((tm, d), lambda i, j: (i, 0)),
                  pl.BlockSpec((1, d), lambda i, j: (0, 0)),
                  pl.BlockSpec((d, tn), lambda i, j: (0, j))] + rot_specs + cast_specs,
        out_specs=out_specs,
        out_shape=out_shape,
        scratch_shapes=[pltpu.VMEM((tm, d), BF16)],
        compiler_params=_params("arbitrary", "arbitrary"),
        name="norm_matmul",
    )(x, g.reshape(1, d), w, *rot_args, *[a for a, _ in casts])
    return res if len(res) > 1 else res[0]


def _norm_matmul_cast_kernel(x_ref, g_ref, w_ref, o_ref, wb_ref, xn_ref):
    @pl.when(pl.program_id(0) == 0)
    def _():
        x = x_ref[...]
        ms = jnp.mean(x * x, axis=-1, keepdims=True)
        xn_ref[...] = (x * lax.rsqrt(ms + NORM_EPS) * g_ref[...]).astype(BF16)

    wb = w_ref[...].astype(BF16)
    wb_ref[...] = wb
    o_ref[...] = _dot(xn_ref[...], wb).astype(o_ref.dtype)


def _norm_matmul_cast(x, g, w, tn):
    m, d = x.shape
    n = w.shape[1]
    return pl.pallas_call(
        _norm_matmul_cast_kernel,
        grid=(n // tn,),
        in_specs=[pl.BlockSpec((m, d), lambda j: (0, 0)),
                  pl.BlockSpec((1, d), lambda j: (0, 0)),
                  pl.BlockSpec((d, tn), lambda j: (0, j))],
        out_specs=[pl.BlockSpec((m, tn), lambda j: (0, j)),
                   pl.BlockSpec((d, tn), lambda j: (0, j))],
        out_shape=[jax.ShapeDtypeStruct((m, n), F32),
                   jax.ShapeDtypeStruct((d, n), BF16)],
        scratch_shapes=[pltpu.VMEM((m, d), BF16)],
        compiler_params=_params("arbitrary"),
        name="norm_matmul_cast",
    )(x, g.reshape(1, d), w)


def _ret_tables(c, reps, pad=None):
    pad = c if pad is None else pad
    rows = pad * reps
    lg = np.log1p(-np.exp2(-5.0 - np.arange(RET_HEADS, dtype=np.float64)))
    row = np.arange(rows)
    step = (row % pad).astype(np.float64)
    live = (row % pad) < c
    rel = step[:, None] - step[None, :]
    keep = ((row // pad)[:, None] == (row // pad)[None, :]) & (rel >= 0) & live[:, None] & live[None, :]
    decay = np.where(keep[None], np.exp(lg[:, None, None] * np.maximum(rel, 0.0)[None]), 0.0)
    q_decay = np.where(live[None], np.exp(lg[:, None] * (step[None, :] + 1.0)), 0.0)
    k_decay = np.where(live[None], np.exp(lg[:, None] * (c - 1.0 - step[None, :])), 0.0)
    chunk_decay = np.exp(lg * c)
    f32 = lambda a, shape: np.ascontiguousarray(np.broadcast_to(a, shape), dtype=np.float32)
    qd = f32(q_decay[:, :, None], (RET_HEADS, rows, RET_DK))
    kd = f32(k_decay[:, :, None], (RET_HEADS, rows, RET_DK))
    cd = f32(chunk_decay[:, None, None], (RET_HEADS, 1, RET_DV))
    return decay.astype(np.float32), qd, kd, cd


def _group_norm_gate(o, gn, g):
    mu = jnp.mean(o, axis=-1, keepdims=True)
    d = o - mu
    var = jnp.mean(d * d, axis=-1, keepdims=True)
    return d * lax.rsqrt(var + NORM_EPS) * gn * _silu(g.astype(F32))


def _ret_prompt_kernel(q_ref, k_ref, v_ref, g_ref, cos_ref, sin_ref, dec_ref, qd_ref, kd_ref, cd_ref, gn_ref,
                       y_ref, st_ref):
    @pl.when(pl.program_id(0) == 0)
    def _():
        st_ref[...] = jnp.zeros_like(st_ref)

    c = dec_ref.shape[1]
    for i in range(q_ref.shape[0] // c):
        rs = slice(i * c, (i + 1) * c)
        cos = cos_ref[rs, :]
        sin = sin_ref[rs, :]
        for h in range(RET_HEADS):
            ks = slice(h * RET_DK, (h + 1) * RET_DK)
            vs = slice(h * RET_DV, (h + 1) * RET_DV)
            q = _rot_full(q_ref[rs, ks].astype(F32), cos, sin)
            k = _rot_full(k_ref[rs, ks].astype(F32) * (RET_DK ** -0.5), cos, sin)
            v = v_ref[rs, vs].astype(BF16)
            st = st_ref[h]
            s = _dot_nt(q.astype(BF16), k.astype(BF16)) * dec_ref[h]
            lhs = jnp.concatenate([s.astype(BF16), (q * qd_ref[h]).astype(BF16)], axis=1)
            o = _dot(lhs, jnp.concatenate([v, st.astype(BF16)], axis=0))
            st_ref[h] = cd_ref[h] * st + _dot((k * kd_ref[h]).T.astype(BF16), v)
            y_ref[rs, vs] = _group_norm_gate(o, gn_ref[:, vs], g_ref[rs, vs]).astype(y_ref.dtype)


def _ret_prompt(proj, cos, sin, ret_gn):
    t = proj.shape[0]
    dec, qd, kd, cd = _ret_tables(RET_PROMPT_CHUNK, 1)
    c = RET_STEP_CHUNKS * RET_PROMPT_CHUNK
    assert t % c == 0
    qw = RET_HEADS * RET_DK
    vw = RET_HEADS * RET_DV
    full3 = lambda n: (0, 0, 0)
    return pl.pallas_call(
        _ret_prompt_kernel,
        grid=(t // c,),
        in_specs=[pl.BlockSpec((c, qw), lambda n: (n, E_AQ // qw)),
                  pl.BlockSpec((c, qw), lambda n: (n, E_AK // qw)),
                  pl.BlockSpec((c, vw), lambda n: (n, E_AV // vw)),
                  pl.BlockSpec((c, vw), lambda n: (n, E_AG // vw)),
                  pl.BlockSpec((c, RET_DK), lambda n: (n, 0)),
                  pl.BlockSpec((c, RET_DK), lambda n: (n, 0)),
                  pl.BlockSpec(dec.shape, full3),
                  pl.BlockSpec(qd.shape, full3),
                  pl.BlockSpec(kd.shape, full3),
                  pl.BlockSpec(cd.shape, full3),
                  pl.BlockSpec((1, vw), lambda n: (0, 0))],
        out_specs=[pl.BlockSpec((c, vw), lambda n: (n, 0)),
                   pl.BlockSpec((RET_HEADS, RET_DK, RET_DV), full3)],
        out_shape=[jax.ShapeDtypeStruct((t, vw), BF16),
                   jax.ShapeDtypeStruct((RET_HEADS, RET_DK, RET_DV), F32)],
        compiler_params=_params("arbitrary"),
        name="ret_prompt",
    )(proj, proj, proj, proj, cos, sin, dec, qd, kd, cd, ret_gn.reshape(1, vw))


def _ret_sample_kernel(q_ref, k_ref, v_ref, g_ref, cos_ref, sin_ref, dec_ref, qd_ref, kd_ref, cd_ref, gn_ref,
                       st_ref, y_ref, sto_ref, cross_ref, qs_ref, kt_ref, *, pad):
    b = pl.program_id(0)
    nb = pl.num_programs(0)
    win = RET_CHUNK

    @pl.when(b == 0)
    def _():
        cross_ref[...] = jnp.zeros_like(cross_ref)
        cos = cos_ref[...]
        sin = sin_ref[...]
        for h in range(RET_HEADS):
            ks = slice(h * RET_DK, (h + 1) * RET_DK)
            qs_ref[:, ks] = _rot_full(q_ref[:, ks], cos, sin) * qd_ref[h]
            kt_ref[h] = (_rot_full(k_ref[:, ks] * (RET_DK ** -0.5), cos, sin) * kd_ref[h]).T

    for i in range(st_ref.shape[0]):
        sq = b * st_ref.shape[0] + i
        w0 = pl.multiple_of(jnp.right_shift(sq * pad, win.bit_length() - 1) * win, win)
        wrows = pl.ds(w0, win)
        rows = lax.broadcasted_iota(jnp.int32, (win, RET_DK), 0) + w0
        cols = lax.broadcasted_iota(jnp.int32, (RET_DK, win), 1) + w0
        mine_r = (rows >= sq * pad) & (rows < (sq + 1) * pad)
        mine_c = (cols >= sq * pad) & (cols < (sq + 1) * pad)
        for h in range(RET_HEADS):
            ks = slice(h * RET_DK, (h + 1) * RET_DK)
            vs = slice(h * RET_DV, (h + 1) * RET_DV)
            st = st_ref[i, h]
            qm = jnp.where(mine_r, qs_ref[wrows, ks], 0.0)
            kmt = jnp.where(mine_c, kt_ref[h, :, wrows], 0.0)
            cross_ref[wrows, vs] += _dot(qm.astype(BF16), st.astype(BF16))
            sto_ref[i, h] = cd_ref[h] * st + _dot(kmt.astype(BF16), v_ref[wrows, vs].astype(BF16))

    @pl.when(b == nb - 1)
    def _():
        cos_all = cos_ref[...]
        sin_all = sin_ref[...]
        for h in range(RET_HEADS):
            ks = slice(h * RET_DK, (h + 1) * RET_DK)
            vs = slice(h * RET_DV, (h + 1) * RET_DV)
            q = _rot_full(q_ref[:, ks], cos_all, sin_all)
            k = _rot_full(k_ref[:, ks] * (RET_DK ** -0.5), cos_all, sin_all)
            v = v_ref[:, vs].astype(BF16)
            s = _dot_nt(q.astype(BF16), k.astype(BF16)) * dec_ref[h]
            o = _dot(s.astype(BF16), v) + cross_ref[:, vs]
            y_ref[:, vs] = _group_norm_gate(o, gn_ref[:, vs], g_ref[:, vs]).astype(y_ref.dtype)


def _ret_sample(proj, state, cos, sin, ret_gn, nseq, seq, pad):
    rows = nseq * pad
    c = math.gcd(seq, RET_CHUNK)
    assert c == seq, "sample chunk must cover the new tokens"
    assert rows % RET_CHUNK == 0 and RET_CHUNK % pad == 0
    dec, qd, kd, cd = _ret_tables(c, nseq, pad)
    qw = RET_HEADS * RET_DK
    vw = RET_HEADS * RET_DV
    full3 = lambda b: (0, 0, 0)
    n = RET_STEP_SEQS
    assert nseq % n == 0
    st_spec = pl.BlockSpec((n, RET_HEADS, RET_DK, RET_DV), lambda b: (b, 0, 0, 0))
    return pl.pallas_call(
        functools.partial(_ret_sample_kernel, pad=pad),
        grid=(nseq // n,),
        in_specs=[pl.BlockSpec((rows, qw), lambda b: (0, E_AQ // qw)),
                  pl.BlockSpec((rows, qw), lambda b: (0, E_AK // qw)),
                  pl.BlockSpec((rows, vw), lambda b: (0, E_AV // vw)),
                  pl.BlockSpec((rows, vw), lambda b: (0, E_AG // vw)),
                  pl.BlockSpec((rows, RET_DK), lambda b: (0, 0)),
                  pl.BlockSpec((rows, RET_DK), lambda b: (0, 0)),
                  pl.BlockSpec(dec.shape, full3),
                  pl.BlockSpec(qd.shape, full3),
                  pl.BlockSpec(kd.shape, full3),
                  pl.BlockSpec(cd.shape, full3),
                  pl.BlockSpec((1, vw), lambda b: (0, 0)),
                  st_spec],
        out_specs=[pl.BlockSpec((rows, vw), lambda b: (0, 0)), st_spec],
        out_shape=[jax.ShapeDtypeStruct((rows, vw), F32),
                   jax.ShapeDtypeStruct(state.shape, F32)],
        scratch_shapes=[pltpu.VMEM((rows, vw), F32),
                        pltpu.VMEM((rows, qw), F32),
                        pltpu.VMEM((RET_HEADS, RET_DK, rows), F32)],
        compiler_params=_params("arbitrary"),
        name="ret_sample",
    )(proj, proj, proj, proj, cos, sin, dec, qd, kd, cd, ret_gn.reshape(1, vw), state)


def _swa_prompt_kernel(q_ref, kc_ref, vc_ref, g_ref, y_ref, q_s, k_s, v_s, m_s, l_s, acc_s):
    t = pl.program_id(1)
    tb = SWA_TILE
    blk = SWA_BLOCK
    hd = SWA_HD

    @pl.when(t == 0)
    def _():
        k_s[0:tb, :] = jnp.zeros((tb, hd), F32)
        v_s[0:tb, :] = jnp.zeros((tb, hd), F32)

    @pl.when(t > 0)
    def _():
        k_s[0:tb, :] = k_s[tb:2 * tb, :]
        v_s[0:tb, :] = v_s[tb:2 * tb, :]

    q_s[...] = q_ref[...].astype(F32)
    k_s[tb:2 * tb, :] = kc_ref[...].astype(F32)
    v_s[tb:2 * tb, :] = vc_ref[...].astype(F32)

    def ds(start, size, stride):
        return pl.ds(start, size) if stride == 1 else pl.ds(start, size, stride=stride)

    def band(g):
        qi = lax.broadcasted_iota(jnp.int32, (g * blk, (g + 1) * blk), 0)
        kj = lax.broadcasted_iota(jnp.int32, (g * blk, (g + 1) * blk), 1)
        ok = (kj >= qi) & (kj <= qi + blk)
        return ok, ok & ((kj >= blk) | (t > 0))

    for bi, (window, r) in enumerate(SWA_PATTERNS):
        assert window // r == blk
        per_class = tb // (blk * r)
        g = SWA_GROUP if per_class % SWA_GROUP == 0 else 1
        ok, ok_first = band(g)
        ones = jnp.ones(((g + 1) * blk, hd), BF16)
        for c in range(r):
            for n in range(0, per_class, g):
                qs = n * blk * r + c
                ks = qs + tb - blk * r
                qrows = ds(qs, g * blk, r)
                krows = ds(ks, (g + 1) * blk, r)
                s = _dot_nt(q_s[qrows, :].astype(BF16), k_s[krows, :].astype(BF16))
                s = jnp.where(ok_first if n == 0 else ok, s, NEG_INF)
                m = jnp.max(s, axis=-1, keepdims=True)
                p = jnp.exp2(s - m).astype(BF16)
                pv = _dot(p, jnp.concatenate([v_s[krows, :].astype(BF16), ones], axis=1))
                m_s[bi, qrows, :] = jnp.broadcast_to(m, (g * blk, hd))
                l_s[bi, qrows, :] = pv[:, hd:]
                acc_s[bi, qrows, :] = pv[:, :hd]

    nbr = len(SWA_PATTERNS)
    ms = [m_s[i] for i in range(nbr)]
    mmax = functools.reduce(jnp.maximum, ms)
    wts = [jnp.exp2(mi - mmax) for mi in ms]
    num = sum(wts[i] * acc_s[i] for i in range(nbr))
    den = sum(wts[i] * l_s[i] for i in range(nbr))
    y_ref[...] = (num / den * _silu(g_ref[...].astype(F32))).astype(y_ref.dtype)


def _swa_prompt(proj):
    t = proj.shape[0]
    tb = SWA_TILE
    assert t % tb == 0
    hd = SWA_HD
    nbr = len(SWA_PATTERNS)
    cur = lambda off: pl.BlockSpec((tb, hd), lambda h, i: (i, off // hd + h))
    return pl.pallas_call(
        _swa_prompt_kernel,
        grid=(SWA_HEADS, t // tb),
        in_specs=[cur(E_BQ), cur(E_BK), cur(E_BV), cur(E_BG)],
        out_specs=pl.BlockSpec((tb, hd), lambda h, i: (i, h)),
        out_shape=jax.ShapeDtypeStruct((t, SWA_HEADS * hd), BF16),
        scratch_shapes=[pltpu.VMEM((tb, hd), F32),
                        pltpu.VMEM((2 * tb, hd), F32),
                        pltpu.VMEM((2 * tb, hd), F32),
                        pltpu.VMEM((nbr, tb, hd), F32),
                        pltpu.VMEM((nbr, tb, hd), F32),
                        pltpu.VMEM((nbr, tb, hd), F32)],
        compiler_params=_params("arbitrary", "arbitrary"),
        name="swa_prompt",
    )(proj, proj, proj, proj)


def _swa_sample_kernel(q_ref, k_ref, v_ref, g_ref, cos_ref, sin_ref, ka_ref, kb_ref, va_ref, vb_ref,
                       y_ref, ko_ref, kn_s, vn_s, ba_s, bb_s, bn_s, *, seq):
    pad = SAMPLE_PAD
    nh = SWA_HEADS
    hd = SWA_HD
    nr = nh * pad
    nj, nrh, _ = kb_ref.shape[1:]
    na = ka_ref.shape[1]
    nb = nj * nrh
    nn = kn_s.shape[0]

    @pl.when(pl.program_id(0) == 0)
    def _():
        (w1, d1), (w2, d2), (w3, d3) = SWA_PATTERNS

        def rows_cols(n):
            r = lax.broadcasted_iota(jnp.int32, (nr, n), 0)
            c = lax.broadcasted_iota(jnp.int32, (nr, n), 1)
            return _div(r, pad), _mod(r, pad), c

        def log2_mult(mult):
            return jnp.where(mult > 0, jnp.log2(jnp.maximum(mult, 1.0)), NEG_INF)

        hq, tq, c = rows_cols(na)
        dist = na // nh + tq - _div(c, nh)
        mult_a = ((dist <= w1).astype(F32) + ((_mod(dist, d2) == 0) & (dist <= w2)).astype(F32)
                  + ((_mod(dist, d3) == 0) & (dist <= w3)).astype(F32))
        ba_s[...] = log2_mult(jnp.where((hq == _mod(c, nh)) & (tq < seq), mult_a, 0.0))
        hq, tq, c = rows_cols(nb)
        bb_s[...] = log2_mult(jnp.where((hq == _mod(c, nh)) & (tq == _mod(_div(c, nh), nrh // nh)), 1.0, 0.0))
        hq, tq, c = rows_cols(nn)
        tk = _mod(c, pad)
        mult_n = (tk <= tq).astype(F32) + 2.0 * (tk == tq).astype(F32)
        bn_s[...] = log2_mult(jnp.where((hq == _div(c, pad)) & (tq < seq) & (tk < seq), mult_n, 0.0))
        kn_s[...] = jnp.zeros_like(kn_s)
        vn_s[...] = jnp.zeros_like(vn_s)

    cos = cos_ref[...]
    sin = sin_ref[...]
    q8 = _rot_partial(q_ref[0] * (SWA_HD ** -0.5 * LOG2E), cos, sin)
    k8 = _rot_partial(k_ref[0], cos, sin)
    ko_ref[0] = k8

    def head_rows(x):
        return jnp.concatenate([x[:, h * hd:(h + 1) * hd] for h in range(nh)], axis=0)

    qrows = head_rows(q8).astype(BF16)
    kn_s[0:nr, :] = head_rows(k8).astype(BF16)
    vn_s[0:nr, :] = head_rows(v_ref[0]).astype(BF16)

    kb = kb_ref[0].reshape(nb, hd)
    vb = vb_ref[0].reshape(nb, hd)
    s_a = _dot_nt(qrows, ka_ref[0].astype(BF16)) + ba_s[...]
    s_b = _dot_nt(qrows, kb.astype(BF16)) + bb_s[...]
    s_n = _dot_nt(qrows, kn_s[...]) + bn_s[...]
    rowmax = lambda s: jnp.max(s, axis=-1, keepdims=True)
    m = jnp.maximum(jnp.maximum(rowmax(s_a), rowmax(s_b)), rowmax(s_n))
    p_a, p_b, p_n = jnp.exp2(s_a - m), jnp.exp2(s_b - m), jnp.exp2(s_n - m)
    rowsum = lambda p: jnp.sum(p, axis=-1, keepdims=True)
    o = (_dot(p_a.astype(BF16), va_ref[0].astype(BF16)) + _dot(p_b.astype(BF16), vb.astype(BF16))
         + _dot(p_n.astype(BF16), vn_s[...])) / (rowsum(p_a) + rowsum(p_b) + rowsum(p_n))
    y = jnp.concatenate([o[h * pad:(h + 1) * pad] for h in range(nh)], axis=1)
    y_ref[0] = y * _silu(g_ref[0])


def _swa_sample(proj_pad, cache_k, cache_v, cos, sin, seq):
    nseq, pad, _ = proj_pad.shape
    buf = cache_k.shape[1]
    nh, hd = SWA_HEADS, SWA_HD
    w = nh * hd
    (w1, d1), (w2, d2), (w3, d3) = SWA_PATTERNS
    assert buf == w3 and d1 == 1 and seq <= d2 and w1 <= w2 and buf % w2 == 0 and buf % d3 == 0
    assert nh * pad <= SWA_BLOCK and (seq * nh) % 8 == 0
    flat = lambda a: a.reshape(nseq, buf * nh, hd)
    strided = lambda a: a.reshape(nseq, buf // d3, d3 * nh, hd)
    col = lambda off: pl.BlockSpec((1, pad, w), lambda b: (b, 0, off // w))
    tail = pl.BlockSpec((1, w2 * nh, hd), lambda b: (b, buf // w2 - 1, 0))
    resid = pl.BlockSpec((1, (buf - w2) // d3, seq * nh, hd), lambda b: (b, 0, 0, 0))
    tab = pl.BlockSpec((pad, w), lambda b: (0, 0))
    out = pl.BlockSpec((1, pad, w), lambda b: (b, 0, 0))
    return pl.pallas_call(
        functools.partial(_swa_sample_kernel, seq=seq),
        grid=(nseq,),
        in_specs=[col(E_BQ), col(E_BK), col(E_BV), col(E_BG), tab, tab, tail, resid, tail, resid],
        out_specs=[out, out],
        out_shape=[jax.ShapeDtypeStruct((nseq, pad, w), F32),
                   jax.ShapeDtypeStruct((nseq, pad, w), F32)],
        scratch_shapes=[pltpu.VMEM((SWA_BLOCK, hd), BF16), pltpu.VMEM((SWA_BLOCK, hd), BF16),
                        pltpu.VMEM((nh * pad, w2 * nh), F32),
                        pltpu.VMEM((nh * pad, (buf - w2) // d3 * seq * nh), F32),
                        pltpu.VMEM((nh * pad, SWA_BLOCK), F32)],
        compiler_params=_params("arbitrary"),
        name="swa_sample",
    )(proj_pad, proj_pad, proj_pad, proj_pad, cos, sin,
      flat(cache_k), strided(cache_k), flat(cache_v), strided(cache_v))


def _mem_attn_kernel(q_ref, g_ref, k_ref, v_ref, o_ref):
    ones = jnp.ones((MEM_LEN, MEM_HD), BF16)
    for h in range(MEM_HEADS):
        hs = slice(h * MEM_HD, (h + 1) * MEM_HD)
        q = (q_ref[0, :, hs].astype(F32) * (MEM_HD ** -0.5 * LOG2E)).astype(BF16)
        s = _dot_nt(q, k_ref[0, :, hs].astype(BF16))
        m = jnp.max(s, axis=-1, keepdims=True)
        p = jnp.exp2(s - m).astype(BF16)
        pv = _dot(p, jnp.concatenate([v_ref[0, :, hs].astype(BF16), ones], axis=1))
        o = pv[:, :MEM_HD] / pv[:, MEM_HD:]
        o_ref[0, :, hs] = (o * _silu(g_ref[0, :, hs].astype(F32))).astype(o_ref.dtype)


def _mem_attn_rows_kernel(q_ref, g_ref, k_ref, v_ref, o_ref, bias_s):
    nseq, pad, _ = q_ref.shape
    nh, hd = MEM_HEADS, MEM_HD

    @pl.when(pl.program_id(0) == 0)
    def _():
        r = lax.broadcasted_iota(jnp.int32, bias_s.shape, 0)
        c = lax.broadcasted_iota(jnp.int32, bias_s.shape, 1)
        bias_s[...] = jnp.where(_div(r, pad) == _mod(c, nh), 0.0, NEG_INF)

    for i in range(nseq):
        q = q_ref[i] * (MEM_HD ** -0.5 * LOG2E)
        qrows = jnp.concatenate([q[:, h * hd:(h + 1) * hd] for h in range(nh)], axis=0).astype(BF16)
        s = _dot_nt(qrows, k_ref[i].astype(BF16)) + bias_s[...]
        m = jnp.max(s, axis=-1, keepdims=True)
        p = jnp.exp2(s - m)
        o = _dot(p.astype(BF16), v_ref[i].astype(BF16)) / jnp.sum(p, axis=-1, keepdims=True)
        y = jnp.concatenate([o[h * pad:(h + 1) * pad] for h in range(nh)], axis=1)
        o_ref[i] = y * _silu(g_ref[i])


def _mem_attn_rows(proj_pad, q_off, g_off, mem_k, mem_v, layer):
    b, pad, _ = proj_pad.shape
    w = MEM_W
    rows = MEM_LEN * MEM_HEADS
    n = MEM_ROWS_STEP_SEQS
    assert b % n == 0
    kv = pl.BlockSpec((n, rows, MEM_HD), lambda i: (layer * (b // n) + i, 0, 0))
    return pl.pallas_call(
        _mem_attn_rows_kernel,
        grid=(b // n,),
        in_specs=[pl.BlockSpec((n, pad, w), lambda i: (i, 0, q_off // w)),
                  pl.BlockSpec((n, pad, w), lambda i: (i, 0, g_off // w)),
                  kv, kv],
        out_specs=pl.BlockSpec((n, pad, w), lambda i: (i, 0, 0)),
        out_shape=jax.ShapeDtypeStruct((b, pad, w), F32),
        scratch_shapes=[pltpu.VMEM((MEM_HEADS * pad, rows), F32)],
        compiler_params=_params("arbitrary"),
        name="mem_attn_rows",
    )(proj_pad, proj_pad, mem_k.reshape(-1, rows, MEM_HD), mem_v.reshape(-1, rows, MEM_HD))


def _mem_attn(proj, q_off, g_off, mem_k, k_off, mem_v, v_off, tq, out_dtype):
    b, t, _ = proj.shape
    w = MEM_W
    return pl.pallas_call(
        _mem_attn_kernel,
        grid=(b, t // tq),
        in_specs=[pl.BlockSpec((1, tq, w), lambda i, j: (i, j, q_off // w)),
                  pl.BlockSpec((1, tq, w), lambda i, j: (i, j, g_off // w)),
                  pl.BlockSpec((1, MEM_LEN, w), lambda i, j: (i, 0, k_off // w)),
                  pl.BlockSpec((1, MEM_LEN, w), lambda i, j: (i, 0, v_off // w))],
        out_specs=pl.BlockSpec((1, tq, w), lambda i, j: (i, j, 0)),
        out_shape=jax.ShapeDtypeStruct((b, t, w), out_dtype),
        compiler_params=_params("arbitrary", "arbitrary"),
        name="mem_attn",
    )(proj, proj, mem_k, mem_v)


def _gmlp_kernel(u_ref, v_ref, z_ref, ln_ref, w_ref, b_ref, y_ref, vo_ref, wc_ref, *, activated):
    c = GMLP_CHUNK
    gelu, silu = (lambda a: a, lambda a: a) if activated else (_gelu, _silu)

    @pl.when(pl.program_id(0) == 0)
    def _():
        causal = (lax.broadcasted_iota(jnp.int32, (c, c), 0) >= lax.broadcasted_iota(jnp.int32, (c, c), 1))
        for g in range(GMLP_GROUPS):
            wc_ref[g] = jnp.where(causal, w_ref[g], 0.0).astype(BF16)

    nchunks = u_ref.shape[0] // c
    keep_all = vo_ref.shape[0] == u_ref.shape[0]
    for i in range(nchunks):
        rs = slice(i * c, (i + 1) * c)
        v = gelu(v_ref[rs, :].astype(F32))
        mu = jnp.mean(v, axis=-1, keepdims=True)
        d = v - mu
        var = jnp.mean(d * d, axis=-1, keepdims=True)
        vn = d * lax.rsqrt(var + NORM_EPS) * ln_ref[...]
        if keep_all:
            vo_ref[rs, :] = vn
        elif i == nchunks - 1:
            vo_ref[...] = vn
        for g in range(GMLP_GROUPS):
            gs = slice(g * GMLP_GROUP, (g + 1) * GMLP_GROUP)
            mixed = _dot(wc_ref[g], vn[:, gs].astype(BF16)) + b_ref[:, gs]
            y_ref[rs, gs] = (gelu(u_ref[rs, gs].astype(F32)) * mixed
                             * silu(z_ref[rs, gs].astype(F32))).astype(y_ref.dtype)


def _gmlp(proj, gmlp_ln, w_mix, bias_rows, keep_all_v, activated):
    t = proj.shape[0]
    c = GMLP_CHUNK
    w = GMLP_WIDTH
    rows = GMLP_STEP_CHUNKS * c
    assert t % rows == 0
    col = lambda off: pl.BlockSpec((rows, w), lambda n: (n, off // w))
    return pl.pallas_call(
        functools.partial(_gmlp_kernel, activated=activated),
        grid=(t // rows,),
        in_specs=[col(O_U), col(O_V), col(O_Z),
                  pl.BlockSpec((1, w), lambda n: (0, 0)),
                  pl.BlockSpec(w_mix.shape, lambda n: (0, 0, 0)),
                  pl.BlockSpec((c, w), lambda n: (0, 0))],
        out_specs=[pl.BlockSpec((rows, w), lambda n: (n, 0)),
                   pl.BlockSpec((rows, w), lambda n: (n, 0)) if keep_all_v
                   else pl.BlockSpec((c, w), lambda n: (0, 0))],
        out_shape=[jax.ShapeDtypeStruct((t, w), BF16),
                   jax.ShapeDtypeStruct((t if keep_all_v else c, w), F32)],
        scratch_shapes=[pltpu.VMEM(w_mix.shape, BF16)],
        compiler_params=_params("arbitrary"),
        name="gmlp",
    )(proj, proj, proj, gmlp_ln.reshape(1, w), w_mix, bias_rows)


def _out_proj_kernel(*refs, n_in):
    a_refs, w_refs = refs[:n_in], refs[n_in:2 * n_in]
    g_ref, x_ref, o_ref = refs[2 * n_in:]
    sub = min(OUT_PROJ_SUB_ROWS, o_ref.shape[0])
    for r0 in range(0, o_ref.shape[0], sub):
        rs = slice(r0, r0 + sub)
        acc = _dot(a_refs[0][rs, :].astype(BF16), w_refs[0][...])
        for a_ref, w_ref in zip(a_refs[1:], w_refs[1:]):
            acc = acc + _dot(a_ref[rs, :].astype(BF16), w_ref[...])
        ms = jnp.mean(acc * acc, axis=-1, keepdims=True)
        o_ref[rs, :] = x_ref[rs, :] + acc * lax.rsqrt(ms + NORM_EPS) * g_ref[...]


def _out_proj(acts, w, g, x, tm):
    m, d = x.shape
    n_in = len(acts)
    w_specs, off = [], 0
    for a in acts:
        k = a.shape[1]
        assert off % k == 0
        w_specs.append(pl.BlockSpec((k, d), functools.partial(lambda i, blk: (blk, 0), blk=off // k),
                                    pipeline_mode=pl.Buffered(1)))
        off += k
    assert off == w.shape[0]
    return pl.pallas_call(
        functools.partial(_out_proj_kernel, n_in=n_in),
        grid=(m // tm,),
        in_specs=([pl.BlockSpec((tm, a.shape[1]), lambda i: (i, 0)) for a in acts]
                  + w_specs
                  + [pl.BlockSpec((1, d), lambda i: (0, 0)),
                     pl.BlockSpec((tm, d), lambda i: (i, 0))]),
        out_specs=pl.BlockSpec((tm, d), lambda i: (i, 0)),
        out_shape=jax.ShapeDtypeStruct((m, d), F32),
        compiler_params=_params("arbitrary"),
        name="out_proj",
    )(*acts, *([w] * n_in), g.reshape(1, d), x)


def _pad_steps(a, pad):
    nseq, seq, n = a.shape
    return jnp.pad(a, ((0, 0), (0, pad - seq), (0, 0))).reshape(nseq * pad, n)


def kernel(x_prompt, x_sample, state_ret, cache_swa_k, cache_swa_v, cache_mem_k, cache_mem_v, mem_prompt,
           pre_norm, post_norm, mem_norm, w_mem_k, w_mem_v, w_in_even, ret_gn, w_out_even,
           w_in_odd, gmlp_ln, w_spatial, b_spatial, w_out_odd):
    bp, tp, d = x_prompt.shape
    nseq, seq, _ = x_sample.shape
    pad = SAMPLE_PAD
    rows_s = nseq * pad
    c = GMLP_CHUNK
    assert bp == 1 and seq <= pad and rows_s % c == 0 and c % pad == 0
    hp = x_prompt.reshape(tp, d)
    hs = _pad_steps(x_sample, pad)
    mem = mem_prompt.reshape(MEM_LEN, d)
    tm_p = 1024 if tp % 1024 == 0 else 512
    unpad = lambda a: a.reshape(nseq, pad, -1)[:, :seq]

    pos_pad = np.concatenate([PAST_LEN + np.arange(seq), np.zeros((pad - seq,), np.int64)])
    ret_cos_p, ret_sin_p = _rot_tables_range(tp, RET_DK, RET_THETA, RET_DK)
    ret_cos_s, ret_sin_s = _rot_tables(np.tile(pos_pad, nseq), RET_DK, RET_THETA, RET_DK)
    swa_cos_p, swa_sin_p = _rot_tables_range(tp, ROPE_DIMS, ROPE_THETA, SWA_HD)
    swa_cos_s, swa_sin_s = _rot_tables(pos_pad, ROPE_DIMS, ROPE_THETA, SWA_HD)
    swa_cos_s = np.tile(swa_cos_s, (1, SWA_HEADS))
    swa_sin_s = np.tile(swa_sin_s, (1, SWA_HEADS))

    w_mem0 = jnp.concatenate([w_mem_k[0], w_mem_v[0]], axis=1).astype(BF16)
    memkv0 = _norm_matmul(mem, mem_norm[0], w_mem0, MEM_LEN, 2 * MEM_W)

    proj_s, w_in0 = _norm_matmul_cast(hs, pre_norm[0], w_in_even[0], 1024)
    proj_s3 = proj_s.reshape(nseq, pad, EVEN_IN)
    ya_s, st_s = _ret_sample(proj_s, state_ret[0], ret_cos_s, ret_sin_s, ret_gn[0], nseq, seq, pad)
    yb_s, sk = _swa_sample(proj_s3, cache_swa_k[0], cache_swa_v[0], swa_cos_s, swa_sin_s, seq)
    ym_s = _mem_attn_rows(proj_s3, E_MQ, E_MG, cache_mem_k, cache_mem_v, 0)
    sk = sk[:, :seq].reshape(nseq, seq, SWA_HEADS, SWA_HD)
    sv = unpad(proj_s[:, E_BV:E_BV + SWA_HEADS * SWA_HD]).reshape(nseq, seq, SWA_HEADS, SWA_HD)

    buf_p = min(SWA_MAX_WINDOW, tp)
    proj_p, kv_last, w_out0, w_in1, w_out1 = _norm_matmul(
        hp, pre_norm[0], w_in0, tm_p, 1024, BF16, (tp - buf_p, E_BK, E_BG - E_BK),
        casts=[(w_out_even[0], 256), (w_in_odd[0], 1024), (w_out_odd[0], 256)],
        rot=(swa_cos_p, swa_sin_p, E_BQ, E_BK, SWA_HD ** -0.5 * LOG2E))
    hs1 = _out_proj([ya_s, yb_s.reshape(rows_s, -1), ym_s.reshape(rows_s, -1)], w_out0, post_norm[0], hs, rows_s)
    ya_p, st_p = _ret_prompt(proj_p, ret_cos_p, ret_sin_p, ret_gn[0])
    yb_p = _swa_prompt(proj_p)
    pk = kv_last[:, :E_BV - E_BK]
    ym_p = _mem_attn(proj_p[None], E_MQ, E_MG, memkv0[None], 0, memkv0[None], MEM_W, 1024, BF16)[0]
    hp1 = _out_proj([ya_p, yb_p, ym_p], w_out0, post_norm[0], hp, 512)
    pv = kv_last[:, E_BV - E_BK:]

    w_mem1 = jnp.concatenate([w_mem_k[1], w_mem_v[1]], axis=1).astype(BF16)
    memkv1 = _norm_matmul(mem, mem_norm[1], w_mem1, MEM_LEN, 2 * MEM_W)
    bias_p = jnp.repeat(b_spatial[0].T, GMLP_GROUP, axis=1)
    blockdiag = lambda wg: jnp.kron(jnp.eye(c // pad, dtype=F32),
                                    jnp.pad(wg[:seq, :seq], ((0, pad - seq), (0, pad - seq))))
    w_mix_s = jax.vmap(blockdiag)(w_spatial[0])
    bias_s = jnp.tile(jnp.pad(bias_p[:seq], ((0, pad - seq), (0, 0))), (c // pad, 1))

    proj_s1 = _norm_matmul(hs1, pre_norm[1], w_in1, rows_s, 1024)
    yc_s, gv_s = _gmlp(proj_s1, gmlp_ln[0], w_mix_s, bias_s, True, False)
    ym_s1 = _mem_attn_rows(proj_s1.reshape(nseq, pad, ODD_IN), O_MQ, O_MG, cache_mem_k, cache_mem_v, 1)
    hs2 = _out_proj([yc_s, ym_s1.reshape(rows_s, -1)], w_out1, post_norm[1], hs1, rows_s)

    proj_p1 = _norm_matmul(hp1, pre_norm[1], w_in1, tm_p, 1024, BF16,
                           acts=[(O_U, O_Z, "gelu"), (O_Z, O_MQ, "silu")])
    yc_p, gv_p = _gmlp(proj_p1, gmlp_ln[0], w_spatial[0], bias_p, False, True)
    ym_p1 = _mem_attn(proj_p1[None], O_MQ, O_MG, memkv1[None], 0, memkv1[None], MEM_W, 1024, BF16)[0]
    hp2 = _out_proj([yc_p, ym_p1], w_out1, post_norm[1], hp1, 512)

    memkv = jnp.stack([memkv0, memkv1])
    p_mk = memkv[:, :, :MEM_W].reshape(2, bp, MEM_LEN, MEM_HEADS, MEM_HD)
    p_mv = memkv[:, :, MEM_W:].reshape(2, bp, MEM_LEN, MEM_HEADS, MEM_HD)
    return (hp2.reshape(bp, tp, d), unpad(hs2),
            st_p[None, None],
            pk.reshape(1, bp, buf_p, SWA_HEADS, SWA_HD), pv.reshape(1, bp, buf_p, SWA_HEADS, SWA_HD),
            p_mk, p_mv,
            gv_p[None, None],
            st_s[None],
            sk[None], sv[None],
            unpad(gv_s)[None])
```

```python
import functools
import math

import jax
import jax.numpy as jnp
import numpy as np
from jax import lax
from jax.experimental import pallas as pl
from jax.experimental.pallas import tpu as pltpu

F32 = jnp.float32
BF16 = jnp.bfloat16

D_MODEL = 2048
PAST_LEN = 8192
RET_HEADS = 8
RET_DK = 128
RET_DV = 256
RET_CHUNK = 128
RET_THETA = 10000.0
SWA_HEADS = 8
SWA_HD = 128
SWA_PATTERNS = ((128, 1), (512, 4), (2048, 16))
SWA_MAX_WINDOW = 2048
SWA_BLOCK = 128
ROPE_THETA = 500000.0
ROPE_DIMS = SWA_HD // 4
GMLP_CHUNK = 128
GMLP_WIDTH = D_MODEL
GMLP_GROUP = 128
GMLP_GROUPS = GMLP_WIDTH // GMLP_GROUP
MEM_LEN = 256
MEM_HEADS = 4
MEM_HD = 128
MEM_W = MEM_HEADS * MEM_HD
NORM_EPS = 1e-6
NEG_INF = -1e30
LOG2E = math.log2(math.e)

E_AQ = 0
E_AK = E_AQ + RET_HEADS * RET_DK
E_AV = E_AK + RET_HEADS * RET_DK
E_AG = E_AV + RET_HEADS * RET_DV
E_BQ = E_AG + RET_HEADS * RET_DV
E_BK = E_BQ + SWA_HEADS * SWA_HD
E_BV = E_BK + SWA_HEADS * SWA_HD
E_BG = E_BV + SWA_HEADS * SWA_HD
E_MQ = E_BG + SWA_HEADS * SWA_HD
E_MG = E_MQ + MEM_W
EVEN_IN = E_MG + MEM_W
O_U = 0
O_V = O_U + GMLP_WIDTH
O_Z = O_V + GMLP_WIDTH
O_MQ = O_Z + GMLP_WIDTH
O_MG = O_MQ + MEM_W
ODD_IN = O_MG + MEM_W

SWA_TILE = SWA_MAX_WINDOW
SAMPLE_PAD = 8
GMLP_STEP_CHUNKS = 2
SWA_GROUP = 1
RET_PROMPT_CHUNK = 256
RET_STEP_CHUNKS = 1
MEM_ROWS_STEP_SEQS = 2
RET_STEP_SEQS = 2
OUT_PROJ_SUB_ROWS = 256
NORM_SUB_ROWS = 256
VMEM_LIMIT = 56 * 1024 * 1024


def _params(*sem):
    return pltpu.CompilerParams(dimension_semantics=sem, vmem_limit_bytes=VMEM_LIMIT)


def _dot(a, b):
    return jnp.dot(a, b, preferred_element_type=F32)


def _dot_nt(a, b):
    return lax.dot_general(a, b, (((1,), (1,)), ((), ())), preferred_element_type=F32)


def _silu(x):
    return x / (1.0 + jnp.exp2(x * (-LOG2E)))


def _gelu(x):
    a = -2.0 * math.sqrt(2.0 / math.pi) * LOG2E
    return x / (1.0 + jnp.exp2(x * (a + (a * 0.044715) * (x * x))))


def _div(x, n):
    assert n & (n - 1) == 0
    return jnp.right_shift(x, n.bit_length() - 1)


def _mod(x, n):
    assert n & (n - 1) == 0
    return jnp.bitwise_and(x, n - 1)


def _rot_lanes(n_rot, theta, width):
    half = n_rot // 2
    inv = 1.0 / (theta ** (np.arange(half, dtype=np.float64) / half))
    rest = np.zeros(width - n_rot)
    return np.concatenate([inv, inv, rest]), np.concatenate([-np.ones(half), np.ones(half), rest])


def _rot_tables(pos, n_rot, theta, width):
    f, sgn = _rot_lanes(n_rot, theta, width)
    ang = np.asarray(pos, np.float64)[:, None] * f[None, :]
    return np.cos(ang).astype(np.float32), (sgn * np.sin(ang)).astype(np.float32)


def _rot_tables_range(n, n_rot, theta, width, blk=128):
    assert n % blk == 0
    f, sgn = _rot_lanes(n_rot, theta, width)
    hi = (np.arange(n // blk, dtype=np.float64) * blk)[:, None] * f[None, :]
    lo = np.arange(blk, dtype=np.float64)[:, None] * f[None, :]
    dev = lambda a: jnp.asarray(a, F32)
    ch, sh, cl, sl = dev(np.cos(hi)[:, None]), dev(np.sin(hi)[:, None]), dev(np.cos(lo)[None]), dev(np.sin(lo)[None])
    chs, shs = dev((sgn * np.cos(hi))[:, None]), dev((sgn * np.sin(hi))[:, None])
    return (ch * cl - sh * sl).reshape(n, width), (shs * cl + chs * sl).reshape(n, width)


def _rot_full(x, cos, sin):
    return x * cos + pltpu.roll(x, RET_DK // 2, 1) * sin


def _rot_partial(x, cos, sin):
    half = ROPE_DIMS // 2
    lane = _mod(lax.broadcasted_iota(jnp.int32, x.shape, 1), SWA_HD)
    width = x.shape[1]
    partner = jnp.where(lane < half, pltpu.roll(x, width - half, 1), pltpu.roll(x, half, 1))
    return x * cos + partner * sin


def _norm_matmul_kernel(x_ref, g_ref, w_ref, *rest, side, cast_cols, rot, acts, ntiles):
    if rot is not None:
        cos_ref, sin_ref, rest = rest[0], rest[1], rest[2:]
    nc = len(cast_cols)
    cast_in, o_ref, rest = rest[:nc], rest[nc], rest[nc + 1:]
    cast_out, xn_ref = rest[len(rest) - 1 - nc:len(rest) - 1], rest[-1]
    tm = x_ref.shape[0]
    sub = min(NORM_SUB_ROWS, tm)

    for src, dst, ncol in zip(cast_in, cast_out, cast_cols):
        @pl.when(pl.program_id(1) < ncol)
        def _(src=src, dst=dst):
            dst[...] = src[...].astype(BF16)

    kind = {}
    if rot is not None:
        jq, jk, q_scale = rot
        kind[jq], kind[jk] = "rope_q", "rope_k"
    for lo, hi, name in acts:
        kind.update({t: name for t in range(lo, hi)})

    def epilogue(name):
        if name in ("rope_q", "rope_k"):
            nrep = o_ref.shape[1] // cos_ref.shape[1]
            cos, sin = jnp.tile(cos_ref[...], (1, nrep)), jnp.tile(sin_ref[...], (1, nrep))
            return (lambda acc: _rot_partial(acc * q_scale, cos, sin)) if name == "rope_q" else (
                lambda acc: _rot_partial(acc, cos, sin))
        return {"gelu": _gelu, "silu": _silu, None: lambda acc: acc}[name]

    j = pl.program_id(1)

    @pl.when(j == 0)
    def _():
        first = epilogue(kind.get(0))
        for r0 in range(0, tm, sub):
            rs = slice(r0, r0 + sub)
            x = x_ref[rs, :]
            ms = jnp.mean(x * x, axis=-1, keepdims=True)
            xn = (x * lax.rsqrt(ms + NORM_EPS) * g_ref[...]).astype(BF16)
            xn_ref[rs, :] = xn
            o_ref[rs, :] = first(_dot(xn, w_ref[...])).astype(o_ref.dtype)

    def tail(name):
        acc = epilogue(name)(_dot(xn_ref[...], w_ref[...]))
        o_ref[...] = acc.astype(o_ref.dtype)
        if side is not None:
            i0, j0, nj = side
            i = pl.program_id(0)

            @pl.when((i >= i0) & (j >= j0) & (j < j0 + nj))
            def _():
                rest[0][...] = acc

    for name in sorted({kind.get(t) for t in range(1, ntiles)}, key=str):
        tiles = [t for t in range(1, ntiles) if kind.get(t) == name]
        pl.when(functools.reduce(jnp.logical_or, [j == t for t in tiles]))(functools.partial(tail, name))


def _norm_matmul(x, g, w, tm, tn, out_dtype=F32, side=None, casts=(), rot=None, acts=()):
    m, d = x.shape
    n = w.shape[1]
    ni, nj_grid = m // tm, n // tn
    out_specs = [pl.BlockSpec((tm, tn), lambda i, j: (i, j))]
    out_shape = [jax.ShapeDtypeStruct((m, n), out_dtype)]
    tiles = None
    if side is not None:
        row0, col0, ncols = side
        assert row0 % tm == 0 and col0 % tn == 0 and ncols % tn == 0 and col0 >= tn
        i0, j0, nj = row0 // tm, col0 // tn, ncols // tn
        tiles = (i0, j0, nj)
        out_specs.append(pl.BlockSpec(
            (tm, tn), lambda i, j: (jnp.maximum(i - i0, 0), jnp.where(i >= i0, jnp.clip(j - j0, 0, nj - 1), 0))))
        out_shape.append(jax.ShapeDtypeStruct((m - row0, ncols), F32))
    rot_tiles, rot_specs, rot_args = None, [], []
    if rot is not None:
        cos, sin, q_col, k_col, q_scale = rot
        assert q_col % tn == 0 and k_col % tn == 0 and q_col >= tn and k_col >= tn and tn % cos.shape[1] == 0
        rot_tiles = (q_col // tn, k_col // tn, q_scale)
        rot_specs = [pl.BlockSpec((tm, cos.shape[1]), lambda i, j: (i, 0))] * 2
        rot_args = [cos, sin]
    assert all(lo % tn == 0 and hi % tn == 0 for lo, hi, _ in acts)
    act_tiles = tuple((lo // tn, hi // tn, name) for lo, hi, name in acts)
    cast_specs, cast_cols = [], []
    for a, bc in casts:
        r, c = a.shape
        assert r % ni == 0 and (r // ni) % 16 == 0 and c % bc == 0 and c // bc <= nj_grid
        ncol = c // bc
        spec = pl.BlockSpec((r // ni, bc), functools.partial(lambda i, j, last: (i, jnp.minimum(j, last)), last=ncol - 1))
        cast_specs.append(spec)
        cast_cols.append(ncol)
        out_specs.append(spec)
        out_shape.append(jax.ShapeDtypeStruct((r, c), BF16))
    res = pl.pallas_call(
        functools.partial(_norm_matmul_kernel, side=tiles, cast_cols=tuple(cast_cols), rot=rot_tiles,
                          acts=act_tiles, ntiles=nj_grid),
        grid=(ni, nj_grid),
        in_specs=[pl.BlockSpec((tm, d), lambda i, j: (i, 0)),
                  pl.BlockSpec((1, d), lambda i, j: (0, 0)),
                  pl.BlockSpec((d, tn), lambda i, j: (0, j))] + rot_specs + cast_specs,
        out_specs=out_specs,
        out_shape=out_shape,
        scratch_shapes=[pltpu.VMEM((tm, d), BF16)],
        compiler_params=_params("arbitrary", "arbitrary"),
        name="norm_matmul",
    )(x, g.reshape(1, d), w, *rot_args, *[a for a, _ in casts])
    return res if len(res) > 1 else res[0]


def _norm_matmul_cast_kernel(x_ref, g_ref, w_ref, o_ref, wb_ref, xn_ref):
    @pl.when(pl.program_id(0) == 0)
    def _():
        x = x_ref[...]
        ms = jnp.mean(x * x, axis=-1, keepdims=True)
        xn_ref[...] = (x * lax.rsqrt(ms + NORM_EPS) * g_ref[...]).astype(BF16)

    wb = w_ref[...].astype(BF16)
    wb_ref[...] = wb
    o_ref[...] = _dot(xn_ref[...], wb).astype(o_ref.dtype)


def _norm_matmul_cast(x, g, w, tn):
    m, d = x.shape
    n = w.shape[1]
    return pl.pallas_call(
        _norm_matmul_cast_kernel,
        grid=(n // tn,),
        in_specs=[pl.BlockSpec((m, d), lambda j: (0, 0)),
                  pl.BlockSpec((1, d), lambda j: (0, 0)),
                  pl.BlockSpec((d, tn), lambda j: (0, j))],
        out_specs=[pl.BlockSpec((m, tn), lambda j: (0, j)),
                   pl.BlockSpec((d, tn), lambda j: (0, j))],
        out_shape=[jax.ShapeDtypeStruct((m, n), F32),
                   jax.ShapeDtypeStruct((d, n), BF16)],
        scratch_shapes=[pltpu.VMEM((m, d), BF16)],
        compiler_params=_params("arbitrary"),
        name="norm_matmul_cast",
    )(x, g.reshape(1, d), w)


def _ret_tables(c, reps, pad=None):
    pad = c if pad is None else pad
    rows = pad * reps
    lg = np.log1p(-np.exp2(-5.0 - np.arange(RET_HEADS, dtype=np.float64)))
    row = np.arange(rows)
    step = (row % pad).astype(np.float64)
    live = (row % pad) < c
    rel = step[:, None] - step[None, :]
    keep = ((row // pad)[:, None] == (row // pad)[None, :]) & (rel >= 0) & live[:, None] & live[None, :]
    decay = np.where(keep[None], np.exp(lg[:, None, None] * np.maximum(rel, 0.0)[None]), 0.0)
    q_decay = np.where(live[None], np.exp(lg[:, None] * (step[None, :] + 1.0)), 0.0)
    k_decay = np.where(live[None], np.exp(lg[:, None] * (c - 1.0 - step[None, :])), 0.0)
    chunk_decay = np.exp(lg * c)
    f32 = lambda a, shape: np.ascontiguousarray(np.broadcast_to(a, shape), dtype=np.float32)
    qd = f32(q_decay[:, :, None], (RET_HEADS, rows, RET_DK))
    kd = f32(k_decay[:, :, None], (RET_HEADS, rows, RET_DK))
    cd = f32(chunk_decay[:, None, None], (RET_HEADS, 1, RET_DV))
    return decay.astype(np.float32), qd, kd, cd


def _group_norm_gate(o, gn, g):
    mu = jnp.mean(o, axis=-1, keepdims=True)
    d = o - mu
    var = jnp.mean(d * d, axis=-1, keepdims=True)
    return d * lax.rsqrt(var + NORM_EPS) * gn * _silu(g.astype(F32))


def _ret_prompt_kernel(q_ref, k_ref, v_ref, g_ref, cos_ref, sin_ref, dec_ref, qd_ref, kd_ref, cd_ref, gn_ref,
                       y_ref, st_ref):
    @pl.when(pl.program_id(0) == 0)
    def _():
        st_ref[...] = jnp.zeros_like(st_ref)

    c = dec_ref.shape[1]
    for i in range(q_ref.shape[0] // c):
        rs = slice(i * c, (i + 1) * c)
        cos = cos_ref[rs, :]
        sin = sin_ref[rs, :]
        for h in range(RET_HEADS):
            ks = slice(h * RET_DK, (h + 1) * RET_DK)
            vs = slice(h * RET_DV, (h + 1) * RET_DV)
            q = _rot_full(q_ref[rs, ks].astype(F32), cos, sin)
            k = _rot_full(k_ref[rs, ks].astype(F32) * (RET_DK ** -0.5), cos, sin)
            v = v_ref[rs, vs].astype(BF16)
            st = st_ref[h]
            s = _dot_nt(q.astype(BF16), k.astype(BF16)) * dec_ref[h]
            lhs = jnp.concatenate([s.astype(BF16), (q * qd_ref[h]).astype(BF16)], axis=1)
            o = _dot(lhs, jnp.concatenate([v, st.astype(BF16)], axis=0))
            st_ref[h] = cd_ref[h] * st + _dot((k * kd_ref[h]).T.astype(BF16), v)
            y_ref[rs, vs] = _group_norm_gate(o, gn_ref[:, vs], g_ref[rs, vs]).astype(y_ref.dtype)


def _ret_prompt(proj, cos, sin, ret_gn):
    t = proj.shape[0]
    dec, qd, kd, cd = _ret_tables(RET_PROMPT_CHUNK, 1)
    c = RET_STEP_CHUNKS * RET_PROMPT_CHUNK
    assert t % c == 0
    qw = RET_HEADS * RET_DK
    vw = RET_HEADS * RET_DV
    full3 = lambda n: (0, 0, 0)
    return pl.pallas_call(
        _ret_prompt_kernel,
        grid=(t // c,),
        in_specs=[pl.BlockSpec((c, qw), lambda n: (n, E_AQ // qw)),
                  pl.BlockSpec((c, qw), lambda n: (n, E_AK // qw)),
                  pl.BlockSpec((c, vw), lambda n: (n, E_AV // vw)),
                  pl.BlockSpec((c, vw), lambda n: (n, E_AG // vw)),
                  pl.BlockSpec((c, RET_DK), lambda n: (n, 0)),
                  pl.BlockSpec((c, RET_DK), lambda n: (n, 0)),
                  pl.BlockSpec(dec.shape, full3),
                  pl.BlockSpec(qd.shape, full3),
                  pl.BlockSpec(kd.shape, full3),
                  pl.BlockSpec(cd.shape, full3),
                  pl.BlockSpec((1, vw), lambda n: (0, 0))],
        out_specs=[pl.BlockSpec((c, vw), lambda n: (n, 0)),
                   pl.BlockSpec((RET_HEADS, RET_DK, RET_DV), full3)],
        out_shape=[jax.ShapeDtypeStruct((t, vw), BF16),
                   jax.ShapeDtypeStruct((RET_HEADS, RET_DK, RET_DV), F32)],
        compiler_params=_params("arbitrary"),
        name="ret_prompt",
    )(proj, proj, proj, proj, cos, sin, dec, qd, kd, cd, ret_gn.reshape(1, vw))


def _ret_sample_kernel(q_ref, k_ref, v_ref, g_ref, cos_ref, sin_ref, dec_ref, qd_ref, kd_ref, cd_ref, gn_ref,
                       st_ref, y_ref, sto_ref, cross_ref, qs_ref, kt_ref, *, pad):
    b = pl.program_id(0)
    nb = pl.num_programs(0)
    win = RET_CHUNK

    @pl.when(b == 0)
    def _():
        cross_ref[...] = jnp.zeros_like(cross_ref)
        cos = cos_ref[...]
        sin = sin_ref[...]
        for h in range(RET_HEADS):
            ks = slice(h * RET_DK, (h + 1) * RET_DK)
            qs_ref[:, ks] = _rot_full(q_ref[:, ks], cos, sin) * qd_ref[h]
            kt_ref[h] = (_rot_full(k_ref[:, ks] * (RET_DK ** -0.5), cos, sin) * kd_ref[h]).T

    for i in range(st_ref.shape[0]):
        sq = b * st_ref.shape[0] + i
        w0 = pl.multiple_of(jnp.right_shift(sq * pad, win.bit_length() - 1) * win, win)
        wrows = pl.ds(w0, win)
        rows = lax.broadcasted_iota(jnp.int32, (win, RET_DK), 0) + w0
        cols = lax.broadcasted_iota(jnp.int32, (RET_DK, win), 1) + w0
        mine_r = (rows >= sq * pad) & (rows < (sq + 1) * pad)
        mine_c = (cols >= sq * pad) & (cols < (sq + 1) * pad)
        for h in range(RET_HEADS):
            ks = slice(h * RET_DK, (h + 1) * RET_DK)
            vs = slice(h * RET_DV, (h + 1) * RET_DV)
            st = st_ref[i, h]
            qm = jnp.where(mine_r, qs_ref[wrows, ks], 0.0)
            kmt = jnp.where(mine_c, kt_ref[h, :, wrows], 0.0)
            cross_ref[wrows, vs] += _dot(qm.astype(BF16), st.astype(BF16))
            sto_ref[i, h] = cd_ref[h] * st + _dot(kmt.astype(BF16), v_ref[wrows, vs].astype(BF16))

    @pl.when(b == nb - 1)
    def _():
        cos_all = cos_ref[...]
        sin_all = sin_ref[...]
        for h in range(RET_HEADS):
            ks = slice(h * RET_DK, (h + 1) * RET_DK)
            vs = slice(h * RET_DV, (h + 1) * RET_DV)
            q = _rot_full(q_ref[:, ks], cos_all, sin_all)
            k = _rot_full(k_ref[:, ks] * (RET_DK ** -0.5), cos_all, sin_all)
            v = v_ref[:, vs].astype(BF16)
            s = _dot_nt(q.astype(BF16), k.astype(BF16)) * dec_ref[h]
            o = _dot(s.astype(BF16), v) + cross_ref[:, vs]
            y_ref[:, vs] = _group_norm_gate(o, gn_ref[:, vs], g_ref[:, vs]).astype(y_ref.dtype)


def _ret_sample(proj, state, cos, sin, ret_gn, nseq, seq, pad):
    rows = nseq * pad
    c = math.gcd(seq, RET_CHUNK)
    assert c == seq, "sample chunk must cover the new tokens"
    assert rows % RET_CHUNK == 0 and RET_CHUNK % pad == 0
    dec, qd, kd, cd = _ret_tables(c, nseq, pad)
    qw = RET_HEADS * RET_DK
    vw = RET_HEADS * RET_DV
    full3 = lambda b: (0, 0, 0)
    n = RET_STEP_SEQS
    assert nseq % n == 0
    st_spec = pl.BlockSpec((n, RET_HEADS, RET_DK, RET_DV), lambda b: (b, 0, 0, 0))
    return pl.pallas_call(
        functools.partial(_ret_sample_kernel, pad=pad),
        grid=(nseq // n,),
        in_specs=[pl.BlockSpec((rows, qw), lambda b: (0, E_AQ // qw)),
                  pl.BlockSpec((rows, qw), lambda b: (0, E_AK // qw)),
                  pl.BlockSpec((rows, vw), lambda b: (0, E_AV // vw)),
                  pl.BlockSpec((rows, vw), lambda b: (0, E_AG // vw)),
                  pl.BlockSpec((rows, RET_DK), lambda b: (0, 0)),
                  pl.BlockSpec((rows, RET_DK), lambda b: (0, 0)),
                  pl.BlockSpec(dec.shape, full3),
                  pl.BlockSpec(qd.shape, full3),
                  pl.BlockSpec(kd.shape, full3),
                  pl.BlockSpec(cd.shape, full3),
                  pl.BlockSpec((1, vw), lambda b: (0, 0)),
                  st_spec],
        out_specs=[pl.BlockSpec((rows, vw), lambda b: (0, 0)), st_spec],
        out_shape=[jax.ShapeDtypeStruct((rows, vw), F32),
                   jax.ShapeDtypeStruct(state.shape, F32)],
        scratch_shapes=[pltpu.VMEM((rows, vw), F32),
                        pltpu.VMEM((rows, qw), F32),
                        pltpu.VMEM((RET_HEADS, RET_DK, rows), F32)],
        compiler_params=_params("arbitrary"),
        name="ret_sample",
    )(proj, proj, proj, proj, cos, sin, dec, qd, kd, cd, ret_gn.reshape(1, vw), state)


def _swa_prompt_kernel(q_ref, kc_ref, vc_ref, g_ref, y_ref, q_s, k_s, v_s, m_s, l_s, acc_s):
    t = pl.program_id(1)
    tb = SWA_TILE
    blk = SWA_BLOCK
    hd = SWA_HD

    @pl.when(t == 0)
    def _():
        k_s[0:tb, :] = jnp.zeros((tb, hd), F32)
        v_s[0:tb, :] = jnp.zeros((tb, hd), F32)

    @pl.when(t > 0)
    def _():
        k_s[0:tb, :] = k_s[tb:2 * tb, :]
        v_s[0:tb, :] = v_s[tb:2 * tb, :]

    q_s[...] = q_ref[...].astype(F32)
    k_s[tb:2 * tb, :] = kc_ref[...].astype(F32)
    v_s[tb:2 * tb, :] = vc_ref[...].astype(F32)

    def ds(start, size, stride):
        return pl.ds(start, size) if stride == 1 else pl.ds(start, size, stride=stride)

    def band(g):
        qi = lax.broadcasted_iota(jnp.int32, (g * blk, (g + 1) * blk), 0)
        kj = lax.broadcasted_iota(jnp.int32, (g * blk, (g + 1) * blk), 1)
        ok = (kj >= qi) & (kj <= qi + blk)
        return ok, ok & ((kj >= blk) | (t > 0))

    for bi, (window, r) in enumerate(SWA_PATTERNS):
        assert window // r == blk
        per_class = tb // (blk * r)
        g = SWA_GROUP if per_class % SWA_GROUP == 0 else 1
        ok, ok_first = band(g)
        ones = jnp.ones(((g + 1) * blk, hd), BF16)
        for c in range(r):
            for n in range(0, per_class, g):
                qs = n * blk * r + c
                ks = qs + tb - blk * r
                qrows = ds(qs, g * blk, r)
                krows = ds(ks, (g + 1) * blk, r)
                s = _dot_nt(q_s[qrows, :].astype(BF16), k_s[krows, :].astype(BF16))
                s = jnp.where(ok_first if n == 0 else ok, s, NEG_INF)
                m = jnp.max(s, axis=-1, keepdims=True)
                p = jnp.exp2(s - m).astype(BF16)
                pv = _dot(p, jnp.concatenate([v_s[krows, :].astype(BF16), ones], axis=1))
                m_s[bi, qrows, :] = jnp.broadcast_to(m, (g * blk, hd))
                l_s[bi, qrows, :] = pv[:, hd:]
                acc_s[bi, qrows, :] = pv[:, :hd]

    nbr = len(SWA_PATTERNS)
    ms = [m_s[i] for i in range(nbr)]
    mmax = functools.reduce(jnp.maximum, ms)
    wts = [jnp.exp2(mi - mmax) for mi in ms]
    num = sum(wts[i] * acc_s[i] for i in range(nbr))
    den = sum(wts[i] * l_s[i] for i in range(nbr))
    y_ref[...] = (num / den * _silu(g_ref[...].astype(F32))).astype(y_ref.dtype)


def _swa_prompt(proj):
    t = proj.shape[0]
    tb = SWA_TILE
    assert t % tb == 0
    hd = SWA_HD
    nbr = len(SWA_PATTERNS)
    cur = lambda off: pl.BlockSpec((tb, hd), lambda h, i: (i, off // hd + h))
    return pl.pallas_call(
        _swa_prompt_kernel,
        grid=(SWA_HEADS, t // tb),
        in_specs=[cur(E_BQ), cur(E_BK), cur(E_BV), cur(E_BG)],
        out_specs=pl.BlockSpec((tb, hd), lambda h, i: (i, h)),
        out_shape=jax.ShapeDtypeStruct((t, SWA_HEADS * hd), BF16),
        scratch_shapes=[pltpu.VMEM((tb, hd), F32),
                        pltpu.VMEM((2 * tb, hd), F32),
                        pltpu.VMEM((2 * tb, hd), F32),
                        pltpu.VMEM((nbr, tb, hd), F32),
                        pltpu.VMEM((nbr, tb, hd), F32),
                        pltpu.VMEM((nbr, tb, hd), F32)],
        compiler_params=_params("arbitrary", "arbitrary"),
        name="swa_prompt",
    )(proj, proj, proj, proj)


def _swa_sample_kernel(q_ref, k_ref, v_ref, g_ref, cos_ref, sin_ref, ka_ref, kb_ref, va_ref, vb_ref,
                       y_ref, ko_ref, kn_s, vn_s, ba_s, bb_s, bn_s, *, seq):
    pad = SAMPLE_PAD
    nh = SWA_HEADS
    hd = SWA_HD
    nr = nh * pad
    nj, nrh, _ = kb_ref.shape[1:]
    na = ka_ref.shape[1]
    nb = nj * nrh
    nn = kn_s.shape[0]

    @pl.when(pl.program_id(0) == 0)
    def _():
        (w1, d1), (w2, d2), (w3, d3) = SWA_PATTERNS

        def rows_cols(n):
            r = lax.broadcasted_iota(jnp.int32, (nr, n), 0)
            c = lax.broadcasted_iota(jnp.int32, (nr, n), 1)
            return _div(r, pad), _mod(r, pad), c

        def log2_mult(mult):
            return jnp.where(mult > 0, jnp.log2(jnp.maximum(mult, 1.0)), NEG_INF)

        hq, tq, c = rows_cols(na)
        dist = na // nh + tq - _div(c, nh)
        mult_a = ((dist <= w1).astype(F32) + ((_mod(dist, d2) == 0) & (dist <= w2)).astype(F32)
                  + ((_mod(dist, d3) == 0) & (dist <= w3)).astype(F32))
        ba_s[...] = log2_mult(jnp.where((hq == _mod(c, nh)) & (tq < seq), mult_a, 0.0))
        hq, tq, c = rows_cols(nb)
        bb_s[...] = log2_mult(jnp.where((hq == _mod(c, nh)) & (tq == _mod(_div(c, nh), nrh // nh)), 1.0, 0.0))
        hq, tq, c = rows_cols(nn)
        tk = _mod(c, pad)
        mult_n = (tk <= tq).astype(F32) + 2.0 * (tk == tq).astype(F32)
        bn_s[...] = log2_mult(jnp.where((hq == _div(c, pad)) & (tq < seq) & (tk < seq), mult_n, 0.0))
        kn_s[...] = jnp.zeros_like(kn_s)
        vn_s[...] = jnp.zeros_like(vn_s)

    cos = cos_ref[...]
    sin = sin_ref[...]
    q8 = _rot_partial(q_ref[0] * (SWA_HD ** -0.5 * LOG2E), cos, sin)
    k8 = _rot_partial(k_ref[0], cos, sin)
    ko_ref[0] = k8

    def head_rows(x):
        return jnp.concatenate([x[:, h * hd:(h + 1) * hd] for h in range(nh)], axis=0)

    qrows = head_rows(q8).astype(BF16)
    kn_s[0:nr, :] = head_rows(k8).astype(BF16)
    vn_s[0:nr, :] = head_rows(v_ref[0]).astype(BF16)

    kb = kb_ref[0].reshape(nb, hd)
    vb = vb_ref[0].reshape(nb, hd)
    s_a = _dot_nt(qrows, ka_ref[0].astype(BF16)) + ba_s[...]
    s_b = _dot_nt(qrows, kb.astype(BF16)) + bb_s[...]
    s_n = _dot_nt(qrows, kn_s[...]) + bn_s[...]
    rowmax = lambda s: jnp.max(s, axis=-1, keepdims=True)
    m = jnp.maximum(jnp.maximum(rowmax(s_a), rowmax(s_b)), rowmax(s_n))
    p_a, p_b, p_n = jnp.exp2(s_a - m), jnp.exp2(s_b - m), jnp.exp2(s_n - m)
    rowsum = lambda p: jnp.sum(p, axis=-1, keepdims=True)
    o = (_dot(p_a.astype(BF16), va_ref[0].astype(BF16)) + _dot(p_b.astype(BF16), vb.astype(BF16))
         + _dot(p_n.astype(BF16), vn_s[...])) / (rowsum(p_a) + rowsum(p_b) + rowsum(p_n))
    y = jnp.concatenate([o[h * pad:(h + 1) * pad] for h in range(nh)], axis=1)
    y_ref[0] = y * _silu(g_ref[0])


def _swa_sample(proj_pad, cache_k, cache_v, cos, sin, seq):
    nseq, pad, _ = proj_pad.shape
    buf = cache_k.shape[1]
    nh, hd = SWA_HEADS, SWA_HD
    w = nh * hd
    (w1, d1), (w2, d2), (w3, d3) = SWA_PATTERNS
    assert buf == w3 and d1 == 1 and seq <= d2 and w1 <= w2 and buf % w2 == 0 and buf % d3 == 0
    assert nh * pad <= SWA_BLOCK and (seq * nh) % 8 == 0
    flat = lambda a: a.reshape(nseq, buf * nh, hd)
    strided = lambda a: a.reshape(nseq, buf // d3, d3 * nh, hd)
    col = lambda off: pl.BlockSpec((1, pad, w), lambda b: (b, 0, off // w))
    tail = pl.BlockSpec((1, w2 * nh, hd), lambda b: (b, buf // w2 - 1, 0))
    resid = pl.BlockSpec((1, (buf - w2) // d3, seq * nh, hd), lambda b: (b, 0, 0, 0))
    tab = pl.BlockSpec((pad, w), lambda b: (0, 0))
    out = pl.BlockSpec((1, pad, w), lambda b: (b, 0, 0))
    return pl.pallas_call(
        functools.partial(_swa_sample_kernel, seq=seq),
        grid=(nseq,),
        in_specs=[col(E_BQ), col(E_BK), col(E_BV), col(E_BG), tab, tab, tail, resid, tail, resid],
        out_specs=[out, out],
        out_shape=[jax.ShapeDtypeStruct((nseq, pad, w), F32),
                   jax.ShapeDtypeStruct((nseq, pad, w), F32)],
        scratch_shapes=[pltpu.VMEM((SWA_BLOCK, hd), BF16), pltpu.VMEM((SWA_BLOCK, hd), BF16),
                        pltpu.VMEM((nh * pad, w2 * nh), F32),
                        pltpu.VMEM((nh * pad, (buf - w2) // d3 * seq * nh), F32),
                        pltpu.VMEM((nh * pad, SWA_BLOCK), F32)],
        compiler_params=_params("arbitrary"),
        name="swa_sample",
    )(proj_pad, proj_pad, proj_pad, proj_pad, cos, sin,
      flat(cache_k), strided(cache_k), flat(cache_v), strided(cache_v))


def _mem_attn_kernel(q_ref, g_ref, k_ref, v_ref, o_ref):
    ones = jnp.ones((MEM_LEN, MEM_HD), BF16)
    for h in range(MEM_HEADS):
        hs = slice(h * MEM_HD, (h + 1) * MEM_HD)
        q = (q_ref[0, :, hs].astype(F32) * (MEM_HD ** -0.5 * LOG2E)).astype(BF16)
        s = _dot_nt(q, k_ref[0, :, hs].astype(BF16))
        m = jnp.max(s, axis=-1, keepdims=True)
        p = jnp.exp2(s - m).astype(BF16)
        pv = _dot(p, jnp.concatenate([v_ref[0, :, hs].astype(BF16), ones], axis=1))
        o = pv[:, :MEM_HD] / pv[:, MEM_HD:]
        o_ref[0, :, hs] = (o * _silu(g_ref[0, :, hs].astype(F32))).astype(o_ref.dtype)


def _mem_attn_rows_kernel(q_ref, g_ref, k_ref, v_ref, o_ref, bias_s):
    nseq, pad, _ = q_ref.shape
    nh, hd = MEM_HEADS, MEM_HD

    @pl.when(pl.program_id(0) == 0)
    def _():
        r = lax.broadcasted_iota(jnp.int32, bias_s.shape, 0)
        c = lax.broadcasted_iota(jnp.int32, bias_s.shape, 1)
        bias_s[...] = jnp.where(_div(r, pad) == _mod(c, nh), 0.0, NEG_INF)

    for i in range(nseq):
        q = q_ref[i] * (MEM_HD ** -0.5 * LOG2E)
        qrows = jnp.concatenate([q[:, h * hd:(h + 1) * hd] for h in range(nh)], axis=0).astype(BF16)
        s = _dot_nt(qrows, k_ref[i].astype(BF16)) + bias_s[...]
        m = jnp.max(s, axis=-1, keepdims=True)
        p = jnp.exp2(s - m)
        o = _dot(p.astype(BF16), v_ref[i].astype(BF16)) / jnp.sum(p, axis=-1, keepdims=True)
        y = jnp.concatenate([o[h * pad:(h + 1) * pad] for h in range(nh)], axis=1)
        o_ref[i] = y * _silu(g_ref[i])


def _mem_attn_rows(proj_pad, q_off, g_off, mem_k, mem_v, layer):
    b, pad, _ = proj_pad.shape
    w = MEM_W
    rows = MEM_LEN * MEM_HEADS
    n = MEM_ROWS_STEP_SEQS
    assert b % n == 0
    kv = pl.BlockSpec((n, rows, MEM_HD), lambda i: (layer * (b // n) + i, 0, 0))
    return pl.pallas_call(
        _mem_attn_rows_kernel,
        grid=(b // n,),
        in_specs=[pl.BlockSpec((n, pad, w), lambda i: (i, 0, q_off // w)),
                  pl.BlockSpec((n, pad, w), lambda i: (i, 0, g_off // w)),
                  kv, kv],
        out_specs=pl.BlockSpec((n, pad, w), lambda i: (i, 0, 0)),
        out_shape=jax.ShapeDtypeStruct((b, pad, w), F32),
        scratch_shapes=[pltpu.VMEM((MEM_HEADS * pad, rows), F32)],
        compiler_params=_params("arbitrary"),
        name="mem_attn_rows",
    )(proj_pad, proj_pad, mem_k.reshape(-1, rows, MEM_HD), mem_v.reshape(-1, rows, MEM_HD))


def _mem_attn(proj, q_off, g_off, mem_k, k_off, mem_v, v_off, tq, out_dtype):
    b, t, _ = proj.shape
    w = MEM_W
    return pl.pallas_call(
        _mem_attn_kernel,
        grid=(b, t // tq),
        in_specs=[pl.BlockSpec((1, tq, w), lambda i, j: (i, j, q_off // w)),
                  pl.BlockSpec((1, tq, w), lambda i, j: (i, j, g_off // w)),
                  pl.BlockSpec((1, MEM_LEN, w), lambda i, j: (i, 0, k_off // w)),
                  pl.BlockSpec((1, MEM_LEN, w), lambda i, j: (i, 0, v_off // w))],
        out_specs=pl.BlockSpec((1, tq, w), lambda i, j: (i, j, 0)),
        out_shape=jax.ShapeDtypeStruct((b, t, w), out_dtype),
        compiler_params=_params("arbitrary", "arbitrary"),
        name="mem_attn",
    )(proj, proj, mem_k, mem_v)


def _gmlp_kernel(u_ref, v_ref, z_ref, ln_ref, w_ref, b_ref, y_ref, vo_ref, wc_ref, *, activated):
    c = GMLP_CHUNK
    gelu, silu = (lambda a: a, lambda a: a) if activated else (_gelu, _silu)

    @pl.when(pl.program_id(0) == 0)
    def _():
        causal = (lax.broadcasted_iota(jnp.int32, (c, c), 0) >= lax.broadcasted_iota(jnp.int32, (c, c), 1))
        for g in range(GMLP_GROUPS):
            wc_ref[g] = jnp.where(causal, w_ref[g], 0.0).astype(BF16)

    nchunks = u_ref.shape[0] // c
    keep_all = vo_ref.shape[0] == u_ref.shape[0]
    for i in range(nchunks):
        rs = slice(i * c, (i + 1) * c)
        v = gelu(v_ref[rs, :].astype(F32))
        mu = jnp.mean(v, axis=-1, keepdims=True)
        d = v - mu
        var = jnp.mean(d * d, axis=-1, keepdims=True)
        vn = d * lax.rsqrt(var + NORM_EPS) * ln_ref[...]
        if keep_all:
            vo_ref[rs, :] = vn
        elif i == nchunks - 1:
            vo_ref[...] = vn
        for g in range(GMLP_GROUPS):
            gs = slice(g * GMLP_GROUP, (g + 1) * GMLP_GROUP)
            mixed = _dot(wc_ref[g], vn[:, gs].astype(BF16)) + b_ref[:, gs]
            y_ref[rs, gs] = (gelu(u_ref[rs, gs].astype(F32)) * mixed
                             * silu(z_ref[rs, gs].astype(F32))).astype(y_ref.dtype)


def _gmlp(proj, gmlp_ln, w_mix, bias_rows, keep_all_v, activated):
    t = proj.shape[0]
    c = GMLP_CHUNK
    w = GMLP_WIDTH
    rows = GMLP_STEP_CHUNKS * c
    assert t % rows == 0
    col = lambda off: pl.BlockSpec((rows, w), lambda n: (n, off // w))
    return pl.pallas_call(
        functools.partial(_gmlp_kernel, activated=activated),
        grid=(t // rows,),
        in_specs=[col(O_U), col(O_V), col(O_Z),
                  pl.BlockSpec((1, w), lambda n: (0, 0)),
                  pl.BlockSpec(w_mix.shape, lambda n: (0, 0, 0)),
                  pl.BlockSpec((c, w), lambda n: (0, 0))],
        out_specs=[pl.BlockSpec((rows, w), lambda n: (n, 0)),
                   pl.BlockSpec((rows, w), lambda n: (n, 0)) if keep_all_v
                   else pl.BlockSpec((c, w), lambda n: (0, 0))],
        out_shape=[jax.ShapeDtypeStruct((t, w), BF16),
                   jax.ShapeDtypeStruct((t if keep_all_v else c, w), F32)],
        scratch_shapes=[pltpu.VMEM(w_mix.shape, BF16)],
        compiler_params=_params("arbitrary"),
        name="gmlp",
    )(proj, proj, proj, gmlp_ln.reshape(1, w), w_mix, bias_rows)


def _out_proj_kernel(*refs, n_in):
    a_refs, w_refs = refs[:n_in], refs[n_in:2 * n_in]
    g_ref, x_ref, o_ref = refs[2 * n_in:]
    sub = min(OUT_PROJ_SUB_ROWS, o_ref.shape[0])
    for r0 in range(0, o_ref.shape[0], sub):
        rs = slice(r0, r0 + sub)
        acc = _dot(a_refs[0][rs, :].astype(BF16), w_refs[0][...])
        for a_ref, w_ref in zip(a_refs[1:], w_refs[1:]):
            acc = acc + _dot(a_ref[rs, :].astype(BF16), w_ref[...])
        ms = jnp.mean(acc * acc, axis=-1, keepdims=True)
        o_ref[rs, :] = x_ref[rs, :] + acc * lax.rsqrt(ms + NORM_EPS) * g_ref[...]


def _out_proj(acts, w, g, x, tm):
    m, d = x.shape
    n_in = len(acts)
    w_specs, off = [], 0
    for a in acts:
        k = a.shape[1]
        assert off % k == 0
        w_specs.append(pl.BlockSpec((k, d), functools.partial(lambda i, blk: (blk, 0), blk=off // k),
                                    pipeline_mode=pl.Buffered(1)))
        off += k
    assert off == w.shape[0]
    return pl.pallas_call(
        functools.partial(_out_proj_kernel, n_in=n_in),
        grid=(m // tm,),
        in_specs=([pl.BlockSpec((tm, a.shape[1]), lambda i: (i, 0)) for a in acts]
                  + w_specs
                  + [pl.BlockSpec((1, d), lambda i: (0, 0)),
                     pl.BlockSpec((tm, d), lambda i: (i, 0))]),
        out_specs=pl.BlockSpec((tm, d), lambda i: (i, 0)),
        out_shape=jax.ShapeDtypeStruct((m, d), F32),
        compiler_params=_params("arbitrary"),
        name="out_proj",
    )(*acts, *([w] * n_in), g.reshape(1, d), x)


def _gmlp_out_proj_kernel(u_ref, v_ref, z_ref, ym_ref, ln_ref, w_ref, b_ref, wy_ref, wm_ref, g_ref, x_ref,
                          o_ref, vo_ref, wc_ref, yc_ref):
    c = GMLP_CHUNK
    sub = yc_ref.shape[0]
    tm = o_ref.shape[0]

    @pl.when(pl.program_id(0) == 0)
    def _():
        causal = (lax.broadcasted_iota(jnp.int32, (c, c), 0) >= lax.broadcasted_iota(jnp.int32, (c, c), 1))
        for g in range(GMLP_GROUPS):
            wc_ref[g] = jnp.where(causal, w_ref[g], 0.0).astype(BF16)

    for r0 in range(0, tm, sub):
        for c0 in range(r0, r0 + sub, c):
            rs = slice(c0, c0 + c)
            v = v_ref[rs, :].astype(F32)
            mu = jnp.mean(v, axis=-1, keepdims=True)
            d = v - mu
            var = jnp.mean(d * d, axis=-1, keepdims=True)
            vn = d * lax.rsqrt(var + NORM_EPS) * ln_ref[...]
            if c0 + c == tm:
                vo_ref[...] = vn
            for g in range(GMLP_GROUPS):
                gs = slice(g * GMLP_GROUP, (g + 1) * GMLP_GROUP)
                mixed = _dot(wc_ref[g], vn[:, gs].astype(BF16)) + b_ref[:, gs]
                yc_ref[c0 - r0:c0 - r0 + c, gs] = (u_ref[rs, gs].astype(F32) * mixed
                                                   * z_ref[rs, gs].astype(F32)).astype(BF16)
        rs = slice(r0, r0 + sub)
        acc = _dot(yc_ref[...], wy_ref[...]) + _dot(ym_ref[rs, :], wm_ref[...])
        ms = jnp.mean(acc * acc, axis=-1, keepdims=True)
        o_ref[rs, :] = x_ref[rs, :] + acc * lax.rsqrt(ms + NORM_EPS) * g_ref[...]


def _gmlp_out_proj(proj, ym, gmlp_ln, w_mix, bias_rows, w, g, x, tm):
    m, d = x.shape
    c = GMLP_CHUNK
    gw = GMLP_WIDTH
    sub = min(OUT_PROJ_SUB_ROWS, tm)
    assert m % tm == 0 and tm % sub == 0 and sub % c == 0 and w.shape[0] == gw + ym.shape[1] and gw % ym.shape[1] == 0
    col = lambda off: pl.BlockSpec((tm, gw), lambda i: (i, off // gw))
    const = lambda shape, idx: pl.BlockSpec(shape, lambda i: idx, pipeline_mode=pl.Buffered(1))
    return pl.pallas_call(
        _gmlp_out_proj_kernel,
        grid=(m // tm,),
        in_specs=[col(O_U), col(O_V), col(O_Z),
                  pl.BlockSpec((tm, ym.shape[1]), lambda i: (i, 0)),
                  pl.BlockSpec((1, gw), lambda i: (0, 0)),
                  pl.BlockSpec(w_mix.shape, lambda i: (0, 0, 0)),
                  pl.BlockSpec((c, gw), lambda i: (0, 0)),
                  const((gw, d), (0, 0)),
                  const((ym.shape[1], d), (gw // ym.shape[1], 0)),
                  pl.BlockSpec((1, d), lambda i: (0, 0)),
                  pl.BlockSpec((tm, d), lambda i: (i, 0))],
        out_specs=[pl.BlockSpec((tm, d), lambda i: (i, 0)),
                   pl.BlockSpec((c, gw), lambda i: (0, 0))],
        out_shape=[jax.ShapeDtypeStruct((m, d), F32),
                   jax.ShapeDtypeStruct((c, gw), F32)],
        scratch_shapes=[pltpu.VMEM(w_mix.shape, BF16), pltpu.VMEM((sub, gw), BF16)],
        compiler_params=_params("arbitrary"),
        name="gmlp_out_proj",
    )(proj, proj, proj, ym, gmlp_ln.reshape(1, gw), w_mix, bias_rows, w, w, g.reshape(1, d), x)


def _pad_steps(a, pad):
    nseq, seq, n = a.shape
    return jnp.pad(a, ((0, 0), (0, pad - seq), (0, 0))).reshape(nseq * pad, n)


def kernel(x_prompt, x_sample, state_ret, cache_swa_k, cache_swa_v, cache_mem_k, cache_mem_v, mem_prompt,
           pre_norm, post_norm, mem_norm, w_mem_k, w_mem_v, w_in_even, ret_gn, w_out_even,
           w_in_odd, gmlp_ln, w_spatial, b_spatial, w_out_odd):
    bp, tp, d = x_prompt.shape
    nseq, seq, _ = x_sample.shape
    pad = SAMPLE_PAD
    rows_s = nseq * pad
    c = GMLP_CHUNK
    assert bp == 1 and seq <= pad and rows_s % c == 0 and c % pad == 0
    hp = x_prompt.reshape(tp, d)
    hs = _pad_steps(x_sample, pad)
    mem = mem_prompt.reshape(MEM_LEN, d)
    tm_p = 1024 if tp % 1024 == 0 else 512
    unpad = lambda a: a.reshape(nseq, pad, -1)[:, :seq]

    pos_pad = np.concatenate([PAST_LEN + np.arange(seq), np.zeros((pad - seq,), np.int64)])
    ret_cos_p, ret_sin_p = _rot_tables_range(tp, RET_DK, RET_THETA, RET_DK)
    ret_cos_s, ret_sin_s = _rot_tables(np.tile(pos_pad, nseq), RET_DK, RET_THETA, RET_DK)
    swa_cos_p, swa_sin_p = _rot_tables_range(tp, ROPE_DIMS, ROPE_THETA, SWA_HD)
    swa_cos_s, swa_sin_s = _rot_tables(pos_pad, ROPE_DIMS, ROPE_THETA, SWA_HD)
    swa_cos_s = np.tile(swa_cos_s, (1, SWA_HEADS))
    swa_sin_s = np.tile(swa_sin_s, (1, SWA_HEADS))

    w_mem0 = jnp.concatenate([w_mem_k[0], w_mem_v[0]], axis=1).astype(BF16)
    memkv0 = _norm_matmul(mem, mem_norm[0], w_mem0, MEM_LEN, 2 * MEM_W)

    proj_s, w_in0 = _norm_matmul_cast(hs, pre_norm[0], w_in_even[0], 1024)
    proj_s3 = proj_s.reshape(nseq, pad, EVEN_IN)
    ya_s, st_s = _ret_sample(proj_s, state_ret[0], ret_cos_s, ret_sin_s, ret_gn[0], nseq, seq, pad)
    yb_s, sk = _swa_sample(proj_s3, cache_swa_k[0], cache_swa_v[0], swa_cos_s, swa_sin_s, seq)
    ym_s = _mem_attn_rows(proj_s3, E_MQ, E_MG, cache_mem_k, cache_mem_v, 0)
    sk = sk[:, :seq].reshape(nseq, seq, SWA_HEADS, SWA_HD)
    sv = unpad(proj_s[:, E_BV:E_BV + SWA_HEADS * SWA_HD]).reshape(nseq, seq, SWA_HEADS, SWA_HD)

    buf_p = min(SWA_MAX_WINDOW, tp)
    proj_p, kv_last, w_out0, w_in1, w_out1 = _norm_matmul(
        hp, pre_norm[0], w_in0, tm_p, 1024, BF16, (tp - buf_p, E_BK, E_BG - E_BK),
        casts=[(w_out_even[0], 256), (w_in_odd[0], 1024), (w_out_odd[0], 256)],
        rot=(swa_cos_p, swa_sin_p, E_BQ, E_BK, SWA_HD ** -0.5 * LOG2E))
    hs1 = _out_proj([ya_s, yb_s.reshape(rows_s, -1), ym_s.reshape(rows_s, -1)], w_out0, post_norm[0], hs, rows_s)
    ya_p, st_p = _ret_prompt(proj_p, ret_cos_p, ret_sin_p, ret_gn[0])
    yb_p = _swa_prompt(proj_p)
    pk = kv_last[:, :E_BV - E_BK]
    ym_p = _mem_attn(proj_p[None], E_MQ, E_MG, memkv0[None], 0, memkv0[None], MEM_W, 1024, BF16)[0]
    hp1 = _out_proj([ya_p, yb_p, ym_p], w_out0, post_norm[0], hp, 512)
    pv = kv_last[:, E_BV - E_BK:]

    w_mem1 = jnp.concatenate([w_mem_k[1], w_mem_v[1]], axis=1).astype(BF16)
    memkv1 = _norm_matmul(mem, mem_norm[1], w_mem1, MEM_LEN, 2 * MEM_W)
    bias_p = jnp.repeat(b_spatial[0].T, GMLP_GROUP, axis=1)
    blockdiag = lambda wg: jnp.kron(jnp.eye(c // pad, dtype=F32),
                                    jnp.pad(wg[:seq, :seq], ((0, pad - seq), (0, pad - seq))))
    w_mix_s = jax.vmap(blockdiag)(w_spatial[0])
    bias_s = jnp.tile(jnp.pad(bias_p[:seq], ((0, pad - seq), (0, 0))), (c // pad, 1))

    proj_s1 = _norm_matmul(hs1, pre_norm[1], w_in1, rows_s, 1024)
    yc_s, gv_s = _gmlp(proj_s1, gmlp_ln[0], w_mix_s, bias_s, True, False)
    ym_s1 = _mem_attn_rows(proj_s1.reshape(nseq, pad, ODD_IN), O_MQ, O_MG, cache_mem_k, cache_mem_v, 1)
    hs2 = _out_proj([yc_s, ym_s1.reshape(rows_s, -1)], w_out1, post_norm[1], hs1, rows_s)

    proj_p1 = _norm_matmul(hp1, pre_norm[1], w_in1, tm_p, 1024, BF16,
                           acts=[(O_U, O_Z, "gelu"), (O_Z, O_MQ, "silu")])
    ym_p1 = _mem_attn(proj_p1[None], O_MQ, O_MG, memkv1[None], 0, memkv1[None], MEM_W, 1024, BF16)[0]
    hp2, gv_p = _gmlp_out_proj(proj_p1, ym_p1, gmlp_ln[0], w_spatial[0], bias_p, w_out1, post_norm[1], hp1, 512)

    memkv = jnp.stack([memkv0, memkv1])
    p_mk = memkv[:, :, :MEM_W].reshape(2, bp, MEM_LEN, MEM_HEADS, MEM_HD)
    p_mv = memkv[:, :, MEM_W:].reshape(2, bp, MEM_LEN, MEM_HEADS, MEM_HD)
    return (hp2.reshape(bp, tp, d), unpad(hs2),
            st_p[None, None],
            pk.reshape(1, bp, buf_p, SWA_HEADS, SWA_HD), pv.reshape(1, bp, buf_p, SWA_HEADS, SWA_HD),
            p_mk, p_mv,
            gv_p[None, None],
            st_s[None],
            sk[None], sv[None],
            unpad(gv_s)[None])
```

```python
import functools
import math

import jax
import jax.numpy as jnp
import numpy as np
from jax import lax
from jax.experimental import pallas as pl
from jax.experimental.pallas import tpu as pltpu

F32 = jnp.float32
BF16 = jnp.bfloat16

D_MODEL = 2048
PAST_LEN = 8192
RET_HEADS = 8
RET_DK = 128
RET_DV = 256
RET_CHUNK = 128
RET_THETA = 10000.0
SWA_HEADS = 8
SWA_HD = 128
SWA_PATTERNS = ((128, 1), (512, 4), (2048, 16))
SWA_MAX_WINDOW = 2048
SWA_BLOCK = 128
ROPE_THETA = 500000.0
ROPE_DIMS = SWA_HD // 4
GMLP_CHUNK = 128
GMLP_WIDTH = D_MODEL
GMLP_GROUP = 128
GMLP_GROUPS = GMLP_WIDTH // GMLP_GROUP
MEM_LEN = 256
MEM_HEADS = 4
MEM_HD = 128
MEM_W = MEM_HEADS * MEM_HD
NORM_EPS = 1e-6
NEG_INF = -1e30
LOG2E = math.log2(math.e)

E_AQ = 0
E_AK = E_AQ + RET_HEADS * RET_DK
E_AV = E_AK + RET_HEADS * RET_DK
E_AG = E_AV + RET_HEADS * RET_DV
E_BQ = E_AG + RET_HEADS * RET_DV
E_BK = E_BQ + SWA_HEADS * SWA_HD
E_BV = E_BK + SWA_HEADS * SWA_HD
E_BG = E_BV + SWA_HEADS * SWA_HD
E_MQ = E_BG + SWA_HEADS * SWA_HD
E_MG = E_MQ + MEM_W
EVEN_IN = E_MG + MEM_W
O_U = 0
O_V = O_U + GMLP_WIDTH
O_Z = O_V + GMLP_WIDTH
O_MQ = O_Z + GMLP_WIDTH
O_MG = O_MQ + MEM_W
ODD_IN = O_MG + MEM_W

SWA_TILE = SWA_MAX_WINDOW
SAMPLE_PAD = 8
GMLP_STEP_CHUNKS = 2
SWA_GROUP = 1
RET_PROMPT_CHUNK = 256
RET_STEP_CHUNKS = 1
MEM_ROWS_STEP_SEQS = 2
RET_STEP_SEQS = 2
OUT_PROJ_SUB_ROWS = 256
NORM_SUB_ROWS = 256
VMEM_LIMIT = 56 * 1024 * 1024


def _params(*sem):
    return pltpu.CompilerParams(dimension_semantics=sem, vmem_limit_bytes=VMEM_LIMIT)


def _dot(a, b):
    return jnp.dot(a, b, preferred_element_type=F32)


def _dot_nt(a, b):
    return lax.dot_general(a, b, (((1,), (1,)), ((), ())), preferred_element_type=F32)


def _silu(x):
    return x / (1.0 + jnp.exp2(x * (-LOG2E)))


def _gelu(x):
    a = -2.0 * math.sqrt(2.0 / math.pi) * LOG2E
    return x / (1.0 + jnp.exp2(x * (a + (a * 0.044715) * (x * x))))


def _div(x, n):
    assert n & (n - 1) == 0
    return jnp.right_shift(x, n.bit_length() - 1)


def _mod(x, n):
    assert n & (n - 1) == 0
    return jnp.bitwise_and(x, n - 1)


def _rot_lanes(n_rot, theta, width):
    half = n_rot // 2
    inv = 1.0 / (theta ** (np.arange(half, dtype=np.float64) / half))
    rest = np.zeros(width - n_rot)
    return np.concatenate([inv, inv, rest]), np.concatenate([-np.ones(half), np.ones(half), rest])


def _rot_tables(pos, n_rot, theta, width):
    f, sgn = _rot_lanes(n_rot, theta, width)
    ang = np.asarray(pos, np.float64)[:, None] * f[None, :]
    return np.cos(ang).astype(np.float32), (sgn * np.sin(ang)).astype(np.float32)


def _rot_tables_range(n, n_rot, theta, width, blk=128):
    assert n % blk == 0
    f, sgn = _rot_lanes(n_rot, theta, width)
    hi = (np.arange(n // blk, dtype=np.float64) * blk)[:, None] * f[None, :]
    lo = np.arange(blk, dtype=np.float64)[:, None] * f[None, :]
    dev = lambda a: jnp.asarray(a, F32)
    ch, sh, cl, sl = dev(np.cos(hi)[:, None]), dev(np.sin(hi)[:, None]), dev(np.cos(lo)[None]), dev(np.sin(lo)[None])
    chs, shs = dev((sgn * np.cos(hi))[:, None]), dev((sgn * np.sin(hi))[:, None])
    return (ch * cl - sh * sl).reshape(n, width), (shs * cl + chs * sl).reshape(n, width)


def _rot_full(x, cos, sin):
    return x * cos + pltpu.roll(x, RET_DK // 2, 1) * sin


def _rot_partial(x, cos, sin):
    half = ROPE_DIMS // 2
    lane = _mod(lax.broadcasted_iota(jnp.int32, x.shape, 1), SWA_HD)
    width = x.shape[1]
    partner = jnp.where(lane < half, pltpu.roll(x, width - half, 1), pltpu.roll(x, half, 1))
    return x * cos + partner * sin


def _norm_matmul_kernel(x_ref, g_ref, w_ref, *rest, side, cast_cols, rot, acts, ntiles):
    if rot is not None:
        cos_ref, sin_ref, rest = rest[0], rest[1], rest[2:]
    nc = len(cast_cols)
    cast_in, o_ref, rest = rest[:nc], rest[nc], rest[nc + 1:]
    cast_out, xn_ref = rest[len(rest) - 1 - nc:len(rest) - 1], rest[-1]
    tm = x_ref.shape[0]
    sub = min(NORM_SUB_ROWS, tm)

    for src, dst, ncol in zip(cast_in, cast_out, cast_cols):
        @pl.when(pl.program_id(1) < ncol)
        def _(src=src, dst=dst):
            dst[...] = src[...].astype(BF16)

    kind = {}
    if rot is not None:
        jq, jk, q_scale = rot
        kind[jq], kind[jk] = "rope_q", "rope_k"
    for lo, hi, name in acts:
        kind.update({t: name for t in range(lo, hi)})

    def epilogue(name):
        if name in ("rope_q", "rope_k"):
            nrep = o_ref.shape[1] // cos_ref.shape[1]
            cos, sin = jnp.tile(cos_ref[...], (1, nrep)), jnp.tile(sin_ref[...], (1, nrep))
            return (lambda acc: _rot_partial(acc * q_scale, cos, sin)) if name == "rope_q" else (
                lambda acc: _rot_partial(acc, cos, sin))
        return {"gelu": _gelu, "silu": _silu, None: lambda acc: acc}[name]

    j = pl.program_id(1)

    @pl.when(j == 0)
    def _():
        first = epilogue(kind.get(0))
        for r0 in range(0, tm, sub):
            rs = slice(r0, r0 + sub)
            x = x_ref[rs, :]
            ms = jnp.mean(x * x, axis=-1, keepdims=True)
            xn = (x * lax.rsqrt(ms + NORM_EPS) * g_ref[...]).astype(BF16)
            xn_ref[rs, :] = xn
            o_ref[rs, :] = first(_dot(xn, w_ref[...])).astype(o_ref.dtype)

    def tail(name):
        acc = epilogue(name)(_dot(xn_ref[...], w_ref[...]))
        o_ref[...] = acc.astype(o_ref.dtype)
        if side is not None:
            i0, j0, nj = side
            i = pl.program_id(0)

            @pl.when((i >= i0) & (j >= j0) & (j < j0 + nj))
            def _():
                rest[0][...] = acc

    for name in sorted({kind.get(t) for t in range(1, ntiles)}, key=str):
        tiles = [t for t in range(1, ntiles) if kind.get(t) == name]
        pl.when(functools.reduce(jnp.logical_or, [j == t for t in tiles]))(functools.partial(tail, name))


def _norm_matmul(x, g, w, tm, tn, out_dtype=F32, side=None, casts=(), rot=None, acts=()):
    m, d = x.shape
    n = w.shape[1]
    ni, nj_grid = m // tm, n // tn
    out_specs = [pl.BlockSpec((tm, tn), lambda i, j: (i, j))]
    out_shape = [jax.ShapeDtypeStruct((m, n), out_dtype)]
    tiles = None
    if side is not None:
        row0, col0, ncols = side
        assert row0 % tm == 0 and col0 % tn == 0 and ncols % tn == 0 and col0 >= tn
        i0, j0, nj = row0 // tm, col0 // tn, ncols // tn
        tiles = (i0, j0, nj)
        out_specs.append(pl.BlockSpec(
            (tm, tn), lambda i, j: (jnp.maximum(i - i0, 0), jnp.where(i >= i0, jnp.clip(j - j0, 0, nj - 1), 0))))
        out_shape.append(jax.ShapeDtypeStruct((m - row0, ncols), F32))
    rot_tiles, rot_specs, rot_args = None, [], []
    if rot is not None:
        cos, sin, q_col, k_col, q_scale = rot
        assert q_col % tn == 0 and k_col % tn == 0 and q_col >= tn and k_col >= tn and tn % cos.shape[1] == 0
        rot_tiles = (q_col // tn, k_col // tn, q_scale)
        rot_specs = [pl.BlockSpec((tm, cos.shape[1]), lambda i, j: (i, 0))] * 2
        rot_args = [cos, sin]
    assert all(lo % tn == 0 and hi % tn == 0 for lo, hi, _ in acts)
    act_tiles = tuple((lo // tn, hi // tn, name) for lo, hi, name in acts)
    cast_specs, cast_cols = [], []
    for a, bc in casts:
        r, c = a.shape
        assert r % ni == 0 and (r // ni) % 16 == 0 and c % bc == 0 and c // bc <= nj_grid
        ncol = c // bc
        spec = pl.BlockSpec((r // ni, bc), functools.partial(lambda i, j, last: (i, jnp.minimum(j, last)), last=ncol - 1))
        cast_specs.append(spec)
        cast_cols.append(ncol)
        out_specs.append(spec)
        out_shape.append(jax.ShapeDtypeStruct((r, c), BF16))
    res = pl.pallas_call(
        functools.partial(_norm_matmul_kernel, side=tiles, cast_cols=tuple(cast_cols), rot=rot_tiles,
                          acts=act_tiles, ntiles=nj_grid),
        grid=(ni, nj_grid),
        in_specs=[pl.BlockSpec((tm, d), lambda i, j: (i, 0)),
                  pl.BlockSpec((1, d), lambda i, j: (0, 0)),
                  pl.BlockSpec((d, tn), lambda i, j: (0, j))] + rot_specs + cast_specs,
        out_specs=out_specs,
        out_shape=out_shape,
        scratch_shapes=[pltpu.VMEM((tm, d), BF16)],
        compiler_params=_params("arbitrary", "arbitrary"),
        name="norm_matmul",
    )(x, g.reshape(1, d), w, *rot_args, *[a for a, _ in casts])
    return res if len(res) > 1 else res[0]


def _norm_matmul_cast_kernel(x_ref, g_ref, w_ref, o_ref, wb_ref, xn_ref):
    @pl.when(pl.program_id(0) == 0)
    def _():
        x = x_ref[...]
        ms = jnp.mean(x * x, axis=-1, keepdims=True)
        xn_ref[...] = (x * lax.rsqrt(ms + NORM_EPS) * g_ref[...]).astype(BF16)

    wb = w_ref[...].astype(BF16)
    wb_ref[...] = wb
    o_ref[...] = _dot(xn_ref[...], wb).astype(o_ref.dtype)


def _norm_matmul_cast(x, g, w, tn):
    m, d = x.shape
    n = w.shape[1]
    return pl.pallas_call(
        _norm_matmul_cast_kernel,
        grid=(n // tn,),
        in_specs=[pl.BlockSpec((m, d), lambda j: (0, 0)),
                  pl.BlockSpec((1, d), lambda j: (0, 0)),
                  pl.BlockSpec((d, tn), lambda j: (0, j))],
        out_specs=[pl.BlockSpec((m, tn), lambda j: (0, j)),
                   pl.BlockSpec((d, tn), lambda j: (0, j))],
        out_shape=[jax.ShapeDtypeStruct((m, n), F32),
                   jax.ShapeDtypeStruct((d, n), BF16)],
        scratch_shapes=[pltpu.VMEM((m, d), BF16)],
        compiler_params=_params("arbitrary"),
        name="norm_matmul_cast",
    )(x, g.reshape(1, d), w)


def _ret_tables(c, reps, pad=None):
    pad = c if pad is None else pad
    rows = pad * reps
    lg = np.log1p(-np.exp2(-5.0 - np.arange(RET_HEADS, dtype=np.float64)))
    row = np.arange(rows)
    step = (row % pad).astype(np.float64)
    live = (row % pad) < c
    rel = step[:, None] - step[None, :]
    keep = ((row // pad)[:, None] == (row // pad)[None, :]) & (rel >= 0) & live[:, None] & live[None, :]
    decay = np.where(keep[None], np.exp(lg[:, None, None] * np.maximum(rel, 0.0)[None]), 0.0)
    q_decay = np.where(live[None], np.exp(lg[:, None] * (step[None, :] + 1.0)), 0.0)
    k_decay = np.where(live[None], np.exp(lg[:, None] * (c - 1.0 - step[None, :])), 0.0)
    chunk_decay = np.exp(lg * c)
    f32 = lambda a, shape: np.ascontiguousarray(np.broadcast_to(a, shape), dtype=np.float32)
    qd = f32(q_decay[:, :, None], (RET_HEADS, rows, RET_DK))
    kd = f32(k_decay[:, :, None], (RET_HEADS, rows, RET_DK))
    cd = f32(chunk_decay[:, None, None], (RET_HEADS, 1, RET_DV))
    return decay.astype(np.float32), qd, kd, cd


def _group_norm_gate(o, gn, g):
    mu = jnp.mean(o, axis=-1, keepdims=True)
    d = o - mu
    var = jnp.mean(d * d, axis=-1, keepdims=True)
    return d * lax.rsqrt(var + NORM_EPS) * gn * _silu(g.astype(F32))


def _ret_prompt_kernel(q_ref, k_ref, v_ref, g_ref, cos_ref, sin_ref, dec_ref, qd_ref, kd_ref, cd_ref, gn_ref,
                       y_ref, st_ref):
    @pl.when(pl.program_id(0) == 0)
    def _():
        st_ref[...] = jnp.zeros_like(st_ref)

    c = dec_ref.shape[1]
    for i in range(q_ref.shape[0] // c):
        rs = slice(i * c, (i + 1) * c)
        cos = cos_ref[rs, :]
        sin = sin_ref[rs, :]
        for h in range(RET_HEADS):
            ks = slice(h * RET_DK, (h + 1) * RET_DK)
            vs = slice(h * RET_DV, (h + 1) * RET_DV)
            q = _rot_full(q_ref[rs, ks].astype(F32), cos, sin)
            k = _rot_full(k_ref[rs, ks].astype(F32) * (RET_DK ** -0.5), cos, sin)
            v = v_ref[rs, vs].astype(BF16)
            st = st_ref[h]
            s = _dot_nt(q.astype(BF16), k.astype(BF16)) * dec_ref[h]
            lhs = jnp.concatenate([s.astype(BF16), (q * qd_ref[h]).astype(BF16)], axis=1)
            o = _dot(lhs, jnp.concatenate([v, st.astype(BF16)], axis=0))
            st_ref[h] = cd_ref[h] * st + _dot((k * kd_ref[h]).T.astype(BF16), v)
            y_ref[rs, vs] = _group_norm_gate(o, gn_ref[:, vs], g_ref[rs, vs]).astype(y_ref.dtype)


def _ret_prompt(proj, cos, sin, ret_gn):
    t = proj.shape[0]
    dec, qd, kd, cd = _ret_tables(RET_PROMPT_CHUNK, 1)
    c = RET_STEP_CHUNKS * RET_PROMPT_CHUNK
    assert t % c == 0
    qw = RET_HEADS * RET_DK
    vw = RET_HEADS * RET_DV
    full3 = lambda n: (0, 0, 0)
    return pl.pallas_call(
        _ret_prompt_kernel,
        grid=(t // c,),
        in_specs=[pl.BlockSpec((c, qw), lambda n: (n, E_AQ // qw)),
                  pl.BlockSpec((c, qw), lambda n: (n, E_AK // qw)),
                  pl.BlockSpec((c, vw), lambda n: (n, E_AV // vw)),
                  pl.BlockSpec((c, vw), lambda n: (n, E_AG // vw)),
                  pl.BlockSpec((c, RET_DK), lambda n: (n, 0)),
                  pl.BlockSpec((c, RET_DK), lambda n: (n, 0)),
                  pl.BlockSpec(dec.shape, full3),
                  pl.BlockSpec(qd.shape, full3),
                  pl.BlockSpec(kd.shape, full3),
                  pl.BlockSpec(cd.shape, full3),
                  pl.BlockSpec((1, vw), lambda n: (0, 0))],
        out_specs=[pl.BlockSpec((c, vw), lambda n: (n, 0)),
                   pl.BlockSpec((RET_HEADS, RET_DK, RET_DV), full3)],
        out_shape=[jax.ShapeDtypeStruct((t, vw), BF16),
                   jax.ShapeDtypeStruct((RET_HEADS, RET_DK, RET_DV), F32)],
        compiler_params=_params("arbitrary"),
        name="ret_prompt",
    )(proj, proj, proj, proj, cos, sin, dec, qd, kd, cd, ret_gn.reshape(1, vw))


def _ret_sample_kernel(q_ref, k_ref, v_ref, g_ref, cos_ref, sin_ref, dec_ref, qd_ref, kd_ref, cd_ref, gn_ref,
                       st_ref, y_ref, sto_ref, cross_ref, qs_ref, kt_ref, *, pad):
    b = pl.program_id(0)
    nb = pl.num_programs(0)
    win = RET_CHUNK

    @pl.when(b == 0)
    def _():
        cross_ref[...] = jnp.zeros_like(cross_ref)
        cos = cos_ref[...]
        sin = sin_ref[...]
        for h in range(RET_HEADS):
            ks = slice(h * RET_DK, (h + 1) * RET_DK)
            qs_ref[:, ks] = _rot_full(q_ref[:, ks], cos, sin) * qd_ref[h]
            kt_ref[h] = (_rot_full(k_ref[:, ks] * (RET_DK ** -0.5), cos, sin) * kd_ref[h]).T

    for i in range(st_ref.shape[0]):
        sq = b * st_ref.shape[0] + i
        w0 = pl.multiple_of(jnp.right_shift(sq * pad, win.bit_length() - 1) * win, win)
        wrows = pl.ds(w0, win)
        rows = lax.broadcasted_iota(jnp.int32, (win, RET_DK), 0) + w0
        cols = lax.broadcasted_iota(jnp.int32, (RET_DK, win), 1) + w0
        mine_r = (rows >= sq * pad) & (rows < (sq + 1) * pad)
        mine_c = (cols >= sq * pad) & (cols < (sq + 1) * pad)
        for h in range(RET_HEADS):
            ks = slice(h * RET_DK, (h + 1) * RET_DK)
            vs = slice(h * RET_DV, (h + 1) * RET_DV)
            st = st_ref[i, h]
            qm = jnp.where(mine_r, qs_ref[wrows, ks], 0.0)
            kmt = jnp.where(mine_c, kt_ref[h, :, wrows], 0.0)
            cross_ref[wrows, vs] += _dot(qm.astype(BF16), st.astype(BF16))
            sto_ref[i, h] = cd_ref[h] * st + _dot(kmt.astype(BF16), v_ref[wrows, vs].astype(BF16))

    @pl.when(b == nb - 1)
    def _():
        cos_all = cos_ref[...]
        sin_all = sin_ref[...]
        for h in range(RET_HEADS):
            ks = slice(h * RET_DK, (h + 1) * RET_DK)
            vs = slice(h * RET_DV, (h + 1) * RET_DV)
            q = _rot_full(q_ref[:, ks], cos_all, sin_all)
            k = _rot_full(k_ref[:, ks] * (RET_DK ** -0.5), cos_all, sin_all)
            v = v_ref[:, vs].astype(BF16)
            s = _dot_nt(q.astype(BF16), k.astype(BF16)) * dec_ref[h]
            o = _dot(s.astype(BF16), v) + cross_ref[:, vs]
            y_ref[:, vs] = _group_norm_gate(o, gn_ref[:, vs], g_ref[:, vs]).astype(y_ref.dtype)


def _ret_sample(proj, state, cos, sin, ret_gn, nseq, seq, pad):
    rows = nseq * pad
    c = math.gcd(seq, RET_CHUNK)
    assert c == seq, "sample chunk must cover the new tokens"
    assert rows % RET_CHUNK == 0 and RET_CHUNK % pad == 0
    dec, qd, kd, cd = _ret_tables(c, nseq, pad)
    qw = RET_HEADS * RET_DK
    vw = RET_HEADS * RET_DV
    full3 = lambda b: (0, 0, 0)
    n = RET_STEP_SEQS
    assert nseq % n == 0
    st_spec = pl.BlockSpec((n, RET_HEADS, RET_DK, RET_DV), lambda b: (b, 0, 0, 0))
    return pl.pallas_call(
        functools.partial(_ret_sample_kernel, pad=pad),
        grid=(nseq // n,),
        in_specs=[pl.BlockSpec((rows, qw), lambda b: (0, E_AQ // qw)),
                  pl.BlockSpec((rows, qw), lambda b: (0, E_AK // qw)),
                  pl.BlockSpec((rows, vw), lambda b: (0, E_AV // vw)),
                  pl.BlockSpec((rows, vw), lambda b: (0, E_AG // vw)),
                  pl.BlockSpec((rows, RET_DK), lambda b: (0, 0)),
                  pl.BlockSpec((rows, RET_DK), lambda b: (0, 0)),
                  pl.BlockSpec(dec.shape, full3),
                  pl.BlockSpec(qd.shape, full3),
                  pl.BlockSpec(kd.shape, full3),
                  pl.BlockSpec(cd.shape, full3),
                  pl.BlockSpec((1, vw), lambda b: (0, 0)),
                  st_spec],
        out_specs=[pl.BlockSpec((rows, vw), lambda b: (0, 0)), st_spec],
        out_shape=[jax.ShapeDtypeStruct((rows, vw), F32),
                   jax.ShapeDtypeStruct(state.shape, F32)],
        scratch_shapes=[pltpu.VMEM((rows, vw), F32),
                        pltpu.VMEM((rows, qw), F32),
                        pltpu.VMEM((RET_HEADS, RET_DK, rows), F32)],
        compiler_params=_params("arbitrary"),
        name="ret_sample",
    )(proj, proj, proj, proj, cos, sin, dec, qd, kd, cd, ret_gn.reshape(1, vw), state)


def _swa_prompt_kernel(q_ref, kc_ref, vc_ref, g_ref, y_ref, q_s, k_s, v_s, m_s, l_s, acc_s):
    t = pl.program_id(1)
    tb = SWA_TILE
    blk = SWA_BLOCK
    hd = SWA_HD

    @pl.when(t == 0)
    def _():
        k_s[0:tb, :] = jnp.zeros((tb, hd), F32)
        v_s[0:tb, :] = jnp.zeros((tb, hd), F32)

    @pl.when(t > 0)
    def _():
        k_s[0:tb, :] = k_s[tb:2 * tb, :]
        v_s[0:tb, :] = v_s[tb:2 * tb, :]

    q_s[...] = q_ref[...].astype(F32)
    k_s[tb:2 * tb, :] = kc_ref[...].astype(F32)
    v_s[tb:2 * tb, :] = vc_ref[...].astype(F32)

    def ds(start, size, stride):
        return pl.ds(start, size) if stride == 1 else pl.ds(start, size, stride=stride)

    def band(g):
        qi = lax.broadcasted_iota(jnp.int32, (g * blk, (g + 1) * blk), 0)
        kj = lax.broadcasted_iota(jnp.int32, (g * blk, (g + 1) * blk), 1)
        ok = (kj >= qi) & (kj <= qi + blk)
        return ok, ok & ((kj >= blk) | (t > 0))

    for bi, (window, r) in enumerate(SWA_PATTERNS):
        assert window // r == blk
        per_class = tb // (blk * r)
        g = SWA_GROUP if per_class % SWA_GROUP == 0 else 1
        ok, ok_first = band(g)
        ones = jnp.ones(((g + 1) * blk, hd), BF16)
        for c in range(r):
            for n in range(0, per_class, g):
                qs = n * blk * r + c
                ks = qs + tb - blk * r
                qrows = ds(qs, g * blk, r)
                krows = ds(ks, (g + 1) * blk, r)
                s = _dot_nt(q_s[qrows, :].astype(BF16), k_s[krows, :].astype(BF16))
                s = jnp.where(ok_first if n == 0 else ok, s, NEG_INF)
                m = jnp.max(s, axis=-1, keepdims=True)
                p = jnp.exp2(s - m).astype(BF16)
                pv = _dot(p, jnp.concatenate([v_s[krows, :].astype(BF16), ones], axis=1))
                m_s[bi, qrows, :] = jnp.broadcast_to(m, (g * blk, hd))
                l_s[bi, qrows, :] = pv[:, hd:]
                acc_s[bi, qrows, :] = pv[:, :hd]

    nbr = len(SWA_PATTERNS)
    ms = [m_s[i] for i in range(nbr)]
    mmax = functools.reduce(jnp.maximum, ms)
    wts = [jnp.exp2(mi - mmax) for mi in ms]
    num = sum(wts[i] * acc_s[i] for i in range(nbr))
    den = sum(wts[i] * l_s[i] for i in range(nbr))
    y_ref[...] = (num / den * _silu(g_ref[...].astype(F32))).astype(y_ref.dtype)


def _swa_prompt(proj):
    t = proj.shape[0]
    tb = SWA_TILE
    assert t % tb == 0
    hd = SWA_HD
    nbr = len(SWA_PATTERNS)
    cur = lambda off: pl.BlockSpec((tb, hd), lambda h, i: (i, off // hd + h))
    return pl.pallas_call(
        _swa_prompt_kernel,
        grid=(SWA_HEADS, t // tb),
        in_specs=[cur(E_BQ), cur(E_BK), cur(E_BV), cur(E_BG)],
        out_specs=pl.BlockSpec((tb, hd), lambda h, i: (i, h)),
        out_shape=jax.ShapeDtypeStruct((t, SWA_HEADS * hd), BF16),
        scratch_shapes=[pltpu.VMEM((tb, hd), F32),
                        pltpu.VMEM((2 * tb, hd), F32),
                        pltpu.VMEM((2 * tb, hd), F32),
                        pltpu.VMEM((nbr, tb, hd), F32),
                        pltpu.VMEM((nbr, tb, hd), F32),
                        pltpu.VMEM((nbr, tb, hd), F32)],
        compiler_params=_params("arbitrary", "arbitrary"),
        name="swa_prompt",
    )(proj, proj, proj, proj)


def _swa_sample_kernel(q_ref, k_ref, v_ref, g_ref, cos_ref, sin_ref, ka_ref, kb_ref, va_ref, vb_ref,
                       y_ref, ko_ref, kn_s, vn_s, ba_s, bb_s, bn_s, *, seq):
    pad = SAMPLE_PAD
    nh = SWA_HEADS
    hd = SWA_HD
    nr = nh * pad
    nj, nrh, _ = kb_ref.shape[1:]
    na = ka_ref.shape[1]
    nb = nj * nrh
    nn = kn_s.shape[0]

    @pl.when(pl.program_id(0) == 0)
    def _():
        (w1, d1), (w2, d2), (w3, d3) = SWA_PATTERNS

        def rows_cols(n):
            r = lax.broadcasted_iota(jnp.int32, (nr, n), 0)
            c = lax.broadcasted_iota(jnp.int32, (nr, n), 1)
            return _div(r, pad), _mod(r, pad), c

        def log2_mult(mult):
            return jnp.where(mult > 0, jnp.log2(jnp.maximum(mult, 1.0)), NEG_INF)

        hq, tq, c = rows_cols(na)
        dist = na // nh + tq - _div(c, nh)
        mult_a = ((dist <= w1).astype(F32) + ((_mod(dist, d2) == 0) & (dist <= w2)).astype(F32)
                  + ((_mod(dist, d3) == 0) & (dist <= w3)).astype(F32))
        ba_s[...] = log2_mult(jnp.where((hq == _mod(c, nh)) & (tq < seq), mult_a, 0.0))
        hq, tq, c = rows_cols(nb)
        bb_s[...] = log2_mult(jnp.where((hq == _mod(c, nh)) & (tq == _mod(_div(c, nh), nrh // nh)), 1.0, 0.0))
        hq, tq, c = rows_cols(nn)
        tk = _mod(c, pad)
        mult_n = (tk <= tq).astype(F32) + 2.0 * (tk == tq).astype(F32)
        bn_s[...] = log2_mult(jnp.where((hq == _div(c, pad)) & (tq < seq) & (tk < seq), mult_n, 0.0))
        kn_s[...] = jnp.zeros_like(kn_s)
        vn_s[...] = jnp.zeros_like(vn_s)

    cos = cos_ref[...]
    sin = sin_ref[...]
    q8 = _rot_partial(q_ref[0] * (SWA_HD ** -0.5 * LOG2E), cos, sin)
    k8 = _rot_partial(k_ref[0], cos, sin)
    ko_ref[0] = k8

    def head_rows(x):
        return jnp.concatenate([x[:, h * hd:(h + 1) * hd] for h in range(nh)], axis=0)

    qrows = head_rows(q8).astype(BF16)
    kn_s[0:nr, :] = head_rows(k8).astype(BF16)
    vn_s[0:nr, :] = head_rows(v_ref[0]).astype(BF16)

    kb = kb_ref[0].reshape(nb, hd)
    vb = vb_ref[0].reshape(nb, hd)
    s_a = _dot_nt(qrows, ka_ref[0].astype(BF16)) + ba_s[...]
    s_b = _dot_nt(qrows, kb.astype(BF16)) + bb_s[...]
    s_n = _dot_nt(qrows, kn_s[...]) + bn_s[...]
    rowmax = lambda s: jnp.max(s, axis=-1, keepdims=True)
    m = jnp.maximum(jnp.maximum(rowmax(s_a), rowmax(s_b)), rowmax(s_n))
    p_a, p_b, p_n = jnp.exp2(s_a - m), jnp.exp2(s_b - m), jnp.exp2(s_n - m)
    rowsum = lambda p: jnp.sum(p, axis=-1, keepdims=True)
    o = (_dot(p_a.astype(BF16), va_ref[0].astype(BF16)) + _dot(p_b.astype(BF16), vb.astype(BF16))
         + _dot(p_n.astype(BF16), vn_s[...])) / (rowsum(p_a) + rowsum(p_b) + rowsum(p_n))
    y = jnp.concatenate([o[h * pad:(h + 1) * pad] for h in range(nh)], axis=1)
    y_ref[0] = y * _silu(g_ref[0])


def _swa_sample(proj_pad, cache_k, cache_v, cos, sin, seq):
    nseq, pad, _ = proj_pad.shape
    buf = cache_k.shape[1]
    nh, hd = SWA_HEADS, SWA_HD
    w = nh * hd
    (w1, d1), (w2, d2), (w3, d3) = SWA_PATTERNS
    assert buf == w3 and d1 == 1 and seq <= d2 and w1 <= w2 and buf % w2 == 0 and buf % d3 == 0
    assert nh * pad <= SWA_BLOCK and (seq * nh) % 8 == 0
    flat = lambda a: a.reshape(nseq, buf * nh, hd)
    strided = lambda a: a.reshape(nseq, buf // d3, d3 * nh, hd)
    col = lambda off: pl.BlockSpec((1, pad, w), lambda b: (b, 0, off // w))
    tail = pl.BlockSpec((1, w2 * nh, hd), lambda b: (b, buf // w2 - 1, 0))
    resid = pl.BlockSpec((1, (buf - w2) // d3, seq * nh, hd), lambda b: (b, 0, 0, 0))
    tab = pl.BlockSpec((pad, w), lambda b: (0, 0))
    out = pl.BlockSpec((1, pad, w), lambda b: (b, 0, 0))
    return pl.pallas_call(
        functools.partial(_swa_sample_kernel, seq=seq),
        grid=(nseq,),
        in_specs=[col(E_BQ), col(E_BK), col(E_BV), col(E_BG), tab, tab, tail, resid, tail, resid],
        out_specs=[out, out],
        out_shape=[jax.ShapeDtypeStruct((nseq, pad, w), F32),
                   jax.ShapeDtypeStruct((nseq, pad, w), F32)],
        scratch_shapes=[pltpu.VMEM((SWA_BLOCK, hd), BF16), pltpu.VMEM((SWA_BLOCK, hd), BF16),
                        pltpu.VMEM((nh * pad, w2 * nh), F32),
                        pltpu.VMEM((nh * pad, (buf - w2) // d3 * seq * nh), F32),
                        pltpu.VMEM((nh * pad, SWA_BLOCK), F32)],
        compiler_params=_params("arbitrary"),
        name="swa_sample",
    )(proj_pad, proj_pad, proj_pad, proj_pad, cos, sin,
      flat(cache_k), strided(cache_k), flat(cache_v), strided(cache_v))


def _mem_attend(q, g, k, v):
    ones = jnp.ones((MEM_LEN, MEM_HD), BF16)
    outs = []
    for h in range(MEM_HEADS):
        hs = slice(h * MEM_HD, (h + 1) * MEM_HD)
        qh = (q[:, hs].astype(F32) * (MEM_HD ** -0.5 * LOG2E)).astype(BF16)
        s = _dot_nt(qh, k[:, hs].astype(BF16))
        m = jnp.max(s, axis=-1, keepdims=True)
        p = jnp.exp2(s - m).astype(BF16)
        pv = _dot(p, jnp.concatenate([v[:, hs].astype(BF16), ones], axis=1))
        o = pv[:, :MEM_HD] / pv[:, MEM_HD:]
        outs.append((o * _silu(g[:, hs].astype(F32))).astype(BF16))
    return jnp.concatenate(outs, axis=1)


def _mem_attn_rows_kernel(q_ref, g_ref, k_ref, v_ref, o_ref, bias_s):
    nseq, pad, _ = q_ref.shape
    nh, hd = MEM_HEADS, MEM_HD

    @pl.when(pl.program_id(0) == 0)
    def _():
        r = lax.broadcasted_iota(jnp.int32, bias_s.shape, 0)
        c = lax.broadcasted_iota(jnp.int32, bias_s.shape, 1)
        bias_s[...] = jnp.where(_div(r, pad) == _mod(c, nh), 0.0, NEG_INF)

    for i in range(nseq):
        q = q_ref[i] * (MEM_HD ** -0.5 * LOG2E)
        qrows = jnp.concatenate([q[:, h * hd:(h + 1) * hd] for h in range(nh)], axis=0).astype(BF16)
        s = _dot_nt(qrows, k_ref[i].astype(BF16)) + bias_s[...]
        m = jnp.max(s, axis=-1, keepdims=True)
        p = jnp.exp2(s - m)
        o = _dot(p.astype(BF16), v_ref[i].astype(BF16)) / jnp.sum(p, axis=-1, keepdims=True)
        y = jnp.concatenate([o[h * pad:(h + 1) * pad] for h in range(nh)], axis=1)
        o_ref[i] = y * _silu(g_ref[i])


def _mem_attn_rows(proj_pad, q_off, g_off, mem_k, mem_v, layer):
    b, pad, _ = proj_pad.shape
    w = MEM_W
    rows = MEM_LEN * MEM_HEADS
    n = MEM_ROWS_STEP_SEQS
    assert b % n == 0
    kv = pl.BlockSpec((n, rows, MEM_HD), lambda i: (layer * (b // n) + i, 0, 0))
    return pl.pallas_call(
        _mem_attn_rows_kernel,
        grid=(b // n,),
        in_specs=[pl.BlockSpec((n, pad, w), lambda i: (i, 0, q_off // w)),
                  pl.BlockSpec((n, pad, w), lambda i: (i, 0, g_off // w)),
                  kv, kv],
        out_specs=pl.BlockSpec((n, pad, w), lambda i: (i, 0, 0)),
        out_shape=jax.ShapeDtypeStruct((b, pad, w), F32),
        scratch_shapes=[pltpu.VMEM((MEM_HEADS * pad, rows), F32)],
        compiler_params=_params("arbitrary"),
        name="mem_attn_rows",
    )(proj_pad, proj_pad, mem_k.reshape(-1, rows, MEM_HD), mem_v.reshape(-1, rows, MEM_HD))


def _gmlp_kernel(u_ref, v_ref, z_ref, ln_ref, w_ref, b_ref, y_ref, vo_ref, wc_ref, *, activated):
    c = GMLP_CHUNK
    gelu, silu = (lambda a: a, lambda a: a) if activated else (_gelu, _silu)

    @pl.when(pl.program_id(0) == 0)
    def _():
        causal = (lax.broadcasted_iota(jnp.int32, (c, c), 0) >= lax.broadcasted_iota(jnp.int32, (c, c), 1))
        for g in range(GMLP_GROUPS):
            wc_ref[g] = jnp.where(causal, w_ref[g], 0.0).astype(BF16)

    nchunks = u_ref.shape[0] // c
    keep_all = vo_ref.shape[0] == u_ref.shape[0]
    for i in range(nchunks):
        rs = slice(i * c, (i + 1) * c)
        v = gelu(v_ref[rs, :].astype(F32))
        mu = jnp.mean(v, axis=-1, keepdims=True)
        d = v - mu
        var = jnp.mean(d * d, axis=-1, keepdims=True)
        vn = d * lax.rsqrt(var + NORM_EPS) * ln_ref[...]
        if keep_all:
            vo_ref[rs, :] = vn
        elif i == nchunks - 1:
            vo_ref[...] = vn
        for g in range(GMLP_GROUPS):
            gs = slice(g * GMLP_GROUP, (g + 1) * GMLP_GROUP)
            mixed = _dot(wc_ref[g], vn[:, gs].astype(BF16)) + b_ref[:, gs]
            y_ref[rs, gs] = (gelu(u_ref[rs, gs].astype(F32)) * mixed
                             * silu(z_ref[rs, gs].astype(F32))).astype(y_ref.dtype)


def _gmlp(proj, gmlp_ln, w_mix, bias_rows, keep_all_v, activated):
    t = proj.shape[0]
    c = GMLP_CHUNK
    w = GMLP_WIDTH
    rows = GMLP_STEP_CHUNKS * c
    assert t % rows == 0
    col = lambda off: pl.BlockSpec((rows, w), lambda n: (n, off // w))
    return pl.pallas_call(
        functools.partial(_gmlp_kernel, activated=activated),
        grid=(t // rows,),
        in_specs=[col(O_U), col(O_V), col(O_Z),
                  pl.BlockSpec((1, w), lambda n: (0, 0)),
                  pl.BlockSpec(w_mix.shape, lambda n: (0, 0, 0)),
                  pl.BlockSpec((c, w), lambda n: (0, 0))],
        out_specs=[pl.BlockSpec((rows, w), lambda n: (n, 0)),
                   pl.BlockSpec((rows, w), lambda n: (n, 0)) if keep_all_v
                   else pl.BlockSpec((c, w), lambda n: (0, 0))],
        out_shape=[jax.ShapeDtypeStruct((t, w), BF16),
                   jax.ShapeDtypeStruct((t if keep_all_v else c, w), F32)],
        scratch_shapes=[pltpu.VMEM(w_mix.shape, BF16)],
        compiler_params=_params("arbitrary"),
        name="gmlp",
    )(proj, proj, proj, gmlp_ln.reshape(1, w), w_mix, bias_rows)


def _out_proj_kernel(*refs, n_in, mem):
    a_refs, w_refs = refs[:n_in], refs[n_in:2 * n_in]
    refs = refs[2 * n_in:]
    if mem:
        (mq_ref, mg_ref, mk_ref, mv_ref, wm_ref), refs = refs[:5], refs[5:]
    g_ref, x_ref, o_ref = refs
    sub = min(OUT_PROJ_SUB_ROWS, o_ref.shape[0])
    for r0 in range(0, o_ref.shape[0], sub):
        rs = slice(r0, r0 + sub)
        acc = _dot(a_refs[0][rs, :].astype(BF16), w_refs[0][...])
        for a_ref, w_ref in zip(a_refs[1:], w_refs[1:]):
            acc = acc + _dot(a_ref[rs, :].astype(BF16), w_ref[...])
        if mem:
            acc = acc + _dot(_mem_attend(mq_ref[rs, :], mg_ref[rs, :], mk_ref[...], mv_ref[...]), wm_ref[...])
        ms = jnp.mean(acc * acc, axis=-1, keepdims=True)
        o_ref[rs, :] = x_ref[rs, :] + acc * lax.rsqrt(ms + NORM_EPS) * g_ref[...]


def _mem_specs(proj, q_off, g_off, memkv, tm, w_rows, d):
    w = MEM_W
    const = lambda shape, idx: pl.BlockSpec(shape, lambda i: idx, pipeline_mode=pl.Buffered(1))
    specs = [pl.BlockSpec((tm, w), lambda i: (i, q_off // w)), pl.BlockSpec((tm, w), lambda i: (i, g_off // w)),
             const((MEM_LEN, w), (0, 0)), const((MEM_LEN, w), (0, 1)), const((w, d), ((w_rows - w) // w, 0))]
    return specs, [proj, proj, memkv, memkv]


def _out_proj(acts, w, g, x, tm, mem=None):
    m, d = x.shape
    n_in = len(acts)
    w_specs, off = [], 0
    for a in acts:
        k = a.shape[1]
        assert off % k == 0
        w_specs.append(pl.BlockSpec((k, d), functools.partial(lambda i, blk: (blk, 0), blk=off // k),
                                    pipeline_mode=pl.Buffered(1)))
        off += k
    mem_specs, mem_args = [], []
    if mem is not None:
        mem_specs, mem_args = _mem_specs(*mem, tm, w.shape[0], d)
        mem_args.append(w)
        off += MEM_W
    assert off == w.shape[0]
    return pl.pallas_call(
        functools.partial(_out_proj_kernel, n_in=n_in, mem=mem is not None),
        grid=(m // tm,),
        in_specs=([pl.BlockSpec((tm, a.shape[1]), lambda i: (i, 0)) for a in acts]
                  + w_specs + mem_specs
                  + [pl.BlockSpec((1, d), lambda i: (0, 0)),
                     pl.BlockSpec((tm, d), lambda i: (i, 0))]),
        out_specs=pl.BlockSpec((tm, d), lambda i: (i, 0)),
        out_shape=jax.ShapeDtypeStruct((m, d), F32),
        compiler_params=_params("arbitrary"),
        name="out_proj",
    )(*acts, *([w] * n_in), *mem_args, g.reshape(1, d), x)


def _gmlp_out_proj_kernel(u_ref, v_ref, z_ref, ln_ref, w_ref, b_ref, wy_ref, mq_ref, mg_ref, mk_ref, mv_ref, wm_ref,
                          g_ref, x_ref, o_ref, vo_ref, wc_ref, yc_ref):
    c = GMLP_CHUNK
    sub = yc_ref.shape[0]
    tm = o_ref.shape[0]

    @pl.when(pl.program_id(0) == 0)
    def _():
        causal = (lax.broadcasted_iota(jnp.int32, (c, c), 0) >= lax.broadcasted_iota(jnp.int32, (c, c), 1))
        for g in range(GMLP_GROUPS):
            wc_ref[g] = jnp.where(causal, w_ref[g], 0.0).astype(BF16)

    for r0 in range(0, tm, sub):
        for c0 in range(r0, r0 + sub, c):
            rs = slice(c0, c0 + c)
            v = v_ref[rs, :].astype(F32)
            mu = jnp.mean(v, axis=-1, keepdims=True)
            d = v - mu
            var = jnp.mean(d * d, axis=-1, keepdims=True)
            vn = d * lax.rsqrt(var + NORM_EPS) * ln_ref[...]
            if c0 + c == tm:
                vo_ref[...] = vn
            for g in range(GMLP_GROUPS):
                gs = slice(g * GMLP_GROUP, (g + 1) * GMLP_GROUP)
                mixed = _dot(wc_ref[g], vn[:, gs].astype(BF16)) + b_ref[:, gs]
                yc_ref[c0 - r0:c0 - r0 + c, gs] = (u_ref[rs, gs].astype(F32) * mixed
                                                   * z_ref[rs, gs].astype(F32)).astype(BF16)
        rs = slice(r0, r0 + sub)
        ym = _mem_attend(mq_ref[rs, :], mg_ref[rs, :], mk_ref[...], mv_ref[...])
        acc = _dot(yc_ref[...], wy_ref[...]) + _dot(ym, wm_ref[...])
        ms = jnp.mean(acc * acc, axis=-1, keepdims=True)
        o_ref[rs, :] = x_ref[rs, :] + acc * lax.rsqrt(ms + NORM_EPS) * g_ref[...]


def _gmlp_out_proj(proj, memkv, gmlp_ln, w_mix, bias_rows, w, g, x, tm):
    m, d = x.shape
    c = GMLP_CHUNK
    gw = GMLP_WIDTH
    sub = min(OUT_PROJ_SUB_ROWS, tm)
    assert m % tm == 0 and tm % sub == 0 and sub % c == 0 and w.shape[0] == gw + MEM_W
    col = lambda off: pl.BlockSpec((tm, gw), lambda i: (i, off // gw))
    mem_specs, mem_args = _mem_specs(proj, O_MQ, O_MG, memkv, tm, w.shape[0], d)
    return pl.pallas_call(
        _gmlp_out_proj_kernel,
        grid=(m // tm,),
        in_specs=[col(O_U), col(O_V), col(O_Z),
                  pl.BlockSpec((1, gw), lambda i: (0, 0)),
                  pl.BlockSpec(w_mix.shape, lambda i: (0, 0, 0)),
                  pl.BlockSpec((c, gw), lambda i: (0, 0)),
                  pl.BlockSpec((gw, d), lambda i: (0, 0), pipeline_mode=pl.Buffered(1))]
                 + mem_specs
                 + [pl.BlockSpec((1, d), lambda i: (0, 0)),
                    pl.BlockSpec((tm, d), lambda i: (i, 0))],
        out_specs=[pl.BlockSpec((tm, d), lambda i: (i, 0)),
                   pl.BlockSpec((c, gw), lambda i: (0, 0))],
        out_shape=[jax.ShapeDtypeStruct((m, d), F32),
                   jax.ShapeDtypeStruct((c, gw), F32)],
        scratch_shapes=[pltpu.VMEM(w_mix.shape, BF16), pltpu.VMEM((sub, gw), BF16)],
        compiler_params=_params("arbitrary"),
        name="gmlp_out_proj",
    )(proj, proj, proj, gmlp_ln.reshape(1, gw), w_mix, bias_rows, w, *mem_args, w, g.reshape(1, d), x)


def _pad_steps(a, pad):
    nseq, seq, n = a.shape
    return jnp.pad(a, ((0, 0), (0, pad - seq), (0, 0))).reshape(nseq * pad, n)


def kernel(x_prompt, x_sample, state_ret, cache_swa_k, cache_swa_v, cache_mem_k, cache_mem_v, mem_prompt,
           pre_norm, post_norm, mem_norm, w_mem_k, w_mem_v, w_in_even, ret_gn, w_out_even,
           w_in_odd, gmlp_ln, w_spatial, b_spatial, w_out_odd):
    bp, tp, d = x_prompt.shape
    nseq, seq, _ = x_sample.shape
    pad = SAMPLE_PAD
    rows_s = nseq * pad
    c = GMLP_CHUNK
    assert bp == 1 and seq <= pad and rows_s % c == 0 and c % pad == 0
    hp = x_prompt.reshape(tp, d)
    hs = _pad_steps(x_sample, pad)
    mem = mem_prompt.reshape(MEM_LEN, d)
    tm_p = 1024 if tp % 1024 == 0 else 512
    unpad = lambda a: a.reshape(nseq, pad, -1)[:, :seq]

    pos_pad = np.concatenate([PAST_LEN + np.arange(seq), np.zeros((pad - seq,), np.int64)])
    ret_cos_p, ret_sin_p = _rot_tables_range(tp, RET_DK, RET_THETA, RET_DK)
    ret_cos_s, ret_sin_s = _rot_tables(np.tile(pos_pad, nseq), RET_DK, RET_THETA, RET_DK)
    swa_cos_p, swa_sin_p = _rot_tables_range(tp, ROPE_DIMS, ROPE_THETA, SWA_HD)
    swa_cos_s, swa_sin_s = _rot_tables(pos_pad, ROPE_DIMS, ROPE_THETA, SWA_HD)
    swa_cos_s = np.tile(swa_cos_s, (1, SWA_HEADS))
    swa_sin_s = np.tile(swa_sin_s, (1, SWA_HEADS))

    w_mem0 = jnp.concatenate([w_mem_k[0], w_mem_v[0]], axis=1).astype(BF16)
    memkv0 = _norm_matmul(mem, mem_norm[0], w_mem0, MEM_LEN, 2 * MEM_W)

    proj_s, w_in0 = _norm_matmul_cast(hs, pre_norm[0], w_in_even[0], 1024)
    proj_s3 = proj_s.reshape(nseq, pad, EVEN_IN)
    ya_s, st_s = _ret_sample(proj_s, state_ret[0], ret_cos_s, ret_sin_s, ret_gn[0], nseq, seq, pad)
    yb_s, sk = _swa_sample(proj_s3, cache_swa_k[0], cache_swa_v[0], swa_cos_s, swa_sin_s, seq)
    ym_s = _mem_attn_rows(proj_s3, E_MQ, E_MG, cache_mem_k, cache_mem_v, 0)
    sk = sk[:, :seq].reshape(nseq, seq, SWA_HEADS, SWA_HD)
    sv = unpad(proj_s[:, E_BV:E_BV + SWA_HEADS * SWA_HD]).reshape(nseq, seq, SWA_HEADS, SWA_HD)

    buf_p = min(SWA_MAX_WINDOW, tp)
    proj_p, kv_last, w_out0, w_in1, w_out1 = _norm_matmul(
        hp, pre_norm[0], w_in0, tm_p, 1024, BF16, (tp - buf_p, E_BK, E_BG - E_BK),
        casts=[(w_out_even[0], 256), (w_in_odd[0], 1024), (w_out_odd[0], 256)],
        rot=(swa_cos_p, swa_sin_p, E_BQ, E_BK, SWA_HD ** -0.5 * LOG2E))
    hs1 = _out_proj([ya_s, yb_s.reshape(rows_s, -1), ym_s.reshape(rows_s, -1)], w_out0, post_norm[0], hs, rows_s)
    ya_p, st_p = _ret_prompt(proj_p, ret_cos_p, ret_sin_p, ret_gn[0])
    yb_p = _swa_prompt(proj_p)
    pk = kv_last[:, :E_BV - E_BK]
    hp1 = _out_proj([ya_p, yb_p], w_out0, post_norm[0], hp, 512, mem=(proj_p, E_MQ, E_MG, memkv0))
    pv = kv_last[:, E_BV - E_BK:]

    w_mem1 = jnp.concatenate([w_mem_k[1], w_mem_v[1]], axis=1).astype(BF16)
    memkv1 = _norm_matmul(mem, mem_norm[1], w_mem1, MEM_LEN, 2 * MEM_W)
    bias_p = jnp.repeat(b_spatial[0].T, GMLP_GROUP, axis=1)
    blockdiag = lambda wg: jnp.kron(jnp.eye(c // pad, dtype=F32),
                                    jnp.pad(wg[:seq, :seq], ((0, pad - seq), (0, pad - seq))))
    w_mix_s = jax.vmap(blockdiag)(w_spatial[0])
    bias_s = jnp.tile(jnp.pad(bias_p[:seq], ((0, pad - seq), (0, 0))), (c // pad, 1))

    proj_s1 = _norm_matmul(hs1, pre_norm[1], w_in1, rows_s, 1024)
    yc_s, gv_s = _gmlp(proj_s1, gmlp_ln[0], w_mix_s, bias_s, True, False)
    ym_s1 = _mem_attn_rows(proj_s1.reshape(nseq, pad, ODD_IN), O_MQ, O_MG, cache_mem_k, cache_mem_v, 1)
    hs2 = _out_proj([yc_s, ym_s1.reshape(rows_s, -1)], w_out1, post_norm[1], hs1, rows_s)

    proj_p1 = _norm_matmul(hp1, pre_norm[1], w_in1, tm_p, 1024, BF16,
                           acts=[(O_U, O_Z, "gelu"), (O_Z, O_MQ, "silu")])
    hp2, gv_p = _gmlp_out_proj(proj_p1, memkv1, gmlp_ln[0], w_spatial[0], bias_p, w_out1, post_norm[1], hp1, 512)

    memkv = jnp.stack([memkv0, memkv1])
    p_mk = memkv[:, :, :MEM_W].reshape(2, bp, MEM_LEN, MEM_HEADS, MEM_HD)
    p_mv = memkv[:, :, MEM_W:].reshape(2, bp, MEM_LEN, MEM_HEADS, MEM_HD)
    return (hp2.reshape(bp, tp, d), unpad(hs2),
            st_p[None, None],
            pk.reshape(1, bp, buf_p, SWA_HEADS, SWA_HD), pv.reshape(1, bp, buf_p, SWA_HEADS, SWA_HD),
            p_mk, p_mv,
            gv_p[None, None],
            st_s[None],
            sk[None], sv[None],
            unpad(gv_s)[None])
```

```python
import functools
import math

import jax
import jax.numpy as jnp
import numpy as np
from jax import lax
from jax.experimental import pallas as pl
from jax.experimental.pallas import tpu as pltpu

F32 = jnp.float32
BF16 = jnp.bfloat16

D_MODEL = 2048
PAST_LEN = 8192
RET_HEADS = 8
RET_DK = 128
RET_DV = 256
RET_CHUNK = 128
RET_THETA = 10000.0
SWA_HEADS = 8
SWA_HD = 128
SWA_PATTERNS = ((128, 1), (512, 4), (2048, 16))
SWA_MAX_WINDOW = 2048
SWA_BLOCK = 128
ROPE_THETA = 500000.0
ROPE_DIMS = SWA_HD // 4
GMLP_CHUNK = 128
GMLP_WIDTH = D_MODEL
GMLP_GROUP = 128
GMLP_GROUPS = GMLP_WIDTH // GMLP_GROUP
MEM_LEN = 256
MEM_HEADS = 4
MEM_HD = 128
MEM_W = MEM_HEADS * MEM_HD
NORM_EPS = 1e-6
NEG_INF = -1e30
LOG2E = math.log2(math.e)

E_AQ = 0
E_AK = E_AQ + RET_HEADS * RET_DK
E_AV = E_AK + RET_HEADS * RET_DK
E_AG = E_AV + RET_HEADS * RET_DV
E_BQ = E_AG + RET_HEADS * RET_DV
E_BK = E_BQ + SWA_HEADS * SWA_HD
E_BV = E_BK + SWA_HEADS * SWA_HD
E_BG = E_BV + SWA_HEADS * SWA_HD
E_MQ = E_BG + SWA_HEADS * SWA_HD
E_MG = E_MQ + MEM_W
EVEN_IN = E_MG + MEM_W
O_U = 0
O_V = O_U + GMLP_WIDTH
O_Z = O_V + GMLP_WIDTH
O_MQ = O_Z + GMLP_WIDTH
O_MG = O_MQ + MEM_W
ODD_IN = O_MG + MEM_W

SWA_TILE = SWA_MAX_WINDOW
SAMPLE_PAD = 8
GMLP_STEP_CHUNKS = 2
SWA_GROUP = 1
RET_PROMPT_CHUNK = 256
RET_STEP_CHUNKS = 1
MEM_ROWS_STEP_SEQS = 2
RET_STEP_SEQS = 2
OUT_PROJ_SUB_ROWS = 256
NORM_SUB_ROWS = 256
VMEM_LIMIT = 56 * 1024 * 1024


def _params(*sem):
    return pltpu.CompilerParams(dimension_semantics=sem, vmem_limit_bytes=VMEM_LIMIT)


def _dot(a, b):
    return jnp.dot(a, b, preferred_element_type=F32)


def _dot_nt(a, b):
    return lax.dot_general(a, b, (((1,), (1,)), ((), ())), preferred_element_type=F32)


def _silu(x):
    return x / (1.0 + jnp.exp2(x * (-LOG2E)))


def _gelu(x):
    a = -2.0 * math.sqrt(2.0 / math.pi) * LOG2E
    return x / (1.0 + jnp.exp2(x * (a + (a * 0.044715) * (x * x))))


def _div(x, n):
    assert n & (n - 1) == 0
    return jnp.right_shift(x, n.bit_length() - 1)


def _mod(x, n):
    assert n & (n - 1) == 0
    return jnp.bitwise_and(x, n - 1)


def _rot_lanes(n_rot, theta, width):
    half = n_rot // 2
    inv = 1.0 / (theta ** (np.arange(half, dtype=np.float64) / half))
    rest = np.zeros(width - n_rot)
    return np.concatenate([inv, inv, rest]), np.concatenate([-np.ones(half), np.ones(half), rest])


def _rot_tables(pos, n_rot, theta, width):
    f, sgn = _rot_lanes(n_rot, theta, width)
    ang = np.asarray(pos, np.float64)[:, None] * f[None, :]
    return np.cos(ang).astype(np.float32), (sgn * np.sin(ang)).astype(np.float32)


def _rot_tables_range(n, n_rot, theta, width, blk=128):
    assert n % blk == 0
    f, sgn = _rot_lanes(n_rot, theta, width)
    hi = (np.arange(n // blk, dtype=np.float64) * blk)[:, None] * f[None, :]
    lo = np.arange(blk, dtype=np.float64)[:, None] * f[None, :]
    dev = lambda a: jnp.asarray(a, F32)
    ch, sh, cl, sl = dev(np.cos(hi)[:, None]), dev(np.sin(hi)[:, None]), dev(np.cos(lo)[None]), dev(np.sin(lo)[None])
    chs, shs = dev((sgn * np.cos(hi))[:, None]), dev((sgn * np.sin(hi))[:, None])
    return (ch * cl - sh * sl).reshape(n, width), (shs * cl + chs * sl).reshape(n, width)


def _rot_full(x, cos, sin):
    return x * cos + pltpu.roll(x, RET_DK // 2, 1) * sin


def _rot_partial(x, cos, sin):
    half = ROPE_DIMS // 2
    lane = _mod(lax.broadcasted_iota(jnp.int32, x.shape, 1), SWA_HD)
    width = x.shape[1]
    partner = jnp.where(lane < half, pltpu.roll(x, width - half, 1), pltpu.roll(x, half, 1))
    return x * cos + partner * sin


def _norm_matmul_kernel(x_ref, g_ref, w_ref, *rest, side, cast_cols, rot, acts, ntiles):
    if rot is not None:
        cos_ref, sin_ref, rest = rest[0], rest[1], rest[2:]
    nc = len(cast_cols)
    cast_in, o_ref, rest = rest[:nc], rest[nc], rest[nc + 1:]
    cast_out, xn_ref = rest[len(rest) - 1 - nc:len(rest) - 1], rest[-1]
    tm = x_ref.shape[0]
    sub = min(NORM_SUB_ROWS, tm)

    for src, dst, ncol in zip(cast_in, cast_out, cast_cols):
        @pl.when(pl.program_id(1) < ncol)
        def _(src=src, dst=dst):
            dst[...] = src[...].astype(BF16)

    kind = {}
    if rot is not None:
        jq, jk, q_scale = rot
        kind[jq], kind[jk] = "rope_q", "rope_k"
    for lo, hi, name in acts:
        kind.update({t: name for t in range(lo, hi)})

    def epilogue(name):
        if name in ("rope_q", "rope_k"):
            nrep = o_ref.shape[1] // cos_ref.shape[1]
            cos, sin = jnp.tile(cos_ref[...], (1, nrep)), jnp.tile(sin_ref[...], (1, nrep))
            return (lambda acc: _rot_partial(acc * q_scale, cos, sin)) if name == "rope_q" else (
                lambda acc: _rot_partial(acc, cos, sin))
        return {"gelu": _gelu, "silu": _silu, None: lambda acc: acc}[name]

    j = pl.program_id(1)

    @pl.when(j == 0)
    def _():
        first = epilogue(kind.get(0))
        for r0 in range(0, tm, sub):
            rs = slice(r0, r0 + sub)
            x = x_ref[rs, :]
            ms = jnp.mean(x * x, axis=-1, keepdims=True)
            xn = (x * lax.rsqrt(ms + NORM_EPS) * g_ref[...]).astype(BF16)
            xn_ref[rs, :] = xn
            o_ref[rs, :] = first(_dot(xn, w_ref[...])).astype(o_ref.dtype)

    def tail(name):
        acc = epilogue(name)(_dot(xn_ref[...], w_ref[...]))
        o_ref[...] = acc.astype(o_ref.dtype)
        if side is not None:
            i0, j0, nj = side
            i = pl.program_id(0)

            @pl.when((i >= i0) & (j >= j0) & (j < j0 + nj))
            def _():
                rest[0][...] = acc

    for name in sorted({kind.get(t) for t in range(1, ntiles)}, key=str):
        tiles = [t for t in range(1, ntiles) if kind.get(t) == name]
        pl.when(functools.reduce(jnp.logical_or, [j == t for t in tiles]))(functools.partial(tail, name))


def _norm_matmul(x, g, w, tm, tn, out_dtype=F32, side=None, casts=(), rot=None, acts=()):
    m, d = x.shape
    n = w.shape[1]
    ni, nj_grid = m // tm, n // tn
    out_specs = [pl.BlockSpec((tm, tn), lambda i, j: (i, j))]
    out_shape = [jax.ShapeDtypeStruct((m, n), out_dtype)]
    tiles = None
    if side is not None:
        row0, col0, ncols = side
        assert row0 % tm == 0 and col0 % tn == 0 and ncols % tn == 0 and col0 >= tn
        i0, j0, nj = row0 // tm, col0 // tn, ncols // tn
        tiles = (i0, j0, nj)
        out_specs.append(pl.BlockSpec(
            (tm, tn), lambda i, j: (jnp.maximum(i - i0, 0), jnp.where(i >= i0, jnp.clip(j - j0, 0, nj - 1), 0))))
        out_shape.append(jax.ShapeDtypeStruct((m - row0, ncols), F32))
    rot_tiles, rot_specs, rot_args = None, [], []
    if rot is not None:
        cos, sin, q_col, k_col, q_scale = rot
        assert q_col % tn == 0 and k_col % tn == 0 and q_col >= tn and k_col >= tn and tn % cos.shape[1] == 0
        rot_tiles = (q_col // tn, k_col // tn, q_scale)
        rot_specs = [pl.BlockSpec((tm, cos.shape[1]), lambda i, j: (i, 0))] * 2
        rot_args = [cos, sin]
    assert all(lo % tn == 0 and hi % tn == 0 for lo, hi, _ in acts)
    act_tiles = tuple((lo // tn, hi // tn, name) for lo, hi, name in acts)
    cast_specs, cast_cols = [], []
    for a, bc in casts:
        r, c = a.shape
        assert r % ni == 0 and (r // ni) % 16 == 0 and c % bc == 0 and c // bc <= nj_grid
        ncol = c // bc
        spec = pl.BlockSpec((r // ni, bc), functools.partial(lambda i, j, last: (i, jnp.minimum(j, last)), last=ncol - 1))
        cast_specs.append(spec)
        cast_cols.append(ncol)
        out_specs.append(spec)
        out_shape.append(jax.ShapeDtypeStruct((r, c), BF16))
    res = pl.pallas_call(
        functools.partial(_norm_matmul_kernel, side=tiles, cast_cols=tuple(cast_cols), rot=rot_tiles,
                          acts=act_tiles, ntiles=nj_grid),
        grid=(ni, nj_grid),
        in_specs=[pl.BlockSpec((tm, d), lambda i, j: (i, 0)),
                  pl.BlockSpec((1, d), lambda i, j: (0, 0)),
                  pl.BlockSpec((d, tn), lambda i, j: (0, j))] + rot_specs + cast_specs,
        out_specs=out_specs,
        out_shape=out_shape,
        scratch_shapes=[pltpu.VMEM((tm, d), BF16)],
        compiler_params=_params("arbitrary", "arbitrary"),
        name="norm_matmul",
    )(x, g.reshape(1, d), w, *rot_args, *[a for a, _ in casts])
    return res if len(res) > 1 else res[0]


def _norm_matmul_cast_kernel(x_ref, g_ref, w_ref, o_ref, wb_ref, xn_ref):
    @pl.when(pl.program_id(0) == 0)
    def _():
        x = x_ref[...]
        ms = jnp.mean(x * x, axis=-1, keepdims=True)
        xn_ref[...] = (x * lax.rsqrt(ms + NORM_EPS) * g_ref[...]).astype(BF16)

    wb = w_ref[...].astype(BF16)
    wb_ref[...] = wb
    o_ref[...] = _dot(xn_ref[...], wb).astype(o_ref.dtype)


def _norm_matmul_cast(x, g, w, tn):
    m, d = x.shape
    n = w.shape[1]
    return pl.pallas_call(
        _norm_matmul_cast_kernel,
        grid=(n // tn,),
        in_specs=[pl.BlockSpec((m, d), lambda j: (0, 0)),
                  pl.BlockSpec((1, d), lambda j: (0, 0)),
                  pl.BlockSpec((d, tn), lambda j: (0, j))],
        out_specs=[pl.BlockSpec((m, tn), lambda j: (0, j)),
                   pl.BlockSpec((d, tn), lambda j: (0, j))],
        out_shape=[jax.ShapeDtypeStruct((m, n), F32),
                   jax.ShapeDtypeStruct((d, n), BF16)],
        scratch_shapes=[pltpu.VMEM((m, d), BF16)],
        compiler_params=_params("arbitrary"),
        name="norm_matmul_cast",
    )(x, g.reshape(1, d), w)


def _ret_tables(c, reps, pad=None):
    pad = c if pad is None else pad
    rows = pad * reps
    lg = np.log1p(-np.exp2(-5.0 - np.arange(RET_HEADS, dtype=np.float64)))
    row = np.arange(rows)
    step = (row % pad).astype(np.float64)
    live = (row % pad) < c
    rel = step[:, None] - step[None, :]
    keep = ((row // pad)[:, None] == (row // pad)[None, :]) & (rel >= 0) & live[:, None] & live[None, :]
    decay = np.where(keep[None], np.exp(lg[:, None, None] * np.maximum(rel, 0.0)[None]), 0.0)
    q_decay = np.where(live[None], np.exp(lg[:, None] * (step[None, :] + 1.0)), 0.0)
    k_decay = np.where(live[None], np.exp(lg[:, None] * (c - 1.0 - step[None, :])), 0.0)
    chunk_decay = np.exp(lg * c)
    f32 = lambda a, shape: np.ascontiguousarray(np.broadcast_to(a, shape), dtype=np.float32)
    qd = f32(q_decay[:, :, None], (RET_HEADS, rows, RET_DK))
    kd = f32(k_decay[:, :, None], (RET_HEADS, rows, RET_DK))
    cd = f32(chunk_decay[:, None, None], (RET_HEADS, 1, RET_DV))
    return decay.astype(np.float32), qd, kd, cd


def _group_norm_gate(o, gn, g):
    mu = jnp.mean(o, axis=-1, keepdims=True)
    d = o - mu
    var = jnp.mean(d * d, axis=-1, keepdims=True)
    return d * lax.rsqrt(var + NORM_EPS) * gn * _silu(g.astype(F32))


def _ret_prompt_kernel(q_ref, k_ref, v_ref, g_ref, cos_ref, sin_ref, dec_ref, qd_ref, kd_ref, cd_ref, gn_ref,
                       y_ref, st_ref):
    @pl.when(pl.program_id(0) == 0)
    def _():
        st_ref[...] = jnp.zeros_like(st_ref)

    c = dec_ref.shape[1]
    for i in range(q_ref.shape[0] // c):
        rs = slice(i * c, (i + 1) * c)
        cos = cos_ref[rs, :]
        sin = sin_ref[rs, :]
        for h in range(RET_HEADS):
            ks = slice(h * RET_DK, (h + 1) * RET_DK)
            vs = slice(h * RET_DV, (h + 1) * RET_DV)
            q = _rot_full(q_ref[rs, ks].astype(F32), cos, sin)
            k = _rot_full(k_ref[rs, ks].astype(F32) * (RET_DK ** -0.5), cos, sin)
            v = v_ref[rs, vs].astype(BF16)
            st = st_ref[h]
            s = _dot_nt(q.astype(BF16), k.astype(BF16)) * dec_ref[h]
            lhs = jnp.concatenate([s.astype(BF16), (q * qd_ref[h]).astype(BF16)], axis=1)
            o = _dot(lhs, jnp.concatenate([v, st.astype(BF16)], axis=0))
            st_ref[h] = cd_ref[h] * st + _dot((k * kd_ref[h]).T.astype(BF16), v)
            y_ref[rs, vs] = _group_norm_gate(o, gn_ref[:, vs], g_ref[rs, vs]).astype(y_ref.dtype)


def _ret_prompt(proj, cos, sin, ret_gn):
    t = proj.shape[0]
    dec, qd, kd, cd = _ret_tables(RET_PROMPT_CHUNK, 1)
    c = RET_STEP_CHUNKS * RET_PROMPT_CHUNK
    assert t % c == 0
    qw = RET_HEADS * RET_DK
    vw = RET_HEADS * RET_DV
    full3 = lambda n: (0, 0, 0)
    return pl.pallas_call(
        _ret_prompt_kernel,
        grid=(t // c,),
        in_specs=[pl.BlockSpec((c, qw), lambda n: (n, E_AQ // qw)),
                  pl.BlockSpec((c, qw), lambda n: (n, E_AK // qw)),
                  pl.BlockSpec((c, vw), lambda n: (n, E_AV // vw)),
                  pl.BlockSpec((c, vw), lambda n: (n, E_AG // vw)),
                  pl.BlockSpec((c, RET_DK), lambda n: (n, 0)),
                  pl.BlockSpec((c, RET_DK), lambda n: (n, 0)),
                  pl.BlockSpec(dec.shape, full3),
                  pl.BlockSpec(qd.shape, full3),
                  pl.BlockSpec(kd.shape, full3),
                  pl.BlockSpec(cd.shape, full3),
                  pl.BlockSpec((1, vw), lambda n: (0, 0))],
        out_specs=[pl.BlockSpec((c, vw), lambda n: (n, 0)),
                   pl.BlockSpec((RET_HEADS, RET_DK, RET_DV), full3)],
        out_shape=[jax.ShapeDtypeStruct((t, vw), BF16),
                   jax.ShapeDtypeStruct((RET_HEADS, RET_DK, RET_DV), F32)],
        compiler_params=_params("arbitrary"),
        name="ret_prompt",
    )(proj, proj, proj, proj, cos, sin, dec, qd, kd, cd, ret_gn.reshape(1, vw))


def _ret_sample_kernel(q_ref, k_ref, v_ref, g_ref, cos_ref, sin_ref, dec_ref, qd_ref, kd_ref, cd_ref, gn_ref,
                       st_ref, y_ref, sto_ref, cross_ref, qs_ref, kt_ref, *, pad):
    b = pl.program_id(0)
    nb = pl.num_programs(0)
    win = RET_CHUNK

    @pl.when(b == 0)
    def _():
        cross_ref[...] = jnp.zeros_like(cross_ref)
        cos = cos_ref[...]
        sin = sin_ref[...]
        for h in range(RET_HEADS):
            ks = slice(h * RET_DK, (h + 1) * RET_DK)
            qs_ref[:, ks] = _rot_full(q_ref[:, ks], cos, sin) * qd_ref[h]
            kt_ref[h] = (_rot_full(k_ref[:, ks] * (RET_DK ** -0.5), cos, sin) * kd_ref[h]).T

    for i in range(st_ref.shape[0]):
        sq = b * st_ref.shape[0] + i
        w0 = pl.multiple_of(jnp.right_shift(sq * pad, win.bit_length() - 1) * win, win)
        wrows = pl.ds(w0, win)
        rows = lax.broadcasted_iota(jnp.int32, (win, RET_DK), 0) + w0
        cols = lax.broadcasted_iota(jnp.int32, (RET_DK, win), 1) + w0
        mine_r = (rows >= sq * pad) & (rows < (sq + 1) * pad)
        mine_c = (cols >= sq * pad) & (cols < (sq + 1) * pad)
        for h in range(RET_HEADS):
            ks = slice(h * RET_DK, (h + 1) * RET_DK)
            vs = slice(h * RET_DV, (h + 1) * RET_DV)
            st = st_ref[i, h]
            qm = jnp.where(mine_r, qs_ref[wrows, ks], 0.0)
            kmt = jnp.where(mine_c, kt_ref[h, :, wrows], 0.0)
            cross_ref[wrows, vs] += _dot(qm.astype(BF16), st.astype(BF16))
            sto_ref[i, h] = cd_ref[h] * st + _dot(kmt.astype(BF16), v_ref[wrows, vs].astype(BF16))

    @pl.when(b == nb - 1)
    def _():
        cos_all = cos_ref[...]
        sin_all = sin_ref[...]
        for h in range(RET_HEADS):
            ks = slice(h * RET_DK, (h + 1) * RET_DK)
            vs = slice(h * RET_DV, (h + 1) * RET_DV)
            q = _rot_full(q_ref[:, ks], cos_all, sin_all)
            k = _rot_full(k_ref[:, ks] * (RET_DK ** -0.5), cos_all, sin_all)
            v = v_ref[:, vs].astype(BF16)
            s = _dot_nt(q.astype(BF16), k.astype(BF16)) * dec_ref[h]
            o = _dot(s.astype(BF16), v) + cross_ref[:, vs]
            y_ref[:, vs] = _group_norm_gate(o, gn_ref[:, vs], g_ref[:, vs]).astype(y_ref.dtype)


def _ret_sample(proj, state, cos, sin, ret_gn, nseq, seq, pad):
    rows = nseq * pad
    c = math.gcd(seq, RET_CHUNK)
    assert c == seq, "sample chunk must cover the new tokens"
    assert rows % RET_CHUNK == 0 and RET_CHUNK % pad == 0
    dec, qd, kd, cd = _ret_tables(c, nseq, pad)
    qw = RET_HEADS * RET_DK
    vw = RET_HEADS * RET_DV
    full3 = lambda b: (0, 0, 0)
    n = RET_STEP_SEQS
    assert nseq % n == 0
    st_spec = pl.BlockSpec((n, RET_HEADS, RET_DK, RET_DV), lambda b: (b, 0, 0, 0))
    return pl.pallas_call(
        functools.partial(_ret_sample_kernel, pad=pad),
        grid=(nseq // n,),
        in_specs=[pl.BlockSpec((rows, qw), lambda b: (0, E_AQ // qw)),
                  pl.BlockSpec((rows, qw), lambda b: (0, E_AK // qw)),
                  pl.BlockSpec((rows, vw), lambda b: (0, E_AV // vw)),
                  pl.BlockSpec((rows, vw), lambda b: (0, E_AG // vw)),
                  pl.BlockSpec((rows, RET_DK), lambda b: (0, 0)),
                  pl.BlockSpec((rows, RET_DK), lambda b: (0, 0)),
                  pl.BlockSpec(dec.shape, full3),
                  pl.BlockSpec(qd.shape, full3),
                  pl.BlockSpec(kd.shape, full3),
                  pl.BlockSpec(cd.shape, full3),
                  pl.BlockSpec((1, vw), lambda b: (0, 0)),
                  st_spec],
        out_specs=[pl.BlockSpec((rows, vw), lambda b: (0, 0)), st_spec],
        out_shape=[jax.ShapeDtypeStruct((rows, vw), F32),
                   jax.ShapeDtypeStruct(state.shape, F32)],
        scratch_shapes=[pltpu.VMEM((rows, vw), F32),
                        pltpu.VMEM((rows, qw), F32),
                        pltpu.VMEM((RET_HEADS, RET_DK, rows), F32)],
        compiler_params=_params("arbitrary"),
        name="ret_sample",
    )(proj, proj, proj, proj, cos, sin, dec, qd, kd, cd, ret_gn.reshape(1, vw), state)


def _swa_prompt_kernel(q_ref, kc_ref, vc_ref, g_ref, y_ref, q_s, k_s, v_s, m_s, l_s, acc_s):
    t = pl.program_id(1)
    tb = SWA_TILE
    blk = SWA_BLOCK
    hd = SWA_HD

    @pl.when(t == 0)
    def _():
        k_s[0:tb, :] = jnp.zeros((tb, hd), F32)
        v_s[0:tb, :] = jnp.zeros((tb, hd), F32)

    @pl.when(t > 0)
    def _():
        k_s[0:tb, :] = k_s[tb:2 * tb, :]
        v_s[0:tb, :] = v_s[tb:2 * tb, :]

    q_s[...] = q_ref[...].astype(F32)
    k_s[tb:2 * tb, :] = kc_ref[...].astype(F32)
    v_s[tb:2 * tb, :] = vc_ref[...].astype(F32)

    def ds(start, size, stride):
        return pl.ds(start, size) if stride == 1 else pl.ds(start, size, stride=stride)

    def band(g):
        qi = lax.broadcasted_iota(jnp.int32, (g * blk, (g + 1) * blk), 0)
        kj = lax.broadcasted_iota(jnp.int32, (g * blk, (g + 1) * blk), 1)
        ok = (kj >= qi) & (kj <= qi + blk)
        return ok, ok & ((kj >= blk) | (t > 0))

    for bi, (window, r) in enumerate(SWA_PATTERNS):
        assert window // r == blk
        per_class = tb // (blk * r)
        g = SWA_GROUP if per_class % SWA_GROUP == 0 else 1
        ok, ok_first = band(g)
        ones = jnp.ones(((g + 1) * blk, hd), BF16)
        for c in range(r):
            for n in range(0, per_class, g):
                qs = n * blk * r + c
                ks = qs + tb - blk * r
                qrows = ds(qs, g * blk, r)
                krows = ds(ks, (g + 1) * blk, r)
                s = _dot_nt(q_s[qrows, :].astype(BF16), k_s[krows, :].astype(BF16))
                s = jnp.where(ok_first if n == 0 else ok, s, NEG_INF)
                m = jnp.max(s, axis=-1, keepdims=True)
                p = jnp.exp2(s - m).astype(BF16)
                pv = _dot(p, jnp.concatenate([v_s[krows, :].astype(BF16), ones], axis=1))
                m_s[bi, qrows, :] = jnp.broadcast_to(m, (g * blk, hd))
                l_s[bi, qrows, :] = pv[:, hd:]
                acc_s[bi, qrows, :] = pv[:, :hd]

    nbr = len(SWA_PATTERNS)
    ms = [m_s[i] for i in range(nbr)]
    mmax = functools.reduce(jnp.maximum, ms)
    wts = [jnp.exp2(mi - mmax) for mi in ms]
    num = sum(wts[i] * acc_s[i] for i in range(nbr))
    den = sum(wts[i] * l_s[i] for i in range(nbr))
    y_ref[...] = (num / den * _silu(g_ref[...].astype(F32))).astype(y_ref.dtype)


def _swa_prompt(proj):
    t = proj.shape[0]
    tb = SWA_TILE
    assert t % tb == 0
    hd = SWA_HD
    nbr = len(SWA_PATTERNS)
    cur = lambda off: pl.BlockSpec((tb, hd), lambda h, i: (i, off // hd + h))
    return pl.pallas_call(
        _swa_prompt_kernel,
        grid=(SWA_HEADS, t // tb),
        in_specs=[cur(E_BQ), cur(E_BK), cur(E_BV), cur(E_BG)],
        out_specs=pl.BlockSpec((tb, hd), lambda h, i: (i, h)),
        out_shape=jax.ShapeDtypeStruct((t, SWA_HEADS * hd), BF16),
        scratch_shapes=[pltpu.VMEM((tb, hd), F32),
                        pltpu.VMEM((2 * tb, hd), F32),
                        pltpu.VMEM((2 * tb, hd), F32),
                        pltpu.VMEM((nbr, tb, hd), F32),
                        pltpu.VMEM((nbr, tb, hd), F32),
                        pltpu.VMEM((nbr, tb, hd), F32)],
        compiler_params=_params("arbitrary", "arbitrary"),
        name="swa_prompt",
    )(proj, proj, proj, proj)


def _swa_sample_kernel(q_ref, k_ref, v_ref, g_ref, cos_ref, sin_ref, ka_ref, kb_ref, va_ref, vb_ref,
                       y_ref, ko_ref, kn_s, vn_s, ba_s, bb_s, bn_s, *, seq):
    pad = SAMPLE_PAD
    nh = SWA_HEADS
    hd = SWA_HD
    nr = nh * pad
    nj, nrh, _ = kb_ref.shape[1:]
    na = ka_ref.shape[1]
    nb = nj * nrh
    nn = kn_s.shape[0]

    @pl.when(pl.program_id(0) == 0)
    def _():
        (w1, d1), (w2, d2), (w3, d3) = SWA_PATTERNS

        def rows_cols(n):
            r = lax.broadcasted_iota(jnp.int32, (nr, n), 0)
            c = lax.broadcasted_iota(jnp.int32, (nr, n), 1)
            return _div(r, pad), _mod(r, pad), c

        def log2_mult(mult):
            return jnp.where(mult > 0, jnp.log2(jnp.maximum(mult, 1.0)), NEG_INF)

        hq, tq, c = rows_cols(na)
        dist = na // nh + tq - _div(c, nh)
        mult_a = ((dist <= w1).astype(F32) + ((_mod(dist, d2) == 0) & (dist <= w2)).astype(F32)
                  + ((_mod(dist, d3) == 0) & (dist <= w3)).astype(F32))
        ba_s[...] = log2_mult(jnp.where((hq == _mod(c, nh)) & (tq < seq), mult_a, 0.0))
        hq, tq, c = rows_cols(nb)
        bb_s[...] = log2_mult(jnp.where((hq == _mod(c, nh)) & (tq == _mod(_div(c, nh), nrh // nh)), 1.0, 0.0))
        hq, tq, c = rows_cols(nn)
        tk = _mod(c, pad)
        mult_n = (tk <= tq).astype(F32) + 2.0 * (tk == tq).astype(F32)
        bn_s[...] = log2_mult(jnp.where((hq == _div(c, pad)) & (tq < seq) & (tk < seq), mult_n, 0.0))
        kn_s[...] = jnp.zeros_like(kn_s)
        vn_s[...] = jnp.zeros_like(vn_s)

    cos = cos_ref[...]
    sin = sin_ref[...]
    q8 = _rot_partial(q_ref[0] * (SWA_HD ** -0.5 * LOG2E), cos, sin)
    k8 = _rot_partial(k_ref[0], cos, sin)
    ko_ref[0] = k8

    def head_rows(x):
        return jnp.concatenate([x[:, h * hd:(h + 1) * hd] for h in range(nh)], axis=0)

    qrows = head_rows(q8).astype(BF16)
    kn_s[0:nr, :] = head_rows(k8).astype(BF16)
    vn_s[0:nr, :] = head_rows(v_ref[0]).astype(BF16)

    kb = kb_ref[0].reshape(nb, hd)
    vb = vb_ref[0].reshape(nb, hd)
    s_a = _dot_nt(qrows, ka_ref[0].astype(BF16)) + ba_s[...]
    s_b = _dot_nt(qrows, kb.astype(BF16)) + bb_s[...]
    s_n = _dot_nt(qrows, kn_s[...]) + bn_s[...]
    rowmax = lambda s: jnp.max(s, axis=-1, keepdims=True)
    m = jnp.maximum(jnp.maximum(rowmax(s_a), rowmax(s_b)), rowmax(s_n))
    p_a, p_b, p_n = jnp.exp2(s_a - m), jnp.exp2(s_b - m), jnp.exp2(s_n - m)
    rowsum = lambda p: jnp.sum(p, axis=-1, keepdims=True)
    o = (_dot(p_a.astype(BF16), va_ref[0].astype(BF16)) + _dot(p_b.astype(BF16), vb.astype(BF16))
         + _dot(p_n.astype(BF16), vn_s[...])) / (rowsum(p_a) + rowsum(p_b) + rowsum(p_n))
    y = jnp.concatenate([o[h * pad:(h + 1) * pad] for h in range(nh)], axis=1)
    y_ref[0] = y * _silu(g_ref[0])


def _swa_sample(proj_pad, cache_k, cache_v, cos, sin, seq):
    nseq, pad, _ = proj_pad.shape
    buf = cache_k.shape[1]
    nh, hd = SWA_HEADS, SWA_HD
    w = nh * hd
    (w1, d1), (w2, d2), (w3, d3) = SWA_PATTERNS
    assert buf == w3 and d1 == 1 and seq <= d2 and w1 <= w2 and buf % w2 == 0 and buf % d3 == 0
    assert nh * pad <= SWA_BLOCK and (seq * nh) % 8 == 0
    flat = lambda a: a.reshape(nseq, buf * nh, hd)
    strided = lambda a: a.reshape(nseq, buf // d3, d3 * nh, hd)
    col = lambda off: pl.BlockSpec((1, pad, w), lambda b: (b, 0, off // w))
    tail = pl.BlockSpec((1, w2 * nh, hd), lambda b: (b, buf // w2 - 1, 0))
    resid = pl.BlockSpec((1, (buf - w2) // d3, seq * nh, hd), lambda b: (b, 0, 0, 0))
    tab = pl.BlockSpec((pad, w), lambda b: (0, 0))
    out = pl.BlockSpec((1, pad, w), lambda b: (b, 0, 0))
    return pl.pallas_call(
        functools.partial(_swa_sample_kernel, seq=seq),
        grid=(nseq,),
        in_specs=[col(E_BQ), col(E_BK), col(E_BV), col(E_BG), tab, tab, tail, resid, tail, resid],
        out_specs=[out, out],
        out_shape=[jax.ShapeDtypeStruct((nseq, pad, w), F32),
                   jax.ShapeDtypeStruct((nseq, pad, w), F32)],
        scratch_shapes=[pltpu.VMEM((SWA_BLOCK, hd), BF16), pltpu.VMEM((SWA_BLOCK, hd), BF16),
                        pltpu.VMEM((nh * pad, w2 * nh), F32),
                        pltpu.VMEM((nh * pad, (buf - w2) // d3 * seq * nh), F32),
                        pltpu.VMEM((nh * pad, SWA_BLOCK), F32)],
        compiler_params=_params("arbitrary"),
        name="swa_sample",
    )(proj_pad, proj_pad, proj_pad, proj_pad, cos, sin,
      flat(cache_k), strided(cache_k), flat(cache_v), strided(cache_v))


def _mem_attend(q, g, k, v):
    ones = jnp.ones((MEM_LEN, MEM_HD), BF16)
    outs = []
    for h in range(MEM_HEADS):
        hs = slice(h * MEM_HD, (h + 1) * MEM_HD)
        qh = (q[:, hs].astype(F32) * (MEM_HD ** -0.5 * LOG2E)).astype(BF16)
        s = _dot_nt(qh, k[:, hs].astype(BF16))
        m = jnp.max(s, axis=-1, keepdims=True)
        p = jnp.exp2(s - m).astype(BF16)
        pv = _dot(p, jnp.concatenate([v[:, hs].astype(BF16), ones], axis=1))
        o = pv[:, :MEM_HD] / pv[:, MEM_HD:]
        outs.append((o * _silu(g[:, hs].astype(F32))).astype(BF16))
    return jnp.concatenate(outs, axis=1)


def _mem_out_proj_rows_kernel(*refs, n_in):
    q_ref, g_ref, k_ref, v_ref = refs[:4]
    a_refs, w_refs = refs[4:4 + n_in], refs[4 + n_in:4 + 2 * n_in]
    wm_ref, gn_ref, x_ref, o_ref, bias_s, ym_s = refs[4 + 2 * n_in:]
    nseq, pad, _ = q_ref.shape
    nh, hd = MEM_HEADS, MEM_HD
    step = pl.program_id(0)

    @pl.when(step == 0)
    def _():
        r = lax.broadcasted_iota(jnp.int32, bias_s.shape, 0)
        c = lax.broadcasted_iota(jnp.int32, bias_s.shape, 1)
        bias_s[...] = jnp.where(_div(r, pad) == _mod(c, nh), 0.0, NEG_INF)

    for i in range(nseq):
        q = q_ref[i] * (MEM_HD ** -0.5 * LOG2E)
        qrows = jnp.concatenate([q[:, h * hd:(h + 1) * hd] for h in range(nh)], axis=0).astype(BF16)
        s = _dot_nt(qrows, k_ref[i].astype(BF16)) + bias_s[...]
        m = jnp.max(s, axis=-1, keepdims=True)
        p = jnp.exp2(s - m)
        o = _dot(p.astype(BF16), v_ref[i].astype(BF16)) / jnp.sum(p, axis=-1, keepdims=True)
        y = jnp.concatenate([o[h * pad:(h + 1) * pad] for h in range(nh)], axis=1)
        row0 = pl.multiple_of((step * nseq + i) * pad, pad)
        ym_s[pl.ds(row0, pad), :] = y * _silu(g_ref[i])

    @pl.when(step == pl.num_programs(0) - 1)
    def _():
        acc = _dot(ym_s[...].astype(BF16), wm_ref[...])
        for a_ref, w_ref in zip(a_refs, w_refs):
            acc = acc + _dot(a_ref[...].astype(BF16), w_ref[...])
        ms = jnp.mean(acc * acc, axis=-1, keepdims=True)
        o_ref[...] = x_ref[...] + acc * lax.rsqrt(ms + NORM_EPS) * gn_ref[...]


def _mem_out_proj_rows(proj_pad, q_off, g_off, mem_k, mem_v, layer, acts, w, g, x):
    b, pad, _ = proj_pad.shape
    m, d = x.shape
    mw = MEM_W
    rows = MEM_LEN * MEM_HEADS
    n = MEM_ROWS_STEP_SEQS
    assert b % n == 0 and m == b * pad
    const = lambda shape, idx: pl.BlockSpec(shape, lambda i: idx, pipeline_mode=pl.Buffered(1))
    kv = pl.BlockSpec((n, rows, MEM_HD), lambda i: (layer * (b // n) + i, 0, 0))
    w_specs, off = [], 0
    for a in acts:
        k = a.shape[1]
        assert off % k == 0
        w_specs.append(const((k, d), (off // k, 0)))
        off += k
    assert off + mw == w.shape[0] and off % mw == 0
    return pl.pallas_call(
        functools.partial(_mem_out_proj_rows_kernel, n_in=len(acts)),
        grid=(b // n,),
        in_specs=[pl.BlockSpec((n, pad, mw), lambda i: (i, 0, q_off // mw)),
                  pl.BlockSpec((n, pad, mw), lambda i: (i, 0, g_off // mw)),
                  kv, kv]
                 + [const(a.shape, (0, 0)) for a in acts] + w_specs
                 + [const((mw, d), (off // mw, 0)), const((1, d), (0, 0)), const((m, d), (0, 0))],
        out_specs=pl.BlockSpec((m, d), lambda i: (0, 0)),
        out_shape=jax.ShapeDtypeStruct((m, d), F32),
        scratch_shapes=[pltpu.VMEM((MEM_HEADS * pad, rows), F32), pltpu.VMEM((m, mw), F32)],
        compiler_params=_params("arbitrary"),
        name="mem_out_proj_rows",
    )(proj_pad, proj_pad, mem_k.reshape(-1, rows, MEM_HD), mem_v.reshape(-1, rows, MEM_HD),
      *acts, *([w] * len(acts)), w, g.reshape(1, d), x)


def _gmlp_kernel(u_ref, v_ref, z_ref, ln_ref, w_ref, b_ref, y_ref, vo_ref, wc_ref, *, activated):
    c = GMLP_CHUNK
    gelu, silu = (lambda a: a, lambda a: a) if activated else (_gelu, _silu)

    @pl.when(pl.program_id(0) == 0)
    def _():
        causal = (lax.broadcasted_iota(jnp.int32, (c, c), 0) >= lax.broadcasted_iota(jnp.int32, (c, c), 1))
        for g in range(GMLP_GROUPS):
            wc_ref[g] = jnp.where(causal, w_ref[g], 0.0).astype(BF16)

    nchunks = u_ref.shape[0] // c
    keep_all = vo_ref.shape[0] == u_ref.shape[0]
    for i in range(nchunks):
        rs = slice(i * c, (i + 1) * c)
        v = gelu(v_ref[rs, :].astype(F32))
        mu = jnp.mean(v, axis=-1, keepdims=True)
        d = v - mu
        var = jnp.mean(d * d, axis=-1, keepdims=True)
        vn = d * lax.rsqrt(var + NORM_EPS) * ln_ref[...]
        if keep_all:
            vo_ref[rs, :] = vn
        elif i == nchunks - 1:
            vo_ref[...] = vn
        for g in range(GMLP_GROUPS):
            gs = slice(g * GMLP_GROUP, (g + 1) * GMLP_GROUP)
            mixed = _dot(wc_ref[g], vn[:, gs].astype(BF16)) + b_ref[:, gs]
            y_ref[rs, gs] = (gelu(u_ref[rs, gs].astype(F32)) * mixed
                             * silu(z_ref[rs, gs].astype(F32))).astype(y_ref.dtype)


def _gmlp(proj, gmlp_ln, w_mix, bias_rows, keep_all_v, activated):
    t = proj.shape[0]
    c = GMLP_CHUNK
    w = GMLP_WIDTH
    rows = GMLP_STEP_CHUNKS * c
    assert t % rows == 0
    col = lambda off: pl.BlockSpec((rows, w), lambda n: (n, off // w))
    return pl.pallas_call(
        functools.partial(_gmlp_kernel, activated=activated),
        grid=(t // rows,),
        in_specs=[col(O_U), col(O_V), col(O_Z),
                  pl.BlockSpec((1, w), lambda n: (0, 0)),
                  pl.BlockSpec(w_mix.shape, lambda n: (0, 0, 0)),
                  pl.BlockSpec((c, w), lambda n: (0, 0))],
        out_specs=[pl.BlockSpec((rows, w), lambda n: (n, 0)),
                   pl.BlockSpec((rows, w), lambda n: (n, 0)) if keep_all_v
                   else pl.BlockSpec((c, w), lambda n: (0, 0))],
        out_shape=[jax.ShapeDtypeStruct((t, w), BF16),
                   jax.ShapeDtypeStruct((t if keep_all_v else c, w), F32)],
        scratch_shapes=[pltpu.VMEM(w_mix.shape, BF16)],
        compiler_params=_params("arbitrary"),
        name="gmlp",
    )(proj, proj, proj, gmlp_ln.reshape(1, w), w_mix, bias_rows)


def _out_proj_kernel(*refs, n_in, mem):
    a_refs, w_refs = refs[:n_in], refs[n_in:2 * n_in]
    refs = refs[2 * n_in:]
    if mem:
        (mq_ref, mg_ref, mk_ref, mv_ref, wm_ref), refs = refs[:5], refs[5:]
    g_ref, x_ref, o_ref = refs
    sub = min(OUT_PROJ_SUB_ROWS, o_ref.shape[0])
    for r0 in range(0, o_ref.shape[0], sub):
        rs = slice(r0, r0 + sub)
        acc = _dot(a_refs[0][rs, :].astype(BF16), w_refs[0][...])
        for a_ref, w_ref in zip(a_refs[1:], w_refs[1:]):
            acc = acc + _dot(a_ref[rs, :].astype(BF16), w_ref[...])
        if mem:
            acc = acc + _dot(_mem_attend(mq_ref[rs, :], mg_ref[rs, :], mk_ref[...], mv_ref[...]), wm_ref[...])
        ms = jnp.mean(acc * acc, axis=-1, keepdims=True)
        o_ref[rs, :] = x_ref[rs, :] + acc * lax.rsqrt(ms + NORM_EPS) * g_ref[...]


def _mem_specs(proj, q_off, g_off, memkv, tm, w_rows, d):
    w = MEM_W
    const = lambda shape, idx: pl.BlockSpec(shape, lambda i: idx, pipeline_mode=pl.Buffered(1))
    specs = [pl.BlockSpec((tm, w), lambda i: (i, q_off // w)), pl.BlockSpec((tm, w), lambda i: (i, g_off // w)),
             const((MEM_LEN, w), (0, 0)), const((MEM_LEN, w), (0, 1)), const((w, d), ((w_rows - w) // w, 0))]
    return specs, [proj, proj, memkv, memkv]


def _out_proj(acts, w, g, x, tm, mem=None):
    m, d = x.shape
    n_in = len(acts)
    w_specs, off = [], 0
    for a in acts:
        k = a.shape[1]
        assert off % k == 0
        w_specs.append(pl.BlockSpec((k, d), functools.partial(lambda i, blk: (blk, 0), blk=off // k),
                                    pipeline_mode=pl.Buffered(1)))
        off += k
    mem_specs, mem_args = [], []
    if mem is not None:
        mem_specs, mem_args = _mem_specs(*mem, tm, w.shape[0], d)
        mem_args.append(w)
        off += MEM_W
    assert off == w.shape[0]
    return pl.pallas_call(
        functools.partial(_out_proj_kernel, n_in=n_in, mem=mem is not None),
        grid=(m // tm,),
        in_specs=([pl.BlockSpec((tm, a.shape[1]), lambda i: (i, 0)) for a in acts]
                  + w_specs + mem_specs
                  + [pl.BlockSpec((1, d), lambda i: (0, 0)),
                     pl.BlockSpec((tm, d), lambda i: (i, 0))]),
        out_specs=pl.BlockSpec((tm, d), lambda i: (i, 0)),
        out_shape=jax.ShapeDtypeStruct((m, d), F32),
        compiler_params=_params("arbitrary"),
        name="out_proj",
    )(*acts, *([w] * n_in), *mem_args, g.reshape(1, d), x)


def _gmlp_out_proj_kernel(u_ref, v_ref, z_ref, ln_ref, w_ref, b_ref, wy_ref, mq_ref, mg_ref, mk_ref, mv_ref, wm_ref,
                          g_ref, x_ref, o_ref, vo_ref, wc_ref, yc_ref):
    c = GMLP_CHUNK
    sub = yc_ref.shape[0]
    tm = o_ref.shape[0]

    @pl.when(pl.program_id(0) == 0)
    def _():
        causal = (lax.broadcasted_iota(jnp.int32, (c, c), 0) >= lax.broadcasted_iota(jnp.int32, (c, c), 1))
        for g in range(GMLP_GROUPS):
            wc_ref[g] = jnp.where(causal, w_ref[g], 0.0).astype(BF16)

    for r0 in range(0, tm, sub):
        for c0 in range(r0, r0 + sub, c):
            rs = slice(c0, c0 + c)
            v = v_ref[rs, :].astype(F32)
            mu = jnp.mean(v, axis=-1, keepdims=True)
            d = v - mu
            var = jnp.mean(d * d, axis=-1, keepdims=True)
            vn = d * lax.rsqrt(var + NORM_EPS) * ln_ref[...]
            if c0 + c == tm:
                vo_ref[...] = vn
            for g in range(GMLP_GROUPS):
                gs = slice(g * GMLP_GROUP, (g + 1) * GMLP_GROUP)
                mixed = _dot(wc_ref[g], vn[:, gs].astype(BF16)) + b_ref[:, gs]
                yc_ref[c0 - r0:c0 - r0 + c, gs] = (u_ref[rs, gs].astype(F32) * mixed
                                                   * z_ref[rs, gs].astype(F32)).astype(BF16)
        rs = slice(r0, r0 + sub)
        ym = _mem_attend(mq_ref[rs, :], mg_ref[rs, :], mk_ref[...], mv_ref[...])
        acc = _dot(yc_ref[...], wy_ref[...]) + _dot(ym, wm_ref[...])
        ms = jnp.mean(acc * acc, axis=-1, keepdims=True)
        o_ref[rs, :] = x_ref[rs, :] + acc * lax.rsqrt(ms + NORM_EPS) * g_ref[...]


def _gmlp_out_proj(proj, memkv, gmlp_ln, w_mix, bias_rows, w, g, x, tm):
    m, d = x.shape
    c = GMLP_CHUNK
    gw = GMLP_WIDTH
    sub = min(OUT_PROJ_SUB_ROWS, tm)
    assert m % tm == 0 and tm % sub == 0 and sub % c == 0 and w.shape[0] == gw + MEM_W
    col = lambda off: pl.BlockSpec((tm, gw), lambda i: (i, off // gw))
    mem_specs, mem_args = _mem_specs(proj, O_MQ, O_MG, memkv, tm, w.shape[0], d)
    return pl.pallas_call(
        _gmlp_out_proj_kernel,
        grid=(m // tm,),
        in_specs=[col(O_U), col(O_V), col(O_Z),
                  pl.BlockSpec((1, gw), lambda i: (0, 0)),
                  pl.BlockSpec(w_mix.shape, lambda i: (0, 0, 0)),
                  pl.BlockSpec((c, gw), lambda i: (0, 0)),
                  pl.BlockSpec((gw, d), lambda i: (0, 0), pipeline_mode=pl.Buffered(1))]
                 + mem_specs
                 + [pl.BlockSpec((1, d), lambda i: (0, 0)),
                    pl.BlockSpec((tm, d), lambda i: (i, 0))],
        out_specs=[pl.BlockSpec((tm, d), lambda i: (i, 0)),
                   pl.BlockSpec((c, gw), lambda i: (0, 0))],
        out_shape=[jax.ShapeDtypeStruct((m, d), F32),
                   jax.ShapeDtypeStruct((c, gw), F32)],
        scratch_shapes=[pltpu.VMEM(w_mix.shape, BF16), pltpu.VMEM((sub, gw), BF16)],
        compiler_params=_params("arbitrary"),
        name="gmlp_out_proj",
    )(proj, proj, proj, gmlp_ln.reshape(1, gw), w_mix, bias_rows, w, *mem_args, w, g.reshape(1, d), x)


def _pad_steps(a, pad):
    nseq, seq, n = a.shape
    return jnp.pad(a, ((0, 0), (0, pad - seq), (0, 0))).reshape(nseq * pad, n)


def kernel(x_prompt, x_sample, state_ret, cache_swa_k, cache_swa_v, cache_mem_k, cache_mem_v, mem_prompt,
           pre_norm, post_norm, mem_norm, w_mem_k, w_mem_v, w_in_even, ret_gn, w_out_even,
           w_in_odd, gmlp_ln, w_spatial, b_spatial, w_out_odd):
    bp, tp, d = x_prompt.shape
    nseq, seq, _ = x_sample.shape
    pad = SAMPLE_PAD
    rows_s = nseq * pad
    c = GMLP_CHUNK
    assert bp == 1 and seq <= pad and rows_s % c == 0 and c % pad == 0
    hp = x_prompt.reshape(tp, d)
    hs = _pad_steps(x_sample, pad)
    mem = mem_prompt.reshape(MEM_LEN, d)
    tm_p = 1024 if tp % 1024 == 0 else 512
    unpad = lambda a: a.reshape(nseq, pad, -1)[:, :seq]

    pos_pad = np.concatenate([PAST_LEN + np.arange(seq), np.zeros((pad - seq,), np.int64)])
    ret_cos_p, ret_sin_p = _rot_tables_range(tp, RET_DK, RET_THETA, RET_DK)
    ret_cos_s, ret_sin_s = _rot_tables(np.tile(pos_pad, nseq), RET_DK, RET_THETA, RET_DK)
    swa_cos_p, swa_sin_p = _rot_tables_range(tp, ROPE_DIMS, ROPE_THETA, SWA_HD)
    swa_cos_s, swa_sin_s = _rot_tables(pos_pad, ROPE_DIMS, ROPE_THETA, SWA_HD)
    swa_cos_s = np.tile(swa_cos_s, (1, SWA_HEADS))
    swa_sin_s = np.tile(swa_sin_s, (1, SWA_HEADS))

    w_mem0 = jnp.concatenate([w_mem_k[0], w_mem_v[0]], axis=1).astype(BF16)
    memkv0 = _norm_matmul(mem, mem_norm[0], w_mem0, MEM_LEN, 2 * MEM_W)

    proj_s, w_in0 = _norm_matmul_cast(hs, pre_norm[0], w_in_even[0], 1024)
    proj_s3 = proj_s.reshape(nseq, pad, EVEN_IN)
    ya_s, st_s = _ret_sample(proj_s, state_ret[0], ret_cos_s, ret_sin_s, ret_gn[0], nseq, seq, pad)
    yb_s, sk = _swa_sample(proj_s3, cache_swa_k[0], cache_swa_v[0], swa_cos_s, swa_sin_s, seq)
    sk = sk[:, :seq].reshape(nseq, seq, SWA_HEADS, SWA_HD)
    sv = unpad(proj_s[:, E_BV:E_BV + SWA_HEADS * SWA_HD]).reshape(nseq, seq, SWA_HEADS, SWA_HD)

    buf_p = min(SWA_MAX_WINDOW, tp)
    proj_p, kv_last, w_out0, w_in1, w_out1 = _norm_matmul(
        hp, pre_norm[0], w_in0, tm_p, 1024, BF16, (tp - buf_p, E_BK, E_BG - E_BK),
        casts=[(w_out_even[0], 256), (w_in_odd[0], 1024), (w_out_odd[0], 256)],
        rot=(swa_cos_p, swa_sin_p, E_BQ, E_BK, SWA_HD ** -0.5 * LOG2E))
    hs1 = _mem_out_proj_rows(proj_s3, E_MQ, E_MG, cache_mem_k, cache_mem_v, 0,
                             [ya_s, yb_s.reshape(rows_s, -1)], w_out0, post_norm[0], hs)
    ya_p, st_p = _ret_prompt(proj_p, ret_cos_p, ret_sin_p, ret_gn[0])
    yb_p = _swa_prompt(proj_p)
    pk = kv_last[:, :E_BV - E_BK]
    hp1 = _out_proj([ya_p, yb_p], w_out0, post_norm[0], hp, 512, mem=(proj_p, E_MQ, E_MG, memkv0))
    pv = kv_last[:, E_BV - E_BK:]

    w_mem1 = jnp.concatenate([w_mem_k[1], w_mem_v[1]], axis=1).astype(BF16)
    memkv1 = _norm_matmul(mem, mem_norm[1], w_mem1, MEM_LEN, 2 * MEM_W)
    bias_p = jnp.repeat(b_spatial[0].T, GMLP_GROUP, axis=1)
    blockdiag = lambda wg: jnp.kron(jnp.eye(c // pad, dtype=F32),
                                    jnp.pad(wg[:seq, :seq], ((0, pad - seq), (0, pad - seq))))
    w_mix_s = jax.vmap(blockdiag)(w_spatial[0])
    bias_s = jnp.tile(jnp.pad(bias_p[:seq], ((0, pad - seq), (0, 0))), (c // pad, 1))

    proj_s1 = _norm_matmul(hs1, pre_norm[1], w_in1, rows_s, 1024)
    yc_s, gv_s = _gmlp(proj_s1, gmlp_ln[0], w_mix_s, bias_s, True, False)
    hs2 = _mem_out_proj_rows(proj_s1.reshape(nseq, pad, ODD_IN), O_MQ, O_MG, cache_mem_k, cache_mem_v, 1,
                             [yc_s], w_out1, post_norm[1], hs1)

    proj_p1 = _norm_matmul(hp1, pre_norm[1], w_in1, tm_p, 1024, BF16,
                           acts=[(O_U, O_Z, "gelu"), (O_Z, O_MQ, "silu")])
    hp2, gv_p = _gmlp_out_proj(proj_p1, memkv1, gmlp_ln[0], w_spatial[0], bias_p, w_out1, post_norm[1], hp1, 512)

    memkv = jnp.stack([memkv0, memkv1])
    p_mk = memkv[:, :, :MEM_W].reshape(2, bp, MEM_LEN, MEM_HEADS, MEM_HD)
    p_mv = memkv[:, :, MEM_W:].reshape(2, bp, MEM_LEN, MEM_HEADS, MEM_HD)
    return (hp2.reshape(bp, tp, d), unpad(hs2),
            st_p[None, None],
            pk.reshape(1, bp, buf_p, SWA_HEADS, SWA_HD), pv.reshape(1, bp, buf_p, SWA_HEADS, SWA_HD),
            p_mk, p_mv,
            gv_p[None, None],
            st_s[None],
            sk[None], sv[None],
            unpad(gv_s)[None])
```

```python
import functools
import math

import jax
import jax.numpy as jnp
import numpy as np
from jax import lax
from jax.experimental import pallas as pl
from jax.experimental.pallas import tpu as pltpu

F32 = jnp.float32
BF16 = jnp.bfloat16

D_MODEL = 2048
PAST_LEN = 8192
RET_HEADS = 8
RET_DK = 128
RET_DV = 256
RET_CHUNK = 128
RET_THETA = 10000.0
SWA_HEADS = 8
SWA_HD = 128
SWA_PATTERNS = ((128, 1), (512, 4), (2048, 16))
SWA_MAX_WINDOW = 2048
SWA_BLOCK = 128
ROPE_THETA = 500000.0
ROPE_DIMS = SWA_HD // 4
GMLP_CHUNK = 128
GMLP_WIDTH = D_MODEL
GMLP_GROUP = 128
GMLP_GROUPS = GMLP_WIDTH // GMLP_GROUP
MEM_LEN = 256
MEM_HEADS = 4
MEM_HD = 128
MEM_W = MEM_HEADS * MEM_HD
NORM_EPS = 1e-6
NEG_INF = -1e30
LOG2E = math.log2(math.e)

E_AQ = 0
E_AK = E_AQ + RET_HEADS * RET_DK
E_AV = E_AK + RET_HEADS * RET_DK
E_AG = E_AV + RET_HEADS * RET_DV
E_BQ = E_AG + RET_HEADS * RET_DV
E_BK = E_BQ + SWA_HEADS * SWA_HD
E_BV = E_BK + SWA_HEADS * SWA_HD
E_BG = E_BV + SWA_HEADS * SWA_HD
E_MQ = E_BG + SWA_HEADS * SWA_HD
E_MG = E_MQ + MEM_W
EVEN_IN = E_MG + MEM_W
O_U = 0
O_V = O_U + GMLP_WIDTH
O_Z = O_V + GMLP_WIDTH
O_MQ = O_Z + GMLP_WIDTH
O_MG = O_MQ + MEM_W
ODD_IN = O_MG + MEM_W

SWA_TILE = SWA_MAX_WINDOW
SAMPLE_PAD = 8
GMLP_STEP_CHUNKS = 2
SWA_GROUP = 1
RET_PROMPT_CHUNK = 256
RET_STEP_CHUNKS = 1
MEM_ROWS_STEP_SEQS = 4
RET_STEP_SEQS = 4
OUT_PROJ_SUB_ROWS = 256
NORM_SUB_ROWS = 256
VMEM_LIMIT = 56 * 1024 * 1024


def _params(*sem):
    return pltpu.CompilerParams(dimension_semantics=sem, vmem_limit_bytes=VMEM_LIMIT)


def _dot(a, b):
    return jnp.dot(a, b, preferred_element_type=F32)


def _dot_nt(a, b):
    return lax.dot_general(a, b, (((1,), (1,)), ((), ())), preferred_element_type=F32)


def _silu(x):
    return x / (1.0 + jnp.exp2(x * (-LOG2E)))


def _gelu(x):
    a = -2.0 * math.sqrt(2.0 / math.pi) * LOG2E
    return x / (1.0 + jnp.exp2(x * (a + (a * 0.044715) * (x * x))))


def _div(x, n):
    assert n & (n - 1) == 0
    return jnp.right_shift(x, n.bit_length() - 1)


def _mod(x, n):
    assert n & (n - 1) == 0
    return jnp.bitwise_and(x, n - 1)


def _rot_lanes(n_rot, theta, width):
    half = n_rot // 2
    inv = 1.0 / (theta ** (np.arange(half, dtype=np.float64) / half))
    rest = np.zeros(width - n_rot)
    return np.concatenate([inv, inv, rest]), np.concatenate([-np.ones(half), np.ones(half), rest])


def _rot_tables(pos, n_rot, theta, width):
    f, sgn = _rot_lanes(n_rot, theta, width)
    ang = np.asarray(pos, np.float64)[:, None] * f[None, :]
    return np.cos(ang).astype(np.float32), (sgn * np.sin(ang)).astype(np.float32)


def _rot_tables_range(n, n_rot, theta, width, blk=128):
    assert n % blk == 0
    f, sgn = _rot_lanes(n_rot, theta, width)
    hi = (np.arange(n // blk, dtype=np.float64) * blk)[:, None] * f[None, :]
    lo = np.arange(blk, dtype=np.float64)[:, None] * f[None, :]
    dev = lambda a: jnp.asarray(a, F32)
    ch, sh, cl, sl = dev(np.cos(hi)[:, None]), dev(np.sin(hi)[:, None]), dev(np.cos(lo)[None]), dev(np.sin(lo)[None])
    chs, shs = dev((sgn * np.cos(hi))[:, None]), dev((sgn * np.sin(hi))[:, None])
    return (ch * cl - sh * sl).reshape(n, width), (shs * cl + chs * sl).reshape(n, width)


def _rot_full(x, cos, sin):
    return x * cos + pltpu.roll(x, RET_DK // 2, 1) * sin


def _rot_partial(x, cos, sin):
    half = ROPE_DIMS // 2
    lane = _mod(lax.broadcasted_iota(jnp.int32, x.shape, 1), SWA_HD)
    width = x.shape[1]
    partner = jnp.where(lane < half, pltpu.roll(x, width - half, 1), pltpu.roll(x, half, 1))
    return x * cos + partner * sin


def _norm_matmul_kernel(x_ref, g_ref, w_ref, *rest, side, cast_cols, rot, acts, ntiles):
    if rot is not None:
        cos_ref, sin_ref, rest = rest[0], rest[1], rest[2:]
    nc = len(cast_cols)
    cast_in, o_ref, rest = rest[:nc], rest[nc], rest[nc + 1:]
    cast_out, xn_ref = rest[len(rest) - 1 - nc:len(rest) - 1], rest[-1]
    tm = x_ref.shape[0]
    sub = min(NORM_SUB_ROWS, tm)

    for src, dst, ncol in zip(cast_in, cast_out, cast_cols):
        @pl.when(pl.program_id(1) < ncol)
        def _(src=src, dst=dst):
            dst[...] = src[...].astype(BF16)

    kind = {}
    if rot is not None:
        jq, jk, q_scale = rot
        kind[jq], kind[jk] = "rope_q", "rope_k"
    for lo, hi, name in acts:
        kind.update({t: name for t in range(lo, hi)})

    def epilogue(name):
        if name in ("rope_q", "rope_k"):
            nrep = o_ref.shape[1] // cos_ref.shape[1]
            cos, sin = jnp.tile(cos_ref[...], (1, nrep)), jnp.tile(sin_ref[...], (1, nrep))
            return (lambda acc: _rot_partial(acc * q_scale, cos, sin)) if name == "rope_q" else (
                lambda acc: _rot_partial(acc, cos, sin))
        return {"gelu": _gelu, "silu": _silu, None: lambda acc: acc}[name]

    j = pl.program_id(1)

    @pl.when(j == 0)
    def _():
        first = epilogue(kind.get(0))
        for r0 in range(0, tm, sub):
            rs = slice(r0, r0 + sub)
            x = x_ref[rs, :]
            ms = jnp.mean(x * x, axis=-1, keepdims=True)
            xn = (x * lax.rsqrt(ms + NORM_EPS) * g_ref[...]).astype(BF16)
            xn_ref[rs, :] = xn
            o_ref[rs, :] = first(_dot(xn, w_ref[...])).astype(o_ref.dtype)

    def tail(name):
        acc = epilogue(name)(_dot(xn_ref[...], w_ref[...]))
        o_ref[...] = acc.astype(o_ref.dtype)
        if side is not None:
            i0, j0, nj = side
            i = pl.program_id(0)

            @pl.when((i >= i0) & (j >= j0) & (j < j0 + nj))
            def _():
                rest[0][...] = acc

    for name in sorted({kind.get(t) for t in range(1, ntiles)}, key=str):
        tiles = [t for t in range(1, ntiles) if kind.get(t) == name]
        pl.when(functools.reduce(jnp.logical_or, [j == t for t in tiles]))(functools.partial(tail, name))


def _norm_matmul(x, g, w, tm, tn, out_dtype=F32, side=None, casts=(), rot=None, acts=()):
    m, d = x.shape
    n = w.shape[1]
    ni, nj_grid = m // tm, n // tn
    out_specs = [pl.BlockSpec((tm, tn), lambda i, j: (i, j))]
    out_shape = [jax.ShapeDtypeStruct((m, n), out_dtype)]
    tiles = None
    if side is not None:
        row0, col0, ncols = side
        assert row0 % tm == 0 and col0 % tn == 0 and ncols % tn == 0 and col0 >= tn
        i0, j0, nj = row0 // tm, col0 // tn, ncols // tn
        tiles = (i0, j0, nj)
        out_specs.append(pl.BlockSpec(
            (tm, tn), lambda i, j: (jnp.maximum(i - i0, 0), jnp.where(i >= i0, jnp.clip(j - j0, 0, nj - 1), 0))))
        out_shape.append(jax.ShapeDtypeStruct((m - row0, ncols), F32))
    rot_tiles, rot_specs, rot_args = None, [], []
    if rot is not None:
        cos, sin, q_col, k_col, q_scale = rot
        assert q_col % tn == 0 and k_col % tn == 0 and q_col >= tn and k_col >= tn and tn % cos.shape[1] == 0
        rot_tiles = (q_col // tn, k_col // tn, q_scale)
        rot_specs = [pl.BlockSpec((tm, cos.shape[1]), lambda i, j: (i, 0))] * 2
        rot_args = [cos, sin]
    assert all(lo % tn == 0 and hi % tn == 0 for lo, hi, _ in acts)
    act_tiles = tuple((lo // tn, hi // tn, name) for lo, hi, name in acts)
    cast_specs, cast_cols = [], []
    for a, bc in casts:
        r, c = a.shape
        assert r % ni == 0 and (r // ni) % 16 == 0 and c % bc == 0 and c // bc <= nj_grid
        ncol = c // bc
        spec = pl.BlockSpec((r // ni, bc), functools.partial(lambda i, j, last: (i, jnp.minimum(j, last)), last=ncol - 1))
        cast_specs.append(spec)
        cast_cols.append(ncol)
        out_specs.append(spec)
        out_shape.append(jax.ShapeDtypeStruct((r, c), BF16))
    res = pl.pallas_call(
        functools.partial(_norm_matmul_kernel, side=tiles, cast_cols=tuple(cast_cols), rot=rot_tiles,
                          acts=act_tiles, ntiles=nj_grid),
        grid=(ni, nj_grid),
        in_specs=[pl.BlockSpec((tm, d), lambda i, j: (i, 0)),
                  pl.BlockSpec((1, d), lambda i, j: (0, 0)),
                  pl.BlockSpec((d, tn), lambda i, j: (0, j))] + rot_specs + cast_specs,
        out_specs=out_specs,
        out_shape=out_shape,
        scratch_shapes=[pltpu.VMEM((tm, d), BF16)],
        compiler_params=_params("arbitrary", "arbitrary"),
        name="norm_matmul",
    )(x, g.reshape(1, d), w, *rot_args, *[a for a, _ in casts])
    return res if len(res) > 1 else res[0]


def _norm_matmul_cast_kernel(x_ref, g_ref, w_ref, o_ref, wb_ref, xn_ref):
    @pl.when(pl.program_id(0) == 0)
    def _():
        x = x_ref[...]
        ms = jnp.mean(x * x, axis=-1, keepdims=True)
        xn_ref[...] = (x * lax.rsqrt(ms + NORM_EPS) * g_ref[...]).astype(BF16)

    wb = w_ref[...].astype(BF16)
    wb_ref[...] = wb
    o_ref[...] = _dot(xn_ref[...], wb).astype(o_ref.dtype)


def _norm_matmul_cast(x, g, w, tn):
    m, d = x.shape
    n = w.shape[1]
    return pl.pallas_call(
        _norm_matmul_cast_kernel,
        grid=(n // tn,),
        in_specs=[pl.BlockSpec((m, d), lambda j: (0, 0)),
                  pl.BlockSpec((1, d), lambda j: (0, 0)),
                  pl.BlockSpec((d, tn), lambda j: (0, j))],
        out_specs=[pl.BlockSpec((m, tn), lambda j: (0, j)),
                   pl.BlockSpec((d, tn), lambda j: (0, j))],
        out_shape=[jax.ShapeDtypeStruct((m, n), F32),
                   jax.ShapeDtypeStruct((d, n), BF16)],
        scratch_shapes=[pltpu.VMEM((m, d), BF16)],
        compiler_params=_params("arbitrary"),
        name="norm_matmul_cast",
    )(x, g.reshape(1, d), w)


def _ret_tables(c, reps, pad=None):
    pad = c if pad is None else pad
    rows = pad * reps
    lg = np.log1p(-np.exp2(-5.0 - np.arange(RET_HEADS, dtype=np.float64)))
    row = np.arange(rows)
    step = (row % pad).astype(np.float64)
    live = (row % pad) < c
    rel = step[:, None] - step[None, :]
    keep = ((row // pad)[:, None] == (row // pad)[None, :]) & (rel >= 0) & live[:, None] & live[None, :]
    decay = np.where(keep[None], np.exp(lg[:, None, None] * np.maximum(rel, 0.0)[None]), 0.0)
    q_decay = np.where(live[None], np.exp(lg[:, None] * (step[None, :] + 1.0)), 0.0)
    k_decay = np.where(live[None], np.exp(lg[:, None] * (c - 1.0 - step[None, :])), 0.0)
    chunk_decay = np.exp(lg * c)
    f32 = lambda a, shape: np.ascontiguousarray(np.broadcast_to(a, shape), dtype=np.float32)
    qd = f32(q_decay[:, :, None], (RET_HEADS, rows, RET_DK))
    kd = f32(k_decay[:, :, None], (RET_HEADS, rows, RET_DK))
    cd = f32(chunk_decay[:, None, None], (RET_HEADS, 1, RET_DV))
    return decay.astype(np.float32), qd, kd, cd


def _group_norm_gate(o, gn, g):
    mu = jnp.mean(o, axis=-1, keepdims=True)
    d = o - mu
    var = jnp.mean(d * d, axis=-1, keepdims=True)
    return d * lax.rsqrt(var + NORM_EPS) * gn * _silu(g.astype(F32))


def _ret_prompt_kernel(q_ref, k_ref, v_ref, g_ref, cos_ref, sin_ref, dec_ref, qd_ref, kd_ref, cd_ref, gn_ref,
                       y_ref, st_ref):
    @pl.when(pl.program_id(0) == 0)
    def _():
        st_ref[...] = jnp.zeros_like(st_ref)

    c = dec_ref.shape[1]
    for i in range(q_ref.shape[0] // c):
        rs = slice(i * c, (i + 1) * c)
        cos = cos_ref[rs, :]
        sin = sin_ref[rs, :]
        for h in range(RET_HEADS):
            ks = slice(h * RET_DK, (h + 1) * RET_DK)
            vs = slice(h * RET_DV, (h + 1) * RET_DV)
            q = _rot_full(q_ref[rs, ks].astype(F32), cos, sin)
            k = _rot_full(k_ref[rs, ks].astype(F32) * (RET_DK ** -0.5), cos, sin)
            v = v_ref[rs, vs].astype(BF16)
            st = st_ref[h]
            s = _dot_nt(q.astype(BF16), k.astype(BF16)) * dec_ref[h]
            lhs = jnp.concatenate([s.astype(BF16), (q * qd_ref[h]).astype(BF16)], axis=1)
            o = _dot(lhs, jnp.concatenate([v, st.astype(BF16)], axis=0))
            st_ref[h] = cd_ref[h] * st + _dot((k * kd_ref[h]).T.astype(BF16), v)
            y_ref[rs, vs] = _group_norm_gate(o, gn_ref[:, vs], g_ref[rs, vs]).astype(y_ref.dtype)


def _ret_prompt(proj, cos, sin, ret_gn):
    t = proj.shape[0]
    dec, qd, kd, cd = _ret_tables(RET_PROMPT_CHUNK, 1)
    c = RET_STEP_CHUNKS * RET_PROMPT_CHUNK
    assert t % c == 0
    qw = RET_HEADS * RET_DK
    vw = RET_HEADS * RET_DV
    full3 = lambda n: (0, 0, 0)
    return pl.pallas_call(
        _ret_prompt_kernel,
        grid=(t // c,),
        in_specs=[pl.BlockSpec((c, qw), lambda n: (n, E_AQ // qw)),
                  pl.BlockSpec((c, qw), lambda n: (n, E_AK // qw)),
                  pl.BlockSpec((c, vw), lambda n: (n, E_AV // vw)),
                  pl.BlockSpec((c, vw), lambda n: (n, E_AG // vw)),
                  pl.BlockSpec((c, RET_DK), lambda n: (n, 0)),
                  pl.BlockSpec((c, RET_DK), lambda n: (n, 0)),
                  pl.BlockSpec(dec.shape, full3),
                  pl.BlockSpec(qd.shape, full3),
                  pl.BlockSpec(kd.shape, full3),
                  pl.BlockSpec(cd.shape, full3),
                  pl.BlockSpec((1, vw), lambda n: (0, 0))],
        out_specs=[pl.BlockSpec((c, vw), lambda n: (n, 0)),
                   pl.BlockSpec((RET_HEADS, RET_DK, RET_DV), full3)],
        out_shape=[jax.ShapeDtypeStruct((t, vw), BF16),
                   jax.ShapeDtypeStruct((RET_HEADS, RET_DK, RET_DV), F32)],
        compiler_params=_params("arbitrary"),
        name="ret_prompt",
    )(proj, proj, proj, proj, cos, sin, dec, qd, kd, cd, ret_gn.reshape(1, vw))


def _ret_sample_kernel(q_ref, k_ref, v_ref, g_ref, cos_ref, sin_ref, dec_ref, qd_ref, kd_ref, cd_ref, gn_ref,
                       st_ref, y_ref, sto_ref, cross_ref, qs_ref, kt_ref, *, pad):
    b = pl.program_id(0)
    nb = pl.num_programs(0)
    win = RET_CHUNK

    @pl.when(b == 0)
    def _():
        cross_ref[...] = jnp.zeros_like(cross_ref)
        cos = cos_ref[...]
        sin = sin_ref[...]
        for h in range(RET_HEADS):
            ks = slice(h * RET_DK, (h + 1) * RET_DK)
            qs_ref[:, ks] = _rot_full(q_ref[:, ks], cos, sin) * qd_ref[h]
            kt_ref[h] = (_rot_full(k_ref[:, ks] * (RET_DK ** -0.5), cos, sin) * kd_ref[h]).T

    for i in range(st_ref.shape[0]):
        sq = b * st_ref.shape[0] + i
        w0 = pl.multiple_of(jnp.right_shift(sq * pad, win.bit_length() - 1) * win, win)
        wrows = pl.ds(w0, win)
        rows = lax.broadcasted_iota(jnp.int32, (win, RET_DK), 0) + w0
        cols = lax.broadcasted_iota(jnp.int32, (RET_DK, win), 1) + w0
        mine_r = (rows >= sq * pad) & (rows < (sq + 1) * pad)
        mine_c = (cols >= sq * pad) & (cols < (sq + 1) * pad)
        for h in range(RET_HEADS):
            ks = slice(h * RET_DK, (h + 1) * RET_DK)
            vs = slice(h * RET_DV, (h + 1) * RET_DV)
            st = st_ref[i, h]
            qm = jnp.where(mine_r, qs_ref[wrows, ks], 0.0)
            kmt = jnp.where(mine_c, kt_ref[h, :, wrows], 0.0)
            cross_ref[wrows, vs] += _dot(qm.astype(BF16), st.astype(BF16))
            sto_ref[i, h] = cd_ref[h] * st + _dot(kmt.astype(BF16), v_ref[wrows, vs].astype(BF16))

    @pl.when(b == nb - 1)
    def _():
        cos_all = cos_ref[...]
        sin_all = sin_ref[...]
        for h in range(RET_HEADS):
            ks = slice(h * RET_DK, (h + 1) * RET_DK)
            vs = slice(h * RET_DV, (h + 1) * RET_DV)
            q = _rot_full(q_ref[:, ks], cos_all, sin_all)
            k = _rot_full(k_ref[:, ks] * (RET_DK ** -0.5), cos_all, sin_all)
            v = v_ref[:, vs].astype(BF16)
            s = _dot_nt(q.astype(BF16), k.astype(BF16)) * dec_ref[h]
            o = _dot(s.astype(BF16), v) + cross_ref[:, vs]
            y_ref[:, vs] = _group_norm_gate(o, gn_ref[:, vs], g_ref[:, vs]).astype(y_ref.dtype)


def _ret_sample(proj, state, cos, sin, ret_gn, nseq, seq, pad):
    rows = nseq * pad
    c = math.gcd(seq, RET_CHUNK)
    assert c == seq, "sample chunk must cover the new tokens"
    assert rows % RET_CHUNK == 0 and RET_CHUNK % pad == 0
    dec, qd, kd, cd = _ret_tables(c, nseq, pad)
    qw = RET_HEADS * RET_DK
    vw = RET_HEADS * RET_DV
    full3 = lambda b: (0, 0, 0)
    n = RET_STEP_SEQS
    assert nseq % n == 0
    st_spec = pl.BlockSpec((n, RET_HEADS, RET_DK, RET_DV), lambda b: (b, 0, 0, 0))
    return pl.pallas_call(
        functools.partial(_ret_sample_kernel, pad=pad),
        grid=(nseq // n,),
        in_specs=[pl.BlockSpec((rows, qw), lambda b: (0, E_AQ // qw)),
                  pl.BlockSpec((rows, qw), lambda b: (0, E_AK // qw)),
                  pl.BlockSpec((rows, vw), lambda b: (0, E_AV // vw)),
                  pl.BlockSpec((rows, vw), lambda b: (0, E_AG // vw)),
                  pl.BlockSpec((rows, RET_DK), lambda b: (0, 0)),
                  pl.BlockSpec((rows, RET_DK), lambda b: (0, 0)),
                  pl.BlockSpec(dec.shape, full3),
                  pl.BlockSpec(qd.shape, full3),
                  pl.BlockSpec(kd.shape, full3),
                  pl.BlockSpec(cd.shape, full3),
                  pl.BlockSpec((1, vw), lambda b: (0, 0)),
                  st_spec],
        out_specs=[pl.BlockSpec((rows, vw), lambda b: (0, 0)), st_spec],
        out_shape=[jax.ShapeDtypeStruct((rows, vw), F32),
                   jax.ShapeDtypeStruct(state.shape, F32)],
        scratch_shapes=[pltpu.VMEM((rows, vw), F32),
                        pltpu.VMEM((rows, qw), F32),
                        pltpu.VMEM((RET_HEADS, RET_DK, rows), F32)],
        compiler_params=_params("arbitrary"),
        name="ret_sample",
    )(proj, proj, proj, proj, cos, sin, dec, qd, kd, cd, ret_gn.reshape(1, vw), state)


def _swa_prompt_kernel(q_ref, kc_ref, vc_ref, g_ref, y_ref, q_s, k_s, v_s, m_s, l_s, acc_s):
    t = pl.program_id(1)
    tb = SWA_TILE
    blk = SWA_BLOCK
    hd = SWA_HD

    @pl.when(t == 0)
    def _():
        k_s[0:tb, :] = jnp.zeros((tb, hd), F32)
        v_s[0:tb, :] = jnp.zeros((tb, hd), F32)

    @pl.when(t > 0)
    def _():
        k_s[0:tb, :] = k_s[tb:2 * tb, :]
        v_s[0:tb, :] = v_s[tb:2 * tb, :]

    q_s[...] = q_ref[...].astype(F32)
    k_s[tb:2 * tb, :] = kc_ref[...].astype(F32)
    v_s[tb:2 * tb, :] = vc_ref[...].astype(F32)

    def ds(start, size, stride):
        return pl.ds(start, size) if stride == 1 else pl.ds(start, size, stride=stride)

    def band(g):
        qi = lax.broadcasted_iota(jnp.int32, (g * blk, (g + 1) * blk), 0)
        kj = lax.broadcasted_iota(jnp.int32, (g * blk, (g + 1) * blk), 1)
        ok = (kj >= qi) & (kj <= qi + blk)
        return ok, ok & ((kj >= blk) | (t > 0))

    for bi, (window, r) in enumerate(SWA_PATTERNS):
        assert window // r == blk
        per_class = tb // (blk * r)
        g = SWA_GROUP if per_class % SWA_GROUP == 0 else 1
        ok, ok_first = band(g)
        ones = jnp.ones(((g + 1) * blk, hd), BF16)
        for c in range(r):
            for n in range(0, per_class, g):
                qs = n * blk * r + c
                ks = qs + tb - blk * r
                qrows = ds(qs, g * blk, r)
                krows = ds(ks, (g + 1) * blk, r)
                s = _dot_nt(q_s[qrows, :].astype(BF16), k_s[krows, :].astype(BF16))
                s = jnp.where(ok_first if n == 0 else ok, s, NEG_INF)
                m = jnp.max(s, axis=-1, keepdims=True)
                p = jnp.exp2(s - m).astype(BF16)
                pv = _dot(p, jnp.concatenate([v_s[krows, :].astype(BF16), ones], axis=1))
                m_s[bi, qrows, :] = jnp.broadcast_to(m, (g * blk, hd))
                l_s[bi, qrows, :] = pv[:, hd:]
                acc_s[bi, qrows, :] = pv[:, :hd]

    nbr = len(SWA_PATTERNS)
    ms = [m_s[i] for i in range(nbr)]
    mmax = functools.reduce(jnp.maximum, ms)
    wts = [jnp.exp2(mi - mmax) for mi in ms]
    num = sum(wts[i] * acc_s[i] for i in range(nbr))
    den = sum(wts[i] * l_s[i] for i in range(nbr))
    y_ref[...] = (num / den * _silu(g_ref[...].astype(F32))).astype(y_ref.dtype)


def _swa_prompt(proj):
    t = proj.shape[0]
    tb = SWA_TILE
    assert t % tb == 0
    hd = SWA_HD
    nbr = len(SWA_PATTERNS)
    cur = lambda off: pl.BlockSpec((tb, hd), lambda h, i: (i, off // hd + h))
    return pl.pallas_call(
        _swa_prompt_kernel,
        grid=(SWA_HEADS, t // tb),
        in_specs=[cur(E_BQ), cur(E_BK), cur(E_BV), cur(E_BG)],
        out_specs=pl.BlockSpec((tb, hd), lambda h, i: (i, h)),
        out_shape=jax.ShapeDtypeStruct((t, SWA_HEADS * hd), BF16),
        scratch_shapes=[pltpu.VMEM((tb, hd), F32),
                        pltpu.VMEM((2 * tb, hd), F32),
                        pltpu.VMEM((2 * tb, hd), F32),
                        pltpu.VMEM((nbr, tb, hd), F32),
                        pltpu.VMEM((nbr, tb, hd), F32),
                        pltpu.VMEM((nbr, tb, hd), F32)],
        compiler_params=_params("arbitrary", "arbitrary"),
        name="swa_prompt",
    )(proj, proj, proj, proj)


def _swa_sample_kernel(q_ref, k_ref, v_ref, g_ref, cos_ref, sin_ref, ka_ref, kb_ref, va_ref, vb_ref,
                       y_ref, ko_ref, kn_s, vn_s, ba_s, bb_s, bn_s, *, seq):
    pad = SAMPLE_PAD
    nh = SWA_HEADS
    hd = SWA_HD
    nr = nh * pad
    nj, nrh, _ = kb_ref.shape[1:]
    na = ka_ref.shape[1]
    nb = nj * nrh
    nn = kn_s.shape[0]

    @pl.when(pl.program_id(0) == 0)
    def _():
        (w1, d1), (w2, d2), (w3, d3) = SWA_PATTERNS

        def rows_cols(n):
            r = lax.broadcasted_iota(jnp.int32, (nr, n), 0)
            c = lax.broadcasted_iota(jnp.int32, (nr, n), 1)
            return _div(r, pad), _mod(r, pad), c

        def log2_mult(mult):
            return jnp.where(mult > 0, jnp.log2(jnp.maximum(mult, 1.0)), NEG_INF)

        hq, tq, c = rows_cols(na)
        dist = na // nh + tq - _div(c, nh)
        mult_a = ((dist <= w1).astype(F32) + ((_mod(dist, d2) == 0) & (dist <= w2)).astype(F32)
                  + ((_mod(dist, d3) == 0) & (dist <= w3)).astype(F32))
        ba_s[...] = log2_mult(jnp.where((hq == _mod(c, nh)) & (tq < seq), mult_a, 0.0))
        hq, tq, c = rows_cols(nb)
        bb_s[...] = log2_mult(jnp.where((hq == _mod(c, nh)) & (tq == _mod(_div(c, nh), nrh // nh)), 1.0, 0.0))
        hq, tq, c = rows_cols(nn)
        tk = _mod(c, pad)
        mult_n = (tk <= tq).astype(F32) + 2.0 * (tk == tq).astype(F32)
        bn_s[...] = log2_mult(jnp.where((hq == _div(c, pad)) & (tq < seq) & (tk < seq), mult_n, 0.0))
        kn_s[...] = jnp.zeros_like(kn_s)
        vn_s[...] = jnp.zeros_like(vn_s)

    cos = cos_ref[...]
    sin = sin_ref[...]
    q8 = _rot_partial(q_ref[0] * (SWA_HD ** -0.5 * LOG2E), cos, sin)
    k8 = _rot_partial(k_ref[0], cos, sin)
    ko_ref[0] = k8

    def head_rows(x):
        return jnp.concatenate([x[:, h * hd:(h + 1) * hd] for h in range(nh)], axis=0)

    qrows = head_rows(q8).astype(BF16)
    kn_s[0:nr, :] = head_rows(k8).astype(BF16)
    vn_s[0:nr, :] = head_rows(v_ref[0]).astype(BF16)

    kb = kb_ref[0].reshape(nb, hd)
    vb = vb_ref[0].reshape(nb, hd)
    s_a = _dot_nt(qrows, ka_ref[0].astype(BF16)) + ba_s[...]
    s_b = _dot_nt(qrows, kb.astype(BF16)) + bb_s[...]
    s_n = _dot_nt(qrows, kn_s[...]) + bn_s[...]
    rowmax = lambda s: jnp.max(s, axis=-1, keepdims=True)
    m = jnp.maximum(jnp.maximum(rowmax(s_a), rowmax(s_b)), rowmax(s_n))
    p_a, p_b, p_n = jnp.exp2(s_a - m), jnp.exp2(s_b - m), jnp.exp2(s_n - m)
    rowsum = lambda p: jnp.sum(p, axis=-1, keepdims=True)
    o = (_dot(p_a.astype(BF16), va_ref[0].astype(BF16)) + _dot(p_b.astype(BF16), vb.astype(BF16))
         + _dot(p_n.astype(BF16), vn_s[...])) / (rowsum(p_a) + rowsum(p_b) + rowsum(p_n))
    y = jnp.concatenate([o[h * pad:(h + 1) * pad] for h in range(nh)], axis=1)
    y_ref[0] = y * _silu(g_ref[0])


def _swa_sample(proj_pad, cache_k, cache_v, cos, sin, seq):
    nseq, pad, _ = proj_pad.shape
    buf = cache_k.shape[1]
    nh, hd = SWA_HEADS, SWA_HD
    w = nh * hd
    (w1, d1), (w2, d2), (w3, d3) = SWA_PATTERNS
    assert buf == w3 and d1 == 1 and seq <= d2 and w1 <= w2 and buf % w2 == 0 and buf % d3 == 0
    assert nh * pad <= SWA_BLOCK and (seq * nh) % 8 == 0
    flat = lambda a: a.reshape(nseq, buf * nh, hd)
    strided = lambda a: a.reshape(nseq, buf // d3, d3 * nh, hd)
    col = lambda off: pl.BlockSpec((1, pad, w), lambda b: (b, 0, off // w))
    tail = pl.BlockSpec((1, w2 * nh, hd), lambda b: (b, buf // w2 - 1, 0))
    resid = pl.BlockSpec((1, (buf - w2) // d3, seq * nh, hd), lambda b: (b, 0, 0, 0))
    tab = pl.BlockSpec((pad, w), lambda b: (0, 0))
    out = pl.BlockSpec((1, pad, w), lambda b: (b, 0, 0))
    return pl.pallas_call(
        functools.partial(_swa_sample_kernel, seq=seq),
        grid=(nseq,),
        in_specs=[col(E_BQ), col(E_BK), col(E_BV), col(E_BG), tab, tab, tail, resid, tail, resid],
        out_specs=[out, out],
        out_shape=[jax.ShapeDtypeStruct((nseq, pad, w), F32),
                   jax.ShapeDtypeStruct((nseq, pad, w), F32)],
        scratch_shapes=[pltpu.VMEM((SWA_BLOCK, hd), BF16), pltpu.VMEM((SWA_BLOCK, hd), BF16),
                        pltpu.VMEM((nh * pad, w2 * nh), F32),
                        pltpu.VMEM((nh * pad, (buf - w2) // d3 * seq * nh), F32),
                        pltpu.VMEM((nh * pad, SWA_BLOCK), F32)],
        compiler_params=_params("arbitrary"),
        name="swa_sample",
    )(proj_pad, proj_pad, proj_pad, proj_pad, cos, sin,
      flat(cache_k), strided(cache_k), flat(cache_v), strided(cache_v))


def _mem_attend(q, g, k, v):
    ones = jnp.ones((MEM_LEN, MEM_HD), BF16)
    outs = []
    for h in range(MEM_HEADS):
        hs = slice(h * MEM_HD, (h + 1) * MEM_HD)
        qh = (q[:, hs].astype(F32) * (MEM_HD ** -0.5 * LOG2E)).astype(BF16)
        s = _dot_nt(qh, k[:, hs].astype(BF16))
        m = jnp.max(s, axis=-1, keepdims=True)
        p = jnp.exp2(s - m).astype(BF16)
        pv = _dot(p, jnp.concatenate([v[:, hs].astype(BF16), ones], axis=1))
        o = pv[:, :MEM_HD] / pv[:, MEM_HD:]
        outs.append((o * _silu(g[:, hs].astype(F32))).astype(BF16))
    return jnp.concatenate(outs, axis=1)


def _mem_out_proj_rows_kernel(*refs, n_in):
    q_ref, g_ref, k_ref, v_ref = refs[:4]
    a_refs, w_refs = refs[4:4 + n_in], refs[4 + n_in:4 + 2 * n_in]
    wm_ref, gn_ref, x_ref, o_ref, bias_s, ym_s = refs[4 + 2 * n_in:]
    nseq, pad, _ = q_ref.shape
    nh, hd = MEM_HEADS, MEM_HD
    step = pl.program_id(0)

    @pl.when(step == 0)
    def _():
        r = lax.broadcasted_iota(jnp.int32, bias_s.shape, 0)
        c = lax.broadcasted_iota(jnp.int32, bias_s.shape, 1)
        bias_s[...] = jnp.where(_div(r, pad) == _mod(c, nh), 0.0, NEG_INF)

    for i in range(nseq):
        q = q_ref[i] * (MEM_HD ** -0.5 * LOG2E)
        qrows = jnp.concatenate([q[:, h * hd:(h + 1) * hd] for h in range(nh)], axis=0).astype(BF16)
        s = _dot_nt(qrows, k_ref[i].astype(BF16)) + bias_s[...]
        m = jnp.max(s, axis=-1, keepdims=True)
        p = jnp.exp2(s - m)
        o = _dot(p.astype(BF16), v_ref[i].astype(BF16)) / jnp.sum(p, axis=-1, keepdims=True)
        y = jnp.concatenate([o[h * pad:(h + 1) * pad] for h in range(nh)], axis=1)
        row0 = pl.multiple_of((step * nseq + i) * pad, pad)
        ym_s[pl.ds(row0, pad), :] = y * _silu(g_ref[i])

    @pl.when(step == pl.num_programs(0) - 1)
    def _():
        acc = _dot(ym_s[...].astype(BF16), wm_ref[...])
        for a_ref, w_ref in zip(a_refs, w_refs):
            acc = acc + _dot(a_ref[...].astype(BF16), w_ref[...])
        ms = jnp.mean(acc * acc, axis=-1, keepdims=True)
        o_ref[...] = x_ref[...] + acc * lax.rsqrt(ms + NORM_EPS) * gn_ref[...]


def _mem_out_proj_rows(proj_pad, q_off, g_off, mem_k, mem_v, layer, acts, w, g, x):
    b, pad, _ = proj_pad.shape
    m, d = x.shape
    mw = MEM_W
    rows = MEM_LEN * MEM_HEADS
    n = MEM_ROWS_STEP_SEQS
    assert b % n == 0 and m == b * pad
    const = lambda shape, idx: pl.BlockSpec(shape, lambda i: idx, pipeline_mode=pl.Buffered(1))
    kv = pl.BlockSpec((n, rows, MEM_HD), lambda i: (layer * (b // n) + i, 0, 0))
    w_specs, off = [], 0
    for a in acts:
        k = a.shape[1]
        assert off % k == 0
        w_specs.append(const((k, d), (off // k, 0)))
        off += k
    assert off + mw == w.shape[0] and off % mw == 0
    return pl.pallas_call(
        functools.partial(_mem_out_proj_rows_kernel, n_in=len(acts)),
        grid=(b // n,),
        in_specs=[pl.BlockSpec((n, pad, mw), lambda i: (i, 0, q_off // mw)),
                  pl.BlockSpec((n, pad, mw), lambda i: (i, 0, g_off // mw)),
                  kv, kv]
                 + [const(a.shape, (0, 0)) for a in acts] + w_specs
                 + [const((mw, d), (off // mw, 0)), const((1, d), (0, 0)), const((m, d), (0, 0))],
        out_specs=pl.BlockSpec((m, d), lambda i: (0, 0)),
        out_shape=jax.ShapeDtypeStruct((m, d), F32),
        scratch_shapes=[pltpu.VMEM((MEM_HEADS * pad, rows), F32), pltpu.VMEM((m, mw), F32)],
        compiler_params=_params("arbitrary"),
        name="mem_out_proj_rows",
    )(proj_pad, proj_pad, mem_k.reshape(-1, rows, MEM_HD), mem_v.reshape(-1, rows, MEM_HD),
      *acts, *([w] * len(acts)), w, g.reshape(1, d), x)


def _gmlp_kernel(u_ref, v_ref, z_ref, ln_ref, w_ref, b_ref, y_ref, vo_ref, wc_ref, *, activated):
    c = GMLP_CHUNK
    gelu, silu = (lambda a: a, lambda a: a) if activated else (_gelu, _silu)

    @pl.when(pl.program_id(0) == 0)
    def _():
        causal = (lax.broadcasted_iota(jnp.int32, (c, c), 0) >= lax.broadcasted_iota(jnp.int32, (c, c), 1))
        for g in range(GMLP_GROUPS):
            wc_ref[g] = jnp.where(causal, w_ref[g], 0.0).astype(BF16)

    nchunks = u_ref.shape[0] // c
    keep_all = vo_ref.shape[0] == u_ref.shape[0]
    for i in range(nchunks):
        rs = slice(i * c, (i + 1) * c)
        v = gelu(v_ref[rs, :].astype(F32))
        mu = jnp.mean(v, axis=-1, keepdims=True)
        d = v - mu
        var = jnp.mean(d * d, axis=-1, keepdims=True)
        vn = d * lax.rsqrt(var + NORM_EPS) * ln_ref[...]
        if keep_all:
            vo_ref[rs, :] = vn
        elif i == nchunks - 1:
            vo_ref[...] = vn
        for g in range(GMLP_GROUPS):
            gs = slice(g * GMLP_GROUP, (g + 1) * GMLP_GROUP)
            mixed = _dot(wc_ref[g], vn[:, gs].astype(BF16)) + b_ref[:, gs]
            y_ref[rs, gs] = (gelu(u_ref[rs, gs].astype(F32)) * mixed
                             * silu(z_ref[rs, gs].astype(F32))).astype(y_ref.dtype)


def _gmlp(proj, gmlp_ln, w_mix, bias_rows, keep_all_v, activated):
    t = proj.shape[0]
    c = GMLP_CHUNK
    w = GMLP_WIDTH
    rows = GMLP_STEP_CHUNKS * c
    assert t % rows == 0
    col = lambda off: pl.BlockSpec((rows, w), lambda n: (n, off // w))
    return pl.pallas_call(
        functools.partial(_gmlp_kernel, activated=activated),
        grid=(t // rows,),
        in_specs=[col(O_U), col(O_V), col(O_Z),
                  pl.BlockSpec((1, w), lambda n: (0, 0)),
                  pl.BlockSpec(w_mix.shape, lambda n: (0, 0, 0)),
                  pl.BlockSpec((c, w), lambda n: (0, 0))],
        out_specs=[pl.BlockSpec((rows, w), lambda n: (n, 0)),
                   pl.BlockSpec((rows, w), lambda n: (n, 0)) if keep_all_v
                   else pl.BlockSpec((c, w), lambda n: (0, 0))],
        out_shape=[jax.ShapeDtypeStruct((t, w), BF16),
                   jax.ShapeDtypeStruct((t if keep_all_v else c, w), F32)],
        scratch_shapes=[pltpu.VMEM(w_mix.shape, BF16)],
        compiler_params=_params("arbitrary"),
        name="gmlp",
    )(proj, proj, proj, gmlp_ln.reshape(1, w), w_mix, bias_rows)


def _out_proj_kernel(*refs, n_in, mem):
    a_refs, w_refs = refs[:n_in], refs[n_in:2 * n_in]
    refs = refs[2 * n_in:]
    if mem:
        (mq_ref, mg_ref, mk_ref, mv_ref, wm_ref), refs = refs[:5], refs[5:]
    g_ref, x_ref, o_ref = refs
    sub = min(OUT_PROJ_SUB_ROWS, o_ref.shape[0])
    for r0 in range(0, o_ref.shape[0], sub):
        rs = slice(r0, r0 + sub)
        acc = _dot(a_refs[0][rs, :].astype(BF16), w_refs[0][...])
        for a_ref, w_ref in zip(a_refs[1:], w_refs[1:]):
            acc = acc + _dot(a_ref[rs, :].astype(BF16), w_ref[...])
        if mem:
            acc = acc + _dot(_mem_attend(mq_ref[rs, :], mg_ref[rs, :], mk_ref[...], mv_ref[...]), wm_ref[...])
        ms = jnp.mean(acc * acc, axis=-1, keepdims=True)
        o_ref[rs, :] = x_ref[rs, :] + acc * lax.rsqrt(ms + NORM_EPS) * g_ref[...]


def _mem_specs(proj, q_off, g_off, memkv, tm, w_rows, d):
    w = MEM_W
    const = lambda shape, idx: pl.BlockSpec(shape, lambda i: idx, pipeline_mode=pl.Buffered(1))
    specs = [pl.BlockSpec((tm, w), lambda i: (i, q_off // w)), pl.BlockSpec((tm, w), lambda i: (i, g_off // w)),
             const((MEM_LEN, w), (0, 0)), const((MEM_LEN, w), (0, 1)), const((w, d), ((w_rows - w) // w, 0))]
    return specs, [proj, proj, memkv, memkv]


def _out_proj(acts, w, g, x, tm, mem=None):
    m, d = x.shape
    n_in = len(acts)
    w_specs, off = [], 0
    for a in acts:
        k = a.shape[1]
        assert off % k == 0
        w_specs.append(pl.BlockSpec((k, d), functools.partial(lambda i, blk: (blk, 0), blk=off // k),
                                    pipeline_mode=pl.Buffered(1)))
        off += k
    mem_specs, mem_args = [], []
    if mem is not None:
        mem_specs, mem_args = _mem_specs(*mem, tm, w.shape[0], d)
        mem_args.append(w)
        off += MEM_W
    assert off == w.shape[0]
    return pl.pallas_call(
        functools.partial(_out_proj_kernel, n_in=n_in, mem=mem is not None),
        grid=(m // tm,),
        in_specs=([pl.BlockSpec((tm, a.shape[1]), lambda i: (i, 0)) for a in acts]
                  + w_specs + mem_specs
                  + [pl.BlockSpec((1, d), lambda i: (0, 0)),
                     pl.BlockSpec((tm, d), lambda i: (i, 0))]),
        out_specs=pl.BlockSpec((tm, d), lambda i: (i, 0)),
        out_shape=jax.ShapeDtypeStruct((m, d), F32),
        compiler_params=_params("arbitrary"),
        name="out_proj",
    )(*acts, *([w] * n_in), *mem_args, g.reshape(1, d), x)


def _gmlp_out_proj_kernel(u_ref, v_ref, z_ref, ln_ref, w_ref, b_ref, wy_ref, mq_ref, mg_ref, mk_ref, mv_ref, wm_ref,
                          g_ref, x_ref, o_ref, vo_ref, wc_ref, yc_ref):
    c = GMLP_CHUNK
    sub = yc_ref.shape[0]
    tm = o_ref.shape[0]

    @pl.when(pl.program_id(0) == 0)
    def _():
        causal = (lax.broadcasted_iota(jnp.int32, (c, c), 0) >= lax.broadcasted_iota(jnp.int32, (c, c), 1))
        for g in range(GMLP_GROUPS):
            wc_ref[g] = jnp.where(causal, w_ref[g], 0.0).astype(BF16)

    for r0 in range(0, tm, sub):
        for c0 in range(r0, r0 + sub, c):
            rs = slice(c0, c0 + c)
            v = v_ref[rs, :].astype(F32)
            mu = jnp.mean(v, axis=-1, keepdims=True)
            d = v - mu
            var = jnp.mean(d * d, axis=-1, keepdims=True)
            vn = d * lax.rsqrt(var + NORM_EPS) * ln_ref[...]
            if c0 + c == tm:
                vo_ref[...] = vn
            for g in range(GMLP_GROUPS):
                gs = slice(g * GMLP_GROUP, (g + 1) * GMLP_GROUP)
                mixed = _dot(wc_ref[g], vn[:, gs].astype(BF16)) + b_ref[:, gs]
                yc_ref[c0 - r0:c0 - r0 + c, gs] = (u_ref[rs, gs].astype(F32) * mixed
                                                   * z_ref[rs, gs].astype(F32)).astype(BF16)
        rs = slice(r0, r0 + sub)
        ym = _mem_attend(mq_ref[rs, :], mg_ref[rs, :], mk_ref[...], mv_ref[...])
        acc = _dot(yc_ref[...], wy_ref[...]) + _dot(ym, wm_ref[...])
        ms = jnp.mean(acc * acc, axis=-1, keepdims=True)
        o_ref[rs, :] = x_ref[rs, :] + acc * lax.rsqrt(ms + NORM_EPS) * g_ref[...]


def _gmlp_out_proj(proj, memkv, gmlp_ln, w_mix, bias_rows, w, g, x, tm):
    m, d = x.shape
    c = GMLP_CHUNK
    gw = GMLP_WIDTH
    sub = min(OUT_PROJ_SUB_ROWS, tm)
    assert m % tm == 0 and tm % sub == 0 and sub % c == 0 and w.shape[0] == gw + MEM_W
    col = lambda off: pl.BlockSpec((tm, gw), lambda i: (i, off // gw))
    mem_specs, mem_args = _mem_specs(proj, O_MQ, O_MG, memkv, tm, w.shape[0], d)
    return pl.pallas_call(
        _gmlp_out_proj_kernel,
        grid=(m // tm,),
        in_specs=[col(O_U), col(O_V), col(O_Z),
                  pl.BlockSpec((1, gw), lambda i: (0, 0)),
                  pl.BlockSpec(w_mix.shape, lambda i: (0, 0, 0)),
                  pl.BlockSpec((c, gw), lambda i: (0, 0)),
                  pl.BlockSpec((gw, d), lambda i: (0, 0), pipeline_mode=pl.Buffered(1))]
                 + mem_specs
                 + [pl.BlockSpec((1, d), lambda i: (0, 0)),
                    pl.BlockSpec((tm, d), lambda i: (i, 0))],
        out_specs=[pl.BlockSpec((tm, d), lambda i: (i, 0)),
                   pl.BlockSpec((c, gw), lambda i: (0, 0))],
        out_shape=[jax.ShapeDtypeStruct((m, d), F32),
                   jax.ShapeDtypeStruct((c, gw), F32)],
        scratch_shapes=[pltpu.VMEM(w_mix.shape, BF16), pltpu.VMEM((sub, gw), BF16)],
        compiler_params=_params("arbitrary"),
        name="gmlp_out_proj",
    )(proj, proj, proj, gmlp_ln.reshape(1, gw), w_mix, bias_rows, w, *mem_args, w, g.reshape(1, d), x)


def _pad_steps(a, pad):
    nseq, seq, n = a.shape
    return jnp.pad(a, ((0, 0), (0, pad - seq), (0, 0))).reshape(nseq * pad, n)


def kernel(x_prompt, x_sample, state_ret, cache_swa_k, cache_swa_v, cache_mem_k, cache_mem_v, mem_prompt,
           pre_norm, post_norm, mem_norm, w_mem_k, w_mem_v, w_in_even, ret_gn, w_out_even,
           w_in_odd, gmlp_ln, w_spatial, b_spatial, w_out_odd):
    bp, tp, d = x_prompt.shape
    nseq, seq, _ = x_sample.shape
    pad = SAMPLE_PAD
    rows_s = nseq * pad
    c = GMLP_CHUNK
    assert bp == 1 and seq <= pad and rows_s % c == 0 and c % pad == 0
    hp = x_prompt.reshape(tp, d)
    hs = _pad_steps(x_sample, pad)
    mem = mem_prompt.reshape(MEM_LEN, d)
    tm_p = 1024 if tp % 1024 == 0 else 512
    unpad = lambda a: a.reshape(nseq, pad, -1)[:, :seq]

    pos_pad = np.concatenate([PAST_LEN + np.arange(seq), np.zeros((pad - seq,), np.int64)])
    ret_cos_p, ret_sin_p = _rot_tables_range(tp, RET_DK, RET_THETA, RET_DK)
    ret_cos_s, ret_sin_s = _rot_tables(np.tile(pos_pad, nseq), RET_DK, RET_THETA, RET_DK)
    swa_cos_p, swa_sin_p = _rot_tables_range(tp, ROPE_DIMS, ROPE_THETA, SWA_HD)
    swa_cos_s, swa_sin_s = _rot_tables(pos_pad, ROPE_DIMS, ROPE_THETA, SWA_HD)
    swa_cos_s = np.tile(swa_cos_s, (1, SWA_HEADS))
    swa_sin_s = np.tile(swa_sin_s, (1, SWA_HEADS))

    w_mem0 = jnp.concatenate([w_mem_k[0], w_mem_v[0]], axis=1).astype(BF16)
    memkv0 = _norm_matmul(mem, mem_norm[0], w_mem0, MEM_LEN, 2 * MEM_W)

    proj_s, w_in0 = _norm_matmul_cast(hs, pre_norm[0], w_in_even[0], 1024)
    proj_s3 = proj_s.reshape(nseq, pad, EVEN_IN)
    ya_s, st_s = _ret_sample(proj_s, state_ret[0], ret_cos_s, ret_sin_s, ret_gn[0], nseq, seq, pad)
    yb_s, sk = _swa_sample(proj_s3, cache_swa_k[0], cache_swa_v[0], swa_cos_s, swa_sin_s, seq)
    sk = sk[:, :seq].reshape(nseq, seq, SWA_HEADS, SWA_HD)
    sv = unpad(proj_s[:, E_BV:E_BV + SWA_HEADS * SWA_HD]).reshape(nseq, seq, SWA_HEADS, SWA_HD)

    buf_p = min(SWA_MAX_WINDOW, tp)
    proj_p, kv_last, w_out0, w_in1, w_out1 = _norm_matmul(
        hp, pre_norm[0], w_in0, tm_p, 1024, BF16, (tp - buf_p, E_BK, E_BG - E_BK),
        casts=[(w_out_even[0], 256), (w_in_odd[0], 1024), (w_out_odd[0], 256)],
        rot=(swa_cos_p, swa_sin_p, E_BQ, E_BK, SWA_HD ** -0.5 * LOG2E))
    hs1 = _mem_out_proj_rows(proj_s3, E_MQ, E_MG, cache_mem_k, cache_mem_v, 0,
                             [ya_s, yb_s.reshape(rows_s, -1)], w_out0, post_norm[0], hs)
    ya_p, st_p = _ret_prompt(proj_p, ret_cos_p, ret_sin_p, ret_gn[0])
    yb_p = _swa_prompt(proj_p)
    pk = kv_last[:, :E_BV - E_BK]
    hp1 = _out_proj([ya_p, yb_p], w_out0, post_norm[0], hp, 512, mem=(proj_p, E_MQ, E_MG, memkv0))
    pv = kv_last[:, E_BV - E_BK:]

    w_mem1 = jnp.concatenate([w_mem_k[1], w_mem_v[1]], axis=1).astype(BF16)
    memkv1 = _norm_matmul(mem, mem_norm[1], w_mem1, MEM_LEN, 2 * MEM_W)
    bias_p = jnp.repeat(b_spatial[0].T, GMLP_GROUP, axis=1)
    blockdiag = lambda wg: jnp.kron(jnp.eye(c // pad, dtype=F32),
                                    jnp.pad(wg[:seq, :seq], ((0, pad - seq), (0, pad - seq))))
    w_mix_s = jax.vmap(blockdiag)(w_spatial[0])
    bias_s = jnp.tile(jnp.pad(bias_p[:seq], ((0, pad - seq), (0, 0))), (c // pad, 1))

    proj_s1 = _norm_matmul(hs1, pre_norm[1], w_in1, rows_s, 1024)
    yc_s, gv_s = _gmlp(proj_s1, gmlp_ln[0], w_mix_s, bias_s, True, False)
    hs2 = _mem_out_proj_rows(proj_s1.reshape(nseq, pad, ODD_IN), O_MQ, O_MG, cache_mem_k, cache_mem_v, 1,
                             [yc_s], w_out1, post_norm[1], hs1)

    proj_p1 = _norm_matmul(hp1, pre_norm[1], w_in1, tm_p, 1024, BF16,
                           acts=[(O_U, O_Z, "gelu"), (O_Z, O_MQ, "silu")])
    hp2, gv_p = _gmlp_out_proj(proj_p1, memkv1, gmlp_ln[0], w_spatial[0], bias_p, w_out1, post_norm[1], hp1, 512)

    memkv = jnp.stack([memkv0, memkv1])
    p_mk = memkv[:, :, :MEM_W].reshape(2, bp, MEM_LEN, MEM_HEADS, MEM_HD)
    p_mv = memkv[:, :, MEM_W:].reshape(2, bp, MEM_LEN, MEM_HEADS, MEM_HD)
    return (hp2.reshape(bp, tp, d), unpad(hs2),
            st_p[None, None],
            pk.reshape(1, bp, buf_p, SWA_HEADS, SWA_HD), pv.reshape(1, bp, buf_p, SWA_HEADS, SWA_HD),
            p_mk, p_mv,
            gv_p[None, None],
            st_s[None],
            sk[None], sv[None],
            unpad(gv_s)[None])
```

```python
import functools
import math

import jax
import jax.numpy as jnp
import numpy as np
from jax import lax
from jax.experimental import pallas as pl
from jax.experimental.pallas import tpu as pltpu

F32 = jnp.float32
BF16 = jnp.bfloat16

D_MODEL = 2048
PAST_LEN = 8192
RET_HEADS = 8
RET_DK = 128
RET_DV = 256
RET_CHUNK = 128
RET_THETA = 10000.0
SWA_HEADS = 8
SWA_HD = 128
SWA_PATTERNS = ((128, 1), (512, 4), (2048, 16))
SWA_MAX_WINDOW = 2048
SWA_BLOCK = 128
ROPE_THETA = 500000.0
ROPE_DIMS = SWA_HD // 4
GMLP_CHUNK = 128
GMLP_WIDTH = D_MODEL
GMLP_GROUP = 128
GMLP_GROUPS = GMLP_WIDTH // GMLP_GROUP
MEM_LEN = 256
MEM_HEADS = 4
MEM_HD = 128
MEM_W = MEM_HEADS * MEM_HD
NORM_EPS = 1e-6
NEG_INF = -1e30
LOG2E = math.log2(math.e)

E_AQ = 0
E_AK = E_AQ + RET_HEADS * RET_DK
E_AV = E_AK + RET_HEADS * RET_DK
E_AG = E_AV + RET_HEADS * RET_DV
E_BQ = E_AG + RET_HEADS * RET_DV
E_BK = E_BQ + SWA_HEADS * SWA_HD
E_BV = E_BK + SWA_HEADS * SWA_HD
E_BG = E_BV + SWA_HEADS * SWA_HD
E_MQ = E_BG + SWA_HEADS * SWA_HD
E_MG = E_MQ + MEM_W
EVEN_IN = E_MG + MEM_W
O_U = 0
O_V = O_U + GMLP_WIDTH
O_Z = O_V + GMLP_WIDTH
O_MQ = O_Z + GMLP_WIDTH
O_MG = O_MQ + MEM_W
ODD_IN = O_MG + MEM_W

SWA_TILE = SWA_MAX_WINDOW
SAMPLE_PAD = 8
GMLP_STEP_CHUNKS = 2
SWA_GROUP = 1
RET_PROMPT_CHUNK = 256
RET_STEP_CHUNKS = 1
MEM_ROWS_STEP_SEQS = 4
RET_STEP_SEQS = 4
OUT_PROJ_SUB_ROWS = 256
NORM_SUB_ROWS = 256
VMEM_LIMIT = 56 * 1024 * 1024


def _params(*sem):
    return pltpu.CompilerParams(dimension_semantics=sem, vmem_limit_bytes=VMEM_LIMIT)


def _dot(a, b):
    return jnp.dot(a, b, preferred_element_type=F32)


def _dot_nt(a, b):
    return lax.dot_general(a, b, (((1,), (1,)), ((), ())), preferred_element_type=F32)


def _silu(x):
    return x / (1.0 + jnp.exp2(x * (-LOG2E)))


def _gelu(x):
    a = -2.0 * math.sqrt(2.0 / math.pi) * LOG2E
    return x / (1.0 + jnp.exp2(x * (a + (a * 0.044715) * (x * x))))


def _div(x, n):
    assert n & (n - 1) == 0
    return jnp.right_shift(x, n.bit_length() - 1)


def _mod(x, n):
    assert n & (n - 1) == 0
    return jnp.bitwise_and(x, n - 1)


def _rot_lanes(n_rot, theta, width):
    half = n_rot // 2
    inv = 1.0 / (theta ** (np.arange(half, dtype=np.float64) / half))
    rest = np.zeros(width - n_rot)
    return np.concatenate([inv, inv, rest]), np.concatenate([-np.ones(half), np.ones(half), rest])


def _rot_tables(pos, n_rot, theta, width):
    f, sgn = _rot_lanes(n_rot, theta, width)
    ang = np.asarray(pos, np.float64)[:, None] * f[None, :]
    return np.cos(ang).astype(np.float32), (sgn * np.sin(ang)).astype(np.float32)


def _rot_tables_range(n, n_rot, theta, width, blk=128):
    assert n % blk == 0
    f, sgn = _rot_lanes(n_rot, theta, width)
    hi = (np.arange(n // blk, dtype=np.float64) * blk)[:, None] * f[None, :]
    lo = np.arange(blk, dtype=np.float64)[:, None] * f[None, :]
    dev = lambda a: jnp.asarray(a, F32)
    ch, sh, cl, sl = dev(np.cos(hi)[:, None]), dev(np.sin(hi)[:, None]), dev(np.cos(lo)[None]), dev(np.sin(lo)[None])
    chs, shs = dev((sgn * np.cos(hi))[:, None]), dev((sgn * np.sin(hi))[:, None])
    return (ch * cl - sh * sl).reshape(n, width), (shs * cl + chs * sl).reshape(n, width)


def _rot_full(x, cos, sin):
    return x * cos + pltpu.roll(x, RET_DK // 2, 1) * sin


def _rot_partial(x, cos, sin):
    half = ROPE_DIMS // 2
    lane = _mod(lax.broadcasted_iota(jnp.int32, x.shape, 1), SWA_HD)
    width = x.shape[1]
    partner = jnp.where(lane < half, pltpu.roll(x, width - half, 1), pltpu.roll(x, half, 1))
    return x * cos + partner * sin


def _norm_matmul_kernel(x_ref, g_ref, w_ref, *rest, side, cast_cols, rot, acts, ntiles):
    if rot is not None:
        cos_ref, sin_ref, rest = rest[0], rest[1], rest[2:]
    nc = len(cast_cols)
    cast_in, o_ref, rest = rest[:nc], rest[nc], rest[nc + 1:]
    cast_out, xn_ref = rest[len(rest) - 1 - nc:len(rest) - 1], rest[-1]
    tm = x_ref.shape[0]
    sub = min(NORM_SUB_ROWS, tm)

    for src, dst, ncol in zip(cast_in, cast_out, cast_cols):
        @pl.when(pl.program_id(1) < ncol)
        def _(src=src, dst=dst):
            dst[...] = src[...].astype(BF16)

    kind = {}
    if rot is not None:
        jq, jk, q_scale = rot
        kind[jq], kind[jk] = "rope_q", "rope_k"
    for lo, hi, name in acts:
        kind.update({t: name for t in range(lo, hi)})

    def epilogue(name):
        if name in ("rope_q", "rope_k"):
            nrep = o_ref.shape[1] // cos_ref.shape[1]
            cos, sin = jnp.tile(cos_ref[...], (1, nrep)), jnp.tile(sin_ref[...], (1, nrep))
            return (lambda acc: _rot_partial(acc * q_scale, cos, sin)) if name == "rope_q" else (
                lambda acc: _rot_partial(acc, cos, sin))
        return {"gelu": _gelu, "silu": _silu, None: lambda acc: acc}[name]

    j = pl.program_id(1)

    @pl.when(j == 0)
    def _():
        first = epilogue(kind.get(0))
        for r0 in range(0, tm, sub):
            rs = slice(r0, r0 + sub)
            x = x_ref[rs, :]
            ms = jnp.mean(x * x, axis=-1, keepdims=True)
            xn = (x * lax.rsqrt(ms + NORM_EPS) * g_ref[...]).astype(BF16)
            xn_ref[rs, :] = xn
            o_ref[rs, :] = first(_dot(xn, w_ref[...])).astype(o_ref.dtype)

    def tail(name):
        acc = epilogue(name)(_dot(xn_ref[...], w_ref[...]))
        o_ref[...] = acc.astype(o_ref.dtype)
        if side is not None:
            i0, j0, nj = side
            i = pl.program_id(0)

            @pl.when((i >= i0) & (j >= j0) & (j < j0 + nj))
            def _():
                rest[0][...] = acc

    for name in sorted({kind.get(t) for t in range(1, ntiles)}, key=str):
        tiles = [t for t in range(1, ntiles) if kind.get(t) == name]
        pl.when(functools.reduce(jnp.logical_or, [j == t for t in tiles]))(functools.partial(tail, name))


def _norm_matmul(x, g, w, tm, tn, out_dtype=F32, side=None, casts=(), rot=None, acts=()):
    m, d = x.shape
    n = w.shape[1]
    ni, nj_grid = m // tm, n // tn
    out_specs = [pl.BlockSpec((tm, tn), lambda i, j: (i, j))]
    out_shape = [jax.ShapeDtypeStruct((m, n), out_dtype)]
    tiles = None
    if side is not None:
        row0, col0, ncols = side
        assert row0 % tm == 0 and col0 % tn == 0 and ncols % tn == 0 and col0 >= tn
        i0, j0, nj = row0 // tm, col0 // tn, ncols // tn
        tiles = (i0, j0, nj)
        out_specs.append(pl.BlockSpec(
            (tm, tn), lambda i, j: (jnp.maximum(i - i0, 0), jnp.where(i >= i0, jnp.clip(j - j0, 0, nj - 1), 0))))
        out_shape.append(jax.ShapeDtypeStruct((m - row0, ncols), F32))
    rot_tiles, rot_specs, rot_args = None, [], []
    if rot is not None:
        cos, sin, q_col, k_col, q_scale = rot
        assert q_col % tn == 0 and k_col % tn == 0 and q_col >= tn and k_col >= tn and tn % cos.shape[1] == 0
        rot_tiles = (q_col // tn, k_col // tn, q_scale)
        rot_specs = [pl.BlockSpec((tm, cos.shape[1]), lambda i, j: (i, 0))] * 2
        rot_args = [cos, sin]
    assert all(lo % tn == 0 and hi % tn == 0 for lo, hi, _ in acts)
    act_tiles = tuple((lo // tn, hi // tn, name) for lo, hi, name in acts)
    cast_specs, cast_cols = [], []
    for a, bc in casts:
        r, c = a.shape
        assert r % ni == 0 and (r // ni) % 16 == 0 and c % bc == 0 and c // bc <= nj_grid
        ncol = c // bc
        spec = pl.BlockSpec((r // ni, bc), functools.partial(lambda i, j, last: (i, jnp.minimum(j, last)), last=ncol - 1))
        cast_specs.append(spec)
        cast_cols.append(ncol)
        out_specs.append(spec)
        out_shape.append(jax.ShapeDtypeStruct((r, c), BF16))
    res = pl.pallas_call(
        functools.partial(_norm_matmul_kernel, side=tiles, cast_cols=tuple(cast_cols), rot=rot_tiles,
                          acts=act_tiles, ntiles=nj_grid),
        grid=(ni, nj_grid),
        in_specs=[pl.BlockSpec((tm, d), lambda i, j: (i, 0)),
                  pl.BlockSpec((1, d), lambda i, j: (0, 0)),
                  pl.BlockSpec((d, tn), lambda i, j: (0, j))] + rot_specs + cast_specs,
        out_specs=out_specs,
        out_shape=out_shape,
        scratch_shapes=[pltpu.VMEM((tm, d), BF16)],
        compiler_params=_params("arbitrary", "arbitrary"),
        name="norm_matmul",
    )(x, g.reshape(1, d), w, *rot_args, *[a for a, _ in casts])
    return res if len(res) > 1 else res[0]


def _norm_matmul_cast_kernel(x_ref, g_ref, w_ref, o_ref, wb_ref, xn_ref):
    @pl.when(pl.program_id(0) == 0)
    def _():
        x = x_ref[...]
        ms = jnp.mean(x * x, axis=-1, keepdims=True)
        xn_ref[...] = (x * lax.rsqrt(ms + NORM_EPS) * g_ref[...]).astype(BF16)

    wb = w_ref[...].astype(BF16)
    wb_ref[...] = wb
    o_ref[...] = _dot(xn_ref[...], wb).astype(o_ref.dtype)


def _norm_matmul_cast(x, g, w, tn):
    m, d = x.shape
    n = w.shape[1]
    return pl.pallas_call(
        _norm_matmul_cast_kernel,
        grid=(n // tn,),
        in_specs=[pl.BlockSpec((m, d), lambda j: (0, 0)),
                  pl.BlockSpec((1, d), lambda j: (0, 0)),
                  pl.BlockSpec((d, tn), lambda j: (0, j))],
        out_specs=[pl.BlockSpec((m, tn), lambda j: (0, j)),
                   pl.BlockSpec((d, tn), lambda j: (0, j))],
        out_shape=[jax.ShapeDtypeStruct((m, n), F32),
                   jax.ShapeDtypeStruct((d, n), BF16)],
        scratch_shapes=[pltpu.VMEM((m, d), BF16)],
        compiler_params=_params("arbitrary"),
        name="norm_matmul_cast",
    )(x, g.reshape(1, d), w)


def _ret_tables(c, reps, pad=None):
    pad = c if pad is None else pad
    rows = pad * reps
    lg = np.log1p(-np.exp2(-5.0 - np.arange(RET_HEADS, dtype=np.float64)))
    row = np.arange(rows)
    step = (row % pad).astype(np.float64)
    live = (row % pad) < c
    rel = step[:, None] - step[None, :]
    keep = ((row // pad)[:, None] == (row // pad)[None, :]) & (rel >= 0) & live[:, None] & live[None, :]
    decay = np.where(keep[None], np.exp(lg[:, None, None] * np.maximum(rel, 0.0)[None]), 0.0)
    q_decay = np.where(live[None], np.exp(lg[:, None] * (step[None, :] + 1.0)), 0.0)
    k_decay = np.where(live[None], np.exp(lg[:, None] * (c - 1.0 - step[None, :])), 0.0)
    chunk_decay = np.exp(lg * c)
    f32 = lambda a, shape: np.ascontiguousarray(np.broadcast_to(a, shape), dtype=np.float32)
    qd = f32(q_decay[:, :, None], (RET_HEADS, rows, RET_DK))
    kd = f32(k_decay[:, :, None], (RET_HEADS, rows, RET_DK))
    cd = f32(chunk_decay[:, None, None], (RET_HEADS, 1, RET_DV))
    return decay.astype(np.float32), qd, kd, cd


def _group_norm_gate(o, gn, g):
    mu = jnp.mean(o, axis=-1, keepdims=True)
    d = o - mu
    var = jnp.mean(d * d, axis=-1, keepdims=True)
    return d * lax.rsqrt(var + NORM_EPS) * gn * _silu(g.astype(F32))


def _ret_prompt_kernel(q_ref, k_ref, v_ref, g_ref, cos_ref, sin_ref, dec_ref, qd_ref, kd_ref, cd_ref, gn_ref,
                       y_ref, st_ref):
    @pl.when(pl.program_id(0) == 0)
    def _():
        st_ref[...] = jnp.zeros_like(st_ref)

    c = dec_ref.shape[1]
    for i in range(q_ref.shape[0] // c):
        rs = slice(i * c, (i + 1) * c)
        cos = cos_ref[rs, :]
        sin = sin_ref[rs, :]
        for h in range(RET_HEADS):
            ks = slice(h * RET_DK, (h + 1) * RET_DK)
            vs = slice(h * RET_DV, (h + 1) * RET_DV)
            q = _rot_full(q_ref[rs, ks].astype(F32), cos, sin)
            k = _rot_full(k_ref[rs, ks].astype(F32) * (RET_DK ** -0.5), cos, sin)
            v = v_ref[rs, vs].astype(BF16)
            st = st_ref[h]
            s = _dot_nt(q.astype(BF16), k.astype(BF16)) * dec_ref[h]
            lhs = jnp.concatenate([s.astype(BF16), (q * qd_ref[h]).astype(BF16)], axis=1)
            o = _dot(lhs, jnp.concatenate([v, st.astype(BF16)], axis=0))
            st_ref[h] = cd_ref[h] * st + _dot((k * kd_ref[h]).T.astype(BF16), v)
            y_ref[rs, vs] = _group_norm_gate(o, gn_ref[:, vs], g_ref[rs, vs]).astype(y_ref.dtype)


def _ret_prompt(proj, cos, sin, ret_gn):
    t = proj.shape[0]
    dec, qd, kd, cd = _ret_tables(RET_PROMPT_CHUNK, 1)
    c = RET_STEP_CHUNKS * RET_PROMPT_CHUNK
    assert t % c == 0
    qw = RET_HEADS * RET_DK
    vw = RET_HEADS * RET_DV
    full3 = lambda n: (0, 0, 0)
    return pl.pallas_call(
        _ret_prompt_kernel,
        grid=(t // c,),
        in_specs=[pl.BlockSpec((c, qw), lambda n: (n, E_AQ // qw)),
                  pl.BlockSpec((c, qw), lambda n: (n, E_AK // qw)),
                  pl.BlockSpec((c, vw), lambda n: (n, E_AV // vw)),
                  pl.BlockSpec((c, vw), lambda n: (n, E_AG // vw)),
                  pl.BlockSpec((c, RET_DK), lambda n: (n, 0)),
                  pl.BlockSpec((c, RET_DK), lambda n: (n, 0)),
                  pl.BlockSpec(dec.shape, full3),
                  pl.BlockSpec(qd.shape, full3),
                  pl.BlockSpec(kd.shape, full3),
                  pl.BlockSpec(cd.shape, full3),
                  pl.BlockSpec((1, vw), lambda n: (0, 0))],
        out_specs=[pl.BlockSpec((c, vw), lambda n: (n, 0)),
                   pl.BlockSpec((RET_HEADS, RET_DK, RET_DV), full3)],
        out_shape=[jax.ShapeDtypeStruct((t, vw), BF16),
                   jax.ShapeDtypeStruct((RET_HEADS, RET_DK, RET_DV), F32)],
        compiler_params=_params("arbitrary"),
        name="ret_prompt",
    )(proj, proj, proj, proj, cos, sin, dec, qd, kd, cd, ret_gn.reshape(1, vw))


def _ret_sample_kernel(q_ref, k_ref, v_ref, g_ref, cos_ref, sin_ref, dec_ref, qd_ref, kd_ref, cd_ref, gn_ref,
                       st_ref, y_ref, sto_ref, cross_ref, qs_ref, kt_ref, *, pad):
    b = pl.program_id(0)
    nb = pl.num_programs(0)
    win = RET_CHUNK

    @pl.when(b == 0)
    def _():
        cross_ref[...] = jnp.zeros_like(cross_ref)
        cos = cos_ref[...]
        sin = sin_ref[...]
        for h in range(RET_HEADS):
            ks = slice(h * RET_DK, (h + 1) * RET_DK)
            qs_ref[:, ks] = _rot_full(q_ref[:, ks], cos, sin) * qd_ref[h]
            kt_ref[h] = (_rot_full(k_ref[:, ks] * (RET_DK ** -0.5), cos, sin) * kd_ref[h]).T

    for i in range(st_ref.shape[0]):
        sq = b * st_ref.shape[0] + i
        w0 = pl.multiple_of(jnp.right_shift(sq * pad, win.bit_length() - 1) * win, win)
        wrows = pl.ds(w0, win)
        rows = lax.broadcasted_iota(jnp.int32, (win, RET_DK), 0) + w0
        cols = lax.broadcasted_iota(jnp.int32, (RET_DK, win), 1) + w0
        mine_r = (rows >= sq * pad) & (rows < (sq + 1) * pad)
        mine_c = (cols >= sq * pad) & (cols < (sq + 1) * pad)
        for h in range(RET_HEADS):
            ks = slice(h * RET_DK, (h + 1) * RET_DK)
            vs = slice(h * RET_DV, (h + 1) * RET_DV)
            st = st_ref[i, h]
            qm = jnp.where(mine_r, qs_ref[wrows, ks], 0.0)
            kmt = jnp.where(mine_c, kt_ref[h, :, wrows], 0.0)
            cross_ref[wrows, vs] += _dot(qm.astype(BF16), st.astype(BF16))
            sto_ref[i, h] = cd_ref[h] * st + _dot(kmt.astype(BF16), v_ref[wrows, vs].astype(BF16))

    @pl.when(b == nb - 1)
    def _():
        cos_all = cos_ref[...]
        sin_all = sin_ref[...]
        for h in range(RET_HEADS):
            ks = slice(h * RET_DK, (h + 1) * RET_DK)
            vs = slice(h * RET_DV, (h + 1) * RET_DV)
            q = _rot_full(q_ref[:, ks], cos_all, sin_all)
            k = _rot_full(k_ref[:, ks] * (RET_DK ** -0.5), cos_all, sin_all)
            v = v_ref[:, vs].astype(BF16)
            s = _dot_nt(q.astype(BF16), k.astype(BF16)) * dec_ref[h]
            o = _dot(s.astype(BF16), v) + cross_ref[:, vs]
            y_ref[:, vs] = _group_norm_gate(o, gn_ref[:, vs], g_ref[:, vs]).astype(y_ref.dtype)


def _ret_sample(proj, state, cos, sin, ret_gn, nseq, seq, pad):
    rows = nseq * pad
    c = math.gcd(seq, RET_CHUNK)
    assert c == seq, "sample chunk must cover the new tokens"
    assert rows % RET_CHUNK == 0 and RET_CHUNK % pad == 0
    dec, qd, kd, cd = _ret_tables(c, nseq, pad)
    qw = RET_HEADS * RET_DK
    vw = RET_HEADS * RET_DV
    full3 = lambda b: (0, 0, 0)
    n = RET_STEP_SEQS
    assert nseq % n == 0
    st_spec = pl.BlockSpec((n, RET_HEADS, RET_DK, RET_DV), lambda b: (b, 0, 0, 0))
    return pl.pallas_call(
        functools.partial(_ret_sample_kernel, pad=pad),
        grid=(nseq // n,),
        in_specs=[pl.BlockSpec((rows, qw), lambda b: (0, E_AQ // qw)),
                  pl.BlockSpec((rows, qw), lambda b: (0, E_AK // qw)),
                  pl.BlockSpec((rows, vw), lambda b: (0, E_AV // vw)),
                  pl.BlockSpec((rows, vw), lambda b: (0, E_AG // vw)),
                  pl.BlockSpec((rows, RET_DK), lambda b: (0, 0)),
                  pl.BlockSpec((rows, RET_DK), lambda b: (0, 0)),
                  pl.BlockSpec(dec.shape, full3),
                  pl.BlockSpec(qd.shape, full3),
                  pl.BlockSpec(kd.shape, full3),
                  pl.BlockSpec(cd.shape, full3),
                  pl.BlockSpec((1, vw), lambda b: (0, 0)),
                  st_spec],
        out_specs=[pl.BlockSpec((rows, vw), lambda b: (0, 0)), st_spec],
        out_shape=[jax.ShapeDtypeStruct((rows, vw), F32),
                   jax.ShapeDtypeStruct(state.shape, F32)],
        scratch_shapes=[pltpu.VMEM((rows, vw), F32),
                        pltpu.VMEM((rows, qw), F32),
                        pltpu.VMEM((RET_HEADS, RET_DK, rows), F32)],
        compiler_params=_params("arbitrary"),
        name="ret_sample",
    )(proj, proj, proj, proj, cos, sin, dec, qd, kd, cd, ret_gn.reshape(1, vw), state)


def _swa_prompt_kernel(q_ref, kc_ref, vc_ref, g_ref, y_ref, q_s, k_s, v_s, m_s, l_s, acc_s):
    t = pl.program_id(1)
    tb = SWA_TILE
    blk = SWA_BLOCK
    hd = SWA_HD

    @pl.when(t == 0)
    def _():
        k_s[0:tb, :] = jnp.zeros((tb, hd), F32)
        v_s[0:tb, :] = jnp.zeros((tb, hd), F32)

    @pl.when(t > 0)
    def _():
        k_s[0:tb, :] = k_s[tb:2 * tb, :]
        v_s[0:tb, :] = v_s[tb:2 * tb, :]

    q_s[...] = q_ref[...].astype(F32)
    k_s[tb:2 * tb, :] = kc_ref[...].astype(F32)
    v_s[tb:2 * tb, :] = vc_ref[...].astype(F32)

    def ds(start, size, stride):
        return pl.ds(start, size) if stride == 1 else pl.ds(start, size, stride=stride)

    def band(g):
        qi = lax.broadcasted_iota(jnp.int32, (g * blk, (g + 1) * blk), 0)
        kj = lax.broadcasted_iota(jnp.int32, (g * blk, (g + 1) * blk), 1)
        ok = (kj >= qi) & (kj <= qi + blk)
        return ok, ok & ((kj >= blk) | (t > 0))

    for bi, (window, r) in enumerate(SWA_PATTERNS):
        assert window // r == blk
        per_class = tb // (blk * r)
        g = SWA_GROUP if per_class % SWA_GROUP == 0 else 1
        ok, ok_first = band(g)
        ones = jnp.ones(((g + 1) * blk, hd), BF16)
        for c in range(r):
            for n in range(0, per_class, g):
                qs = n * blk * r + c
                ks = qs + tb - blk * r
                qrows = ds(qs, g * blk, r)
                krows = ds(ks, (g + 1) * blk, r)
                s = _dot_nt(q_s[qrows, :].astype(BF16), k_s[krows, :].astype(BF16))
                s = jnp.where(ok_first if n == 0 else ok, s, NEG_INF)
                m = jnp.max(s, axis=-1, keepdims=True)
                p = jnp.exp2(s - m).astype(BF16)
                pv = _dot(p, jnp.concatenate([v_s[krows, :].astype(BF16), ones], axis=1))
                m_s[bi, qrows, :] = jnp.broadcast_to(m, (g * blk, hd))
                l_s[bi, qrows, :] = pv[:, hd:]
                acc_s[bi, qrows, :] = pv[:, :hd]

    nbr = len(SWA_PATTERNS)
    ms = [m_s[i] for i in range(nbr)]
    mmax = functools.reduce(jnp.maximum, ms)
    wts = [jnp.exp2(mi - mmax) for mi in ms]
    num = sum(wts[i] * acc_s[i] for i in range(nbr))
    den = sum(wts[i] * l_s[i] for i in range(nbr))
    y_ref[...] = (num / den * _silu(g_ref[...].astype(F32))).astype(y_ref.dtype)


def _swa_prompt(proj):
    t = proj.shape[0]
    tb = SWA_TILE
    assert t % tb == 0
    hd = SWA_HD
    nbr = len(SWA_PATTERNS)
    cur = lambda off: pl.BlockSpec((tb, hd), lambda h, i: (i, off // hd + h))
    return pl.pallas_call(
        _swa_prompt_kernel,
        grid=(SWA_HEADS, t // tb),
        in_specs=[cur(E_BQ), cur(E_BK), cur(E_BV), cur(E_BG)],
        out_specs=pl.BlockSpec((tb, hd), lambda h, i: (i, h)),
        out_shape=jax.ShapeDtypeStruct((t, SWA_HEADS * hd), BF16),
        scratch_shapes=[pltpu.VMEM((tb, hd), F32),
                        pltpu.VMEM((2 * tb, hd), F32),
                        pltpu.VMEM((2 * tb, hd), F32),
                        pltpu.VMEM((nbr, tb, hd), F32),
                        pltpu.VMEM((nbr, tb, hd), F32),
                        pltpu.VMEM((nbr, tb, hd), F32)],
        compiler_params=_params("arbitrary", "arbitrary"),
        name="swa_prompt",
    )(proj, proj, proj, proj)


def _swa_sample_kernel(q_ref, k_ref, v_ref, g_ref, cos_ref, sin_ref, ka_ref, kb_ref, va_ref, vb_ref,
                       y_ref, ko_ref, kn_s, vn_s, ba_s, bb_s, bn_s, *, seq):
    pad = SAMPLE_PAD
    nh = SWA_HEADS
    hd = SWA_HD
    nr = nh * pad
    nj, nrh, _ = kb_ref.shape[1:]
    na = ka_ref.shape[1]
    nb = nj * nrh
    nn = kn_s.shape[0]

    @pl.when(pl.program_id(0) == 0)
    def _():
        (w1, d1), (w2, d2), (w3, d3) = SWA_PATTERNS

        def rows_cols(n):
            r = lax.broadcasted_iota(jnp.int32, (nr, n), 0)
            c = lax.broadcasted_iota(jnp.int32, (nr, n), 1)
            return _div(r, pad), _mod(r, pad), c

        def log2_mult(mult):
            return jnp.where(mult > 0, jnp.log2(jnp.maximum(mult, 1.0)), NEG_INF)

        hq, tq, c = rows_cols(na)
        dist = na // nh + tq - _div(c, nh)
        mult_a = ((dist <= w1).astype(F32) + ((_mod(dist, d2) == 0) & (dist <= w2)).astype(F32)
                  + ((_mod(dist, d3) == 0) & (dist <= w3)).astype(F32))
        ba_s[...] = log2_mult(jnp.where((hq == _mod(c, nh)) & (tq < seq), mult_a, 0.0))
        hq, tq, c = rows_cols(nb)
        bb_s[...] = log2_mult(jnp.where((hq == _mod(c, nh)) & (tq == _mod(_div(c, nh), nrh // nh)), 1.0, 0.0))
        hq, tq, c = rows_cols(nn)
        tk = _mod(c, pad)
        mult_n = (tk <= tq).astype(F32) + 2.0 * (tk == tq).astype(F32)
        bn_s[...] = log2_mult(jnp.where((hq == _div(c, pad)) & (tq < seq) & (tk < seq), mult_n, 0.0))
        kn_s[...] = jnp.zeros_like(kn_s)
        vn_s[...] = jnp.zeros_like(vn_s)

    cos = cos_ref[...]
    sin = sin_ref[...]
    q8 = _rot_partial(q_ref[0] * (SWA_HD ** -0.5 * LOG2E), cos, sin)
    k8 = _rot_partial(k_ref[0], cos, sin)
    ko_ref[0] = k8

    def head_rows(x):
        return jnp.concatenate([x[:, h * hd:(h + 1) * hd] for h in range(nh)], axis=0)

    qrows = head_rows(q8).astype(BF16)
    kn_s[0:nr, :] = head_rows(k8).astype(BF16)
    vn_s[0:nr, :] = head_rows(v_ref[0]).astype(BF16)

    kb = kb_ref[0].reshape(nb, hd)
    vb = vb_ref[0].reshape(nb, hd)
    s_a = _dot_nt(qrows, ka_ref[0].astype(BF16)) + ba_s[...]
    s_b = _dot_nt(qrows, kb.astype(BF16)) + bb_s[...]
    s_n = _dot_nt(qrows, kn_s[...]) + bn_s[...]
    rowmax = lambda s: jnp.max(s, axis=-1, keepdims=True)
    m = jnp.maximum(jnp.maximum(rowmax(s_a), rowmax(s_b)), rowmax(s_n))
    p_a, p_b, p_n = jnp.exp2(s_a - m), jnp.exp2(s_b - m), jnp.exp2(s_n - m)
    rowsum = lambda p: jnp.sum(p, axis=-1, keepdims=True)
    o = (_dot(p_a.astype(BF16), va_ref[0].astype(BF16)) + _dot(p_b.astype(BF16), vb.astype(BF16))
         + _dot(p_n.astype(BF16), vn_s[...])) / (rowsum(p_a) + rowsum(p_b) + rowsum(p_n))
    y = jnp.concatenate([o[h * pad:(h + 1) * pad] for h in range(nh)], axis=1)
    y_ref[0] = y * _silu(g_ref[0])


def _swa_sample(proj_pad, cache_k, cache_v, cos, sin, seq):
    nseq, pad, _ = proj_pad.shape
    buf = cache_k.shape[1]
    nh, hd = SWA_HEADS, SWA_HD
    w = nh * hd
    (w1, d1), (w2, d2), (w3, d3) = SWA_PATTERNS
    assert buf == w3 and d1 == 1 and seq <= d2 and w1 <= w2 and buf % w2 == 0 and buf % d3 == 0
    assert nh * pad <= SWA_BLOCK and (seq * nh) % 8 == 0
    flat = lambda a: a.reshape(nseq, buf * nh, hd)
    strided = lambda a: a.reshape(nseq, buf // d3, d3 * nh, hd)
    col = lambda off: pl.BlockSpec((1, pad, w), lambda b: (b, 0, off // w))
    tail = pl.BlockSpec((1, w2 * nh, hd), lambda b: (b, buf // w2 - 1, 0))
    resid = pl.BlockSpec((1, (buf - w2) // d3, seq * nh, hd), lambda b: (b, 0, 0, 0))
    tab = pl.BlockSpec((pad, w), lambda b: (0, 0))
    out = pl.BlockSpec((1, pad, w), lambda b: (b, 0, 0))
    return pl.pallas_call(
        functools.partial(_swa_sample_kernel, seq=seq),
        grid=(nseq,),
        in_specs=[col(E_BQ), col(E_BK), col(E_BV), col(E_BG), tab, tab, tail, resid, tail, resid],
        out_specs=[out, out],
        out_shape=[jax.ShapeDtypeStruct((nseq, pad, w), F32),
                   jax.ShapeDtypeStruct((nseq, pad, w), F32)],
        scratch_shapes=[pltpu.VMEM((SWA_BLOCK, hd), BF16), pltpu.VMEM((SWA_BLOCK, hd), BF16),
                        pltpu.VMEM((nh * pad, w2 * nh), F32),
                        pltpu.VMEM((nh * pad, (buf - w2) // d3 * seq * nh), F32),
                        pltpu.VMEM((nh * pad, SWA_BLOCK), F32)],
        compiler_params=_params("arbitrary"),
        name="swa_sample",
    )(proj_pad, proj_pad, proj_pad, proj_pad, cos, sin,
      flat(cache_k), strided(cache_k), flat(cache_v), strided(cache_v))


def _mem_attend(q, g, k, v):
    ones = jnp.ones((MEM_LEN, MEM_HD), BF16)
    outs = []
    for h in range(MEM_HEADS):
        hs = slice(h * MEM_HD, (h + 1) * MEM_HD)
        qh = (q[:, hs].astype(F32) * (MEM_HD ** -0.5 * LOG2E)).astype(BF16)
        s = _dot_nt(qh, k[:, hs].astype(BF16))
        m = jnp.max(s, axis=-1, keepdims=True)
        p = jnp.exp2(s - m).astype(BF16)
        pv = _dot(p, jnp.concatenate([v[:, hs].astype(BF16), ones], axis=1))
        o = pv[:, :MEM_HD] / pv[:, MEM_HD:]
        outs.append((o * _silu(g[:, hs].astype(F32))).astype(BF16))
    return jnp.concatenate(outs, axis=1)


def _mem_out_proj_rows_kernel(*refs, n_in):
    q_ref, g_ref, k_ref, v_ref = refs[:4]
    a_refs, w_refs = refs[4:4 + n_in], refs[4 + n_in:4 + 2 * n_in]
    wm_ref, gn_ref, x_ref, o_ref, bias_s, ym_s = refs[4 + 2 * n_in:]
    nseq, pad, _ = q_ref.shape
    nh, hd = MEM_HEADS, MEM_HD
    step = pl.program_id(0)

    @pl.when(step == 0)
    def _():
        r = lax.broadcasted_iota(jnp.int32, bias_s.shape, 0)
        c = lax.broadcasted_iota(jnp.int32, bias_s.shape, 1)
        bias_s[...] = jnp.where(_div(r, pad) == _mod(c, nh), 0.0, NEG_INF)

    for i in range(nseq):
        q = q_ref[i] * (MEM_HD ** -0.5 * LOG2E)
        qrows = jnp.concatenate([q[:, h * hd:(h + 1) * hd] for h in range(nh)], axis=0).astype(BF16)
        s = _dot_nt(qrows, k_ref[i].astype(BF16)) + bias_s[...]
        m = jnp.max(s, axis=-1, keepdims=True)
        p = jnp.exp2(s - m)
        o = _dot(p.astype(BF16), v_ref[i].astype(BF16)) / jnp.sum(p, axis=-1, keepdims=True)
        y = jnp.concatenate([o[h * pad:(h + 1) * pad] for h in range(nh)], axis=1)
        row0 = pl.multiple_of((step * nseq + i) * pad, pad)
        ym_s[pl.ds(row0, pad), :] = y * _silu(g_ref[i])

    @pl.when(step == pl.num_programs(0) - 1)
    def _():
        acc = _dot(ym_s[...].astype(BF16), wm_ref[...])
        for a_ref, w_ref in zip(a_refs, w_refs):
            acc = acc + _dot(a_ref[...].astype(BF16), w_ref[...])
        ms = jnp.mean(acc * acc, axis=-1, keepdims=True)
        o_ref[...] = x_ref[...] + acc * lax.rsqrt(ms + NORM_EPS) * gn_ref[...]


def _mem_out_proj_rows(proj_pad, q_off, g_off, mem_k, mem_v, layer, acts, w, g, x):
    b, pad, _ = proj_pad.shape
    m, d = x.shape
    mw = MEM_W
    rows = MEM_LEN * MEM_HEADS
    n = MEM_ROWS_STEP_SEQS
    assert b % n == 0 and m == b * pad
    const = lambda shape, idx: pl.BlockSpec(shape, lambda i: idx, pipeline_mode=pl.Buffered(1))
    kv = pl.BlockSpec((n, rows, MEM_HD), lambda i: (layer * (b // n) + i, 0, 0))
    w_specs, off = [], 0
    for a in acts:
        k = a.shape[1]
        assert off % k == 0
        w_specs.append(const((k, d), (off // k, 0)))
        off += k
    assert off + mw == w.shape[0] and off % mw == 0
    return pl.pallas_call(
        functools.partial(_mem_out_proj_rows_kernel, n_in=len(acts)),
        grid=(b // n,),
        in_specs=[pl.BlockSpec((n, pad, mw), lambda i: (i, 0, q_off // mw)),
                  pl.BlockSpec((n, pad, mw), lambda i: (i, 0, g_off // mw)),
                  kv, kv]
                 + [const(a.shape, (0, 0)) for a in acts] + w_specs
                 + [const((mw, d), (off // mw, 0)), const((1, d), (0, 0)), const((m, d), (0, 0))],
        out_specs=pl.BlockSpec((m, d), lambda i: (0, 0)),
        out_shape=jax.ShapeDtypeStruct((m, d), F32),
        scratch_shapes=[pltpu.VMEM((MEM_HEADS * pad, rows), F32), pltpu.VMEM((m, mw), F32)],
        compiler_params=_params("arbitrary"),
        name="mem_out_proj_rows",
    )(proj_pad, proj_pad, mem_k.reshape(-1, rows, MEM_HD), mem_v.reshape(-1, rows, MEM_HD),
      *acts, *([w] * len(acts)), w, g.reshape(1, d), x)


def _gmlp_kernel(u_ref, v_ref, z_ref, ln_ref, w_ref, b_ref, y_ref, vo_ref, wc_ref, *, activated):
    c = GMLP_CHUNK
    gelu, silu = (lambda a: a, lambda a: a) if activated else (_gelu, _silu)

    @pl.when(pl.program_id(0) == 0)
    def _():
        causal = (lax.broadcasted_iota(jnp.int32, (c, c), 0) >= lax.broadcasted_iota(jnp.int32, (c, c), 1))
        for g in range(GMLP_GROUPS):
            wc_ref[g] = jnp.where(causal, w_ref[g], 0.0).astype(BF16)

    nchunks = u_ref.shape[0] // c
    keep_all = vo_ref.shape[0] == u_ref.shape[0]
    for i in range(nchunks):
        rs = slice(i * c, (i + 1) * c)
        v = gelu(v_ref[rs, :].astype(F32))
        mu = jnp.mean(v, axis=-1, keepdims=True)
        d = v - mu
        var = jnp.mean(d * d, axis=-1, keepdims=True)
        vn = d * lax.rsqrt(var + NORM_EPS) * ln_ref[...]
        if keep_all:
            vo_ref[rs, :] = vn
        elif i == nchunks - 1:
            vo_ref[...] = vn
        for g in range(GMLP_GROUPS):
            gs = slice(g * GMLP_GROUP, (g + 1) * GMLP_GROUP)
            mixed = _dot(wc_ref[g], vn[:, gs].astype(BF16)) + b_ref[:, gs]
            y_ref[rs, gs] = (gelu(u_ref[rs, gs].astype(F32)) * mixed
                             * silu(z_ref[rs, gs].astype(F32))).astype(y_ref.dtype)


def _gmlp(proj, gmlp_ln, w_mix, bias_rows, keep_all_v, activated):
    t = proj.shape[0]
    c = GMLP_CHUNK
    w = GMLP_WIDTH
    rows = GMLP_STEP_CHUNKS * c
    assert t % rows == 0
    col = lambda off: pl.BlockSpec((rows, w), lambda n: (n, off // w))
    return pl.pallas_call(
        functools.partial(_gmlp_kernel, activated=activated),
        grid=(t // rows,),
        in_specs=[col(O_U), col(O_V), col(O_Z),
                  pl.BlockSpec((1, w), lambda n: (0, 0)),
                  pl.BlockSpec(w_mix.shape, lambda n: (0, 0, 0)),
                  pl.BlockSpec((c, w), lambda n: (0, 0))],
        out_specs=[pl.BlockSpec((rows, w), lambda n: (n, 0)),
                   pl.BlockSpec((rows, w), lambda n: (n, 0)) if keep_all_v
                   else pl.BlockSpec((c, w), lambda n: (0, 0))],
        out_shape=[jax.ShapeDtypeStruct((t, w), BF16),
                   jax.ShapeDtypeStruct((t if keep_all_v else c, w), F32)],
        scratch_shapes=[pltpu.VMEM(w_mix.shape, BF16)],
        compiler_params=_params("arbitrary"),
        name="gmlp",
    )(proj, proj, proj, gmlp_ln.reshape(1, w), w_mix, bias_rows)


def _out_proj_kernel(*refs, n_in, mem):
    a_refs, w_refs = refs[:n_in], refs[n_in:2 * n_in]
    refs = refs[2 * n_in:]
    if mem:
        (mq_ref, mg_ref, mk_ref, mv_ref, wm_ref), refs = refs[:5], refs[5:]
    g_ref, x_ref, o_ref = refs
    sub = min(OUT_PROJ_SUB_ROWS, o_ref.shape[0])
    for r0 in range(0, o_ref.shape[0], sub):
        rs = slice(r0, r0 + sub)
        acc = _dot(a_refs[0][rs, :].astype(BF16), w_refs[0][...])
        for a_ref, w_ref in zip(a_refs[1:], w_refs[1:]):
            acc = acc + _dot(a_ref[rs, :].astype(BF16), w_ref[...])
        if mem:
            acc = acc + _dot(_mem_attend(mq_ref[rs, :], mg_ref[rs, :], mk_ref[...], mv_ref[...]), wm_ref[...])
        ms = jnp.mean(acc * acc, axis=-1, keepdims=True)
        o_ref[rs, :] = x_ref[rs, :] + acc * lax.rsqrt(ms + NORM_EPS) * g_ref[...]


def _mem_specs(proj, q_off, g_off, memkv, tm, w_rows, d):
    w = MEM_W
    const = lambda shape, idx: pl.BlockSpec(shape, lambda i: idx, pipeline_mode=pl.Buffered(1))
    specs = [pl.BlockSpec((tm, w), lambda i: (i, q_off // w)), pl.BlockSpec((tm, w), lambda i: (i, g_off // w)),
             const((MEM_LEN, w), (0, 0)), const((MEM_LEN, w), (0, 1)), const((w, d), ((w_rows - w) // w, 0))]
    return specs, [proj, proj, memkv, memkv]


def _out_proj(acts, w, g, x, tm, mem=None):
    m, d = x.shape
    n_in = len(acts)
    w_specs, off = [], 0
    for a in acts:
        k = a.shape[1]
        assert off % k == 0
        w_specs.append(pl.BlockSpec((k, d), functools.partial(lambda i, blk: (blk, 0), blk=off // k),
                                    pipeline_mode=pl.Buffered(1)))
        off += k
    mem_specs, mem_args = [], []
    if mem is not None:
        mem_specs, mem_args = _mem_specs(*mem, tm, w.shape[0], d)
        mem_args.append(w)
        off += MEM_W
    assert off == w.shape[0]
    return pl.pallas_call(
        functools.partial(_out_proj_kernel, n_in=n_in, mem=mem is not None),
        grid=(m // tm,),
        in_specs=([pl.BlockSpec((tm, a.shape[1]), lambda i: (i, 0)) for a in acts]
                  + w_specs + mem_specs
                  + [pl.BlockSpec((1, d), lambda i: (0, 0)),
                     pl.BlockSpec((tm, d), lambda i: (i, 0))]),
        out_specs=pl.BlockSpec((tm, d), lambda i: (i, 0)),
        out_shape=jax.ShapeDtypeStruct((m, d), F32),
        compiler_params=_params("arbitrary"),
        name="out_proj",
    )(*acts, *([w] * n_in), *mem_args, g.reshape(1, d), x)


def _gmlp_out_proj_kernel(u_ref, v_ref, z_ref, ln_ref, w_ref, b_ref, wy_ref, mq_ref, mg_ref, mk_ref, mv_ref, wm_ref,
                          g_ref, x_ref, o_ref, vo_ref, wc_ref, yc_ref):
    c = GMLP_CHUNK
    sub = yc_ref.shape[0]
    tm = o_ref.shape[0]

    @pl.when(pl.program_id(0) == 0)
    def _():
        causal = (lax.broadcasted_iota(jnp.int32, (c, c), 0) >= lax.broadcasted_iota(jnp.int32, (c, c), 1))
        for g in range(GMLP_GROUPS):
            wc_ref[g] = jnp.where(causal, w_ref[g], 0.0).astype(BF16)

    for r0 in range(0, tm, sub):
        for c0 in range(r0, r0 + sub, c):
            rs = slice(c0, c0 + c)
            v = v_ref[rs, :].astype(F32)
            mu = jnp.mean(v, axis=-1, keepdims=True)
            d = v - mu
            var = jnp.mean(d * d, axis=-1, keepdims=True)
            vn = d * lax.rsqrt(var + NORM_EPS) * ln_ref[...]
            if c0 + c == tm:
                vo_ref[...] = vn
            for g in range(GMLP_GROUPS):
                gs = slice(g * GMLP_GROUP, (g + 1) * GMLP_GROUP)
                mixed = _dot(wc_ref[g], vn[:, gs].astype(BF16)) + b_ref[:, gs]
                yc_ref[c0 - r0:c0 - r0 + c, gs] = (u_ref[rs, gs].astype(F32) * mixed
                                                   * z_ref[rs, gs].astype(F32)).astype(BF16)
        rs = slice(r0, r0 + sub)
        ym = _mem_attend(mq_ref[rs, :], mg_ref[rs, :], mk_ref[...], mv_ref[...])
        acc = _dot(yc_ref[...], wy_ref[...]) + _dot(ym, wm_ref[...])
        ms = jnp.mean(acc * acc, axis=-1, keepdims=True)
        o_ref[rs, :] = x_ref[rs, :] + acc * lax.rsqrt(ms + NORM_EPS) * g_ref[...]


def _gmlp_out_proj(proj, memkv, gmlp_ln, w_mix, bias_rows, w, g, x, tm):
    m, d = x.shape
    c = GMLP_CHUNK
    gw = GMLP_WIDTH
    sub = min(OUT_PROJ_SUB_ROWS, tm)
    assert m % tm == 0 and tm % sub == 0 and sub % c == 0 and w.shape[0] == gw + MEM_W
    col = lambda off: pl.BlockSpec((tm, gw), lambda i: (i, off // gw))
    mem_specs, mem_args = _mem_specs(proj, O_MQ, O_MG, memkv, tm, w.shape[0], d)
    return pl.pallas_call(
        _gmlp_out_proj_kernel,
        grid=(m // tm,),
        in_specs=[col(O_U), col(O_V), col(O_Z),
                  pl.BlockSpec((1, gw), lambda i: (0, 0)),
                  pl.BlockSpec(w_mix.shape, lambda i: (0, 0, 0)),
                  pl.BlockSpec((c, gw), lambda i: (0, 0)),
                  pl.BlockSpec((gw, d), lambda i: (0, 0), pipeline_mode=pl.Buffered(1))]
                 + mem_specs
                 + [pl.BlockSpec((1, d), lambda i: (0, 0)),
                    pl.BlockSpec((tm, d), lambda i: (i, 0))],
        out_specs=[pl.BlockSpec((tm, d), lambda i: (i, 0)),
                   pl.BlockSpec((c, gw), lambda i: (0, 0))],
        out_shape=[jax.ShapeDtypeStruct((m, d), F32),
                   jax.ShapeDtypeStruct((c, gw), F32)],
        scratch_shapes=[pltpu.VMEM(w_mix.shape, BF16), pltpu.VMEM((sub, gw), BF16)],
        compiler_params=_params("arbitrary"),
        name="gmlp_out_proj",
    )(proj, proj, proj, gmlp_ln.reshape(1, gw), w_mix, bias_rows, w, *mem_args, w, g.reshape(1, d), x)


def _head_rows_kernel(x_ref, k_ref, v_ref, *, nh):
    rows = x_ref.shape[0]
    hd = k_ref.shape[1]
    for h in range(nh):
        dst = pl.ds(h, rows, stride=nh)
        k_ref[dst, :] = x_ref[:, h * hd:(h + 1) * hd]
        v_ref[dst, :] = x_ref[:, (nh + h) * hd:(nh + h + 1) * hd]


def _head_rows(kv, nh, hd, tr):
    p = kv.shape[0]
    assert p % tr == 0 and kv.shape[1] == 2 * nh * hd
    out = pl.BlockSpec((tr * nh, hd), lambda i: (i, 0))
    return pl.pallas_call(
        functools.partial(_head_rows_kernel, nh=nh),
        grid=(p // tr,),
        in_specs=[pl.BlockSpec((tr, 2 * nh * hd), lambda i: (i, 0))],
        out_specs=[out, out],
        out_shape=[jax.ShapeDtypeStruct((p * nh, hd), kv.dtype)] * 2,
        compiler_params=_params("arbitrary"),
        name="head_rows",
    )(kv)


def _pad_steps(a, pad):
    nseq, seq, n = a.shape
    return jnp.pad(a, ((0, 0), (0, pad - seq), (0, 0))).reshape(nseq * pad, n)


def kernel(x_prompt, x_sample, state_ret, cache_swa_k, cache_swa_v, cache_mem_k, cache_mem_v, mem_prompt,
           pre_norm, post_norm, mem_norm, w_mem_k, w_mem_v, w_in_even, ret_gn, w_out_even,
           w_in_odd, gmlp_ln, w_spatial, b_spatial, w_out_odd):
    bp, tp, d = x_prompt.shape
    nseq, seq, _ = x_sample.shape
    pad = SAMPLE_PAD
    rows_s = nseq * pad
    c = GMLP_CHUNK
    assert bp == 1 and seq <= pad and rows_s % c == 0 and c % pad == 0
    hp = x_prompt.reshape(tp, d)
    hs = _pad_steps(x_sample, pad)
    mem = mem_prompt.reshape(MEM_LEN, d)
    tm_p = 1024 if tp % 1024 == 0 else 512
    unpad = lambda a: a.reshape(nseq, pad, -1)[:, :seq]

    pos_pad = np.concatenate([PAST_LEN + np.arange(seq), np.zeros((pad - seq,), np.int64)])
    ret_cos_p, ret_sin_p = _rot_tables_range(tp, RET_DK, RET_THETA, RET_DK)
    ret_cos_s, ret_sin_s = _rot_tables(np.tile(pos_pad, nseq), RET_DK, RET_THETA, RET_DK)
    swa_cos_p, swa_sin_p = _rot_tables_range(tp, ROPE_DIMS, ROPE_THETA, SWA_HD)
    swa_cos_s, swa_sin_s = _rot_tables(pos_pad, ROPE_DIMS, ROPE_THETA, SWA_HD)
    swa_cos_s = np.tile(swa_cos_s, (1, SWA_HEADS))
    swa_sin_s = np.tile(swa_sin_s, (1, SWA_HEADS))

    w_mem0 = jnp.concatenate([w_mem_k[0], w_mem_v[0]], axis=1).astype(BF16)
    memkv0 = _norm_matmul(mem, mem_norm[0], w_mem0, MEM_LEN, 2 * MEM_W)

    proj_s, w_in0 = _norm_matmul_cast(hs, pre_norm[0], w_in_even[0], 1024)
    proj_s3 = proj_s.reshape(nseq, pad, EVEN_IN)
    ya_s, st_s = _ret_sample(proj_s, state_ret[0], ret_cos_s, ret_sin_s, ret_gn[0], nseq, seq, pad)
    yb_s, sk = _swa_sample(proj_s3, cache_swa_k[0], cache_swa_v[0], swa_cos_s, swa_sin_s, seq)
    sk = sk[:, :seq].reshape(nseq, seq, SWA_HEADS, SWA_HD)
    sv = unpad(proj_s[:, E_BV:E_BV + SWA_HEADS * SWA_HD]).reshape(nseq, seq, SWA_HEADS, SWA_HD)

    buf_p = min(SWA_MAX_WINDOW, tp)
    proj_p, kv_last, w_out0, w_in1, w_out1 = _norm_matmul(
        hp, pre_norm[0], w_in0, tm_p, 1024, BF16, (tp - buf_p, E_BK, E_BG - E_BK),
        casts=[(w_out_even[0], 256), (w_in_odd[0], 1024), (w_out_odd[0], 256)],
        rot=(swa_cos_p, swa_sin_p, E_BQ, E_BK, SWA_HD ** -0.5 * LOG2E))
    hs1 = _mem_out_proj_rows(proj_s3, E_MQ, E_MG, cache_mem_k, cache_mem_v, 0,
                             [ya_s, yb_s.reshape(rows_s, -1)], w_out0, post_norm[0], hs)
    ya_p, st_p = _ret_prompt(proj_p, ret_cos_p, ret_sin_p, ret_gn[0])
    yb_p = _swa_prompt(proj_p)
    pk, pv = _head_rows(kv_last, SWA_HEADS, SWA_HD, 256)
    hp1 = _out_proj([ya_p, yb_p], w_out0, post_norm[0], hp, 512, mem=(proj_p, E_MQ, E_MG, memkv0))

    w_mem1 = jnp.concatenate([w_mem_k[1], w_mem_v[1]], axis=1).astype(BF16)
    memkv1 = _norm_matmul(mem, mem_norm[1], w_mem1, MEM_LEN, 2 * MEM_W)
    bias_p = jnp.repeat(b_spatial[0].T, GMLP_GROUP, axis=1)
    blockdiag = lambda wg: jnp.kron(jnp.eye(c // pad, dtype=F32),
                                    jnp.pad(wg[:seq, :seq], ((0, pad - seq), (0, pad - seq))))
    w_mix_s = jax.vmap(blockdiag)(w_spatial[0])
    bias_s = jnp.tile(jnp.pad(bias_p[:seq], ((0, pad - seq), (0, 0))), (c // pad, 1))

    proj_s1 = _norm_matmul(hs1, pre_norm[1], w_in1, rows_s, 1024)
    yc_s, gv_s = _gmlp(proj_s1, gmlp_ln[0], w_mix_s, bias_s, True, False)
    hs2 = _mem_out_proj_rows(proj_s1.reshape(nseq, pad, ODD_IN), O_MQ, O_MG, cache_mem_k, cache_mem_v, 1,
                             [yc_s], w_out1, post_norm[1], hs1)

    proj_p1 = _norm_matmul(hp1, pre_norm[1], w_in1, tm_p, 1024, BF16,
                           acts=[(O_U, O_Z, "gelu"), (O_Z, O_MQ, "silu")])
    hp2, gv_p = _gmlp_out_proj(proj_p1, memkv1, gmlp_ln[0], w_spatial[0], bias_p, w_out1, post_norm[1], hp1, 512)

    memkv = jnp.stack([memkv0, memkv1])
    p_mk = memkv[:, :, :MEM_W].reshape(2, bp, MEM_LEN, MEM_HEADS, MEM_HD)
    p_mv = memkv[:, :, MEM_W:].reshape(2, bp, MEM_LEN, MEM_HEADS, MEM_HD)
    return (hp2.reshape(bp, tp, d), unpad(hs2),
            st_p[None, None],
            pk.reshape(1, bp, buf_p, SWA_HEADS, SWA_HD), pv.reshape(1, bp, buf_p, SWA_HEADS, SWA_HD),
            p_mk, p_mv,
            gv_p[None, None],
            st_s[None],
            sk[None], sv[None],
            unpad(gv_s)[None])
```

```python
import functools
import math

import jax
import jax.numpy as jnp
import numpy as np
from jax import lax
from jax.experimental import pallas as pl
from jax.experimental.pallas import tpu as pltpu

F32 = jnp.float32
BF16 = jnp.bfloat16

D_MODEL = 2048
PAST_LEN = 8192
RET_HEADS = 8
RET_DK = 128
RET_DV = 256
RET_CHUNK = 128
RET_THETA = 10000.0
SWA_HEADS = 8
SWA_HD = 128
SWA_PATTERNS = ((128, 1), (512, 4), (2048, 16))
SWA_MAX_WINDOW = 2048
SWA_BLOCK = 128
ROPE_THETA = 500000.0
ROPE_DIMS = SWA_HD // 4
GMLP_CHUNK = 128
GMLP_WIDTH = D_MODEL
GMLP_GROUP = 128
GMLP_GROUPS = GMLP_WIDTH // GMLP_GROUP
MEM_LEN = 256
MEM_HEADS = 4
MEM_HD = 128
MEM_W = MEM_HEADS * MEM_HD
NORM_EPS = 1e-6
NEG_INF = -1e30
LOG2E = math.log2(math.e)

E_AQ = 0
E_AK = E_AQ + RET_HEADS * RET_DK
E_AV = E_AK + RET_HEADS * RET_DK
E_AG = E_AV + RET_HEADS * RET_DV
E_BQ = E_AG + RET_HEADS * RET_DV
E_BK = E_BQ + SWA_HEADS * SWA_HD
E_BV = E_BK + SWA_HEADS * SWA_HD
E_BG = E_BV + SWA_HEADS * SWA_HD
E_MQ = E_BG + SWA_HEADS * SWA_HD
E_MG = E_MQ + MEM_W
EVEN_IN = E_MG + MEM_W
O_U = 0
O_V = O_U + GMLP_WIDTH
O_Z = O_V + GMLP_WIDTH
O_MQ = O_Z + GMLP_WIDTH
O_MG = O_MQ + MEM_W
ODD_IN = O_MG + MEM_W

SWA_TILE = SWA_MAX_WINDOW
SAMPLE_PAD = 8
GMLP_STEP_CHUNKS = 2
SWA_GROUP = 1
RET_PROMPT_CHUNK = 256
RET_STEP_CHUNKS = 1
MEM_ROWS_STEP_SEQS = 4
RET_STEP_SEQS = 4
OUT_PROJ_SUB_ROWS = 256
NORM_SUB_ROWS = 256
VMEM_LIMIT = 56 * 1024 * 1024


def _params(*sem):
    return pltpu.CompilerParams(dimension_semantics=sem, vmem_limit_bytes=VMEM_LIMIT)


def _dot(a, b):
    return jnp.dot(a, b, preferred_element_type=F32)


def _dot_nt(a, b):
    return lax.dot_general(a, b, (((1,), (1,)), ((), ())), preferred_element_type=F32)


def _silu(x):
    return x / (1.0 + jnp.exp2(x * (-LOG2E)))


def _gelu(x):
    a = -2.0 * math.sqrt(2.0 / math.pi) * LOG2E
    return x / (1.0 + jnp.exp2(x * (a + (a * 0.044715) * (x * x))))


def _div(x, n):
    assert n & (n - 1) == 0
    return jnp.right_shift(x, n.bit_length() - 1)


def _mod(x, n):
    assert n & (n - 1) == 0
    return jnp.bitwise_and(x, n - 1)


def _rot_lanes(n_rot, theta, width):
    half = n_rot // 2
    inv = 1.0 / (theta ** (np.arange(half, dtype=np.float64) / half))
    rest = np.zeros(width - n_rot)
    return np.concatenate([inv, inv, rest]), np.concatenate([-np.ones(half), np.ones(half), rest])


def _rot_tables(pos, n_rot, theta, width):
    f, sgn = _rot_lanes(n_rot, theta, width)
    ang = np.asarray(pos, np.float64)[:, None] * f[None, :]
    return np.cos(ang).astype(np.float32), (sgn * np.sin(ang)).astype(np.float32)


def _rot_tables_range(n, n_rot, theta, width, blk=128):
    assert n % blk == 0
    f, sgn = _rot_lanes(n_rot, theta, width)
    hi = (np.arange(n // blk, dtype=np.float64) * blk)[:, None] * f[None, :]
    lo = np.arange(blk, dtype=np.float64)[:, None] * f[None, :]
    dev = lambda a: jnp.asarray(a, F32)
    ch, sh, cl, sl = dev(np.cos(hi)[:, None]), dev(np.sin(hi)[:, None]), dev(np.cos(lo)[None]), dev(np.sin(lo)[None])
    chs, shs = dev((sgn * np.cos(hi))[:, None]), dev((sgn * np.sin(hi))[:, None])
    return (ch * cl - sh * sl).reshape(n, width), (shs * cl + chs * sl).reshape(n, width)


def _rot_full(x, cos, sin):
    return x * cos + pltpu.roll(x, RET_DK // 2, 1) * sin


def _rot_partial(x, cos, sin):
    half = ROPE_DIMS // 2
    lane = _mod(lax.broadcasted_iota(jnp.int32, x.shape, 1), SWA_HD)
    width = x.shape[1]
    partner = jnp.where(lane < half, pltpu.roll(x, width - half, 1), pltpu.roll(x, half, 1))
    return x * cos + partner * sin


def _norm_matmul_kernel(x_ref, g_ref, w_ref, *rest, side, cast_cols, rot, acts, ntiles):
    if rot is not None:
        cos_ref, sin_ref, rest = rest[0], rest[1], rest[2:]
    nc = len(cast_cols)
    cast_in, o_ref, rest = rest[:nc], rest[nc], rest[nc + 1:]
    cast_out, xn_ref = rest[len(rest) - 1 - nc:len(rest) - 1], rest[-1]
    tm = x_ref.shape[0]
    sub = min(NORM_SUB_ROWS, tm)

    for src, dst, ncol in zip(cast_in, cast_out, cast_cols):
        @pl.when(pl.program_id(1) < ncol)
        def _(src=src, dst=dst):
            dst[...] = src[...].astype(BF16)

    kind = {}
    if rot is not None:
        jq, jk, q_scale = rot
        kind[jq], kind[jk] = "rope_q", "rope_k"
    for lo, hi, name in acts:
        kind.update({t: name for t in range(lo, hi)})

    def epilogue(name):
        if name in ("rope_q", "rope_k"):
            nrep = o_ref.shape[1] // cos_ref.shape[1]
            cos, sin = jnp.tile(cos_ref[...], (1, nrep)), jnp.tile(sin_ref[...], (1, nrep))
            return (lambda acc: _rot_partial(acc * q_scale, cos, sin)) if name == "rope_q" else (
                lambda acc: _rot_partial(acc, cos, sin))
        return {"gelu": _gelu, "silu": _silu, None: lambda acc: acc}[name]

    j = pl.program_id(1)

    @pl.when(j == 0)
    def _():
        first = epilogue(kind.get(0))
        for r0 in range(0, tm, sub):
            rs = slice(r0, r0 + sub)
            x = x_ref[rs, :]
            ms = jnp.mean(x * x, axis=-1, keepdims=True)
            xn = (x * lax.rsqrt(ms + NORM_EPS) * g_ref[...]).astype(BF16)
            xn_ref[rs, :] = xn
            o_ref[rs, :] = first(_dot(xn, w_ref[...])).astype(o_ref.dtype)

    def tail(name):
        acc = epilogue(name)(_dot(xn_ref[...], w_ref[...]))
        o_ref[...] = acc.astype(o_ref.dtype)
        if side is not None:
            i0, j0, nj = side
            i = pl.program_id(0)

            @pl.when((i >= i0) & (j >= j0) & (j < j0 + nj))
            def _():
                rest[0][...] = acc

    for name in sorted({kind.get(t) for t in range(1, ntiles)}, key=str):
        tiles = [t for t in range(1, ntiles) if kind.get(t) == name]
        pl.when(functools.reduce(jnp.logical_or, [j == t for t in tiles]))(functools.partial(tail, name))


def _norm_matmul(x, g, w, tm, tn, out_dtype=F32, side=None, casts=(), rot=None, acts=()):
    m, d = x.shape
    n = w.shape[1]
    ni, nj_grid = m // tm, n // tn
    out_specs = [pl.BlockSpec((tm, tn), lambda i, j: (i, j))]
    out_shape = [jax.ShapeDtypeStruct((m, n), out_dtype)]
    tiles = None
    if side is not None:
        row0, col0, ncols = side
        assert row0 % tm == 0 and col0 % tn == 0 and ncols % tn == 0 and col0 >= tn
        i0, j0, nj = row0 // tm, col0 // tn, ncols // tn
        tiles = (i0, j0, nj)
        out_specs.append(pl.BlockSpec(
            (tm, tn), lambda i, j: (jnp.maximum(i - i0, 0), jnp.where(i >= i0, jnp.clip(j - j0, 0, nj - 1), 0))))
        out_shape.append(jax.ShapeDtypeStruct((m - row0, ncols), F32))
    rot_tiles, rot_specs, rot_args = None, [], []
    if rot is not None:
        cos, sin, q_col, k_col, q_scale = rot
        assert q_col % tn == 0 and k_col % tn == 0 and q_col >= tn and k_col >= tn and tn % cos.shape[1] == 0
        rot_tiles = (q_col // tn, k_col // tn, q_scale)
        rot_specs = [pl.BlockSpec((tm, cos.shape[1]), lambda i, j: (i, 0))] * 2
        rot_args = [cos, sin]
    assert all(lo % tn == 0 and hi % tn == 0 for lo, hi, _ in acts)
    act_tiles = tuple((lo // tn, hi // tn, name) for lo, hi, name in acts)
    cast_specs, cast_cols = [], []
    for a, bc in casts:
        r, c = a.shape
        assert r % ni == 0 and (r // ni) % 16 == 0 and c % bc == 0 and c // bc <= nj_grid
        ncol = c // bc
        spec = pl.BlockSpec((r // ni, bc), functools.partial(lambda i, j, last: (i, jnp.minimum(j, last)), last=ncol - 1))
        cast_specs.append(spec)
        cast_cols.append(ncol)
        out_specs.append(spec)
        out_shape.append(jax.ShapeDtypeStruct((r, c), BF16))
    res = pl.pallas_call(
        functools.partial(_norm_matmul_kernel, side=tiles, cast_cols=tuple(cast_cols), rot=rot_tiles,
                          acts=act_tiles, ntiles=nj_grid),
        grid=(ni, nj_grid),
        in_specs=[pl.BlockSpec((tm, d), lambda i, j: (i, 0)),
                  pl.BlockSpec((1, d), lambda i, j: (0, 0)),
                  pl.BlockSpec((d, tn), lambda i, j: (0, j))] + rot_specs + cast_specs,
        out_specs=out_specs,
        out_shape=out_shape,
        scratch_shapes=[pltpu.VMEM((tm, d), BF16)],
        compiler_params=_params("arbitrary", "arbitrary"),
        name="norm_matmul",
    )(x, g.reshape(1, d), w, *rot_args, *[a for a, _ in casts])
    return res if len(res) > 1 else res[0]


def _norm_matmul_cast_kernel(x_ref, g_ref, w_ref, o_ref, wb_ref, xn_ref):
    @pl.when(pl.program_id(0) == 0)
    def _():
        x = x_ref[...]
        ms = jnp.mean(x * x, axis=-1, keepdims=True)
        xn_ref[...] = (x * lax.rsqrt(ms + NORM_EPS) * g_ref[...]).astype(BF16)

    wb = w_ref[...].astype(BF16)
    wb_ref[...] = wb
    o_ref[...] = _dot(xn_ref[...], wb).astype(o_ref.dtype)


def _norm_matmul_cast(x, g, w, tn):
    m, d = x.shape
    n = w.shape[1]
    return pl.pallas_call(
        _norm_matmul_cast_kernel,
        grid=(n // tn,),
        in_specs=[pl.BlockSpec((m, d), lambda j: (0, 0)),
                  pl.BlockSpec((1, d), lambda j: (0, 0)),
                  pl.BlockSpec((d, tn), lambda j: (0, j))],
        out_specs=[pl.BlockSpec((m, tn), lambda j: (0, j)),
                   pl.BlockSpec((d, tn), lambda j: (0, j))],
        out_shape=[jax.ShapeDtypeStruct((m, n), F32),
                   jax.ShapeDtypeStruct((d, n), BF16)],
        scratch_shapes=[pltpu.VMEM((m, d), BF16)],
        compiler_params=_params("arbitrary"),
        name="norm_matmul_cast",
    )(x, g.reshape(1, d), w)


def _ret_tables(c, reps, pad=None):
    pad = c if pad is None else pad
    rows = pad * reps
    lg = np.log1p(-np.exp2(-5.0 - np.arange(RET_HEADS, dtype=np.float64)))
    row = np.arange(rows)
    step = (row % pad).astype(np.float64)
    live = (row % pad) < c
    rel = step[:, None] - step[None, :]
    keep = ((row // pad)[:, None] == (row // pad)[None, :]) & (rel >= 0) & live[:, None] & live[None, :]
    decay = np.where(keep[None], np.exp(lg[:, None, None] * np.maximum(rel, 0.0)[None]), 0.0)
    q_decay = np.where(live[None], np.exp(lg[:, None] * (step[None, :] + 1.0)), 0.0)
    k_decay = np.where(live[None], np.exp(lg[:, None] * (c - 1.0 - step[None, :])), 0.0)
    chunk_decay = np.exp(lg * c)
    f32 = lambda a, shape: np.ascontiguousarray(np.broadcast_to(a, shape), dtype=np.float32)
    qd = f32(q_decay[:, :, None], (RET_HEADS, rows, RET_DK))
    kd = f32(k_decay[:, :, None], (RET_HEADS, rows, RET_DK))
    cd = f32(chunk_decay[:, None, None], (RET_HEADS, 1, RET_DV))
    return decay.astype(np.float32), qd, kd, cd


def _group_norm_gate(o, gn, g):
    mu = jnp.mean(o, axis=-1, keepdims=True)
    d = o - mu
    var = jnp.mean(d * d, axis=-1, keepdims=True)
    return d * lax.rsqrt(var + NORM_EPS) * gn * _silu(g.astype(F32))


def _ret_prompt_kernel(q_ref, k_ref, v_ref, g_ref, cos_ref, sin_ref, dec_ref, qd_ref, kd_ref, cd_ref, gn_ref,
                       y_ref, st_ref):
    @pl.when(pl.program_id(0) == 0)
    def _():
        st_ref[...] = jnp.zeros_like(st_ref)

    c = dec_ref.shape[1]
    for i in range(q_ref.shape[0] // c):
        rs = slice(i * c, (i + 1) * c)
        cos = cos_ref[rs, :]
        sin = sin_ref[rs, :]
        for h in range(RET_HEADS):
            ks = slice(h * RET_DK, (h + 1) * RET_DK)
            vs = slice(h * RET_DV, (h + 1) * RET_DV)
            q = _rot_full(q_ref[rs, ks].astype(F32), cos, sin)
            k = _rot_full(k_ref[rs, ks].astype(F32) * (RET_DK ** -0.5), cos, sin)
            v = v_ref[rs, vs].astype(BF16)
            st = st_ref[h]
            s = _dot_nt(q.astype(BF16), k.astype(BF16)) * dec_ref[h]
            lhs = jnp.concatenate([s.astype(BF16), (q * qd_ref[h]).astype(BF16)], axis=1)
            o = _dot(lhs, jnp.concatenate([v, st.astype(BF16)], axis=0))
            st_ref[h] = cd_ref[h] * st + _dot((k * kd_ref[h]).T.astype(BF16), v)
            y_ref[rs, vs] = _group_norm_gate(o, gn_ref[:, vs], g_ref[rs, vs]).astype(y_ref.dtype)


def _ret_prompt(proj, cos, sin, ret_gn):
    t = proj.shape[0]
    dec, qd, kd, cd = _ret_tables(RET_PROMPT_CHUNK, 1)
    c = RET_STEP_CHUNKS * RET_PROMPT_CHUNK
    assert t % c == 0
    qw = RET_HEADS * RET_DK
    vw = RET_HEADS * RET_DV
    full3 = lambda n: (0, 0, 0)
    return pl.pallas_call(
        _ret_prompt_kernel,
        grid=(t // c,),
        in_specs=[pl.BlockSpec((c, qw), lambda n: (n, E_AQ // qw)),
                  pl.BlockSpec((c, qw), lambda n: (n, E_AK // qw)),
                  pl.BlockSpec((c, vw), lambda n: (n, E_AV // vw)),
                  pl.BlockSpec((c, vw), lambda n: (n, E_AG // vw)),
                  pl.BlockSpec((c, RET_DK), lambda n: (n, 0)),
                  pl.BlockSpec((c, RET_DK), lambda n: (n, 0)),
                  pl.BlockSpec(dec.shape, full3),
                  pl.BlockSpec(qd.shape, full3),
                  pl.BlockSpec(kd.shape, full3),
                  pl.BlockSpec(cd.shape, full3),
                  pl.BlockSpec((1, vw), lambda n: (0, 0))],
        out_specs=[pl.BlockSpec((c, vw), lambda n: (n, 0)),
                   pl.BlockSpec((RET_HEADS, RET_DK, RET_DV), full3)],
        out_shape=[jax.ShapeDtypeStruct((t, vw), BF16),
                   jax.ShapeDtypeStruct((RET_HEADS, RET_DK, RET_DV), F32)],
        compiler_params=_params("arbitrary"),
        name="ret_prompt",
    )(proj, proj, proj, proj, cos, sin, dec, qd, kd, cd, ret_gn.reshape(1, vw))


def _ret_sample_kernel(q_ref, k_ref, v_ref, g_ref, cos_ref, sin_ref, dec_ref, qd_ref, kd_ref, cd_ref, gn_ref,
                       st_ref, y_ref, sto_ref, cross_ref, qs_ref, kt_ref, *, pad):
    b = pl.program_id(0)
    nb = pl.num_programs(0)
    win = RET_CHUNK

    @pl.when(b == 0)
    def _():
        cross_ref[...] = jnp.zeros_like(cross_ref)
        cos = cos_ref[...]
        sin = sin_ref[...]
        for h in range(RET_HEADS):
            ks = slice(h * RET_DK, (h + 1) * RET_DK)
            qs_ref[:, ks] = _rot_full(q_ref[:, ks], cos, sin) * qd_ref[h]
            kt_ref[h] = (_rot_full(k_ref[:, ks] * (RET_DK ** -0.5), cos, sin) * kd_ref[h]).T

    for i in range(st_ref.shape[0]):
        sq = b * st_ref.shape[0] + i
        w0 = pl.multiple_of(jnp.right_shift(sq * pad, win.bit_length() - 1) * win, win)
        wrows = pl.ds(w0, win)
        rows = lax.broadcasted_iota(jnp.int32, (win, RET_DK), 0) + w0
        cols = lax.broadcasted_iota(jnp.int32, (RET_DK, win), 1) + w0
        mine_r = (rows >= sq * pad) & (rows < (sq + 1) * pad)
        mine_c = (cols >= sq * pad) & (cols < (sq + 1) * pad)
        for h in range(RET_HEADS):
            ks = slice(h * RET_DK, (h + 1) * RET_DK)
            vs = slice(h * RET_DV, (h + 1) * RET_DV)
            st = st_ref[i, h]
            qm = jnp.where(mine_r, qs_ref[wrows, ks], 0.0)
            kmt = jnp.where(mine_c, kt_ref[h, :, wrows], 0.0)
            cross_ref[wrows, vs] += _dot(qm.astype(BF16), st.astype(BF16))
            sto_ref[i, h] = cd_ref[h] * st + _dot(kmt.astype(BF16), v_ref[wrows, vs].astype(BF16))

    @pl.when(b == nb - 1)
    def _():
        cos_all = cos_ref[...]
        sin_all = sin_ref[...]
        for h in range(RET_HEADS):
            ks = slice(h * RET_DK, (h + 1) * RET_DK)
            vs = slice(h * RET_DV, (h + 1) * RET_DV)
            q = _rot_full(q_ref[:, ks], cos_all, sin_all)
            k = _rot_full(k_ref[:, ks] * (RET_DK ** -0.5), cos_all, sin_all)
            v = v_ref[:, vs].astype(BF16)
            s = _dot_nt(q.astype(BF16), k.astype(BF16)) * dec_ref[h]
            o = _dot(s.astype(BF16), v) + cross_ref[:, vs]
            y_ref[:, vs] = _group_norm_gate(o, gn_ref[:, vs], g_ref[:, vs]).astype(y_ref.dtype)


def _ret_sample(proj, state, cos, sin, ret_gn, nseq, seq, pad):
    rows = nseq * pad
    c = math.gcd(seq, RET_CHUNK)
    assert c == seq, "sample chunk must cover the new tokens"
    assert rows % RET_CHUNK == 0 and RET_CHUNK % pad == 0
    dec, qd, kd, cd = _ret_tables(c, nseq, pad)
    qw = RET_HEADS * RET_DK
    vw = RET_HEADS * RET_DV
    full3 = lambda b: (0, 0, 0)
    n = RET_STEP_SEQS
    assert nseq % n == 0
    st_spec = pl.BlockSpec((n, RET_HEADS, RET_DK, RET_DV), lambda b: (b, 0, 0, 0))
    return pl.pallas_call(
        functools.partial(_ret_sample_kernel, pad=pad),
        grid=(nseq // n,),
        in_specs=[pl.BlockSpec((rows, qw), lambda b: (0, E_AQ // qw)),
                  pl.BlockSpec((rows, qw), lambda b: (0, E_AK // qw)),
                  pl.BlockSpec((rows, vw), lambda b: (0, E_AV // vw)),
                  pl.BlockSpec((rows, vw), lambda b: (0, E_AG // vw)),
                  pl.BlockSpec((rows, RET_DK), lambda b: (0, 0)),
                  pl.BlockSpec((rows, RET_DK), lambda b: (0, 0)),
                  pl.BlockSpec(dec.shape, full3),
                  pl.BlockSpec(qd.shape, full3),
                  pl.BlockSpec(kd.shape, full3),
                  pl.BlockSpec(cd.shape, full3),
                  pl.BlockSpec((1, vw), lambda b: (0, 0)),
                  st_spec],
        out_specs=[pl.BlockSpec((rows, vw), lambda b: (0, 0)), st_spec],
        out_shape=[jax.ShapeDtypeStruct((rows, vw), F32),
                   jax.ShapeDtypeStruct(state.shape, F32)],
        scratch_shapes=[pltpu.VMEM((rows, vw), F32),
                        pltpu.VMEM((rows, qw), F32),
                        pltpu.VMEM((RET_HEADS, RET_DK, rows), F32)],
        compiler_params=_params("arbitrary"),
        name="ret_sample",
    )(proj, proj, proj, proj, cos, sin, dec, qd, kd, cd, ret_gn.reshape(1, vw), state)


def _swa_prompt_kernel(q_ref, kc_ref, vc_ref, g_ref, y_ref, q_s, k_s, v_s, m_s, l_s, acc_s):
    t = pl.program_id(1)
    tb = SWA_TILE
    blk = SWA_BLOCK
    hd = SWA_HD

    @pl.when(t == 0)
    def _():
        k_s[0:tb, :] = jnp.zeros((tb, hd), F32)
        v_s[0:tb, :] = jnp.zeros((tb, hd), F32)

    @pl.when(t > 0)
    def _():
        k_s[0:tb, :] = k_s[tb:2 * tb, :]
        v_s[0:tb, :] = v_s[tb:2 * tb, :]

    q_s[...] = q_ref[...].astype(F32)
    k_s[tb:2 * tb, :] = kc_ref[...].astype(F32)
    v_s[tb:2 * tb, :] = vc_ref[...].astype(F32)

    def ds(start, size, stride):
        return pl.ds(start, size) if stride == 1 else pl.ds(start, size, stride=stride)

    def band(g):
        qi = lax.broadcasted_iota(jnp.int32, (g * blk, (g + 1) * blk), 0)
        kj = lax.broadcasted_iota(jnp.int32, (g * blk, (g + 1) * blk), 1)
        ok = (kj >= qi) & (kj <= qi + blk)
        return ok, ok & ((kj >= blk) | (t > 0))

    for bi, (window, r) in enumerate(SWA_PATTERNS):
        assert window // r == blk
        per_class = tb // (blk * r)
        g = SWA_GROUP if per_class % SWA_GROUP == 0 else 1
        ok, ok_first = band(g)
        ones = jnp.ones(((g + 1) * blk, hd), BF16)
        for c in range(r):
            for n in range(0, per_class, g):
                qs = n * blk * r + c
                ks = qs + tb - blk * r
                qrows = ds(qs, g * blk, r)
                krows = ds(ks, (g + 1) * blk, r)
                s = _dot_nt(q_s[qrows, :].astype(BF16), k_s[krows, :].astype(BF16))
                s = jnp.where(ok_first if n == 0 else ok, s, NEG_INF)
                m = jnp.max(s, axis=-1, keepdims=True)
                p = jnp.exp2(s - m).astype(BF16)
                pv = _dot(p, jnp.concatenate([v_s[krows, :].astype(BF16), ones], axis=1))
                m_s[bi, qrows, :] = jnp.broadcast_to(m, (g * blk, hd))
                l_s[bi, qrows, :] = pv[:, hd:]
                acc_s[bi, qrows, :] = pv[:, :hd]

    nbr = len(SWA_PATTERNS)
    ms = [m_s[i] for i in range(nbr)]
    mmax = functools.reduce(jnp.maximum, ms)
    wts = [jnp.exp2(mi - mmax) for mi in ms]
    num = sum(wts[i] * acc_s[i] for i in range(nbr))
    den = sum(wts[i] * l_s[i] for i in range(nbr))
    y_ref[...] = (num / den * _silu(g_ref[...].astype(F32))).astype(y_ref.dtype)


def _swa_prompt(proj):
    t = proj.shape[0]
    tb = SWA_TILE
    assert t % tb == 0
    hd = SWA_HD
    nbr = len(SWA_PATTERNS)
    cur = lambda off: pl.BlockSpec((tb, hd), lambda h, i: (i, off // hd + h))
    return pl.pallas_call(
        _swa_prompt_kernel,
        grid=(SWA_HEADS, t // tb),
        in_specs=[cur(E_BQ), cur(E_BK), cur(E_BV), cur(E_BG)],
        out_specs=pl.BlockSpec((tb, hd), lambda h, i: (i, h)),
        out_shape=jax.ShapeDtypeStruct((t, SWA_HEADS * hd), BF16),
        scratch_shapes=[pltpu.VMEM((tb, hd), F32),
                        pltpu.VMEM((2 * tb, hd), F32),
                        pltpu.VMEM((2 * tb, hd), F32),
                        pltpu.VMEM((nbr, tb, hd), F32),
                        pltpu.VMEM((nbr, tb, hd), F32),
                        pltpu.VMEM((nbr, tb, hd), F32)],
        compiler_params=_params("arbitrary", "arbitrary"),
        name="swa_prompt",
    )(proj, proj, proj, proj)


def _swa_sample_kernel(q_ref, k_ref, v_ref, g_ref, cos_ref, sin_ref, ka_ref, kb_ref, va_ref, vb_ref,
                       y_ref, ko_ref, kn_s, vn_s, ba_s, bb_s, bn_s, *, seq):
    pad = SAMPLE_PAD
    nh = SWA_HEADS
    hd = SWA_HD
    nr = nh * pad
    nj, nrh, _ = kb_ref.shape[1:]
    na = ka_ref.shape[1]
    nb = nj * nrh
    nn = kn_s.shape[0]

    @pl.when(pl.program_id(0) == 0)
    def _():
        (w1, d1), (w2, d2), (w3, d3) = SWA_PATTERNS

        def rows_cols(n):
            r = lax.broadcasted_iota(jnp.int32, (nr, n), 0)
            c = lax.broadcasted_iota(jnp.int32, (nr, n), 1)
            return _div(r, pad), _mod(r, pad), c

        def log2_mult(mult):
            return jnp.where(mult > 0, jnp.log2(jnp.maximum(mult, 1.0)), NEG_INF)

        hq, tq, c = rows_cols(na)
        dist = na // nh + tq - _div(c, nh)
        mult_a = ((dist <= w1).astype(F32) + ((_mod(dist, d2) == 0) & (dist <= w2)).astype(F32)
                  + ((_mod(dist, d3) == 0) & (dist <= w3)).astype(F32))
        ba_s[...] = log2_mult(jnp.where((hq == _mod(c, nh)) & (tq < seq), mult_a, 0.0))
        hq, tq, c = rows_cols(nb)
        bb_s[...] = log2_mult(jnp.where((hq == _mod(c, nh)) & (tq == _mod(_div(c, nh), nrh // nh)), 1.0, 0.0))
        hq, tq, c = rows_cols(nn)
        tk = _mod(c, pad)
        mult_n = (tk <= tq).astype(F32) + 2.0 * (tk == tq).astype(F32)
        bn_s[...] = log2_mult(jnp.where((hq == _div(c, pad)) & (tq < seq) & (tk < seq), mult_n, 0.0))
        kn_s[...] = jnp.zeros_like(kn_s)
        vn_s[...] = jnp.zeros_like(vn_s)

    cos = cos_ref[...]
    sin = sin_ref[...]
    q8 = _rot_partial(q_ref[0] * (SWA_HD ** -0.5 * LOG2E), cos, sin)
    k8 = _rot_partial(k_ref[0], cos, sin)
    ko_ref[0] = k8

    def head_rows(x):
        return jnp.concatenate([x[:, h * hd:(h + 1) * hd] for h in range(nh)], axis=0)

    qrows = head_rows(q8).astype(BF16)
    kn_s[0:nr, :] = head_rows(k8).astype(BF16)
    vn_s[0:nr, :] = head_rows(v_ref[0]).astype(BF16)

    kb = kb_ref[0].reshape(nb, hd)
    vb = vb_ref[0].reshape(nb, hd)
    s_a = _dot_nt(qrows, ka_ref[0].astype(BF16)) + ba_s[...]
    s_b = _dot_nt(qrows, kb.astype(BF16)) + bb_s[...]
    s_n = _dot_nt(qrows, kn_s[...]) + bn_s[...]
    rowmax = lambda s: jnp.max(s, axis=-1, keepdims=True)
    m = jnp.maximum(jnp.maximum(rowmax(s_a), rowmax(s_b)), rowmax(s_n))
    p_a, p_b, p_n = jnp.exp2(s_a - m), jnp.exp2(s_b - m), jnp.exp2(s_n - m)
    rowsum = lambda p: jnp.sum(p, axis=-1, keepdims=True)
    o = (_dot(p_a.astype(BF16), va_ref[0].astype(BF16)) + _dot(p_b.astype(BF16), vb.astype(BF16))
         + _dot(p_n.astype(BF16), vn_s[...])) / (rowsum(p_a) + rowsum(p_b) + rowsum(p_n))
    y = jnp.concatenate([o[h * pad:(h + 1) * pad] for h in range(nh)], axis=1)
    y_ref[0] = y * _silu(g_ref[0])


def _swa_sample(proj_pad, cache_k, cache_v, cos, sin, seq):
    nseq, pad, _ = proj_pad.shape
    buf = cache_k.shape[1]
    nh, hd = SWA_HEADS, SWA_HD
    w = nh * hd
    (w1, d1), (w2, d2), (w3, d3) = SWA_PATTERNS
    assert buf == w3 and d1 == 1 and seq <= d2 and w1 <= w2 and buf % w2 == 0 and buf % d3 == 0
    assert nh * pad <= SWA_BLOCK and (seq * nh) % 8 == 0
    flat = lambda a: a.reshape(nseq, buf * nh, hd)
    strided = lambda a: a.reshape(nseq, buf // d3, d3 * nh, hd)
    col = lambda off: pl.BlockSpec((1, pad, w), lambda b: (b, 0, off // w))
    tail = pl.BlockSpec((1, w2 * nh, hd), lambda b: (b, buf // w2 - 1, 0))
    resid = pl.BlockSpec((1, (buf - w2) // d3, seq * nh, hd), lambda b: (b, 0, 0, 0))
    tab = pl.BlockSpec((pad, w), lambda b: (0, 0))
    out = pl.BlockSpec((1, pad, w), lambda b: (b, 0, 0))
    return pl.pallas_call(
        functools.partial(_swa_sample_kernel, seq=seq),
        grid=(nseq,),
        in_specs=[col(E_BQ), col(E_BK), col(E_BV), col(E_BG), tab, tab, tail, resid, tail, resid],
        out_specs=[out, out],
        out_shape=[jax.ShapeDtypeStruct((nseq, pad, w), F32),
                   jax.ShapeDtypeStruct((nseq, pad, w), F32)],
        scratch_shapes=[pltpu.VMEM((SWA_BLOCK, hd), BF16), pltpu.VMEM((SWA_BLOCK, hd), BF16),
                        pltpu.VMEM((nh * pad, w2 * nh), F32),
                        pltpu.VMEM((nh * pad, (buf - w2) // d3 * seq * nh), F32),
                        pltpu.VMEM((nh * pad, SWA_BLOCK), F32)],
        compiler_params=_params("arbitrary"),
        name="swa_sample",
    )(proj_pad, proj_pad, proj_pad, proj_pad, cos, sin,
      flat(cache_k), strided(cache_k), flat(cache_v), strided(cache_v))


def _mem_attend(q, g, k, v):
    ones = jnp.ones((MEM_LEN, MEM_HD), BF16)
    outs = []
    for h in range(MEM_HEADS):
        hs = slice(h * MEM_HD, (h + 1) * MEM_HD)
        qh = (q[:, hs].astype(F32) * (MEM_HD ** -0.5 * LOG2E)).astype(BF16)
        s = _dot_nt(qh, k[:, hs].astype(BF16))
        m = jnp.max(s, axis=-1, keepdims=True)
        p = jnp.exp2(s - m).astype(BF16)
        pv = _dot(p, jnp.concatenate([v[:, hs].astype(BF16), ones], axis=1))
        o = pv[:, :MEM_HD] / pv[:, MEM_HD:]
        outs.append((o * _silu(g[:, hs].astype(F32))).astype(BF16))
    return jnp.concatenate(outs, axis=1)


def _mem_out_proj_rows_kernel(*refs, n_in):
    q_ref, g_ref, k_ref, v_ref = refs[:4]
    a_refs, w_refs = refs[4:4 + n_in], refs[4 + n_in:4 + 2 * n_in]
    wm_ref, gn_ref, x_ref, o_ref, bias_s, ym_s = refs[4 + 2 * n_in:]
    nseq, pad, _ = q_ref.shape
    nh, hd = MEM_HEADS, MEM_HD
    step = pl.program_id(0)

    @pl.when(step == 0)
    def _():
        r = lax.broadcasted_iota(jnp.int32, bias_s.shape, 0)
        c = lax.broadcasted_iota(jnp.int32, bias_s.shape, 1)
        bias_s[...] = jnp.where(_div(r, pad) == _mod(c, nh), 0.0, NEG_INF)

    for i in range(nseq):
        q = q_ref[i] * (MEM_HD ** -0.5 * LOG2E)
        qrows = jnp.concatenate([q[:, h * hd:(h + 1) * hd] for h in range(nh)], axis=0).astype(BF16)
        s = _dot_nt(qrows, k_ref[i].astype(BF16)) + bias_s[...]
        m = jnp.max(s, axis=-1, keepdims=True)
        p = jnp.exp2(s - m)
        o = _dot(p.astype(BF16), v_ref[i].astype(BF16)) / jnp.sum(p, axis=-1, keepdims=True)
        y = jnp.concatenate([o[h * pad:(h + 1) * pad] for h in range(nh)], axis=1)
        row0 = pl.multiple_of((step * nseq + i) * pad, pad)
        ym_s[pl.ds(row0, pad), :] = y * _silu(g_ref[i])

    @pl.when(step == pl.num_programs(0) - 1)
    def _():
        acc = _dot(ym_s[...].astype(BF16), wm_ref[...])
        for a_ref, w_ref in zip(a_refs, w_refs):
            acc = acc + _dot(a_ref[...].astype(BF16), w_ref[...])
        ms = jnp.mean(acc * acc, axis=-1, keepdims=True)
        o_ref[...] = x_ref[...] + acc * lax.rsqrt(ms + NORM_EPS) * gn_ref[...]


def _mem_out_proj_rows(proj_pad, q_off, g_off, mem_k, mem_v, layer, acts, w, g, x):
    b, pad, _ = proj_pad.shape
    m, d = x.shape
    mw = MEM_W
    rows = MEM_LEN * MEM_HEADS
    n = MEM_ROWS_STEP_SEQS
    assert b % n == 0 and m == b * pad
    const = lambda shape, idx: pl.BlockSpec(shape, lambda i: idx, pipeline_mode=pl.Buffered(1))
    kv = pl.BlockSpec((n, rows, MEM_HD), lambda i: (layer * (b // n) + i, 0, 0))
    w_specs, off = [], 0
    for a in acts:
        k = a.shape[1]
        assert off % k == 0
        w_specs.append(const((k, d), (off // k, 0)))
        off += k
    assert off + mw == w.shape[0] and off % mw == 0
    return pl.pallas_call(
        functools.partial(_mem_out_proj_rows_kernel, n_in=len(acts)),
        grid=(b // n,),
        in_specs=[pl.BlockSpec((n, pad, mw), lambda i: (i, 0, q_off // mw)),
                  pl.BlockSpec((n, pad, mw), lambda i: (i, 0, g_off // mw)),
                  kv, kv]
                 + [const(a.shape, (0, 0)) for a in acts] + w_specs
                 + [const((mw, d), (off // mw, 0)), const((1, d), (0, 0)), const((m, d), (0, 0))],
        out_specs=pl.BlockSpec((m, d), lambda i: (0, 0)),
        out_shape=jax.ShapeDtypeStruct((m, d), F32),
        scratch_shapes=[pltpu.VMEM((MEM_HEADS * pad, rows), F32), pltpu.VMEM((m, mw), F32)],
        compiler_params=_params("arbitrary"),
        name="mem_out_proj_rows",
    )(proj_pad, proj_pad, mem_k.reshape(-1, rows, MEM_HD), mem_v.reshape(-1, rows, MEM_HD),
      *acts, *([w] * len(acts)), w, g.reshape(1, d), x)


def _gmlp_kernel(u_ref, v_ref, z_ref, ln_ref, w_ref, b_ref, y_ref, vo_ref, wc_ref, *, activated):
    c = GMLP_CHUNK
    gelu, silu = (lambda a: a, lambda a: a) if activated else (_gelu, _silu)

    @pl.when(pl.program_id(0) == 0)
    def _():
        causal = (lax.broadcasted_iota(jnp.int32, (c, c), 0) >= lax.broadcasted_iota(jnp.int32, (c, c), 1))
        for g in range(GMLP_GROUPS):
            wc_ref[g] = jnp.where(causal, w_ref[g], 0.0).astype(BF16)

    nchunks = u_ref.shape[0] // c
    keep_all = vo_ref.shape[0] == u_ref.shape[0]
    for i in range(nchunks):
        rs = slice(i * c, (i + 1) * c)
        v = gelu(v_ref[rs, :].astype(F32))
        mu = jnp.mean(v, axis=-1, keepdims=True)
        d = v - mu
        var = jnp.mean(d * d, axis=-1, keepdims=True)
        vn = d * lax.rsqrt(var + NORM_EPS) * ln_ref[...]
        if keep_all:
            vo_ref[rs, :] = vn
        elif i == nchunks - 1:
            vo_ref[...] = vn
        for g in range(GMLP_GROUPS):
            gs = slice(g * GMLP_GROUP, (g + 1) * GMLP_GROUP)
            mixed = _dot(wc_ref[g], vn[:, gs].astype(BF16)) + b_ref[:, gs]
            y_ref[rs, gs] = (gelu(u_ref[rs, gs].astype(F32)) * mixed
                             * silu(z_ref[rs, gs].astype(F32))).astype(y_ref.dtype)


def _gmlp(proj, gmlp_ln, w_mix, bias_rows, keep_all_v, activated):
    t = proj.shape[0]
    c = GMLP_CHUNK
    w = GMLP_WIDTH
    rows = GMLP_STEP_CHUNKS * c
    assert t % rows == 0
    col = lambda off: pl.BlockSpec((rows, w), lambda n: (n, off // w))
    return pl.pallas_call(
        functools.partial(_gmlp_kernel, activated=activated),
        grid=(t // rows,),
        in_specs=[col(O_U), col(O_V), col(O_Z),
                  pl.BlockSpec((1, w), lambda n: (0, 0)),
                  pl.BlockSpec(w_mix.shape, lambda n: (0, 0, 0)),
                  pl.BlockSpec((c, w), lambda n: (0, 0))],
        out_specs=[pl.BlockSpec((rows, w), lambda n: (n, 0)),
                   pl.BlockSpec((rows, w), lambda n: (n, 0)) if keep_all_v
                   else pl.BlockSpec((c, w), lambda n: (0, 0))],
        out_shape=[jax.ShapeDtypeStruct((t, w), BF16),
                   jax.ShapeDtypeStruct((t if keep_all_v else c, w), F32)],
        scratch_shapes=[pltpu.VMEM(w_mix.shape, BF16)],
        compiler_params=_params("arbitrary"),
        name="gmlp",
    )(proj, proj, proj, gmlp_ln.reshape(1, w), w_mix, bias_rows)


def _out_proj_kernel(*refs, n_in, mem):
    a_refs, w_refs = refs[:n_in], refs[n_in:2 * n_in]
    refs = refs[2 * n_in:]
    if mem:
        (mq_ref, mg_ref, mk_ref, mv_ref, wm_ref), refs = refs[:5], refs[5:]
    g_ref, x_ref, o_ref = refs
    sub = min(OUT_PROJ_SUB_ROWS, o_ref.shape[0])
    for r0 in range(0, o_ref.shape[0], sub):
        rs = slice(r0, r0 + sub)
        acc = _dot(a_refs[0][rs, :].astype(BF16), w_refs[0][...])
        for a_ref, w_ref in zip(a_refs[1:], w_refs[1:]):
            acc = acc + _dot(a_ref[rs, :].astype(BF16), w_ref[...])
        if mem:
            acc = acc + _dot(_mem_attend(mq_ref[rs, :], mg_ref[rs, :], mk_ref[...], mv_ref[...]), wm_ref[...])
        ms = jnp.mean(acc * acc, axis=-1, keepdims=True)
        o_ref[rs, :] = x_ref[rs, :] + acc * lax.rsqrt(ms + NORM_EPS) * g_ref[...]


def _mem_specs(proj, q_off, g_off, memkv, tm, w_rows, d):
    w = MEM_W
    const = lambda shape, idx: pl.BlockSpec(shape, lambda i: idx, pipeline_mode=pl.Buffered(1))
    specs = [pl.BlockSpec((tm, w), lambda i: (i, q_off // w)), pl.BlockSpec((tm, w), lambda i: (i, g_off // w)),
             const((MEM_LEN, w), (0, 0)), const((MEM_LEN, w), (0, 1)), const((w, d), ((w_rows - w) // w, 0))]
    return specs, [proj, proj, memkv, memkv]


def _out_proj(acts, w, g, x, tm, mem=None):
    m, d = x.shape
    n_in = len(acts)
    w_specs, off = [], 0
    for a in acts:
        k = a.shape[1]
        assert off % k == 0
        w_specs.append(pl.BlockSpec((k, d), functools.partial(lambda i, blk: (blk, 0), blk=off // k),
                                    pipeline_mode=pl.Buffered(1)))
        off += k
    mem_specs, mem_args = [], []
    if mem is not None:
        mem_specs, mem_args = _mem_specs(*mem, tm, w.shape[0], d)
        mem_args.append(w)
        off += MEM_W
    assert off == w.shape[0]
    return pl.pallas_call(
        functools.partial(_out_proj_kernel, n_in=n_in, mem=mem is not None),
        grid=(m // tm,),
        in_specs=([pl.BlockSpec((tm, a.shape[1]), lambda i: (i, 0)) for a in acts]
                  + w_specs + mem_specs
                  + [pl.BlockSpec((1, d), lambda i: (0, 0)),
                     pl.BlockSpec((tm, d), lambda i: (i, 0))]),
        out_specs=pl.BlockSpec((tm, d), lambda i: (i, 0)),
        out_shape=jax.ShapeDtypeStruct((m, d), F32),
        compiler_params=_params("arbitrary"),
        name="out_proj",
    )(*acts, *([w] * n_in), *mem_args, g.reshape(1, d), x)


def _gmlp_out_proj_kernel(u_ref, v_ref, z_ref, ln_ref, w_ref, b_ref, wy_ref, mq_ref, mg_ref, mk_ref, mv_ref, wm_ref,
                          g_ref, x_ref, o_ref, vo_ref, wc_ref, yc_ref):
    c = GMLP_CHUNK
    sub = yc_ref.shape[0]
    tm = o_ref.shape[0]

    @pl.when(pl.program_id(0) == 0)
    def _():
        causal = (lax.broadcasted_iota(jnp.int32, (c, c), 0) >= lax.broadcasted_iota(jnp.int32, (c, c), 1))
        for g in range(GMLP_GROUPS):
            wc_ref[g] = jnp.where(causal, w_ref[g], 0.0).astype(BF16)

    for r0 in range(0, tm, sub):
        for c0 in range(r0, r0 + sub, c):
            rs = slice(c0, c0 + c)
            v = v_ref[rs, :].astype(F32)
            mu = jnp.mean(v, axis=-1, keepdims=True)
            d = v - mu
            var = jnp.mean(d * d, axis=-1, keepdims=True)
            vn = d * lax.rsqrt(var + NORM_EPS) * ln_ref[...]
            if c0 + c == tm:
                vo_ref[...] = vn
            for g in range(GMLP_GROUPS):
                gs = slice(g * GMLP_GROUP, (g + 1) * GMLP_GROUP)
                mixed = _dot(wc_ref[g], vn[:, gs].astype(BF16)) + b_ref[:, gs]
                yc_ref[c0 - r0:c0 - r0 + c, gs] = (u_ref[rs, gs].astype(F32) * mixed
                                                   * z_ref[rs, gs].astype(F32)).astype(BF16)
        rs = slice(r0, r0 + sub)
        ym = _mem_attend(mq_ref[rs, :], mg_ref[rs, :], mk_ref[...], mv_ref[...])
        acc = _dot(yc_ref[...], wy_ref[...]) + _dot(ym, wm_ref[...])
        ms = jnp.mean(acc * acc, axis=-1, keepdims=True)
        o_ref[rs, :] = x_ref[rs, :] + acc * lax.rsqrt(ms + NORM_EPS) * g_ref[...]


def _gmlp_out_proj(proj, memkv, gmlp_ln, w_mix, bias_rows, w, g, x, tm):
    m, d = x.shape
    c = GMLP_CHUNK
    gw = GMLP_WIDTH
    sub = min(OUT_PROJ_SUB_ROWS, tm)
    assert m % tm == 0 and tm % sub == 0 and sub % c == 0 and w.shape[0] == gw + MEM_W
    col = lambda off: pl.BlockSpec((tm, gw), lambda i: (i, off // gw))
    mem_specs, mem_args = _mem_specs(proj, O_MQ, O_MG, memkv, tm, w.shape[0], d)
    return pl.pallas_call(
        _gmlp_out_proj_kernel,
        grid=(m // tm,),
        in_specs=[col(O_U), col(O_V), col(O_Z),
                  pl.BlockSpec((1, gw), lambda i: (0, 0)),
                  pl.BlockSpec(w_mix.shape, lambda i: (0, 0, 0)),
                  pl.BlockSpec((c, gw), lambda i: (0, 0)),
                  pl.BlockSpec((gw, d), lambda i: (0, 0), pipeline_mode=pl.Buffered(1))]
                 + mem_specs
                 + [pl.BlockSpec((1, d), lambda i: (0, 0)),
                    pl.BlockSpec((tm, d), lambda i: (i, 0))],
        out_specs=[pl.BlockSpec((tm, d), lambda i: (i, 0)),
                   pl.BlockSpec((c, gw), lambda i: (0, 0))],
        out_shape=[jax.ShapeDtypeStruct((m, d), F32),
                   jax.ShapeDtypeStruct((c, gw), F32)],
        scratch_shapes=[pltpu.VMEM(w_mix.shape, BF16), pltpu.VMEM((sub, gw), BF16)],
        compiler_params=_params("arbitrary"),
        name="gmlp_out_proj",
    )(proj, proj, proj, gmlp_ln.reshape(1, gw), w_mix, bias_rows, w, *mem_args, w, g.reshape(1, d), x)


def _head_rows_kernel(x_ref, k_ref, v_ref, *, nh):
    rows = x_ref.shape[0]
    hd = k_ref.shape[1]
    for h in range(nh):
        dst = pl.ds(h, rows, stride=nh)
        k_ref[dst, :] = x_ref[:, h * hd:(h + 1) * hd]
        v_ref[dst, :] = x_ref[:, (nh + h) * hd:(nh + h + 1) * hd]


def _head_rows(kv, nh, hd, tr):
    p = kv.shape[0]
    assert p % tr == 0 and kv.shape[1] == 2 * nh * hd
    out = pl.BlockSpec((tr * nh, hd), lambda i: (i, 0))
    return pl.pallas_call(
        functools.partial(_head_rows_kernel, nh=nh),
        grid=(p // tr,),
        in_specs=[pl.BlockSpec((tr, 2 * nh * hd), lambda i: (i, 0))],
        out_specs=[out, out],
        out_shape=[jax.ShapeDtypeStruct((p * nh, hd), kv.dtype)] * 2,
        compiler_params=_params("arbitrary"),
        name="head_rows",
    )(kv)


def _pad_steps(a, pad):
    nseq, seq, n = a.shape
    return jnp.pad(a, ((0, 0), (0, pad - seq), (0, 0))).reshape(nseq * pad, n)


def kernel(x_prompt, x_sample, state_ret, cache_swa_k, cache_swa_v, cache_mem_k, cache_mem_v, mem_prompt,
           pre_norm, post_norm, mem_norm, w_mem_k, w_mem_v, w_in_even, ret_gn, w_out_even,
           w_in_odd, gmlp_ln, w_spatial, b_spatial, w_out_odd):
    bp, tp, d = x_prompt.shape
    nseq, seq, _ = x_sample.shape
    pad = SAMPLE_PAD
    rows_s = nseq * pad
    c = GMLP_CHUNK
    assert bp == 1 and seq <= pad and rows_s % c == 0 and c % pad == 0
    hp = x_prompt.reshape(tp, d)
    hs = _pad_steps(x_sample, pad)
    mem = mem_prompt.reshape(MEM_LEN, d)
    tm_p = 1024 if tp % 1024 == 0 else 512
    unpad = lambda a: a.reshape(nseq, pad, -1)[:, :seq]

    pos_pad = np.concatenate([PAST_LEN + np.arange(seq), np.zeros((pad - seq,), np.int64)])
    ret_cos_p, ret_sin_p = _rot_tables_range(tp, RET_DK, RET_THETA, RET_DK)
    ret_cos_s, ret_sin_s = _rot_tables(np.tile(pos_pad, nseq), RET_DK, RET_THETA, RET_DK)
    swa_cos_p, swa_sin_p = _rot_tables_range(tp, ROPE_DIMS, ROPE_THETA, SWA_HD)
    swa_cos_s, swa_sin_s = _rot_tables(pos_pad, ROPE_DIMS, ROPE_THETA, SWA_HD)
    swa_cos_s = np.tile(swa_cos_s, (1, SWA_HEADS))
    swa_sin_s = np.tile(swa_sin_s, (1, SWA_HEADS))

    mem_proj = lambda layer: jnp.concatenate(
        [_norm_matmul_cast(mem, mem_norm[layer], wl[layer], MEM_W)[0] for wl in (w_mem_k, w_mem_v)], axis=1)
    memkv0 = mem_proj(0)

    proj_s, w_in0 = _norm_matmul_cast(hs, pre_norm[0], w_in_even[0], 1024)
    proj_s3 = proj_s.reshape(nseq, pad, EVEN_IN)
    ya_s, st_s = _ret_sample(proj_s, state_ret[0], ret_cos_s, ret_sin_s, ret_gn[0], nseq, seq, pad)
    yb_s, sk = _swa_sample(proj_s3, cache_swa_k[0], cache_swa_v[0], swa_cos_s, swa_sin_s, seq)
    sk = sk[:, :seq].reshape(nseq, seq, SWA_HEADS, SWA_HD)
    sv = unpad(proj_s[:, E_BV:E_BV + SWA_HEADS * SWA_HD]).reshape(nseq, seq, SWA_HEADS, SWA_HD)

    buf_p = min(SWA_MAX_WINDOW, tp)
    proj_p, kv_last, w_out0, w_in1, w_out1 = _norm_matmul(
        hp, pre_norm[0], w_in0, tm_p, 1024, BF16, (tp - buf_p, E_BK, E_BG - E_BK),
        casts=[(w_out_even[0], 256), (w_in_odd[0], 1024), (w_out_odd[0], 256)],
        rot=(swa_cos_p, swa_sin_p, E_BQ, E_BK, SWA_HD ** -0.5 * LOG2E))
    hs1 = _mem_out_proj_rows(proj_s3, E_MQ, E_MG, cache_mem_k, cache_mem_v, 0,
                             [ya_s, yb_s.reshape(rows_s, -1)], w_out0, post_norm[0], hs)
    ya_p, st_p = _ret_prompt(proj_p, ret_cos_p, ret_sin_p, ret_gn[0])
    yb_p = _swa_prompt(proj_p)
    pk, pv = _head_rows(kv_last, SWA_HEADS, SWA_HD, 256)
    hp1 = _out_proj([ya_p, yb_p], w_out0, post_norm[0], hp, 512, mem=(proj_p, E_MQ, E_MG, memkv0))

    memkv1 = mem_proj(1)
    bias_p = jnp.repeat(b_spatial[0].T, GMLP_GROUP, axis=1)
    blockdiag = lambda wg: jnp.kron(jnp.eye(c // pad, dtype=F32),
                                    jnp.pad(wg[:seq, :seq], ((0, pad - seq), (0, pad - seq))))
    w_mix_s = jax.vmap(blockdiag)(w_spatial[0])
    bias_s = jnp.tile(jnp.pad(bias_p[:seq], ((0, pad - seq), (0, 0))), (c // pad, 1))

    proj_s1 = _norm_matmul(hs1, pre_norm[1], w_in1, rows_s, 1024)
    yc_s, gv_s = _gmlp(proj_s1, gmlp_ln[0], w_mix_s, bias_s, True, False)
    hs2 = _mem_out_proj_rows(proj_s1.reshape(nseq, pad, ODD_IN), O_MQ, O_MG, cache_mem_k, cache_mem_v, 1,
                             [yc_s], w_out1, post_norm[1], hs1)

    proj_p1 = _norm_matmul(hp1, pre_norm[1], w_in1, tm_p, 1024, BF16,
                           acts=[(O_U, O_Z, "gelu"), (O_Z, O_MQ, "silu")])
    hp2, gv_p = _gmlp_out_proj(proj_p1, memkv1, gmlp_ln[0], w_spatial[0], bias_p, w_out1, post_norm[1], hp1, 512)

    memkv = jnp.stack([memkv0, memkv1])
    p_mk = memkv[:, :, :MEM_W].reshape(2, bp, MEM_LEN, MEM_HEADS, MEM_HD)
    p_mv = memkv[:, :, MEM_W:].reshape(2, bp, MEM_LEN, MEM_HEADS, MEM_HD)
    return (hp2.reshape(bp, tp, d), unpad(hs2),
            st_p[None, None],
            pk.reshape(1, bp, buf_p, SWA_HEADS, SWA_HD), pv.reshape(1, bp, buf_p, SWA_HEADS, SWA_HD),
            p_mk, p_mv,
            gv_p[None, None],
            st_s[None],
            sk[None], sv[None],
            unpad(gv_s)[None])
```
